```python
import math
import jax, jax.numpy as jnp
from jax import lax
import numpy as np

D_MODEL = 1024
BATCH = 8
SEQ = 4096
DEPTH = 2

HEAD_DIM = 64
GRID_W = 64
Q_BLOCK = 128
ROPE_THETA = 10000.0
MLA_HEADS = 6
MLA_Q_RANK = 256
MLA_KV_RANK = 128
MLA_NOPE_DIM = 64
MLA_ROPE_DIM = 32
MLA_V_DIM = 64
DIL_HEADS = 6
DIL_BRANCHES = ((128, 1), (512, 4), (2048, 16))
GQA_Q_HEADS = 4
GQA_KV_HEADS = 2
REL_BUCKETS = 32
REL_MAX_DIST = 1024
D_FF = -(-8 * D_MODEL // (3 * 256)) * 256
MLA_IN = MLA_Q_RANK + MLA_KV_RANK + MLA_ROPE_DIM
DIL_IN = 3 * DIL_HEADS * HEAD_DIM
GQA_IN = (GQA_Q_HEADS + 2 * GQA_KV_HEADS) * HEAD_DIM
IN_WIDTH = MLA_IN + DIL_IN + GQA_IN
MIX_WIDTH = MLA_HEADS * MLA_V_DIM + DIL_HEADS * HEAD_DIM + GQA_Q_HEADS * HEAD_DIM
DN_ALPHA = (2.0 * DEPTH) ** 0.25
DN_BETA = (8.0 * DEPTH) ** -0.25
NEG_INF = -1e30

kernel_name = "hybrid_mla_dilated_axialgqa_deepnorm_encoder"


def rms_norm(x, g, eps=1e-6):
    xf = x.astype(jnp.float32)
    y = xf * lax.rsqrt(jnp.mean(xf * xf, axis=-1, keepdims=True) + eps)
    return (y * g.astype(jnp.float32)).astype(x.dtype)


def layer_norm(x, g, b, eps=1e-5):
    xf = x.astype(jnp.float32)
    mu = jnp.mean(xf, axis=-1, keepdims=True)
    var = jnp.mean(jnp.square(xf - mu), axis=-1, keepdims=True)
    y = (xf - mu) * lax.rsqrt(var + eps)
    return (y * g.astype(jnp.float32) + b.astype(jnp.float32)).astype(x.dtype)


def rope(x, pos):
    d = x.shape[-1]
    inv = ROPE_THETA ** (-jnp.arange(0, d, 2, dtype=jnp.float32) / d)
    ang = pos[:, None] * inv[None, :]
    cos = jnp.cos(ang)[None, :, None, :]
    sin = jnp.sin(ang)[None, :, None, :]
    xf = x.astype(jnp.float32)
    x1, x2 = xf[..., : d // 2], xf[..., d // 2:]
    return jnp.concatenate([x1 * cos - x2 * sin, x1 * sin + x2 * cos], axis=-1).astype(x.dtype)


def t5_bucket(rel):
    nb = REL_BUCKETS // 2
    exact = nb // 2
    ret = jnp.where(rel > 0, nb, 0)
    n = jnp.abs(rel)
    nf = jnp.maximum(n, 1).astype(jnp.float32)
    large = exact + (jnp.log(nf / exact) / math.log(REL_MAX_DIST / exact) * (nb - exact)).astype(jnp.int32)
    large = jnp.minimum(large, nb - 1)
    return ret + jnp.where(n < exact, n, large)


def dense_attention(q, k, v, scale):
    B, S, H, Dk = q.shape
    Hkv, Dv = k.shape[2], v.shape[-1]
    G = H // Hkv
    nblk = S // Q_BLOCK
    qb = q.reshape(B, nblk, Q_BLOCK, Hkv, G, Dk).transpose(1, 0, 2, 3, 4, 5)

    def one_block(qblk):
        logits = jnp.einsum('bqkgd,bskd->bkgqs', qblk, k).astype(jnp.float32) * scale
        p = jax.nn.softmax(logits, axis=-1)
        return jnp.einsum('bkgqs,bskd->bqkgd', p.astype(v.dtype), v)

    o = lax.map(one_block, qb)
    return o.transpose(1, 0, 2, 3, 4, 5).reshape(B, S, H, Dv)


def dilated_branch(q, k, v, rel_bias, window, dil):
    B, S, H, D = q.shape
    half = window // (2 * dil)
    L = S // dil
    nb = -(-L // half)
    Lp = nb * half
    BB = B * dil

    def to_sub(t):
        return t.reshape(B, L, dil, H, D).transpose(0, 2, 1, 3, 4).reshape(BB, L, H, D)

    qs, ks, vs = to_sub(q), to_sub(k), to_sub(v)
    qb = jnp.pad(qs, ((0, 0), (0, Lp - L), (0, 0), (0, 0))).reshape(BB, nb, half, H, D)

    def band(t):
        tp = jnp.pad(t, ((0, 0), (half, Lp - L + half), (0, 0), (0, 0))).reshape(BB, nb + 2, half, H, D)
        return jnp.concatenate([tp[:, :-2], tp[:, 1:-1], tp[:, 2:]], axis=2)

    kb, vb = band(ks), band(vs)
    rel = jnp.arange(3 * half)[None, :] - half - jnp.arange(half)[:, None]
    bias = jnp.transpose(rel_bias[t5_bucket(rel * dil)], (2, 0, 1)).astype(jnp.float32)
    key_idx = jnp.arange(nb)[:, None] * half - half + jnp.arange(3 * half)[None, :]
    mask = (jnp.abs(rel) <= half)[None] & ((key_idx >= 0) & (key_idx < L))[:, None, :]

    logits = jnp.einsum('bnqhd,bnkhd->bnhqk', qb, kb).astype(jnp.float32) * (HEAD_DIM ** -0.5)
    logits = jnp.where(mask[None, :, None], logits + bias[None, None], NEG_INF)
    m = jnp.max(logits, axis=-1, keepdims=True)
    e = jnp.exp(logits - m)
    s = jnp.sum(e, axis=-1)
    o = jnp.einsum('bnhqk,bnkhd->bnqhd', e.astype(v.dtype), vb).astype(jnp.float32)
    o = o / jnp.transpose(s, (0, 1, 3, 2))[..., None]
    lse = jnp.transpose(m[..., 0] + jnp.log(s), (0, 1, 3, 2))
    o = o.reshape(BB, Lp, H, D)[:, :L]
    lse = lse.reshape(BB, Lp, H)[:, :L]
    o = o.reshape(B, dil, L, H, D).transpose(0, 2, 1, 3, 4).reshape(B, S, H, D)
    lse = lse.reshape(B, dil, L, H).transpose(0, 2, 1, 3).reshape(B, S, H)
    return o, lse


def dilated_mixture(q, k, v, rel_bias):
    outs, lses = [], []
    for window, dil in DIL_BRANCHES:
        o, lse = dilated_branch(q, k, v, rel_bias, window, dil)
        outs.append(o)
        lses.append(lse)
    w = jax.nn.softmax(jnp.stack(lses, axis=0), axis=0)
    o = jnp.sum(w[..., None] * jnp.stack(outs, axis=0), axis=0)
    return o.astype(q.dtype)


def _fwd_setup_inputs(seed: int = 0) -> dict:
    key = jax.random.key(seed)
    ks = jax.random.split(key, 20)
    f32 = jnp.float32

    def nrm(k, shape, scale):
        return jax.random.normal(k, shape, f32) * scale

    def gain(k, shape):
        return 1.0 + 0.02 * jax.random.normal(k, shape, f32)

    return {
        "x": jax.random.normal(ks[0], (BATCH, SEQ, D_MODEL), f32),
        "w_in": nrm(ks[1], (DEPTH, D_MODEL, IN_WIDTH), D_MODEL ** -0.5),
        "mla_q_norm": gain(ks[2], (DEPTH, MLA_Q_RANK)),
        "mla_kv_norm": gain(ks[3], (DEPTH, MLA_KV_RANK)),
        "mla_w_uq": nrm(ks[4], (DEPTH, MLA_Q_RANK, MLA_HEADS * (MLA_NOPE_DIM + MLA_ROPE_DIM)), MLA_Q_RANK ** -0.5),
        "mla_w_ukv": nrm(ks[5], (DEPTH, MLA_KV_RANK, MLA_HEADS * (MLA_NOPE_DIM + MLA_V_DIM)), MLA_KV_RANK ** -0.5),
        "gqa_q_norm": gain(ks[6], (DEPTH, HEAD_DIM)),
        "gqa_k_norm": gain(ks[7], (DEPTH, HEAD_DIM)),
        "rel_bias": nrm(ks[8], (REL_BUCKETS, DIL_HEADS), 0.1),
        "w_out": nrm(ks[9], (DEPTH, MIX_WIDTH, D_MODEL), DN_BETA * MIX_WIDTH ** -0.5),
        "ln1_g": gain(ks[10], (DEPTH, D_MODEL)),
        "ln1_b": nrm(ks[11], (DEPTH, D_MODEL), 0.02),
        "ffn_w_gate": nrm(ks[12], (DEPTH, D_MODEL, D_FF), D_MODEL ** -0.5),
        "ffn_w_up": nrm(ks[13], (DEPTH, D_MODEL, D_FF), D_MODEL ** -0.5),
        "ffn_w_down": nrm(ks[14], (DEPTH, D_FF, D_MODEL), DN_BETA * D_FF ** -0.5),
        "ln2_g": gain(ks[15], (DEPTH, D_MODEL)),
        "ln2_b": nrm(ks[16], (DEPTH, D_MODEL), 0.02),
    }


def _fwd_reference(x, w_in, mla_q_norm, mla_kv_norm, mla_w_uq, mla_w_ukv, gqa_q_norm, gqa_k_norm,
              rel_bias, w_out, ln1_g, ln1_b, ffn_w_gate, ffn_w_up, ffn_w_down, ln2_g, ln2_b):
    B, S, _ = x.shape
    rows = S // GRID_W
    pos = jnp.arange(S, dtype=jnp.float32)
    row_pos = jnp.repeat(jnp.arange(rows), GRID_W).astype(jnp.float32)
    col_pos = jnp.tile(jnp.arange(GRID_W), rows).astype(jnp.float32)
    half_rot = HEAD_DIM // 2

    for l in range(DEPTH):
        h = x @ w_in[l]
        o0 = 0
        cq = rms_norm(h[..., o0:o0 + MLA_Q_RANK], mla_q_norm[l]); o0 += MLA_Q_RANK
        ckv = rms_norm(h[..., o0:o0 + MLA_KV_RANK], mla_kv_norm[l]); o0 += MLA_KV_RANK
        k_rope = h[..., o0:o0 + MLA_ROPE_DIM][:, :, None, :]; o0 += MLA_ROPE_DIM
        qa = (cq @ mla_w_uq[l]).reshape(B, S, MLA_HEADS, MLA_NOPE_DIM + MLA_ROPE_DIM)
        qa = jnp.concatenate([qa[..., :MLA_NOPE_DIM], rope(qa[..., MLA_NOPE_DIM:], pos)], axis=-1)
        kva = (ckv @ mla_w_ukv[l]).reshape(B, S, MLA_HEADS, MLA_NOPE_DIM + MLA_V_DIM)
        k_rope = jnp.broadcast_to(rope(k_rope, pos), (B, S, MLA_HEADS, MLA_ROPE_DIM))
        ka = jnp.concatenate([kva[..., :MLA_NOPE_DIM], k_rope], axis=-1)
        va = kva[..., MLA_NOPE_DIM:]
        out_a = dense_attention(qa, ka, va, (MLA_NOPE_DIM + MLA_ROPE_DIM) ** -0.5)
        out_a = out_a.reshape(B, S, MLA_HEADS * MLA_V_DIM)
        hb = h[..., o0:o0 + DIL_IN].reshape(B, S, 3, DIL_HEADS, HEAD_DIM); o0 += DIL_IN
        out_b = dilated_mixture(hb[:, :, 0], hb[:, :, 1], hb[:, :, 2], rel_bias)
        out_b = out_b.reshape(B, S, DIL_HEADS * HEAD_DIM)
        nq, nkv = GQA_Q_HEADS * HEAD_DIM, GQA_KV_HEADS * HEAD_DIM
        qc = h[..., o0:o0 + nq].reshape(B, S, GQA_Q_HEADS, HEAD_DIM); o0 += nq
        kc = h[..., o0:o0 + nkv].reshape(B, S, GQA_KV_HEADS, HEAD_DIM); o0 += nkv
        vc = h[..., o0:o0 + nkv].reshape(B, S, GQA_KV_HEADS, HEAD_DIM); o0 += nkv
        qc = rms_norm(qc, gqa_q_norm[l])
        kc = rms_norm(kc, gqa_k_norm[l])
        qc = jnp.concatenate([rope(qc[..., :half_rot], row_pos), rope(qc[..., half_rot:], col_pos)], axis=-1)
        kc = jnp.concatenate([rope(kc[..., :half_rot], row_pos), rope(kc[..., half_rot:], col_pos)], axis=-1)
        out_c = dense_attention(qc, kc, vc, HEAD_DIM ** -0.5).reshape(B, S, GQA_Q_HEADS * HEAD_DIM)
        mix = jnp.concatenate([out_a, out_b, out_c], axis=-1) @ w_out[l]
        x = layer_norm(DN_ALPHA * x + mix, ln1_g[l], ln1_b[l])
        ff = (jax.nn.silu(x @ ffn_w_gate[l]) * (x @ ffn_w_up[l])) @ ffn_w_down[l]
        x = layer_norm(DN_ALPHA * x + ff, ln2_g[l], ln2_b[l])
    return x


import jax as _jax
import jax.numpy as _jnp

TWIN_FORMAT = 'train_step'
FWD_PARAMS = ['x', 'w_in', 'mla_q_norm', 'mla_kv_norm', 'mla_w_uq', 'mla_w_ukv', 'gqa_q_norm', 'gqa_k_norm', 'rel_bias', 'w_out', 'ln1_g', 'ln1_b', 'ffn_w_gate', 'ffn_w_up', 'ffn_w_down', 'ln2_g', 'ln2_b']
TWIN_WEIGHTS = ['w_in', 'mla_q_norm', 'mla_kv_norm', 'mla_w_uq', 'mla_w_ukv', 'gqa_q_norm', 'gqa_k_norm', 'rel_bias', 'w_out', 'ln1_g', 'ln1_b', 'ffn_w_gate', 'ffn_w_up', 'ffn_w_down', 'ln2_g', 'ln2_b']
TWIN_DIFF_INPUT = 'x'
TWIN_INPUTS = ['x', 'w_in', 'mla_q_norm', 'mla_kv_norm', 'mla_w_uq', 'mla_w_ukv', 'gqa_q_norm', 'gqa_k_norm', 'rel_bias', 'w_out', 'ln1_g', 'ln1_b', 'ffn_w_gate', 'ffn_w_up', 'ffn_w_down', 'ln2_g', 'ln2_b', 'loss_target', 'm_w_in', 'm_mla_q_norm', 'm_mla_kv_norm', 'm_mla_w_uq', 'm_mla_w_ukv', 'm_gqa_q_norm', 'm_gqa_k_norm', 'm_rel_bias', 'm_w_out', 'm_ln1_g', 'm_ln1_b', 'm_ffn_w_gate', 'm_ffn_w_up', 'm_ffn_w_down', 'm_ln2_g', 'm_ln2_b', 'v_w_in', 'v_mla_q_norm', 'v_mla_kv_norm', 'v_mla_w_uq', 'v_mla_w_ukv', 'v_gqa_q_norm', 'v_gqa_k_norm', 'v_rel_bias', 'v_w_out', 'v_ln1_g', 'v_ln1_b', 'v_ffn_w_gate', 'v_ffn_w_up', 'v_ffn_w_down', 'v_ln2_g', 'v_ln2_b']
TWIN_OUTPUTS = ['loss', 'grad_x', 'grad_w_in', 'grad_mla_q_norm', 'grad_mla_kv_norm', 'grad_mla_w_uq', 'grad_mla_w_ukv', 'grad_gqa_q_norm', 'grad_gqa_k_norm', 'grad_rel_bias', 'grad_w_out', 'grad_ln1_g', 'grad_ln1_b', 'grad_ffn_w_gate', 'grad_ffn_w_up', 'grad_ffn_w_down', 'grad_ln2_g', 'grad_ln2_b', 'delta_w_in', 'delta_mla_q_norm', 'delta_mla_kv_norm', 'delta_mla_w_uq', 'delta_mla_w_ukv', 'delta_gqa_q_norm', 'delta_gqa_k_norm', 'delta_rel_bias', 'delta_w_out', 'delta_ln1_g', 'delta_ln1_b', 'delta_ffn_w_gate', 'delta_ffn_w_up', 'delta_ffn_w_down', 'delta_ln2_g', 'delta_ln2_b', 'new_m_w_in', 'new_m_mla_q_norm', 'new_m_mla_kv_norm', 'new_m_mla_w_uq', 'new_m_mla_w_ukv', 'new_m_gqa_q_norm', 'new_m_gqa_k_norm', 'new_m_rel_bias', 'new_m_w_out', 'new_m_ln1_g', 'new_m_ln1_b', 'new_m_ffn_w_gate', 'new_m_ffn_w_up', 'new_m_ffn_w_down', 'new_m_ln2_g', 'new_m_ln2_b', 'new_v_w_in', 'new_v_mla_q_norm', 'new_v_mla_kv_norm', 'new_v_mla_w_uq', 'new_v_mla_w_ukv', 'new_v_gqa_q_norm', 'new_v_gqa_k_norm', 'new_v_rel_bias', 'new_v_w_out', 'new_v_ln1_g', 'new_v_ln1_b', 'new_v_ffn_w_gate', 'new_v_ffn_w_up', 'new_v_ffn_w_down', 'new_v_ln2_g', 'new_v_ln2_b']
TWIN_LEAF_KINDS = {'loss': 'loss', 'grad_x': 'grad_x', 'grad_w_in': 'grad_w', 'grad_mla_q_norm': 'grad_w', 'grad_mla_kv_norm': 'grad_w', 'grad_mla_w_uq': 'grad_w', 'grad_mla_w_ukv': 'grad_w', 'grad_gqa_q_norm': 'grad_w', 'grad_gqa_k_norm': 'grad_w', 'grad_rel_bias': 'grad_w', 'grad_w_out': 'grad_w', 'grad_ln1_g': 'grad_w', 'grad_ln1_b': 'grad_w', 'grad_ffn_w_gate': 'grad_w', 'grad_ffn_w_up': 'grad_w', 'grad_ffn_w_down': 'grad_w', 'grad_ln2_g': 'grad_w', 'grad_ln2_b': 'grad_w', 'delta_w_in': 'delta_w', 'delta_mla_q_norm': 'delta_w', 'delta_mla_kv_norm': 'delta_w', 'delta_mla_w_uq': 'delta_w', 'delta_mla_w_ukv': 'delta_w', 'delta_gqa_q_norm': 'delta_w', 'delta_gqa_k_norm': 'delta_w', 'delta_rel_bias': 'delta_w', 'delta_w_out': 'delta_w', 'delta_ln1_g': 'delta_w', 'delta_ln1_b': 'delta_w', 'delta_ffn_w_gate': 'delta_w', 'delta_ffn_w_up': 'delta_w', 'delta_ffn_w_down': 'delta_w', 'delta_ln2_g': 'delta_w', 'delta_ln2_b': 'delta_w', 'new_m_w_in': 'new_m', 'new_m_mla_q_norm': 'new_m', 'new_m_mla_kv_norm': 'new_m', 'new_m_mla_w_uq': 'new_m', 'new_m_mla_w_ukv': 'new_m', 'new_m_gqa_q_norm': 'new_m', 'new_m_gqa_k_norm': 'new_m', 'new_m_rel_bias': 'new_m', 'new_m_w_out': 'new_m', 'new_m_ln1_g': 'new_m', 'new_m_ln1_b': 'new_m', 'new_m_ffn_w_gate': 'new_m', 'new_m_ffn_w_up': 'new_m', 'new_m_ffn_w_down': 'new_m', 'new_m_ln2_g': 'new_m', 'new_m_ln2_b': 'new_m', 'new_v_w_in': 'new_v', 'new_v_mla_q_norm': 'new_v', 'new_v_mla_kv_norm': 'new_v', 'new_v_mla_w_uq': 'new_v', 'new_v_mla_w_ukv': 'new_v', 'new_v_gqa_q_norm': 'new_v', 'new_v_gqa_k_norm': 'new_v', 'new_v_rel_bias': 'new_v', 'new_v_w_out': 'new_v', 'new_v_ln1_g': 'new_v', 'new_v_ln1_b': 'new_v', 'new_v_ffn_w_gate': 'new_v', 'new_v_ffn_w_up': 'new_v', 'new_v_ffn_w_down': 'new_v', 'new_v_ln2_g': 'new_v', 'new_v_ln2_b': 'new_v'}


def _forward(args):
    return _fwd_reference(*[args[k] for k in FWD_PARAMS])


def _output_shape():
    out = _jax.eval_shape(lambda: _forward(_fwd_setup_inputs(0)))
    return out.shape, out.dtype

N_MICROBATCH = 1
ADAM_LR = 0.001
ADAM_B1 = 0.9
ADAM_B2 = 0.999
ADAM_EPS = 1e-08
ADAM_WD = 0.01
ADAM_STEP = 10
PER_EXAMPLE_BATCH_AXIS = {'x': 0, 'loss_target': 0}
SHARED_INPUTS = []
_WEIGHT_DTYPES = {'w_in': _jnp.float32, 'mla_q_norm': _jnp.float32, 'mla_kv_norm': _jnp.float32, 'mla_w_uq': _jnp.float32, 'mla_w_ukv': _jnp.float32, 'gqa_q_norm': _jnp.float32, 'gqa_k_norm': _jnp.float32, 'rel_bias': _jnp.float32, 'w_out': _jnp.float32, 'ln1_g': _jnp.float32, 'ln1_b': _jnp.float32, 'ffn_w_gate': _jnp.float32, 'ffn_w_up': _jnp.float32, 'ffn_w_down': _jnp.float32, 'ln2_g': _jnp.float32, 'ln2_b': _jnp.float32}
MOMENT_SCALE = {'w_in': 9.764903e-03, 'mla_q_norm': 1.041635e-02, 'mla_kv_norm': 2.384219e-02, 'mla_w_uq': 6.535033e-03, 'mla_w_ukv': 7.834106e-03, 'gqa_q_norm': 1.307431e-02, 'gqa_k_norm': 1.339593e-02, 'rel_bias': 1.828966e-02, 'w_out': 1.737594e-02, 'ln1_g': 1.012427e+00, 'ln1_b': 4.572043e-01, 'ffn_w_gate': 2.303338e-02, 'ffn_w_up': 2.228958e-02, 'ffn_w_down': 7.396788e-02, 'ln2_g': 2.265103e+01, 'ln2_b': 7.800780e-01}


def _to_microbatches(a, axis):
    t = _jnp.moveaxis(a, axis, 0)
    t = t.reshape((N_MICROBATCH, t.shape[0] // N_MICROBATCH) + t.shape[1:])
    return _jnp.moveaxis(t, 1, axis + 1)


def setup_inputs(seed: int = 0) -> dict:
    inp = _fwd_setup_inputs(seed)
    key = _jax.random.fold_in(_jax.random.key(seed), 7919)
    shape, _ = _output_shape()
    out = dict(inp)
    out["loss_target"] = _jax.random.normal(_jax.random.fold_in(key, 0), shape, _jnp.float32)
    for i, name in enumerate(TWIN_WEIGHTS):
        w = inp[name].astype(_jnp.float32)
        if MOMENT_SCALE is None:
            s = _jnp.sqrt(_jnp.mean(_jnp.square(w)) + 1e-30)
        else:
            s = MOMENT_SCALE[name]
        km, kv = _jax.random.split(_jax.random.fold_in(key, i + 1))
        out[name] = w
        out["m_" + name] = s * _jax.random.normal(km, w.shape, _jnp.float32)
        out["v_" + name] = (s * s) * _jax.random.uniform(kv, w.shape, _jnp.float32, 0.5, 1.5)
    if N_MICROBATCH > 1:
        for name, axis in PER_EXAMPLE_BATCH_AXIS.items():
            out[name] = _to_microbatches(out[name], axis)
    return {'x': out['x'], 'w_in': out['w_in'], 'mla_q_norm': out['mla_q_norm'], 'mla_kv_norm': out['mla_kv_norm'], 'mla_w_uq': out['mla_w_uq'], 'mla_w_ukv': out['mla_w_ukv'], 'gqa_q_norm': out['gqa_q_norm'], 'gqa_k_norm': out['gqa_k_norm'], 'rel_bias': out['rel_bias'], 'w_out': out['w_out'], 'ln1_g': out['ln1_g'], 'ln1_b': out['ln1_b'], 'ffn_w_gate': out['ffn_w_gate'], 'ffn_w_up': out['ffn_w_up'], 'ffn_w_down': out['ffn_w_down'], 'ln2_g': out['ln2_g'], 'ln2_b': out['ln2_b'], 'loss_target': out['loss_target'], 'm_w_in': out['m_w_in'], 'm_mla_q_norm': out['m_mla_q_norm'], 'm_mla_kv_norm': out['m_mla_kv_norm'], 'm_mla_w_uq': out['m_mla_w_uq'], 'm_mla_w_ukv': out['m_mla_w_ukv'], 'm_gqa_q_norm': out['m_gqa_q_norm'], 'm_gqa_k_norm': out['m_gqa_k_norm'], 'm_rel_bias': out['m_rel_bias'], 'm_w_out': out['m_w_out'], 'm_ln1_g': out['m_ln1_g'], 'm_ln1_b': out['m_ln1_b'], 'm_ffn_w_gate': out['m_ffn_w_gate'], 'm_ffn_w_up': out['m_ffn_w_up'], 'm_ffn_w_down': out['m_ffn_w_down'], 'm_ln2_g': out['m_ln2_g'], 'm_ln2_b': out['m_ln2_b'], 'v_w_in': out['v_w_in'], 'v_mla_q_norm': out['v_mla_q_norm'], 'v_mla_kv_norm': out['v_mla_kv_norm'], 'v_mla_w_uq': out['v_mla_w_uq'], 'v_mla_w_ukv': out['v_mla_w_ukv'], 'v_gqa_q_norm': out['v_gqa_q_norm'], 'v_gqa_k_norm': out['v_gqa_k_norm'], 'v_rel_bias': out['v_rel_bias'], 'v_w_out': out['v_w_out'], 'v_ln1_g': out['v_ln1_g'], 'v_ln1_b': out['v_ln1_b'], 'v_ffn_w_gate': out['v_ffn_w_gate'], 'v_ffn_w_up': out['v_ffn_w_up'], 'v_ffn_w_down': out['v_ffn_w_down'], 'v_ln2_g': out['v_ln2_g'], 'v_ln2_b': out['v_ln2_b']}


def _loss(weights, diff, rest, loss_target):
    with _jax.named_scope("forward"):
        args = {**rest, TWIN_DIFF_INPUT: diff, **{k: w.astype(_WEIGHT_DTYPES[k]) for k, w in weights.items()}}
        y = _forward(args)
    with _jax.named_scope("loss_head"):
        err = _jnp.square(y.astype(_jnp.float32) - loss_target)
        return 0.5 * _jnp.sum(_jnp.mean(err, axis=-1)) if err.ndim else 0.5 * err


def _adamw(w, g, m, v):
    m = ADAM_B1 * m + (1.0 - ADAM_B1) * g
    v = ADAM_B2 * v + (1.0 - ADAM_B2) * _jnp.square(g)
    m_hat = m / (1.0 - ADAM_B1 ** ADAM_STEP)
    v_hat = v / (1.0 - ADAM_B2 ** ADAM_STEP)
    delta = -ADAM_LR * (m_hat / (_jnp.sqrt(v_hat) + ADAM_EPS) + ADAM_WD * w)
    return delta, m, v


def reference(x, w_in, mla_q_norm, mla_kv_norm, mla_w_uq, mla_w_ukv, gqa_q_norm, gqa_k_norm, rel_bias, w_out, ln1_g, ln1_b, ffn_w_gate, ffn_w_up, ffn_w_down, ln2_g, ln2_b, loss_target, m_w_in, m_mla_q_norm, m_mla_kv_norm, m_mla_w_uq, m_mla_w_ukv, m_gqa_q_norm, m_gqa_k_norm, m_rel_bias, m_w_out, m_ln1_g, m_ln1_b, m_ffn_w_gate, m_ffn_w_up, m_ffn_w_down, m_ln2_g, m_ln2_b, v_w_in, v_mla_q_norm, v_mla_kv_norm, v_mla_w_uq, v_mla_w_ukv, v_gqa_q_norm, v_gqa_k_norm, v_rel_bias, v_w_out, v_ln1_g, v_ln1_b, v_ffn_w_gate, v_ffn_w_up, v_ffn_w_down, v_ln2_g, v_ln2_b):
    given = dict(x=x, w_in=w_in, mla_q_norm=mla_q_norm, mla_kv_norm=mla_kv_norm, mla_w_uq=mla_w_uq, mla_w_ukv=mla_w_ukv, gqa_q_norm=gqa_q_norm, gqa_k_norm=gqa_k_norm, rel_bias=rel_bias, w_out=w_out, ln1_g=ln1_g, ln1_b=ln1_b, ffn_w_gate=ffn_w_gate, ffn_w_up=ffn_w_up, ffn_w_down=ffn_w_down, ln2_g=ln2_g, ln2_b=ln2_b, loss_target=loss_target, m_w_in=m_w_in, m_mla_q_norm=m_mla_q_norm, m_mla_kv_norm=m_mla_kv_norm, m_mla_w_uq=m_mla_w_uq, m_mla_w_ukv=m_mla_w_ukv, m_gqa_q_norm=m_gqa_q_norm, m_gqa_k_norm=m_gqa_k_norm, m_rel_bias=m_rel_bias, m_w_out=m_w_out, m_ln1_g=m_ln1_g, m_ln1_b=m_ln1_b, m_ffn_w_gate=m_ffn_w_gate, m_ffn_w_up=m_ffn_w_up, m_ffn_w_down=m_ffn_w_down, m_ln2_g=m_ln2_g, m_ln2_b=m_ln2_b, v_w_in=v_w_in, v_mla_q_norm=v_mla_q_norm, v_mla_kv_norm=v_mla_kv_norm, v_mla_w_uq=v_mla_w_uq, v_mla_w_ukv=v_mla_w_ukv, v_gqa_q_norm=v_gqa_q_norm, v_gqa_k_norm=v_gqa_k_norm, v_rel_bias=v_rel_bias, v_w_out=v_w_out, v_ln1_g=v_ln1_g, v_ln1_b=v_ln1_b, v_ffn_w_gate=v_ffn_w_gate, v_ffn_w_up=v_ffn_w_up, v_ffn_w_down=v_ffn_w_down, v_ln2_g=v_ln2_g, v_ln2_b=v_ln2_b)
    weights = {n: given[n] for n in TWIN_WEIGHTS}
    shared = {n: given[n] for n in SHARED_INPUTS}
    per_example = {n: given[n] for n in ['x']}
    grad_fn = _jax.value_and_grad(_loss, argnums=(0, 1))

    def one_microbatch(ex, loss_target):
        ex = dict(ex)
        diff = ex.pop(TWIN_DIFF_INPUT)
        return grad_fn(weights, diff, {**shared, **ex}, loss_target)

    if N_MICROBATCH == 1:
        loss, (grad_w, grad_x) = one_microbatch(per_example, given["loss_target"])
    else:
        def body(carry, xs):
            loss_sum, grad_sum = carry
            l_k, (gw_k, gx_k) = one_microbatch(xs[0], xs[1])
            with _jax.named_scope("update"):
                return (loss_sum + l_k, _jax.tree.map(_jnp.add, grad_sum, gw_k)), gx_k

        init = (_jnp.zeros((), _jnp.float32), _jax.tree.map(_jnp.zeros_like, weights))
        (loss, grad_w), grad_x = _jax.lax.scan(body, init, (per_example, given["loss_target"]))
    with _jax.named_scope("update"):
        delta_w, new_m, new_v = {}, {}, {}
        for n in TWIN_WEIGHTS:
            delta_w[n], new_m[n], new_v[n] = _adamw(weights[n], grad_w[n], given["m_" + n], given["v_" + n])
    return (loss, grad_x, *[grad_w[n] for n in TWIN_WEIGHTS], *[delta_w[n] for n in TWIN_WEIGHTS],
            *[new_m[n] for n in TWIN_WEIGHTS], *[new_v[n] for n in TWIN_WEIGHTS])
```

```python
import functools
import math

import numpy as np
import jax
import jax.numpy as jnp
from jax import lax
from jax.experimental import pallas as pl
from jax.experimental.pallas import tpu as pltpu

F32 = jnp.float32
CDT = jnp.bfloat16
WIRE = jnp.bfloat16

HEAD_DIM = 64
GRID_W = 64
ROPE_THETA = 10000.0
MLA_HEADS = 6
MLA_Q_RANK = 256
MLA_KV_RANK = 128
MLA_ROPE_DIM = 32
DIL_HEADS = 6
DIL_BRANCHES = ((128, 1), (512, 4), (2048, 16))
DIL_HALF = 64
GQA_Q_HEADS = 4
REL_BUCKETS = 32
REL_MAX_DIST = 1024
NEG_INF = -1e30
LANES = 128
VMEM_LIMIT = 56 * 1024 * 1024

ADAM_LR, ADAM_B1, ADAM_B2, ADAM_EPS, ADAM_WD, ADAM_STEP = 0.001, 0.9, 0.999, 1e-08, 0.01, 10

C_CQ, C_CKV, C_KR, C_DQ, C_DK, C_DV, C_GQ, C_GK, C_GV, IN_P = 0, 256, 384, 512, 896, 1280, 1664, 1920, 2048, 2176
IN_W = 2080
MESH_ID = pl.DeviceIdType.MESH


def _cparams(n_axes, vmem=VMEM_LIMIT):
    return pltpu.CompilerParams(dimension_semantics=("arbitrary",) * n_axes, vmem_limit_bytes=vmem)


def _pick(n, target):
    best = None
    for t in range(LANES, min(n, target) + 1, LANES):
        if n % t == 0:
            best = t
    return best if best is not None else n


def _sds(shape, dtype):
    return jax.ShapeDtypeStruct(tuple(shape), dtype)


def _in_cols():
    idx = -np.ones((IN_P,), np.int64)
    idx[C_CQ:C_CQ + 256] = np.arange(0, 256)
    idx[C_CKV:C_CKV + 128] = np.arange(256, 384)
    idx[C_KR + 64:C_KR + 96] = np.arange(384, 416)
    idx[C_DQ:C_DQ + 1152] = np.arange(416, 1568)
    gq = 1568 + (np.array([0, 2, 1, 3])[:, None] * 64 + np.arange(64)[None, :]).reshape(-1)
    idx[C_GQ:C_GQ + 256] = gq
    idx[C_GK:C_GK + 256] = np.arange(1824, 2080)
    return idx


def _uq_cols():
    idx = -np.ones((MLA_HEADS * 128,), np.int64)
    for h in range(MLA_HEADS):
        idx[h * 128:h * 128 + 96] = np.arange(96 * h, 96 * h + 96)
    return idx


def _ukv_cols():
    idx = -np.ones((MLA_HEADS * 128 + MLA_HEADS * 64,), np.int64)
    for h in range(MLA_HEADS):
        idx[h * 128:h * 128 + 64] = np.arange(128 * h, 128 * h + 64)
        idx[768 + h * 64:768 + h * 64 + 64] = np.arange(128 * h + 64, 128 * h + 128)
    return idx


def _out_rows():
    idx = np.arange(1024)
    idx[768:1024] = 768 + (np.array([0, 2, 1, 3])[:, None] * 64 + np.arange(64)[None, :]).reshape(-1)
    return idx


def _take_cols(w, idx):
    n = w.shape[-1]
    wz = jnp.concatenate([w, jnp.zeros(w.shape[:-1] + (1,), w.dtype)], axis=-1)
    return jnp.take(wz, jnp.asarray(np.where(idx < 0, n, idx), jnp.int32), axis=-1)


def _untake_cols(wp, idx, n):
    inv = np.zeros((n,), np.int64)
    pos = np.nonzero(idx >= 0)[0]
    inv[idx[pos]] = pos
    return jnp.take(wp, jnp.asarray(inv, jnp.int32), axis=-1)


def _t5_bucket_np(rel):
    nb = REL_BUCKETS // 2
    exact = nb // 2
    ret = np.where(rel > 0, nb, 0)
    n = np.abs(rel)
    nf = np.maximum(n, 1).astype(np.float32)
    large = exact + (np.log(nf / np.float32(exact)) / np.float32(math.log(REL_MAX_DIST / exact))
                     * np.float32(nb - exact)).astype(np.int32)
    large = np.minimum(large, nb - 1)
    return ret + np.where(n < exact, n, large)


def _branch_bucket_idx(tq, dil):
    kw = tq + 2 * DIL_HALF
    rel = np.arange(kw)[None, :] - DIL_HALF - np.arange(tq)[:, None]
    idx = _t5_bucket_np(rel * dil)
    return np.where(np.abs(rel) <= DIL_HALF, idx, -1).astype(np.int32)


def _rope_tables(S):
    inv = ROPE_THETA ** (-jnp.arange(0, 32, 2, dtype=F32) / 32)
    t = jnp.arange(S)
    pos = t.astype(F32)
    row = (t // GRID_W).astype(F32)
    col = (t % GRID_W).astype(F32)
    lane = np.arange(LANES)
    wm = lane - 64
    is_rope = (lane >= 64) & (lane < 96)
    ang = pos[:, None] * inv[np.where(is_rope, wm % 16, 0)][None, :]
    cm = jnp.where(is_rope[None], jnp.cos(ang), 1.0)
    smm = jnp.where((is_rope & (wm < 16))[None], -jnp.sin(ang), 0.0)
    spm = jnp.where((is_rope & (wm >= 16))[None], jnp.sin(ang), 0.0)
    g = lane % 64
    w = g % 32
    angg = jnp.where((g < 32)[None], row[:, None], col[:, None]) * inv[w % 16][None, :]
    cg = jnp.cos(angg)
    smg = jnp.where((w < 16)[None], -jnp.sin(angg), 0.0)
    spg = jnp.where((w >= 16)[None], jnp.sin(angg), 0.0)
    return (cm, smm, spm), (cg, smg, spg)


def _lanes(t, width):
    return t if width == LANES else jnp.concatenate([t] * (width // LANES), axis=1)


def _rope(x, tabs):
    c, sm, sp = (_lanes(t, x.shape[1]) for t in tabs)
    w = x.shape[1]
    return x * c + pltpu.roll(x, w - 16, 1) * sm + pltpu.roll(x, 16, 1) * sp


def _rope_t(dy, tabs):
    c, sm, sp = (_lanes(t, dy.shape[1]) for t in tabs)
    w = dy.shape[1]
    return dy * c + pltpu.roll(dy * sm, 16, 1) + pltpu.roll(dy * sp, w - 16, 1)


def _head_ones(width):
    i = np.arange(width)
    return jnp.asarray((i[:, None] // HEAD_DIM == i[None, :] // HEAD_DIM).astype(np.float32))


def _headsum(x, j):
    return jnp.dot(x, j, preferred_element_type=F32, precision=lax.Precision.HIGHEST)


def _mm(a, b, *, ta=False, tb=False, out_dtype=F32, name):
    K, M = a.shape if ta else a.shape[::-1]
    N = b.shape[0] if tb else b.shape[1]
    assert (b.shape[1] if tb else b.shape[0]) == K
    tm, tn, tk = _pick(M, 512), _pick(N, 1024), _pick(K, 1024)
    if tn == N and N > 2304:
        tn = _pick(N, 1536)
    nk = K // tk
    dn = (((0 if ta else 1,), (1 if tb else 0,)), ((), ()))

    def body(a_ref, b_ref, o_ref, *acc):
        part = lax.dot_general(a_ref[...], b_ref[...], dn, preferred_element_type=F32)
        if nk == 1:
            o_ref[...] = part.astype(o_ref.dtype)
            return
        acc_ref, = acc
        k = pl.program_id(2)

        @pl.when(k == 0)
        def _():
            acc_ref[...] = part

        @pl.when(k > 0)
        def _():
            acc_ref[...] += part

        @pl.when(k == nk - 1)
        def _():
            o_ref[...] = acc_ref[...].astype(o_ref.dtype)

    a_spec = pl.BlockSpec((tk, tm), lambda i, j, k: (k, i)) if ta else pl.BlockSpec((tm, tk), lambda i, j, k: (i, k))
    b_spec = pl.BlockSpec((tn, tk), lambda i, j, k: (j, k)) if tb else pl.BlockSpec((tk, tn), lambda i, j, k: (k, j))
    return pl.pallas_call(
        body, name=name, grid=(M // tm, N // tn, nk),
        in_specs=[a_spec, b_spec], out_specs=pl.BlockSpec((tm, tn), lambda i, j, k: (i, j)),
        out_shape=_sds((M, N), out_dtype),
        scratch_shapes=[pltpu.VMEM((tm, tn), F32)] if nk > 1 else [],
        compiler_params=_cparams(3),
    )(a, b)


def _row(ts, w, cb=0):
    return pl.BlockSpec((ts, w), lambda i: (i, cb))


def _full(shape):
    nd = len(shape)
    return pl.BlockSpec(tuple(shape), lambda i: (0,) * nd)


def _rms_fwd(x, g, eps=1e-6):
    r = lax.rsqrt(jnp.mean(x * x, axis=-1, keepdims=True) + eps)
    return x * r * g


def _rms_bwd(x, g, dy, eps=1e-6):
    r = lax.rsqrt(jnp.mean(x * x, axis=-1, keepdims=True) + eps)
    gdy = g * dy
    dx = r * gdy - x * (r * r * r) * jnp.mean(x * gdy, axis=-1, keepdims=True)
    return dx, x * r * dy


def _rms_head_fwd(x, g, j, eps=1e-6):
    r = lax.rsqrt(_headsum(x * x, j) * (1.0 / HEAD_DIM) + eps)
    return x * r * g


def _rms_head_bwd(x, g, dy, j, eps=1e-6):
    r = lax.rsqrt(_headsum(x * x, j) * (1.0 / HEAD_DIM) + eps)
    gdy = g * dy
    dx = r * gdy - x * (r * r * r) * (_headsum(x * gdy, j) * (1.0 / HEAD_DIM))
    return dx, x * r * dy


def _prep_fwd(h, gq, gkv, ggq, ggk, tm, tg, j256):
    S = h.shape[0]
    ts = min(256, S)
    scale = HEAD_DIM ** -0.5

    def body(h_ref, gq_ref, gkv_ref, ggq_ref, ggk_ref, cm, smm, spm, cg, smg, spg, j_ref,
             cq_o, ckv_o, kr_o, dq_o, dk_o, dv_o, gq_o, gk_o, gv_o):
        tabm = (cm[...], smm[...], spm[...])
        tabg = (cg[...], smg[...], spg[...])
        cq_o[...] = _rms_fwd(h_ref[:, C_CQ:C_CQ + 256], gq_ref[...]).astype(CDT)
        ckv_o[...] = _rms_fwd(h_ref[:, C_CKV:C_CKV + 128], gkv_ref[...]).astype(CDT)
        kr_o[...] = _rope(h_ref[:, C_KR:C_KR + 128], tabm).astype(CDT)
        dq_o[...] = (h_ref[:, C_DQ:C_DQ + 384] * scale).astype(CDT)
        dk_o[...] = h_ref[:, C_DK:C_DK + 384].astype(CDT)
        dv_o[...] = h_ref[:, C_DV:C_DV + 384].astype(CDT)
        qn = _rms_head_fwd(h_ref[:, C_GQ:C_GQ + 256], ggq_ref[...], j_ref[...])
        gq_o[...] = (_rope(qn, tabg) * scale).astype(CDT)
        kn = _rms_head_fwd(h_ref[:, C_GK:C_GK + 128], ggk_ref[...], j_ref[0:128, 0:128])
        gk_o[...] = _rope(kn, tabg).astype(CDT)
        gv_o[...] = h_ref[:, C_GV:C_GV + 128].astype(CDT)

    widths = (256, 128, 128, 384, 384, 384, 256, 128, 128)
    return pl.pallas_call(
        body, name="prep_fwd", grid=(S // ts,),
        in_specs=[_row(ts, IN_P), _full(gq.shape), _full(gkv.shape), _full(ggq.shape), _full(ggk.shape)]
        + [_row(ts, LANES)] * 6 + [_full(j256.shape)],
        out_specs=[_row(ts, w) for w in widths],
        out_shape=[_sds((S, w), CDT) for w in widths],
        compiler_params=_cparams(1),
    )(h, gq, gkv, ggq, ggk, *tm, *tg, j256)


def _prep_bwd(h, dcq, dckv, dkr, ddq, ddk, ddv, dgq, dgk, dgv, gq, gkv, ggq, ggk, tg, j256):
    S = h.shape[0]
    ts = min(256, S)
    scale = HEAD_DIM ** -0.5

    def body(h_ref, dcq_r, dckv_r, dkr_r, q1, q2, q3, k1, k2, k3, v1, v2, v3, dgq_r, dgk_r, dgv_r,
             gq_ref, gkv_ref, ggq_ref, ggk_ref, cg, smg, spg, j_ref,
             dh_o, ngq_o, ngkv_o, nggq_o, nggk_o):
        tabg = (cg[...], smg[...], spg[...])
        first = pl.program_id(0) == 0

        def acc(o_ref, val):
            s = jnp.sum(val, axis=0, keepdims=True)

            @pl.when(first)
            def _():
                o_ref[...] = s

            @pl.when(jnp.logical_not(first))
            def _():
                o_ref[...] += s

        dx, dg = _rms_bwd(h_ref[:, C_CQ:C_CQ + 256], gq_ref[...], dcq_r[...])
        dh_o[:, C_CQ:C_CQ + 256] = dx.astype(CDT)
        acc(ngq_o, dg)
        dx, dg = _rms_bwd(h_ref[:, C_CKV:C_CKV + 128], gkv_ref[...], dckv_r[...])
        dh_o[:, C_CKV:C_CKV + 128] = dx.astype(CDT)
        acc(ngkv_o, dg)
        dh_o[:, C_KR:C_KR + 128] = dkr_r[...].astype(CDT)
        dh_o[:, C_DQ:C_DQ + 384] = ((q1[...] + q2[...] + q3[...]) * scale).astype(CDT)
        dh_o[:, C_DK:C_DK + 384] = (k1[...] + k2[...] + k3[...]).astype(CDT)
        dh_o[:, C_DV:C_DV + 384] = (v1[...] + v2[...] + v3[...]).astype(CDT)
        dqn = _rope_t(dgq_r[...] * scale, tabg)
        dx, dg = _rms_head_bwd(h_ref[:, C_GQ:C_GQ + 256], ggq_ref[...], dqn, j_ref[...])
        dh_o[:, C_GQ:C_GQ + 256] = dx.astype(CDT)
        acc(nggq_o, dg)
        dkn = _rope_t(dgk_r[...], tabg)
        dx, dg = _rms_head_bwd(h_ref[:, C_GK:C_GK + 128], ggk_ref[...], dkn, j_ref[0:128, 0:128])
        dh_o[:, C_GK:C_GK + 128] = dx.astype(CDT)
        acc(nggk_o, dg)
        dh_o[:, C_GV:C_GV + 128] = dgv_r[...].astype(CDT)

    return pl.pallas_call(
        body, name="prep_bwd", grid=(S // ts,),
        in_specs=[_row(ts, IN_P), _row(ts, 256), _row(ts, 128), _row(ts, 128)] + [_row(ts, 384)] * 9
        + [_row(ts, 256), _row(ts, 128), _row(ts, 128)]
        + [_full(gq.shape), _full(gkv.shape), _full(ggq.shape), _full(ggk.shape)] + [_row(ts, LANES)] * 3
        + [_full(j256.shape)],
        out_specs=[_row(ts, IN_P), _full((1, 256)), _full((1, 128)), _full((1, 256)), _full((1, 128))],
        out_shape=[_sds((S, IN_P), CDT), _sds((1, 256), F32), _sds((1, 128), F32), _sds((1, 256), F32),
                   _sds((1, 128), F32)],
        compiler_params=_cparams(1),
    )(h, dcq, dckv, dkr, *ddq, *ddk, *ddv, dgq, dgk, dgv, gq, gkv, ggq, ggk, *tg, j256)


def _mla_prep_fwd(qa, kvp, kr, tm, scale):
    S = qa.shape[0]
    ts = min(256, S)

    def body(qa_ref, kv_ref, kr_ref, cm, smm, spm, q_o, k_o):
        tabm = (cm[...], smm[...], spm[...])
        q_o[...] = (_rope(qa_ref[...], tabm) * scale).astype(CDT)
        k_o[...] = kv_ref[:, 0:768] + _lanes(kr_ref[...], 768)

    return pl.pallas_call(
        body, name="mla_prep_fwd", grid=(S // ts,),
        in_specs=[_row(ts, 768), _row(ts, 1152), _row(ts, 128)] + [_row(ts, LANES)] * 3,
        out_specs=[_row(ts, 768)] * 2, out_shape=[_sds((S, 768), CDT)] * 2,
        compiler_params=_cparams(1),
    )(qa, kvp, kr, *tm)


def _mla_prep_bwd(dq, dk, dv, tm, scale):
    S = dq.shape[0]
    ts = min(256, S)

    def body(dq_ref, dk_ref, dv_ref, cm, smm, spm, dqa_o, dkv_o, dkr_o):
        tabm = (cm[...], smm[...], spm[...])
        lane = lax.broadcasted_iota(jnp.int32, (1, LANES), 1)
        dqa_o[...] = _rope_t(dq_ref[...] * scale, tabm).astype(CDT)
        dkr = jnp.zeros((ts, LANES), F32)
        for hd in range(MLA_HEADS):
            blk = dk_ref[:, hd * 128:(hd + 1) * 128]
            dkv_o[:, hd * 128:(hd + 1) * 128] = jnp.where(lane < 64, blk, 0.0).astype(CDT)
            dkr = dkr + jnp.where((lane >= 64) & (lane < 96), blk, 0.0)
        dkv_o[:, 768:1152] = dv_ref[...].astype(CDT)
        dkr_o[...] = jnp.where((lane >= 64) & (lane < 96), _rope_t(dkr, tabm), 0.0)

    return pl.pallas_call(
        body, name="mla_prep_bwd", grid=(S // ts,),
        in_specs=[_row(ts, 768), _row(ts, 768), _row(ts, 384)] + [_row(ts, LANES)] * 3,
        out_specs=[_row(ts, 768), _row(ts, 1152), _row(ts, 128)],
        out_shape=[_sds((S, 768), CDT), _sds((S, 1152), CDT), _sds((S, 128), F32)],
        compiler_params=_cparams(1),
    )(dq, dk, dv, *tm)


def _ln_fwd(xa, xb, g, b, alpha, name):
    S, D = xa.shape
    ts = min(256, S)

    def body(xa_ref, xb_ref, g_ref, b_ref, y_o, yb_o, z_o):
        z = alpha * xa_ref[...] + xb_ref[...]
        mu = jnp.mean(z, axis=-1, keepdims=True)
        zc = z - mu
        var = jnp.mean(zc * zc, axis=-1, keepdims=True)
        y = zc * lax.rsqrt(var + 1e-5) * g_ref[...] + b_ref[...]
        y_o[...] = y
        yb_o[...] = y.astype(CDT)
        z_o[...] = z

    return pl.pallas_call(
        body, name=name, grid=(S // ts,),
        in_specs=[_row(ts, D), _row(ts, D), _full(g.shape), _full(b.shape)],
        out_specs=[_row(ts, D)] * 3, out_shape=[_sds((S, D), F32), _sds((S, D), CDT), _sds((S, D), F32)],
        compiler_params=_cparams(1),
    )(xa, xb, g, b)


def _ln_bwd(dya, dyb, z, g, alpha, name):
    S, D = z.shape
    ts = min(256, S)
    two = dyb is not None

    def body(*refs):
        if two:
            dya_ref, dyb_ref, z_ref, g_ref, dz_o, dzb_o, dg_o, db_o = refs
            dy = dya_ref[...] + alpha * dyb_ref[...]
        else:
            dya_ref, z_ref, g_ref, dz_o, dzb_o, dg_o, db_o = refs
            dy = dya_ref[...]
        z = z_ref[...]
        mu = jnp.mean(z, axis=-1, keepdims=True)
        zc = z - mu
        r = lax.rsqrt(jnp.mean(zc * zc, axis=-1, keepdims=True) + 1e-5)
        xh = zc * r
        dxh = dy * g_ref[...]
        dz = r * (dxh - jnp.mean(dxh, axis=-1, keepdims=True) - xh * jnp.mean(dxh * xh, axis=-1, keepdims=True))
        dz_o[...] = dz
        dzb_o[...] = dz.astype(CDT)
        sg = jnp.sum(dy * xh, axis=0, keepdims=True)
        sb = jnp.sum(dy, axis=0, keepdims=True)
        first = pl.program_id(0) == 0

        @pl.when(first)
        def _():
            dg_o[...] = sg
            db_o[...] = sb

        @pl.when(jnp.logical_not(first))
        def _():
            dg_o[...] += sg
            db_o[...] += sb

    ins = [dya] + ([dyb] if two else []) + [z, g]
    return pl.pallas_call(
        body, name=name, grid=(S // ts,),
        in_specs=[_row(ts, D)] * (3 if two else 2) + [_full(g.shape)],
        out_specs=[_row(ts, D), _row(ts, D), _full((1, D)), _full((1, D))],
        out_shape=[_sds((S, D), F32), _sds((S, D), CDT), _sds((1, D), F32), _sds((1, D), F32)],
        compiler_params=_cparams(1),
    )(*ins)


def _swiglu_fwd(gu):
    S, F2 = gu.shape
    Fh = F2 // 2
    ts = min(256, S)

    def body(gu_ref, a_o):
        g = gu_ref[:, 0:Fh]
        a_o[...] = (g / (1.0 + jnp.exp(-g)) * gu_ref[:, Fh:F2]).astype(CDT)

    return pl.pallas_call(
        body, name="swiglu_fwd", grid=(S // ts,), in_specs=[_row(ts, F2)], out_specs=_row(ts, Fh),
        out_shape=_sds((S, Fh), CDT), compiler_params=_cparams(1),
    )(gu)


def _swiglu_bwd(da, gu):
    S, F2 = gu.shape
    Fh = F2 // 2
    ts = min(256, S)

    def body(da_ref, gu_ref, d_o):
        g = gu_ref[:, 0:Fh]
        u = gu_ref[:, Fh:F2]
        sg = 1.0 / (1.0 + jnp.exp(-g))
        da = da_ref[...]
        d_o[:, 0:Fh] = (da * u * (sg * (1.0 + g * (1.0 - sg)))).astype(CDT)
        d_o[:, Fh:F2] = (da * (g * sg)).astype(CDT)

    return pl.pallas_call(
        body, name="swiglu_bwd", grid=(S // ts,), in_specs=[_row(ts, Fh), _row(ts, F2)], out_specs=_row(ts, F2),
        out_shape=_sds((S, F2), CDT), compiler_params=_cparams(1),
    )(da, gu)


def _loss_kernel(y, target):
    S, D = y.shape
    ts = min(256, S)

    def body(y_ref, t_ref, dy_o, loss_o):
        e = y_ref[...] - t_ref[...]
        dy_o[...] = e * (1.0 / D)
        part = jnp.sum(jnp.sum(e * e, axis=1, keepdims=True), axis=0, keepdims=True) * (0.5 / D)
        first = pl.program_id(0) == 0

        @pl.when(first)
        def _():
            loss_o[...] = part

        @pl.when(jnp.logical_not(first))
        def _():
            loss_o[...] += part

    return pl.pallas_call(
        body, name="loss", grid=(S // ts,), in_specs=[_row(ts, D)] * 2,
        out_specs=[_row(ts, D), _full((1, 1))], out_shape=[_sds((S, D), F32), _sds((1, 1), F32)],
        compiler_params=_cparams(1),
    )(y, target)


def _axpy(a, b, alpha, name):
    S, D = a.shape
    ts = min(256, S)

    def body(a_ref, b_ref, o_ref):
        o_ref[...] = a_ref[...] + alpha * b_ref[...]

    return pl.pallas_call(
        body, name=name, grid=(S // ts,), in_specs=[_row(ts, D)] * 2, out_specs=_row(ts, D),
        out_shape=_sds((S, D), F32), compiler_params=_cparams(1),
    )(a, b)


def _pair_masks():
    lane = lax.broadcasted_iota(jnp.int32, (1, LANES), 1)
    first = lane < HEAD_DIM
    return first, jnp.logical_not(first)


def _head_scalar(x, m):
    return jnp.max(jnp.where(m, x, -jnp.inf), axis=-1, keepdims=True)


_NT = (((1,), (1,)), ((), ()))
_TN = (((0,), (0,)), ((), ()))


def _attn_fwd(q, k, v, *, split, npairs, kblk, vblk, name):
    S = q.shape[0]
    qw = 256 if split else LANES
    tq = min(256, S)

    def body(q_ref, k_ref, v_ref, o_ref, lse_ref):
        masks = _pair_masks()
        outs, lses = [], []
        for hd in range(2):
            if split:
                qh = q_ref[:, hd * LANES:(hd + 1) * LANES]
                kh = k_ref[:, hd * LANES:(hd + 1) * LANES]
            else:
                qh = jnp.where(masks[hd], q_ref[...], jnp.zeros_like(q_ref[...]))
                kh = k_ref[...]
            s = lax.dot_general(qh, kh, _NT, preferred_element_type=F32)
            mx = jnp.max(s, axis=-1, keepdims=True)
            p = jnp.exp(s - mx)
            l = jnp.sum(p, axis=-1, keepdims=True)
            o = jnp.dot(p.astype(CDT), v_ref[...], preferred_element_type=F32)
            outs.append(o / l)
            lses.append(jnp.broadcast_to(mx + jnp.log(l), (tq, LANES)))
        o_ref[...] = jnp.where(masks[0], outs[0], outs[1]).astype(o_ref.dtype)
        lse_ref[...] = jnp.where(masks[0], lses[0], lses[1])

    return pl.pallas_call(
        body, name=name, grid=(npairs, S // tq),
        in_specs=[pl.BlockSpec((tq, qw), lambda p, i: (i, p)),
                  pl.BlockSpec((S, qw), lambda p, i: (0, kblk(p))),
                  pl.BlockSpec((S, LANES), lambda p, i: (0, vblk(p)))],
        out_specs=[pl.BlockSpec((tq, LANES), lambda p, i: (i, p))] * 2,
        out_shape=[_sds((S, LANES * npairs), CDT), _sds((S, LANES * npairs), F32)],
        compiler_params=_cparams(2),
    )(q, k, v)


def _attn_bwd(q, k, v, do, o, lse, *, split, npairs, kblk, vblk, doblk, shared_kv, name):
    S = q.shape[0]
    qw = 256 if split else LANES
    tq = min(256, S)
    nkv = 1 if shared_kv else npairs

    def body(q_ref, k_ref, v_ref, do_ref, o_ref, lse_ref, dq_ref, dk_ref, dv_ref):
        masks = _pair_masks()
        p_id, i_id = pl.program_id(0), pl.program_id(1)
        first = (i_id == 0) & ((p_id == 0) if shared_kv else True)
        do = do_ref[...]
        o = o_ref[...].astype(F32)
        lse = lse_ref[...]
        v = v_ref[...]
        dqs, dks, dvs = [], [], []
        for hd in range(2):
            m = masks[hd]
            if split:
                qh = q_ref[:, hd * LANES:(hd + 1) * LANES]
                kh = k_ref[:, hd * LANES:(hd + 1) * LANES]
            else:
                qh = jnp.where(m, q_ref[...], jnp.zeros_like(q_ref[...]))
                kh = k_ref[...]
            doh = jnp.where(m, do, 0.0)
            s = lax.dot_general(qh, kh, _NT, preferred_element_type=F32)
            p = jnp.exp(s - _head_scalar(lse, m))
            delta = jnp.sum(doh * o, axis=-1, keepdims=True)
            dohb = doh.astype(CDT)
            dp = lax.dot_general(dohb, v, _NT, preferred_element_type=F32)
            ds = (p * (dp - delta)).astype(CDT)
            dq = jnp.dot(ds, kh, preferred_element_type=F32)
            dqs.append(dq if split else jnp.where(m, dq, 0.0))
            dks.append(lax.dot_general(ds, qh, _TN, preferred_element_type=F32))
            dvs.append(lax.dot_general(p.astype(CDT), dohb, _TN, preferred_element_type=F32))
        if split:
            dq_ref[:, 0:LANES] = dqs[0]
            dq_ref[:, LANES:2 * LANES] = dqs[1]
        else:
            dq_ref[...] = dqs[0] + dqs[1]
        dv = dvs[0] + dvs[1]

        @pl.when(first)
        def _():
            if split:
                dk_ref[:, 0:LANES] = dks[0]
                dk_ref[:, LANES:2 * LANES] = dks[1]
            else:
                dk_ref[...] = dks[0] + dks[1]
            dv_ref[...] = dv

        @pl.when(jnp.logical_not(first))
        def _():
            if split:
                dk_ref[:, 0:LANES] += dks[0]
                dk_ref[:, LANES:2 * LANES] += dks[1]
            else:
                dk_ref[...] += dks[0] + dks[1]
            dv_ref[...] += dv

    kvo = (lambda p, i: (0, 0)) if shared_kv else (lambda p, i: (0, p))
    return pl.pallas_call(
        body, name=name, grid=(npairs, S // tq),
        in_specs=[pl.BlockSpec((tq, qw), lambda p, i: (i, p)),
                  pl.BlockSpec((S, qw), lambda p, i: (0, kblk(p))),
                  pl.BlockSpec((S, LANES), lambda p, i: (0, vblk(p))),
                  pl.BlockSpec((tq, LANES), lambda p, i: (i, doblk(p))),
                  pl.BlockSpec((tq, LANES), lambda p, i: (i, p)),
                  pl.BlockSpec((tq, LANES), lambda p, i: (i, p))],
        out_specs=[pl.BlockSpec((tq, qw), lambda p, i: (i, p)),
                   pl.BlockSpec((S, qw), kvo), pl.BlockSpec((S, LANES), kvo)],
        out_shape=[_sds((S, qw * npairs), F32), _sds((S, qw * nkv), F32), _sds((S, LANES * nkv), F32)],
        compiler_params=_cparams(2),
    )(q, k, v, do, o, lse)


def _bias_expand(idx, rel_bias, name):
    tq, kw = idx.shape

    def body(idx_ref, rb_ref, o_ref):
        idx = idx_ref[...]
        for hd in range(DIL_HEADS):
            acc = jnp.full((tq, kw), NEG_INF, F32)
            for u in range(REL_BUCKETS):
                acc = jnp.where(idx == u, rb_ref[u, hd], acc)
            o_ref[hd] = acc

    return pl.pallas_call(
        body, name=name,
        in_specs=[pl.BlockSpec(memory_space=pltpu.VMEM), pl.BlockSpec(memory_space=pltpu.SMEM)],
        out_specs=pl.BlockSpec(memory_space=pltpu.VMEM),
        out_shape=_sds((DIL_HEADS, tq, kw), F32),
    )(idx, rel_bias)


def _bias_reduce(idx, dtab, name):
    tq, kw = idx.shape

    def body(idx_ref, d_ref, o_ref):
        idx = idx_ref[...]
        rowid = lax.broadcasted_iota(jnp.int32, (REL_BUCKETS, kw), 0)
        for hd in range(DIL_HEADS):
            d = d_ref[hd]
            acc = jnp.zeros((REL_BUCKETS, kw), F32)
            for u in range(REL_BUCKETS):
                r = jnp.sum(jnp.where(idx == u, d, 0.0), axis=0, keepdims=True)
                acc = jnp.where(rowid == u, r, acc)
            o_ref[hd] = jnp.sum(acc, axis=1, keepdims=True)

    return pl.pallas_call(
        body, name=name,
        in_specs=[pl.BlockSpec(memory_space=pltpu.VMEM)] * 2, out_specs=pl.BlockSpec(memory_space=pltpu.VMEM),
        out_shape=_sds((DIL_HEADS, REL_BUCKETS, 1), F32),
    )(idx, dtab)


def _dil_window(i, tq, kw, L):
    start = pl.multiple_of(i * tq, DIL_HALF)
    key = start + lax.broadcasted_iota(jnp.int32, (1, kw), 1) - DIL_HALF
    return start, (key >= 0) & (key < L)


def _dil_fwd(qv, kv, vv, tab, *, dil, L, tq, name):
    kw = tq + 2 * DIL_HALF
    npair = DIL_HEADS // 2

    def body(q_ref, k_ref, v_ref, t_ref, o_ref, lse_ref):
        masks = _pair_masks()
        start, valid = _dil_window(pl.program_id(2), tq, kw, L)
        kwin = k_ref[pl.ds(start, kw), :]
        vwin = v_ref[pl.ds(start, kw), :]
        outs, lses = [], []
        for hd in range(2):
            qh = jnp.where(masks[hd], q_ref[...], jnp.zeros_like(q_ref[...]))
            s = lax.dot_general(qh, kwin, _NT, preferred_element_type=F32) + t_ref[hd]
            s = jnp.where(valid, s, NEG_INF)
            mx = jnp.max(s, axis=-1, keepdims=True)
            p = jnp.exp(s - mx)
            l = jnp.sum(p, axis=-1, keepdims=True)
            outs.append(jnp.dot(p.astype(CDT), vwin, preferred_element_type=F32) / l)
            lses.append(jnp.broadcast_to(mx + jnp.log(l), (tq, LANES)))
        o_ref[...] = jnp.where(masks[0], outs[0], outs[1])
        lse_ref[...] = jnp.where(masks[0], lses[0], lses[1])

    blk = lambda p, c, i: (i, c * npair + p)
    res = lambda p, c, i: (0, c * npair + p)
    return pl.pallas_call(
        body, name=name, grid=(npair, dil, L // tq),
        in_specs=[pl.BlockSpec((tq, LANES), blk), pl.BlockSpec((L + 2 * DIL_HALF, LANES), res),
                  pl.BlockSpec((L + 2 * DIL_HALF, LANES), res), pl.BlockSpec((2, tq, kw), lambda p, c, i: (p, 0, 0))],
        out_specs=[pl.BlockSpec((tq, LANES), blk)] * 2,
        out_shape=[_sds(qv.shape, F32)] * 2,
        compiler_params=_cparams(3),
    )(qv, kv, vv, tab)


def _dil_bwd(qv, kv, vv, tab, dov, lsev, deltav, *, dil, L, tq, name):
    kw = tq + 2 * DIL_HALF
    npair = DIL_HEADS // 2

    def body(q_ref, k_ref, v_ref, t_ref, do_ref, lse_ref, dl_ref, dq_ref, dk_ref, dv_ref, dt_ref):
        masks = _pair_masks()
        c_id, i_id = pl.program_id(1), pl.program_id(2)
        start, valid = _dil_window(i_id, tq, kw, L)
        kwin = k_ref[pl.ds(start, kw), :]
        vwin = v_ref[pl.ds(start, kw), :]

        @pl.when(i_id == 0)
        def _():
            dk_ref[...] = jnp.zeros_like(dk_ref)
            dv_ref[...] = jnp.zeros_like(dv_ref)

        @pl.when((i_id == 0) & (c_id == 0))
        def _():
            dt_ref[...] = jnp.zeros_like(dt_ref)

        do = do_ref[...]
        dq = jnp.zeros((tq, LANES), F32)
        dk = jnp.zeros((kw, LANES), F32)
        dv = jnp.zeros((kw, LANES), F32)
        for hd in range(2):
            m = masks[hd]
            qh = jnp.where(m, q_ref[...], jnp.zeros_like(q_ref[...]))
            doh = jnp.where(m, do, jnp.zeros_like(do))
            s = lax.dot_general(qh, kwin, _NT, preferred_element_type=F32) + t_ref[hd]
            s = jnp.where(valid, s, NEG_INF)
            p = jnp.exp(s - _head_scalar(lse_ref[...], m))
            dp = lax.dot_general(doh, vwin, _NT, preferred_element_type=F32)
            ds = p * (dp - _head_scalar(dl_ref[...], m))
            dt_ref[hd] += ds
            dsb = ds.astype(CDT)
            dq = dq + jnp.where(m, jnp.dot(dsb, kwin, preferred_element_type=F32), 0.0)
            dk = dk + lax.dot_general(dsb, qh, _TN, preferred_element_type=F32)
            dv = dv + lax.dot_general(p.astype(CDT), doh, _TN, preferred_element_type=F32)
        dq_ref[...] = dq
        dk_ref[pl.ds(start, kw), :] += dk
        dv_ref[pl.ds(start, kw), :] += dv

    blk = lambda p, c, i: (i, c * npair + p)
    res = lambda p, c, i: (0, c * npair + p)
    tsp = pl.BlockSpec((2, tq, kw), lambda p, c, i: (p, 0, 0))
    Lp = L + 2 * DIL_HALF
    return pl.pallas_call(
        body, name=name, grid=(npair, dil, L // tq),
        in_specs=[pl.BlockSpec((tq, LANES), blk), pl.BlockSpec((Lp, LANES), res), pl.BlockSpec((Lp, LANES), res), tsp,
                  pl.BlockSpec((tq, LANES), blk), pl.BlockSpec((tq, LANES), blk), pl.BlockSpec((tq, LANES), blk)],
        out_specs=[pl.BlockSpec((tq, LANES), blk), pl.BlockSpec((Lp, LANES), res), pl.BlockSpec((Lp, LANES), res), tsp],
        out_shape=[_sds(qv.shape, F32), _sds(kv.shape, F32), _sds(kv.shape, F32), _sds(tab.shape, F32)],
        compiler_params=_cparams(3),
    )(qv, kv, vv, tab, dov, lsev, deltav)


def _mix_weights(l1, l2, l3):
    mx = jnp.maximum(jnp.maximum(l1, l2), l3)
    e1, e2, e3 = jnp.exp(l1 - mx), jnp.exp(l2 - mx), jnp.exp(l3 - mx)
    inv = 1.0 / (e1 + e2 + e3)
    return e1 * inv, e2 * inv, e3 * inv


def _dil_mix_fwd(os, ls):
    S, W = os[0].shape
    ts = min(256, S)

    def body(o1, o2, o3, l1, l2, l3, out):
        w1, w2, w3 = _mix_weights(l1[...], l2[...], l3[...])
        out[...] = (w1 * o1[...] + w2 * o2[...] + w3 * o3[...]).astype(CDT)

    return pl.pallas_call(
        body, name="dil_mix_fwd", grid=(S // ts,), in_specs=[_row(ts, W)] * 6, out_specs=_row(ts, W),
        out_shape=_sds((S, W), CDT), compiler_params=_cparams(1),
    )(*os, *ls)


def _dil_mix_bwd(dcat, os, ls, j384):
    S, W = os[0].shape
    ts = min(256, S)

    def body(do_ref, o1, o2, o3, l1, l2, l3, j_ref, d1, d2, d3, e1, e2, e3):
        ws = _mix_weights(l1[...], l2[...], l3[...])
        do = do_ref[...]
        o = ws[0] * o1[...] + ws[1] * o2[...] + ws[2] * o3[...]
        dot = _headsum(do * o, j_ref[...])
        for w, d_o, e_o in zip(ws, (d1, d2, d3), (e1, e2, e3)):
            d_o[...] = (w * do).astype(CDT)
            e_o[...] = w * dot

    return pl.pallas_call(
        body, name="dil_mix_bwd", grid=(S // ts,),
        in_specs=[_row(ts, W, 1)] + [_row(ts, W)] * 6 + [_full(j384.shape)],
        out_specs=[_row(ts, W)] * 6,
        out_shape=[_sds((S, W), CDT)] * 3 + [_sds((S, W), F32)] * 3,
        compiler_params=_cparams(1),
    )(dcat, *os, *ls, j384)


def _adamw_math(w, g, m, v):
    m = ADAM_B1 * m + (1.0 - ADAM_B1) * g
    v = ADAM_B2 * v + (1.0 - ADAM_B2) * (g * g)
    m_hat = m / (1.0 - ADAM_B1 ** ADAM_STEP)
    v_hat = v / (1.0 - ADAM_B2 ** ADAM_STEP)
    delta = -ADAM_LR * (m_hat / (jnp.sqrt(v_hat) + ADAM_EPS) + ADAM_WD * w)
    return delta, m, v


def _adamw(w, g, m, v):
    R = w.shape[0]
    tr = _pick8(R, 2048)

    def body(w_ref, g_ref, m_ref, v_ref, d_o, m_o, v_o):
        d, mm, vv = _adamw_math(w_ref[...], g_ref[...], m_ref[...], v_ref[...])
        d_o[...] = d
        m_o[...] = mm
        v_o[...] = vv

    return pl.pallas_call(
        body, name="adamw", grid=(R // tr,), in_specs=[_row(tr, LANES)] * 4, out_specs=[_row(tr, LANES)] * 3,
        out_shape=[_sds((R, LANES), F32)] * 3, compiler_params=_cparams(1),
    )(w, g, m, v)


def _pick8(n, target):
    best = None
    for t in range(8, min(n, target) + 1, 8):
        if n % t == 0:
            best = t
    return best if best is not None else n


def _adamw_small(w, gall, m, v):
    R = w.shape[0]

    def body(w_ref, g_ref, m_ref, v_ref, g_o, d_o, m_o, v_o):
        g = g_ref[0]
        for k in range(1, 8):
            g = g + g_ref[k]
        d, mm, vv = _adamw_math(w_ref[...], g, m_ref[...], v_ref[...])
        g_o[...] = g
        d_o[...] = d
        m_o[...] = mm
        v_o[...] = vv

    vm = pl.BlockSpec(memory_space=pltpu.VMEM)
    return pl.pallas_call(
        body, name="adamw_small", in_specs=[vm] * 4, out_specs=[vm] * 4, out_shape=[_sds((R, LANES), F32)] * 4,
    )(w, gall, m, v)


def _sum_pair(a, b, out_dtype, name):
    n, R, _ = a.shape
    tr = _pick8(R, 1024)
    spec = pl.BlockSpec((1, tr, LANES), lambda k, i: (k, i, 0))

    def body(a_ref, b_ref, o_ref):
        o_ref[...] = (a_ref[...] + b_ref[...]).astype(o_ref.dtype)

    return pl.pallas_call(
        body, name=name, grid=(n, R // tr), in_specs=[spec] * 2, out_specs=spec, out_shape=_sds(a.shape, out_dtype),
        compiler_params=_cparams(2),
    )(a, b)


def _sum_chips(t):
    _, R, _ = t.shape
    tr = _pick8(R, 1024)

    def body(t_ref, o_ref):
        acc = t_ref[0].astype(F32)
        for k in range(1, 4):
            acc = acc + t_ref[k].astype(F32)
        o_ref[...] = acc

    return pl.pallas_call(
        body, name="sum_chips", grid=(R // tr,), in_specs=[pl.BlockSpec((4, tr, LANES), lambda i: (0, i, 0))],
        out_specs=_row(tr, LANES), out_shape=_sds((R, LANES), F32), compiler_params=_cparams(1),
    )(t)


_HBM = pl.BlockSpec(memory_space=pltpu.HBM)


def _place():
    x, y, c = lax.axis_index("x"), lax.axis_index("y"), lax.axis_index("c")
    chips = [(1 - x, y), (x, 1 - y), (1 - x, 1 - y)]
    return x, y, c, chips


def _remote(src, dst, ssem, rsem, to):
    return pltpu.make_async_remote_copy(src_ref=src, dst_ref=dst, send_sem=ssem, recv_sem=rsem, device_id=to,
                                        device_id_type=MESH_ID)


def _gather_weights(p):
    _, Rh, _ = p.shape

    def body(p_ref, g_ref, ssem, rsem, lsem):
        x, y, c, chips = _place()
        me = 2 * x + y
        sib = (x, y, 1 - c)
        own = pltpu.make_async_copy(p_ref, g_ref.at[me], lsem)
        own.start()
        first = [_remote(p_ref.at[c], g_ref.at[me, c], ssem.at[j], rsem.at[j], (cx, cy, c))
                 for j, (cx, cy) in enumerate(chips)]
        for cp in first:
            cp.start()
        passed = []
        for j, (cx, cy) in enumerate(chips):
            blk = g_ref.at[2 * cx + cy, c]
            _remote(blk, blk, ssem.at[j], rsem.at[j], sib).wait_recv()
            fw = _remote(blk, blk, ssem.at[3 + j], rsem.at[3 + j], sib)
            fw.start()
            passed.append(fw)
        for j, (cx, cy) in enumerate(chips):
            blk = g_ref.at[2 * cx + cy, 1 - c]
            _remote(blk, blk, ssem.at[3 + j], rsem.at[3 + j], sib).wait_recv()
        for cp in first + passed:
            cp.wait_send()
        own.wait()

    return pl.pallas_call(
        body, name="gather_weights", in_specs=[_HBM], out_specs=_HBM, out_shape=_sds((4, 2, Rh, LANES), p.dtype),
        scratch_shapes=[pltpu.SemaphoreType.DMA((6,)), pltpu.SemaphoreType.DMA((6,)), pltpu.SemaphoreType.DMA],
    )(p)


def _sibling_halves(g):
    _, n, Rh, _ = g.shape

    def body(g_ref, t_ref, ssem, rsem):
        x, y, c, _ = _place()
        cp = _remote(g_ref.at[1 - c], t_ref, ssem, rsem, (x, y, 1 - c))
        cp.start()
        cp.wait()

    return pl.pallas_call(
        body, name="rs_sibling_halves", in_specs=[_HBM], out_specs=_HBM, out_shape=_sds((n, Rh, LANES), g.dtype),
        scratch_shapes=[pltpu.SemaphoreType.DMA, pltpu.SemaphoreType.DMA],
    )(g)


def _chip_all_to_all(a):
    _, Rh, _ = a.shape

    def body(a_ref, t_ref, ssem, rsem, lsem):
        x, y, c, chips = _place()
        me = 2 * x + y
        own = pltpu.make_async_copy(a_ref.at[me], t_ref.at[me], lsem)
        own.start()
        sends = [_remote(a_ref.at[2 * cx + cy], t_ref.at[me], ssem.at[j], rsem.at[j], (cx, cy, c))
                 for j, (cx, cy) in enumerate(chips)]
        for cp in sends:
            cp.start()
        for j, (cx, cy) in enumerate(chips):
            blk = t_ref.at[2 * cx + cy]
            _remote(blk, blk, ssem.at[j], rsem.at[j], (cx, cy, c)).wait_recv()
        for cp in sends:
            cp.wait_send()
        own.wait()

    return pl.pallas_call(
        body, name="rs_chip_all_to_all", in_specs=[_HBM], out_specs=_HBM, out_shape=_sds(a.shape, a.dtype),
        scratch_shapes=[pltpu.SemaphoreType.DMA((3,)), pltpu.SemaphoreType.DMA((3,)), pltpu.SemaphoreType.DMA],
    )(a)


def _sibling_gather(q):
    Rh, _ = q.shape

    def body(q_ref, o_ref, ssem, rsem, lsem):
        x, y, c, _ = _place()
        own = pltpu.make_async_copy(q_ref, o_ref.at[c], lsem)
        own.start()
        cp = _remote(q_ref, o_ref.at[c], ssem, rsem, (x, y, 1 - c))
        cp.start()
        other = o_ref.at[1 - c]
        _remote(other, other, ssem, rsem, (x, y, 1 - c)).wait_recv()
        cp.wait_send()
        own.wait()

    return pl.pallas_call(
        body, name="rs_sibling_gather", in_specs=[_HBM], out_specs=_HBM, out_shape=_sds((2, Rh, LANES), q.dtype),
        scratch_shapes=[pltpu.SemaphoreType.DMA, pltpu.SemaphoreType.DMA, pltpu.SemaphoreType.DMA],
    )(q)


def _gather_small(s):
    R, _ = s.shape

    def body(s_ref, o_ref, ssem, rsem, lsem):
        x, y, c, _ = _place()
        me = 4 * x + 2 * y + c
        own = pltpu.make_async_copy(s_ref, o_ref.at[me], lsem)
        own.start()
        sends = []
        for k in range(1, 8):
            px, py, pc = x ^ (k >> 2), y ^ ((k >> 1) & 1), c ^ (k & 1)
            cp = _remote(s_ref, o_ref.at[me], ssem.at[k - 1], rsem.at[k - 1], (px, py, pc))
            cp.start()
            sends.append(cp)
        for k in range(1, 8):
            px, py, pc = x ^ (k >> 2), y ^ ((k >> 1) & 1), c ^ (k & 1)
            blk = o_ref.at[4 * px + 2 * py + pc]
            _remote(blk, blk, ssem.at[k - 1], rsem.at[k - 1], (px, py, pc)).wait_recv()
        for cp in sends:
            cp.wait_send()
        own.wait()

    vm = pl.BlockSpec(memory_space=pltpu.VMEM)
    return pl.pallas_call(
        body, name="gather_small", in_specs=[vm], out_specs=vm, out_shape=_sds((8, R, LANES), s.dtype),
        scratch_shapes=[pltpu.SemaphoreType.DMA((7,)), pltpu.SemaphoreType.DMA((7,)), pltpu.SemaphoreType.DMA],
    )(s)


_COL_SHARDED = ("w_in", "mla_w_uq", "mla_w_ukv", "ffn_w_gate", "ffn_w_up")
_ROW_SHARDED = ("w_out", "ffn_w_down")
_BIG = ("w_in", "mla_w_uq", "mla_w_ukv", "w_out", "ffn_w_gate", "ffn_w_up", "ffn_w_down")
_SMALL = ("mla_q_norm", "mla_kv_norm", "gqa_q_norm", "gqa_k_norm", "rel_bias", "ln1_g", "ln1_b", "ln2_g", "ln2_b")
_PACK_ALIGN = 32 * LANES


def _pack_flat(arrs, align):
    flat = jnp.concatenate([a.reshape(-1) for a in arrs])
    pad = (-flat.shape[0]) % align
    return jnp.pad(flat, (0, pad)) if pad else flat


def _unpack_flat(flat, shapes):
    out, off = [], 0
    for s in shapes:
        n = int(np.prod(s))
        out.append(flat[off:off + n].reshape(s))
        off += n
    return out


def _shards_to_full(name, sh):
    if name in _COL_SHARDED:
        _, l, r, cs = sh.shape
        return sh.transpose(1, 2, 0, 3).reshape(l, r, 4 * cs)
    _, l, rs, cdim = sh.shape
    return sh.transpose(1, 0, 2, 3).reshape(l, 4 * rs, cdim)


def _full_to_shards(name, w):
    if name in _COL_SHARDED:
        l, r, cdim = w.shape
        return w.reshape(l, r, 4, cdim // 4).transpose(2, 0, 1, 3)
    l, rdim, cdim = w.shape
    return w.reshape(l, 4, rdim // 4, cdim).transpose(1, 0, 2, 3)


def _local_step(x, target, W, small):
    S, D = x.shape
    depth = W["w_in"].shape[0]
    alpha = (2.0 * depth) ** 0.25
    in_idx, uq_idx, ukv_idx, out_idx = _in_cols(), _uq_cols(), _ukv_cols(), _out_rows()
    win = _take_cols(W["w_in"], in_idx)
    wuq = _take_cols(W["mla_w_uq"], uq_idx)
    wukv = _take_cols(W["mla_w_ukv"], ukv_idx)
    wout = jnp.take(W["w_out"], jnp.asarray(out_idx, jnp.int32), axis=1)
    wgu = jnp.concatenate([W["ffn_w_gate"], W["ffn_w_up"]], axis=2)
    wdn = W["ffn_w_down"]
    Fh = wdn.shape[1]

    tm, tg = _rope_tables(S)
    j256, j384 = _head_ones(256), _head_ones(384)
    mla_scale = (64 + MLA_ROPE_DIM) ** -0.5
    branches = []
    for (_, dil) in DIL_BRANCHES:
        L = S // dil
        tq = min(256, L)
        idx = jnp.asarray(_branch_bucket_idx(tq, dil))
        branches.append((dil, L, tq, idx))
    tabs = [_bias_expand(idx, small["rel_bias"], name=f"bias_expand_{b}") for b, (_, _, _, idx) in enumerate(branches)]

    def view(a, dil):
        return a.reshape(a.shape[0] // dil, dil * a.shape[1])

    def padded_view(a, dil):
        z = jnp.zeros((DIL_HALF * dil, a.shape[1]), a.dtype)
        return view(jnp.concatenate([z, a, z], axis=0), dil)

    def unpad_view(a, dil):
        return a[DIL_HALF:a.shape[0] - DIL_HALF].reshape(S, 384)

    saved = []
    xf, xb = x, x.astype(CDT)
    for l in range(depth):
        gq, gkv = small["mla_q_norm"][l][None], small["mla_kv_norm"][l][None]
        ggq = jnp.tile(small["gqa_q_norm"][l], 4)[None]
        ggk = jnp.tile(small["gqa_k_norm"][l], 2)[None]
        h = _mm(xb, win[l], name="mm_in")
        cq, ckv, kr, qd, kd, vd, qg, kg, vg = _prep_fwd(h, gq, gkv, ggq, ggk, tm, tg, j256)
        qa = _mm(cq, wuq[l], name="mm_uq")
        kvp = _mm(ckv, wukv[l], out_dtype=CDT, name="mm_ukv")
        qm, km = _mla_prep_fwd(qa, kvp, kr, tm, mla_scale)
        oa, lsa = _attn_fwd(qm, km, kvp, split=True, npairs=3, kblk=lambda p: p, vblk=lambda p: 6 + p,
                            name="mla_attn_fwd")
        oc, lsc = _attn_fwd(qg, kg, vg, split=False, npairs=2, kblk=lambda p: 0, vblk=lambda p: 0,
                            name="gqa_attn_fwd")
        obs, lbs, kvs = [], [], []
        for b, (dil, L, tq, _) in enumerate(branches):
            kpv, vpv = padded_view(kd, dil), padded_view(vd, dil)
            o_b, l_b = _dil_fwd(view(qd, dil), kpv, vpv, tabs[b], dil=dil, L=L, tq=tq, name=f"dil_fwd_{b}")
            obs.append(o_b.reshape(S, 384))
            lbs.append(l_b.reshape(S, 384))
            kvs.append((kpv, vpv))
        ob = _dil_mix_fwd(obs, lbs)
        cat = jnp.concatenate([oa, ob, oc], axis=1)
        mix = _mm(cat, wout[l], name="mm_out")
        x1, x1b, z1 = _ln_fwd(xf, mix, small["ln1_g"][l][None], small["ln1_b"][l][None], alpha, name="ln1_fwd")
        gu = _mm(x1b, wgu[l], name="mm_gu")
        act = _swiglu_fwd(gu)
        ff = _mm(act, wdn[l], name="mm_down")
        x2, x2b, z2 = _ln_fwd(x1, ff, small["ln2_g"][l][None], small["ln2_b"][l][None], alpha, name="ln2_fwd")
        saved.append(dict(xb=xb, h=h, cq=cq, ckv=ckv, qd=qd, qg=qg, kg=kg, vg=vg, kvp=kvp, qm=qm, km=km, oa=oa, lsa=lsa,
                          oc=oc, lsc=lsc, obs=obs, lbs=lbs, kvs=kvs, cat=cat, z1=z1, x1b=x1b, gu=gu, act=act, z2=z2,
                          gq=gq, gkv=gkv, ggq=ggq, ggk=ggk))
        xf, xb = x2, x2b

    dy, loss = _loss_kernel(xf, target)

    gW = {k: [None] * depth for k in ("win", "wuq", "wukv", "wout", "wgu", "wdn")}
    gS = {k: [None] * depth for k in ("mla_q_norm", "mla_kv_norm", "gqa_q_norm", "gqa_k_norm", "ln1_g", "ln1_b", "ln2_g",
                                      "ln2_b")}
    g_rel = None
    dya, dyb = dy, None
    for l in reversed(range(depth)):
        sv = saved[l]
        dz2, dz2b, gS["ln2_g"][l], gS["ln2_b"][l] = _ln_bwd(dya, dyb, sv["z2"], small["ln2_g"][l][None], alpha,
                                                             name="ln2_bwd" if dyb is not None else "ln2_bwd_last")
        gW["wdn"][l] = _mm(sv["act"], dz2b, ta=True, name="mm_down_dw")
        da = _mm(dz2b, wdn[l], tb=True, name="mm_down_dx")
        dgu = _swiglu_bwd(da, sv["gu"])
        gW["wgu"][l] = _mm(sv["x1b"], dgu, ta=True, name="mm_gu_dw")
        dx1 = _mm(dgu, wgu[l], tb=True, name="mm_gu_dx")
        dz1, dz1b, gS["ln1_g"][l], gS["ln1_b"][l] = _ln_bwd(dx1, dz2, sv["z1"], small["ln1_g"][l][None], alpha,
                                                             name="ln1_bwd")
        gW["wout"][l] = _mm(sv["cat"], dz1b, ta=True, name="mm_out_dw")
        dcat = _mm(dz1b, wout[l], tb=True, name="mm_out_dx")
        dqg, dkg, dvg = _attn_bwd(sv["qg"], sv["kg"], sv["vg"], dcat, sv["oc"], sv["lsc"], split=False, npairs=2,
                                  kblk=lambda p: 0, vblk=lambda p: 0, doblk=lambda p: 6 + p, shared_kv=True,
                                  name="gqa_attn_bwd")
        dqm, dkm, dvm = _attn_bwd(sv["qm"], sv["km"], sv["kvp"], dcat, sv["oa"], sv["lsa"], split=True, npairs=3,
                                  kblk=lambda p: p, vblk=lambda p: 6 + p, doblk=lambda p: p, shared_kv=False,
                                  name="mla_attn_bwd")
        dqa, dkvp, dkr = _mla_prep_bwd(dqm, dkm, dvm, tm, mla_scale)
        gW["wuq"][l] = _mm(sv["cq"], dqa, ta=True, name="mm_uq_dw")
        dcq = _mm(dqa, wuq[l], tb=True, name="mm_uq_dx")
        gW["wukv"][l] = _mm(sv["ckv"], dkvp, ta=True, name="mm_ukv_dw")
        dckv = _mm(dkvp, wukv[l], tb=True, name="mm_ukv_dx")
        mixb = _dil_mix_bwd(dcat, sv["obs"], sv["lbs"], j384)
        ddq, ddk, ddv = [], [], []
        for b, (dil, L, tq, idx) in enumerate(branches):
            kpv, vpv = sv["kvs"][b]
            dq_b, dk_b, dv_b, dtab = _dil_bwd(view(sv["qd"], dil), kpv, vpv, tabs[b], view(mixb[b], dil),
                                              view(sv["lbs"][b], dil), view(mixb[3 + b], dil), dil=dil, L=L, tq=tq,
                                              name=f"dil_bwd_{b}")
            ddq.append(dq_b.reshape(S, 384))
            ddk.append(unpad_view(dk_b, dil))
            ddv.append(unpad_view(dv_b, dil))
            g_b = _bias_reduce(idx, dtab, name=f"bias_reduce_{b}")[:, :, 0].T
            g_rel = g_b if g_rel is None else g_rel + g_b
        dh, n1, n2, n3, n4 = _prep_bwd(sv["h"], dcq, dckv, dkr, ddq, ddk, ddv, dqg, dkg, dvg, sv["gq"], sv["gkv"],
                                       sv["ggq"], sv["ggk"], tg, j256)
        gS["mla_q_norm"][l], gS["mla_kv_norm"][l] = n1[0], n2[0]
        gS["gqa_q_norm"][l] = n3[0].reshape(4, 64).sum(0)
        gS["gqa_k_norm"][l] = n4[0].reshape(2, 64).sum(0)
        gW["win"][l] = _mm(sv["xb"], dh, ta=True, name="mm_in_dw")
        dya = _mm(dh, win[l], tb=True, name="mm_in_dx")
        dyb = dz1
    grad_x = _axpy(dya, dyb, alpha, name="grad_x")

    st = {k: jnp.stack(v) for k, v in gW.items()}
    grads = {
        "w_in": _untake_cols(st["win"], in_idx, IN_W),
        "mla_w_uq": _untake_cols(st["wuq"], uq_idx, MLA_HEADS * 96),
        "mla_w_ukv": _untake_cols(st["wukv"], ukv_idx, MLA_HEADS * 128),
        "w_out": jnp.take(st["wout"], jnp.asarray(np.argsort(out_idx), jnp.int32), axis=1),
        "ffn_w_gate": st["wgu"][:, :, :Fh],
        "ffn_w_up": st["wgu"][:, :, Fh:],
        "ffn_w_down": st["wdn"],
    }
    gsmall = {k: jnp.stack([a.reshape(-1) for a in v]) for k, v in gS.items()}
    gsmall["rel_bias"] = g_rel
    return loss, grad_x, grads, gsmall


_ORDER = ("w_in", "mla_q_norm", "mla_kv_norm", "mla_w_uq", "mla_w_ukv", "gqa_q_norm", "gqa_k_norm", "rel_bias", "w_out",
          "ln1_g", "ln1_b", "ffn_w_gate", "ffn_w_up", "ffn_w_down", "ln2_g", "ln2_b")


def kernel(x, w_in, mla_q_norm, mla_kv_norm, mla_w_uq, mla_w_ukv, gqa_q_norm, gqa_k_norm, rel_bias, w_out, ln1_g, ln1_b, ffn_w_gate, ffn_w_up, ffn_w_down, ln2_g, ln2_b, loss_target, m_w_in, m_mla_q_norm, m_mla_kv_norm, m_mla_w_uq, m_mla_w_ukv, m_gqa_q_norm, m_gqa_k_norm, m_rel_bias, m_w_out, m_ln1_g, m_ln1_b, m_ffn_w_gate, m_ffn_w_up, m_ffn_w_down, m_ln2_g, m_ln2_b, v_w_in, v_mla_q_norm, v_mla_kv_norm, v_mla_w_uq, v_mla_w_ukv, v_gqa_q_norm, v_gqa_k_norm, v_rel_bias, v_w_out, v_ln1_g, v_ln1_b, v_ffn_w_gate, v_ffn_w_up, v_ffn_w_down, v_ln2_g, v_ln2_b):
    wts = dict(zip(_ORDER, (w_in, mla_q_norm, mla_kv_norm, mla_w_uq, mla_w_ukv, gqa_q_norm, gqa_k_norm, rel_bias, w_out,
                            ln1_g, ln1_b, ffn_w_gate, ffn_w_up, ffn_w_down, ln2_g, ln2_b)))
    mom = dict(zip(_ORDER, (m_w_in, m_mla_q_norm, m_mla_kv_norm, m_mla_w_uq, m_mla_w_ukv, m_gqa_q_norm, m_gqa_k_norm,
                            m_rel_bias, m_w_out, m_ln1_g, m_ln1_b, m_ffn_w_gate, m_ffn_w_up, m_ffn_w_down, m_ln2_g,
                            m_ln2_b)))
    var = dict(zip(_ORDER, (v_w_in, v_mla_q_norm, v_mla_kv_norm, v_mla_w_uq, v_mla_w_ukv, v_gqa_q_norm, v_gqa_k_norm,
                            v_rel_bias, v_w_out, v_ln1_g, v_ln1_b, v_ffn_w_gate, v_ffn_w_up, v_ffn_w_down, v_ln2_g,
                            v_ln2_b)))
    big_shapes = [wts[n].shape for n in _BIG]
    small_shapes = [wts[n].shape for n in _SMALL]

    wflat = _pack_flat([wts[n].astype(WIRE) for n in _BIG], _PACK_ALIGN)
    npk = wflat.shape[0]
    rh = npk // (2 * LANES)
    gathered = _gather_weights(wflat.reshape(2, rh, LANES)).reshape(4, npk)
    full = {}
    off = 0
    for n, s in zip(_BIG, big_shapes):
        cnt = int(np.prod(s))
        full[n] = _shards_to_full(n, gathered[:, off:off + cnt].reshape((4,) + tuple(s))).astype(CDT)
        off += cnt

    small = {n: wts[n] for n in _SMALL}
    loss, grad_x, gbig, gsmall = _local_step(x[0], loss_target[0], full, small)

    gpk = jnp.concatenate([_full_to_shards(n, gbig[n]).reshape(4, -1) for n in _BIG], axis=1)
    gpk = jnp.pad(gpk, ((0, 0), (0, npk - gpk.shape[1])))
    gpk = gpk.reshape(4, 2, rh, LANES).transpose(1, 0, 2, 3)
    c = lax.axis_index("c")
    mine = lax.dynamic_index_in_dim(gpk, c, axis=0, keepdims=False)
    pair = _sum_pair(mine, _sibling_halves(gpk), WIRE, name="rs_pair_sum")
    red = _sum_chips(_chip_all_to_all(pair))
    gshard = _sibling_gather(red).reshape(npk)

    sflat = _pack_flat([gsmall[n].reshape(-1) for n in _SMALL], 8 * LANES)
    rs = sflat.shape[0] // LANES
    sall = _gather_small(sflat.reshape(rs, LANES))

    def packed(d, names, align):
        return _pack_flat([d[n] for n in names], align)

    r = npk // LANES
    dlt, nm, nv = _adamw(packed(wts, _BIG, _PACK_ALIGN).reshape(r, LANES), gshard.reshape(r, LANES),
                         packed(mom, _BIG, _PACK_ALIGN).reshape(r, LANES), packed(var, _BIG, _PACK_ALIGN).reshape(r, LANES))
    sg, sd, sm, sv = _adamw_small(packed(wts, _SMALL, 8 * LANES).reshape(rs, LANES), sall,
                                  packed(mom, _SMALL, 8 * LANES).reshape(rs, LANES),
                                  packed(var, _SMALL, 8 * LANES).reshape(rs, LANES))
    outs = {}
    for tag, bigflat, smallflat in (("grad", gshard, sg), ("delta", dlt, sd), ("new_m", nm, sm), ("new_v", nv, sv)):
        d = dict(zip(_BIG, _unpack_flat(bigflat.reshape(-1), big_shapes)))
        d.update(zip(_SMALL, _unpack_flat(smallflat.reshape(-1), small_shapes)))
        outs[tag] = d

    total = lax.psum(loss[0, 0], ("x", "y", "c"))
    return (total, grad_x[None], *[outs["grad"][n] for n in _ORDER], *[outs["delta"][n] for n in _ORDER],
            *[outs["new_m"][n] for n in _ORDER], *[outs["new_v"][n] for n in _ORDER])
```

```python
import functools
import math

import numpy as np
import jax
import jax.numpy as jnp
from jax import lax
from jax.experimental import pallas as pl
from jax.experimental.pallas import tpu as pltpu

F32 = jnp.float32
CDT = jnp.bfloat16
WIRE = jnp.bfloat16

HEAD_DIM = 64
GRID_W = 64
ROPE_THETA = 10000.0
MLA_HEADS = 6
MLA_Q_RANK = 256
MLA_KV_RANK = 128
MLA_ROPE_DIM = 32
DIL_HEADS = 6
DIL_BRANCHES = ((128, 1), (512, 4), (2048, 16))
DIL_HALF = 64
GQA_Q_HEADS = 4
REL_BUCKETS = 32
REL_MAX_DIST = 1024
NEG_INF = -1e30
LANES = 128
VMEM_LIMIT = 56 * 1024 * 1024

ADAM_LR, ADAM_B1, ADAM_B2, ADAM_EPS, ADAM_WD, ADAM_STEP = 0.001, 0.9, 0.999, 1e-08, 0.01, 10

C_CQ, C_CKV, C_KR, C_DQ, C_DK, C_DV, C_GQ, C_GK, C_GV, IN_P = 0, 256, 384, 512, 896, 1280, 1664, 1920, 2048, 2176
IN_W = 2080
MESH_ID = pl.DeviceIdType.MESH


def _cparams(n_axes, vmem=VMEM_LIMIT):
    return pltpu.CompilerParams(dimension_semantics=("arbitrary",) * n_axes, vmem_limit_bytes=vmem)


def _pick(n, target):
    best = None
    for t in range(LANES, min(n, target) + 1, LANES):
        if n % t == 0:
            best = t
    return best if best is not None else n


def _sds(shape, dtype):
    return jax.ShapeDtypeStruct(tuple(shape), dtype)


def _in_cols():
    idx = -np.ones((IN_P,), np.int64)
    idx[C_CQ:C_CQ + 256] = np.arange(0, 256)
    idx[C_CKV:C_CKV + 128] = np.arange(256, 384)
    idx[C_KR + 64:C_KR + 96] = np.arange(384, 416)
    idx[C_DQ:C_DQ + 1152] = np.arange(416, 1568)
    gq = 1568 + (np.array([0, 2, 1, 3])[:, None] * 64 + np.arange(64)[None, :]).reshape(-1)
    idx[C_GQ:C_GQ + 256] = gq
    idx[C_GK:C_GK + 256] = np.arange(1824, 2080)
    return idx


def _uq_cols():
    idx = -np.ones((MLA_HEADS * 128,), np.int64)
    for h in range(MLA_HEADS):
        idx[h * 128:h * 128 + 96] = np.arange(96 * h, 96 * h + 96)
    return idx


def _ukv_cols():
    idx = -np.ones((MLA_HEADS * 128 + MLA_HEADS * 64,), np.int64)
    for h in range(MLA_HEADS):
        idx[h * 128:h * 128 + 64] = np.arange(128 * h, 128 * h + 64)
        idx[768 + h * 64:768 + h * 64 + 64] = np.arange(128 * h + 64, 128 * h + 128)
    return idx


def _out_rows():
    idx = np.arange(1024)
    idx[768:1024] = 768 + (np.array([0, 2, 1, 3])[:, None] * 64 + np.arange(64)[None, :]).reshape(-1)
    return idx


def _runs(idx):
    out, i = [], 0
    while i < len(idx):
        j = i + 1
        while j < len(idx) and ((idx[i] < 0 and idx[j] < 0) or (idx[i] >= 0 and idx[j] == idx[j - 1] + 1)):
            j += 1
        out.append((int(idx[i]), j - i))
        i = j
    return out


def _cols_from_shards(sh, idx):
    _, r, cs = sh.shape
    pieces = []
    for first, ln in _runs(idx):
        if first < 0:
            pieces.append(jnp.zeros((r, ln), sh.dtype))
            continue
        while ln > 0:
            k, off = divmod(first, cs)
            take = min(ln, cs - off)
            pieces.append(sh[k, :, off:off + take])
            first, ln = first + take, ln - take
    return jnp.concatenate(pieces, axis=1)


def _cols_to_shards(wp, idx, n):
    inv = np.zeros((n,), np.int64)
    pos = np.nonzero(idx >= 0)[0]
    inv[idx[pos]] = pos
    cs = n // 4
    shards = []
    for k in range(4):
        pieces = [wp[:, first:first + ln] for first, ln in _runs(inv[k * cs:(k + 1) * cs])]
        shards.append(jnp.concatenate(pieces, axis=1))
    return jnp.stack(shards)


def _t5_bucket_np(rel):
    nb = REL_BUCKETS // 2
    exact = nb // 2
    ret = np.where(rel > 0, nb, 0)
    n = np.abs(rel)
    nf = np.maximum(n, 1).astype(np.float32)
    large = exact + (np.log(nf / np.float32(exact)) / np.float32(math.log(REL_MAX_DIST / exact))
                     * np.float32(nb - exact)).astype(np.int32)
    large = np.minimum(large, nb - 1)
    return ret + np.where(n < exact, n, large)


def _branch_bucket_idx(tq, dil):
    kw = tq + 2 * DIL_HALF
    rel = np.arange(kw)[None, :] - DIL_HALF - np.arange(tq)[:, None]
    idx = _t5_bucket_np(rel * dil)
    return np.where(np.abs(rel) <= DIL_HALF, idx, -1).astype(np.int32)


def _rope_tables(S):
    inv = ROPE_THETA ** (-jnp.arange(0, 32, 2, dtype=F32) / 32)
    t = jnp.arange(S)
    pos = t.astype(F32)
    row = (t // GRID_W).astype(F32)
    col = (t % GRID_W).astype(F32)
    lane = np.arange(LANES)
    wm = lane - 64
    is_rope = (lane >= 64) & (lane < 96)
    ang = pos[:, None] * inv[np.where(is_rope, wm % 16, 0)][None, :]
    cm = jnp.where(is_rope[None], jnp.cos(ang), 1.0)
    smm = jnp.where((is_rope & (wm < 16))[None], -jnp.sin(ang), 0.0)
    spm = jnp.where((is_rope & (wm >= 16))[None], jnp.sin(ang), 0.0)
    g = lane % 64
    w = g % 32
    angg = jnp.where((g < 32)[None], row[:, None], col[:, None]) * inv[w % 16][None, :]
    cg = jnp.cos(angg)
    smg = jnp.where((w < 16)[None], -jnp.sin(angg), 0.0)
    spg = jnp.where((w >= 16)[None], jnp.sin(angg), 0.0)
    return (cm, smm, spm), (cg, smg, spg)


def _lanes(t, width):
    return t if width == LANES else jnp.concatenate([t] * (width // LANES), axis=1)


def _rope(x, tabs):
    c, sm, sp = (_lanes(t, x.shape[1]) for t in tabs)
    w = x.shape[1]
    return x * c + pltpu.roll(x, w - 16, 1) * sm + pltpu.roll(x, 16, 1) * sp


def _rope_t(dy, tabs):
    c, sm, sp = (_lanes(t, dy.shape[1]) for t in tabs)
    w = dy.shape[1]
    return dy * c + pltpu.roll(dy * sm, 16, 1) + pltpu.roll(dy * sp, w - 16, 1)


def _head_ones(width):
    i = np.arange(width)
    return jnp.asarray((i[:, None] // HEAD_DIM == i[None, :] // HEAD_DIM).astype(np.float32))


def _headsum(x, j):
    return jnp.dot(x, j, preferred_element_type=F32, precision=lax.Precision.HIGHEST)


def _mm(a, b, *, ta=False, tb=False, ga=False, gb=False, go=False, out_dtype=F32, add=None, name):
    G = a.shape[0] if ga else (b.shape[0] if gb else 1)
    a2 = a.shape[1:] if ga else a.shape
    b2 = b.shape[1:] if gb else b.shape
    K, M = a2 if ta else a2[::-1]
    N = b2[0] if tb else b2[1]
    assert (b2[1] if tb else b2[0]) == K
    tm, tn, tk = _pick(M, 512), _pick(N, 1024), _pick(K, 1024)
    if tn == N and N > 2304:
        tn = _pick(N, 1536)
    nk = K // tk
    steps = nk if (go or G == 1) else G * nk
    dn = (((0 if ta else 1,), (1 if tb else 0,)), ((), ()))

    def body(a_ref, b_ref, *rest):
        rest = list(rest)
        add_ref = rest.pop(0) if add is not None else None
        o_ref = rest.pop(0)
        part = lax.dot_general(a_ref[...], b_ref[...], dn, preferred_element_type=F32)
        if steps == 1:
            if add_ref is not None:
                part = part + add_ref[...]
            o_ref[...] = part.astype(o_ref.dtype)
            return
        acc_ref, = rest
        s = pl.program_id(3)

        @pl.when(s == 0)
        def _():
            acc_ref[...] = part if add_ref is None else part + add_ref[...]

        @pl.when(s > 0)
        def _():
            acc_ref[...] += part

        @pl.when(s == steps - 1)
        def _():
            o_ref[...] = acc_ref[...].astype(o_ref.dtype)

    def grp(g, s):
        return g if go else s // nk

    def kk(s):
        return s if steps == nk else s % nk

    def spec(grouped, block, index):
        if grouped:
            return pl.BlockSpec((None,) + block, lambda g, i, j, s: (grp(g, s),) + index(i, j, s))
        return pl.BlockSpec(block, lambda g, i, j, s: index(i, j, s))

    a_spec = (spec(ga, (tk, tm), lambda i, j, s: (kk(s), i)) if ta else spec(ga, (tm, tk), lambda i, j, s: (i, kk(s))))
    b_spec = (spec(gb, (tn, tk), lambda i, j, s: (j, kk(s))) if tb else spec(gb, (tk, tn), lambda i, j, s: (kk(s), j)))
    o_spec = spec(go, (tm, tn), lambda i, j, s: (i, j))
    return pl.pallas_call(
        body, name=name, grid=(G if go else 1, M // tm, N // tn, steps),
        in_specs=[a_spec, b_spec] + ([o_spec] if add is not None else []), out_specs=o_spec,
        out_shape=_sds(((G,) if go else ()) + (M, N), out_dtype),
        scratch_shapes=[pltpu.VMEM((tm, tn), F32)] if steps > 1 else [],
        compiler_params=_cparams(4),
    )(*([a, b] + ([add] if add is not None else [])))


def _row(ts, w, cb=0):
    return pl.BlockSpec((ts, w), lambda i: (i, cb))


def _full(shape):
    nd = len(shape)
    return pl.BlockSpec(tuple(shape), lambda i: (0,) * nd)


def _rms_fwd(x, g, eps=1e-6):
    r = lax.rsqrt(jnp.mean(x * x, axis=-1, keepdims=True) + eps)
    return x * r * g


def _rms_bwd(x, g, dy, eps=1e-6):
    r = lax.rsqrt(jnp.mean(x * x, axis=-1, keepdims=True) + eps)
    gdy = g * dy
    dx = r * gdy - x * (r * r * r) * jnp.mean(x * gdy, axis=-1, keepdims=True)
    return dx, x * r * dy


def _rms_head_fwd(x, g, j, eps=1e-6):
    r = lax.rsqrt(_headsum(x * x, j) * (1.0 / HEAD_DIM) + eps)
    return x * r * g


def _rms_head_bwd(x, g, dy, j, eps=1e-6):
    r = lax.rsqrt(_headsum(x * x, j) * (1.0 / HEAD_DIM) + eps)
    gdy = g * dy
    dx = r * gdy - x * (r * r * r) * (_headsum(x * gdy, j) * (1.0 / HEAD_DIM))
    return dx, x * r * dy


def _prep_fwd(h, gq, gkv, ggq, ggk, tm, tg, j256):
    S = h.shape[0]
    ts = min(256, S)
    scale = HEAD_DIM ** -0.5

    def body(h_ref, gq_ref, gkv_ref, ggq_ref, ggk_ref, cm, smm, spm, cg, smg, spg, j_ref,
             cq_o, ckv_o, kr_o, dq_o, dk_o, dv_o, gq_o, gk_o, gv_o):
        tabm = (cm[...], smm[...], spm[...])
        tabg = (cg[...], smg[...], spg[...])
        cq_o[...] = _rms_fwd(h_ref[:, C_CQ:C_CQ + 256], gq_ref[...]).astype(CDT)
        ckv_o[...] = _rms_fwd(h_ref[:, C_CKV:C_CKV + 128], gkv_ref[...]).astype(CDT)
        kr_o[...] = _rope(h_ref[:, C_KR:C_KR + 128], tabm).astype(CDT)
        dq_o[...] = (h_ref[:, C_DQ:C_DQ + 384] * scale).astype(CDT)
        dk_o[...] = h_ref[:, C_DK:C_DK + 384].astype(CDT)
        dv_o[...] = h_ref[:, C_DV:C_DV + 384].astype(CDT)
        qn = _rms_head_fwd(h_ref[:, C_GQ:C_GQ + 256], ggq_ref[...], j_ref[...])
        gq_o[...] = (_rope(qn, tabg) * scale).astype(CDT)
        kn = _rms_head_fwd(h_ref[:, C_GK:C_GK + 128], ggk_ref[...], j_ref[0:128, 0:128])
        gk_o[...] = _rope(kn, tabg).astype(CDT)
        gv_o[...] = h_ref[:, C_GV:C_GV + 128].astype(CDT)

    widths = (256, 128, 128, 384, 384, 384, 256, 128, 128)
    return pl.pallas_call(
        body, name="prep_fwd", grid=(S // ts,),
        in_specs=[_row(ts, IN_P), _full(gq.shape), _full(gkv.shape), _full(ggq.shape), _full(ggk.shape)]
        + [_row(ts, LANES)] * 6 + [_full(j256.shape)],
        out_specs=[_row(ts, w) for w in widths],
        out_shape=[_sds((S, w), CDT) for w in widths],
        compiler_params=_cparams(1),
    )(h, gq, gkv, ggq, ggk, *tm, *tg, j256)


def _prep_bwd(h, dcq, dckv, dkr, ddq, ddk, ddv, dgq, dgk, dgv, gq, gkv, ggq, ggk, tg, j256):
    S = h.shape[0]
    ts = min(256, S)
    scale = HEAD_DIM ** -0.5

    def body(h_ref, dcq_r, dckv_r, dkr_r, q1, q2, q3, k1, k2, k3, v1, v2, v3, dgq_r, dgk_r, dgv_r,
             gq_ref, gkv_ref, ggq_ref, ggk_ref, cg, smg, spg, j_ref,
             dh_o, ngq_o, ngkv_o, nggq_o, nggk_o):
        tabg = (cg[...], smg[...], spg[...])
        first = pl.program_id(0) == 0

        def acc(o_ref, val):
            s = jnp.sum(val, axis=0, keepdims=True)

            @pl.when(first)
            def _():
                o_ref[...] = s

            @pl.when(jnp.logical_not(first))
            def _():
                o_ref[...] += s

        dx, dg = _rms_bwd(h_ref[:, C_CQ:C_CQ + 256], gq_ref[...], dcq_r[...])
        dh_o[:, C_CQ:C_CQ + 256] = dx.astype(CDT)
        acc(ngq_o, dg)
        dx, dg = _rms_bwd(h_ref[:, C_CKV:C_CKV + 128], gkv_ref[...], dckv_r[...])
        dh_o[:, C_CKV:C_CKV + 128] = dx.astype(CDT)
        acc(ngkv_o, dg)
        dh_o[:, C_KR:C_KR + 128] = dkr_r[...].astype(CDT)
        dh_o[:, C_DQ:C_DQ + 384] = ((q1[...] + q2[...] + q3[...]) * scale).astype(CDT)
        dh_o[:, C_DK:C_DK + 384] = (k1[...] + k2[...] + k3[...]).astype(CDT)
        dh_o[:, C_DV:C_DV + 384] = (v1[...] + v2[...] + v3[...]).astype(CDT)
        dqn = _rope_t(dgq_r[...] * scale, tabg)
        dx, dg = _rms_head_bwd(h_ref[:, C_GQ:C_GQ + 256], ggq_ref[...], dqn, j_ref[...])
        dh_o[:, C_GQ:C_GQ + 256] = dx.astype(CDT)
        acc(nggq_o, dg)
        dkn = _rope_t(dgk_r[...], tabg)
        dx, dg = _rms_head_bwd(h_ref[:, C_GK:C_GK + 128], ggk_ref[...], dkn, j_ref[0:128, 0:128])
        dh_o[:, C_GK:C_GK + 128] = dx.astype(CDT)
        acc(nggk_o, dg)
        dh_o[:, C_GV:C_GV + 128] = dgv_r[...].astype(CDT)

    return pl.pallas_call(
        body, name="prep_bwd", grid=(S // ts,),
        in_specs=[_row(ts, IN_P), _row(ts, 256), _row(ts, 128), _row(ts, 128)] + [_row(ts, 384)] * 9
        + [_row(ts, 256), _row(ts, 128), _row(ts, 128)]
        + [_full(gq.shape), _full(gkv.shape), _full(ggq.shape), _full(ggk.shape)] + [_row(ts, LANES)] * 3
        + [_full(j256.shape)],
        out_specs=[_row(ts, IN_P), _full((1, 256)), _full((1, 128)), _full((1, 256)), _full((1, 128))],
        out_shape=[_sds((S, IN_P), CDT), _sds((1, 256), F32), _sds((1, 128), F32), _sds((1, 256), F32),
                   _sds((1, 128), F32)],
        compiler_params=_cparams(1),
    )(h, dcq, dckv, dkr, *ddq, *ddk, *ddv, dgq, dgk, dgv, gq, gkv, ggq, ggk, *tg, j256)


def _mla_prep_fwd(qa, kvp, kr, tm, scale):
    S = qa.shape[0]
    ts = min(256, S)

    def body(qa_ref, kv_ref, kr_ref, cm, smm, spm, q_o, k_o):
        tabm = (cm[...], smm[...], spm[...])
        q_o[...] = (_rope(qa_ref[...], tabm) * scale).astype(CDT)
        k_o[...] = kv_ref[:, 0:768] + _lanes(kr_ref[...], 768)

    return pl.pallas_call(
        body, name="mla_prep_fwd", grid=(S // ts,),
        in_specs=[_row(ts, 768), _row(ts, 1152), _row(ts, 128)] + [_row(ts, LANES)] * 3,
        out_specs=[_row(ts, 768)] * 2, out_shape=[_sds((S, 768), CDT)] * 2,
        compiler_params=_cparams(1),
    )(qa, kvp, kr, *tm)


def _mla_prep_bwd(dq, dk, dv, tm, scale):
    S = dq.shape[0]
    ts = min(256, S)

    def body(dq_ref, dk_ref, dv_ref, cm, smm, spm, dqa_o, dkv_o, dkr_o):
        tabm = (cm[...], smm[...], spm[...])
        lane = lax.broadcasted_iota(jnp.int32, (1, LANES), 1)
        dqa_o[...] = _rope_t(dq_ref[...] * scale, tabm).astype(CDT)
        dkr = jnp.zeros((ts, LANES), F32)
        for hd in range(MLA_HEADS):
            blk = dk_ref[:, hd * 128:(hd + 1) * 128]
            dkv_o[:, hd * 128:(hd + 1) * 128] = jnp.where(lane < 64, blk, 0.0).astype(CDT)
            dkr = dkr + jnp.where((lane >= 64) & (lane < 96), blk, 0.0)
        dkv_o[:, 768:1152] = dv_ref[...].astype(CDT)
        dkr_o[...] = jnp.where((lane >= 64) & (lane < 96), _rope_t(dkr, tabm), 0.0)

    return pl.pallas_call(
        body, name="mla_prep_bwd", grid=(S // ts,),
        in_specs=[_row(ts, 768), _row(ts, 768), _row(ts, 384)] + [_row(ts, LANES)] * 3,
        out_specs=[_row(ts, 768), _row(ts, 1152), _row(ts, 128)],
        out_shape=[_sds((S, 768), CDT), _sds((S, 1152), CDT), _sds((S, 128), F32)],
        compiler_params=_cparams(1),
    )(dq, dk, dv, *tm)


def _ln_fwd(xa, xb, g, b, alpha, name):
    S, D = xa.shape
    ts = min(256, S)

    def body(xa_ref, xb_ref, g_ref, b_ref, y_o, yb_o, z_o):
        z = alpha * xa_ref[...] + xb_ref[...]
        mu = jnp.mean(z, axis=-1, keepdims=True)
        zc = z - mu
        var = jnp.mean(zc * zc, axis=-1, keepdims=True)
        y = zc * lax.rsqrt(var + 1e-5) * g_ref[...] + b_ref[...]
        y_o[...] = y
        yb_o[...] = y.astype(CDT)
        z_o[...] = z

    return pl.pallas_call(
        body, name=name, grid=(S // ts,),
        in_specs=[_row(ts, D), _row(ts, D), _full(g.shape), _full(b.shape)],
        out_specs=[_row(ts, D)] * 3, out_shape=[_sds((S, D), F32), _sds((S, D), CDT), _sds((S, D), F32)],
        compiler_params=_cparams(1),
    )(xa, xb, g, b)


def _ln_bwd(dya, dyb, z, g, alpha, name):
    S, D = z.shape
    ts = min(256, S)
    two = dyb is not None

    def body(*refs):
        if two:
            dya_ref, dyb_ref, z_ref, g_ref, dz_o, dzb_o, dg_o, db_o = refs
            dy = dya_ref[...] + alpha * dyb_ref[...]
        else:
            dya_ref, z_ref, g_ref, dz_o, dzb_o, dg_o, db_o = refs
            dy = dya_ref[...]
        z = z_ref[...]
        mu = jnp.mean(z, axis=-1, keepdims=True)
        zc = z - mu
        r = lax.rsqrt(jnp.mean(zc * zc, axis=-1, keepdims=True) + 1e-5)
        xh = zc * r
        dxh = dy * g_ref[...]
        dz = r * (dxh - jnp.mean(dxh, axis=-1, keepdims=True) - xh * jnp.mean(dxh * xh, axis=-1, keepdims=True))
        dz_o[...] = dz
        dzb_o[...] = dz.astype(CDT)
        sg = jnp.sum(dy * xh, axis=0, keepdims=True)
        sb = jnp.sum(dy, axis=0, keepdims=True)
        first = pl.program_id(0) == 0

        @pl.when(first)
        def _():
            dg_o[...] = sg
            db_o[...] = sb

        @pl.when(jnp.logical_not(first))
        def _():
            dg_o[...] += sg
            db_o[...] += sb

    ins = [dya] + ([dyb] if two else []) + [z, g]
    return pl.pallas_call(
        body, name=name, grid=(S // ts,),
        in_specs=[_row(ts, D)] * (3 if two else 2) + [_full(g.shape)],
        out_specs=[_row(ts, D), _row(ts, D), _full((1, D)), _full((1, D))],
        out_shape=[_sds((S, D), F32), _sds((S, D), CDT), _sds((1, D), F32), _sds((1, D), F32)],
        compiler_params=_cparams(1),
    )(*ins)


def _grp_spec(ts, w):
    return pl.BlockSpec((None, ts, w), lambda k, i: (k, i, 0))


def _swiglu_fwd(g3, u3):
    G, S, Fc = g3.shape
    ts = min(512, S)

    def body(g_ref, u_ref, a_o):
        g = g_ref[...]
        a_o[...] = (g / (1.0 + jnp.exp(-g)) * u_ref[...]).astype(CDT)

    return pl.pallas_call(
        body, name="swiglu_fwd", grid=(G, S // ts), in_specs=[_grp_spec(ts, Fc)] * 2, out_specs=_grp_spec(ts, Fc),
        out_shape=_sds(g3.shape, CDT), compiler_params=_cparams(2),
    )(g3, u3)


def _swiglu_bwd(da3, g3, u3):
    G, S, Fc = g3.shape
    ts = min(512, S)

    def body(da_ref, g_ref, u_ref, dg_o, du_o):
        g = g_ref[...]
        sg = 1.0 / (1.0 + jnp.exp(-g))
        da = da_ref[...]
        dg_o[...] = (da * u_ref[...] * (sg * (1.0 + g * (1.0 - sg)))).astype(CDT)
        du_o[...] = (da * (g * sg)).astype(CDT)

    return pl.pallas_call(
        body, name="swiglu_bwd", grid=(G, S // ts), in_specs=[_grp_spec(ts, Fc)] * 3, out_specs=[_grp_spec(ts, Fc)] * 2,
        out_shape=[_sds(g3.shape, CDT)] * 2, compiler_params=_cparams(2),
    )(da3, g3, u3)


def _loss_kernel(y, target):
    S, D = y.shape
    ts = min(256, S)

    def body(y_ref, t_ref, dy_o, loss_o):
        e = y_ref[...] - t_ref[...]
        dy_o[...] = e * (1.0 / D)
        part = jnp.sum(jnp.sum(e * e, axis=1, keepdims=True), axis=0, keepdims=True) * (0.5 / D)
        first = pl.program_id(0) == 0

        @pl.when(first)
        def _():
            loss_o[...] = part

        @pl.when(jnp.logical_not(first))
        def _():
            loss_o[...] += part

    return pl.pallas_call(
        body, name="loss", grid=(S // ts,), in_specs=[_row(ts, D)] * 2,
        out_specs=[_row(ts, D), _full((1, 1))], out_shape=[_sds((S, D), F32), _sds((1, 1), F32)],
        compiler_params=_cparams(1),
    )(y, target)


def _axpy(a, b, alpha, name):
    S, D = a.shape
    ts = min(256, S)

    def body(a_ref, b_ref, o_ref):
        o_ref[...] = a_ref[...] + alpha * b_ref[...]

    return pl.pallas_call(
        body, name=name, grid=(S // ts,), in_specs=[_row(ts, D)] * 2, out_specs=_row(ts, D),
        out_shape=_sds((S, D), F32), compiler_params=_cparams(1),
    )(a, b)


def _pair_masks():
    lane = lax.broadcasted_iota(jnp.int32, (1, LANES), 1)
    first = lane < HEAD_DIM
    return first, jnp.logical_not(first)


def _head_scalar(x, m):
    return jnp.max(jnp.where(m, x, -jnp.inf), axis=-1, keepdims=True)


_NT = (((1,), (1,)), ((), ()))
_TN = (((0,), (0,)), ((), ()))


def _attn_fwd(q, k, v, *, split, npairs, kblk, vblk, name):
    S = q.shape[0]
    qw = 256 if split else LANES
    tq = min(256, S)

    def body(q_ref, k_ref, v_ref, o_ref, lse_ref):
        masks = _pair_masks()
        outs, lses = [], []
        for hd in range(2):
            if split:
                qh = q_ref[:, hd * LANES:(hd + 1) * LANES]
                kh = k_ref[:, hd * LANES:(hd + 1) * LANES]
            else:
                qh = jnp.where(masks[hd], q_ref[...], jnp.zeros_like(q_ref[...]))
                kh = k_ref[...]
            s = lax.dot_general(qh, kh, _NT, preferred_element_type=F32)
            mx = jnp.max(s, axis=-1, keepdims=True)
            p = jnp.exp(s - mx)
            l = jnp.sum(p, axis=-1, keepdims=True)
            o = jnp.dot(p.astype(CDT), v_ref[...], preferred_element_type=F32)
            outs.append(o / l)
            lses.append(jnp.broadcast_to(mx + jnp.log(l), (tq, LANES)))
        o_ref[...] = jnp.where(masks[0], outs[0], outs[1]).astype(o_ref.dtype)
        lse_ref[...] = jnp.where(masks[0], lses[0], lses[1])

    return pl.pallas_call(
        body, name=name, grid=(npairs, S // tq),
        in_specs=[pl.BlockSpec((tq, qw), lambda p, i: (i, p)),
                  pl.BlockSpec((S, qw), lambda p, i: (0, kblk(p))),
                  pl.BlockSpec((S, LANES), lambda p, i: (0, vblk(p)))],
        out_specs=[pl.BlockSpec((tq, LANES), lambda p, i: (i, p))] * 2,
        out_shape=[_sds((S, LANES * npairs), CDT), _sds((S, LANES * npairs), F32)],
        compiler_params=_cparams(2),
    )(q, k, v)


def _attn_bwd(q, k, v, do, o, lse, *, split, npairs, kblk, vblk, doblk, shared_kv, name):
    S = q.shape[0]
    qw = 256 if split else LANES
    tq = min(256, S)
    nkv = 1 if shared_kv else npairs

    def body(q_ref, k_ref, v_ref, do_ref, o_ref, lse_ref, dq_ref, dk_ref, dv_ref):
        masks = _pair_masks()
        p_id, i_id = pl.program_id(0), pl.program_id(1)
        first = (i_id == 0) & ((p_id == 0) if shared_kv else True)
        do = do_ref[...]
        o = o_ref[...].astype(F32)
        lse = lse_ref[...]
        v = v_ref[...]
        dqs, dks, dvs = [], [], []
        for hd in range(2):
            m = masks[hd]
            if split:
                qh = q_ref[:, hd * LANES:(hd + 1) * LANES]
                kh = k_ref[:, hd * LANES:(hd + 1) * LANES]
            else:
                qh = jnp.where(m, q_ref[...], jnp.zeros_like(q_ref[...]))
                kh = k_ref[...]
            doh = jnp.where(m, do, 0.0)
            s = lax.dot_general(qh, kh, _NT, preferred_element_type=F32)
            p = jnp.exp(s - _head_scalar(lse, m))
            delta = jnp.sum(doh * o, axis=-1, keepdims=True)
            dohb = doh.astype(CDT)
            dp = lax.dot_general(dohb, v, _NT, preferred_element_type=F32)
            ds = (p * (dp - delta)).astype(CDT)
            dq = jnp.dot(ds, kh, preferred_element_type=F32)
            dqs.append(dq if split else jnp.where(m, dq, 0.0))
            dks.append(lax.dot_general(ds, qh, _TN, preferred_element_type=F32))
            dvs.append(lax.dot_general(p.astype(CDT), dohb, _TN, preferred_element_type=F32))
        if split:
            dq_ref[:, 0:LANES] = dqs[0]
            dq_ref[:, LANES:2 * LANES] = dqs[1]
        else:
            dq_ref[...] = dqs[0] + dqs[1]
        dv = dvs[0] + dvs[1]

        @pl.when(first)
        def _():
            if split:
                dk_ref[:, 0:LANES] = dks[0]
                dk_ref[:, LANES:2 * LANES] = dks[1]
            else:
                dk_ref[...] = dks[0] + dks[1]
            dv_ref[...] = dv

        @pl.when(jnp.logical_not(first))
        def _():
            if split:
                dk_ref[:, 0:LANES] += dks[0]
                dk_ref[:, LANES:2 * LANES] += dks[1]
            else:
                dk_ref[...] += dks[0] + dks[1]
            dv_ref[...] += dv

    kvo = (lambda p, i: (0, 0)) if shared_kv else (lambda p, i: (0, p))
    return pl.pallas_call(
        body, name=name, grid=(npairs, S // tq),
        in_specs=[pl.BlockSpec((tq, qw), lambda p, i: (i, p)),
                  pl.BlockSpec((S, qw), lambda p, i: (0, kblk(p))),
                  pl.BlockSpec((S, LANES), lambda p, i: (0, vblk(p))),
                  pl.BlockSpec((tq, LANES), lambda p, i: (i, doblk(p))),
                  pl.BlockSpec((tq, LANES), lambda p, i: (i, p)),
                  pl.BlockSpec((tq, LANES), lambda p, i: (i, p))],
        out_specs=[pl.BlockSpec((tq, qw), lambda p, i: (i, p)),
                   pl.BlockSpec((S, qw), kvo), pl.BlockSpec((S, LANES), kvo)],
        out_shape=[_sds((S, qw * npairs), F32), _sds((S, qw * nkv), F32), _sds((S, LANES * nkv), F32)],
        compiler_params=_cparams(2),
    )(q, k, v, do, o, lse)


def _bias_expand(idx, rel_bias, name):
    tq, kw = idx.shape

    def body(idx_ref, rb_ref, o_ref):
        idx = idx_ref[...]
        for hd in range(DIL_HEADS):
            acc = jnp.full((tq, kw), NEG_INF, F32)
            for u in range(REL_BUCKETS):
                acc = jnp.where(idx == u, rb_ref[u, hd], acc)
            o_ref[hd] = acc

    return pl.pallas_call(
        body, name=name,
        in_specs=[pl.BlockSpec(memory_space=pltpu.VMEM), pl.BlockSpec(memory_space=pltpu.SMEM)],
        out_specs=pl.BlockSpec(memory_space=pltpu.VMEM),
        out_shape=_sds((DIL_HEADS, tq, kw), F32),
    )(idx, rel_bias)


def _bias_reduce(idx, dtab, name):
    tq, kw = idx.shape

    def body(idx_ref, d_ref, o_ref):
        idx = idx_ref[...]
        rowid = lax.broadcasted_iota(jnp.int32, (REL_BUCKETS, kw), 0)
        for hd in range(DIL_HEADS):
            d = d_ref[hd]
            acc = jnp.zeros((REL_BUCKETS, kw), F32)
            for u in range(REL_BUCKETS):
                r = jnp.sum(jnp.where(idx == u, d, 0.0), axis=0, keepdims=True)
                acc = jnp.where(rowid == u, r, acc)
            o_ref[hd] = jnp.sum(acc, axis=1, keepdims=True)

    return pl.pallas_call(
        body, name=name,
        in_specs=[pl.BlockSpec(memory_space=pltpu.VMEM)] * 2, out_specs=pl.BlockSpec(memory_space=pltpu.VMEM),
        out_shape=_sds((DIL_HEADS, REL_BUCKETS, 1), F32),
    )(idx, dtab)


def _dil_window(i, tq, kw, L):
    start = pl.multiple_of(i * tq, DIL_HALF)
    key = start + lax.broadcasted_iota(jnp.int32, (1, kw), 1) - DIL_HALF
    return start, (key >= 0) & (key < L)


def _dil_fwd(qv, kv, vv, tab, *, dil, L, tq, name):
    kw = tq + 2 * DIL_HALF
    npair = DIL_HEADS // 2

    def body(q_ref, k_ref, v_ref, t_ref, o_ref, lse_ref):
        masks = _pair_masks()
        start, valid = _dil_window(pl.program_id(2), tq, kw, L)
        kwin = k_ref[pl.ds(start, kw), :]
        vwin = v_ref[pl.ds(start, kw), :]
        outs, lses = [], []
        for hd in range(2):
            qh = jnp.where(masks[hd], q_ref[...], jnp.zeros_like(q_ref[...]))
            s = lax.dot_general(qh, kwin, _NT, preferred_element_type=F32) + t_ref[hd]
            s = jnp.where(valid, s, NEG_INF)
            mx = jnp.max(s, axis=-1, keepdims=True)
            p = jnp.exp(s - mx)
            l = jnp.sum(p, axis=-1, keepdims=True)
            outs.append(jnp.dot(p.astype(CDT), vwin, preferred_element_type=F32) / l)
            lses.append(jnp.broadcast_to(mx + jnp.log(l), (tq, LANES)))
        o_ref[...] = jnp.where(masks[0], outs[0], outs[1])
        lse_ref[...] = jnp.where(masks[0], lses[0], lses[1])

    blk = lambda p, c, i: (i, c * npair + p)
    res = lambda p, c, i: (0, c * npair + p)
    return pl.pallas_call(
        body, name=name, grid=(npair, dil, L // tq),
        in_specs=[pl.BlockSpec((tq, LANES), blk), pl.BlockSpec((L + 2 * DIL_HALF, LANES), res),
                  pl.BlockSpec((L + 2 * DIL_HALF, LANES), res), pl.BlockSpec((2, tq, kw), lambda p, c, i: (p, 0, 0))],
        out_specs=[pl.BlockSpec((tq, LANES), blk)] * 2,
        out_shape=[_sds(qv.shape, F32)] * 2,
        compiler_params=_cparams(3),
    )(qv, kv, vv, tab)


def _dil_bwd(qv, kv, vv, tab, dov, lsev, deltav, *, dil, L, tq, name):
    kw = tq + 2 * DIL_HALF
    npair = DIL_HEADS // 2

    def body(q_ref, k_ref, v_ref, t_ref, do_ref, lse_ref, dl_ref, dq_ref, dk_ref, dv_ref, dt_ref):
        masks = _pair_masks()
        c_id, i_id = pl.program_id(1), pl.program_id(2)
        start, valid = _dil_window(i_id, tq, kw, L)
        kwin = k_ref[pl.ds(start, kw), :]
        vwin = v_ref[pl.ds(start, kw), :]

        @pl.when(i_id == 0)
        def _():
            dk_ref[...] = jnp.zeros_like(dk_ref)
            dv_ref[...] = jnp.zeros_like(dv_ref)

        @pl.when((i_id == 0) & (c_id == 0))
        def _():
            dt_ref[...] = jnp.zeros_like(dt_ref)

        do = do_ref[...]
        dq = jnp.zeros((tq, LANES), F32)
        dk = jnp.zeros((kw, LANES), F32)
        dv = jnp.zeros((kw, LANES), F32)
        for hd in range(2):
            m = masks[hd]
            qh = jnp.where(m, q_ref[...], jnp.zeros_like(q_ref[...]))
            doh = jnp.where(m, do, jnp.zeros_like(do))
            s = lax.dot_general(qh, kwin, _NT, preferred_element_type=F32) + t_ref[hd]
            s = jnp.where(valid, s, NEG_INF)
            p = jnp.exp(s - _head_scalar(lse_ref[...], m))
            dp = lax.dot_general(doh, vwin, _NT, preferred_element_type=F32)
            ds = p * (dp - _head_scalar(dl_ref[...], m))
            dt_ref[hd] += ds
            dsb = ds.astype(CDT)
            dq = dq + jnp.where(m, jnp.dot(dsb, kwin, preferred_element_type=F32), 0.0)
            dk = dk + lax.dot_general(dsb, qh, _TN, preferred_element_type=F32)
            dv = dv + lax.dot_general(p.astype(CDT), doh, _TN, preferred_element_type=F32)
        dq_ref[...] = dq
        dk_ref[pl.ds(start, kw), :] += dk
        dv_ref[pl.ds(start, kw), :] += dv

    blk = lambda p, c, i: (i, c * npair + p)
    res = lambda p, c, i: (0, c * npair + p)
    tsp = pl.BlockSpec((2, tq, kw), lambda p, c, i: (p, 0, 0))
    Lp = L + 2 * DIL_HALF
    return pl.pallas_call(
        body, name=name, grid=(npair, dil, L // tq),
        in_specs=[pl.BlockSpec((tq, LANES), blk), pl.BlockSpec((Lp, LANES), res), pl.BlockSpec((Lp, LANES), res), tsp,
                  pl.BlockSpec((tq, LANES), blk), pl.BlockSpec((tq, LANES), blk), pl.BlockSpec((tq, LANES), blk)],
        out_specs=[pl.BlockSpec((tq, LANES), blk), pl.BlockSpec((Lp, LANES), res), pl.BlockSpec((Lp, LANES), res), tsp],
        out_shape=[_sds(qv.shape, F32), _sds(kv.shape, F32), _sds(kv.shape, F32), _sds(tab.shape, F32)],
        compiler_params=_cparams(3),
    )(qv, kv, vv, tab, dov, lsev, deltav)


def _mix_weights(l1, l2, l3):
    mx = jnp.maximum(jnp.maximum(l1, l2), l3)
    e1, e2, e3 = jnp.exp(l1 - mx), jnp.exp(l2 - mx), jnp.exp(l3 - mx)
    inv = 1.0 / (e1 + e2 + e3)
    return e1 * inv, e2 * inv, e3 * inv


def _dil_mix_fwd(os, ls):
    S, W = os[0].shape
    ts = min(256, S)

    def body(o1, o2, o3, l1, l2, l3, out):
        w1, w2, w3 = _mix_weights(l1[...], l2[...], l3[...])
        out[...] = (w1 * o1[...] + w2 * o2[...] + w3 * o3[...]).astype(CDT)

    return pl.pallas_call(
        body, name="dil_mix_fwd", grid=(S // ts,), in_specs=[_row(ts, W)] * 6, out_specs=_row(ts, W),
        out_shape=_sds((S, W), CDT), compiler_params=_cparams(1),
    )(*os, *ls)


def _dil_mix_bwd(dcat, os, ls, j384):
    S, W = os[0].shape
    ts = min(256, S)

    def body(do_ref, o1, o2, o3, l1, l2, l3, j_ref, d1, d2, d3, e1, e2, e3):
        ws = _mix_weights(l1[...], l2[...], l3[...])
        do = do_ref[...]
        o = ws[0] * o1[...] + ws[1] * o2[...] + ws[2] * o3[...]
        dot = _headsum(do * o, j_ref[...])
        for w, d_o, e_o in zip(ws, (d1, d2, d3), (e1, e2, e3)):
            d_o[...] = (w * do).astype(CDT)
            e_o[...] = w * dot

    return pl.pallas_call(
        body, name="dil_mix_bwd", grid=(S // ts,),
        in_specs=[_row(ts, W, 1)] + [_row(ts, W)] * 6 + [_full(j384.shape)],
        out_specs=[_row(ts, W)] * 6,
        out_shape=[_sds((S, W), CDT)] * 3 + [_sds((S, W), F32)] * 3,
        compiler_params=_cparams(1),
    )(dcat, *os, *ls, j384)


def _adamw_math(w, g, m, v):
    m = ADAM_B1 * m + (1.0 - ADAM_B1) * g
    v = ADAM_B2 * v + (1.0 - ADAM_B2) * (g * g)
    m_hat = m / (1.0 - ADAM_B1 ** ADAM_STEP)
    v_hat = v / (1.0 - ADAM_B2 ** ADAM_STEP)
    delta = -ADAM_LR * (m_hat / (jnp.sqrt(v_hat) + ADAM_EPS) + ADAM_WD * w)
    return delta, m, v


def _pick8(n, target):
    best = None
    for t in range(16, min(n, target) + 1, 16):
        if n % t == 0:
            best = t
    return best if best is not None else n


_ELEMS_PER_BLOCK = 256 * 1024


def _lead_spec(a, b):
    ta = _pick8(a, max(16, _ELEMS_PER_BLOCK // b))
    return ta, pl.BlockSpec((None, ta, b), lambda l, i: (l, i, 0))


def _adamw(w, g, m, v, name):
    L, a, b = w.shape
    ta, spec = _lead_spec(a, b)

    def body(w_ref, g_ref, m_ref, v_ref, d_o, m_o, v_o):
        d, mm, vv = _adamw_math(w_ref[...], g_ref[...], m_ref[...], v_ref[...])
        d_o[...] = d
        m_o[...] = mm
        v_o[...] = vv

    return pl.pallas_call(
        body, name=name, grid=(L, a // ta), in_specs=[spec] * 4, out_specs=[spec] * 3,
        out_shape=[_sds(w.shape, F32)] * 3, compiler_params=_cparams(2),
    )(w, g, m, v)


def _adamw_small(w, gall, m, v):
    R = w.shape[0]

    def body(w_ref, g_ref, m_ref, v_ref, g_o, d_o, m_o, v_o):
        g = g_ref[0]
        for k in range(1, 8):
            g = g + g_ref[k]
        d, mm, vv = _adamw_math(w_ref[...], g, m_ref[...], v_ref[...])
        g_o[...] = g
        d_o[...] = d
        m_o[...] = mm
        v_o[...] = vv

    vm = pl.BlockSpec(memory_space=pltpu.VMEM)
    return pl.pallas_call(
        body, name="adamw_small", in_specs=[vm] * 4, out_specs=[vm] * 4, out_shape=[_sds((R, LANES), F32)] * 4,
    )(w, gall, m, v)


def _sum_pair(g0, g1, t, name):
    n, a, b = t.shape
    ta, spec = _lead_spec(a, b)

    def body(g0_ref, g1_ref, t_ref, o_ref):
        mine = jnp.where(lax.axis_index("c") == 0, g0_ref[...], g1_ref[...])
        o_ref[...] = (mine + t_ref[...]).astype(o_ref.dtype)

    return pl.pallas_call(
        body, name=name, grid=(n, a // ta), in_specs=[spec] * 3, out_specs=spec, out_shape=_sds(t.shape, WIRE),
        compiler_params=_cparams(2),
    )(g0, g1, t)


def _sum_chips(t, name):
    _, a, b = t.shape
    ta = _pick8(a, max(16, _ELEMS_PER_BLOCK // b))

    def body(t_ref, o_ref):
        acc = t_ref[0].astype(F32)
        for k in range(1, 4):
            acc = acc + t_ref[k].astype(F32)
        o_ref[...] = acc

    return pl.pallas_call(
        body, name=name, grid=(a // ta,), in_specs=[pl.BlockSpec((4, ta, b), lambda i: (0, i, 0))],
        out_specs=pl.BlockSpec((ta, b), lambda i: (i, 0)), out_shape=_sds((a, b), F32), compiler_params=_cparams(1),
    )(t)


_HBM = pl.BlockSpec(memory_space=pltpu.HBM)


def _place():
    x, y, c = lax.axis_index("x"), lax.axis_index("y"), lax.axis_index("c")
    chips = [(1 - x, y), (x, 1 - y), (1 - x, 1 - y)]
    return x, y, c, chips


def _remote(src, dst, ssem, rsem, to):
    return pltpu.make_async_remote_copy(src_ref=src, dst_ref=dst, send_sem=ssem, recv_sem=rsem, device_id=to,
                                        device_id_type=MESH_ID)


def _dma_sems(n):
    return pltpu.SemaphoreType.DMA((n,))


def _gather_weights(shards):
    n = len(shards)

    def body(*refs):
        w_refs, g_refs = refs[:n], refs[n:2 * n]
        ssem, rsem, lsem = refs[2 * n:]
        x, y, c, chips = _place()
        me = 2 * x + y
        sib = (x, y, 1 - c)
        owns = [pltpu.make_async_copy(w.at[l], g.at[l, me], lsem.at[2 * i + l])
                for i, (w, g) in enumerate(zip(w_refs, g_refs)) for l in range(2)]
        for cp in owns:
            cp.start()
        first = [_remote(w.at[c], g.at[c, me], ssem.at[3 * i + j], rsem.at[3 * i + j], (cx, cy, c))
                 for j, (cx, cy) in enumerate(chips) for i, (w, g) in enumerate(zip(w_refs, g_refs))]
        for cp in first:
            cp.start()
        passed = []
        for j, (cx, cy) in enumerate(chips):
            for i, g in enumerate(g_refs):
                blk = g.at[c, 2 * cx + cy]
                _remote(blk, blk, ssem.at[3 * i + j], rsem.at[3 * i + j], sib).wait_recv()
                fw = _remote(blk, blk, ssem.at[3 * n + 3 * i + j], rsem.at[3 * n + 3 * i + j], sib)
                fw.start()
                passed.append(fw)
        for j, (cx, cy) in enumerate(chips):
            for i, g in enumerate(g_refs):
                blk = g.at[1 - c, 2 * cx + cy]
                _remote(blk, blk, ssem.at[3 * n + 3 * i + j], rsem.at[3 * n + 3 * i + j], sib).wait_recv()
        for cp in first + passed:
            cp.wait_send()
        for cp in owns:
            cp.wait()

    return pl.pallas_call(
        body, name="gather_weights", in_specs=[_HBM] * n, out_specs=[_HBM] * n,
        out_shape=[_sds((2, 4) + s.shape[1:], s.dtype) for s in shards],
        scratch_shapes=[_dma_sems(6 * n), _dma_sems(6 * n), _dma_sems(2 * n)],
    )(*shards)


def _sibling_halves(g0s, g1s):
    n = len(g0s)

    def body(*refs):
        g0_refs, g1_refs, t_refs = refs[:n], refs[n:2 * n], refs[2 * n:3 * n]
        ssem, rsem = refs[3 * n:]
        x, y, c, _ = _place()

        def swap(srcs):
            cps = [_remote(s, t, ssem.at[i], rsem.at[i], (x, y, 1 - c)) for i, (s, t) in enumerate(zip(srcs, t_refs))]
            for cp in cps:
                cp.start()
            for cp in cps:
                cp.wait()

        @pl.when(c == 0)
        def _():
            swap(g1_refs)

        @pl.when(c == 1)
        def _():
            swap(g0_refs)

    return pl.pallas_call(
        body, name="rs_sibling_halves", in_specs=[_HBM] * (2 * n), out_specs=[_HBM] * n,
        out_shape=[_sds(g.shape, g.dtype) for g in g0s], scratch_shapes=[_dma_sems(n), _dma_sems(n)],
    )(*g0s, *g1s)


def _chip_all_to_all(parts):
    n = len(parts)

    def body(*refs):
        a_refs, t_refs = refs[:n], refs[n:2 * n]
        ssem, rsem, lsem = refs[2 * n:]
        x, y, c, chips = _place()
        me = 2 * x + y
        owns = [pltpu.make_async_copy(a.at[me], t.at[me], lsem.at[i]) for i, (a, t) in enumerate(zip(a_refs, t_refs))]
        for cp in owns:
            cp.start()
        sends = [_remote(a.at[2 * cx + cy], t.at[me], ssem.at[3 * i + j], rsem.at[3 * i + j], (cx, cy, c))
                 for j, (cx, cy) in enumerate(chips) for i, (a, t) in enumerate(zip(a_refs, t_refs))]
        for cp in sends:
            cp.start()
        for j, (cx, cy) in enumerate(chips):
            for i, t in enumerate(t_refs):
                blk = t.at[2 * cx + cy]
                _remote(blk, blk, ssem.at[3 * i + j], rsem.at[3 * i + j], (cx, cy, c)).wait_recv()
        for cp in sends:
            cp.wait_send()
        for cp in owns:
            cp.wait()

    return pl.pallas_call(
        body, name="rs_chip_all_to_all", in_specs=[_HBM] * n, out_specs=[_HBM] * n,
        out_shape=[_sds(p.shape, p.dtype) for p in parts],
        scratch_shapes=[_dma_sems(3 * n), _dma_sems(3 * n), _dma_sems(n)],
    )(*parts)


def _sibling_gather(reds):
    n = len(reds)

    def body(*refs):
        q_refs, o_refs = refs[:n], refs[n:2 * n]
        ssem, rsem, lsem = refs[2 * n:]
        x, y, c, _ = _place()
        sib = (x, y, 1 - c)
        owns = [pltpu.make_async_copy(q, o.at[c], lsem.at[i]) for i, (q, o) in enumerate(zip(q_refs, o_refs))]
        cps = [_remote(q, o.at[c], ssem.at[i], rsem.at[i], sib) for i, (q, o) in enumerate(zip(q_refs, o_refs))]
        for cp in owns + cps:
            cp.start()
        for i, o in enumerate(o_refs):
            other = o.at[1 - c]
            _remote(other, other, ssem.at[i], rsem.at[i], sib).wait_recv()
        for cp in cps:
            cp.wait_send()
        for cp in owns:
            cp.wait()

    return pl.pallas_call(
        body, name="rs_sibling_gather", in_specs=[_HBM] * n, out_specs=[_HBM] * n,
        out_shape=[_sds((2,) + q.shape, q.dtype) for q in reds],
        scratch_shapes=[_dma_sems(n), _dma_sems(n), _dma_sems(n)],
    )(*reds)


def _gather_small(s):
    R, _ = s.shape

    def body(s_ref, o_ref, ssem, rsem, lsem):
        x, y, c, _ = _place()
        me = 4 * x + 2 * y + c
        own = pltpu.make_async_copy(s_ref, o_ref.at[me], lsem)
        own.start()
        sends = []
        for k in range(1, 8):
            px, py, pc = x ^ (k >> 2), y ^ ((k >> 1) & 1), c ^ (k & 1)
            cp = _remote(s_ref, o_ref.at[me], ssem.at[k - 1], rsem.at[k - 1], (px, py, pc))
            cp.start()
            sends.append(cp)
        for k in range(1, 8):
            px, py, pc = x ^ (k >> 2), y ^ ((k >> 1) & 1), c ^ (k & 1)
            blk = o_ref.at[4 * px + 2 * py + pc]
            _remote(blk, blk, ssem.at[k - 1], rsem.at[k - 1], (px, py, pc)).wait_recv()
        for cp in sends:
            cp.wait_send()
        own.wait()

    vm = pl.BlockSpec(memory_space=pltpu.VMEM)
    return pl.pallas_call(
        body, name="gather_small", in_specs=[vm], out_specs=vm, out_shape=_sds((8, R, LANES), s.dtype),
        scratch_shapes=[pltpu.SemaphoreType.DMA((7,)), pltpu.SemaphoreType.DMA((7,)), pltpu.SemaphoreType.DMA],
    )(s)


_BIG = ("w_in", "mla_w_uq", "mla_w_ukv", "w_out", "ffn_w_gate", "ffn_w_up", "ffn_w_down")
_SMALL = ("mla_q_norm", "mla_kv_norm", "gqa_q_norm", "gqa_k_norm", "rel_bias", "ln1_g", "ln1_b", "ln2_g", "ln2_b")


def _pack_flat(arrs, align):
    flat = jnp.concatenate([a.reshape(-1) for a in arrs])
    pad = (-flat.shape[0]) % align
    return jnp.pad(flat, (0, pad)) if pad else flat


def _unpack_flat(flat, shapes):
    out, off = [], 0
    for s in shapes:
        n = int(np.prod(s))
        out.append(flat[off:off + n].reshape(s))
        off += n
    return out


def _perm_gqa_rows(w):
    return jnp.concatenate([w[:832], w[896:960], w[832:896], w[960:]], axis=0)


def _local_step(x, target, W, small):
    S, D = x.shape
    depth = W["w_in"].shape[0]
    alpha = (2.0 * depth) ** 0.25
    in_idx, uq_idx, ukv_idx = _in_cols(), _uq_cols(), _ukv_cols()
    win = [_cols_from_shards(W["w_in"][l], in_idx) for l in range(depth)]
    wuq = [_cols_from_shards(W["mla_w_uq"][l], uq_idx) for l in range(depth)]
    wukv = [_cols_from_shards(W["mla_w_ukv"][l], ukv_idx) for l in range(depth)]
    wout = [_perm_gqa_rows(W["w_out"][l].reshape(-1, D)) for l in range(depth)]
    wg, wu, wdn = W["ffn_w_gate"], W["ffn_w_up"], W["ffn_w_down"]

    tm, tg = _rope_tables(S)
    j256, j384 = _head_ones(256), _head_ones(384)
    mla_scale = (64 + MLA_ROPE_DIM) ** -0.5
    branches = []
    for (_, dil) in DIL_BRANCHES:
        L = S // dil
        tq = min(256, L)
        idx = jnp.asarray(_branch_bucket_idx(tq, dil))
        branches.append((dil, L, tq, idx))
    tabs = [_bias_expand(idx, small["rel_bias"], name=f"bias_expand_{b}") for b, (_, _, _, idx) in enumerate(branches)]

    def view(a, dil):
        return a.reshape(a.shape[0] // dil, dil * a.shape[1])

    def padded_view(a, dil):
        z = jnp.zeros((DIL_HALF * dil, a.shape[1]), a.dtype)
        return view(jnp.concatenate([z, a, z], axis=0), dil)

    def unpad_view(a, dil):
        return a[DIL_HALF:a.shape[0] - DIL_HALF].reshape(S, 384)

    saved = []
    xf, xb = x, x.astype(CDT)
    for l in range(depth):
        gq, gkv = small["mla_q_norm"][l][None], small["mla_kv_norm"][l][None]
        ggq = jnp.tile(small["gqa_q_norm"][l], 4)[None]
        ggk = jnp.tile(small["gqa_k_norm"][l], 2)[None]
        h = _mm(xb, win[l], name="mm_in")
        cq, ckv, kr, qd, kd, vd, qg, kg, vg = _prep_fwd(h, gq, gkv, ggq, ggk, tm, tg, j256)
        qa = _mm(cq, wuq[l], name="mm_uq")
        kvp = _mm(ckv, wukv[l], out_dtype=CDT, name="mm_ukv")
        qm, km = _mla_prep_fwd(qa, kvp, kr, tm, mla_scale)
        oa, lsa = _attn_fwd(qm, km, kvp, split=True, npairs=3, kblk=lambda p: p, vblk=lambda p: 6 + p,
                            name="mla_attn_fwd")
        oc, lsc = _attn_fwd(qg, kg, vg, split=False, npairs=2, kblk=lambda p: 0, vblk=lambda p: 0,
                            name="gqa_attn_fwd")
        obs, lbs, kvs = [], [], []
        for b, (dil, L, tq, _) in enumerate(branches):
            kpv, vpv = padded_view(kd, dil), padded_view(vd, dil)
            o_b, l_b = _dil_fwd(view(qd, dil), kpv, vpv, tabs[b], dil=dil, L=L, tq=tq, name=f"dil_fwd_{b}")
            obs.append(o_b.reshape(S, 384))
            lbs.append(l_b.reshape(S, 384))
            kvs.append((kpv, vpv))
        ob = _dil_mix_fwd(obs, lbs)
        cat = jnp.concatenate([oa, ob, oc], axis=1)
        mix = _mm(cat, wout[l], name="mm_out")
        x1, x1b, z1 = _ln_fwd(xf, mix, small["ln1_g"][l][None], small["ln1_b"][l][None], alpha, name="ln1_fwd")
        g3 = _mm(x1b, wg[l], gb=True, go=True, name="mm_gate")
        u3 = _mm(x1b, wu[l], gb=True, go=True, name="mm_up")
        act = _swiglu_fwd(g3, u3)
        ff = _mm(act, wdn[l], ga=True, gb=True, name="mm_down")
        x2, x2b, z2 = _ln_fwd(x1, ff, small["ln2_g"][l][None], small["ln2_b"][l][None], alpha, name="ln2_fwd")
        saved.append(dict(xb=xb, h=h, cq=cq, ckv=ckv, qd=qd, qg=qg, kg=kg, vg=vg, kvp=kvp, qm=qm, km=km, oa=oa, lsa=lsa,
                          oc=oc, lsc=lsc, obs=obs, lbs=lbs, kvs=kvs, cat=cat, z1=z1, x1b=x1b, g3=g3, u3=u3, act=act, z2=z2,
                          gq=gq, gkv=gkv, ggq=ggq, ggk=ggk))
        xf, xb = x2, x2b

    dy, loss = _loss_kernel(xf, target)

    gW = {k: [None] * depth for k in _BIG}
    gS = {k: [None] * depth for k in ("mla_q_norm", "mla_kv_norm", "gqa_q_norm", "gqa_k_norm", "ln1_g", "ln1_b", "ln2_g",
                                      "ln2_b")}
    g_rel = None
    dya, dyb = dy, None
    for l in reversed(range(depth)):
        sv = saved[l]
        dz2, dz2b, gS["ln2_g"][l], gS["ln2_b"][l] = _ln_bwd(dya, dyb, sv["z2"], small["ln2_g"][l][None], alpha,
                                                             name="ln2_bwd" if dyb is not None else "ln2_bwd_last")
        gW["ffn_w_down"][l] = _mm(sv["act"], dz2b, ta=True, ga=True, go=True, name="mm_down_dw")
        da3 = _mm(dz2b, wdn[l], tb=True, gb=True, go=True, name="mm_down_dx")
        dg3, du3 = _swiglu_bwd(da3, sv["g3"], sv["u3"])
        gW["ffn_w_gate"][l] = _mm(sv["x1b"], dg3, ta=True, gb=True, go=True, name="mm_gate_dw")
        gW["ffn_w_up"][l] = _mm(sv["x1b"], du3, ta=True, gb=True, go=True, name="mm_up_dw")
        dx1 = _mm(dg3, wg[l], tb=True, ga=True, gb=True, name="mm_gate_dx")
        dx1 = _mm(du3, wu[l], tb=True, ga=True, gb=True, add=dx1, name="mm_up_dx")
        dz1, dz1b, gS["ln1_g"][l], gS["ln1_b"][l] = _ln_bwd(dx1, dz2, sv["z1"], small["ln1_g"][l][None], alpha,
                                                             name="ln1_bwd")
        gW["w_out"][l] = _perm_gqa_rows(_mm(sv["cat"], dz1b, ta=True, name="mm_out_dw")).reshape(4, -1, D)
        dcat = _mm(dz1b, wout[l], tb=True, name="mm_out_dx")
        dqg, dkg, dvg = _attn_bwd(sv["qg"], sv["kg"], sv["vg"], dcat, sv["oc"], sv["lsc"], split=False, npairs=2,
                                  kblk=lambda p: 0, vblk=lambda p: 0, doblk=lambda p: 6 + p, shared_kv=True,
                                  name="gqa_attn_bwd")
        dqm, dkm, dvm = _attn_bwd(sv["qm"], sv["km"], sv["kvp"], dcat, sv["oa"], sv["lsa"], split=True, npairs=3,
                                  kblk=lambda p: p, vblk=lambda p: 6 + p, doblk=lambda p: p, shared_kv=False,
                                  name="mla_attn_bwd")
        dqa, dkvp, dkr = _mla_prep_bwd(dqm, dkm, dvm, tm, mla_scale)
        gW["mla_w_uq"][l] = _cols_to_shards(_mm(sv["cq"], dqa, ta=True, name="mm_uq_dw"), uq_idx, MLA_HEADS * 96)
        dcq = _mm(dqa, wuq[l], tb=True, name="mm_uq_dx")
        gW["mla_w_ukv"][l] = _cols_to_shards(_mm(sv["ckv"], dkvp, ta=True, name="mm_ukv_dw"), ukv_idx, MLA_HEADS * 128)
        dckv = _mm(dkvp, wukv[l], tb=True, name="mm_ukv_dx")
        mixb = _dil_mix_bwd(dcat, sv["obs"], sv["lbs"], j384)
        ddq, ddk, ddv = [], [], []
        for b, (dil, L, tq, idx) in enumerate(branches):
            kpv, vpv = sv["kvs"][b]
            dq_b, dk_b, dv_b, dtab = _dil_bwd(view(sv["qd"], dil), kpv, vpv, tabs[b], view(mixb[b], dil),
                                              view(sv["lbs"][b], dil), view(mixb[3 + b], dil), dil=dil, L=L, tq=tq,
                                              name=f"dil_bwd_{b}")
            ddq.append(dq_b.reshape(S, 384))
            ddk.append(unpad_view(dk_b, dil))
            ddv.append(unpad_view(dv_b, dil))
            g_b = _bias_reduce(idx, dtab, name=f"bias_reduce_{b}")[:, :, 0].T
            g_rel = g_b if g_rel is None else g_rel + g_b
        dh, n1, n2, n3, n4 = _prep_bwd(sv["h"], dcq, dckv, dkr, ddq, ddk, ddv, dqg, dkg, dvg, sv["gq"], sv["gkv"],
                                       sv["ggq"], sv["ggk"], tg, j256)
        gS["mla_q_norm"][l], gS["mla_kv_norm"][l] = n1[0], n2[0]
        gS["gqa_q_norm"][l] = n3[0].reshape(4, 64).sum(0)
        gS["gqa_k_norm"][l] = n4[0].reshape(2, 64).sum(0)
        gW["w_in"][l] = _cols_to_shards(_mm(sv["xb"], dh, ta=True, name="mm_in_dw"), in_idx, IN_W)
        dya = _mm(dh, win[l], tb=True, name="mm_in_dx")
        dyb = dz1
    grad_x = _axpy(dya, dyb, alpha, name="grad_x")

    gsmall = {k: jnp.stack([a.reshape(-1) for a in v]) for k, v in gS.items()}
    gsmall["rel_bias"] = g_rel
    return loss, grad_x, gW, gsmall


_ORDER = ("w_in", "mla_q_norm", "mla_kv_norm", "mla_w_uq", "mla_w_ukv", "gqa_q_norm", "gqa_k_norm", "rel_bias", "w_out",
          "ln1_g", "ln1_b", "ffn_w_gate", "ffn_w_up", "ffn_w_down", "ln2_g", "ln2_b")


def kernel(x, w_in, mla_q_norm, mla_kv_norm, mla_w_uq, mla_w_ukv, gqa_q_norm, gqa_k_norm, rel_bias, w_out, ln1_g, ln1_b, ffn_w_gate, ffn_w_up, ffn_w_down, ln2_g, ln2_b, loss_target, m_w_in, m_mla_q_norm, m_mla_kv_norm, m_mla_w_uq, m_mla_w_ukv, m_gqa_q_norm, m_gqa_k_norm, m_rel_bias, m_w_out, m_ln1_g, m_ln1_b, m_ffn_w_gate, m_ffn_w_up, m_ffn_w_down, m_ln2_g, m_ln2_b, v_w_in, v_mla_q_norm, v_mla_kv_norm, v_mla_w_uq, v_mla_w_ukv, v_gqa_q_norm, v_gqa_k_norm, v_rel_bias, v_w_out, v_ln1_g, v_ln1_b, v_ffn_w_gate, v_ffn_w_up, v_ffn_w_down, v_ln2_g, v_ln2_b):
    wts = dict(zip(_ORDER, (w_in, mla_q_norm, mla_kv_norm, mla_w_uq, mla_w_ukv, gqa_q_norm, gqa_k_norm, rel_bias, w_out,
                            ln1_g, ln1_b, ffn_w_gate, ffn_w_up, ffn_w_down, ln2_g, ln2_b)))
    mom = dict(zip(_ORDER, (m_w_in, m_mla_q_norm, m_mla_kv_norm, m_mla_w_uq, m_mla_w_ukv, m_gqa_q_norm, m_gqa_k_norm,
                            m_rel_bias, m_w_out, m_ln1_g, m_ln1_b, m_ffn_w_gate, m_ffn_w_up, m_ffn_w_down, m_ln2_g,
                            m_ln2_b)))
    var = dict(zip(_ORDER, (v_w_in, v_mla_q_norm, v_mla_kv_norm, v_mla_w_uq, v_mla_w_ukv, v_gqa_q_norm, v_gqa_k_norm,
                            v_rel_bias, v_w_out, v_ln1_g, v_ln1_b, v_ffn_w_gate, v_ffn_w_up, v_ffn_w_down, v_ln2_g,
                            v_ln2_b)))
    small_shapes = [wts[n].shape for n in _SMALL]

    gathered = _gather_weights([wts[n].astype(WIRE) for n in _BIG])
    full = {n: g.astype(CDT) for n, g in zip(_BIG, gathered)}

    small = {n: wts[n] for n in _SMALL}
    loss, grad_x, gbig, gsmall = _local_step(x[0], loss_target[0], full, small)

    g0s, g1s = [gbig[n][0] for n in _BIG], [gbig[n][1] for n in _BIG]
    theirs = _sibling_halves(g0s, g1s)
    pairs = [_sum_pair(g0, g1, t, name=f"rs_pair_sum_{n}") for n, g0, g1, t in zip(_BIG, g0s, g1s, theirs)]
    reds = [_sum_chips(t, name=f"rs_sum_chips_{n}") for n, t in zip(_BIG, _chip_all_to_all(pairs))]
    gshard = dict(zip(_BIG, _sibling_gather(reds)))

    sflat = _pack_flat([gsmall[n].reshape(-1) for n in _SMALL], 8 * LANES)
    rs = sflat.shape[0] // LANES
    sall = _gather_small(sflat.reshape(rs, LANES))

    def packed(d):
        return _pack_flat([d[n] for n in _SMALL], 8 * LANES).reshape(rs, LANES)

    outs = {tag: {} for tag in ("grad", "delta", "new_m", "new_v")}
    for n in _BIG:
        outs["grad"][n] = gshard[n]
        outs["delta"][n], outs["new_m"][n], outs["new_v"][n] = _adamw(wts[n], gshard[n], mom[n], var[n],
                                                                     name=f"adamw_{n}")
    for tag, smallflat in zip(("grad", "delta", "new_m", "new_v"), _adamw_small(packed(wts), sall, packed(mom), packed(var))):
        outs[tag].update(zip(_SMALL, _unpack_flat(smallflat.reshape(-1), small_shapes)))

    total = lax.psum(loss[0, 0], ("x", "y", "c"))
    return (total, grad_x[None], *[outs["grad"][n] for n in _ORDER], *[outs["delta"][n] for n in _ORDER],
            *[outs["new_m"][n] for n in _ORDER], *[outs["new_v"][n] for n in _ORDER])
```

```python
import functools
import math

import numpy as np
import jax
import jax.numpy as jnp
from jax import lax
from jax.experimental import pallas as pl
from jax.experimental.pallas import tpu as pltpu

F32 = jnp.float32
CDT = jnp.bfloat16
WIRE = jnp.bfloat16

HEAD_DIM = 64
GRID_W = 64
ROPE_THETA = 10000.0
MLA_HEADS = 6
MLA_Q_RANK = 256
MLA_KV_RANK = 128
MLA_ROPE_DIM = 32
DIL_HEADS = 6
DIL_BRANCHES = ((128, 1), (512, 4), (2048, 16))
DIL_HALF = 64
GQA_Q_HEADS = 4
REL_BUCKETS = 32
REL_MAX_DIST = 1024
NEG_INF = -1e30
LANES = 128
VMEM_LIMIT = 56 * 1024 * 1024

ADAM_LR, ADAM_B1, ADAM_B2, ADAM_EPS, ADAM_WD, ADAM_STEP = 0.001, 0.9, 0.999, 1e-08, 0.01, 10

C_CQ, C_CKV, C_KR, C_DQ, C_DK, C_DV, C_GQ, C_GK, C_GV, IN_P = 0, 256, 384, 512, 896, 1280, 1664, 1920, 2048, 2176
IN_W = 2080
MESH_ID = pl.DeviceIdType.MESH


def _cparams(n_axes, vmem=VMEM_LIMIT):
    return pltpu.CompilerParams(dimension_semantics=("arbitrary",) * n_axes, vmem_limit_bytes=vmem)


MAX_WHOLE_DIM = 2304


def _pick(n, target):
    best = None
    for t in range(LANES, min(n, target) + 1, LANES):
        if n % t == 0:
            best = t
    if best is not None and (2 * best >= target or n > MAX_WHOLE_DIM):
        return best
    return n


def _sds(shape, dtype):
    return jax.ShapeDtypeStruct(tuple(shape), dtype)


def _in_cols():
    idx = -np.ones((IN_P,), np.int64)
    idx[C_CQ:C_CQ + 256] = np.arange(0, 256)
    idx[C_CKV:C_CKV + 128] = np.arange(256, 384)
    idx[C_KR + 64:C_KR + 96] = np.arange(384, 416)
    idx[C_DQ:C_DQ + 1152] = np.arange(416, 1568)
    gq = 1568 + (np.array([0, 2, 1, 3])[:, None] * 64 + np.arange(64)[None, :]).reshape(-1)
    idx[C_GQ:C_GQ + 256] = gq
    idx[C_GK:C_GK + 256] = np.arange(1824, 2080)
    return idx


def _uq_cols():
    idx = -np.ones((MLA_HEADS * 128,), np.int64)
    for h in range(MLA_HEADS):
        idx[h * 128:h * 128 + 96] = np.arange(96 * h, 96 * h + 96)
    return idx


def _ukv_cols():
    idx = -np.ones((MLA_HEADS * 128 + MLA_HEADS * 64,), np.int64)
    for h in range(MLA_HEADS):
        idx[h * 128:h * 128 + 64] = np.arange(128 * h, 128 * h + 64)
        idx[768 + h * 64:768 + h * 64 + 64] = np.arange(128 * h + 64, 128 * h + 128)
    return idx


def _out_rows():
    idx = np.arange(1024)
    idx[768:1024] = 768 + (np.array([0, 2, 1, 3])[:, None] * 64 + np.arange(64)[None, :]).reshape(-1)
    return idx


def _runs(idx):
    out, i = [], 0
    while i < len(idx):
        j = i + 1
        while j < len(idx) and ((idx[i] < 0 and idx[j] < 0) or (idx[i] >= 0 and idx[j] == idx[j - 1] + 1)):
            j += 1
        out.append((int(idx[i]), j - i))
        i = j
    return out


def _cols_from_shards(sh, idx):
    _, r, cs = sh.shape
    pieces = []
    for first, ln in _runs(idx):
        if first < 0:
            pieces.append(jnp.zeros((r, ln), sh.dtype))
            continue
        while ln > 0:
            k, off = divmod(first, cs)
            take = min(ln, cs - off)
            pieces.append(sh[k, :, off:off + take])
            first, ln = first + take, ln - take
    return jnp.concatenate(pieces, axis=1)


def _cols_to_shards(wp, idx, n):
    inv = np.zeros((n,), np.int64)
    pos = np.nonzero(idx >= 0)[0]
    inv[idx[pos]] = pos
    cs = n // 4
    shards = []
    for k in range(4):
        pieces = [wp[:, first:first + ln] for first, ln in _runs(inv[k * cs:(k + 1) * cs])]
        shards.append(jnp.concatenate(pieces, axis=1))
    return jnp.stack(shards)


def _t5_bucket_np(rel):
    nb = REL_BUCKETS // 2
    exact = nb // 2
    ret = np.where(rel > 0, nb, 0)
    n = np.abs(rel)
    nf = np.maximum(n, 1).astype(np.float32)
    large = exact + (np.log(nf / np.float32(exact)) / np.float32(math.log(REL_MAX_DIST / exact))
                     * np.float32(nb - exact)).astype(np.int32)
    large = np.minimum(large, nb - 1)
    return ret + np.where(n < exact, n, large)


def _branch_bucket_idx(tq, dil):
    kw = tq + 2 * DIL_HALF
    rel = np.arange(kw)[None, :] - DIL_HALF - np.arange(tq)[:, None]
    idx = _t5_bucket_np(rel * dil)
    return np.where(np.abs(rel) <= DIL_HALF, idx, -1).astype(np.int32)


def _rope_tables(S):
    inv = ROPE_THETA ** (-jnp.arange(0, 32, 2, dtype=F32) / 32)
    t = jnp.arange(S)
    pos = t.astype(F32)
    row = (t // GRID_W).astype(F32)
    col = (t % GRID_W).astype(F32)
    lane = np.arange(LANES)
    wm = lane - 64
    is_rope = (lane >= 64) & (lane < 96)
    ang = pos[:, None] * inv[np.where(is_rope, wm % 16, 0)][None, :]
    cm = jnp.where(is_rope[None], jnp.cos(ang), 1.0)
    smm = jnp.where((is_rope & (wm < 16))[None], -jnp.sin(ang), 0.0)
    spm = jnp.where((is_rope & (wm >= 16))[None], jnp.sin(ang), 0.0)
    g = lane % 64
    w = g % 32
    angg = jnp.where((g < 32)[None], row[:, None], col[:, None]) * inv[w % 16][None, :]
    cg = jnp.cos(angg)
    smg = jnp.where((w < 16)[None], -jnp.sin(angg), 0.0)
    spg = jnp.where((w >= 16)[None], jnp.sin(angg), 0.0)
    return (cm, smm, spm), (cg, smg, spg)


def _lanes(t, width):
    return t if width == LANES else jnp.concatenate([t] * (width // LANES), axis=1)


def _rope(x, tabs):
    c, sm, sp = (_lanes(t, x.shape[1]) for t in tabs)
    w = x.shape[1]
    return x * c + pltpu.roll(x, w - 16, 1) * sm + pltpu.roll(x, 16, 1) * sp


def _rope_t(dy, tabs):
    c, sm, sp = (_lanes(t, dy.shape[1]) for t in tabs)
    w = dy.shape[1]
    return dy * c + pltpu.roll(dy * sm, 16, 1) + pltpu.roll(dy * sp, w - 16, 1)


def _head_ones(width):
    i = np.arange(width)
    return jnp.asarray((i[:, None] // HEAD_DIM == i[None, :] // HEAD_DIM).astype(np.float32))


def _headsum(x, j):
    return jnp.dot(x, j, preferred_element_type=F32, precision=lax.Precision.HIGHEST)


def _mm(a, b, *, ta=False, tb=False, ga=False, gb=False, go=False, out_dtype=F32, add=None, name):
    G = a.shape[0] if ga else (b.shape[0] if gb else 1)
    a2 = a.shape[1:] if ga else a.shape
    b2 = b.shape[1:] if gb else b.shape
    K, M = a2 if ta else a2[::-1]
    N = b2[0] if tb else b2[1]
    assert (b2[1] if tb else b2[0]) == K
    tm, tn, tk = _pick(M, 1024), _pick(N, 1024), _pick(K, 1024)
    if tm * tn > 1024 * 1152:
        tm = _pick(M, 512)
    nk = K // tk
    steps = nk if (go or G == 1) else G * nk
    dn = (((0 if ta else 1,), (1 if tb else 0,)), ((), ()))

    def body(a_ref, b_ref, *rest):
        rest = list(rest)
        add_ref = rest.pop(0) if add is not None else None
        o_ref = rest.pop(0)
        part = lax.dot_general(a_ref[...], b_ref[...], dn, preferred_element_type=F32)
        if steps == 1:
            if add_ref is not None:
                part = part + add_ref[...]
            o_ref[...] = part.astype(o_ref.dtype)
            return
        acc_ref, = rest
        s = pl.program_id(3)

        @pl.when(s == 0)
        def _():
            acc_ref[...] = part if add_ref is None else part + add_ref[...]

        @pl.when(s > 0)
        def _():
            acc_ref[...] += part

        @pl.when(s == steps - 1)
        def _():
            o_ref[...] = acc_ref[...].astype(o_ref.dtype)

    def grp(g, s):
        return g if go else s // nk

    def kk(s):
        return s if steps == nk else s % nk

    def spec(grouped, block, index):
        if grouped:
            return pl.BlockSpec((None,) + block, lambda g, i, j, s: (grp(g, s),) + index(i, j, s))
        return pl.BlockSpec(block, lambda g, i, j, s: index(i, j, s))

    a_spec = (spec(ga, (tk, tm), lambda i, j, s: (kk(s), i)) if ta else spec(ga, (tm, tk), lambda i, j, s: (i, kk(s))))
    b_spec = (spec(gb, (tn, tk), lambda i, j, s: (j, kk(s))) if tb else spec(gb, (tk, tn), lambda i, j, s: (kk(s), j)))
    o_spec = spec(go, (tm, tn), lambda i, j, s: (i, j))
    return pl.pallas_call(
        body, name=name, grid=(G if go else 1, M // tm, N // tn, steps),
        in_specs=[a_spec, b_spec] + ([o_spec] if add is not None else []), out_specs=o_spec,
        out_shape=_sds(((G,) if go else ()) + (M, N), out_dtype),
        scratch_shapes=[pltpu.VMEM((tm, tn), F32)] if steps > 1 else [],
        compiler_params=_cparams(4),
    )(*([a, b] + ([add] if add is not None else [])))


def _row(ts, w, cb=0):
    return pl.BlockSpec((ts, w), lambda i: (i, cb))


def _full(shape):
    nd = len(shape)
    return pl.BlockSpec(tuple(shape), lambda i: (0,) * nd)


def _rms_fwd(x, g, eps=1e-6):
    r = lax.rsqrt(jnp.mean(x * x, axis=-1, keepdims=True) + eps)
    return x * r * g


def _rms_bwd(x, g, dy, eps=1e-6):
    r = lax.rsqrt(jnp.mean(x * x, axis=-1, keepdims=True) + eps)
    gdy = g * dy
    dx = r * gdy - x * (r * r * r) * jnp.mean(x * gdy, axis=-1, keepdims=True)
    return dx, x * r * dy


def _rms_head_fwd(x, g, j, eps=1e-6):
    r = lax.rsqrt(_headsum(x * x, j) * (1.0 / HEAD_DIM) + eps)
    return x * r * g


def _rms_head_bwd(x, g, dy, j, eps=1e-6):
    r = lax.rsqrt(_headsum(x * x, j) * (1.0 / HEAD_DIM) + eps)
    gdy = g * dy
    dx = r * gdy - x * (r * r * r) * (_headsum(x * gdy, j) * (1.0 / HEAD_DIM))
    return dx, x * r * dy


def _prep_fwd(h, gq, gkv, ggq, ggk, tm, tg, j256):
    S = h.shape[0]
    ts = min(256, S)
    scale = HEAD_DIM ** -0.5

    def body(h_ref, gq_ref, gkv_ref, ggq_ref, ggk_ref, cm, smm, spm, cg, smg, spg, j_ref,
             cq_o, ckv_o, kr_o, dq_o, dk_o, dv_o, gq_o, gk_o, gv_o):
        tabm = (cm[...], smm[...], spm[...])
        tabg = (cg[...], smg[...], spg[...])
        cq_o[...] = _rms_fwd(h_ref[:, C_CQ:C_CQ + 256], gq_ref[...]).astype(CDT)
        ckv_o[...] = _rms_fwd(h_ref[:, C_CKV:C_CKV + 128], gkv_ref[...]).astype(CDT)
        kr_o[...] = _rope(h_ref[:, C_KR:C_KR + 128], tabm).astype(CDT)
        dq_o[...] = (h_ref[:, C_DQ:C_DQ + 384] * scale).astype(CDT)
        dk_o[...] = h_ref[:, C_DK:C_DK + 384].astype(CDT)
        dv_o[...] = h_ref[:, C_DV:C_DV + 384].astype(CDT)
        qn = _rms_head_fwd(h_ref[:, C_GQ:C_GQ + 256], ggq_ref[...], j_ref[...])
        gq_o[...] = (_rope(qn, tabg) * scale).astype(CDT)
        kn = _rms_head_fwd(h_ref[:, C_GK:C_GK + 128], ggk_ref[...], j_ref[0:128, 0:128])
        gk_o[...] = _rope(kn, tabg).astype(CDT)
        gv_o[...] = h_ref[:, C_GV:C_GV + 128].astype(CDT)

    widths = (256, 128, 128, 384, 384, 384, 256, 128, 128)
    return pl.pallas_call(
        body, name="prep_fwd", grid=(S // ts,),
        in_specs=[_row(ts, IN_P), _full(gq.shape), _full(gkv.shape), _full(ggq.shape), _full(ggk.shape)]
        + [_row(ts, LANES)] * 6 + [_full(j256.shape)],
        out_specs=[_row(ts, w) for w in widths],
        out_shape=[_sds((S, w), CDT) for w in widths],
        compiler_params=_cparams(1),
    )(h, gq, gkv, ggq, ggk, *tm, *tg, j256)


def _prep_bwd(h, dcq, dckv, dkr, ddq, ddk, ddv, dgq, dgk, dgv, gq, gkv, ggq, ggk, tg, j256):
    S = h.shape[0]
    ts = min(256, S)
    scale = HEAD_DIM ** -0.5

    def body(h_ref, dcq_r, dckv_r, dkr_r, q1, q2, q3, k1, k2, k3, v1, v2, v3, dgq_r, dgk_r, dgv_r,
             gq_ref, gkv_ref, ggq_ref, ggk_ref, cg, smg, spg, j_ref,
             dh_o, ngq_o, ngkv_o, nggq_o, nggk_o):
        tabg = (cg[...], smg[...], spg[...])
        first = pl.program_id(0) == 0

        def acc(o_ref, val):
            s = jnp.sum(val, axis=0, keepdims=True)

            @pl.when(first)
            def _():
                o_ref[...] = s

            @pl.when(jnp.logical_not(first))
            def _():
                o_ref[...] += s

        dx, dg = _rms_bwd(h_ref[:, C_CQ:C_CQ + 256], gq_ref[...], dcq_r[...])
        dh_o[:, C_CQ:C_CQ + 256] = dx.astype(CDT)
        acc(ngq_o, dg)
        dx, dg = _rms_bwd(h_ref[:, C_CKV:C_CKV + 128], gkv_ref[...], dckv_r[...])
        dh_o[:, C_CKV:C_CKV + 128] = dx.astype(CDT)
        acc(ngkv_o, dg)
        dh_o[:, C_KR:C_KR + 128] = dkr_r[...].astype(CDT)
        dh_o[:, C_DQ:C_DQ + 384] = ((q1[...] + q2[...] + q3[...]) * scale).astype(CDT)
        dh_o[:, C_DK:C_DK + 384] = (k1[...] + k2[...] + k3[...]).astype(CDT)
        dh_o[:, C_DV:C_DV + 384] = (v1[...] + v2[...] + v3[...]).astype(CDT)
        dqn = _rope_t(dgq_r[...] * scale, tabg)
        dx, dg = _rms_head_bwd(h_ref[:, C_GQ:C_GQ + 256], ggq_ref[...], dqn, j_ref[...])
        dh_o[:, C_GQ:C_GQ + 256] = dx.astype(CDT)
        acc(nggq_o, dg)
        dkn = _rope_t(dgk_r[...], tabg)
        dx, dg = _rms_head_bwd(h_ref[:, C_GK:C_GK + 128], ggk_ref[...], dkn, j_ref[0:128, 0:128])
        dh_o[:, C_GK:C_GK + 128] = dx.astype(CDT)
        acc(nggk_o, dg)
        dh_o[:, C_GV:C_GV + 128] = dgv_r[...].astype(CDT)

    return pl.pallas_call(
        body, name="prep_bwd", grid=(S // ts,),
        in_specs=[_row(ts, IN_P), _row(ts, 256), _row(ts, 128), _row(ts, 128)] + [_row(ts, 384)] * 9
        + [_row(ts, 256), _row(ts, 128), _row(ts, 128)]
        + [_full(gq.shape), _full(gkv.shape), _full(ggq.shape), _full(ggk.shape)] + [_row(ts, LANES)] * 3
        + [_full(j256.shape)],
        out_specs=[_row(ts, IN_P), _full((1, 256)), _full((1, 128)), _full((1, 256)), _full((1, 128))],
        out_shape=[_sds((S, IN_P), CDT), _sds((1, 256), F32), _sds((1, 128), F32), _sds((1, 256), F32),
                   _sds((1, 128), F32)],
        compiler_params=_cparams(1),
    )(h, dcq, dckv, dkr, *ddq, *ddk, *ddv, dgq, dgk, dgv, gq, gkv, ggq, ggk, *tg, j256)


def _mla_prep_fwd(qa, kvp, kr, tm, scale):
    S = qa.shape[0]
    ts = min(256, S)

    def body(qa_ref, kv_ref, kr_ref, cm, smm, spm, q_o, k_o):
        tabm = (cm[...], smm[...], spm[...])
        q_o[...] = (_rope(qa_ref[...], tabm) * scale).astype(CDT)
        k_o[...] = kv_ref[:, 0:768] + _lanes(kr_ref[...], 768)

    return pl.pallas_call(
        body, name="mla_prep_fwd", grid=(S // ts,),
        in_specs=[_row(ts, 768), _row(ts, 1152), _row(ts, 128)] + [_row(ts, LANES)] * 3,
        out_specs=[_row(ts, 768)] * 2, out_shape=[_sds((S, 768), CDT)] * 2,
        compiler_params=_cparams(1),
    )(qa, kvp, kr, *tm)


def _mla_prep_bwd(dq, dk, dv, tm, scale):
    S = dq.shape[0]
    ts = min(256, S)

    def body(dq_ref, dk_ref, dv_ref, cm, smm, spm, dqa_o, dkv_o, dkr_o):
        tabm = (cm[...], smm[...], spm[...])
        lane = lax.broadcasted_iota(jnp.int32, (1, LANES), 1)
        dqa_o[...] = _rope_t(dq_ref[...] * scale, tabm).astype(CDT)
        dkr = jnp.zeros((ts, LANES), F32)
        for hd in range(MLA_HEADS):
            blk = dk_ref[:, hd * 128:(hd + 1) * 128]
            dkv_o[:, hd * 128:(hd + 1) * 128] = jnp.where(lane < 64, blk, 0.0).astype(CDT)
            dkr = dkr + jnp.where((lane >= 64) & (lane < 96), blk, 0.0)
        dkv_o[:, 768:1152] = dv_ref[...].astype(CDT)
        dkr_o[...] = jnp.where((lane >= 64) & (lane < 96), _rope_t(dkr, tabm), 0.0)

    return pl.pallas_call(
        body, name="mla_prep_bwd", grid=(S // ts,),
        in_specs=[_row(ts, 768), _row(ts, 768), _row(ts, 384)] + [_row(ts, LANES)] * 3,
        out_specs=[_row(ts, 768), _row(ts, 1152), _row(ts, 128)],
        out_shape=[_sds((S, 768), CDT), _sds((S, 1152), CDT), _sds((S, 128), F32)],
        compiler_params=_cparams(1),
    )(dq, dk, dv, *tm)


def _ln_fwd(xa, xb, g, b, alpha, name):
    S, D = xa.shape
    ts = min(256, S)

    def body(xa_ref, xb_ref, g_ref, b_ref, y_o, yb_o, z_o):
        z = alpha * xa_ref[...] + xb_ref[...]
        mu = jnp.mean(z, axis=-1, keepdims=True)
        zc = z - mu
        var = jnp.mean(zc * zc, axis=-1, keepdims=True)
        y = zc * lax.rsqrt(var + 1e-5) * g_ref[...] + b_ref[...]
        y_o[...] = y
        yb_o[...] = y.astype(CDT)
        z_o[...] = z

    return pl.pallas_call(
        body, name=name, grid=(S // ts,),
        in_specs=[_row(ts, D), _row(ts, D), _full(g.shape), _full(b.shape)],
        out_specs=[_row(ts, D)] * 3, out_shape=[_sds((S, D), F32), _sds((S, D), CDT), _sds((S, D), F32)],
        compiler_params=_cparams(1),
    )(xa, xb, g, b)


def _ln_bwd(dya, dyb, z, g, alpha, name):
    S, D = z.shape
    ts = min(256, S)
    two = dyb is not None

    def body(*refs):
        if two:
            dya_ref, dyb_ref, z_ref, g_ref, dz_o, dzb_o, dg_o, db_o = refs
            dy = dya_ref[...] + alpha * dyb_ref[...]
        else:
            dya_ref, z_ref, g_ref, dz_o, dzb_o, dg_o, db_o = refs
            dy = dya_ref[...]
        z = z_ref[...]
        mu = jnp.mean(z, axis=-1, keepdims=True)
        zc = z - mu
        r = lax.rsqrt(jnp.mean(zc * zc, axis=-1, keepdims=True) + 1e-5)
        xh = zc * r
        dxh = dy * g_ref[...]
        dz = r * (dxh - jnp.mean(dxh, axis=-1, keepdims=True) - xh * jnp.mean(dxh * xh, axis=-1, keepdims=True))
        dz_o[...] = dz
        dzb_o[...] = dz.astype(CDT)
        sg = jnp.sum(dy * xh, axis=0, keepdims=True)
        sb = jnp.sum(dy, axis=0, keepdims=True)
        first = pl.program_id(0) == 0

        @pl.when(first)
        def _():
            dg_o[...] = sg
            db_o[...] = sb

        @pl.when(jnp.logical_not(first))
        def _():
            dg_o[...] += sg
            db_o[...] += sb

    ins = [dya] + ([dyb] if two else []) + [z, g]
    return pl.pallas_call(
        body, name=name, grid=(S // ts,),
        in_specs=[_row(ts, D)] * (3 if two else 2) + [_full(g.shape)],
        out_specs=[_row(ts, D), _row(ts, D), _full((1, D)), _full((1, D))],
        out_shape=[_sds((S, D), F32), _sds((S, D), CDT), _sds((1, D), F32), _sds((1, D), F32)],
        compiler_params=_cparams(1),
    )(*ins)


def _grp_spec(ts, w):
    return pl.BlockSpec((None, ts, w), lambda k, i: (k, i, 0))


def _swiglu_fwd(g3, u3):
    G, S, Fc = g3.shape
    ts = min(512, S)

    def body(g_ref, u_ref, a_o):
        g = g_ref[...]
        a_o[...] = (g / (1.0 + jnp.exp(-g)) * u_ref[...]).astype(CDT)

    return pl.pallas_call(
        body, name="swiglu_fwd", grid=(G, S // ts), in_specs=[_grp_spec(ts, Fc)] * 2, out_specs=_grp_spec(ts, Fc),
        out_shape=_sds(g3.shape, CDT), compiler_params=_cparams(2),
    )(g3, u3)


def _swiglu_bwd(da3, g3, u3):
    G, S, Fc = g3.shape
    ts = min(512, S)

    def body(da_ref, g_ref, u_ref, dg_o, du_o):
        g = g_ref[...]
        sg = 1.0 / (1.0 + jnp.exp(-g))
        da = da_ref[...]
        dg_o[...] = (da * u_ref[...] * (sg * (1.0 + g * (1.0 - sg)))).astype(CDT)
        du_o[...] = (da * (g * sg)).astype(CDT)

    return pl.pallas_call(
        body, name="swiglu_bwd", grid=(G, S // ts), in_specs=[_grp_spec(ts, Fc)] * 3, out_specs=[_grp_spec(ts, Fc)] * 2,
        out_shape=[_sds(g3.shape, CDT)] * 2, compiler_params=_cparams(2),
    )(da3, g3, u3)


def _loss_kernel(y, target):
    S, D = y.shape
    ts = min(256, S)

    def body(y_ref, t_ref, dy_o, loss_o):
        e = y_ref[...] - t_ref[...]
        dy_o[...] = e * (1.0 / D)
        part = jnp.sum(jnp.sum(e * e, axis=1, keepdims=True), axis=0, keepdims=True) * (0.5 / D)
        first = pl.program_id(0) == 0

        @pl.when(first)
        def _():
            loss_o[...] = part

        @pl.when(jnp.logical_not(first))
        def _():
            loss_o[...] += part

    return pl.pallas_call(
        body, name="loss", grid=(S // ts,), in_specs=[_row(ts, D)] * 2,
        out_specs=[_row(ts, D), _full((1, 1))], out_shape=[_sds((S, D), F32), _sds((1, 1), F32)],
        compiler_params=_cparams(1),
    )(y, target)


def _axpy(a, b, alpha, name):
    S, D = a.shape
    ts = min(256, S)

    def body(a_ref, b_ref, o_ref):
        o_ref[...] = a_ref[...] + alpha * b_ref[...]

    return pl.pallas_call(
        body, name=name, grid=(S // ts,), in_specs=[_row(ts, D)] * 2, out_specs=_row(ts, D),
        out_shape=_sds((S, D), F32), compiler_params=_cparams(1),
    )(a, b)


def _pair_masks():
    lane = lax.broadcasted_iota(jnp.int32, (1, LANES), 1)
    first = lane < HEAD_DIM
    return first, jnp.logical_not(first)


def _head_scalar(x, m):
    return jnp.max(jnp.where(m, x, -jnp.inf), axis=-1, keepdims=True)


_NT = (((1,), (1,)), ((), ()))
_TN = (((0,), (0,)), ((), ()))


def _attn_fwd(q, k, v, *, split, npairs, kblk, vblk, name):
    S = q.shape[0]
    qw = 256 if split else LANES
    tq = min(256, S)

    def body(q_ref, k_ref, v_ref, o_ref, lse_ref):
        masks = _pair_masks()
        outs, lses = [], []
        for hd in range(2):
            if split:
                qh = q_ref[:, hd * LANES:(hd + 1) * LANES]
                kh = k_ref[:, hd * LANES:(hd + 1) * LANES]
            else:
                qh = jnp.where(masks[hd], q_ref[...], jnp.zeros_like(q_ref[...]))
                kh = k_ref[...]
            s = lax.dot_general(qh, kh, _NT, preferred_element_type=F32)
            mx = jnp.max(s, axis=-1, keepdims=True)
            p = jnp.exp(s - mx)
            l = jnp.sum(p, axis=-1, keepdims=True)
            o = jnp.dot(p.astype(CDT), v_ref[...], preferred_element_type=F32)
            outs.append(o / l)
            lses.append(jnp.broadcast_to(mx + jnp.log(l), (tq, LANES)))
        o_ref[...] = jnp.where(masks[0], outs[0], outs[1]).astype(o_ref.dtype)
        lse_ref[...] = jnp.where(masks[0], lses[0], lses[1])

    return pl.pallas_call(
        body, name=name, grid=(npairs, S // tq),
        in_specs=[pl.BlockSpec((tq, qw), lambda p, i: (i, p)),
                  pl.BlockSpec((S, qw), lambda p, i: (0, kblk(p))),
                  pl.BlockSpec((S, LANES), lambda p, i: (0, vblk(p)))],
        out_specs=[pl.BlockSpec((tq, LANES), lambda p, i: (i, p))] * 2,
        out_shape=[_sds((S, LANES * npairs), CDT), _sds((S, LANES * npairs), F32)],
        compiler_params=_cparams(2),
    )(q, k, v)


def _attn_bwd(q, k, v, do, o, lse, *, split, npairs, kblk, vblk, doblk, shared_kv, name):
    S = q.shape[0]
    qw = 256 if split else LANES
    tq = min(256, S)
    nkv = 1 if shared_kv else npairs

    def body(q_ref, k_ref, v_ref, do_ref, o_ref, lse_ref, dq_ref, dk_ref, dv_ref):
        masks = _pair_masks()
        p_id, i_id = pl.program_id(0), pl.program_id(1)
        first = (i_id == 0) & ((p_id == 0) if shared_kv else True)
        do = do_ref[...]
        o = o_ref[...].astype(F32)
        lse = lse_ref[...]
        v = v_ref[...]
        dqs, dks, dvs = [], [], []
        for hd in range(2):
            m = masks[hd]
            if split:
                qh = q_ref[:, hd * LANES:(hd + 1) * LANES]
                kh = k_ref[:, hd * LANES:(hd + 1) * LANES]
            else:
                qh = jnp.where(m, q_ref[...], jnp.zeros_like(q_ref[...]))
                kh = k_ref[...]
            doh = jnp.where(m, do, 0.0)
            s = lax.dot_general(qh, kh, _NT, preferred_element_type=F32)
            p = jnp.exp(s - _head_scalar(lse, m))
            delta = jnp.sum(doh * o, axis=-1, keepdims=True)
            dohb = doh.astype(CDT)
            dp = lax.dot_general(dohb, v, _NT, preferred_element_type=F32)
            ds = (p * (dp - delta)).astype(CDT)
            dq = jnp.dot(ds, kh, preferred_element_type=F32)
            dqs.append(dq if split else jnp.where(m, dq, 0.0))
            dks.append(lax.dot_general(ds, qh, _TN, preferred_element_type=F32))
            dvs.append(lax.dot_general(p.astype(CDT), dohb, _TN, preferred_element_type=F32))
        if split:
            dq_ref[:, 0:LANES] = dqs[0]
            dq_ref[:, LANES:2 * LANES] = dqs[1]
        else:
            dq_ref[...] = dqs[0] + dqs[1]
        dv = dvs[0] + dvs[1]

        @pl.when(first)
        def _():
            if split:
                dk_ref[:, 0:LANES] = dks[0]
                dk_ref[:, LANES:2 * LANES] = dks[1]
            else:
                dk_ref[...] = dks[0] + dks[1]
            dv_ref[...] = dv

        @pl.when(jnp.logical_not(first))
        def _():
            if split:
                dk_ref[:, 0:LANES] += dks[0]
                dk_ref[:, LANES:2 * LANES] += dks[1]
            else:
                dk_ref[...] += dks[0] + dks[1]
            dv_ref[...] += dv

    kvo = (lambda p, i: (0, 0)) if shared_kv else (lambda p, i: (0, p))
    return pl.pallas_call(
        body, name=name, grid=(npairs, S // tq),
        in_specs=[pl.BlockSpec((tq, qw), lambda p, i: (i, p)),
                  pl.BlockSpec((S, qw), lambda p, i: (0, kblk(p))),
                  pl.BlockSpec((S, LANES), lambda p, i: (0, vblk(p))),
                  pl.BlockSpec((tq, LANES), lambda p, i: (i, doblk(p))),
                  pl.BlockSpec((tq, LANES), lambda p, i: (i, p)),
                  pl.BlockSpec((tq, LANES), lambda p, i: (i, p))],
        out_specs=[pl.BlockSpec((tq, qw), lambda p, i: (i, p)),
                   pl.BlockSpec((S, qw), kvo), pl.BlockSpec((S, LANES), kvo)],
        out_shape=[_sds((S, qw * npairs), F32), _sds((S, qw * nkv), F32), _sds((S, LANES * nkv), F32)],
        compiler_params=_cparams(2),
    )(q, k, v, do, o, lse)


def _bias_expand(idx, rel_bias, name):
    tq, kw = idx.shape

    def body(idx_ref, rb_ref, o_ref):
        idx = idx_ref[...]
        for hd in range(DIL_HEADS):
            acc = jnp.full((tq, kw), NEG_INF, F32)
            for u in range(REL_BUCKETS):
                acc = jnp.where(idx == u, rb_ref[u, hd], acc)
            o_ref[hd] = acc

    return pl.pallas_call(
        body, name=name,
        in_specs=[pl.BlockSpec(memory_space=pltpu.VMEM), pl.BlockSpec(memory_space=pltpu.SMEM)],
        out_specs=pl.BlockSpec(memory_space=pltpu.VMEM),
        out_shape=_sds((DIL_HEADS, tq, kw), F32),
    )(idx, rel_bias)


def _bias_reduce(idx, dtab, name):
    tq, kw = idx.shape

    def body(idx_ref, d_ref, o_ref):
        idx = idx_ref[...]
        rowid = lax.broadcasted_iota(jnp.int32, (REL_BUCKETS, kw), 0)
        for hd in range(DIL_HEADS):
            d = d_ref[hd]
            acc = jnp.zeros((REL_BUCKETS, kw), F32)
            for u in range(REL_BUCKETS):
                r = jnp.sum(jnp.where(idx == u, d, 0.0), axis=0, keepdims=True)
                acc = jnp.where(rowid == u, r, acc)
            o_ref[hd] = jnp.sum(acc, axis=1, keepdims=True)

    return pl.pallas_call(
        body, name=name,
        in_specs=[pl.BlockSpec(memory_space=pltpu.VMEM)] * 2, out_specs=pl.BlockSpec(memory_space=pltpu.VMEM),
        out_shape=_sds((DIL_HEADS, REL_BUCKETS, 1), F32),
    )(idx, dtab)


def _dil_window(i, tq, kw, L):
    start = pl.multiple_of(i * tq, DIL_HALF)
    key = start + lax.broadcasted_iota(jnp.int32, (1, kw), 1) - DIL_HALF
    return start, (key >= 0) & (key < L)


def _dil_fwd(qv, kv, vv, tab, *, dil, L, tq, name):
    kw = tq + 2 * DIL_HALF
    npair = DIL_HEADS // 2

    def body(q_ref, k_ref, v_ref, t_ref, o_ref, lse_ref):
        masks = _pair_masks()
        start, valid = _dil_window(pl.program_id(2), tq, kw, L)
        kwin = k_ref[pl.ds(start, kw), :]
        vwin = v_ref[pl.ds(start, kw), :]
        outs, lses = [], []
        for hd in range(2):
            qh = jnp.where(masks[hd], q_ref[...], jnp.zeros_like(q_ref[...]))
            s = lax.dot_general(qh, kwin, _NT, preferred_element_type=F32) + t_ref[hd]
            s = jnp.where(valid, s, NEG_INF)
            mx = jnp.max(s, axis=-1, keepdims=True)
            p = jnp.exp(s - mx)
            l = jnp.sum(p, axis=-1, keepdims=True)
            outs.append(jnp.dot(p.astype(CDT), vwin, preferred_element_type=F32) / l)
            lses.append(jnp.broadcast_to(mx + jnp.log(l), (tq, LANES)))
        o_ref[...] = jnp.where(masks[0], outs[0], outs[1])
        lse_ref[...] = jnp.where(masks[0], lses[0], lses[1])

    blk = lambda p, c, i: (i, c * npair + p)
    res = lambda p, c, i: (0, c * npair + p)
    return pl.pallas_call(
        body, name=name, grid=(npair, dil, L // tq),
        in_specs=[pl.BlockSpec((tq, LANES), blk), pl.BlockSpec((L + 2 * DIL_HALF, LANES), res),
                  pl.BlockSpec((L + 2 * DIL_HALF, LANES), res), pl.BlockSpec((2, tq, kw), lambda p, c, i: (p, 0, 0))],
        out_specs=[pl.BlockSpec((tq, LANES), blk)] * 2,
        out_shape=[_sds(qv.shape, F32)] * 2,
        compiler_params=_cparams(3),
    )(qv, kv, vv, tab)


def _dil_bwd(qv, kv, vv, tab, dov, lsev, deltav, *, dil, L, tq, name):
    kw = tq + 2 * DIL_HALF
    npair = DIL_HEADS // 2

    def body(q_ref, k_ref, v_ref, t_ref, do_ref, lse_ref, dl_ref, dq_ref, dk_ref, dv_ref, dt_ref):
        masks = _pair_masks()
        c_id, i_id = pl.program_id(1), pl.program_id(2)
        start, valid = _dil_window(i_id, tq, kw, L)
        kwin = k_ref[pl.ds(start, kw), :]
        vwin = v_ref[pl.ds(start, kw), :]

        @pl.when(i_id == 0)
        def _():
            dk_ref[...] = jnp.zeros_like(dk_ref)
            dv_ref[...] = jnp.zeros_like(dv_ref)

        @pl.when((i_id == 0) & (c_id == 0))
        def _():
            dt_ref[...] = jnp.zeros_like(dt_ref)

        do = do_ref[...]
        dq = jnp.zeros((tq, LANES), F32)
        dk = jnp.zeros((kw, LANES), F32)
        dv = jnp.zeros((kw, LANES), F32)
        for hd in range(2):
            m = masks[hd]
            qh = jnp.where(m, q_ref[...], jnp.zeros_like(q_ref[...]))
            doh = jnp.where(m, do, jnp.zeros_like(do))
            s = lax.dot_general(qh, kwin, _NT, preferred_element_type=F32) + t_ref[hd]
            s = jnp.where(valid, s, NEG_INF)
            p = jnp.exp(s - _head_scalar(lse_ref[...], m))
            dp = lax.dot_general(doh, vwin, _NT, preferred_element_type=F32)
            ds = p * (dp - _head_scalar(dl_ref[...], m))
            dt_ref[hd] += ds
            dsb = ds.astype(CDT)
            dq = dq + jnp.where(m, jnp.dot(dsb, kwin, preferred_element_type=F32), 0.0)
            dk = dk + lax.dot_general(dsb, qh, _TN, preferred_element_type=F32)
            dv = dv + lax.dot_general(p.astype(CDT), doh, _TN, preferred_element_type=F32)
        dq_ref[...] = dq
        dk_ref[pl.ds(start, kw), :] += dk
        dv_ref[pl.ds(start, kw), :] += dv

    blk = lambda p, c, i: (i, c * npair + p)
    res = lambda p, c, i: (0, c * npair + p)
    tsp = pl.BlockSpec((2, tq, kw), lambda p, c, i: (p, 0, 0))
    Lp = L + 2 * DIL_HALF
    return pl.pallas_call(
        body, name=name, grid=(npair, dil, L // tq),
        in_specs=[pl.BlockSpec((tq, LANES), blk), pl.BlockSpec((Lp, LANES), res), pl.BlockSpec((Lp, LANES), res), tsp,
                  pl.BlockSpec((tq, LANES), blk), pl.BlockSpec((tq, LANES), blk), pl.BlockSpec((tq, LANES), blk)],
        out_specs=[pl.BlockSpec((tq, LANES), blk), pl.BlockSpec((Lp, LANES), res), pl.BlockSpec((Lp, LANES), res), tsp],
        out_shape=[_sds(qv.shape, F32), _sds(kv.shape, F32), _sds(kv.shape, F32), _sds(tab.shape, F32)],
        compiler_params=_cparams(3),
    )(qv, kv, vv, tab, dov, lsev, deltav)


def _mix_weights(l1, l2, l3):
    mx = jnp.maximum(jnp.maximum(l1, l2), l3)
    e1, e2, e3 = jnp.exp(l1 - mx), jnp.exp(l2 - mx), jnp.exp(l3 - mx)
    inv = 1.0 / (e1 + e2 + e3)
    return e1 * inv, e2 * inv, e3 * inv


def _dil_mix_fwd(os, ls):
    S, W = os[0].shape
    ts = min(256, S)

    def body(o1, o2, o3, l1, l2, l3, out):
        w1, w2, w3 = _mix_weights(l1[...], l2[...], l3[...])
        out[...] = (w1 * o1[...] + w2 * o2[...] + w3 * o3[...]).astype(CDT)

    return pl.pallas_call(
        body, name="dil_mix_fwd", grid=(S // ts,), in_specs=[_row(ts, W)] * 6, out_specs=_row(ts, W),
        out_shape=_sds((S, W), CDT), compiler_params=_cparams(1),
    )(*os, *ls)


def _dil_mix_bwd(dcat, os, ls, j384):
    S, W = os[0].shape
    ts = min(256, S)

    def body(do_ref, o1, o2, o3, l1, l2, l3, j_ref, d1, d2, d3, e1, e2, e3):
        ws = _mix_weights(l1[...], l2[...], l3[...])
        do = do_ref[...]
        o = ws[0] * o1[...] + ws[1] * o2[...] + ws[2] * o3[...]
        dot = _headsum(do * o, j_ref[...])
        for w, d_o, e_o in zip(ws, (d1, d2, d3), (e1, e2, e3)):
            d_o[...] = (w * do).astype(CDT)
            e_o[...] = w * dot

    return pl.pallas_call(
        body, name="dil_mix_bwd", grid=(S // ts,),
        in_specs=[_row(ts, W, 1)] + [_row(ts, W)] * 6 + [_full(j384.shape)],
        out_specs=[_row(ts, W)] * 6,
        out_shape=[_sds((S, W), CDT)] * 3 + [_sds((S, W), F32)] * 3,
        compiler_params=_cparams(1),
    )(dcat, *os, *ls, j384)


def _adamw_math(w, g, m, v):
    m = ADAM_B1 * m + (1.0 - ADAM_B1) * g
    v = ADAM_B2 * v + (1.0 - ADAM_B2) * (g * g)
    m_hat = m / (1.0 - ADAM_B1 ** ADAM_STEP)
    v_hat = v / (1.0 - ADAM_B2 ** ADAM_STEP)
    delta = -ADAM_LR * (m_hat / (jnp.sqrt(v_hat) + ADAM_EPS) + ADAM_WD * w)
    return delta, m, v


def _pick8(n, target):
    best = None
    for t in range(16, min(n, target) + 1, 16):
        if n % t == 0:
            best = t
    return best if best is not None else n


_ELEMS_PER_BLOCK = 256 * 1024


def _lead_spec(a, b):
    ta = _pick8(a, max(16, _ELEMS_PER_BLOCK // b))
    return ta, pl.BlockSpec((None, ta, b), lambda l, i: (l, i, 0))


def _adamw(w, g_mine, g_sib, m, v, name):
    L, a, b = w.shape
    ta, spec = _lead_spec(a, b)
    gspec = pl.BlockSpec((ta, b), lambda l, i: (i, 0))

    def body(w_ref, gm_ref, gs_ref, m_ref, v_ref, g_o, d_o, m_o, v_o):
        g = jnp.where(pl.program_id(0) == lax.axis_index("c"), gm_ref[...], gs_ref[...])
        d, mm, vv = _adamw_math(w_ref[...], g, m_ref[...], v_ref[...])
        g_o[...] = g
        d_o[...] = d
        m_o[...] = mm
        v_o[...] = vv

    return pl.pallas_call(
        body, name=name, grid=(L, a // ta), in_specs=[spec, gspec, gspec, spec, spec], out_specs=[spec] * 4,
        out_shape=[_sds(w.shape, F32)] * 4, compiler_params=_cparams(2),
    )(w, g_mine, g_sib, m, v)


def _adamw_small(w, gall, m, v):
    R = w.shape[0]

    def body(w_ref, g_ref, m_ref, v_ref, g_o, d_o, m_o, v_o):
        g = g_ref[0]
        for k in range(1, 8):
            g = g + g_ref[k]
        d, mm, vv = _adamw_math(w_ref[...], g, m_ref[...], v_ref[...])
        g_o[...] = g
        d_o[...] = d
        m_o[...] = mm
        v_o[...] = vv

    vm = pl.BlockSpec(memory_space=pltpu.VMEM)
    return pl.pallas_call(
        body, name="adamw_small", in_specs=[vm] * 4, out_specs=[vm] * 4, out_shape=[_sds((R, LANES), F32)] * 4,
    )(w, gall, m, v)


def _sum_pair(g0, g1, t, name):
    n, a, b = t.shape
    ta, spec = _lead_spec(a, b)

    def body(g0_ref, g1_ref, t_ref, o_ref):
        mine = jnp.where(lax.axis_index("c") == 0, g0_ref[...], g1_ref[...])
        o_ref[...] = (mine + t_ref[...]).astype(o_ref.dtype)

    return pl.pallas_call(
        body, name=name, grid=(n, a // ta), in_specs=[spec] * 3, out_specs=spec, out_shape=_sds(t.shape, WIRE),
        compiler_params=_cparams(2),
    )(g0, g1, t)


def _sum_chips(pair, t, name):
    _, a, b = t.shape
    ta = _pick8(a, max(16, _ELEMS_PER_BLOCK // b))

    def body(p_ref, t_ref, o_ref):
        me = 2 * lax.axis_index("x") + lax.axis_index("y")
        acc = p_ref[me].astype(F32)
        for k in range(3):
            acc = acc + t_ref[k].astype(F32)
        o_ref[...] = acc

    return pl.pallas_call(
        body, name=name, grid=(a // ta,),
        in_specs=[pl.BlockSpec((4, ta, b), lambda i: (0, i, 0)), pl.BlockSpec((3, ta, b), lambda i: (0, i, 0))],
        out_specs=pl.BlockSpec((ta, b), lambda i: (i, 0)), out_shape=_sds((a, b), F32), compiler_params=_cparams(1),
    )(pair, t)


_HBM = pl.BlockSpec(memory_space=pltpu.HBM)


def _place():
    x, y, c = lax.axis_index("x"), lax.axis_index("y"), lax.axis_index("c")
    chips = [(1 - x, y), (x, 1 - y), (1 - x, 1 - y)]
    return x, y, c, chips


def _remote(src, dst, ssem, rsem, to):
    return pltpu.make_async_remote_copy(src_ref=src, dst_ref=dst, send_sem=ssem, recv_sem=rsem, device_id=to,
                                        device_id_type=MESH_ID)


def _dma_sems(n):
    return pltpu.SemaphoreType.DMA((n,))


def _gather_weights(shards):
    n = len(shards)

    def body(*refs):
        w_refs, g_refs = refs[:n], refs[n:2 * n]
        ssem, rsem = refs[2 * n:]
        x, y, c, chips = _place()
        me = 2 * x + y
        sib = (x, y, 1 - c)
        owns = [_remote(w.at[l], g.at[l, me], ssem.at[6 * n + 2 * i + l], rsem.at[6 * n + 2 * i + l], sib)
                for i, (w, g) in enumerate(zip(w_refs, g_refs)) for l in range(2)]
        first = [_remote(w.at[c], g.at[c, me], ssem.at[3 * i + j], rsem.at[3 * i + j], (cx, cy, c))
                 for j, (cx, cy) in enumerate(chips) for i, (w, g) in enumerate(zip(w_refs, g_refs))]
        for cp in first + owns:
            cp.start()
        passed = []
        for j, (cx, cy) in enumerate(chips):
            for i, g in enumerate(g_refs):
                blk = g.at[c, 2 * cx + cy]
                _remote(blk, blk, ssem.at[3 * i + j], rsem.at[3 * i + j], sib).wait_recv()
                fw = _remote(blk, blk, ssem.at[3 * n + 3 * i + j], rsem.at[3 * n + 3 * i + j], sib)
                fw.start()
                passed.append(fw)
        for j, (cx, cy) in enumerate(chips):
            for i, g in enumerate(g_refs):
                blk = g.at[1 - c, 2 * cx + cy]
                _remote(blk, blk, ssem.at[3 * n + 3 * i + j], rsem.at[3 * n + 3 * i + j], sib).wait_recv()
        for cp in owns:
            cp.wait_recv()
        for cp in first + passed + owns:
            cp.wait_send()

    return pl.pallas_call(
        body, name="gather_weights", in_specs=[_HBM] * n, out_specs=[_HBM] * n,
        out_shape=[_sds((2, 4) + s.shape[1:], s.dtype) for s in shards],
        scratch_shapes=[_dma_sems(8 * n), _dma_sems(8 * n)],
    )(*shards)


def _sibling_halves(g0s, g1s):
    n = len(g0s)

    def body(*refs):
        g0_refs, g1_refs, t_refs = refs[:n], refs[n:2 * n], refs[2 * n:3 * n]
        ssem, rsem = refs[3 * n:]
        x, y, c, _ = _place()

        def swap(srcs):
            cps = [_remote(s, t, ssem.at[i], rsem.at[i], (x, y, 1 - c)) for i, (s, t) in enumerate(zip(srcs, t_refs))]
            for cp in cps:
                cp.start()
            for cp in cps:
                cp.wait()

        @pl.when(c == 0)
        def _():
            swap(g1_refs)

        @pl.when(c == 1)
        def _():
            swap(g0_refs)

    return pl.pallas_call(
        body, name="rs_sibling_halves", in_specs=[_HBM] * (2 * n), out_specs=[_HBM] * n,
        out_shape=[_sds(g.shape, g.dtype) for g in g0s], scratch_shapes=[_dma_sems(n), _dma_sems(n)],
    )(*g0s, *g1s)


def _chip_all_to_all(parts):
    n = len(parts)

    def body(*refs):
        a_refs, t_refs = refs[:n], refs[n:2 * n]
        ssem, rsem = refs[2 * n:]
        x, y, c, chips = _place()
        sends = [_remote(a.at[2 * cx + cy], t.at[j], ssem.at[3 * i + j], rsem.at[3 * i + j], (cx, cy, c))
                 for j, (cx, cy) in enumerate(chips) for i, (a, t) in enumerate(zip(a_refs, t_refs))]
        for cp in sends:
            cp.start()
        for cp in sends:
            cp.wait_recv()
        for cp in sends:
            cp.wait_send()

    return pl.pallas_call(
        body, name="rs_chip_all_to_all", in_specs=[_HBM] * n, out_specs=[_HBM] * n,
        out_shape=[_sds((3,) + p.shape[1:], p.dtype) for p in parts],
        scratch_shapes=[_dma_sems(3 * n), _dma_sems(3 * n)],
    )(*parts)


def _sibling_swap(reds):
    n = len(reds)

    def body(*refs):
        q_refs, o_refs = refs[:n], refs[n:2 * n]
        ssem, rsem = refs[2 * n:]
        x, y, c, _ = _place()
        cps = [_remote(q, o, ssem.at[i], rsem.at[i], (x, y, 1 - c)) for i, (q, o) in enumerate(zip(q_refs, o_refs))]
        for cp in cps:
            cp.start()
        for cp in cps:
            cp.wait()

    return pl.pallas_call(
        body, name="rs_sibling_swap", in_specs=[_HBM] * n, out_specs=[_HBM] * n,
        out_shape=[_sds(q.shape, q.dtype) for q in reds],
        scratch_shapes=[_dma_sems(n), _dma_sems(n)],
    )(*reds)


def _gather_small(s):
    R, _ = s.shape

    def body(s_ref, o_ref, ssem, rsem, lsem):
        x, y, c, _ = _place()
        me = 4 * x + 2 * y + c
        own = pltpu.make_async_copy(s_ref, o_ref.at[me], lsem)
        own.start()
        sends = []
        for k in range(1, 8):
            px, py, pc = x ^ (k >> 2), y ^ ((k >> 1) & 1), c ^ (k & 1)
            cp = _remote(s_ref, o_ref.at[me], ssem.at[k - 1], rsem.at[k - 1], (px, py, pc))
            cp.start()
            sends.append(cp)
        for k in range(1, 8):
            px, py, pc = x ^ (k >> 2), y ^ ((k >> 1) & 1), c ^ (k & 1)
            blk = o_ref.at[4 * px + 2 * py + pc]
            _remote(blk, blk, ssem.at[k - 1], rsem.at[k - 1], (px, py, pc)).wait_recv()
        for cp in sends:
            cp.wait_send()
        own.wait()

    vm = pl.BlockSpec(memory_space=pltpu.VMEM)
    return pl.pallas_call(
        body, name="gather_small", in_specs=[vm], out_specs=vm, out_shape=_sds((8, R, LANES), s.dtype),
        scratch_shapes=[pltpu.SemaphoreType.DMA((7,)), pltpu.SemaphoreType.DMA((7,)), pltpu.SemaphoreType.DMA],
    )(s)


_BIG = ("w_in", "mla_w_uq", "mla_w_ukv", "w_out", "ffn_w_gate", "ffn_w_up", "ffn_w_down")
_SMALL = ("mla_q_norm", "mla_kv_norm", "gqa_q_norm", "gqa_k_norm", "rel_bias", "ln1_g", "ln1_b", "ln2_g", "ln2_b")


def _pack_flat(arrs, align):
    flat = jnp.concatenate([a.reshape(-1) for a in arrs])
    pad = (-flat.shape[0]) % align
    return jnp.pad(flat, (0, pad)) if pad else flat


def _unpack_flat(flat, shapes):
    out, off = [], 0
    for s in shapes:
        n = int(np.prod(s))
        out.append(flat[off:off + n].reshape(s))
        off += n
    return out


def _perm_gqa_rows(w):
    return jnp.concatenate([w[:832], w[896:960], w[832:896], w[960:]], axis=0)


def _local_step(x, target, W, small):
    S, D = x.shape
    depth = W["w_in"].shape[0]
    alpha = (2.0 * depth) ** 0.25
    in_idx, uq_idx, ukv_idx = _in_cols(), _uq_cols(), _ukv_cols()
    win = [_cols_from_shards(W["w_in"][l], in_idx) for l in range(depth)]
    wuq = [_cols_from_shards(W["mla_w_uq"][l], uq_idx) for l in range(depth)]
    wukv = [_cols_from_shards(W["mla_w_ukv"][l], ukv_idx) for l in range(depth)]
    wout = [_perm_gqa_rows(W["w_out"][l].reshape(-1, D)) for l in range(depth)]
    wg, wu, wdn = W["ffn_w_gate"], W["ffn_w_up"], W["ffn_w_down"]

    tm, tg = _rope_tables(S)
    j256, j384 = _head_ones(256), _head_ones(384)
    mla_scale = (64 + MLA_ROPE_DIM) ** -0.5
    branches = []
    for (_, dil) in DIL_BRANCHES:
        L = S // dil
        tq = min(256, L)
        idx = jnp.asarray(_branch_bucket_idx(tq, dil))
        branches.append((dil, L, tq, idx))
    tabs = [_bias_expand(idx, small["rel_bias"], name=f"bias_expand_{b}") for b, (_, _, _, idx) in enumerate(branches)]

    def view(a, dil):
        return a.reshape(a.shape[0] // dil, dil * a.shape[1])

    def padded_view(a, dil):
        z = jnp.zeros((DIL_HALF * dil, a.shape[1]), a.dtype)
        return view(jnp.concatenate([z, a, z], axis=0), dil)

    def unpad_view(a, dil):
        return a[DIL_HALF:a.shape[0] - DIL_HALF].reshape(S, 384)

    saved = []
    xf, xb = x, x.astype(CDT)
    for l in range(depth):
        gq, gkv = small["mla_q_norm"][l][None], small["mla_kv_norm"][l][None]
        ggq = jnp.tile(small["gqa_q_norm"][l], 4)[None]
        ggk = jnp.tile(small["gqa_k_norm"][l], 2)[None]
        h = _mm(xb, win[l], name="mm_in")
        cq, ckv, kr, qd, kd, vd, qg, kg, vg = _prep_fwd(h, gq, gkv, ggq, ggk, tm, tg, j256)
        qa = _mm(cq, wuq[l], name="mm_uq")
        kvp = _mm(ckv, wukv[l], out_dtype=CDT, name="mm_ukv")
        qm, km = _mla_prep_fwd(qa, kvp, kr, tm, mla_scale)
        oa, lsa = _attn_fwd(qm, km, kvp, split=True, npairs=3, kblk=lambda p: p, vblk=lambda p: 6 + p,
                            name="mla_attn_fwd")
        oc, lsc = _attn_fwd(qg, kg, vg, split=False, npairs=2, kblk=lambda p: 0, vblk=lambda p: 0,
                            name="gqa_attn_fwd")
        obs, lbs, kvs = [], [], []
        for b, (dil, L, tq, _) in enumerate(branches):
            kpv, vpv = padded_view(kd, dil), padded_view(vd, dil)
            o_b, l_b = _dil_fwd(view(qd, dil), kpv, vpv, tabs[b], dil=dil, L=L, tq=tq, name=f"dil_fwd_{b}")
            obs.append(o_b.reshape(S, 384))
            lbs.append(l_b.reshape(S, 384))
            kvs.append((kpv, vpv))
        ob = _dil_mix_fwd(obs, lbs)
        cat = jnp.concatenate([oa, ob, oc], axis=1)
        mix = _mm(cat, wout[l], name="mm_out")
        x1, x1b, z1 = _ln_fwd(xf, mix, small["ln1_g"][l][None], small["ln1_b"][l][None], alpha, name="ln1_fwd")
        g3 = _mm(x1b, wg[l], gb=True, go=True, name="mm_gate")
        u3 = _mm(x1b, wu[l], gb=True, go=True, name="mm_up")
        act = _swiglu_fwd(g3, u3)
        ff = _mm(act, wdn[l], ga=True, gb=True, name="mm_down")
        x2, x2b, z2 = _ln_fwd(x1, ff, small["ln2_g"][l][None], small["ln2_b"][l][None], alpha, name="ln2_fwd")
        saved.append(dict(xb=xb, h=h, cq=cq, ckv=ckv, qd=qd, qg=qg, kg=kg, vg=vg, kvp=kvp, qm=qm, km=km, oa=oa, lsa=lsa,
                          oc=oc, lsc=lsc, obs=obs, lbs=lbs, kvs=kvs, cat=cat, z1=z1, x1b=x1b, g3=g3, u3=u3, act=act, z2=z2,
                          gq=gq, gkv=gkv, ggq=ggq, ggk=ggk))
        xf, xb = x2, x2b

    dy, loss = _loss_kernel(xf, target)

    gW = {k: [None] * depth for k in _BIG}
    gS = {k: [None] * depth for k in ("mla_q_norm", "mla_kv_norm", "gqa_q_norm", "gqa_k_norm", "ln1_g", "ln1_b", "ln2_g",
                                      "ln2_b")}
    g_rel = None
    dya, dyb = dy, None
    for l in reversed(range(depth)):
        sv = saved[l]
        dz2, dz2b, gS["ln2_g"][l], gS["ln2_b"][l] = _ln_bwd(dya, dyb, sv["z2"], small["ln2_g"][l][None], alpha,
                                                             name="ln2_bwd" if dyb is not None else "ln2_bwd_last")
        gW["ffn_w_down"][l] = _mm(sv["act"], dz2b, ta=True, ga=True, go=True, name="mm_down_dw")
        da3 = _mm(dz2b, wdn[l], tb=True, gb=True, go=True, name="mm_down_dx")
        dg3, du3 = _swiglu_bwd(da3, sv["g3"], sv["u3"])
        gW["ffn_w_gate"][l] = _mm(sv["x1b"], dg3, ta=True, gb=True, go=True, name="mm_gate_dw")
        gW["ffn_w_up"][l] = _mm(sv["x1b"], du3, ta=True, gb=True, go=True, name="mm_up_dw")
        dx1 = _mm(dg3, wg[l], tb=True, ga=True, gb=True, name="mm_gate_dx")
        dx1 = _mm(du3, wu[l], tb=True, ga=True, gb=True, add=dx1, name="mm_up_dx")
        dz1, dz1b, gS["ln1_g"][l], gS["ln1_b"][l] = _ln_bwd(dx1, dz2, sv["z1"], small["ln1_g"][l][None], alpha,
                                                             name="ln1_bwd")
        gW["w_out"][l] = _perm_gqa_rows(_mm(sv["cat"], dz1b, ta=True, name="mm_out_dw")).reshape(4, -1, D)
        dcat = _mm(dz1b, wout[l], tb=True, name="mm_out_dx")
        dqg, dkg, dvg = _attn_bwd(sv["qg"], sv["kg"], sv["vg"], dcat, sv["oc"], sv["lsc"], split=False, npairs=2,
                                  kblk=lambda p: 0, vblk=lambda p: 0, doblk=lambda p: 6 + p, shared_kv=True,
                                  name="gqa_attn_bwd")
        dqm, dkm, dvm = _attn_bwd(sv["qm"], sv["km"], sv["kvp"], dcat, sv["oa"], sv["lsa"], split=True, npairs=3,
                                  kblk=lambda p: p, vblk=lambda p: 6 + p, doblk=lambda p: p, shared_kv=False,
                                  name="mla_attn_bwd")
        dqa, dkvp, dkr = _mla_prep_bwd(dqm, dkm, dvm, tm, mla_scale)
        gW["mla_w_uq"][l] = _cols_to_shards(_mm(sv["cq"], dqa, ta=True, name="mm_uq_dw"), uq_idx, MLA_HEADS * 96)
        dcq = _mm(dqa, wuq[l], tb=True, name="mm_uq_dx")
        gW["mla_w_ukv"][l] = _cols_to_shards(_mm(sv["ckv"], dkvp, ta=True, name="mm_ukv_dw"), ukv_idx, MLA_HEADS * 128)
        dckv = _mm(dkvp, wukv[l], tb=True, name="mm_ukv_dx")
        mixb = _dil_mix_bwd(dcat, sv["obs"], sv["lbs"], j384)
        ddq, ddk, ddv = [], [], []
        for b, (dil, L, tq, idx) in enumerate(branches):
            kpv, vpv = sv["kvs"][b]
            dq_b, dk_b, dv_b, dtab = _dil_bwd(view(sv["qd"], dil), kpv, vpv, tabs[b], view(mixb[b], dil),
                                              view(sv["lbs"][b], dil), view(mixb[3 + b], dil), dil=dil, L=L, tq=tq,
                                              name=f"dil_bwd_{b}")
            ddq.append(dq_b.reshape(S, 384))
            ddk.append(unpad_view(dk_b, dil))
            ddv.append(unpad_view(dv_b, dil))
            g_b = _bias_reduce(idx, dtab, name=f"bias_reduce_{b}")[:, :, 0].T
            g_rel = g_b if g_rel is None else g_rel + g_b
        dh, n1, n2, n3, n4 = _prep_bwd(sv["h"], dcq, dckv, dkr, ddq, ddk, ddv, dqg, dkg, dvg, sv["gq"], sv["gkv"],
                                       sv["ggq"], sv["ggk"], tg, j256)
        gS["mla_q_norm"][l], gS["mla_kv_norm"][l] = n1[0], n2[0]
        gS["gqa_q_norm"][l] = n3[0].reshape(4, 64).sum(0)
        gS["gqa_k_norm"][l] = n4[0].reshape(2, 64).sum(0)
        gW["w_in"][l] = _cols_to_shards(_mm(sv["xb"], dh, ta=True, name="mm_in_dw"), in_idx, IN_W)
        dya = _mm(dh, win[l], tb=True, name="mm_in_dx")
        dyb = dz1
    grad_x = _axpy(dya, dyb, alpha, name="grad_x")

    gsmall = {k: jnp.stack([a.reshape(-1) for a in v]) for k, v in gS.items()}
    gsmall["rel_bias"] = g_rel
    return loss, grad_x, gW, gsmall


_ORDER = ("w_in", "mla_q_norm", "mla_kv_norm", "mla_w_uq", "mla_w_ukv", "gqa_q_norm", "gqa_k_norm", "rel_bias", "w_out",
          "ln1_g", "ln1_b", "ffn_w_gate", "ffn_w_up", "ffn_w_down", "ln2_g", "ln2_b")


def kernel(x, w_in, mla_q_norm, mla_kv_norm, mla_w_uq, mla_w_ukv, gqa_q_norm, gqa_k_norm, rel_bias, w_out, ln1_g, ln1_b, ffn_w_gate, ffn_w_up, ffn_w_down, ln2_g, ln2_b, loss_target, m_w_in, m_mla_q_norm, m_mla_kv_norm, m_mla_w_uq, m_mla_w_ukv, m_gqa_q_norm, m_gqa_k_norm, m_rel_bias, m_w_out, m_ln1_g, m_ln1_b, m_ffn_w_gate, m_ffn_w_up, m_ffn_w_down, m_ln2_g, m_ln2_b, v_w_in, v_mla_q_norm, v_mla_kv_norm, v_mla_w_uq, v_mla_w_ukv, v_gqa_q_norm, v_gqa_k_norm, v_rel_bias, v_w_out, v_ln1_g, v_ln1_b, v_ffn_w_gate, v_ffn_w_up, v_ffn_w_down, v_ln2_g, v_ln2_b):
    wts = dict(zip(_ORDER, (w_in, mla_q_norm, mla_kv_norm, mla_w_uq, mla_w_ukv, gqa_q_norm, gqa_k_norm, rel_bias, w_out,
                            ln1_g, ln1_b, ffn_w_gate, ffn_w_up, ffn_w_down, ln2_g, ln2_b)))
    mom = dict(zip(_ORDER, (m_w_in, m_mla_q_norm, m_mla_kv_norm, m_mla_w_uq, m_mla_w_ukv, m_gqa_q_norm, m_gqa_k_norm,
                            m_rel_bias, m_w_out, m_ln1_g, m_ln1_b, m_ffn_w_gate, m_ffn_w_up, m_ffn_w_down, m_ln2_g,
                            m_ln2_b)))
    var = dict(zip(_ORDER, (v_w_in, v_mla_q_norm, v_mla_kv_norm, v_mla_w_uq, v_mla_w_ukv, v_gqa_q_norm, v_gqa_k_norm,
                            v_rel_bias, v_w_out, v_ln1_g, v_ln1_b, v_ffn_w_gate, v_ffn_w_up, v_ffn_w_down, v_ln2_g,
                            v_ln2_b)))
    small_shapes = [wts[n].shape for n in _SMALL]

    gathered = _gather_weights([wts[n].astype(WIRE) for n in _BIG])
    full = {n: g.astype(CDT) for n, g in zip(_BIG, gathered)}

    small = {n: wts[n] for n in _SMALL}
    loss, grad_x, gbig, gsmall = _local_step(x[0], loss_target[0], full, small)

    g0s, g1s = [gbig[n][0] for n in _BIG], [gbig[n][1] for n in _BIG]
    theirs = _sibling_halves(g0s, g1s)
    pairs = [_sum_pair(g0, g1, t, name=f"rs_pair_sum_{n}") for n, g0, g1, t in zip(_BIG, g0s, g1s, theirs)]
    reds = [_sum_chips(p, t, name=f"rs_sum_chips_{n}") for n, p, t in zip(_BIG, pairs, _chip_all_to_all(pairs))]
    sibs = _sibling_swap(reds)

    sflat = _pack_flat([gsmall[n].reshape(-1) for n in _SMALL], 8 * LANES)
    rs = sflat.shape[0] // LANES
    sall = _gather_small(sflat.reshape(rs, LANES))

    def packed(d):
        return _pack_flat([d[n] for n in _SMALL], 8 * LANES).reshape(rs, LANES)

    outs = {tag: {} for tag in ("grad", "delta", "new_m", "new_v")}
    for n, red, sib in zip(_BIG, reds, sibs):
        outs["grad"][n], outs["delta"][n], outs["new_m"][n], outs["new_v"][n] = _adamw(
            wts[n], red, sib, mom[n], var[n], name=f"adamw_{n}")
    for tag, smallflat in zip(("grad", "delta", "new_m", "new_v"), _adamw_small(packed(wts), sall, packed(mom), packed(var))):
        outs[tag].update(zip(_SMALL, _unpack_flat(smallflat.reshape(-1), small_shapes)))

    total = lax.psum(loss[0, 0], ("x", "y", "c"))
    return (total, grad_x[None], *[outs["grad"][n] for n in _ORDER], *[outs["delta"][n] for n in _ORDER],
            *[outs["new_m"][n] for n in _ORDER], *[outs["new_v"][n] for n in _ORDER])
```

```python
import functools
import math

import numpy as np
import jax
import jax.numpy as jnp
from jax import lax
from jax.experimental import pallas as pl
from jax.experimental.pallas import tpu as pltpu

F32 = jnp.float32
CDT = jnp.bfloat16
WIRE = jnp.bfloat16

HEAD_DIM = 64
GRID_W = 64
ROPE_THETA = 10000.0
MLA_HEADS = 6
MLA_Q_RANK = 256
MLA_KV_RANK = 128
MLA_ROPE_DIM = 32
DIL_HEADS = 6
DIL_BRANCHES = ((128, 1), (512, 4), (2048, 16))
DIL_HALF = 64
GQA_Q_HEADS = 4
REL_BUCKETS = 32
REL_MAX_DIST = 1024
NEG_INF = -1e30
LANES = 128
VMEM_LIMIT = 56 * 1024 * 1024

ADAM_LR, ADAM_B1, ADAM_B2, ADAM_EPS, ADAM_WD, ADAM_STEP = 0.001, 0.9, 0.999, 1e-08, 0.01, 10

C_CQ, C_CKV, C_KR, C_DQ, C_DK, C_DV, C_GQ, C_GK, C_GV, IN_P = 0, 256, 384, 512, 896, 1280, 1664, 1920, 2048, 2176
IN_W = 2080
MESH_ID = pl.DeviceIdType.MESH


def _cparams(n_axes, vmem=VMEM_LIMIT):
    return pltpu.CompilerParams(dimension_semantics=("arbitrary",) * n_axes, vmem_limit_bytes=vmem)


MAX_WHOLE_DIM = 2304


def _pick(n, target):
    best = None
    for t in range(LANES, min(n, target) + 1, LANES):
        if n % t == 0:
            best = t
    if best is not None and (2 * best >= target or n > MAX_WHOLE_DIM):
        return best
    return n


def _sds(shape, dtype):
    return jax.ShapeDtypeStruct(tuple(shape), dtype)


def _in_cols():
    idx = -np.ones((IN_P,), np.int64)
    idx[C_CQ:C_CQ + 256] = np.arange(0, 256)
    idx[C_CKV:C_CKV + 128] = np.arange(256, 384)
    idx[C_KR + 64:C_KR + 96] = np.arange(384, 416)
    idx[C_DQ:C_DQ + 1152] = np.arange(416, 1568)
    gq = 1568 + (np.array([0, 2, 1, 3])[:, None] * 64 + np.arange(64)[None, :]).reshape(-1)
    idx[C_GQ:C_GQ + 256] = gq
    idx[C_GK:C_GK + 256] = np.arange(1824, 2080)
    return idx


def _uq_cols():
    idx = -np.ones((MLA_HEADS * 128,), np.int64)
    for h in range(MLA_HEADS):
        idx[h * 128:h * 128 + 96] = np.arange(96 * h, 96 * h + 96)
    return idx


def _ukv_cols():
    idx = -np.ones((MLA_HEADS * 128 + MLA_HEADS * 64,), np.int64)
    for h in range(MLA_HEADS):
        idx[h * 128:h * 128 + 64] = np.arange(128 * h, 128 * h + 64)
        idx[768 + h * 64:768 + h * 64 + 64] = np.arange(128 * h + 64, 128 * h + 128)
    return idx


def _out_rows():
    idx = np.arange(1024)
    idx[768:1024] = 768 + (np.array([0, 2, 1, 3])[:, None] * 64 + np.arange(64)[None, :]).reshape(-1)
    return idx


def _runs(idx):
    out, i = [], 0
    while i < len(idx):
        j = i + 1
        while j < len(idx) and ((idx[i] < 0 and idx[j] < 0) or (idx[i] >= 0 and idx[j] == idx[j - 1] + 1)):
            j += 1
        out.append((int(idx[i]), j - i))
        i = j
    return out


def _cols_from_shards(sh, idx):
    _, r, cs = sh.shape
    pieces = []
    for first, ln in _runs(idx):
        if first < 0:
            pieces.append(jnp.zeros((r, ln), sh.dtype))
            continue
        while ln > 0:
            k, off = divmod(first, cs)
            take = min(ln, cs - off)
            pieces.append(sh[k, :, off:off + take])
            first, ln = first + take, ln - take
    return jnp.concatenate(pieces, axis=1)


def _cols_to_shards(wp, idx, n):
    inv = np.zeros((n,), np.int64)
    pos = np.nonzero(idx >= 0)[0]
    inv[idx[pos]] = pos
    cs = n // 4
    shards = []
    for k in range(4):
        pieces = [wp[:, first:first + ln] for first, ln in _runs(inv[k * cs:(k + 1) * cs])]
        shards.append(jnp.concatenate(pieces, axis=1))
    return jnp.stack(shards)


def _t5_bucket_np(rel):
    nb = REL_BUCKETS // 2
    exact = nb // 2
    ret = np.where(rel > 0, nb, 0)
    n = np.abs(rel)
    nf = np.maximum(n, 1).astype(np.float32)
    large = exact + (np.log(nf / np.float32(exact)) / np.float32(math.log(REL_MAX_DIST / exact))
                     * np.float32(nb - exact)).astype(np.int32)
    large = np.minimum(large, nb - 1)
    return ret + np.where(n < exact, n, large)


def _branch_bucket_idx(tq, dil):
    kw = tq + 2 * DIL_HALF
    rel = np.arange(kw)[None, :] - DIL_HALF - np.arange(tq)[:, None]
    idx = _t5_bucket_np(rel * dil)
    return np.where(np.abs(rel) <= DIL_HALF, idx, -1).astype(np.int32)


def _rope_tables(S):
    inv = ROPE_THETA ** (-jnp.arange(0, 32, 2, dtype=F32) / 32)
    t = jnp.arange(S)
    pos = t.astype(F32)
    row = (t // GRID_W).astype(F32)
    col = (t % GRID_W).astype(F32)
    lane = np.arange(LANES)
    wm = lane - 64
    is_rope = (lane >= 64) & (lane < 96)
    ang = pos[:, None] * inv[np.where(is_rope, wm % 16, 0)][None, :]
    cm = jnp.where(is_rope[None], jnp.cos(ang), 1.0)
    smm = jnp.where((is_rope & (wm < 16))[None], -jnp.sin(ang), 0.0)
    spm = jnp.where((is_rope & (wm >= 16))[None], jnp.sin(ang), 0.0)
    g = lane % 64
    w = g % 32
    angg = jnp.where((g < 32)[None], row[:, None], col[:, None]) * inv[w % 16][None, :]
    cg = jnp.cos(angg)
    smg = jnp.where((w < 16)[None], -jnp.sin(angg), 0.0)
    spg = jnp.where((w >= 16)[None], jnp.sin(angg), 0.0)
    return (cm, smm, spm), (cg, smg, spg)


def _lanes(t, width):
    return t if width == LANES else jnp.concatenate([t] * (width // LANES), axis=1)


def _rope(x, tabs):
    c, sm, sp = (_lanes(t, x.shape[1]) for t in tabs)
    w = x.shape[1]
    return x * c + pltpu.roll(x, w - 16, 1) * sm + pltpu.roll(x, 16, 1) * sp


def _rope_t(dy, tabs):
    c, sm, sp = (_lanes(t, dy.shape[1]) for t in tabs)
    w = dy.shape[1]
    return dy * c + pltpu.roll(dy * sm, 16, 1) + pltpu.roll(dy * sp, w - 16, 1)


def _head_ones(width):
    i = np.arange(width)
    return jnp.asarray((i[:, None] // HEAD_DIM == i[None, :] // HEAD_DIM).astype(np.float32))


def _headsum(x, j):
    return jnp.dot(x, j, preferred_element_type=F32, precision=lax.Precision.HIGHEST)


def _mm(a, b, *, ta=False, tb=False, ga=False, gb=False, go=False, out_dtype=F32, add=None, name):
    G = a.shape[0] if ga else (b.shape[0] if gb else 1)
    a2 = a.shape[1:] if ga else a.shape
    b2 = b.shape[1:] if gb else b.shape
    K, M = a2 if ta else a2[::-1]
    N = b2[0] if tb else b2[1]
    assert (b2[1] if tb else b2[0]) == K
    tm, tn, tk = _pick(M, 1024), _pick(N, 1024), _pick(K, 1024)
    if tm * tn > 1024 * 1152:
        tm = _pick(M, 512)
    nk = K // tk
    steps = nk if (go or G == 1) else G * nk
    dn = (((0 if ta else 1,), (1 if tb else 0,)), ((), ()))

    def body(a_ref, b_ref, *rest):
        rest = list(rest)
        add_ref = rest.pop(0) if add is not None else None
        o_ref = rest.pop(0)
        part = lax.dot_general(a_ref[...], b_ref[...], dn, preferred_element_type=F32)
        if steps == 1:
            if add_ref is not None:
                part = part + add_ref[...]
            o_ref[...] = part.astype(o_ref.dtype)
            return
        acc_ref, = rest
        s = pl.program_id(3)

        @pl.when(s == 0)
        def _():
            acc_ref[...] = part if add_ref is None else part + add_ref[...]

        @pl.when(s > 0)
        def _():
            acc_ref[...] += part

        @pl.when(s == steps - 1)
        def _():
            o_ref[...] = acc_ref[...].astype(o_ref.dtype)

    def grp(g, s):
        return g if go else s // nk

    def kk(s):
        return s if steps == nk else s % nk

    def spec(grouped, block, index):
        if grouped:
            return pl.BlockSpec((None,) + block, lambda g, i, j, s: (grp(g, s),) + index(i, j, s))
        return pl.BlockSpec(block, lambda g, i, j, s: index(i, j, s))

    a_spec = (spec(ga, (tk, tm), lambda i, j, s: (kk(s), i)) if ta else spec(ga, (tm, tk), lambda i, j, s: (i, kk(s))))
    b_spec = (spec(gb, (tn, tk), lambda i, j, s: (j, kk(s))) if tb else spec(gb, (tk, tn), lambda i, j, s: (kk(s), j)))
    o_spec = spec(go, (tm, tn), lambda i, j, s: (i, j))
    return pl.pallas_call(
        body, name=name, grid=(G if go else 1, M // tm, N // tn, steps),
        in_specs=[a_spec, b_spec] + ([o_spec] if add is not None else []), out_specs=o_spec,
        out_shape=_sds(((G,) if go else ()) + (M, N), out_dtype),
        scratch_shapes=[pltpu.VMEM((tm, tn), F32)] if steps > 1 else [],
        compiler_params=_cparams(4),
    )(*([a, b] + ([add] if add is not None else [])))


def _row(ts, w, cb=0):
    return pl.BlockSpec((ts, w), lambda i: (i, cb))


def _full(shape):
    nd = len(shape)
    return pl.BlockSpec(tuple(shape), lambda i: (0,) * nd)


def _rms_fwd(x, g, eps=1e-6):
    r = lax.rsqrt(jnp.mean(x * x, axis=-1, keepdims=True) + eps)
    return x * r * g


def _rms_bwd(x, g, dy, eps=1e-6):
    r = lax.rsqrt(jnp.mean(x * x, axis=-1, keepdims=True) + eps)
    gdy = g * dy
    dx = r * gdy - x * (r * r * r) * jnp.mean(x * gdy, axis=-1, keepdims=True)
    return dx, x * r * dy


def _rms_head_fwd(x, g, j, eps=1e-6):
    r = lax.rsqrt(_headsum(x * x, j) * (1.0 / HEAD_DIM) + eps)
    return x * r * g


def _rms_head_bwd(x, g, dy, j, eps=1e-6):
    r = lax.rsqrt(_headsum(x * x, j) * (1.0 / HEAD_DIM) + eps)
    gdy = g * dy
    dx = r * gdy - x * (r * r * r) * (_headsum(x * gdy, j) * (1.0 / HEAD_DIM))
    return dx, x * r * dy


def _prep_fwd(h, gq, gkv, ggq, ggk, tm, tg, j256):
    S = h.shape[0]
    ts = min(256, S)
    scale = HEAD_DIM ** -0.5

    def body(h_ref, gq_ref, gkv_ref, ggq_ref, ggk_ref, cm, smm, spm, cg, smg, spg, j_ref,
             cq_o, ckv_o, kr_o, dq_o, dk_o, dv_o, gq_o, gk_o, gv_o):
        tabm = (cm[...], smm[...], spm[...])
        tabg = (cg[...], smg[...], spg[...])
        cq_o[...] = _rms_fwd(h_ref[:, C_CQ:C_CQ + 256], gq_ref[...]).astype(CDT)
        ckv_o[...] = _rms_fwd(h_ref[:, C_CKV:C_CKV + 128], gkv_ref[...]).astype(CDT)
        kr_o[...] = _rope(h_ref[:, C_KR:C_KR + 128], tabm).astype(CDT)
        dq_o[...] = (h_ref[:, C_DQ:C_DQ + 384] * scale).astype(CDT)
        dk_o[...] = h_ref[:, C_DK:C_DK + 384].astype(CDT)
        dv_o[...] = h_ref[:, C_DV:C_DV + 384].astype(CDT)
        qn = _rms_head_fwd(h_ref[:, C_GQ:C_GQ + 256], ggq_ref[...], j_ref[...])
        gq_o[...] = (_rope(qn, tabg) * scale).astype(CDT)
        kn = _rms_head_fwd(h_ref[:, C_GK:C_GK + 128], ggk_ref[...], j_ref[0:128, 0:128])
        gk_o[...] = _rope(kn, tabg).astype(CDT)
        gv_o[...] = h_ref[:, C_GV:C_GV + 128].astype(CDT)

    widths = (256, 128, 128, 384, 384, 384, 256, 128, 128)
    return pl.pallas_call(
        body, name="prep_fwd", grid=(S // ts,),
        in_specs=[_row(ts, IN_P), _full(gq.shape), _full(gkv.shape), _full(ggq.shape), _full(ggk.shape)]
        + [_row(ts, LANES)] * 6 + [_full(j256.shape)],
        out_specs=[_row(ts, w) for w in widths],
        out_shape=[_sds((S, w), CDT) for w in widths],
        compiler_params=_cparams(1),
    )(h, gq, gkv, ggq, ggk, *tm, *tg, j256)


def _prep_bwd(h, dcq, dckv, dkr, ddq, ddk, ddv, dgq, dgk, dgv, gq, gkv, ggq, ggk, tg, j256):
    S = h.shape[0]
    ts = min(256, S)
    scale = HEAD_DIM ** -0.5

    def body(h_ref, dcq_r, dckv_r, dkr_r, q1, q2, q3, k1, k2, k3, v1, v2, v3, dgq_r, dgk_r, dgv_r,
             gq_ref, gkv_ref, ggq_ref, ggk_ref, cg, smg, spg, j_ref,
             dh_o, ngq_o, ngkv_o, nggq_o, nggk_o):
        tabg = (cg[...], smg[...], spg[...])
        first = pl.program_id(0) == 0

        def acc(o_ref, val):
            s = jnp.sum(val, axis=0, keepdims=True)

            @pl.when(first)
            def _():
                o_ref[...] = s

            @pl.when(jnp.logical_not(first))
            def _():
                o_ref[...] += s

        dx, dg = _rms_bwd(h_ref[:, C_CQ:C_CQ + 256], gq_ref[...], dcq_r[...])
        dh_o[:, C_CQ:C_CQ + 256] = dx.astype(CDT)
        acc(ngq_o, dg)
        dx, dg = _rms_bwd(h_ref[:, C_CKV:C_CKV + 128], gkv_ref[...], dckv_r[...])
        dh_o[:, C_CKV:C_CKV + 128] = dx.astype(CDT)
        acc(ngkv_o, dg)
        dh_o[:, C_KR:C_KR + 128] = dkr_r[...].astype(CDT)
        dh_o[:, C_DQ:C_DQ + 384] = ((q1[...] + q2[...] + q3[...]) * scale).astype(CDT)
        dh_o[:, C_DK:C_DK + 384] = (k1[...] + k2[...] + k3[...]).astype(CDT)
        dh_o[:, C_DV:C_DV + 384] = (v1[...] + v2[...] + v3[...]).astype(CDT)
        dqn = _rope_t(dgq_r[...] * scale, tabg)
        dx, dg = _rms_head_bwd(h_ref[:, C_GQ:C_GQ + 256], ggq_ref[...], dqn, j_ref[...])
        dh_o[:, C_GQ:C_GQ + 256] = dx.astype(CDT)
        acc(nggq_o, dg)
        dkn = _rope_t(dgk_r[...], tabg)
        dx, dg = _rms_head_bwd(h_ref[:, C_GK:C_GK + 128], ggk_ref[...], dkn, j_ref[0:128, 0:128])
        dh_o[:, C_GK:C_GK + 128] = dx.astype(CDT)
        acc(nggk_o, dg)
        dh_o[:, C_GV:C_GV + 128] = dgv_r[...].astype(CDT)

    return pl.pallas_call(
        body, name="prep_bwd", grid=(S // ts,),
        in_specs=[_row(ts, IN_P), _row(ts, 256), _row(ts, 128), _row(ts, 128)] + [_row(ts, 384)] * 9
        + [_row(ts, 256), _row(ts, 128), _row(ts, 128)]
        + [_full(gq.shape), _full(gkv.shape), _full(ggq.shape), _full(ggk.shape)] + [_row(ts, LANES)] * 3
        + [_full(j256.shape)],
        out_specs=[_row(ts, IN_P), _full((1, 256)), _full((1, 128)), _full((1, 256)), _full((1, 128))],
        out_shape=[_sds((S, IN_P), CDT), _sds((1, 256), F32), _sds((1, 128), F32), _sds((1, 256), F32),
                   _sds((1, 128), F32)],
        compiler_params=_cparams(1),
    )(h, dcq, dckv, dkr, *ddq, *ddk, *ddv, dgq, dgk, dgv, gq, gkv, ggq, ggk, *tg, j256)


def _mla_prep_fwd(qa, kvp, kr, tm, scale):
    S = qa.shape[0]
    ts = min(256, S)

    def body(qa_ref, kv_ref, kr_ref, cm, smm, spm, q_o, k_o):
        tabm = (cm[...], smm[...], spm[...])
        q_o[...] = (_rope(qa_ref[...], tabm) * scale).astype(CDT)
        k_o[...] = kv_ref[:, 0:768] + _lanes(kr_ref[...], 768)

    return pl.pallas_call(
        body, name="mla_prep_fwd", grid=(S // ts,),
        in_specs=[_row(ts, 768), _row(ts, 1152), _row(ts, 128)] + [_row(ts, LANES)] * 3,
        out_specs=[_row(ts, 768)] * 2, out_shape=[_sds((S, 768), CDT)] * 2,
        compiler_params=_cparams(1),
    )(qa, kvp, kr, *tm)


def _mla_prep_bwd(dq, dk, dv, tm, scale):
    S = dq.shape[0]
    ts = min(256, S)

    def body(dq_ref, dk_ref, dv_ref, cm, smm, spm, dqa_o, dkv_o, dkr_o):
        tabm = (cm[...], smm[...], spm[...])
        lane = lax.broadcasted_iota(jnp.int32, (1, LANES), 1)
        dqa_o[...] = _rope_t(dq_ref[...] * scale, tabm).astype(CDT)
        dkr = jnp.zeros((ts, LANES), F32)
        for hd in range(MLA_HEADS):
            blk = dk_ref[:, hd * 128:(hd + 1) * 128]
            dkv_o[:, hd * 128:(hd + 1) * 128] = jnp.where(lane < 64, blk, 0.0).astype(CDT)
            dkr = dkr + jnp.where((lane >= 64) & (lane < 96), blk, 0.0)
        dkv_o[:, 768:1152] = dv_ref[...].astype(CDT)
        dkr_o[...] = jnp.where((lane >= 64) & (lane < 96), _rope_t(dkr, tabm), 0.0)

    return pl.pallas_call(
        body, name="mla_prep_bwd", grid=(S // ts,),
        in_specs=[_row(ts, 768), _row(ts, 768), _row(ts, 384)] + [_row(ts, LANES)] * 3,
        out_specs=[_row(ts, 768), _row(ts, 1152), _row(ts, 128)],
        out_shape=[_sds((S, 768), CDT), _sds((S, 1152), CDT), _sds((S, 128), F32)],
        compiler_params=_cparams(1),
    )(dq, dk, dv, *tm)


def _ln_fwd(xa, xb, g, b, alpha, name):
    S, D = xa.shape
    ts = min(256, S)

    def body(xa_ref, xb_ref, g_ref, b_ref, y_o, yb_o, z_o):
        z = alpha * xa_ref[...] + xb_ref[...]
        mu = jnp.mean(z, axis=-1, keepdims=True)
        zc = z - mu
        var = jnp.mean(zc * zc, axis=-1, keepdims=True)
        y = zc * lax.rsqrt(var + 1e-5) * g_ref[...] + b_ref[...]
        y_o[...] = y
        yb_o[...] = y.astype(CDT)
        z_o[...] = z

    return pl.pallas_call(
        body, name=name, grid=(S // ts,),
        in_specs=[_row(ts, D), _row(ts, D), _full(g.shape), _full(b.shape)],
        out_specs=[_row(ts, D)] * 3, out_shape=[_sds((S, D), F32), _sds((S, D), CDT), _sds((S, D), F32)],
        compiler_params=_cparams(1),
    )(xa, xb, g, b)


def _ln_bwd(dya, dyb, z, g, alpha, name):
    S, D = z.shape
    ts = min(256, S)
    two = dyb is not None

    def body(*refs):
        if two:
            dya_ref, dyb_ref, z_ref, g_ref, dz_o, dzb_o, dg_o, db_o = refs
            dy = dya_ref[...] + alpha * dyb_ref[...]
        else:
            dya_ref, z_ref, g_ref, dz_o, dzb_o, dg_o, db_o = refs
            dy = dya_ref[...]
        z = z_ref[...]
        mu = jnp.mean(z, axis=-1, keepdims=True)
        zc = z - mu
        r = lax.rsqrt(jnp.mean(zc * zc, axis=-1, keepdims=True) + 1e-5)
        xh = zc * r
        dxh = dy * g_ref[...]
        dz = r * (dxh - jnp.mean(dxh, axis=-1, keepdims=True) - xh * jnp.mean(dxh * xh, axis=-1, keepdims=True))
        dz_o[...] = dz
        dzb_o[...] = dz.astype(CDT)
        sg = jnp.sum(dy * xh, axis=0, keepdims=True)
        sb = jnp.sum(dy, axis=0, keepdims=True)
        first = pl.program_id(0) == 0

        @pl.when(first)
        def _():
            dg_o[...] = sg
            db_o[...] = sb

        @pl.when(jnp.logical_not(first))
        def _():
            dg_o[...] += sg
            db_o[...] += sb

    ins = [dya] + ([dyb] if two else []) + [z, g]
    return pl.pallas_call(
        body, name=name, grid=(S // ts,),
        in_specs=[_row(ts, D)] * (3 if two else 2) + [_full(g.shape)],
        out_specs=[_row(ts, D), _row(ts, D), _full((1, D)), _full((1, D))],
        out_shape=[_sds((S, D), F32), _sds((S, D), CDT), _sds((1, D), F32), _sds((1, D), F32)],
        compiler_params=_cparams(1),
    )(*ins)


def _grp_spec(ts, w):
    return pl.BlockSpec((None, ts, w), lambda k, i: (k, i, 0))


def _ffn_up(xb, wg3, wu3):
    S, D = xb.shape
    G, _, Fc = wg3.shape
    tm = _pick(S, 1024)
    wspec = pl.BlockSpec((None, D, Fc), lambda k, i: (k, 0, 0))

    def body(x_ref, wg_ref, wu_ref, g_o, u_o, a_o):
        x = x_ref[...]
        g = jnp.dot(x, wg_ref[...], preferred_element_type=F32)
        u = jnp.dot(x, wu_ref[...], preferred_element_type=F32)
        g_o[...] = g
        u_o[...] = u
        a_o[...] = (g / (1.0 + jnp.exp(-g)) * u).astype(CDT)

    return pl.pallas_call(
        body, name="ffn_up", grid=(G, S // tm),
        in_specs=[pl.BlockSpec((tm, D), lambda k, i: (i, 0)), wspec, wspec], out_specs=[_grp_spec(tm, Fc)] * 3,
        out_shape=[_sds((G, S, Fc), F32), _sds((G, S, Fc), F32), _sds((G, S, Fc), CDT)], compiler_params=_cparams(2),
    )(xb, wg3, wu3)


def _ffn_up_dx(dg3, du3, wg3, wu3):
    G, S, Fc = dg3.shape
    D = wg3.shape[1]
    tm = _pick(S, 1024)
    wspec = pl.BlockSpec((None, D, Fc), lambda i, k: (k, 0, 0))
    aspec = pl.BlockSpec((None, tm, Fc), lambda i, k: (k, i, 0))

    def body(dg_ref, du_ref, wg_ref, wu_ref, o_ref):
        part = (lax.dot_general(dg_ref[...], wg_ref[...], _NT, preferred_element_type=F32)
                + lax.dot_general(du_ref[...], wu_ref[...], _NT, preferred_element_type=F32))
        k = pl.program_id(1)

        @pl.when(k == 0)
        def _():
            o_ref[...] = part

        @pl.when(k > 0)
        def _():
            o_ref[...] += part

    return pl.pallas_call(
        body, name="ffn_up_dx", grid=(S // tm, G), in_specs=[aspec, aspec, wspec, wspec],
        out_specs=pl.BlockSpec((tm, D), lambda i, k: (i, 0)), out_shape=_sds((S, D), F32), compiler_params=_cparams(2),
    )(dg3, du3, wg3, wu3)


def _ffn_down_dx(dzb, wd3, g3, u3):
    S, D = dzb.shape
    G, Fc, _ = wd3.shape
    tm = _pick(S, 1024)

    def body(dz_ref, wd_ref, g_ref, u_ref, dg_o, du_o):
        da = lax.dot_general(dz_ref[...], wd_ref[...], _NT, preferred_element_type=F32)
        g = g_ref[...]
        sg = 1.0 / (1.0 + jnp.exp(-g))
        dg_o[...] = (da * u_ref[...] * (sg * (1.0 + g * (1.0 - sg)))).astype(CDT)
        du_o[...] = (da * (g * sg)).astype(CDT)

    return pl.pallas_call(
        body, name="ffn_down_dx", grid=(G, S // tm),
        in_specs=[pl.BlockSpec((tm, D), lambda k, i: (i, 0)), pl.BlockSpec((None, Fc, D), lambda k, i: (k, 0, 0)),
                  _grp_spec(tm, Fc), _grp_spec(tm, Fc)],
        out_specs=[_grp_spec(tm, Fc)] * 2, out_shape=[_sds((G, S, Fc), CDT)] * 2, compiler_params=_cparams(2),
    )(dzb, wd3, g3, u3)


def _loss_kernel(y, target):
    S, D = y.shape
    ts = min(256, S)

    def body(y_ref, t_ref, dy_o, loss_o):
        e = y_ref[...] - t_ref[...]
        dy_o[...] = e * (1.0 / D)
        part = jnp.sum(jnp.sum(e * e, axis=1, keepdims=True), axis=0, keepdims=True) * (0.5 / D)
        first = pl.program_id(0) == 0

        @pl.when(first)
        def _():
            loss_o[...] = part

        @pl.when(jnp.logical_not(first))
        def _():
            loss_o[...] += part

    return pl.pallas_call(
        body, name="loss", grid=(S // ts,), in_specs=[_row(ts, D)] * 2,
        out_specs=[_row(ts, D), _full((1, 1))], out_shape=[_sds((S, D), F32), _sds((1, 1), F32)],
        compiler_params=_cparams(1),
    )(y, target)


def _axpy(a, b, alpha, name):
    S, D = a.shape
    ts = min(256, S)

    def body(a_ref, b_ref, o_ref):
        o_ref[...] = a_ref[...] + alpha * b_ref[...]

    return pl.pallas_call(
        body, name=name, grid=(S // ts,), in_specs=[_row(ts, D)] * 2, out_specs=_row(ts, D),
        out_shape=_sds((S, D), F32), compiler_params=_cparams(1),
    )(a, b)


def _pair_masks():
    lane = lax.broadcasted_iota(jnp.int32, (1, LANES), 1)
    first = lane < HEAD_DIM
    return first, jnp.logical_not(first)


def _head_scalar(x, m):
    return jnp.max(jnp.where(m, x, -jnp.inf), axis=-1, keepdims=True)


_NT = (((1,), (1,)), ((), ()))
_TN = (((0,), (0,)), ((), ()))


def _attn_fwd(q, k, v, *, split, npairs, kblk, vblk, name):
    S = q.shape[0]
    qw = 256 if split else LANES
    tq = min(256, S)

    def body(q_ref, k_ref, v_ref, o_ref, lse_ref):
        masks = _pair_masks()
        outs, lses = [], []
        for hd in range(2):
            if split:
                qh = q_ref[:, hd * LANES:(hd + 1) * LANES]
                kh = k_ref[:, hd * LANES:(hd + 1) * LANES]
            else:
                qh = jnp.where(masks[hd], q_ref[...], jnp.zeros_like(q_ref[...]))
                kh = k_ref[...]
            s = lax.dot_general(qh, kh, _NT, preferred_element_type=F32)
            mx = jnp.max(s, axis=-1, keepdims=True)
            p = jnp.exp(s - mx)
            l = jnp.sum(p, axis=-1, keepdims=True)
            o = jnp.dot(p.astype(CDT), v_ref[...], preferred_element_type=F32)
            outs.append(o / l)
            lses.append(jnp.broadcast_to(mx + jnp.log(l), (tq, LANES)))
        o_ref[...] = jnp.where(masks[0], outs[0], outs[1]).astype(o_ref.dtype)
        lse_ref[...] = jnp.where(masks[0], lses[0], lses[1])

    return pl.pallas_call(
        body, name=name, grid=(npairs, S // tq),
        in_specs=[pl.BlockSpec((tq, qw), lambda p, i: (i, p)),
                  pl.BlockSpec((S, qw), lambda p, i: (0, kblk(p))),
                  pl.BlockSpec((S, LANES), lambda p, i: (0, vblk(p)))],
        out_specs=[pl.BlockSpec((tq, LANES), lambda p, i: (i, p))] * 2,
        out_shape=[_sds((S, LANES * npairs), CDT), _sds((S, LANES * npairs), F32)],
        compiler_params=_cparams(2),
    )(q, k, v)


def _attn_bwd(q, k, v, do, o, lse, *, split, npairs, kblk, vblk, doblk, shared_kv, name):
    S = q.shape[0]
    qw = 256 if split else LANES
    tq = min(256, S)
    nkv = 1 if shared_kv else npairs

    def body(q_ref, k_ref, v_ref, do_ref, o_ref, lse_ref, dq_ref, dk_ref, dv_ref):
        masks = _pair_masks()
        p_id, i_id = pl.program_id(0), pl.program_id(1)
        first = (i_id == 0) & ((p_id == 0) if shared_kv else True)
        do = do_ref[...]
        o = o_ref[...].astype(F32)
        lse = lse_ref[...]
        v = v_ref[...]
        dqs, dks, dvs = [], [], []
        for hd in range(2):
            m = masks[hd]
            if split:
                qh = q_ref[:, hd * LANES:(hd + 1) * LANES]
                kh = k_ref[:, hd * LANES:(hd + 1) * LANES]
            else:
                qh = jnp.where(m, q_ref[...], jnp.zeros_like(q_ref[...]))
                kh = k_ref[...]
            doh = jnp.where(m, do, 0.0)
            s = lax.dot_general(qh, kh, _NT, preferred_element_type=F32)
            p = jnp.exp(s - _head_scalar(lse, m))
            delta = jnp.sum(doh * o, axis=-1, keepdims=True)
            dohb = doh.astype(CDT)
            dp = lax.dot_general(dohb, v, _NT, preferred_element_type=F32)
            ds = (p * (dp - delta)).astype(CDT)
            dq = jnp.dot(ds, kh, preferred_element_type=F32)
            dqs.append(dq if split else jnp.where(m, dq, 0.0))
            dks.append(lax.dot_general(ds, qh, _TN, preferred_element_type=F32))
            dvs.append(lax.dot_general(p.astype(CDT), dohb, _TN, preferred_element_type=F32))
        if split:
            dq_ref[:, 0:LANES] = dqs[0]
            dq_ref[:, LANES:2 * LANES] = dqs[1]
        else:
            dq_ref[...] = dqs[0] + dqs[1]
        dv = dvs[0] + dvs[1]

        @pl.when(first)
        def _():
            if split:
                dk_ref[:, 0:LANES] = dks[0]
                dk_ref[:, LANES:2 * LANES] = dks[1]
            else:
                dk_ref[...] = dks[0] + dks[1]
            dv_ref[...] = dv

        @pl.when(jnp.logical_not(first))
        def _():
            if split:
                dk_ref[:, 0:LANES] += dks[0]
                dk_ref[:, LANES:2 * LANES] += dks[1]
            else:
                dk_ref[...] += dks[0] + dks[1]
            dv_ref[...] += dv

    kvo = (lambda p, i: (0, 0)) if shared_kv else (lambda p, i: (0, p))
    return pl.pallas_call(
        body, name=name, grid=(npairs, S // tq),
        in_specs=[pl.BlockSpec((tq, qw), lambda p, i: (i, p)),
                  pl.BlockSpec((S, qw), lambda p, i: (0, kblk(p))),
                  pl.BlockSpec((S, LANES), lambda p, i: (0, vblk(p))),
                  pl.BlockSpec((tq, LANES), lambda p, i: (i, doblk(p))),
                  pl.BlockSpec((tq, LANES), lambda p, i: (i, p)),
                  pl.BlockSpec((tq, LANES), lambda p, i: (i, p))],
        out_specs=[pl.BlockSpec((tq, qw), lambda p, i: (i, p)),
                   pl.BlockSpec((S, qw), kvo), pl.BlockSpec((S, LANES), kvo)],
        out_shape=[_sds((S, qw * npairs), F32), _sds((S, qw * nkv), F32), _sds((S, LANES * nkv), F32)],
        compiler_params=_cparams(2),
    )(q, k, v, do, o, lse)


def _bias_expand(idx, rel_bias, name):
    tq, kw = idx.shape

    def body(idx_ref, rb_ref, o_ref):
        idx = idx_ref[...]
        for hd in range(DIL_HEADS):
            acc = jnp.full((tq, kw), NEG_INF, F32)
            for u in range(REL_BUCKETS):
                acc = jnp.where(idx == u, rb_ref[u, hd], acc)
            o_ref[hd] = acc

    return pl.pallas_call(
        body, name=name,
        in_specs=[pl.BlockSpec(memory_space=pltpu.VMEM), pl.BlockSpec(memory_space=pltpu.SMEM)],
        out_specs=pl.BlockSpec(memory_space=pltpu.VMEM),
        out_shape=_sds((DIL_HEADS, tq, kw), F32),
    )(idx, rel_bias)


def _bias_reduce(idx, dtab, name):
    tq, kw = idx.shape

    def body(idx_ref, d_ref, o_ref):
        idx = idx_ref[...]
        rowid = lax.broadcasted_iota(jnp.int32, (REL_BUCKETS, kw), 0)
        for hd in range(DIL_HEADS):
            d = d_ref[hd]
            acc = jnp.zeros((REL_BUCKETS, kw), F32)
            for u in range(REL_BUCKETS):
                r = jnp.sum(jnp.where(idx == u, d, 0.0), axis=0, keepdims=True)
                acc = jnp.where(rowid == u, r, acc)
            o_ref[hd] = jnp.sum(acc, axis=1, keepdims=True)

    return pl.pallas_call(
        body, name=name,
        in_specs=[pl.BlockSpec(memory_space=pltpu.VMEM)] * 2, out_specs=pl.BlockSpec(memory_space=pltpu.VMEM),
        out_shape=_sds((DIL_HEADS, REL_BUCKETS, 1), F32),
    )(idx, dtab)


def _dil_window(i, tq, kw, L):
    start = pl.multiple_of(i * tq, DIL_HALF)
    key = start + lax.broadcasted_iota(jnp.int32, (1, kw), 1) - DIL_HALF
    return start, (key >= 0) & (key < L)


def _dil_fwd(qv, kv, vv, tab, *, dil, L, tq, name):
    kw = tq + 2 * DIL_HALF
    npair = DIL_HEADS // 2

    def body(q_ref, k_ref, v_ref, t_ref, o_ref, lse_ref):
        masks = _pair_masks()
        start, valid = _dil_window(pl.program_id(2), tq, kw, L)
        kwin = k_ref[pl.ds(start, kw), :]
        vwin = v_ref[pl.ds(start, kw), :]
        outs, lses = [], []
        for hd in range(2):
            qh = jnp.where(masks[hd], q_ref[...], jnp.zeros_like(q_ref[...]))
            s = lax.dot_general(qh, kwin, _NT, preferred_element_type=F32) + t_ref[hd]
            s = jnp.where(valid, s, NEG_INF)
            mx = jnp.max(s, axis=-1, keepdims=True)
            p = jnp.exp(s - mx)
            l = jnp.sum(p, axis=-1, keepdims=True)
            outs.append(jnp.dot(p.astype(CDT), vwin, preferred_element_type=F32) / l)
            lses.append(jnp.broadcast_to(mx + jnp.log(l), (tq, LANES)))
        o_ref[...] = jnp.where(masks[0], outs[0], outs[1])
        lse_ref[...] = jnp.where(masks[0], lses[0], lses[1])

    blk = lambda p, c, i: (i, c * npair + p)
    res = lambda p, c, i: (0, c * npair + p)
    return pl.pallas_call(
        body, name=name, grid=(npair, dil, L // tq),
        in_specs=[pl.BlockSpec((tq, LANES), blk), pl.BlockSpec((L + 2 * DIL_HALF, LANES), res),
                  pl.BlockSpec((L + 2 * DIL_HALF, LANES), res), pl.BlockSpec((2, tq, kw), lambda p, c, i: (p, 0, 0))],
        out_specs=[pl.BlockSpec((tq, LANES), blk)] * 2,
        out_shape=[_sds(qv.shape, F32)] * 2,
        compiler_params=_cparams(3),
    )(qv, kv, vv, tab)


def _dil_bwd(qv, kv, vv, tab, dov, lsev, deltav, *, dil, L, tq, name):
    kw = tq + 2 * DIL_HALF
    npair = DIL_HEADS // 2

    def body(q_ref, k_ref, v_ref, t_ref, do_ref, lse_ref, dl_ref, dq_ref, dk_ref, dv_ref, dt_ref):
        masks = _pair_masks()
        c_id, i_id = pl.program_id(1), pl.program_id(2)
        start, valid = _dil_window(i_id, tq, kw, L)
        kwin = k_ref[pl.ds(start, kw), :]
        vwin = v_ref[pl.ds(start, kw), :]

        @pl.when(i_id == 0)
        def _():
            dk_ref[...] = jnp.zeros_like(dk_ref)
            dv_ref[...] = jnp.zeros_like(dv_ref)

        @pl.when((i_id == 0) & (c_id == 0))
        def _():
            dt_ref[...] = jnp.zeros_like(dt_ref)

        do = do_ref[...]
        dq = jnp.zeros((tq, LANES), F32)
        dk = jnp.zeros((kw, LANES), F32)
        dv = jnp.zeros((kw, LANES), F32)
        for hd in range(2):
            m = masks[hd]
            qh = jnp.where(m, q_ref[...], jnp.zeros_like(q_ref[...]))
            doh = jnp.where(m, do, jnp.zeros_like(do))
            s = lax.dot_general(qh, kwin, _NT, preferred_element_type=F32) + t_ref[hd]
            s = jnp.where(valid, s, NEG_INF)
            p = jnp.exp(s - _head_scalar(lse_ref[...], m))
            dp = lax.dot_general(doh, vwin, _NT, preferred_element_type=F32)
            ds = p * (dp - _head_scalar(dl_ref[...], m))
            dt_ref[hd] += ds
            dsb = ds.astype(CDT)
            dq = dq + jnp.where(m, jnp.dot(dsb, kwin, preferred_element_type=F32), 0.0)
            dk = dk + lax.dot_general(dsb, qh, _TN, preferred_element_type=F32)
            dv = dv + lax.dot_general(p.astype(CDT), doh, _TN, preferred_element_type=F32)
        dq_ref[...] = dq
        dk_ref[pl.ds(start, kw), :] += dk
        dv_ref[pl.ds(start, kw), :] += dv

    blk = lambda p, c, i: (i, c * npair + p)
    res = lambda p, c, i: (0, c * npair + p)
    tsp = pl.BlockSpec((2, tq, kw), lambda p, c, i: (p, 0, 0))
    Lp = L + 2 * DIL_HALF
    return pl.pallas_call(
        body, name=name, grid=(npair, dil, L // tq),
        in_specs=[pl.BlockSpec((tq, LANES), blk), pl.BlockSpec((Lp, LANES), res), pl.BlockSpec((Lp, LANES), res), tsp,
                  pl.BlockSpec((tq, LANES), blk), pl.BlockSpec((tq, LANES), blk), pl.BlockSpec((tq, LANES), blk)],
        out_specs=[pl.BlockSpec((tq, LANES), blk), pl.BlockSpec((Lp, LANES), res), pl.BlockSpec((Lp, LANES), res), tsp],
        out_shape=[_sds(qv.shape, F32), _sds(kv.shape, F32), _sds(kv.shape, F32), _sds(tab.shape, F32)],
        compiler_params=_cparams(3),
    )(qv, kv, vv, tab, dov, lsev, deltav)


def _mix_weights(l1, l2, l3):
    mx = jnp.maximum(jnp.maximum(l1, l2), l3)
    e1, e2, e3 = jnp.exp(l1 - mx), jnp.exp(l2 - mx), jnp.exp(l3 - mx)
    inv = 1.0 / (e1 + e2 + e3)
    return e1 * inv, e2 * inv, e3 * inv


def _dil_mix_fwd(os, ls):
    S, W = os[0].shape
    ts = min(256, S)

    def body(o1, o2, o3, l1, l2, l3, out):
        w1, w2, w3 = _mix_weights(l1[...], l2[...], l3[...])
        out[...] = (w1 * o1[...] + w2 * o2[...] + w3 * o3[...]).astype(CDT)

    return pl.pallas_call(
        body, name="dil_mix_fwd", grid=(S // ts,), in_specs=[_row(ts, W)] * 6, out_specs=_row(ts, W),
        out_shape=_sds((S, W), CDT), compiler_params=_cparams(1),
    )(*os, *ls)


def _dil_mix_bwd(dcat, os, ls, j384):
    S, W = os[0].shape
    ts = min(256, S)

    def body(do_ref, o1, o2, o3, l1, l2, l3, j_ref, d1, d2, d3, e1, e2, e3):
        ws = _mix_weights(l1[...], l2[...], l3[...])
        do = do_ref[...]
        o = ws[0] * o1[...] + ws[1] * o2[...] + ws[2] * o3[...]
        dot = _headsum(do * o, j_ref[...])
        for w, d_o, e_o in zip(ws, (d1, d2, d3), (e1, e2, e3)):
            d_o[...] = (w * do).astype(CDT)
            e_o[...] = w * dot

    return pl.pallas_call(
        body, name="dil_mix_bwd", grid=(S // ts,),
        in_specs=[_row(ts, W, 1)] + [_row(ts, W)] * 6 + [_full(j384.shape)],
        out_specs=[_row(ts, W)] * 6,
        out_shape=[_sds((S, W), CDT)] * 3 + [_sds((S, W), F32)] * 3,
        compiler_params=_cparams(1),
    )(dcat, *os, *ls, j384)


def _adamw_math(w, g, m, v):
    m = ADAM_B1 * m + (1.0 - ADAM_B1) * g
    v = ADAM_B2 * v + (1.0 - ADAM_B2) * (g * g)
    m_hat = m / (1.0 - ADAM_B1 ** ADAM_STEP)
    v_hat = v / (1.0 - ADAM_B2 ** ADAM_STEP)
    delta = -ADAM_LR * (m_hat / (jnp.sqrt(v_hat) + ADAM_EPS) + ADAM_WD * w)
    return delta, m, v


def _pick8(n, target):
    best = None
    for t in range(16, min(n, target) + 1, 16):
        if n % t == 0:
            best = t
    return best if best is not None else n


_ELEMS_PER_BLOCK = 256 * 1024


def _lead_spec(a, b):
    ta = _pick8(a, max(16, _ELEMS_PER_BLOCK // b))
    return ta, pl.BlockSpec((None, ta, b), lambda l, i: (l, i, 0))


def _adamw(w, g_mine, g_sib, m, v, name):
    L, a, b = w.shape
    ta, spec = _lead_spec(a, b)
    gspec = pl.BlockSpec((ta, b), lambda l, i: (i, 0))

    def body(w_ref, gm_ref, gs_ref, m_ref, v_ref, g_o, d_o, m_o, v_o):
        g = jnp.where(pl.program_id(0) == lax.axis_index("c"), gm_ref[...], gs_ref[...])
        d, mm, vv = _adamw_math(w_ref[...], g, m_ref[...], v_ref[...])
        g_o[...] = g
        d_o[...] = d
        m_o[...] = mm
        v_o[...] = vv

    return pl.pallas_call(
        body, name=name, grid=(L, a // ta), in_specs=[spec, gspec, gspec, spec, spec], out_specs=[spec] * 4,
        out_shape=[_sds(w.shape, F32)] * 4, compiler_params=_cparams(2),
    )(w, g_mine, g_sib, m, v)


def _adamw_small(w, gall, m, v):
    R = w.shape[0]

    def body(w_ref, g_ref, m_ref, v_ref, g_o, d_o, m_o, v_o):
        g = g_ref[0]
        for k in range(1, 8):
            g = g + g_ref[k]
        d, mm, vv = _adamw_math(w_ref[...], g, m_ref[...], v_ref[...])
        g_o[...] = g
        d_o[...] = d
        m_o[...] = mm
        v_o[...] = vv

    vm = pl.BlockSpec(memory_space=pltpu.VMEM)
    return pl.pallas_call(
        body, name="adamw_small", in_specs=[vm] * 4, out_specs=[vm] * 4, out_shape=[_sds((R, LANES), F32)] * 4,
    )(w, gall, m, v)


def _sum_pair(g0, g1, t, name):
    n, a, b = t.shape
    ta, spec = _lead_spec(a, b)

    def body(g0_ref, g1_ref, t_ref, o_ref):
        mine = jnp.where(lax.axis_index("c") == 0, g0_ref[...], g1_ref[...])
        o_ref[...] = (mine + t_ref[...]).astype(o_ref.dtype)

    return pl.pallas_call(
        body, name=name, grid=(n, a // ta), in_specs=[spec] * 3, out_specs=spec, out_shape=_sds(t.shape, WIRE),
        compiler_params=_cparams(2),
    )(g0, g1, t)


def _sum_chips(pair, t, name):
    _, a, b = t.shape
    ta = _pick8(a, max(16, _ELEMS_PER_BLOCK // b))

    def body(p_ref, t_ref, o_ref):
        me = 2 * lax.axis_index("x") + lax.axis_index("y")
        acc = p_ref[me].astype(F32)
        for k in range(3):
            acc = acc + t_ref[k].astype(F32)
        o_ref[...] = acc

    return pl.pallas_call(
        body, name=name, grid=(a // ta,),
        in_specs=[pl.BlockSpec((4, ta, b), lambda i: (0, i, 0)), pl.BlockSpec((3, ta, b), lambda i: (0, i, 0))],
        out_specs=pl.BlockSpec((ta, b), lambda i: (i, 0)), out_shape=_sds((a, b), F32), compiler_params=_cparams(1),
    )(pair, t)


_HBM = pl.BlockSpec(memory_space=pltpu.HBM)


def _place():
    x, y, c = lax.axis_index("x"), lax.axis_index("y"), lax.axis_index("c")
    chips = [(1 - x, y), (x, 1 - y), (1 - x, 1 - y)]
    return x, y, c, chips


def _remote(src, dst, ssem, rsem, to):
    return pltpu.make_async_remote_copy(src_ref=src, dst_ref=dst, send_sem=ssem, recv_sem=rsem, device_id=to,
                                        device_id_type=MESH_ID)


def _dma_sems(n):
    return pltpu.SemaphoreType.DMA((n,))


def _gather_weights(shards):
    n = len(shards)

    def body(*refs):
        w_refs, g_refs = refs[:n], refs[n:2 * n]
        ssem, rsem = refs[2 * n:]
        x, y, c, chips = _place()
        me = 2 * x + y
        sib = (x, y, 1 - c)
        owns = [_remote(w.at[l], g.at[l, me], ssem.at[6 * n + 2 * i + l], rsem.at[6 * n + 2 * i + l], sib)
                for i, (w, g) in enumerate(zip(w_refs, g_refs)) for l in range(2)]
        first = [_remote(w.at[c], g.at[c, me], ssem.at[3 * i + j], rsem.at[3 * i + j], (cx, cy, c))
                 for j, (cx, cy) in enumerate(chips) for i, (w, g) in enumerate(zip(w_refs, g_refs))]
        for cp in first + owns:
            cp.start()
        passed = []
        for j, (cx, cy) in enumerate(chips):
            for i, g in enumerate(g_refs):
                blk = g.at[c, 2 * cx + cy]
                _remote(blk, blk, ssem.at[3 * i + j], rsem.at[3 * i + j], sib).wait_recv()
                fw = _remote(blk, blk, ssem.at[3 * n + 3 * i + j], rsem.at[3 * n + 3 * i + j], sib)
                fw.start()
                passed.append(fw)
        for j, (cx, cy) in enumerate(chips):
            for i, g in enumerate(g_refs):
                blk = g.at[1 - c, 2 * cx + cy]
                _remote(blk, blk, ssem.at[3 * n + 3 * i + j], rsem.at[3 * n + 3 * i + j], sib).wait_recv()
        for cp in owns:
            cp.wait_recv()
        for cp in first + passed + owns:
            cp.wait_send()

    return pl.pallas_call(
        body, name="gather_weights", in_specs=[_HBM] * n, out_specs=[_HBM] * n,
        out_shape=[_sds((2, 4) + s.shape[1:], s.dtype) for s in shards],
        scratch_shapes=[_dma_sems(8 * n), _dma_sems(8 * n)],
    )(*shards)


def _sibling_halves(g0s, g1s):
    n = len(g0s)

    def body(*refs):
        g0_refs, g1_refs, t_refs = refs[:n], refs[n:2 * n], refs[2 * n:3 * n]
        ssem, rsem = refs[3 * n:]
        x, y, c, _ = _place()

        def swap(srcs):
            cps = [_remote(s, t, ssem.at[i], rsem.at[i], (x, y, 1 - c)) for i, (s, t) in enumerate(zip(srcs, t_refs))]
            for cp in cps:
                cp.start()
            for cp in cps:
                cp.wait()

        @pl.when(c == 0)
        def _():
            swap(g1_refs)

        @pl.when(c == 1)
        def _():
            swap(g0_refs)

    return pl.pallas_call(
        body, name="rs_sibling_halves", in_specs=[_HBM] * (2 * n), out_specs=[_HBM] * n,
        out_shape=[_sds(g.shape, g.dtype) for g in g0s], scratch_shapes=[_dma_sems(n), _dma_sems(n)],
    )(*g0s, *g1s)


def _chip_all_to_all(parts):
    n = len(parts)

    def body(*refs):
        a_refs, t_refs = refs[:n], refs[n:2 * n]
        ssem, rsem = refs[2 * n:]
        x, y, c, chips = _place()
        sends = [_remote(a.at[2 * cx + cy], t.at[j], ssem.at[3 * i + j], rsem.at[3 * i + j], (cx, cy, c))
                 for j, (cx, cy) in enumerate(chips) for i, (a, t) in enumerate(zip(a_refs, t_refs))]
        for cp in sends:
            cp.start()
        for cp in sends:
            cp.wait_recv()
        for cp in sends:
            cp.wait_send()

    return pl.pallas_call(
        body, name="rs_chip_all_to_all", in_specs=[_HBM] * n, out_specs=[_HBM] * n,
        out_shape=[_sds((3,) + p.shape[1:], p.dtype) for p in parts],
        scratch_shapes=[_dma_sems(3 * n), _dma_sems(3 * n)],
    )(*parts)


def _sibling_swap(reds):
    n = len(reds)

    def body(*refs):
        q_refs, o_refs = refs[:n], refs[n:2 * n]
        ssem, rsem = refs[2 * n:]
        x, y, c, _ = _place()
        cps = [_remote(q, o, ssem.at[i], rsem.at[i], (x, y, 1 - c)) for i, (q, o) in enumerate(zip(q_refs, o_refs))]
        for cp in cps:
            cp.start()
        for cp in cps:
            cp.wait()

    return pl.pallas_call(
        body, name="rs_sibling_swap", in_specs=[_HBM] * n, out_specs=[_HBM] * n,
        out_shape=[_sds(q.shape, q.dtype) for q in reds],
        scratch_shapes=[_dma_sems(n), _dma_sems(n)],
    )(*reds)


def _gather_small(s):
    R, _ = s.shape

    def body(s_ref, o_ref, ssem, rsem, lsem):
        x, y, c, _ = _place()
        me = 4 * x + 2 * y + c
        own = pltpu.make_async_copy(s_ref, o_ref.at[me], lsem)
        own.start()
        sends = []
        for k in range(1, 8):
            px, py, pc = x ^ (k >> 2), y ^ ((k >> 1) & 1), c ^ (k & 1)
            cp = _remote(s_ref, o_ref.at[me], ssem.at[k - 1], rsem.at[k - 1], (px, py, pc))
            cp.start()
            sends.append(cp)
        for k in range(1, 8):
            px, py, pc = x ^ (k >> 2), y ^ ((k >> 1) & 1), c ^ (k & 1)
            blk = o_ref.at[4 * px + 2 * py + pc]
            _remote(blk, blk, ssem.at[k - 1], rsem.at[k - 1], (px, py, pc)).wait_recv()
        for cp in sends:
            cp.wait_send()
        own.wait()

    vm = pl.BlockSpec(memory_space=pltpu.VMEM)
    return pl.pallas_call(
        body, name="gather_small", in_specs=[vm], out_specs=vm, out_shape=_sds((8, R, LANES), s.dtype),
        scratch_shapes=[pltpu.SemaphoreType.DMA((7,)), pltpu.SemaphoreType.DMA((7,)), pltpu.SemaphoreType.DMA],
    )(s)


_BIG = ("w_in", "mla_w_uq", "mla_w_ukv", "w_out", "ffn_w_gate", "ffn_w_up", "ffn_w_down")
_SMALL = ("mla_q_norm", "mla_kv_norm", "gqa_q_norm", "gqa_k_norm", "rel_bias", "ln1_g", "ln1_b", "ln2_g", "ln2_b")


def _pack_flat(arrs, align):
    flat = jnp.concatenate([a.reshape(-1) for a in arrs])
    pad = (-flat.shape[0]) % align
    return jnp.pad(flat, (0, pad)) if pad else flat


def _unpack_flat(flat, shapes):
    out, off = [], 0
    for s in shapes:
        n = int(np.prod(s))
        out.append(flat[off:off + n].reshape(s))
        off += n
    return out


def _perm_gqa_rows(w):
    return jnp.concatenate([w[:832], w[896:960], w[832:896], w[960:]], axis=0)


def _local_step(x, target, W, small):
    S, D = x.shape
    depth = W["w_in"].shape[0]
    alpha = (2.0 * depth) ** 0.25
    in_idx, uq_idx, ukv_idx = _in_cols(), _uq_cols(), _ukv_cols()
    win = [_cols_from_shards(W["w_in"][l], in_idx) for l in range(depth)]
    wuq = [_cols_from_shards(W["mla_w_uq"][l], uq_idx) for l in range(depth)]
    wukv = [_cols_from_shards(W["mla_w_ukv"][l], ukv_idx) for l in range(depth)]
    wout = [_perm_gqa_rows(W["w_out"][l].reshape(-1, D)) for l in range(depth)]
    wg, wu, wdn = W["ffn_w_gate"], W["ffn_w_up"], W["ffn_w_down"]

    tm, tg = _rope_tables(S)
    j256, j384 = _head_ones(256), _head_ones(384)
    mla_scale = (64 + MLA_ROPE_DIM) ** -0.5
    branches = []
    for (_, dil) in DIL_BRANCHES:
        L = S // dil
        tq = min(256, L)
        idx = jnp.asarray(_branch_bucket_idx(tq, dil))
        branches.append((dil, L, tq, idx))
    tabs = [_bias_expand(idx, small["rel_bias"], name=f"bias_expand_{b}") for b, (_, _, _, idx) in enumerate(branches)]

    def view(a, dil):
        return a.reshape(a.shape[0] // dil, dil * a.shape[1])

    def padded_view(a, dil):
        z = jnp.zeros((DIL_HALF * dil, a.shape[1]), a.dtype)
        return view(jnp.concatenate([z, a, z], axis=0), dil)

    def unpad_view(a, dil):
        return a[DIL_HALF:a.shape[0] - DIL_HALF].reshape(S, 384)

    saved = []
    xf, xb = x, x.astype(CDT)
    for l in range(depth):
        gq, gkv = small["mla_q_norm"][l][None], small["mla_kv_norm"][l][None]
        ggq = jnp.tile(small["gqa_q_norm"][l], 4)[None]
        ggk = jnp.tile(small["gqa_k_norm"][l], 2)[None]
        h = _mm(xb, win[l], name="mm_in")
        cq, ckv, kr, qd, kd, vd, qg, kg, vg = _prep_fwd(h, gq, gkv, ggq, ggk, tm, tg, j256)
        qa = _mm(cq, wuq[l], name="mm_uq")
        kvp = _mm(ckv, wukv[l], out_dtype=CDT, name="mm_ukv")
        qm, km = _mla_prep_fwd(qa, kvp, kr, tm, mla_scale)
        oa, lsa = _attn_fwd(qm, km, kvp, split=True, npairs=3, kblk=lambda p: p, vblk=lambda p: 6 + p,
                            name="mla_attn_fwd")
        oc, lsc = _attn_fwd(qg, kg, vg, split=False, npairs=2, kblk=lambda p: 0, vblk=lambda p: 0,
                            name="gqa_attn_fwd")
        obs, lbs, kvs = [], [], []
        for b, (dil, L, tq, _) in enumerate(branches):
            kpv, vpv = padded_view(kd, dil), padded_view(vd, dil)
            o_b, l_b = _dil_fwd(view(qd, dil), kpv, vpv, tabs[b], dil=dil, L=L, tq=tq, name=f"dil_fwd_{b}")
            obs.append(o_b.reshape(S, 384))
            lbs.append(l_b.reshape(S, 384))
            kvs.append((kpv, vpv))
        ob = _dil_mix_fwd(obs, lbs)
        cat = jnp.concatenate([oa, ob, oc], axis=1)
        mix = _mm(cat, wout[l], name="mm_out")
        x1, x1b, z1 = _ln_fwd(xf, mix, small["ln1_g"][l][None], small["ln1_b"][l][None], alpha, name="ln1_fwd")
        g3, u3, act = _ffn_up(x1b, wg[l], wu[l])
        ff = _mm(act, wdn[l], ga=True, gb=True, name="mm_down")
        x2, x2b, z2 = _ln_fwd(x1, ff, small["ln2_g"][l][None], small["ln2_b"][l][None], alpha, name="ln2_fwd")
        saved.append(dict(xb=xb, h=h, cq=cq, ckv=ckv, qd=qd, qg=qg, kg=kg, vg=vg, kvp=kvp, qm=qm, km=km, oa=oa, lsa=lsa,
                          oc=oc, lsc=lsc, obs=obs, lbs=lbs, kvs=kvs, cat=cat, z1=z1, x1b=x1b, g3=g3, u3=u3, act=act, z2=z2,
                          gq=gq, gkv=gkv, ggq=ggq, ggk=ggk))
        xf, xb = x2, x2b

    dy, loss = _loss_kernel(xf, target)

    gW = {k: [None] * depth for k in _BIG}
    gS = {k: [None] * depth for k in ("mla_q_norm", "mla_kv_norm", "gqa_q_norm", "gqa_k_norm", "ln1_g", "ln1_b", "ln2_g",
                                      "ln2_b")}
    g_rel = None
    dya, dyb = dy, None
    for l in reversed(range(depth)):
        sv = saved[l]
        dz2, dz2b, gS["ln2_g"][l], gS["ln2_b"][l] = _ln_bwd(dya, dyb, sv["z2"], small["ln2_g"][l][None], alpha,
                                                             name="ln2_bwd" if dyb is not None else "ln2_bwd_last")
        gW["ffn_w_down"][l] = _mm(sv["act"], dz2b, ta=True, ga=True, go=True, name="mm_down_dw")
        dg3, du3 = _ffn_down_dx(dz2b, wdn[l], sv["g3"], sv["u3"])
        gW["ffn_w_gate"][l] = _mm(sv["x1b"], dg3, ta=True, gb=True, go=True, name="mm_gate_dw")
        gW["ffn_w_up"][l] = _mm(sv["x1b"], du3, ta=True, gb=True, go=True, name="mm_up_dw")
        dx1 = _ffn_up_dx(dg3, du3, wg[l], wu[l])
        dz1, dz1b, gS["ln1_g"][l], gS["ln1_b"][l] = _ln_bwd(dx1, dz2, sv["z1"], small["ln1_g"][l][None], alpha,
                                                             name="ln1_bwd")
        gW["w_out"][l] = _perm_gqa_rows(_mm(sv["cat"], dz1b, ta=True, name="mm_out_dw")).reshape(4, -1, D)
        dcat = _mm(dz1b, wout[l], tb=True, name="mm_out_dx")
        dqg, dkg, dvg = _attn_bwd(sv["qg"], sv["kg"], sv["vg"], dcat, sv["oc"], sv["lsc"], split=False, npairs=2,
                                  kblk=lambda p: 0, vblk=lambda p: 0, doblk=lambda p: 6 + p, shared_kv=True,
                                  name="gqa_attn_bwd")
        dqm, dkm, dvm = _attn_bwd(sv["qm"], sv["km"], sv["kvp"], dcat, sv["oa"], sv["lsa"], split=True, npairs=3,
                                  kblk=lambda p: p, vblk=lambda p: 6 + p, doblk=lambda p: p, shared_kv=False,
                                  name="mla_attn_bwd")
        dqa, dkvp, dkr = _mla_prep_bwd(dqm, dkm, dvm, tm, mla_scale)
        gW["mla_w_uq"][l] = _cols_to_shards(_mm(sv["cq"], dqa, ta=True, name="mm_uq_dw"), uq_idx, MLA_HEADS * 96)
        dcq = _mm(dqa, wuq[l], tb=True, name="mm_uq_dx")
        gW["mla_w_ukv"][l] = _cols_to_shards(_mm(sv["ckv"], dkvp, ta=True, name="mm_ukv_dw"), ukv_idx, MLA_HEADS * 128)
        dckv = _mm(dkvp, wukv[l], tb=True, name="mm_ukv_dx")
        mixb = _dil_mix_bwd(dcat, sv["obs"], sv["lbs"], j384)
        ddq, ddk, ddv = [], [], []
        for b, (dil, L, tq, idx) in enumerate(branches):
            kpv, vpv = sv["kvs"][b]
            dq_b, dk_b, dv_b, dtab = _dil_bwd(view(sv["qd"], dil), kpv, vpv, tabs[b], view(mixb[b], dil),
                                              view(sv["lbs"][b], dil), view(mixb[3 + b], dil), dil=dil, L=L, tq=tq,
                                              name=f"dil_bwd_{b}")
            ddq.append(dq_b.reshape(S, 384))
            ddk.append(unpad_view(dk_b, dil))
            ddv.append(unpad_view(dv_b, dil))
            g_b = _bias_reduce(idx, dtab, name=f"bias_reduce_{b}")[:, :, 0].T
            g_rel = g_b if g_rel is None else g_rel + g_b
        dh, n1, n2, n3, n4 = _prep_bwd(sv["h"], dcq, dckv, dkr, ddq, ddk, ddv, dqg, dkg, dvg, sv["gq"], sv["gkv"],
                                       sv["ggq"], sv["ggk"], tg, j256)
        gS["mla_q_norm"][l], gS["mla_kv_norm"][l] = n1[0], n2[0]
        gS["gqa_q_norm"][l] = n3[0].reshape(4, 64).sum(0)
        gS["gqa_k_norm"][l] = n4[0].reshape(2, 64).sum(0)
        gW["w_in"][l] = _cols_to_shards(_mm(sv["xb"], dh, ta=True, name="mm_in_dw"), in_idx, IN_W)
        dya = _mm(dh, win[l], tb=True, name="mm_in_dx")
        dyb = dz1
    grad_x = _axpy(dya, dyb, alpha, name="grad_x")

    gsmall = {k: jnp.stack([a.reshape(-1) for a in v]) for k, v in gS.items()}
    gsmall["rel_bias"] = g_rel
    return loss, grad_x, gW, gsmall


_ORDER = ("w_in", "mla_q_norm", "mla_kv_norm", "mla_w_uq", "mla_w_ukv", "gqa_q_norm", "gqa_k_norm", "rel_bias", "w_out",
          "ln1_g", "ln1_b", "ffn_w_gate", "ffn_w_up", "ffn_w_down", "ln2_g", "ln2_b")


def kernel(x, w_in, mla_q_norm, mla_kv_norm, mla_w_uq, mla_w_ukv, gqa_q_norm, gqa_k_norm, rel_bias, w_out, ln1_g, ln1_b, ffn_w_gate, ffn_w_up, ffn_w_down, ln2_g, ln2_b, loss_target, m_w_in, m_mla_q_norm, m_mla_kv_norm, m_mla_w_uq, m_mla_w_ukv, m_gqa_q_norm, m_gqa_k_norm, m_rel_bias, m_w_out, m_ln1_g, m_ln1_b, m_ffn_w_gate, m_ffn_w_up, m_ffn_w_down, m_ln2_g, m_ln2_b, v_w_in, v_mla_q_norm, v_mla_kv_norm, v_mla_w_uq, v_mla_w_ukv, v_gqa_q_norm, v_gqa_k_norm, v_rel_bias, v_w_out, v_ln1_g, v_ln1_b, v_ffn_w_gate, v_ffn_w_up, v_ffn_w_down, v_ln2_g, v_ln2_b):
    wts = dict(zip(_ORDER, (w_in, mla_q_norm, mla_kv_norm, mla_w_uq, mla_w_ukv, gqa_q_norm, gqa_k_norm, rel_bias, w_out,
                            ln1_g, ln1_b, ffn_w_gate, ffn_w_up, ffn_w_down, ln2_g, ln2_b)))
    mom = dict(zip(_ORDER, (m_w_in, m_mla_q_norm, m_mla_kv_norm, m_mla_w_uq, m_mla_w_ukv, m_gqa_q_norm, m_gqa_k_norm,
                            m_rel_bias, m_w_out, m_ln1_g, m_ln1_b, m_ffn_w_gate, m_ffn_w_up, m_ffn_w_down, m_ln2_g,
                            m_ln2_b)))
    var = dict(zip(_ORDER, (v_w_in, v_mla_q_norm, v_mla_kv_norm, v_mla_w_uq, v_mla_w_ukv, v_gqa_q_norm, v_gqa_k_norm,
                            v_rel_bias, v_w_out, v_ln1_g, v_ln1_b, v_ffn_w_gate, v_ffn_w_up, v_ffn_w_down, v_ln2_g,
                            v_ln2_b)))
    small_shapes = [wts[n].shape for n in _SMALL]

    gathered = _gather_weights([wts[n].astype(WIRE) for n in _BIG])
    full = {n: g.astype(CDT) for n, g in zip(_BIG, gathered)}

    small = {n: wts[n] for n in _SMALL}
    loss, grad_x, gbig, gsmall = _local_step(x[0], loss_target[0], full, small)

    g0s, g1s = [gbig[n][0] for n in _BIG], [gbig[n][1] for n in _BIG]
    theirs = _sibling_halves(g0s, g1s)
    pairs = [_sum_pair(g0, g1, t, name=f"rs_pair_sum_{n}") for n, g0, g1, t in zip(_BIG, g0s, g1s, theirs)]
    reds = [_sum_chips(p, t, name=f"rs_sum_chips_{n}") for n, p, t in zip(_BIG, pairs, _chip_all_to_all(pairs))]
    sibs = _sibling_swap(reds)

    sflat = _pack_flat([gsmall[n].reshape(-1) for n in _SMALL], 8 * LANES)
    rs = sflat.shape[0] // LANES
    sall = _gather_small(sflat.reshape(rs, LANES))

    def packed(d):
        return _pack_flat([d[n] for n in _SMALL], 8 * LANES).reshape(rs, LANES)

    outs = {tag: {} for tag in ("grad", "delta", "new_m", "new_v")}
    for n, red, sib in zip(_BIG, reds, sibs):
        outs["grad"][n], outs["delta"][n], outs["new_m"][n], outs["new_v"][n] = _adamw(
            wts[n], red, sib, mom[n], var[n], name=f"adamw_{n}")
    for tag, smallflat in zip(("grad", "delta", "new_m", "new_v"), _adamw_small(packed(wts), sall, packed(mom), packed(var))):
        outs[tag].update(zip(_SMALL, _unpack_flat(smallflat.reshape(-1), small_shapes)))

    total = lax.psum(loss[0, 0], ("x", "y", "c"))
    return (total, grad_x[None], *[outs["grad"][n] for n in _ORDER], *[outs["delta"][n] for n in _ORDER],
            *[outs["new_m"][n] for n in _ORDER], *[outs["new_v"][n] for n in _ORDER])
```

```python
import functools
import math

import numpy as np
import jax
import jax.numpy as jnp
from jax import lax
from jax.experimental import pallas as pl
from jax.experimental.pallas import tpu as pltpu

F32 = jnp.float32
CDT = jnp.bfloat16
WIRE = jnp.bfloat16

HEAD_DIM = 64
GRID_W = 64
ROPE_THETA = 10000.0
MLA_HEADS = 6
MLA_Q_RANK = 256
MLA_KV_RANK = 128
MLA_ROPE_DIM = 32
DIL_HEADS = 6
DIL_BRANCHES = ((128, 1), (512, 4), (2048, 16))
DIL_HALF = 64
GQA_Q_HEADS = 4
REL_BUCKETS = 32
REL_MAX_DIST = 1024
NEG_INF = -1e30
LANES = 128
VMEM_LIMIT = 56 * 1024 * 1024

ADAM_LR, ADAM_B1, ADAM_B2, ADAM_EPS, ADAM_WD, ADAM_STEP = 0.001, 0.9, 0.999, 1e-08, 0.01, 10

C_CQ, C_CKV, C_KR, C_DQ, C_DK, C_DV, C_GQ, C_GK, C_GV, IN_P = 0, 256, 384, 512, 896, 1280, 1664, 1920, 2048, 2176
IN_W = 2080
MESH_ID = pl.DeviceIdType.MESH


def _cparams(n_axes, vmem=VMEM_LIMIT):
    return pltpu.CompilerParams(dimension_semantics=("arbitrary",) * n_axes, vmem_limit_bytes=vmem)


MAX_WHOLE_DIM = 2304


def _pick(n, target):
    best = None
    for t in range(LANES, min(n, target) + 1, LANES):
        if n % t == 0:
            best = t
    if best is not None and (2 * best >= target or n > MAX_WHOLE_DIM):
        return best
    return n


def _sds(shape, dtype):
    return jax.ShapeDtypeStruct(tuple(shape), dtype)


def _in_cols():
    idx = -np.ones((IN_P,), np.int64)
    idx[C_CQ:C_CQ + 256] = np.arange(0, 256)
    idx[C_CKV:C_CKV + 128] = np.arange(256, 384)
    idx[C_KR + 64:C_KR + 96] = np.arange(384, 416)
    idx[C_DQ:C_DQ + 1152] = np.arange(416, 1568)
    gq = 1568 + (np.array([0, 2, 1, 3])[:, None] * 64 + np.arange(64)[None, :]).reshape(-1)
    idx[C_GQ:C_GQ + 256] = gq
    idx[C_GK:C_GK + 256] = np.arange(1824, 2080)
    return idx


def _uq_cols():
    idx = -np.ones((MLA_HEADS * 128,), np.int64)
    for h in range(MLA_HEADS):
        idx[h * 128:h * 128 + 96] = np.arange(96 * h, 96 * h + 96)
    return idx


def _ukv_cols():
    idx = -np.ones((MLA_HEADS * 128 + MLA_HEADS * 64,), np.int64)
    for h in range(MLA_HEADS):
        idx[h * 128:h * 128 + 64] = np.arange(128 * h, 128 * h + 64)
        idx[768 + h * 64:768 + h * 64 + 64] = np.arange(128 * h + 64, 128 * h + 128)
    return idx


def _out_rows():
    idx = np.arange(1024)
    idx[768:1024] = 768 + (np.array([0, 2, 1, 3])[:, None] * 64 + np.arange(64)[None, :]).reshape(-1)
    return idx


def _runs(idx):
    out, i = [], 0
    while i < len(idx):
        j = i + 1
        while j < len(idx) and ((idx[i] < 0 and idx[j] < 0) or (idx[i] >= 0 and idx[j] == idx[j - 1] + 1)):
            j += 1
        out.append((int(idx[i]), j - i))
        i = j
    return out


def _rows_from_shards(sh, idx):
    _, cs, r = sh.shape
    pieces = []
    for first, ln in _runs(idx):
        if first < 0:
            pieces.append(jnp.zeros((ln, r), sh.dtype))
            continue
        while ln > 0:
            k, off = divmod(first, cs)
            take = min(ln, cs - off)
            pieces.append(sh[k, off:off + take, :])
            first, ln = first + take, ln - take
    return jnp.concatenate(pieces, axis=0)


def _rows_to_shards(wp, idx, n):
    inv = np.zeros((n,), np.int64)
    pos = np.nonzero(idx >= 0)[0]
    inv[idx[pos]] = pos
    cs = n // 4
    shards = []
    for k in range(4):
        pieces = [wp[first:first + ln, :] for first, ln in _runs(inv[k * cs:(k + 1) * cs])]
        shards.append(jnp.concatenate(pieces, axis=0))
    return jnp.stack(shards)


def _t5_bucket_np(rel):
    nb = REL_BUCKETS // 2
    exact = nb // 2
    ret = np.where(rel > 0, nb, 0)
    n = np.abs(rel)
    nf = np.maximum(n, 1).astype(np.float32)
    large = exact + (np.log(nf / np.float32(exact)) / np.float32(math.log(REL_MAX_DIST / exact))
                     * np.float32(nb - exact)).astype(np.int32)
    large = np.minimum(large, nb - 1)
    return ret + np.where(n < exact, n, large)


def _branch_bucket_idx(tq, dil):
    kw = tq + 2 * DIL_HALF
    rel = np.arange(kw)[None, :] - DIL_HALF - np.arange(tq)[:, None]
    idx = _t5_bucket_np(rel * dil)
    return np.where(np.abs(rel) <= DIL_HALF, idx, -1).astype(np.int32)


def _rope_tables(S):
    inv = ROPE_THETA ** (-jnp.arange(0, 32, 2, dtype=F32) / 32)
    t = jnp.arange(S)
    pos = t.astype(F32)
    row = (t // GRID_W).astype(F32)
    col = (t % GRID_W).astype(F32)
    lane = np.arange(LANES)
    wm = lane - 64
    is_rope = (lane >= 64) & (lane < 96)
    ang = pos[:, None] * inv[np.where(is_rope, wm % 16, 0)][None, :]
    cm = jnp.where(is_rope[None], jnp.cos(ang), 1.0)
    smm = jnp.where((is_rope & (wm < 16))[None], -jnp.sin(ang), 0.0)
    spm = jnp.where((is_rope & (wm >= 16))[None], jnp.sin(ang), 0.0)
    g = lane % 64
    w = g % 32
    angg = jnp.where((g < 32)[None], row[:, None], col[:, None]) * inv[w % 16][None, :]
    cg = jnp.cos(angg)
    smg = jnp.where((w < 16)[None], -jnp.sin(angg), 0.0)
    spg = jnp.where((w >= 16)[None], jnp.sin(angg), 0.0)
    return (cm, smm, spm), (cg, smg, spg)


def _lanes(t, width):
    return t if width == LANES else jnp.concatenate([t] * (width // LANES), axis=1)


def _rope(x, tabs):
    c, sm, sp = (_lanes(t, x.shape[1]) for t in tabs)
    w = x.shape[1]
    return x * c + pltpu.roll(x, w - 16, 1) * sm + pltpu.roll(x, 16, 1) * sp


def _rope_t(dy, tabs):
    c, sm, sp = (_lanes(t, dy.shape[1]) for t in tabs)
    w = dy.shape[1]
    return dy * c + pltpu.roll(dy * sm, 16, 1) + pltpu.roll(dy * sp, w - 16, 1)


def _head_ones(width):
    i = np.arange(width)
    return jnp.asarray((i[:, None] // HEAD_DIM == i[None, :] // HEAD_DIM).astype(np.float32))


def _headsum(x, j):
    return jnp.dot(x, j, preferred_element_type=F32, precision=lax.Precision.HIGHEST)


def _mm(a, b, *, ta=False, tb=False, ga=False, gb=False, go=False, out_dtype=F32, add=None, name):
    G = a.shape[0] if ga else (b.shape[0] if gb else 1)
    a2 = a.shape[1:] if ga else a.shape
    b2 = b.shape[1:] if gb else b.shape
    K, M = a2 if ta else a2[::-1]
    N = b2[0] if tb else b2[1]
    assert (b2[1] if tb else b2[0]) == K
    tm, tn, tk = _pick(M, 1024), _pick(N, 1024), _pick(K, 1024)
    if tm * tn > 1024 * 1152:
        tm = _pick(M, 512)
    nk = K // tk
    steps = nk if (go or G == 1) else G * nk
    dn = (((0 if ta else 1,), (1 if tb else 0,)), ((), ()))

    def body(a_ref, b_ref, *rest):
        rest = list(rest)
        add_ref = rest.pop(0) if add is not None else None
        o_ref = rest.pop(0)
        part = lax.dot_general(a_ref[...], b_ref[...], dn, preferred_element_type=F32)
        if steps == 1:
            if add_ref is not None:
                part = part + add_ref[...]
            o_ref[...] = part.astype(o_ref.dtype)
            return
        acc_ref, = rest
        s = pl.program_id(3)

        @pl.when(s == 0)
        def _():
            acc_ref[...] = part if add_ref is None else part + add_ref[...]

        @pl.when(s > 0)
        def _():
            acc_ref[...] += part

        @pl.when(s == steps - 1)
        def _():
            o_ref[...] = acc_ref[...].astype(o_ref.dtype)

    def grp(g, s):
        return g if go else s // nk

    def kk(s):
        return s if steps == nk else s % nk

    def spec(grouped, block, index):
        if grouped:
            return pl.BlockSpec((None,) + block, lambda g, i, j, s: (grp(g, s),) + index(i, j, s))
        return pl.BlockSpec(block, lambda g, i, j, s: index(i, j, s))

    a_spec = (spec(ga, (tk, tm), lambda i, j, s: (kk(s), i)) if ta else spec(ga, (tm, tk), lambda i, j, s: (i, kk(s))))
    b_spec = (spec(gb, (tn, tk), lambda i, j, s: (j, kk(s))) if tb else spec(gb, (tk, tn), lambda i, j, s: (kk(s), j)))
    o_spec = spec(go, (tm, tn), lambda i, j, s: (i, j))
    return pl.pallas_call(
        body, name=name, grid=(G if go else 1, M // tm, N // tn, steps),
        in_specs=[a_spec, b_spec] + ([o_spec] if add is not None else []), out_specs=o_spec,
        out_shape=_sds(((G,) if go else ()) + (M, N), out_dtype),
        scratch_shapes=[pltpu.VMEM((tm, tn), F32)] if steps > 1 else [],
        compiler_params=_cparams(4),
    )(*([a, b] + ([add] if add is not None else [])))


def _row(ts, w, cb=0):
    return pl.BlockSpec((ts, w), lambda i: (i, cb))


def _full(shape):
    nd = len(shape)
    return pl.BlockSpec(tuple(shape), lambda i: (0,) * nd)


def _rms_fwd(x, g, eps=1e-6):
    r = lax.rsqrt(jnp.mean(x * x, axis=-1, keepdims=True) + eps)
    return x * r * g


def _rms_bwd(x, g, dy, eps=1e-6):
    r = lax.rsqrt(jnp.mean(x * x, axis=-1, keepdims=True) + eps)
    gdy = g * dy
    dx = r * gdy - x * (r * r * r) * jnp.mean(x * gdy, axis=-1, keepdims=True)
    return dx, x * r * dy


def _rms_head_fwd(x, g, j, eps=1e-6):
    r = lax.rsqrt(_headsum(x * x, j) * (1.0 / HEAD_DIM) + eps)
    return x * r * g


def _rms_head_bwd(x, g, dy, j, eps=1e-6):
    r = lax.rsqrt(_headsum(x * x, j) * (1.0 / HEAD_DIM) + eps)
    gdy = g * dy
    dx = r * gdy - x * (r * r * r) * (_headsum(x * gdy, j) * (1.0 / HEAD_DIM))
    return dx, x * r * dy


def _prep_fwd(h, gq, gkv, ggq, ggk, tm, tg, j256):
    S = h.shape[0]
    ts = min(256, S)
    scale = HEAD_DIM ** -0.5

    def body(h_ref, gq_ref, gkv_ref, ggq_ref, ggk_ref, cm, smm, spm, cg, smg, spg, j_ref,
             cq_o, ckv_o, kr_o, dq_o, dk_o, dv_o, gq_o, gk_o, gv_o):
        tabm = (cm[...], smm[...], spm[...])
        tabg = (cg[...], smg[...], spg[...])
        cq_o[...] = _rms_fwd(h_ref[:, C_CQ:C_CQ + 256], gq_ref[...]).astype(CDT)
        ckv_o[...] = _rms_fwd(h_ref[:, C_CKV:C_CKV + 128], gkv_ref[...]).astype(CDT)
        kr_o[...] = _rope(h_ref[:, C_KR:C_KR + 128], tabm).astype(CDT)
        dq_o[...] = (h_ref[:, C_DQ:C_DQ + 384] * scale).astype(CDT)
        dk_o[...] = h_ref[:, C_DK:C_DK + 384].astype(CDT)
        dv_o[...] = h_ref[:, C_DV:C_DV + 384].astype(CDT)
        qn = _rms_head_fwd(h_ref[:, C_GQ:C_GQ + 256], ggq_ref[...], j_ref[...])
        gq_o[...] = (_rope(qn, tabg) * scale).astype(CDT)
        kn = _rms_head_fwd(h_ref[:, C_GK:C_GK + 128], ggk_ref[...], j_ref[0:128, 0:128])
        gk_o[...] = _rope(kn, tabg).astype(CDT)
        gv_o[...] = h_ref[:, C_GV:C_GV + 128].astype(CDT)

    widths = (256, 128, 128, 384, 384, 384, 256, 128, 128)
    return pl.pallas_call(
        body, name="prep_fwd", grid=(S // ts,),
        in_specs=[_row(ts, IN_P), _full(gq.shape), _full(gkv.shape), _full(ggq.shape), _full(ggk.shape)]
        + [_row(ts, LANES)] * 6 + [_full(j256.shape)],
        out_specs=[_row(ts, w) for w in widths],
        out_shape=[_sds((S, w), CDT) for w in widths],
        compiler_params=_cparams(1),
    )(h, gq, gkv, ggq, ggk, *tm, *tg, j256)


def _prep_bwd(h, dcq, dckv, dkr, ddq, ddk, ddv, dgq, dgk, dgv, gq, gkv, ggq, ggk, tg, j256):
    S = h.shape[0]
    ts = min(256, S)
    scale = HEAD_DIM ** -0.5

    def body(h_ref, dcq_r, dckv_r, dkr_r, q1, q2, q3, k1, k2, k3, v1, v2, v3, dgq_r, dgk_r, dgv_r,
             gq_ref, gkv_ref, ggq_ref, ggk_ref, cg, smg, spg, j_ref,
             dh_o, ngq_o, ngkv_o, nggq_o, nggk_o):
        tabg = (cg[...], smg[...], spg[...])
        first = pl.program_id(0) == 0

        def acc(o_ref, val):
            s = jnp.sum(val, axis=0, keepdims=True)

            @pl.when(first)
            def _():
                o_ref[...] = s

            @pl.when(jnp.logical_not(first))
            def _():
                o_ref[...] += s

        dx, dg = _rms_bwd(h_ref[:, C_CQ:C_CQ + 256], gq_ref[...], dcq_r[...])
        dh_o[:, C_CQ:C_CQ + 256] = dx.astype(CDT)
        acc(ngq_o, dg)
        dx, dg = _rms_bwd(h_ref[:, C_CKV:C_CKV + 128], gkv_ref[...], dckv_r[...])
        dh_o[:, C_CKV:C_CKV + 128] = dx.astype(CDT)
        acc(ngkv_o, dg)
        dh_o[:, C_KR:C_KR + 128] = dkr_r[...].astype(CDT)
        dh_o[:, C_DQ:C_DQ + 384] = ((q1[...] + q2[...] + q3[...]) * scale).astype(CDT)
        dh_o[:, C_DK:C_DK + 384] = (k1[...] + k2[...] + k3[...]).astype(CDT)
        dh_o[:, C_DV:C_DV + 384] = (v1[...] + v2[...] + v3[...]).astype(CDT)
        dqn = _rope_t(dgq_r[...] * scale, tabg)
        dx, dg = _rms_head_bwd(h_ref[:, C_GQ:C_GQ + 256], ggq_ref[...], dqn, j_ref[...])
        dh_o[:, C_GQ:C_GQ + 256] = dx.astype(CDT)
        acc(nggq_o, dg)
        dkn = _rope_t(dgk_r[...], tabg)
        dx, dg = _rms_head_bwd(h_ref[:, C_GK:C_GK + 128], ggk_ref[...], dkn, j_ref[0:128, 0:128])
        dh_o[:, C_GK:C_GK + 128] = dx.astype(CDT)
        acc(nggk_o, dg)
        dh_o[:, C_GV:C_GV + 128] = dgv_r[...].astype(CDT)

    return pl.pallas_call(
        body, name="prep_bwd", grid=(S // ts,),
        in_specs=[_row(ts, IN_P), _row(ts, 256), _row(ts, 128), _row(ts, 128)] + [_row(ts, 384)] * 9
        + [_row(ts, 256), _row(ts, 128), _row(ts, 128)]
        + [_full(gq.shape), _full(gkv.shape), _full(ggq.shape), _full(ggk.shape)] + [_row(ts, LANES)] * 3
        + [_full(j256.shape)],
        out_specs=[_row(ts, IN_P), _full((1, 256)), _full((1, 128)), _full((1, 256)), _full((1, 128))],
        out_shape=[_sds((S, IN_P), CDT), _sds((1, 256), F32), _sds((1, 128), F32), _sds((1, 256), F32),
                   _sds((1, 128), F32)],
        compiler_params=_cparams(1),
    )(h, dcq, dckv, dkr, *ddq, *ddk, *ddv, dgq, dgk, dgv, gq, gkv, ggq, ggk, *tg, j256)


def _mla_prep_fwd(qa, kvp, kr, tm, scale):
    S = qa.shape[0]
    ts = min(256, S)

    def body(qa_ref, kv_ref, kr_ref, cm, smm, spm, q_o, k_o):
        tabm = (cm[...], smm[...], spm[...])
        q_o[...] = (_rope(qa_ref[...], tabm) * scale).astype(CDT)
        k_o[...] = kv_ref[:, 0:768] + _lanes(kr_ref[...], 768)

    return pl.pallas_call(
        body, name="mla_prep_fwd", grid=(S // ts,),
        in_specs=[_row(ts, 768), _row(ts, 1152), _row(ts, 128)] + [_row(ts, LANES)] * 3,
        out_specs=[_row(ts, 768)] * 2, out_shape=[_sds((S, 768), CDT)] * 2,
        compiler_params=_cparams(1),
    )(qa, kvp, kr, *tm)


def _mla_prep_bwd(dq, dk, dv, tm, scale):
    S = dq.shape[0]
    ts = min(256, S)

    def body(dq_ref, dk_ref, dv_ref, cm, smm, spm, dqa_o, dkv_o, dkr_o):
        tabm = (cm[...], smm[...], spm[...])
        lane = lax.broadcasted_iota(jnp.int32, (1, LANES), 1)
        dqa_o[...] = _rope_t(dq_ref[...] * scale, tabm).astype(CDT)
        dkr = jnp.zeros((ts, LANES), F32)
        for hd in range(MLA_HEADS):
            blk = dk_ref[:, hd * 128:(hd + 1) * 128]
            dkv_o[:, hd * 128:(hd + 1) * 128] = jnp.where(lane < 64, blk, 0.0).astype(CDT)
            dkr = dkr + jnp.where((lane >= 64) & (lane < 96), blk, 0.0)
        dkv_o[:, 768:1152] = dv_ref[...].astype(CDT)
        dkr_o[...] = jnp.where((lane >= 64) & (lane < 96), _rope_t(dkr, tabm), 0.0)

    return pl.pallas_call(
        body, name="mla_prep_bwd", grid=(S // ts,),
        in_specs=[_row(ts, 768), _row(ts, 768), _row(ts, 384)] + [_row(ts, LANES)] * 3,
        out_specs=[_row(ts, 768), _row(ts, 1152), _row(ts, 128)],
        out_shape=[_sds((S, 768), CDT), _sds((S, 1152), CDT), _sds((S, 128), F32)],
        compiler_params=_cparams(1),
    )(dq, dk, dv, *tm)


def _ln_fwd(xa, xb, g, b, alpha, name):
    S, D = xa.shape
    ts = min(256, S)

    def body(xa_ref, xb_ref, g_ref, b_ref, y_o, yb_o, z_o):
        z = alpha * xa_ref[...] + xb_ref[...]
        mu = jnp.mean(z, axis=-1, keepdims=True)
        zc = z - mu
        var = jnp.mean(zc * zc, axis=-1, keepdims=True)
        y = zc * lax.rsqrt(var + 1e-5) * g_ref[...] + b_ref[...]
        y_o[...] = y
        yb_o[...] = y.astype(CDT)
        z_o[...] = z

    return pl.pallas_call(
        body, name=name, grid=(S // ts,),
        in_specs=[_row(ts, D), _row(ts, D), _full(g.shape), _full(b.shape)],
        out_specs=[_row(ts, D)] * 3, out_shape=[_sds((S, D), F32), _sds((S, D), CDT), _sds((S, D), F32)],
        compiler_params=_cparams(1),
    )(xa, xb, g, b)


def _ln_bwd(dya, dyb, z, g, alpha, name):
    S, D = z.shape
    ts = min(256, S)
    two = dyb is not None

    def body(*refs):
        if two:
            dya_ref, dyb_ref, z_ref, g_ref, dz_o, dzb_o, dg_o, db_o = refs
            dy = dya_ref[...] + alpha * dyb_ref[...]
        else:
            dya_ref, z_ref, g_ref, dz_o, dzb_o, dg_o, db_o = refs
            dy = dya_ref[...]
        z = z_ref[...]
        mu = jnp.mean(z, axis=-1, keepdims=True)
        zc = z - mu
        r = lax.rsqrt(jnp.mean(zc * zc, axis=-1, keepdims=True) + 1e-5)
        xh = zc * r
        dxh = dy * g_ref[...]
        dz = r * (dxh - jnp.mean(dxh, axis=-1, keepdims=True) - xh * jnp.mean(dxh * xh, axis=-1, keepdims=True))
        dz_o[...] = dz
        dzb_o[...] = dz.astype(CDT)
        sg = jnp.sum(dy * xh, axis=0, keepdims=True)
        sb = jnp.sum(dy, axis=0, keepdims=True)
        first = pl.program_id(0) == 0

        @pl.when(first)
        def _():
            dg_o[...] = sg
            db_o[...] = sb

        @pl.when(jnp.logical_not(first))
        def _():
            dg_o[...] += sg
            db_o[...] += sb

    ins = [dya] + ([dyb] if two else []) + [z, g]
    return pl.pallas_call(
        body, name=name, grid=(S // ts,),
        in_specs=[_row(ts, D)] * (3 if two else 2) + [_full(g.shape)],
        out_specs=[_row(ts, D), _row(ts, D), _full((1, D)), _full((1, D))],
        out_shape=[_sds((S, D), F32), _sds((S, D), CDT), _sds((1, D), F32), _sds((1, D), F32)],
        compiler_params=_cparams(1),
    )(*ins)


def _grp_spec(ts, w):
    return pl.BlockSpec((None, ts, w), lambda k, i: (k, i, 0))


def _ffn_up(xb, wg3, wu3):
    S, D = xb.shape
    G, Fc, _ = wg3.shape
    tm = _pick(S, 1024)
    wspec = pl.BlockSpec((None, Fc, D), lambda k, i: (k, 0, 0))

    def body(x_ref, wg_ref, wu_ref, g_o, u_o, a_o):
        x = x_ref[...]
        g = lax.dot_general(x, wg_ref[...], _NT, preferred_element_type=F32)
        u = lax.dot_general(x, wu_ref[...], _NT, preferred_element_type=F32)
        g_o[...] = g
        u_o[...] = u
        a_o[...] = (g / (1.0 + jnp.exp(-g)) * u).astype(CDT)

    return pl.pallas_call(
        body, name="ffn_up", grid=(G, S // tm),
        in_specs=[pl.BlockSpec((tm, D), lambda k, i: (i, 0)), wspec, wspec], out_specs=[_grp_spec(tm, Fc)] * 3,
        out_shape=[_sds((G, S, Fc), F32), _sds((G, S, Fc), F32), _sds((G, S, Fc), CDT)], compiler_params=_cparams(2),
    )(xb, wg3, wu3)


def _ffn_up_dx(dg3, du3, wg3, wu3):
    G, S, Fc = dg3.shape
    D = wg3.shape[2]
    tm = _pick(S, 1024)
    wspec = pl.BlockSpec((None, Fc, D), lambda i, k: (k, 0, 0))
    aspec = pl.BlockSpec((None, tm, Fc), lambda i, k: (k, i, 0))

    def body(dg_ref, du_ref, wg_ref, wu_ref, o_ref):
        part = (jnp.dot(dg_ref[...], wg_ref[...], preferred_element_type=F32)
                + jnp.dot(du_ref[...], wu_ref[...], preferred_element_type=F32))
        k = pl.program_id(1)

        @pl.when(k == 0)
        def _():
            o_ref[...] = part

        @pl.when(k > 0)
        def _():
            o_ref[...] += part

    return pl.pallas_call(
        body, name="ffn_up_dx", grid=(S // tm, G), in_specs=[aspec, aspec, wspec, wspec],
        out_specs=pl.BlockSpec((tm, D), lambda i, k: (i, 0)), out_shape=_sds((S, D), F32), compiler_params=_cparams(2),
    )(dg3, du3, wg3, wu3)


def _ffn_down_dx(dzb, wd3, g3, u3):
    S, D = dzb.shape
    G, Fc, _ = wd3.shape
    tm = _pick(S, 1024)

    def body(dz_ref, wd_ref, g_ref, u_ref, dg_o, du_o):
        da = lax.dot_general(dz_ref[...], wd_ref[...], _NT, preferred_element_type=F32)
        g = g_ref[...]
        sg = 1.0 / (1.0 + jnp.exp(-g))
        dg_o[...] = (da * u_ref[...] * (sg * (1.0 + g * (1.0 - sg)))).astype(CDT)
        du_o[...] = (da * (g * sg)).astype(CDT)

    return pl.pallas_call(
        body, name="ffn_down_dx", grid=(G, S // tm),
        in_specs=[pl.BlockSpec((tm, D), lambda k, i: (i, 0)), pl.BlockSpec((None, Fc, D), lambda k, i: (k, 0, 0)),
                  _grp_spec(tm, Fc), _grp_spec(tm, Fc)],
        out_specs=[_grp_spec(tm, Fc)] * 2, out_shape=[_sds((G, S, Fc), CDT)] * 2, compiler_params=_cparams(2),
    )(dzb, wd3, g3, u3)


def _loss_kernel(y, target):
    S, D = y.shape
    ts = min(256, S)

    def body(y_ref, t_ref, dy_o, loss_o):
        e = y_ref[...] - t_ref[...]
        dy_o[...] = e * (1.0 / D)
        part = jnp.sum(jnp.sum(e * e, axis=1, keepdims=True), axis=0, keepdims=True) * (0.5 / D)
        first = pl.program_id(0) == 0

        @pl.when(first)
        def _():
            loss_o[...] = part

        @pl.when(jnp.logical_not(first))
        def _():
            loss_o[...] += part

    return pl.pallas_call(
        body, name="loss", grid=(S // ts,), in_specs=[_row(ts, D)] * 2,
        out_specs=[_row(ts, D), _full((1, 1))], out_shape=[_sds((S, D), F32), _sds((1, 1), F32)],
        compiler_params=_cparams(1),
    )(y, target)


def _axpy(a, b, alpha, name):
    S, D = a.shape
    ts = min(256, S)

    def body(a_ref, b_ref, o_ref):
        o_ref[...] = a_ref[...] + alpha * b_ref[...]

    return pl.pallas_call(
        body, name=name, grid=(S // ts,), in_specs=[_row(ts, D)] * 2, out_specs=_row(ts, D),
        out_shape=_sds((S, D), F32), compiler_params=_cparams(1),
    )(a, b)


def _pair_masks():
    lane = lax.broadcasted_iota(jnp.int32, (1, LANES), 1)
    first = lane < HEAD_DIM
    return first, jnp.logical_not(first)


def _head_scalar(x, m):
    return jnp.max(jnp.where(m, x, -jnp.inf), axis=-1, keepdims=True)


_NT = (((1,), (1,)), ((), ()))
_TN = (((0,), (0,)), ((), ()))


def _attn_fwd(q, k, v, *, split, npairs, kblk, vblk, name):
    S = q.shape[0]
    qw = 256 if split else LANES
    tq = min(256, S)

    def body(q_ref, k_ref, v_ref, o_ref, lse_ref):
        masks = _pair_masks()
        outs, lses = [], []
        for hd in range(2):
            if split:
                qh = q_ref[:, hd * LANES:(hd + 1) * LANES]
                kh = k_ref[:, hd * LANES:(hd + 1) * LANES]
            else:
                qh = jnp.where(masks[hd], q_ref[...], jnp.zeros_like(q_ref[...]))
                kh = k_ref[...]
            s = lax.dot_general(qh, kh, _NT, preferred_element_type=F32)
            mx = jnp.max(s, axis=-1, keepdims=True)
            p = jnp.exp(s - mx)
            l = jnp.sum(p, axis=-1, keepdims=True)
            o = jnp.dot(p.astype(CDT), v_ref[...], preferred_element_type=F32)
            outs.append(o / l)
            lses.append(jnp.broadcast_to(mx + jnp.log(l), (tq, LANES)))
        o_ref[...] = jnp.where(masks[0], outs[0], outs[1]).astype(o_ref.dtype)
        lse_ref[...] = jnp.where(masks[0], lses[0], lses[1])

    return pl.pallas_call(
        body, name=name, grid=(npairs, S // tq),
        in_specs=[pl.BlockSpec((tq, qw), lambda p, i: (i, p)),
                  pl.BlockSpec((S, qw), lambda p, i: (0, kblk(p))),
                  pl.BlockSpec((S, LANES), lambda p, i: (0, vblk(p)))],
        out_specs=[pl.BlockSpec((tq, LANES), lambda p, i: (i, p))] * 2,
        out_shape=[_sds((S, LANES * npairs), CDT), _sds((S, LANES * npairs), F32)],
        compiler_params=_cparams(2),
    )(q, k, v)


def _attn_bwd(q, k, v, do, o, lse, *, split, npairs, kblk, vblk, doblk, shared_kv, name):
    S = q.shape[0]
    qw = 256 if split else LANES
    tq = min(256, S)
    nkv = 1 if shared_kv else npairs

    def body(q_ref, k_ref, v_ref, do_ref, o_ref, lse_ref, dq_ref, dk_ref, dv_ref):
        masks = _pair_masks()
        p_id, i_id = pl.program_id(0), pl.program_id(1)
        first = (i_id == 0) & ((p_id == 0) if shared_kv else True)
        do = do_ref[...]
        o = o_ref[...].astype(F32)
        lse = lse_ref[...]
        v = v_ref[...]
        dqs, dks, dvs = [], [], []
        for hd in range(2):
            m = masks[hd]
            if split:
                qh = q_ref[:, hd * LANES:(hd + 1) * LANES]
                kh = k_ref[:, hd * LANES:(hd + 1) * LANES]
            else:
                qh = jnp.where(m, q_ref[...], jnp.zeros_like(q_ref[...]))
                kh = k_ref[...]
            doh = jnp.where(m, do, 0.0)
            s = lax.dot_general(qh, kh, _NT, preferred_element_type=F32)
            p = jnp.exp(s - _head_scalar(lse, m))
            delta = jnp.sum(doh * o, axis=-1, keepdims=True)
            dohb = doh.astype(CDT)
            dp = lax.dot_general(dohb, v, _NT, preferred_element_type=F32)
            ds = (p * (dp - delta)).astype(CDT)
            dq = jnp.dot(ds, kh, preferred_element_type=F32)
            dqs.append(dq if split else jnp.where(m, dq, 0.0))
            dks.append(lax.dot_general(ds, qh, _TN, preferred_element_type=F32))
            dvs.append(lax.dot_general(p.astype(CDT), dohb, _TN, preferred_element_type=F32))
        if split:
            dq_ref[:, 0:LANES] = dqs[0]
            dq_ref[:, LANES:2 * LANES] = dqs[1]
        else:
            dq_ref[...] = dqs[0] + dqs[1]
        dv = dvs[0] + dvs[1]

        @pl.when(first)
        def _():
            if split:
                dk_ref[:, 0:LANES] = dks[0]
                dk_ref[:, LANES:2 * LANES] = dks[1]
            else:
                dk_ref[...] = dks[0] + dks[1]
            dv_ref[...] = dv

        @pl.when(jnp.logical_not(first))
        def _():
            if split:
                dk_ref[:, 0:LANES] += dks[0]
                dk_ref[:, LANES:2 * LANES] += dks[1]
            else:
                dk_ref[...] += dks[0] + dks[1]
            dv_ref[...] += dv

    kvo = (lambda p, i: (0, 0)) if shared_kv else (lambda p, i: (0, p))
    return pl.pallas_call(
        body, name=name, grid=(npairs, S // tq),
        in_specs=[pl.BlockSpec((tq, qw), lambda p, i: (i, p)),
                  pl.BlockSpec((S, qw), lambda p, i: (0, kblk(p))),
                  pl.BlockSpec((S, LANES), lambda p, i: (0, vblk(p))),
                  pl.BlockSpec((tq, LANES), lambda p, i: (i, doblk(p))),
                  pl.BlockSpec((tq, LANES), lambda p, i: (i, p)),
                  pl.BlockSpec((tq, LANES), lambda p, i: (i, p))],
        out_specs=[pl.BlockSpec((tq, qw), lambda p, i: (i, p)),
                   pl.BlockSpec((S, qw), kvo), pl.BlockSpec((S, LANES), kvo)],
        out_shape=[_sds((S, qw * npairs), F32), _sds((S, qw * nkv), F32), _sds((S, LANES * nkv), F32)],
        compiler_params=_cparams(2),
    )(q, k, v, do, o, lse)


def _bias_expand(idx, rel_bias, name):
    tq, kw = idx.shape

    def body(idx_ref, rb_ref, o_ref):
        idx = idx_ref[...]
        for hd in range(DIL_HEADS):
            acc = jnp.full((tq, kw), NEG_INF, F32)
            for u in range(REL_BUCKETS):
                acc = jnp.where(idx == u, rb_ref[u, hd], acc)
            o_ref[hd] = acc

    return pl.pallas_call(
        body, name=name,
        in_specs=[pl.BlockSpec(memory_space=pltpu.VMEM), pl.BlockSpec(memory_space=pltpu.SMEM)],
        out_specs=pl.BlockSpec(memory_space=pltpu.VMEM),
        out_shape=_sds((DIL_HEADS, tq, kw), F32),
    )(idx, rel_bias)


def _bias_reduce(idx, dtab, name):
    tq, kw = idx.shape

    def body(idx_ref, d_ref, o_ref):
        idx = idx_ref[...]
        rowid = lax.broadcasted_iota(jnp.int32, (REL_BUCKETS, kw), 0)
        for hd in range(DIL_HEADS):
            d = d_ref[hd]
            acc = jnp.zeros((REL_BUCKETS, kw), F32)
            for u in range(REL_BUCKETS):
                r = jnp.sum(jnp.where(idx == u, d, 0.0), axis=0, keepdims=True)
                acc = jnp.where(rowid == u, r, acc)
            o_ref[hd] = jnp.sum(acc, axis=1, keepdims=True)

    return pl.pallas_call(
        body, name=name,
        in_specs=[pl.BlockSpec(memory_space=pltpu.VMEM)] * 2, out_specs=pl.BlockSpec(memory_space=pltpu.VMEM),
        out_shape=_sds((DIL_HEADS, REL_BUCKETS, 1), F32),
    )(idx, dtab)


def _dil_window(i, tq, kw, L):
    start = pl.multiple_of(i * tq, DIL_HALF)
    key = start + lax.broadcasted_iota(jnp.int32, (1, kw), 1) - DIL_HALF
    return start, (key >= 0) & (key < L)


def _dil_fwd(qv, kv, vv, tab, *, dil, L, tq, name):
    kw = tq + 2 * DIL_HALF
    npair = DIL_HEADS // 2

    def body(q_ref, k_ref, v_ref, t_ref, o_ref, lse_ref):
        masks = _pair_masks()
        start, valid = _dil_window(pl.program_id(2), tq, kw, L)
        kwin = k_ref[pl.ds(start, kw), :]
        vwin = v_ref[pl.ds(start, kw), :]
        outs, lses = [], []
        for hd in range(2):
            qh = jnp.where(masks[hd], q_ref[...], jnp.zeros_like(q_ref[...]))
            s = lax.dot_general(qh, kwin, _NT, preferred_element_type=F32) + t_ref[hd]
            s = jnp.where(valid, s, NEG_INF)
            mx = jnp.max(s, axis=-1, keepdims=True)
            p = jnp.exp(s - mx)
            l = jnp.sum(p, axis=-1, keepdims=True)
            outs.append(jnp.dot(p.astype(CDT), vwin, preferred_element_type=F32) / l)
            lses.append(jnp.broadcast_to(mx + jnp.log(l), (tq, LANES)))
        o_ref[...] = jnp.where(masks[0], outs[0], outs[1])
        lse_ref[...] = jnp.where(masks[0], lses[0], lses[1])

    blk = lambda p, c, i: (i, c * npair + p)
    res = lambda p, c, i: (0, c * npair + p)
    return pl.pallas_call(
        body, name=name, grid=(npair, dil, L // tq),
        in_specs=[pl.BlockSpec((tq, LANES), blk), pl.BlockSpec((L + 2 * DIL_HALF, LANES), res),
                  pl.BlockSpec((L + 2 * DIL_HALF, LANES), res), pl.BlockSpec((2, tq, kw), lambda p, c, i: (p, 0, 0))],
        out_specs=[pl.BlockSpec((tq, LANES), blk)] * 2,
        out_shape=[_sds(qv.shape, F32)] * 2,
        compiler_params=_cparams(3),
    )(qv, kv, vv, tab)


def _dil_bwd(qv, kv, vv, tab, dov, lsev, deltav, *, dil, L, tq, name):
    kw = tq + 2 * DIL_HALF
    npair = DIL_HEADS // 2

    def body(q_ref, k_ref, v_ref, t_ref, do_ref, lse_ref, dl_ref, dq_ref, dk_ref, dv_ref, dt_ref):
        masks = _pair_masks()
        c_id, i_id = pl.program_id(1), pl.program_id(2)
        start, valid = _dil_window(i_id, tq, kw, L)
        kwin = k_ref[pl.ds(start, kw), :]
        vwin = v_ref[pl.ds(start, kw), :]

        @pl.when(i_id == 0)
        def _():
            dk_ref[...] = jnp.zeros_like(dk_ref)
            dv_ref[...] = jnp.zeros_like(dv_ref)

        @pl.when((i_id == 0) & (c_id == 0))
        def _():
            dt_ref[...] = jnp.zeros_like(dt_ref)

        do = do_ref[...]
        dq = jnp.zeros((tq, LANES), F32)
        dk = jnp.zeros((kw, LANES), F32)
        dv = jnp.zeros((kw, LANES), F32)
        for hd in range(2):
            m = masks[hd]
            qh = jnp.where(m, q_ref[...], jnp.zeros_like(q_ref[...]))
            doh = jnp.where(m, do, jnp.zeros_like(do))
            s = lax.dot_general(qh, kwin, _NT, preferred_element_type=F32) + t_ref[hd]
            s = jnp.where(valid, s, NEG_INF)
            p = jnp.exp(s - _head_scalar(lse_ref[...], m))
            dp = lax.dot_general(doh, vwin, _NT, preferred_element_type=F32)
            ds = p * (dp - _head_scalar(dl_ref[...], m))
            dt_ref[hd] += ds
            dsb = ds.astype(CDT)
            dq = dq + jnp.where(m, jnp.dot(dsb, kwin, preferred_element_type=F32), 0.0)
            dk = dk + lax.dot_general(dsb, qh, _TN, preferred_element_type=F32)
            dv = dv + lax.dot_general(p.astype(CDT), doh, _TN, preferred_element_type=F32)
        dq_ref[...] = dq
        dk_ref[pl.ds(start, kw), :] += dk
        dv_ref[pl.ds(start, kw), :] += dv

    blk = lambda p, c, i: (i, c * npair + p)
    res = lambda p, c, i: (0, c * npair + p)
    tsp = pl.BlockSpec((2, tq, kw), lambda p, c, i: (p, 0, 0))
    Lp = L + 2 * DIL_HALF
    return pl.pallas_call(
        body, name=name, grid=(npair, dil, L // tq),
        in_specs=[pl.BlockSpec((tq, LANES), blk), pl.BlockSpec((Lp, LANES), res), pl.BlockSpec((Lp, LANES), res), tsp,
                  pl.BlockSpec((tq, LANES), blk), pl.BlockSpec((tq, LANES), blk), pl.BlockSpec((tq, LANES), blk)],
        out_specs=[pl.BlockSpec((tq, LANES), blk), pl.BlockSpec((Lp, LANES), res), pl.BlockSpec((Lp, LANES), res), tsp],
        out_shape=[_sds(qv.shape, F32), _sds(kv.shape, F32), _sds(kv.shape, F32), _sds(tab.shape, F32)],
        compiler_params=_cparams(3),
    )(qv, kv, vv, tab, dov, lsev, deltav)


def _mix_weights(l1, l2, l3):
    mx = jnp.maximum(jnp.maximum(l1, l2), l3)
    e1, e2, e3 = jnp.exp(l1 - mx), jnp.exp(l2 - mx), jnp.exp(l3 - mx)
    inv = 1.0 / (e1 + e2 + e3)
    return e1 * inv, e2 * inv, e3 * inv


def _dil_mix_fwd(os, ls):
    S, W = os[0].shape
    ts = min(256, S)

    def body(o1, o2, o3, l1, l2, l3, out):
        w1, w2, w3 = _mix_weights(l1[...], l2[...], l3[...])
        out[...] = (w1 * o1[...] + w2 * o2[...] + w3 * o3[...]).astype(CDT)

    return pl.pallas_call(
        body, name="dil_mix_fwd", grid=(S // ts,), in_specs=[_row(ts, W)] * 6, out_specs=_row(ts, W),
        out_shape=_sds((S, W), CDT), compiler_params=_cparams(1),
    )(*os, *ls)


def _dil_mix_bwd(dcat, os, ls, j384):
    S, W = os[0].shape
    ts = min(256, S)

    def body(do_ref, o1, o2, o3, l1, l2, l3, j_ref, d1, d2, d3, e1, e2, e3):
        ws = _mix_weights(l1[...], l2[...], l3[...])
        do = do_ref[...]
        o = ws[0] * o1[...] + ws[1] * o2[...] + ws[2] * o3[...]
        dot = _headsum(do * o, j_ref[...])
        for w, d_o, e_o in zip(ws, (d1, d2, d3), (e1, e2, e3)):
            d_o[...] = (w * do).astype(CDT)
            e_o[...] = w * dot

    return pl.pallas_call(
        body, name="dil_mix_bwd", grid=(S // ts,),
        in_specs=[_row(ts, W, 1)] + [_row(ts, W)] * 6 + [_full(j384.shape)],
        out_specs=[_row(ts, W)] * 6,
        out_shape=[_sds((S, W), CDT)] * 3 + [_sds((S, W), F32)] * 3,
        compiler_params=_cparams(1),
    )(dcat, *os, *ls, j384)


def _adamw_math(w, g, m, v):
    m = ADAM_B1 * m + (1.0 - ADAM_B1) * g
    v = ADAM_B2 * v + (1.0 - ADAM_B2) * (g * g)
    m_hat = m / (1.0 - ADAM_B1 ** ADAM_STEP)
    v_hat = v / (1.0 - ADAM_B2 ** ADAM_STEP)
    delta = -ADAM_LR * (m_hat / (jnp.sqrt(v_hat) + ADAM_EPS) + ADAM_WD * w)
    return delta, m, v


def _pick8(n, target):
    best = None
    for t in range(16, min(n, target) + 1, 16):
        if n % t == 0:
            best = t
    return best if best is not None else n


_ELEMS_PER_BLOCK = 256 * 1024


def _lead_spec(a, b):
    ta = _pick8(a, max(16, _ELEMS_PER_BLOCK // b))
    return ta, pl.BlockSpec((None, ta, b), lambda l, i: (l, i, 0))


def _adamw(w, g_mine, g_sib, m, v, name):
    L, a, b = w.shape
    ta, spec = _lead_spec(a, b)
    gspec = pl.BlockSpec((ta, b), lambda l, i: (i, 0))

    def body(w_ref, gm_ref, gs_ref, m_ref, v_ref, g_o, d_o, m_o, v_o):
        g = jnp.where(pl.program_id(0) == lax.axis_index("c"), gm_ref[...], gs_ref[...])
        d, mm, vv = _adamw_math(w_ref[...], g, m_ref[...], v_ref[...])
        g_o[...] = g
        d_o[...] = d
        m_o[...] = mm
        v_o[...] = vv

    return pl.pallas_call(
        body, name=name, grid=(L, a // ta), in_specs=[spec, gspec, gspec, spec, spec], out_specs=[spec] * 4,
        out_shape=[_sds(w.shape, F32)] * 4, compiler_params=_cparams(2),
    )(w, g_mine, g_sib, m, v)


def _adamw_small(w, gall, m, v):
    R = w.shape[0]

    def body(w_ref, g_ref, m_ref, v_ref, g_o, d_o, m_o, v_o):
        g = g_ref[0]
        for k in range(1, 8):
            g = g + g_ref[k]
        d, mm, vv = _adamw_math(w_ref[...], g, m_ref[...], v_ref[...])
        g_o[...] = g
        d_o[...] = d
        m_o[...] = mm
        v_o[...] = vv

    vm = pl.BlockSpec(memory_space=pltpu.VMEM)
    return pl.pallas_call(
        body, name="adamw_small", in_specs=[vm] * 4, out_specs=[vm] * 4, out_shape=[_sds((R, LANES), F32)] * 4,
    )(w, gall, m, v)


def _sum_pair(g0, g1, t, name):
    n, a, b = t.shape
    ta, spec = _lead_spec(a, b)

    def body(g0_ref, g1_ref, t_ref, o_ref):
        mine = jnp.where(lax.axis_index("c") == 0, g0_ref[...], g1_ref[...])
        o_ref[...] = (mine + t_ref[...]).astype(o_ref.dtype)

    return pl.pallas_call(
        body, name=name, grid=(n, a // ta), in_specs=[spec] * 3, out_specs=spec, out_shape=_sds(t.shape, WIRE),
        compiler_params=_cparams(2),
    )(g0, g1, t)


def _sum_chips(pair, t, name):
    _, a, b = t.shape
    ta = _pick8(a, max(16, _ELEMS_PER_BLOCK // b))

    def body(p_ref, t_ref, o_ref):
        me = 2 * lax.axis_index("x") + lax.axis_index("y")
        acc = p_ref[me].astype(F32)
        for k in range(3):
            acc = acc + t_ref[k].astype(F32)
        o_ref[...] = acc

    return pl.pallas_call(
        body, name=name, grid=(a // ta,),
        in_specs=[pl.BlockSpec((4, ta, b), lambda i: (0, i, 0)), pl.BlockSpec((3, ta, b), lambda i: (0, i, 0))],
        out_specs=pl.BlockSpec((ta, b), lambda i: (i, 0)), out_shape=_sds((a, b), F32), compiler_params=_cparams(1),
    )(pair, t)


_HBM = pl.BlockSpec(memory_space=pltpu.HBM)


def _place():
    x, y, c = lax.axis_index("x"), lax.axis_index("y"), lax.axis_index("c")
    chips = [(1 - x, y), (x, 1 - y), (1 - x, 1 - y)]
    return x, y, c, chips


def _remote(src, dst, ssem, rsem, to):
    return pltpu.make_async_remote_copy(src_ref=src, dst_ref=dst, send_sem=ssem, recv_sem=rsem, device_id=to,
                                        device_id_type=MESH_ID)


def _dma_sems(n):
    return pltpu.SemaphoreType.DMA((n,))


def _gather_weights(shards):
    n = len(shards)

    def body(*refs):
        w_refs, g_refs = refs[:n], refs[n:2 * n]
        ssem, rsem = refs[2 * n:]
        x, y, c, chips = _place()
        me = 2 * x + y
        sib = (x, y, 1 - c)
        owns = [_remote(w.at[l], g.at[l, me], ssem.at[6 * n + 2 * i + l], rsem.at[6 * n + 2 * i + l], sib)
                for i, (w, g) in enumerate(zip(w_refs, g_refs)) for l in range(2)]
        first = [_remote(w.at[c], g.at[c, me], ssem.at[3 * i + j], rsem.at[3 * i + j], (cx, cy, c))
                 for j, (cx, cy) in enumerate(chips) for i, (w, g) in enumerate(zip(w_refs, g_refs))]
        for cp in first + owns:
            cp.start()
        passed = []
        for j, (cx, cy) in enumerate(chips):
            for i, g in enumerate(g_refs):
                blk = g.at[c, 2 * cx + cy]
                _remote(blk, blk, ssem.at[3 * i + j], rsem.at[3 * i + j], sib).wait_recv()
                fw = _remote(blk, blk, ssem.at[3 * n + 3 * i + j], rsem.at[3 * n + 3 * i + j], sib)
                fw.start()
                passed.append(fw)
        for j, (cx, cy) in enumerate(chips):
            for i, g in enumerate(g_refs):
                blk = g.at[1 - c, 2 * cx + cy]
                _remote(blk, blk, ssem.at[3 * n + 3 * i + j], rsem.at[3 * n + 3 * i + j], sib).wait_recv()
        for cp in owns:
            cp.wait_recv()
        for cp in first + passed + owns:
            cp.wait_send()

    return pl.pallas_call(
        body, name="gather_weights", in_specs=[_HBM] * n, out_specs=[_HBM] * n,
        out_shape=[_sds((2, 4) + s.shape[1:], s.dtype) for s in shards],
        scratch_shapes=[_dma_sems(8 * n), _dma_sems(8 * n)],
    )(*shards)


def _sibling_halves(g0s, g1s):
    n = len(g0s)

    def body(*refs):
        g0_refs, g1_refs, t_refs = refs[:n], refs[n:2 * n], refs[2 * n:3 * n]
        ssem, rsem = refs[3 * n:]
        x, y, c, _ = _place()

        def swap(srcs):
            cps = [_remote(s, t, ssem.at[i], rsem.at[i], (x, y, 1 - c)) for i, (s, t) in enumerate(zip(srcs, t_refs))]
            for cp in cps:
                cp.start()
            for cp in cps:
                cp.wait()

        @pl.when(c == 0)
        def _():
            swap(g1_refs)

        @pl.when(c == 1)
        def _():
            swap(g0_refs)

    return pl.pallas_call(
        body, name="rs_sibling_halves", in_specs=[_HBM] * (2 * n), out_specs=[_HBM] * n,
        out_shape=[_sds(g.shape, g.dtype) for g in g0s], scratch_shapes=[_dma_sems(n), _dma_sems(n)],
    )(*g0s, *g1s)


def _chip_all_to_all(parts):
    n = len(parts)

    def body(*refs):
        a_refs, t_refs = refs[:n], refs[n:2 * n]
        ssem, rsem = refs[2 * n:]
        x, y, c, chips = _place()
        sends = [_remote(a.at[2 * cx + cy], t.at[j], ssem.at[3 * i + j], rsem.at[3 * i + j], (cx, cy, c))
                 for j, (cx, cy) in enumerate(chips) for i, (a, t) in enumerate(zip(a_refs, t_refs))]
        for cp in sends:
            cp.start()
        for cp in sends:
            cp.wait_recv()
        for cp in sends:
            cp.wait_send()

    return pl.pallas_call(
        body, name="rs_chip_all_to_all", in_specs=[_HBM] * n, out_specs=[_HBM] * n,
        out_shape=[_sds((3,) + p.shape[1:], p.dtype) for p in parts],
        scratch_shapes=[_dma_sems(3 * n), _dma_sems(3 * n)],
    )(*parts)


def _sibling_swap(reds):
    n = len(reds)

    def body(*refs):
        q_refs, o_refs = refs[:n], refs[n:2 * n]
        ssem, rsem = refs[2 * n:]
        x, y, c, _ = _place()
        cps = [_remote(q, o, ssem.at[i], rsem.at[i], (x, y, 1 - c)) for i, (q, o) in enumerate(zip(q_refs, o_refs))]
        for cp in cps:
            cp.start()
        for cp in cps:
            cp.wait()

    return pl.pallas_call(
        body, name="rs_sibling_swap", in_specs=[_HBM] * n, out_specs=[_HBM] * n,
        out_shape=[_sds(q.shape, q.dtype) for q in reds],
        scratch_shapes=[_dma_sems(n), _dma_sems(n)],
    )(*reds)


def _gather_small(s):
    R, _ = s.shape

    def body(s_ref, o_ref, ssem, rsem, lsem):
        x, y, c, _ = _place()
        me = 4 * x + 2 * y + c
        own = pltpu.make_async_copy(s_ref, o_ref.at[me], lsem)
        own.start()
        sends = []
        for k in range(1, 8):
            px, py, pc = x ^ (k >> 2), y ^ ((k >> 1) & 1), c ^ (k & 1)
            cp = _remote(s_ref, o_ref.at[me], ssem.at[k - 1], rsem.at[k - 1], (px, py, pc))
            cp.start()
            sends.append(cp)
        for k in range(1, 8):
            px, py, pc = x ^ (k >> 2), y ^ ((k >> 1) & 1), c ^ (k & 1)
            blk = o_ref.at[4 * px + 2 * py + pc]
            _remote(blk, blk, ssem.at[k - 1], rsem.at[k - 1], (px, py, pc)).wait_recv()
        for cp in sends:
            cp.wait_send()
        own.wait()

    vm = pl.BlockSpec(memory_space=pltpu.VMEM)
    return pl.pallas_call(
        body, name="gather_small", in_specs=[vm], out_specs=vm, out_shape=_sds((8, R, LANES), s.dtype),
        scratch_shapes=[pltpu.SemaphoreType.DMA((7,)), pltpu.SemaphoreType.DMA((7,)), pltpu.SemaphoreType.DMA],
    )(s)


_BIG = ("w_in", "mla_w_uq", "mla_w_ukv", "w_out", "ffn_w_gate", "ffn_w_up", "ffn_w_down")
_COL_SHARDED = ("w_in", "mla_w_uq", "mla_w_ukv", "ffn_w_gate", "ffn_w_up")
_SMALL = ("mla_q_norm", "mla_kv_norm", "gqa_q_norm", "gqa_k_norm", "rel_bias", "ln1_g", "ln1_b", "ln2_g", "ln2_b")


def _pack_flat(arrs, align):
    flat = jnp.concatenate([a.reshape(-1) for a in arrs])
    pad = (-flat.shape[0]) % align
    return jnp.pad(flat, (0, pad)) if pad else flat


def _unpack_flat(flat, shapes):
    out, off = [], 0
    for s in shapes:
        n = int(np.prod(s))
        out.append(flat[off:off + n].reshape(s))
        off += n
    return out


def _perm_gqa_rows(w):
    return jnp.concatenate([w[:832], w[896:960], w[832:896], w[960:]], axis=0)


def _local_step(x, target, W, small):
    S, D = x.shape
    depth = W["w_in"].shape[0]
    alpha = (2.0 * depth) ** 0.25
    in_idx, uq_idx, ukv_idx = _in_cols(), _uq_cols(), _ukv_cols()
    win = [_rows_from_shards(W["w_in"][l], in_idx) for l in range(depth)]
    wuq = [_rows_from_shards(W["mla_w_uq"][l], uq_idx) for l in range(depth)]
    wukv = [_rows_from_shards(W["mla_w_ukv"][l], ukv_idx) for l in range(depth)]
    wout = [_perm_gqa_rows(W["w_out"][l].reshape(-1, D)) for l in range(depth)]
    wg, wu, wdn = W["ffn_w_gate"], W["ffn_w_up"], W["ffn_w_down"]

    tm, tg = _rope_tables(S)
    j256, j384 = _head_ones(256), _head_ones(384)
    mla_scale = (64 + MLA_ROPE_DIM) ** -0.5
    branches = []
    for (_, dil) in DIL_BRANCHES:
        L = S // dil
        tq = min(256, L)
        idx = jnp.asarray(_branch_bucket_idx(tq, dil))
        branches.append((dil, L, tq, idx))
    tabs = [_bias_expand(idx, small["rel_bias"], name=f"bias_expand_{b}") for b, (_, _, _, idx) in enumerate(branches)]

    def view(a, dil):
        return a.reshape(a.shape[0] // dil, dil * a.shape[1])

    def padded_view(a, dil):
        z = jnp.zeros((DIL_HALF * dil, a.shape[1]), a.dtype)
        return view(jnp.concatenate([z, a, z], axis=0), dil)

    def unpad_view(a, dil):
        return a[DIL_HALF:a.shape[0] - DIL_HALF].reshape(S, 384)

    saved = []
    xf, xb = x, x.astype(CDT)
    for l in range(depth):
        gq, gkv = small["mla_q_norm"][l][None], small["mla_kv_norm"][l][None]
        ggq = jnp.tile(small["gqa_q_norm"][l], 4)[None]
        ggk = jnp.tile(small["gqa_k_norm"][l], 2)[None]
        h = _mm(xb, win[l], tb=True, name="mm_in")
        cq, ckv, kr, qd, kd, vd, qg, kg, vg = _prep_fwd(h, gq, gkv, ggq, ggk, tm, tg, j256)
        qa = _mm(cq, wuq[l], tb=True, name="mm_uq")
        kvp = _mm(ckv, wukv[l], tb=True, out_dtype=CDT, name="mm_ukv")
        qm, km = _mla_prep_fwd(qa, kvp, kr, tm, mla_scale)
        oa, lsa = _attn_fwd(qm, km, kvp, split=True, npairs=3, kblk=lambda p: p, vblk=lambda p: 6 + p,
                            name="mla_attn_fwd")
        oc, lsc = _attn_fwd(qg, kg, vg, split=False, npairs=2, kblk=lambda p: 0, vblk=lambda p: 0,
                            name="gqa_attn_fwd")
        obs, lbs, kvs = [], [], []
        for b, (dil, L, tq, _) in enumerate(branches):
            kpv, vpv = padded_view(kd, dil), padded_view(vd, dil)
            o_b, l_b = _dil_fwd(view(qd, dil), kpv, vpv, tabs[b], dil=dil, L=L, tq=tq, name=f"dil_fwd_{b}")
            obs.append(o_b.reshape(S, 384))
            lbs.append(l_b.reshape(S, 384))
            kvs.append((kpv, vpv))
        ob = _dil_mix_fwd(obs, lbs)
        cat = jnp.concatenate([oa, ob, oc], axis=1)
        mix = _mm(cat, wout[l], name="mm_out")
        x1, x1b, z1 = _ln_fwd(xf, mix, small["ln1_g"][l][None], small["ln1_b"][l][None], alpha, name="ln1_fwd")
        g3, u3, act = _ffn_up(x1b, wg[l], wu[l])
        ff = _mm(act, wdn[l], ga=True, gb=True, name="mm_down")
        x2, x2b, z2 = _ln_fwd(x1, ff, small["ln2_g"][l][None], small["ln2_b"][l][None], alpha, name="ln2_fwd")
        saved.append(dict(xb=xb, h=h, cq=cq, ckv=ckv, qd=qd, qg=qg, kg=kg, vg=vg, kvp=kvp, qm=qm, km=km, oa=oa, lsa=lsa,
                          oc=oc, lsc=lsc, obs=obs, lbs=lbs, kvs=kvs, cat=cat, z1=z1, x1b=x1b, g3=g3, u3=u3, act=act, z2=z2,
                          gq=gq, gkv=gkv, ggq=ggq, ggk=ggk))
        xf, xb = x2, x2b

    dy, loss = _loss_kernel(xf, target)

    gW = {k: [None] * depth for k in _BIG}
    gS = {k: [None] * depth for k in ("mla_q_norm", "mla_kv_norm", "gqa_q_norm", "gqa_k_norm", "ln1_g", "ln1_b", "ln2_g",
                                      "ln2_b")}
    g_rel = None
    dya, dyb = dy, None
    for l in reversed(range(depth)):
        sv = saved[l]
        dz2, dz2b, gS["ln2_g"][l], gS["ln2_b"][l] = _ln_bwd(dya, dyb, sv["z2"], small["ln2_g"][l][None], alpha,
                                                             name="ln2_bwd" if dyb is not None else "ln2_bwd_last")
        gW["ffn_w_down"][l] = _mm(sv["act"], dz2b, ta=True, ga=True, go=True, name="mm_down_dw")
        dg3, du3 = _ffn_down_dx(dz2b, wdn[l], sv["g3"], sv["u3"])
        gW["ffn_w_gate"][l] = _mm(dg3, sv["x1b"], ta=True, ga=True, go=True, name="mm_gate_dw")
        gW["ffn_w_up"][l] = _mm(du3, sv["x1b"], ta=True, ga=True, go=True, name="mm_up_dw")
        dx1 = _ffn_up_dx(dg3, du3, wg[l], wu[l])
        dz1, dz1b, gS["ln1_g"][l], gS["ln1_b"][l] = _ln_bwd(dx1, dz2, sv["z1"], small["ln1_g"][l][None], alpha,
                                                             name="ln1_bwd")
        gW["w_out"][l] = _perm_gqa_rows(_mm(sv["cat"], dz1b, ta=True, name="mm_out_dw")).reshape(4, -1, D)
        dcat = _mm(dz1b, wout[l], tb=True, name="mm_out_dx")
        dqg, dkg, dvg = _attn_bwd(sv["qg"], sv["kg"], sv["vg"], dcat, sv["oc"], sv["lsc"], split=False, npairs=2,
                                  kblk=lambda p: 0, vblk=lambda p: 0, doblk=lambda p: 6 + p, shared_kv=True,
                                  name="gqa_attn_bwd")
        dqm, dkm, dvm = _attn_bwd(sv["qm"], sv["km"], sv["kvp"], dcat, sv["oa"], sv["lsa"], split=True, npairs=3,
                                  kblk=lambda p: p, vblk=lambda p: 6 + p, doblk=lambda p: p, shared_kv=False,
                                  name="mla_attn_bwd")
        dqa, dkvp, dkr = _mla_prep_bwd(dqm, dkm, dvm, tm, mla_scale)
        gW["mla_w_uq"][l] = _rows_to_shards(_mm(dqa, sv["cq"], ta=True, name="mm_uq_dw"), uq_idx, MLA_HEADS * 96)
        dcq = _mm(dqa, wuq[l], name="mm_uq_dx")
        gW["mla_w_ukv"][l] = _rows_to_shards(_mm(dkvp, sv["ckv"], ta=True, name="mm_ukv_dw"), ukv_idx, MLA_HEADS * 128)
        dckv = _mm(dkvp, wukv[l], name="mm_ukv_dx")
        mixb = _dil_mix_bwd(dcat, sv["obs"], sv["lbs"], j384)
        ddq, ddk, ddv = [], [], []
        for b, (dil, L, tq, idx) in enumerate(branches):
            kpv, vpv = sv["kvs"][b]
            dq_b, dk_b, dv_b, dtab = _dil_bwd(view(sv["qd"], dil), kpv, vpv, tabs[b], view(mixb[b], dil),
                                              view(sv["lbs"][b], dil), view(mixb[3 + b], dil), dil=dil, L=L, tq=tq,
                                              name=f"dil_bwd_{b}")
            ddq.append(dq_b.reshape(S, 384))
            ddk.append(unpad_view(dk_b, dil))
            ddv.append(unpad_view(dv_b, dil))
            g_b = _bias_reduce(idx, dtab, name=f"bias_reduce_{b}")[:, :, 0].T
            g_rel = g_b if g_rel is None else g_rel + g_b
        dh, n1, n2, n3, n4 = _prep_bwd(sv["h"], dcq, dckv, dkr, ddq, ddk, ddv, dqg, dkg, dvg, sv["gq"], sv["gkv"],
                                       sv["ggq"], sv["ggk"], tg, j256)
        gS["mla_q_norm"][l], gS["mla_kv_norm"][l] = n1[0], n2[0]
        gS["gqa_q_norm"][l] = n3[0].reshape(4, 64).sum(0)
        gS["gqa_k_norm"][l] = n4[0].reshape(2, 64).sum(0)
        gW["w_in"][l] = _rows_to_shards(_mm(dh, sv["xb"], ta=True, name="mm_in_dw"), in_idx, IN_W)
        dya = _mm(dh, win[l], name="mm_in_dx")
        dyb = dz1
    grad_x = _axpy(dya, dyb, alpha, name="grad_x")

    gsmall = {k: jnp.stack([a.reshape(-1) for a in v]) for k, v in gS.items()}
    gsmall["rel_bias"] = g_rel
    return loss, grad_x, gW, gsmall


_ORDER = ("w_in", "mla_q_norm", "mla_kv_norm", "mla_w_uq", "mla_w_ukv", "gqa_q_norm", "gqa_k_norm", "rel_bias", "w_out",
          "ln1_g", "ln1_b", "ffn_w_gate", "ffn_w_up", "ffn_w_down", "ln2_g", "ln2_b")


def kernel(x, w_in, mla_q_norm, mla_kv_norm, mla_w_uq, mla_w_ukv, gqa_q_norm, gqa_k_norm, rel_bias, w_out, ln1_g, ln1_b, ffn_w_gate, ffn_w_up, ffn_w_down, ln2_g, ln2_b, loss_target, m_w_in, m_mla_q_norm, m_mla_kv_norm, m_mla_w_uq, m_mla_w_ukv, m_gqa_q_norm, m_gqa_k_norm, m_rel_bias, m_w_out, m_ln1_g, m_ln1_b, m_ffn_w_gate, m_ffn_w_up, m_ffn_w_down, m_ln2_g, m_ln2_b, v_w_in, v_mla_q_norm, v_mla_kv_norm, v_mla_w_uq, v_mla_w_ukv, v_gqa_q_norm, v_gqa_k_norm, v_rel_bias, v_w_out, v_ln1_g, v_ln1_b, v_ffn_w_gate, v_ffn_w_up, v_ffn_w_down, v_ln2_g, v_ln2_b):
    wts = dict(zip(_ORDER, (w_in, mla_q_norm, mla_kv_norm, mla_w_uq, mla_w_ukv, gqa_q_norm, gqa_k_norm, rel_bias, w_out,
                            ln1_g, ln1_b, ffn_w_gate, ffn_w_up, ffn_w_down, ln2_g, ln2_b)))
    mom = dict(zip(_ORDER, (m_w_in, m_mla_q_norm, m_mla_kv_norm, m_mla_w_uq, m_mla_w_ukv, m_gqa_q_norm, m_gqa_k_norm,
                            m_rel_bias, m_w_out, m_ln1_g, m_ln1_b, m_ffn_w_gate, m_ffn_w_up, m_ffn_w_down, m_ln2_g,
                            m_ln2_b)))
    var = dict(zip(_ORDER, (v_w_in, v_mla_q_norm, v_mla_kv_norm, v_mla_w_uq, v_mla_w_ukv, v_gqa_q_norm, v_gqa_k_norm,
                            v_rel_bias, v_w_out, v_ln1_g, v_ln1_b, v_ffn_w_gate, v_ffn_w_up, v_ffn_w_down, v_ln2_g,
                            v_ln2_b)))
    small_shapes = [wts[n].shape for n in _SMALL]
    for d in (wts, mom, var):
        for n in _COL_SHARDED:
            d[n] = d[n].transpose(0, 2, 1)

    gathered = _gather_weights([wts[n].astype(WIRE) for n in _BIG])
    full = {n: g.astype(CDT) for n, g in zip(_BIG, gathered)}

    small = {n: wts[n] for n in _SMALL}
    loss, grad_x, gbig, gsmall = _local_step(x[0], loss_target[0], full, small)

    g0s, g1s = [gbig[n][0] for n in _BIG], [gbig[n][1] for n in _BIG]
    theirs = _sibling_halves(g0s, g1s)
    pairs = [_sum_pair(g0, g1, t, name=f"rs_pair_sum_{n}") for n, g0, g1, t in zip(_BIG, g0s, g1s, theirs)]
    reds = [_sum_chips(p, t, name=f"rs_sum_chips_{n}") for n, p, t in zip(_BIG, pairs, _chip_all_to_all(pairs))]
    sibs = _sibling_swap(reds)

    sflat = _pack_flat([gsmall[n].reshape(-1) for n in _SMALL], 8 * LANES)
    rs = sflat.shape[0] // LANES
    sall = _gather_small(sflat.reshape(rs, LANES))

    def packed(d):
        return _pack_flat([d[n] for n in _SMALL], 8 * LANES).reshape(rs, LANES)

    outs = {tag: {} for tag in ("grad", "delta", "new_m", "new_v")}
    for n, red, sib in zip(_BIG, reds, sibs):
        res = _adamw(wts[n], red, sib, mom[n], var[n], name=f"adamw_{n}")
        for tag, r in zip(("grad", "delta", "new_m", "new_v"), res):
            outs[tag][n] = r.transpose(0, 2, 1) if n in _COL_SHARDED else r
    for tag, smallflat in zip(("grad", "delta", "new_m", "new_v"), _adamw_small(packed(wts), sall, packed(mom), packed(var))):
        outs[tag].update(zip(_SMALL, _unpack_flat(smallflat.reshape(-1), small_shapes)))

    total = lax.psum(loss[0, 0], ("x", "y", "c"))
    return (total, grad_x[None], *[outs["grad"][n] for n in _ORDER], *[outs["delta"][n] for n in _ORDER],
            *[outs["new_m"][n] for n in _ORDER], *[outs["new_v"][n] for n in _ORDER])
```

```python
import functools
import math

import numpy as np
import jax
import jax.numpy as jnp
from jax import lax
from jax.experimental import pallas as pl
from jax.experimental.pallas import tpu as pltpu

F32 = jnp.float32
CDT = jnp.bfloat16
WIRE = jnp.bfloat16

HEAD_DIM = 64
GRID_W = 64
ROPE_THETA = 10000.0
MLA_HEADS = 6
MLA_Q_RANK = 256
MLA_KV_RANK = 128
MLA_ROPE_DIM = 32
DIL_HEADS = 6
DIL_BRANCHES = ((128, 1), (512, 4), (2048, 16))
DIL_HALF = 64
GQA_Q_HEADS = 4
REL_BUCKETS = 32
REL_MAX_DIST = 1024
NEG_INF = -1e30
LANES = 128
VMEM_LIMIT = 56 * 1024 * 1024

ADAM_LR, ADAM_B1, ADAM_B2, ADAM_EPS, ADAM_WD, ADAM_STEP = 0.001, 0.9, 0.999, 1e-08, 0.01, 10

C_CQ, C_CKV, C_KR, C_DQ, C_DK, C_DV, C_GQ, C_GK, C_GV, IN_P = 0, 256, 384, 512, 896, 1280, 1664, 1920, 2048, 2176
IN_W = 2080
MESH_ID = pl.DeviceIdType.MESH


def _cparams(n_axes, vmem=VMEM_LIMIT):
    return pltpu.CompilerParams(dimension_semantics=("arbitrary",) * n_axes, vmem_limit_bytes=vmem)


MAX_WHOLE_DIM = 2304


def _pick(n, target):
    best = None
    for t in range(LANES, min(n, target) + 1, LANES):
        if n % t == 0:
            best = t
    if best is not None and (2 * best >= target or n > MAX_WHOLE_DIM):
        return best
    return n


def _sds(shape, dtype):
    return jax.ShapeDtypeStruct(tuple(shape), dtype)


def _in_cols():
    idx = -np.ones((IN_P,), np.int64)
    idx[C_CQ:C_CQ + 256] = np.arange(0, 256)
    idx[C_CKV:C_CKV + 128] = np.arange(256, 384)
    idx[C_KR + 64:C_KR + 96] = np.arange(384, 416)
    idx[C_DQ:C_DQ + 1152] = np.arange(416, 1568)
    gq = 1568 + (np.array([0, 2, 1, 3])[:, None] * 64 + np.arange(64)[None, :]).reshape(-1)
    idx[C_GQ:C_GQ + 256] = gq
    idx[C_GK:C_GK + 256] = np.arange(1824, 2080)
    return idx


def _uq_cols():
    idx = -np.ones((MLA_HEADS * 128,), np.int64)
    for h in range(MLA_HEADS):
        idx[h * 128:h * 128 + 96] = np.arange(96 * h, 96 * h + 96)
    return idx


def _ukv_cols():
    idx = -np.ones((MLA_HEADS * 128 + MLA_HEADS * 64,), np.int64)
    for h in range(MLA_HEADS):
        idx[h * 128:h * 128 + 64] = np.arange(128 * h, 128 * h + 64)
        idx[768 + h * 64:768 + h * 64 + 64] = np.arange(128 * h + 64, 128 * h + 128)
    return idx


def _out_rows():
    idx = np.arange(1024)
    idx[768:1024] = 768 + (np.array([0, 2, 1, 3])[:, None] * 64 + np.arange(64)[None, :]).reshape(-1)
    return idx


def _runs(idx):
    out, i = [], 0
    while i < len(idx):
        j = i + 1
        while j < len(idx) and ((idx[i] < 0 and idx[j] < 0) or (idx[i] >= 0 and idx[j] == idx[j - 1] + 1)):
            j += 1
        out.append((int(idx[i]), j - i))
        i = j
    return out


def _rows_from_shards(sh, idx):
    _, cs, r = sh.shape
    pieces = []
    for first, ln in _runs(idx):
        if first < 0:
            pieces.append(jnp.zeros((ln, r), sh.dtype))
            continue
        while ln > 0:
            k, off = divmod(first, cs)
            take = min(ln, cs - off)
            pieces.append(sh[k, off:off + take, :])
            first, ln = first + take, ln - take
    return jnp.concatenate(pieces, axis=0)


def _rows_to_shards(wp, idx, n):
    inv = np.zeros((n,), np.int64)
    pos = np.nonzero(idx >= 0)[0]
    inv[idx[pos]] = pos
    cs = n // 4
    shards = []
    for k in range(4):
        pieces = [wp[first:first + ln, :] for first, ln in _runs(inv[k * cs:(k + 1) * cs])]
        shards.append(jnp.concatenate(pieces, axis=0))
    return jnp.stack(shards)


def _t5_bucket_np(rel):
    nb = REL_BUCKETS // 2
    exact = nb // 2
    ret = np.where(rel > 0, nb, 0)
    n = np.abs(rel)
    nf = np.maximum(n, 1).astype(np.float32)
    large = exact + (np.log(nf / np.float32(exact)) / np.float32(math.log(REL_MAX_DIST / exact))
                     * np.float32(nb - exact)).astype(np.int32)
    large = np.minimum(large, nb - 1)
    return ret + np.where(n < exact, n, large)


def _branch_bucket_idx(tq, dil):
    kw = tq + 2 * DIL_HALF
    rel = np.arange(kw)[None, :] - DIL_HALF - np.arange(tq)[:, None]
    idx = _t5_bucket_np(rel * dil)
    return np.where(np.abs(rel) <= DIL_HALF, idx, -1).astype(np.int32)


def _rope_tables(S):
    inv = ROPE_THETA ** (-jnp.arange(0, 32, 2, dtype=F32) / 32)
    t = jnp.arange(S)
    pos = t.astype(F32)
    row = (t // GRID_W).astype(F32)
    col = (t % GRID_W).astype(F32)
    lane = np.arange(LANES)
    wm = lane - 64
    is_rope = (lane >= 64) & (lane < 96)
    ang = pos[:, None] * inv[np.where(is_rope, wm % 16, 0)][None, :]
    cm = jnp.where(is_rope[None], jnp.cos(ang), 1.0)
    smm = jnp.where((is_rope & (wm < 16))[None], -jnp.sin(ang), 0.0)
    spm = jnp.where((is_rope & (wm >= 16))[None], jnp.sin(ang), 0.0)
    g = lane % 64
    w = g % 32
    angg = jnp.where((g < 32)[None], row[:, None], col[:, None]) * inv[w % 16][None, :]
    cg = jnp.cos(angg)
    smg = jnp.where((w < 16)[None], -jnp.sin(angg), 0.0)
    spg = jnp.where((w >= 16)[None], jnp.sin(angg), 0.0)
    return (cm, smm, spm), (cg, smg, spg)


def _lanes(t, width):
    return t if width == LANES else jnp.concatenate([t] * (width // LANES), axis=1)


def _rope(x, tabs):
    c, sm, sp = (_lanes(t, x.shape[1]) for t in tabs)
    w = x.shape[1]
    return x * c + pltpu.roll(x, w - 16, 1) * sm + pltpu.roll(x, 16, 1) * sp


def _rope_t(dy, tabs):
    c, sm, sp = (_lanes(t, dy.shape[1]) for t in tabs)
    w = dy.shape[1]
    return dy * c + pltpu.roll(dy * sm, 16, 1) + pltpu.roll(dy * sp, w - 16, 1)


def _head_ones(width):
    i = np.arange(width)
    return jnp.asarray((i[:, None] // HEAD_DIM == i[None, :] // HEAD_DIM).astype(np.float32))


def _headsum(x, j):
    return jnp.dot(x, j, preferred_element_type=F32, precision=lax.Precision.HIGHEST)


def _mm(a, b, *, ta=False, tb=False, ga=False, gb=False, go=False, out_dtype=F32, add=None, name):
    G = a.shape[0] if ga else (b.shape[0] if gb else 1)
    a2 = a.shape[1:] if ga else a.shape
    b2 = b.shape[1:] if gb else b.shape
    K, M = a2 if ta else a2[::-1]
    N = b2[0] if tb else b2[1]
    assert (b2[1] if tb else b2[0]) == K
    tm, tn, tk = _pick(M, 1024), _pick(N, 1024), _pick(K, 1024)
    if tm * tn > 1024 * 1152:
        tm = _pick(M, 512)
    nk = K // tk
    steps = nk if (go or G == 1) else G * nk
    dn = (((0 if ta else 1,), (1 if tb else 0,)), ((), ()))

    def body(a_ref, b_ref, *rest):
        rest = list(rest)
        add_ref = rest.pop(0) if add is not None else None
        o_ref = rest.pop(0)
        part = lax.dot_general(a_ref[...], b_ref[...], dn, preferred_element_type=F32)
        if steps == 1:
            if add_ref is not None:
                part = part + add_ref[...]
            o_ref[...] = part.astype(o_ref.dtype)
            return
        acc_ref, = rest
        s = pl.program_id(3)

        @pl.when(s == 0)
        def _():
            acc_ref[...] = part if add_ref is None else part + add_ref[...]

        @pl.when(s > 0)
        def _():
            acc_ref[...] += part

        @pl.when(s == steps - 1)
        def _():
            o_ref[...] = acc_ref[...].astype(o_ref.dtype)

    def grp(g, s):
        return g if go else s // nk

    def kk(s):
        return s if steps == nk else s % nk

    def spec(grouped, block, index):
        if grouped:
            return pl.BlockSpec((None,) + block, lambda g, i, j, s: (grp(g, s),) + index(i, j, s))
        return pl.BlockSpec(block, lambda g, i, j, s: index(i, j, s))

    a_spec = (spec(ga, (tk, tm), lambda i, j, s: (kk(s), i)) if ta else spec(ga, (tm, tk), lambda i, j, s: (i, kk(s))))
    b_spec = (spec(gb, (tn, tk), lambda i, j, s: (j, kk(s))) if tb else spec(gb, (tk, tn), lambda i, j, s: (kk(s), j)))
    o_spec = spec(go, (tm, tn), lambda i, j, s: (i, j))
    return pl.pallas_call(
        body, name=name, grid=(G if go else 1, M // tm, N // tn, steps),
        in_specs=[a_spec, b_spec] + ([o_spec] if add is not None else []), out_specs=o_spec,
        out_shape=_sds(((G,) if go else ()) + (M, N), out_dtype),
        scratch_shapes=[pltpu.VMEM((tm, tn), F32)] if steps > 1 else [],
        compiler_params=_cparams(4),
    )(*([a, b] + ([add] if add is not None else [])))


def _row(ts, w, cb=0):
    return pl.BlockSpec((ts, w), lambda i: (i, cb))


def _full(shape):
    nd = len(shape)
    return pl.BlockSpec(tuple(shape), lambda i: (0,) * nd)


def _rms_fwd(x, g, eps=1e-6):
    r = lax.rsqrt(jnp.mean(x * x, axis=-1, keepdims=True) + eps)
    return x * r * g


def _rms_bwd(x, g, dy, eps=1e-6):
    r = lax.rsqrt(jnp.mean(x * x, axis=-1, keepdims=True) + eps)
    gdy = g * dy
    dx = r * gdy - x * (r * r * r) * jnp.mean(x * gdy, axis=-1, keepdims=True)
    return dx, x * r * dy


def _rms_head_fwd(x, g, j, eps=1e-6):
    r = lax.rsqrt(_headsum(x * x, j) * (1.0 / HEAD_DIM) + eps)
    return x * r * g


def _rms_head_bwd(x, g, dy, j, eps=1e-6):
    r = lax.rsqrt(_headsum(x * x, j) * (1.0 / HEAD_DIM) + eps)
    gdy = g * dy
    dx = r * gdy - x * (r * r * r) * (_headsum(x * gdy, j) * (1.0 / HEAD_DIM))
    return dx, x * r * dy


DIL_STRIDES = tuple(d for _, d in DIL_BRANCHES if d > 1)
DIL_W = DIL_HEADS * HEAD_DIM


def _res_spec(d, n, pad_blocks=0):
    return pl.BlockSpec((d, n, DIL_W), lambda i: (0, i + pad_blocks, 0))


def _prep_fwd(h, gq, gkv, ggq, ggk, tm, tg, j256):
    S = h.shape[0]
    ts = min(256, S)
    scale = HEAD_DIM ** -0.5
    nres = len(DIL_STRIDES)

    def body(*refs):
        (h_ref, gq_ref, gkv_ref, ggq_ref, ggk_ref, cm, smm, spm, cg, smg, spg, j_ref), refs = refs[:12], refs[12:]
        refs = refs[2 * nres:]
        (cq_o, ckv_o, kr_o, dq_o, dk_o, dv_o, gq_o, gk_o, gv_o), res_o = refs[:9], refs[9:-1]
        st = refs[-1]
        tabm = (cm[...], smm[...], spm[...])
        tabg = (cg[...], smg[...], spg[...])
        cq_o[...] = _rms_fwd(h_ref[:, C_CQ:C_CQ + 256], gq_ref[...]).astype(CDT)
        ckv_o[...] = _rms_fwd(h_ref[:, C_CKV:C_CKV + 128], gkv_ref[...]).astype(CDT)
        kr_o[...] = _rope(h_ref[:, C_KR:C_KR + 128], tabm).astype(CDT)
        dq_o[...] = (h_ref[:, C_DQ:C_DQ + 384] * scale).astype(CDT)
        dk_o[...] = h_ref[:, C_DK:C_DK + 384].astype(CDT)
        dv_o[...] = h_ref[:, C_DV:C_DV + 384].astype(CDT)
        for j, lanes in _lane_blocks(3 * DIL_W):
            st[j] = h_ref[:, C_DQ + lanes.start:C_DQ + lanes.stop] * (scale if j < 3 else 1.0)
        for bi, d in enumerate(DIL_STRIDES):
            for c in range(d):
                rows = pl.ds(c, ts // d, stride=d)
                for j, lanes in _lane_blocks(3 * DIL_W):
                    res_o[3 * bi + j // 3][c, :, (j % 3) * LANES:(j % 3 + 1) * LANES] = st.at[j][rows, :].astype(CDT)
        qn = _rms_head_fwd(h_ref[:, C_GQ:C_GQ + 256], ggq_ref[...], j_ref[...])
        gq_o[...] = (_rope(qn, tabg) * scale).astype(CDT)
        kn = _rms_head_fwd(h_ref[:, C_GK:C_GK + 128], ggk_ref[...], j_ref[0:128, 0:128])
        gk_o[...] = _rope(kn, tabg).astype(CDT)
        gv_o[...] = h_ref[:, C_GV:C_GV + 128].astype(CDT)

    widths = (256, 128, 128, 384, 384, 384, 256, 128, 128)
    out_specs = [_row(ts, w) for w in widths]
    out_shape = [_sds((S, w), CDT) for w in widths]
    zeros, aliases = [], {}
    for d in DIL_STRIDES:
        n, L = ts // d, S // d
        out_specs += [_res_spec(d, n), _res_spec(d, n, DIL_HALF // n), _res_spec(d, n, DIL_HALF // n)]
        out_shape += [_sds((d, L, DIL_W), CDT)] + [_sds((d, L + 2 * DIL_HALF, DIL_W), CDT)] * 2
        for t in range(2):
            aliases[12 + len(zeros)] = len(out_shape) - 2 + t
            zeros.append(jnp.zeros((d, L + 2 * DIL_HALF, DIL_W), CDT))
    return pl.pallas_call(
        body, name="prep_fwd", grid=(S // ts,),
        in_specs=[_row(ts, IN_P), _full(gq.shape), _full(gkv.shape), _full(ggq.shape), _full(ggk.shape)]
        + [_row(ts, LANES)] * 6 + [_full(j256.shape)] + [pl.BlockSpec(memory_space=pl.ANY)] * len(zeros),
        out_specs=out_specs, out_shape=out_shape, input_output_aliases=aliases,
        scratch_shapes=[pltpu.VMEM((3 * DIL_W // LANES, ts, LANES), F32)],
        compiler_params=_cparams(1),
    )(h, gq, gkv, ggq, ggk, *tm, *tg, j256, *zeros)


def _prep_bwd(h, dcq, dckv, dkr, ddq, ddk, ddv, dgq, dgk, dgv, gq, gkv, ggq, ggk, tg, j256):
    S = h.shape[0]
    ts = min(256, S)
    scale = HEAD_DIM ** -0.5

    def body(h_ref, dcq_r, dckv_r, dkr_r, q1, q2, q3, k1, k2, k3, v1, v2, v3, dgq_r, dgk_r, dgv_r,
             gq_ref, gkv_ref, ggq_ref, ggk_ref, cg, smg, spg, j_ref,
             dh_o, ngq_o, ngkv_o, nggq_o, nggk_o, *scr):
        tabg = (cg[...], smg[...], spg[...])
        first = pl.program_id(0) == 0
        scr, = scr
        d2, d3 = DIL_STRIDES
        dq = q1[...] + _by_token(q2, scr, d2) + _by_token(q3, scr, d3)
        dk = k1[...] + _by_token(k2, scr, d2) + _by_token(k3, scr, d3)
        dv = v1[...] + _by_token(v2, scr, d2) + _by_token(v3, scr, d3)

        def acc(o_ref, val):
            s = jnp.sum(val, axis=0, keepdims=True)

            @pl.when(first)
            def _():
                o_ref[...] = s

            @pl.when(jnp.logical_not(first))
            def _():
                o_ref[...] += s

        dx, dg = _rms_bwd(h_ref[:, C_CQ:C_CQ + 256], gq_ref[...], dcq_r[...])
        dh_o[:, C_CQ:C_CQ + 256] = dx.astype(CDT)
        acc(ngq_o, dg)
        dx, dg = _rms_bwd(h_ref[:, C_CKV:C_CKV + 128], gkv_ref[...], dckv_r[...])
        dh_o[:, C_CKV:C_CKV + 128] = dx.astype(CDT)
        acc(ngkv_o, dg)
        dh_o[:, C_KR:C_KR + 128] = dkr_r[...].astype(CDT)
        dh_o[:, C_DQ:C_DQ + 384] = (dq * scale).astype(CDT)
        dh_o[:, C_DK:C_DK + 384] = dk.astype(CDT)
        dh_o[:, C_DV:C_DV + 384] = dv.astype(CDT)
        dqn = _rope_t(dgq_r[...] * scale, tabg)
        dx, dg = _rms_head_bwd(h_ref[:, C_GQ:C_GQ + 256], ggq_ref[...], dqn, j_ref[...])
        dh_o[:, C_GQ:C_GQ + 256] = dx.astype(CDT)
        acc(nggq_o, dg)
        dkn = _rope_t(dgk_r[...], tabg)
        dx, dg = _rms_head_bwd(h_ref[:, C_GK:C_GK + 128], ggk_ref[...], dkn, j_ref[0:128, 0:128])
        dh_o[:, C_GK:C_GK + 128] = dx.astype(CDT)
        acc(nggk_o, dg)
        dh_o[:, C_GV:C_GV + 128] = dgv_r[...].astype(CDT)

    d2, d3 = DIL_STRIDES
    n2, n3 = ts // d2, ts // d3
    tok = _row(ts, DIL_W)
    return pl.pallas_call(
        body, name="prep_bwd", grid=(S // ts,),
        in_specs=[_row(ts, IN_P), _row(ts, 256), _row(ts, 128), _row(ts, 128)]
        + [tok, _res_spec(d2, n2), _res_spec(d3, n3)]
        + [tok, _res_spec(d2, n2, DIL_HALF // n2), _res_spec(d3, n3, DIL_HALF // n3)] * 2
        + [_row(ts, 256), _row(ts, 128), _row(ts, 128)]
        + [_full(gq.shape), _full(gkv.shape), _full(ggq.shape), _full(ggk.shape)] + [_row(ts, LANES)] * 3
        + [_full(j256.shape)],
        out_specs=[_row(ts, IN_P), _full((1, 256)), _full((1, 128)), _full((1, 256)), _full((1, 128))],
        out_shape=[_sds((S, IN_P), CDT), _sds((1, 256), F32), _sds((1, 128), F32), _sds((1, 256), F32),
                   _sds((1, 128), F32)],
        scratch_shapes=[_TOKEN_SCRATCH(ts)],
        compiler_params=_cparams(1),
    )(h, dcq, dckv, dkr, *ddq, *ddk, *ddv, dgq, dgk, dgv, gq, gkv, ggq, ggk, *tg, j256)


def _lane_blocks(width):
    return [(j, slice(j * LANES, (j + 1) * LANES)) for j in range(width // LANES)]


_TOKEN_SCRATCH = lambda ts: pltpu.VMEM((DIL_W // LANES, ts, LANES), F32)


def _by_token(res_ref, scr_ref, d):
    n = res_ref.shape[1]
    if d == 1:
        return res_ref[0].astype(F32)
    for c in range(d):
        for j, lanes in _lane_blocks(res_ref.shape[2]):
            scr_ref.at[j][pl.ds(c, n, stride=d), :] = res_ref[c, :, lanes].astype(F32)
    return jnp.concatenate([scr_ref[j] for j, _ in _lane_blocks(res_ref.shape[2])], axis=1)


def _by_residue(val, scr_ref, out_ref, d):
    n = out_ref.shape[1]
    if d == 1:
        out_ref[0] = val.astype(out_ref.dtype)
        return
    for j, lanes in _lane_blocks(out_ref.shape[2]):
        scr_ref[j] = val[:, lanes]
    for c in range(d):
        for j, lanes in _lane_blocks(out_ref.shape[2]):
            out_ref[c, :, lanes] = scr_ref.at[j][pl.ds(c, n, stride=d), :].astype(out_ref.dtype)


def _mla_prep_fwd(qa, kvp, kr, tm, scale):
    S = qa.shape[0]
    ts = min(256, S)

    def body(qa_ref, kv_ref, kr_ref, cm, smm, spm, q_o, k_o):
        tabm = (cm[...], smm[...], spm[...])
        q_o[...] = (_rope(qa_ref[...], tabm) * scale).astype(CDT)
        k_o[...] = kv_ref[:, 0:768] + _lanes(kr_ref[...], 768)

    return pl.pallas_call(
        body, name="mla_prep_fwd", grid=(S // ts,),
        in_specs=[_row(ts, 768), _row(ts, 1152), _row(ts, 128)] + [_row(ts, LANES)] * 3,
        out_specs=[_row(ts, 768)] * 2, out_shape=[_sds((S, 768), CDT)] * 2,
        compiler_params=_cparams(1),
    )(qa, kvp, kr, *tm)


def _mla_prep_bwd(dq, dk, dv, tm, scale):
    S = dq.shape[0]
    ts = min(256, S)

    def body(dq_ref, dk_ref, dv_ref, cm, smm, spm, dqa_o, dkv_o, dkr_o):
        tabm = (cm[...], smm[...], spm[...])
        lane = lax.broadcasted_iota(jnp.int32, (1, LANES), 1)
        dqa_o[...] = _rope_t(dq_ref[...] * scale, tabm).astype(CDT)
        dkr = jnp.zeros((ts, LANES), F32)
        for hd in range(MLA_HEADS):
            blk = dk_ref[:, hd * 128:(hd + 1) * 128]
            dkv_o[:, hd * 128:(hd + 1) * 128] = jnp.where(lane < 64, blk, 0.0).astype(CDT)
            dkr = dkr + jnp.where((lane >= 64) & (lane < 96), blk, 0.0)
        dkv_o[:, 768:1152] = dv_ref[...].astype(CDT)
        dkr_o[...] = jnp.where((lane >= 64) & (lane < 96), _rope_t(dkr, tabm), 0.0)

    return pl.pallas_call(
        body, name="mla_prep_bwd", grid=(S // ts,),
        in_specs=[_row(ts, 768), _row(ts, 768), _row(ts, 384)] + [_row(ts, LANES)] * 3,
        out_specs=[_row(ts, 768), _row(ts, 1152), _row(ts, 128)],
        out_shape=[_sds((S, 768), CDT), _sds((S, 1152), CDT), _sds((S, 128), F32)],
        compiler_params=_cparams(1),
    )(dq, dk, dv, *tm)


def _ln_fwd(xa, xb, g, b, alpha, name):
    S, D = xa.shape
    ts = min(256, S)

    def body(xa_ref, xb_ref, g_ref, b_ref, y_o, yb_o, z_o):
        z = alpha * xa_ref[...] + xb_ref[...]
        mu = jnp.mean(z, axis=-1, keepdims=True)
        zc = z - mu
        var = jnp.mean(zc * zc, axis=-1, keepdims=True)
        y = zc * lax.rsqrt(var + 1e-5) * g_ref[...] + b_ref[...]
        y_o[...] = y
        yb_o[...] = y.astype(CDT)
        z_o[...] = z

    return pl.pallas_call(
        body, name=name, grid=(S // ts,),
        in_specs=[_row(ts, D), _row(ts, D), _full(g.shape), _full(b.shape)],
        out_specs=[_row(ts, D)] * 3, out_shape=[_sds((S, D), F32), _sds((S, D), CDT), _sds((S, D), F32)],
        compiler_params=_cparams(1),
    )(xa, xb, g, b)


def _ln_bwd(dya, dyb, z, g, alpha, name):
    S, D = z.shape
    ts = min(256, S)
    two = dyb is not None

    def body(*refs):
        if two:
            dya_ref, dyb_ref, z_ref, g_ref, dz_o, dzb_o, dg_o, db_o = refs
            dy = dya_ref[...] + alpha * dyb_ref[...]
        else:
            dya_ref, z_ref, g_ref, dz_o, dzb_o, dg_o, db_o = refs
            dy = dya_ref[...]
        z = z_ref[...]
        mu = jnp.mean(z, axis=-1, keepdims=True)
        zc = z - mu
        r = lax.rsqrt(jnp.mean(zc * zc, axis=-1, keepdims=True) + 1e-5)
        xh = zc * r
        dxh = dy * g_ref[...]
        dz = r * (dxh - jnp.mean(dxh, axis=-1, keepdims=True) - xh * jnp.mean(dxh * xh, axis=-1, keepdims=True))
        dz_o[...] = dz
        dzb_o[...] = dz.astype(CDT)
        sg = jnp.sum(dy * xh, axis=0, keepdims=True)
        sb = jnp.sum(dy, axis=0, keepdims=True)
        first = pl.program_id(0) == 0

        @pl.when(first)
        def _():
            dg_o[...] = sg
            db_o[...] = sb

        @pl.when(jnp.logical_not(first))
        def _():
            dg_o[...] += sg
            db_o[...] += sb

    ins = [dya] + ([dyb] if two else []) + [z, g]
    return pl.pallas_call(
        body, name=name, grid=(S // ts,),
        in_specs=[_row(ts, D)] * (3 if two else 2) + [_full(g.shape)],
        out_specs=[_row(ts, D), _row(ts, D), _full((1, D)), _full((1, D))],
        out_shape=[_sds((S, D), F32), _sds((S, D), CDT), _sds((1, D), F32), _sds((1, D), F32)],
        compiler_params=_cparams(1),
    )(*ins)


def _grp_spec(ts, w):
    return pl.BlockSpec((None, ts, w), lambda k, i: (k, i, 0))


def _ffn_up(xb, wg3, wu3):
    S, D = xb.shape
    G, Fc, _ = wg3.shape
    tm = _pick(S, 1024)
    wspec = pl.BlockSpec((None, Fc, D), lambda k, i: (k, 0, 0))

    def body(x_ref, wg_ref, wu_ref, g_o, u_o, a_o):
        x = x_ref[...]
        g = lax.dot_general(x, wg_ref[...], _NT, preferred_element_type=F32)
        u = lax.dot_general(x, wu_ref[...], _NT, preferred_element_type=F32)
        g_o[...] = g
        u_o[...] = u
        a_o[...] = (g / (1.0 + jnp.exp(-g)) * u).astype(CDT)

    return pl.pallas_call(
        body, name="ffn_up", grid=(G, S // tm),
        in_specs=[pl.BlockSpec((tm, D), lambda k, i: (i, 0)), wspec, wspec], out_specs=[_grp_spec(tm, Fc)] * 3,
        out_shape=[_sds((G, S, Fc), F32), _sds((G, S, Fc), F32), _sds((G, S, Fc), CDT)], compiler_params=_cparams(2),
    )(xb, wg3, wu3)


def _ffn_up_dx(dg3, du3, wg3, wu3):
    G, S, Fc = dg3.shape
    D = wg3.shape[2]
    tm = _pick(S, 1024)
    wspec = pl.BlockSpec((None, Fc, D), lambda i, k: (k, 0, 0))
    aspec = pl.BlockSpec((None, tm, Fc), lambda i, k: (k, i, 0))

    def body(dg_ref, du_ref, wg_ref, wu_ref, o_ref):
        part = (jnp.dot(dg_ref[...], wg_ref[...], preferred_element_type=F32)
                + jnp.dot(du_ref[...], wu_ref[...], preferred_element_type=F32))
        k = pl.program_id(1)

        @pl.when(k == 0)
        def _():
            o_ref[...] = part

        @pl.when(k > 0)
        def _():
            o_ref[...] += part

    return pl.pallas_call(
        body, name="ffn_up_dx", grid=(S // tm, G), in_specs=[aspec, aspec, wspec, wspec],
        out_specs=pl.BlockSpec((tm, D), lambda i, k: (i, 0)), out_shape=_sds((S, D), F32), compiler_params=_cparams(2),
    )(dg3, du3, wg3, wu3)


def _ffn_down_dx(dzb, wd3, g3, u3):
    S, D = dzb.shape
    G, Fc, _ = wd3.shape
    tm = _pick(S, 1024)

    def body(dz_ref, wd_ref, g_ref, u_ref, dg_o, du_o):
        da = lax.dot_general(dz_ref[...], wd_ref[...], _NT, preferred_element_type=F32)
        g = g_ref[...]
        sg = 1.0 / (1.0 + jnp.exp(-g))
        dg_o[...] = (da * u_ref[...] * (sg * (1.0 + g * (1.0 - sg)))).astype(CDT)
        du_o[...] = (da * (g * sg)).astype(CDT)

    return pl.pallas_call(
        body, name="ffn_down_dx", grid=(G, S // tm),
        in_specs=[pl.BlockSpec((tm, D), lambda k, i: (i, 0)), pl.BlockSpec((None, Fc, D), lambda k, i: (k, 0, 0)),
                  _grp_spec(tm, Fc), _grp_spec(tm, Fc)],
        out_specs=[_grp_spec(tm, Fc)] * 2, out_shape=[_sds((G, S, Fc), CDT)] * 2, compiler_params=_cparams(2),
    )(dzb, wd3, g3, u3)


def _loss_kernel(y, target):
    S, D = y.shape
    ts = min(256, S)

    def body(y_ref, t_ref, dy_o, loss_o):
        e = y_ref[...] - t_ref[...]
        dy_o[...] = e * (1.0 / D)
        part = jnp.sum(jnp.sum(e * e, axis=1, keepdims=True), axis=0, keepdims=True) * (0.5 / D)
        first = pl.program_id(0) == 0

        @pl.when(first)
        def _():
            loss_o[...] = part

        @pl.when(jnp.logical_not(first))
        def _():
            loss_o[...] += part

    return pl.pallas_call(
        body, name="loss", grid=(S // ts,), in_specs=[_row(ts, D)] * 2,
        out_specs=[_row(ts, D), _full((1, 1))], out_shape=[_sds((S, D), F32), _sds((1, 1), F32)],
        compiler_params=_cparams(1),
    )(y, target)


def _axpy(a, b, alpha, name):
    S, D = a.shape
    ts = min(256, S)

    def body(a_ref, b_ref, o_ref):
        o_ref[...] = a_ref[...] + alpha * b_ref[...]

    return pl.pallas_call(
        body, name=name, grid=(S // ts,), in_specs=[_row(ts, D)] * 2, out_specs=_row(ts, D),
        out_shape=_sds((S, D), F32), compiler_params=_cparams(1),
    )(a, b)


def _pair_masks():
    lane = lax.broadcasted_iota(jnp.int32, (1, LANES), 1)
    first = lane < HEAD_DIM
    return first, jnp.logical_not(first)


def _head_scalar(x, m):
    return jnp.max(jnp.where(m, x, -jnp.inf), axis=-1, keepdims=True)


_NT = (((1,), (1,)), ((), ()))
_TN = (((0,), (0,)), ((), ()))


def _attn_fwd(q, k, v, *, split, npairs, kblk, vblk, name):
    S = q.shape[0]
    qw = 256 if split else LANES
    tq = min(256, S)

    def body(q_ref, k_ref, v_ref, o_ref, lse_ref):
        masks = _pair_masks()
        outs, lses = [], []
        for hd in range(2):
            if split:
                qh = q_ref[:, hd * LANES:(hd + 1) * LANES]
                kh = k_ref[:, hd * LANES:(hd + 1) * LANES]
            else:
                qh = jnp.where(masks[hd], q_ref[...], jnp.zeros_like(q_ref[...]))
                kh = k_ref[...]
            s = lax.dot_general(qh, kh, _NT, preferred_element_type=F32)
            mx = jnp.max(s, axis=-1, keepdims=True)
            p = jnp.exp(s - mx)
            l = jnp.sum(p, axis=-1, keepdims=True)
            o = jnp.dot(p.astype(CDT), v_ref[...], preferred_element_type=F32)
            outs.append(o / l)
            lses.append(jnp.broadcast_to(mx + jnp.log(l), (tq, LANES)))
        o_ref[...] = jnp.where(masks[0], outs[0], outs[1]).astype(o_ref.dtype)
        lse_ref[...] = jnp.where(masks[0], lses[0], lses[1])

    return pl.pallas_call(
        body, name=name, grid=(npairs, S // tq),
        in_specs=[pl.BlockSpec((tq, qw), lambda p, i: (i, p)),
                  pl.BlockSpec((S, qw), lambda p, i: (0, kblk(p))),
                  pl.BlockSpec((S, LANES), lambda p, i: (0, vblk(p)))],
        out_specs=[pl.BlockSpec((tq, LANES), lambda p, i: (i, p))] * 2,
        out_shape=[_sds((S, LANES * npairs), CDT), _sds((S, LANES * npairs), F32)],
        compiler_params=_cparams(2),
    )(q, k, v)


def _attn_bwd(q, k, v, do, o, lse, *, split, npairs, kblk, vblk, doblk, shared_kv, name):
    S = q.shape[0]
    qw = 256 if split else LANES
    tq = min(256, S)
    nkv = 1 if shared_kv else npairs

    def body(q_ref, k_ref, v_ref, do_ref, o_ref, lse_ref, dq_ref, dk_ref, dv_ref):
        masks = _pair_masks()
        p_id, i_id = pl.program_id(0), pl.program_id(1)
        first = (i_id == 0) & ((p_id == 0) if shared_kv else True)
        do = do_ref[...]
        o = o_ref[...].astype(F32)
        lse = lse_ref[...]
        v = v_ref[...]
        dqs, dks, dvs = [], [], []
        for hd in range(2):
            m = masks[hd]
            if split:
                qh = q_ref[:, hd * LANES:(hd + 1) * LANES]
                kh = k_ref[:, hd * LANES:(hd + 1) * LANES]
            else:
                qh = jnp.where(m, q_ref[...], jnp.zeros_like(q_ref[...]))
                kh = k_ref[...]
            doh = jnp.where(m, do, 0.0)
            s = lax.dot_general(qh, kh, _NT, preferred_element_type=F32)
            p = jnp.exp(s - _head_scalar(lse, m))
            delta = jnp.sum(doh * o, axis=-1, keepdims=True)
            dohb = doh.astype(CDT)
            dp = lax.dot_general(dohb, v, _NT, preferred_element_type=F32)
            ds = (p * (dp - delta)).astype(CDT)
            dq = jnp.dot(ds, kh, preferred_element_type=F32)
            dqs.append(dq if split else jnp.where(m, dq, 0.0))
            dks.append(lax.dot_general(ds, qh, _TN, preferred_element_type=F32))
            dvs.append(lax.dot_general(p.astype(CDT), dohb, _TN, preferred_element_type=F32))
        if split:
            dq_ref[:, 0:LANES] = dqs[0]
            dq_ref[:, LANES:2 * LANES] = dqs[1]
        else:
            dq_ref[...] = dqs[0] + dqs[1]
        dv = dvs[0] + dvs[1]

        @pl.when(first)
        def _():
            if split:
                dk_ref[:, 0:LANES] = dks[0]
                dk_ref[:, LANES:2 * LANES] = dks[1]
            else:
                dk_ref[...] = dks[0] + dks[1]
            dv_ref[...] = dv

        @pl.when(jnp.logical_not(first))
        def _():
            if split:
                dk_ref[:, 0:LANES] += dks[0]
                dk_ref[:, LANES:2 * LANES] += dks[1]
            else:
                dk_ref[...] += dks[0] + dks[1]
            dv_ref[...] += dv

    kvo = (lambda p, i: (0, 0)) if shared_kv else (lambda p, i: (0, p))
    return pl.pallas_call(
        body, name=name, grid=(npairs, S // tq),
        in_specs=[pl.BlockSpec((tq, qw), lambda p, i: (i, p)),
                  pl.BlockSpec((S, qw), lambda p, i: (0, kblk(p))),
                  pl.BlockSpec((S, LANES), lambda p, i: (0, vblk(p))),
                  pl.BlockSpec((tq, LANES), lambda p, i: (i, doblk(p))),
                  pl.BlockSpec((tq, LANES), lambda p, i: (i, p)),
                  pl.BlockSpec((tq, LANES), lambda p, i: (i, p))],
        out_specs=[pl.BlockSpec((tq, qw), lambda p, i: (i, p)),
                   pl.BlockSpec((S, qw), kvo), pl.BlockSpec((S, LANES), kvo)],
        out_shape=[_sds((S, qw * npairs), F32), _sds((S, qw * nkv), F32), _sds((S, LANES * nkv), F32)],
        compiler_params=_cparams(2),
    )(q, k, v, do, o, lse)


def _bias_expand(idx, rel_bias, name):
    tq, kw = idx.shape

    def body(idx_ref, rb_ref, o_ref):
        idx = idx_ref[...]
        for hd in range(DIL_HEADS):
            acc = jnp.full((tq, kw), NEG_INF, F32)
            for u in range(REL_BUCKETS):
                acc = jnp.where(idx == u, rb_ref[u, hd], acc)
            o_ref[hd] = acc

    return pl.pallas_call(
        body, name=name,
        in_specs=[pl.BlockSpec(memory_space=pltpu.VMEM), pl.BlockSpec(memory_space=pltpu.SMEM)],
        out_specs=pl.BlockSpec(memory_space=pltpu.VMEM),
        out_shape=_sds((DIL_HEADS, tq, kw), F32),
    )(idx, rel_bias)


def _bias_reduce(idx, dtab, name):
    tq, kw = idx.shape

    def body(idx_ref, d_ref, o_ref):
        idx = idx_ref[...]
        rowid = lax.broadcasted_iota(jnp.int32, (REL_BUCKETS, kw), 0)
        for hd in range(DIL_HEADS):
            d = d_ref[hd]
            acc = jnp.zeros((REL_BUCKETS, kw), F32)
            for u in range(REL_BUCKETS):
                r = jnp.sum(jnp.where(idx == u, d, 0.0), axis=0, keepdims=True)
                acc = jnp.where(rowid == u, r, acc)
            o_ref[hd] = jnp.sum(acc, axis=1, keepdims=True)

    return pl.pallas_call(
        body, name=name,
        in_specs=[pl.BlockSpec(memory_space=pltpu.VMEM)] * 2, out_specs=pl.BlockSpec(memory_space=pltpu.VMEM),
        out_shape=_sds((DIL_HEADS, REL_BUCKETS, 1), F32),
    )(idx, dtab)


def _dil_window(i, tq, kw, L):
    start = pl.multiple_of(i * tq, DIL_HALF)
    key = start + lax.broadcasted_iota(jnp.int32, (1, kw), 1) - DIL_HALF
    return start, (key >= 0) & (key < L)


def _dil_fwd(qv, kv, vv, tab, *, dil, L, tq, name):
    kw = tq + 2 * DIL_HALF
    npair = DIL_HEADS // 2

    def body(q_ref, k_ref, v_ref, t_ref, o_ref, lse_ref):
        masks = _pair_masks()
        start, valid = _dil_window(pl.program_id(2), tq, kw, L)
        kwin = k_ref[pl.ds(start, kw), :]
        vwin = v_ref[pl.ds(start, kw), :]
        outs, lses = [], []
        for hd in range(2):
            qh = jnp.where(masks[hd], q_ref[...], jnp.zeros_like(q_ref[...]))
            s = lax.dot_general(qh, kwin, _NT, preferred_element_type=F32) + t_ref[hd]
            s = jnp.where(valid, s, NEG_INF)
            mx = jnp.max(s, axis=-1, keepdims=True)
            p = jnp.exp(s - mx)
            l = jnp.sum(p, axis=-1, keepdims=True)
            outs.append(jnp.dot(p.astype(CDT), vwin, preferred_element_type=F32) / l)
            lses.append(jnp.broadcast_to(mx + jnp.log(l), (tq, LANES)))
        o_ref[...] = jnp.where(masks[0], outs[0], outs[1])
        lse_ref[...] = jnp.where(masks[0], lses[0], lses[1])

    blk = pl.BlockSpec((None, tq, LANES), lambda p, c, i: (c, i, p))
    res = pl.BlockSpec((None, L + 2 * DIL_HALF, LANES), lambda p, c, i: (c, 0, p))
    return pl.pallas_call(
        body, name=name, grid=(npair, dil, L // tq),
        in_specs=[blk, res, res, pl.BlockSpec((2, tq, kw), lambda p, c, i: (p, 0, 0))],
        out_specs=[blk] * 2, out_shape=[_sds(qv.shape, F32)] * 2,
        compiler_params=_cparams(3),
    )(qv, kv, vv, tab)


def _dil_bwd(qv, kv, vv, tab, dov, lsev, deltav, *, dil, L, tq, name):
    kw = tq + 2 * DIL_HALF
    npair = DIL_HEADS // 2

    def body(q_ref, k_ref, v_ref, t_ref, do_ref, lse_ref, dl_ref, dq_ref, dk_ref, dv_ref, dt_ref):
        masks = _pair_masks()
        c_id, i_id = pl.program_id(1), pl.program_id(2)
        start, valid = _dil_window(i_id, tq, kw, L)
        kwin = k_ref[pl.ds(start, kw), :]
        vwin = v_ref[pl.ds(start, kw), :]

        @pl.when(i_id == 0)
        def _():
            dk_ref[...] = jnp.zeros_like(dk_ref)
            dv_ref[...] = jnp.zeros_like(dv_ref)

        @pl.when((i_id == 0) & (c_id == 0))
        def _():
            dt_ref[...] = jnp.zeros_like(dt_ref)

        do = do_ref[...]
        dq = jnp.zeros((tq, LANES), F32)
        dk = jnp.zeros((kw, LANES), F32)
        dv = jnp.zeros((kw, LANES), F32)
        for hd in range(2):
            m = masks[hd]
            qh = jnp.where(m, q_ref[...], jnp.zeros_like(q_ref[...]))
            doh = jnp.where(m, do, jnp.zeros_like(do))
            s = lax.dot_general(qh, kwin, _NT, preferred_element_type=F32) + t_ref[hd]
            s = jnp.where(valid, s, NEG_INF)
            p = jnp.exp(s - _head_scalar(lse_ref[...], m))
            dp = lax.dot_general(doh, vwin, _NT, preferred_element_type=F32)
            ds = p * (dp - _head_scalar(dl_ref[...], m))
            dt_ref[hd] += ds
            dsb = ds.astype(CDT)
            dq = dq + jnp.where(m, jnp.dot(dsb, kwin, preferred_element_type=F32), 0.0)
            dk = dk + lax.dot_general(dsb, qh, _TN, preferred_element_type=F32)
            dv = dv + lax.dot_general(p.astype(CDT), doh, _TN, preferred_element_type=F32)
        dq_ref[...] = dq
        dk_ref[pl.ds(start, kw), :] += dk
        dv_ref[pl.ds(start, kw), :] += dv

    blk = pl.BlockSpec((None, tq, LANES), lambda p, c, i: (c, i, p))
    res = pl.BlockSpec((None, L + 2 * DIL_HALF, LANES), lambda p, c, i: (c, 0, p))
    tsp = pl.BlockSpec((2, tq, kw), lambda p, c, i: (p, 0, 0))
    return pl.pallas_call(
        body, name=name, grid=(npair, dil, L // tq),
        in_specs=[blk, res, res, tsp, blk, blk, blk], out_specs=[blk, res, res, tsp],
        out_shape=[_sds(qv.shape, F32), _sds(kv.shape, F32), _sds(kv.shape, F32), _sds(tab.shape, F32)],
        compiler_params=_cparams(3),
    )(qv, kv, vv, tab, dov, lsev, deltav)


def _mix_weights(l1, l2, l3):
    mx = jnp.maximum(jnp.maximum(l1, l2), l3)
    e1, e2, e3 = jnp.exp(l1 - mx), jnp.exp(l2 - mx), jnp.exp(l3 - mx)
    inv = 1.0 / (e1 + e2 + e3)
    return e1 * inv, e2 * inv, e3 * inv


def _branch_specs(S, ts):
    dils = [d for _, d in DIL_BRANCHES]
    return dils, [_res_spec(d, ts // d) for d in dils], [(d, S // d, DIL_W) for d in dils]


def _dil_mix_fwd(os, ls):
    S = os[0].shape[0] * os[0].shape[1]
    ts = min(256, S)
    dils, specs, _ = _branch_specs(S, ts)

    def body(o1, o2, o3, l1, l2, l3, out, scr):
        o1, o2, o3, l1, l2, l3 = [_by_token(r, scr, d) for r, d in zip((o1, o2, o3, l1, l2, l3), dils + dils)]
        w1, w2, w3 = _mix_weights(l1, l2, l3)
        out[...] = (w1 * o1 + w2 * o2 + w3 * o3).astype(CDT)

    return pl.pallas_call(
        body, name="dil_mix_fwd", grid=(S // ts,), in_specs=specs + specs, out_specs=_row(ts, DIL_W),
        out_shape=_sds((S, DIL_W), CDT), scratch_shapes=[_TOKEN_SCRATCH(ts)],
        compiler_params=_cparams(1),
    )(*os, *ls)


def _dil_mix_bwd(dcat, os, ls, j384):
    S = os[0].shape[0] * os[0].shape[1]
    ts = min(256, S)
    dils, specs, shapes = _branch_specs(S, ts)

    def body(do_ref, o1, o2, o3, l1, l2, l3, j_ref, d1, d2, d3, e1, e2, e3, scr):
        o1, o2, o3, l1, l2, l3 = [_by_token(r, scr, d) for r, d in zip((o1, o2, o3, l1, l2, l3), dils + dils)]
        ws = _mix_weights(l1, l2, l3)
        do = do_ref[...]
        o = ws[0] * o1 + ws[1] * o2 + ws[2] * o3
        dot = _headsum(do * o, j_ref[...])
        for w, d, d_o, e_o in zip(ws, dils, (d1, d2, d3), (e1, e2, e3)):
            _by_residue(w * do, scr, d_o, d)
            _by_residue(w * dot, scr, e_o, d)

    return pl.pallas_call(
        body, name="dil_mix_bwd", grid=(S // ts,),
        in_specs=[_row(ts, DIL_W, 1)] + specs + specs + [_full(j384.shape)],
        out_specs=specs + specs,
        out_shape=[_sds(s, CDT) for s in shapes] + [_sds(s, F32) for s in shapes],
        scratch_shapes=[_TOKEN_SCRATCH(ts)],
        compiler_params=_cparams(1),
    )(dcat, *os, *ls, j384)


def _adamw_math(w, g, m, v):
    m = ADAM_B1 * m + (1.0 - ADAM_B1) * g
    v = ADAM_B2 * v + (1.0 - ADAM_B2) * (g * g)
    m_hat = m / (1.0 - ADAM_B1 ** ADAM_STEP)
    v_hat = v / (1.0 - ADAM_B2 ** ADAM_STEP)
    delta = -ADAM_LR * (m_hat / (jnp.sqrt(v_hat) + ADAM_EPS) + ADAM_WD * w)
    return delta, m, v


def _pick8(n, target):
    best = None
    for t in range(16, min(n, target) + 1, 16):
        if n % t == 0:
            best = t
    return best if best is not None else n


_ELEMS_PER_BLOCK = 256 * 1024


def _lead_spec(a, b):
    ta = _pick8(a, max(16, _ELEMS_PER_BLOCK // b))
    return ta, pl.BlockSpec((None, ta, b), lambda l, i: (l, i, 0))


def _adamw(w, g_mine, g_sib, m, v, name):
    L, a, b = w.shape
    ta, spec = _lead_spec(a, b)
    gspec = pl.BlockSpec((ta, b), lambda l, i: (i, 0))

    def body(w_ref, gm_ref, gs_ref, m_ref, v_ref, g_o, d_o, m_o, v_o):
        g = jnp.where(pl.program_id(0) == lax.axis_index("c"), gm_ref[...], gs_ref[...])
        d, mm, vv = _adamw_math(w_ref[...], g, m_ref[...], v_ref[...])
        g_o[...] = g
        d_o[...] = d
        m_o[...] = mm
        v_o[...] = vv

    return pl.pallas_call(
        body, name=name, grid=(L, a // ta), in_specs=[spec, gspec, gspec, spec, spec], out_specs=[spec] * 4,
        out_shape=[_sds(w.shape, F32)] * 4, compiler_params=_cparams(2),
    )(w, g_mine, g_sib, m, v)


def _adamw_small(w, gall, m, v):
    R = w.shape[0]

    def body(w_ref, g_ref, m_ref, v_ref, g_o, d_o, m_o, v_o):
        g = g_ref[0]
        for k in range(1, 8):
            g = g + g_ref[k]
        d, mm, vv = _adamw_math(w_ref[...], g, m_ref[...], v_ref[...])
        g_o[...] = g
        d_o[...] = d
        m_o[...] = mm
        v_o[...] = vv

    vm = pl.BlockSpec(memory_space=pltpu.VMEM)
    return pl.pallas_call(
        body, name="adamw_small", in_specs=[vm] * 4, out_specs=[vm] * 4, out_shape=[_sds((R, LANES), F32)] * 4,
    )(w, gall, m, v)


def _sum_pair(g0, g1, t, name):
    n, a, b = t.shape
    ta, spec = _lead_spec(a, b)

    def body(g0_ref, g1_ref, t_ref, o_ref):
        mine = jnp.where(lax.axis_index("c") == 0, g0_ref[...], g1_ref[...])
        o_ref[...] = (mine + t_ref[...]).astype(o_ref.dtype)

    return pl.pallas_call(
        body, name=name, grid=(n, a // ta), in_specs=[spec] * 3, out_specs=spec, out_shape=_sds(t.shape, WIRE),
        compiler_params=_cparams(2),
    )(g0, g1, t)


def _sum_chips(pair, t, name):
    _, a, b = t.shape
    ta = _pick8(a, max(16, _ELEMS_PER_BLOCK // b))

    def body(p_ref, t_ref, o_ref):
        me = 2 * lax.axis_index("x") + lax.axis_index("y")
        acc = p_ref[me].astype(F32)
        for k in range(3):
            acc = acc + t_ref[k].astype(F32)
        o_ref[...] = acc

    return pl.pallas_call(
        body, name=name, grid=(a // ta,),
        in_specs=[pl.BlockSpec((4, ta, b), lambda i: (0, i, 0)), pl.BlockSpec((3, ta, b), lambda i: (0, i, 0))],
        out_specs=pl.BlockSpec((ta, b), lambda i: (i, 0)), out_shape=_sds((a, b), F32), compiler_params=_cparams(1),
    )(pair, t)


_HBM = pl.BlockSpec(memory_space=pltpu.HBM)


def _place():
    x, y, c = lax.axis_index("x"), lax.axis_index("y"), lax.axis_index("c")
    chips = [(1 - x, y), (x, 1 - y), (1 - x, 1 - y)]
    return x, y, c, chips


def _remote(src, dst, ssem, rsem, to):
    return pltpu.make_async_remote_copy(src_ref=src, dst_ref=dst, send_sem=ssem, recv_sem=rsem, device_id=to,
                                        device_id_type=MESH_ID)


def _dma_sems(n):
    return pltpu.SemaphoreType.DMA((n,))


def _gather_weights(shards):
    n = len(shards)

    def body(*refs):
        w_refs, g_refs = refs[:n], refs[n:2 * n]
        ssem, rsem = refs[2 * n:]
        x, y, c, chips = _place()
        me = 2 * x + y
        sib = (x, y, 1 - c)
        owns = [_remote(w.at[l], g.at[l, me], ssem.at[6 * n + 2 * i + l], rsem.at[6 * n + 2 * i + l], sib)
                for i, (w, g) in enumerate(zip(w_refs, g_refs)) for l in range(2)]
        first = [_remote(w.at[c], g.at[c, me], ssem.at[3 * i + j], rsem.at[3 * i + j], (cx, cy, c))
                 for j, (cx, cy) in enumerate(chips) for i, (w, g) in enumerate(zip(w_refs, g_refs))]
        for cp in first + owns:
            cp.start()
        passed = []
        for j, (cx, cy) in enumerate(chips):
            for i, g in enumerate(g_refs):
                blk = g.at[c, 2 * cx + cy]
                _remote(blk, blk, ssem.at[3 * i + j], rsem.at[3 * i + j], sib).wait_recv()
                fw = _remote(blk, blk, ssem.at[3 * n + 3 * i + j], rsem.at[3 * n + 3 * i + j], sib)
                fw.start()
                passed.append(fw)
        for j, (cx, cy) in enumerate(chips):
            for i, g in enumerate(g_refs):
                blk = g.at[1 - c, 2 * cx + cy]
                _remote(blk, blk, ssem.at[3 * n + 3 * i + j], rsem.at[3 * n + 3 * i + j], sib).wait_recv()
        for cp in owns:
            cp.wait_recv()
        for cp in first + passed + owns:
            cp.wait_send()

    return pl.pallas_call(
        body, name="gather_weights", in_specs=[_HBM] * n, out_specs=[_HBM] * n,
        out_shape=[_sds((2, 4) + s.shape[1:], s.dtype) for s in shards],
        scratch_shapes=[_dma_sems(8 * n), _dma_sems(8 * n)],
    )(*shards)


def _sibling_halves(g0s, g1s):
    n = len(g0s)

    def body(*refs):
        g0_refs, g1_refs, t_refs = refs[:n], refs[n:2 * n], refs[2 * n:3 * n]
        ssem, rsem = refs[3 * n:]
        x, y, c, _ = _place()

        def swap(srcs):
            cps = [_remote(s, t, ssem.at[i], rsem.at[i], (x, y, 1 - c)) for i, (s, t) in enumerate(zip(srcs, t_refs))]
            for cp in cps:
                cp.start()
            for cp in cps:
                cp.wait()

        @pl.when(c == 0)
        def _():
            swap(g1_refs)

        @pl.when(c == 1)
        def _():
            swap(g0_refs)

    return pl.pallas_call(
        body, name="rs_sibling_halves", in_specs=[_HBM] * (2 * n), out_specs=[_HBM] * n,
        out_shape=[_sds(g.shape, g.dtype) for g in g0s], scratch_shapes=[_dma_sems(n), _dma_sems(n)],
    )(*g0s, *g1s)


def _chip_all_to_all(parts):
    n = len(parts)

    def body(*refs):
        a_refs, t_refs = refs[:n], refs[n:2 * n]
        ssem, rsem = refs[2 * n:]
        x, y, c, chips = _place()
        sends = [_remote(a.at[2 * cx + cy], t.at[j], ssem.at[3 * i + j], rsem.at[3 * i + j], (cx, cy, c))
                 for j, (cx, cy) in enumerate(chips) for i, (a, t) in enumerate(zip(a_refs, t_refs))]
        for cp in sends:
            cp.start()
        for cp in sends:
            cp.wait_recv()
        for cp in sends:
            cp.wait_send()

    return pl.pallas_call(
        body, name="rs_chip_all_to_all", in_specs=[_HBM] * n, out_specs=[_HBM] * n,
        out_shape=[_sds((3,) + p.shape[1:], p.dtype) for p in parts],
        scratch_shapes=[_dma_sems(3 * n), _dma_sems(3 * n)],
    )(*parts)


def _sibling_swap(reds):
    n = len(reds)

    def body(*refs):
        q_refs, o_refs = refs[:n], refs[n:2 * n]
        ssem, rsem = refs[2 * n:]
        x, y, c, _ = _place()
        cps = [_remote(q, o, ssem.at[i], rsem.at[i], (x, y, 1 - c)) for i, (q, o) in enumerate(zip(q_refs, o_refs))]
        for cp in cps:
            cp.start()
        for cp in cps:
            cp.wait()

    return pl.pallas_call(
        body, name="rs_sibling_swap", in_specs=[_HBM] * n, out_specs=[_HBM] * n,
        out_shape=[_sds(q.shape, q.dtype) for q in reds],
        scratch_shapes=[_dma_sems(n), _dma_sems(n)],
    )(*reds)


def _gather_small(s):
    R, _ = s.shape

    def body(s_ref, o_ref, ssem, rsem, lsem):
        x, y, c, _ = _place()
        me = 4 * x + 2 * y + c
        own = pltpu.make_async_copy(s_ref, o_ref.at[me], lsem)
        own.start()
        sends = []
        for k in range(1, 8):
            px, py, pc = x ^ (k >> 2), y ^ ((k >> 1) & 1), c ^ (k & 1)
            cp = _remote(s_ref, o_ref.at[me], ssem.at[k - 1], rsem.at[k - 1], (px, py, pc))
            cp.start()
            sends.append(cp)
        for k in range(1, 8):
            px, py, pc = x ^ (k >> 2), y ^ ((k >> 1) & 1), c ^ (k & 1)
            blk = o_ref.at[4 * px + 2 * py + pc]
            _remote(blk, blk, ssem.at[k - 1], rsem.at[k - 1], (px, py, pc)).wait_recv()
        for cp in sends:
            cp.wait_send()
        own.wait()

    vm = pl.BlockSpec(memory_space=pltpu.VMEM)
    return pl.pallas_call(
        body, name="gather_small", in_specs=[vm], out_specs=vm, out_shape=_sds((8, R, LANES), s.dtype),
        scratch_shapes=[pltpu.SemaphoreType.DMA((7,)), pltpu.SemaphoreType.DMA((7,)), pltpu.SemaphoreType.DMA],
    )(s)


_BIG = ("w_in", "mla_w_uq", "mla_w_ukv", "w_out", "ffn_w_gate", "ffn_w_up", "ffn_w_down")
_COL_SHARDED = ("w_in", "mla_w_uq", "mla_w_ukv", "ffn_w_gate", "ffn_w_up")
_SMALL = ("mla_q_norm", "mla_kv_norm", "gqa_q_norm", "gqa_k_norm", "rel_bias", "ln1_g", "ln1_b", "ln2_g", "ln2_b")


def _pack_flat(arrs, align):
    flat = jnp.concatenate([a.reshape(-1) for a in arrs])
    pad = (-flat.shape[0]) % align
    return jnp.pad(flat, (0, pad)) if pad else flat


def _unpack_flat(flat, shapes):
    out, off = [], 0
    for s in shapes:
        n = int(np.prod(s))
        out.append(flat[off:off + n].reshape(s))
        off += n
    return out


def _perm_gqa_rows(w):
    return jnp.concatenate([w[:832], w[896:960], w[832:896], w[960:]], axis=0)


def _local_step(x, target, W, small):
    S, D = x.shape
    depth = W["w_in"].shape[0]
    alpha = (2.0 * depth) ** 0.25
    in_idx, uq_idx, ukv_idx = _in_cols(), _uq_cols(), _ukv_cols()
    win = [_rows_from_shards(W["w_in"][l], in_idx) for l in range(depth)]
    wuq = [_rows_from_shards(W["mla_w_uq"][l], uq_idx) for l in range(depth)]
    wukv = [_rows_from_shards(W["mla_w_ukv"][l], ukv_idx) for l in range(depth)]
    wout = [_perm_gqa_rows(W["w_out"][l].reshape(-1, D)) for l in range(depth)]
    wg, wu, wdn = W["ffn_w_gate"], W["ffn_w_up"], W["ffn_w_down"]

    tm, tg = _rope_tables(S)
    j256, j384 = _head_ones(256), _head_ones(384)
    mla_scale = (64 + MLA_ROPE_DIM) ** -0.5
    branches = []
    for (_, dil) in DIL_BRANCHES:
        L = S // dil
        tq = min(256, L)
        idx = jnp.asarray(_branch_bucket_idx(tq, dil))
        branches.append((dil, L, tq, idx))
    tabs = [_bias_expand(idx, small["rel_bias"], name=f"bias_expand_{b}") for b, (_, _, _, idx) in enumerate(branches)]

    def padded(a):
        z = jnp.zeros((DIL_HALF, a.shape[1]), a.dtype)
        return jnp.concatenate([z, a, z], axis=0)[None]

    saved = []
    xf, xb = x, x.astype(CDT)
    for l in range(depth):
        gq, gkv = small["mla_q_norm"][l][None], small["mla_kv_norm"][l][None]
        ggq = jnp.tile(small["gqa_q_norm"][l], 4)[None]
        ggk = jnp.tile(small["gqa_k_norm"][l], 2)[None]
        h = _mm(xb, win[l], tb=True, name="mm_in")
        cq, ckv, kr, qd, kd, vd, qg, kg, vg, *strided = _prep_fwd(h, gq, gkv, ggq, ggk, tm, tg, j256)
        qkv = [(qd[None], padded(kd), padded(vd))] + [tuple(strided[3 * b:3 * b + 3]) for b in range(len(DIL_STRIDES))]
        qa = _mm(cq, wuq[l], tb=True, name="mm_uq")
        kvp = _mm(ckv, wukv[l], tb=True, out_dtype=CDT, name="mm_ukv")
        qm, km = _mla_prep_fwd(qa, kvp, kr, tm, mla_scale)
        oa, lsa = _attn_fwd(qm, km, kvp, split=True, npairs=3, kblk=lambda p: p, vblk=lambda p: 6 + p,
                            name="mla_attn_fwd")
        oc, lsc = _attn_fwd(qg, kg, vg, split=False, npairs=2, kblk=lambda p: 0, vblk=lambda p: 0,
                            name="gqa_attn_fwd")
        obs, lbs = [], []
        for b, (dil, L, tq, _) in enumerate(branches):
            o_b, l_b = _dil_fwd(*qkv[b], tabs[b], dil=dil, L=L, tq=tq, name=f"dil_fwd_{b}")
            obs.append(o_b)
            lbs.append(l_b)
        ob = _dil_mix_fwd(obs, lbs)
        cat = jnp.concatenate([oa, ob, oc], axis=1)
        mix = _mm(cat, wout[l], name="mm_out")
        x1, x1b, z1 = _ln_fwd(xf, mix, small["ln1_g"][l][None], small["ln1_b"][l][None], alpha, name="ln1_fwd")
        g3, u3, act = _ffn_up(x1b, wg[l], wu[l])
        ff = _mm(act, wdn[l], ga=True, gb=True, name="mm_down")
        x2, x2b, z2 = _ln_fwd(x1, ff, small["ln2_g"][l][None], small["ln2_b"][l][None], alpha, name="ln2_fwd")
        saved.append(dict(xb=xb, h=h, cq=cq, ckv=ckv, qg=qg, kg=kg, vg=vg, kvp=kvp, qm=qm, km=km, oa=oa, lsa=lsa,
                          oc=oc, lsc=lsc, obs=obs, lbs=lbs, qkv=qkv, cat=cat, z1=z1, x1b=x1b, g3=g3, u3=u3, act=act, z2=z2,
                          gq=gq, gkv=gkv, ggq=ggq, ggk=ggk))
        xf, xb = x2, x2b

    dy, loss = _loss_kernel(xf, target)

    gW = {k: [None] * depth for k in _BIG}
    gS = {k: [None] * depth for k in ("mla_q_norm", "mla_kv_norm", "gqa_q_norm", "gqa_k_norm", "ln1_g", "ln1_b", "ln2_g",
                                      "ln2_b")}
    g_rel = None
    dya, dyb = dy, None
    for l in reversed(range(depth)):
        sv = saved[l]
        dz2, dz2b, gS["ln2_g"][l], gS["ln2_b"][l] = _ln_bwd(dya, dyb, sv["z2"], small["ln2_g"][l][None], alpha,
                                                             name="ln2_bwd" if dyb is not None else "ln2_bwd_last")
        gW["ffn_w_down"][l] = _mm(sv["act"], dz2b, ta=True, ga=True, go=True, name="mm_down_dw")
        dg3, du3 = _ffn_down_dx(dz2b, wdn[l], sv["g3"], sv["u3"])
        gW["ffn_w_gate"][l] = _mm(dg3, sv["x1b"], ta=True, ga=True, go=True, name="mm_gate_dw")
        gW["ffn_w_up"][l] = _mm(du3, sv["x1b"], ta=True, ga=True, go=True, name="mm_up_dw")
        dx1 = _ffn_up_dx(dg3, du3, wg[l], wu[l])
        dz1, dz1b, gS["ln1_g"][l], gS["ln1_b"][l] = _ln_bwd(dx1, dz2, sv["z1"], small["ln1_g"][l][None], alpha,
                                                             name="ln1_bwd")
        gW["w_out"][l] = _perm_gqa_rows(_mm(sv["cat"], dz1b, ta=True, name="mm_out_dw")).reshape(4, -1, D)
        dcat = _mm(dz1b, wout[l], tb=True, name="mm_out_dx")
        dqg, dkg, dvg = _attn_bwd(sv["qg"], sv["kg"], sv["vg"], dcat, sv["oc"], sv["lsc"], split=False, npairs=2,
                                  kblk=lambda p: 0, vblk=lambda p: 0, doblk=lambda p: 6 + p, shared_kv=True,
                                  name="gqa_attn_bwd")
        dqm, dkm, dvm = _attn_bwd(sv["qm"], sv["km"], sv["kvp"], dcat, sv["oa"], sv["lsa"], split=True, npairs=3,
                                  kblk=lambda p: p, vblk=lambda p: 6 + p, doblk=lambda p: p, shared_kv=False,
                                  name="mla_attn_bwd")
        dqa, dkvp, dkr = _mla_prep_bwd(dqm, dkm, dvm, tm, mla_scale)
        gW["mla_w_uq"][l] = _rows_to_shards(_mm(dqa, sv["cq"], ta=True, name="mm_uq_dw"), uq_idx, MLA_HEADS * 96)
        dcq = _mm(dqa, wuq[l], name="mm_uq_dx")
        gW["mla_w_ukv"][l] = _rows_to_shards(_mm(dkvp, sv["ckv"], ta=True, name="mm_ukv_dw"), ukv_idx, MLA_HEADS * 128)
        dckv = _mm(dkvp, wukv[l], name="mm_ukv_dx")
        mixb = _dil_mix_bwd(dcat, sv["obs"], sv["lbs"], j384)
        ddq, ddk, ddv = [], [], []
        for b, (dil, L, tq, idx) in enumerate(branches):
            dq_b, dk_b, dv_b, dtab = _dil_bwd(*sv["qkv"][b], tabs[b], mixb[b], sv["lbs"][b], mixb[3 + b], dil=dil, L=L,
                                              tq=tq, name=f"dil_bwd_{b}")
            if dil == 1:
                dq_b, dk_b, dv_b = dq_b[0], dk_b[0, DIL_HALF:DIL_HALF + S], dv_b[0, DIL_HALF:DIL_HALF + S]
            ddq.append(dq_b)
            ddk.append(dk_b)
            ddv.append(dv_b)
            g_b = _bias_reduce(idx, dtab, name=f"bias_reduce_{b}")[:, :, 0].T
            g_rel = g_b if g_rel is None else g_rel + g_b
        dh, n1, n2, n3, n4 = _prep_bwd(sv["h"], dcq, dckv, dkr, ddq, ddk, ddv, dqg, dkg, dvg, sv["gq"], sv["gkv"],
                                       sv["ggq"], sv["ggk"], tg, j256)
        gS["mla_q_norm"][l], gS["mla_kv_norm"][l] = n1[0], n2[0]
        gS["gqa_q_norm"][l] = n3[0].reshape(4, 64).sum(0)
        gS["gqa_k_norm"][l] = n4[0].reshape(2, 64).sum(0)
        gW["w_in"][l] = _rows_to_shards(_mm(dh, sv["xb"], ta=True, name="mm_in_dw"), in_idx, IN_W)
        dya = _mm(dh, win[l], name="mm_in_dx")
        dyb = dz1
    grad_x = _axpy(dya, dyb, alpha, name="grad_x")

    gsmall = {k: jnp.stack([a.reshape(-1) for a in v]) for k, v in gS.items()}
    gsmall["rel_bias"] = g_rel
    return loss, grad_x, gW, gsmall


_ORDER = ("w_in", "mla_q_norm", "mla_kv_norm", "mla_w_uq", "mla_w_ukv", "gqa_q_norm", "gqa_k_norm", "rel_bias", "w_out",
          "ln1_g", "ln1_b", "ffn_w_gate", "ffn_w_up", "ffn_w_down", "ln2_g", "ln2_b")


def kernel(x, w_in, mla_q_norm, mla_kv_norm, mla_w_uq, mla_w_ukv, gqa_q_norm, gqa_k_norm, rel_bias, w_out, ln1_g, ln1_b, ffn_w_gate, ffn_w_up, ffn_w_down, ln2_g, ln2_b, loss_target, m_w_in, m_mla_q_norm, m_mla_kv_norm, m_mla_w_uq, m_mla_w_ukv, m_gqa_q_norm, m_gqa_k_norm, m_rel_bias, m_w_out, m_ln1_g, m_ln1_b, m_ffn_w_gate, m_ffn_w_up, m_ffn_w_down, m_ln2_g, m_ln2_b, v_w_in, v_mla_q_norm, v_mla_kv_norm, v_mla_w_uq, v_mla_w_ukv, v_gqa_q_norm, v_gqa_k_norm, v_rel_bias, v_w_out, v_ln1_g, v_ln1_b, v_ffn_w_gate, v_ffn_w_up, v_ffn_w_down, v_ln2_g, v_ln2_b):
    wts = dict(zip(_ORDER, (w_in, mla_q_norm, mla_kv_norm, mla_w_uq, mla_w_ukv, gqa_q_norm, gqa_k_norm, rel_bias, w_out,
                            ln1_g, ln1_b, ffn_w_gate, ffn_w_up, ffn_w_down, ln2_g, ln2_b)))
    mom = dict(zip(_ORDER, (m_w_in, m_mla_q_norm, m_mla_kv_norm, m_mla_w_uq, m_mla_w_ukv, m_gqa_q_norm, m_gqa_k_norm,
                            m_rel_bias, m_w_out, m_ln1_g, m_ln1_b, m_ffn_w_gate, m_ffn_w_up, m_ffn_w_down, m_ln2_g,
                            m_ln2_b)))
    var = dict(zip(_ORDER, (v_w_in, v_mla_q_norm, v_mla_kv_norm, v_mla_w_uq, v_mla_w_ukv, v_gqa_q_norm, v_gqa_k_norm,
                            v_rel_bias, v_w_out, v_ln1_g, v_ln1_b, v_ffn_w_gate, v_ffn_w_up, v_ffn_w_down, v_ln2_g,
                            v_ln2_b)))
    small_shapes = [wts[n].shape for n in _SMALL]
    for d in (wts, mom, var):
        for n in _COL_SHARDED:
            d[n] = d[n].transpose(0, 2, 1)

    gathered = _gather_weights([wts[n].astype(WIRE) for n in _BIG])
    full = {n: g.astype(CDT) for n, g in zip(_BIG, gathered)}

    small = {n: wts[n] for n in _SMALL}
    loss, grad_x, gbig, gsmall = _local_step(x[0], loss_target[0], full, small)

    g0s, g1s = [gbig[n][0] for n in _BIG], [gbig[n][1] for n in _BIG]
    theirs = _sibling_halves(g0s, g1s)
    pairs = [_sum_pair(g0, g1, t, name=f"rs_pair_sum_{n}") for n, g0, g1, t in zip(_BIG, g0s, g1s, theirs)]
    reds = [_sum_chips(p, t, name=f"rs_sum_chips_{n}") for n, p, t in zip(_BIG, pairs, _chip_all_to_all(pairs))]
    sibs = _sibling_swap(reds)

    sflat = _pack_flat([gsmall[n].reshape(-1) for n in _SMALL], 8 * LANES)
    rs = sflat.shape[0] // LANES
    sall = _gather_small(sflat.reshape(rs, LANES))

    def packed(d):
        return _pack_flat([d[n] for n in _SMALL], 8 * LANES).reshape(rs, LANES)

    outs = {tag: {} for tag in ("grad", "delta", "new_m", "new_v")}
    for n, red, sib in zip(_BIG, reds, sibs):
        res = _adamw(wts[n], red, sib, mom[n], var[n], name=f"adamw_{n}")
        for tag, r in zip(("grad", "delta", "new_m", "new_v"), res):
            outs[tag][n] = r.transpose(0, 2, 1) if n in _COL_SHARDED else r
    for tag, smallflat in zip(("grad", "delta", "new_m", "new_v"), _adamw_small(packed(wts), sall, packed(mom), packed(var))):
        outs[tag].update(zip(_SMALL, _unpack_flat(smallflat.reshape(-1), small_shapes)))

    total = lax.psum(loss[0, 0], ("x", "y", "c"))
    return (total, grad_x[None], *[outs["grad"][n] for n in _ORDER], *[outs["delta"][n] for n in _ORDER],
            *[outs["new_m"][n] for n in _ORDER], *[outs["new_v"][n] for n in _ORDER])
```

```python
import functools
import math

import numpy as np
import jax
import jax.numpy as jnp
from jax import lax
from jax.experimental import pallas as pl
from jax.experimental.pallas import tpu as pltpu

F32 = jnp.float32
CDT = jnp.bfloat16
WIRE = jnp.bfloat16

HEAD_DIM = 64
GRID_W = 64
ROPE_THETA = 10000.0
MLA_HEADS = 6
MLA_Q_RANK = 256
MLA_KV_RANK = 128
MLA_ROPE_DIM = 32
DIL_HEADS = 6
DIL_BRANCHES = ((128, 1), (512, 4), (2048, 16))
DIL_HALF = 64
GQA_Q_HEADS = 4
REL_BUCKETS = 32
REL_MAX_DIST = 1024
NEG_INF = -1e30
LANES = 128
VMEM_LIMIT = 56 * 1024 * 1024

ADAM_LR, ADAM_B1, ADAM_B2, ADAM_EPS, ADAM_WD, ADAM_STEP = 0.001, 0.9, 0.999, 1e-08, 0.01, 10

C_CQ, C_CKV, C_KR, C_DQ, C_DK, C_DV, C_GQ, C_GK, C_GV, IN_P = 0, 256, 384, 512, 896, 1280, 1664, 1920, 2048, 2176
IN_W = 2080
MESH_ID = pl.DeviceIdType.MESH


def _cparams(n_axes, vmem=VMEM_LIMIT):
    return pltpu.CompilerParams(dimension_semantics=("arbitrary",) * n_axes, vmem_limit_bytes=vmem)


MAX_WHOLE_DIM = 2304


def _pick(n, target):
    best = None
    for t in range(LANES, min(n, target) + 1, LANES):
        if n % t == 0:
            best = t
    if best is not None and (2 * best >= target or n > MAX_WHOLE_DIM):
        return best
    return n


def _sds(shape, dtype):
    return jax.ShapeDtypeStruct(tuple(shape), dtype)


def _in_cols():
    idx = -np.ones((IN_P,), np.int64)
    idx[C_CQ:C_CQ + 256] = np.arange(0, 256)
    idx[C_CKV:C_CKV + 128] = np.arange(256, 384)
    idx[C_KR + 64:C_KR + 96] = np.arange(384, 416)
    idx[C_DQ:C_DQ + 1152] = np.arange(416, 1568)
    gq = 1568 + (np.array([0, 2, 1, 3])[:, None] * 64 + np.arange(64)[None, :]).reshape(-1)
    idx[C_GQ:C_GQ + 256] = gq
    idx[C_GK:C_GK + 256] = np.arange(1824, 2080)
    return idx


def _uq_cols():
    idx = -np.ones((MLA_HEADS * 128,), np.int64)
    for h in range(MLA_HEADS):
        idx[h * 128:h * 128 + 96] = np.arange(96 * h, 96 * h + 96)
    return idx


def _ukv_cols():
    idx = -np.ones((MLA_HEADS * 128 + MLA_HEADS * 64,), np.int64)
    for h in range(MLA_HEADS):
        idx[h * 128:h * 128 + 64] = np.arange(128 * h, 128 * h + 64)
        idx[768 + h * 64:768 + h * 64 + 64] = np.arange(128 * h + 64, 128 * h + 128)
    return idx


def _out_rows():
    idx = np.arange(1024)
    idx[768:1024] = 768 + (np.array([0, 2, 1, 3])[:, None] * 64 + np.arange(64)[None, :]).reshape(-1)
    return idx


def _runs(idx):
    out, i = [], 0
    while i < len(idx):
        j = i + 1
        while j < len(idx) and ((idx[i] < 0 and idx[j] < 0) or (idx[i] >= 0 and idx[j] == idx[j - 1] + 1)):
            j += 1
        out.append((int(idx[i]), j - i))
        i = j
    return out


def _rows_from_shards(sh, idx):
    _, cs, r = sh.shape
    pieces = []
    for first, ln in _runs(idx):
        if first < 0:
            pieces.append(jnp.zeros((ln, r), sh.dtype))
            continue
        while ln > 0:
            k, off = divmod(first, cs)
            take = min(ln, cs - off)
            pieces.append(sh[k, off:off + take, :])
            first, ln = first + take, ln - take
    return jnp.concatenate(pieces, axis=0)


def _rows_to_shards(wp, idx, n):
    inv = np.zeros((n,), np.int64)
    pos = np.nonzero(idx >= 0)[0]
    inv[idx[pos]] = pos
    cs = n // 4
    shards = []
    for k in range(4):
        pieces = [wp[first:first + ln, :] for first, ln in _runs(inv[k * cs:(k + 1) * cs])]
        shards.append(jnp.concatenate(pieces, axis=0))
    return jnp.stack(shards)


def _t5_bucket_np(rel):
    nb = REL_BUCKETS // 2
    exact = nb // 2
    ret = np.where(rel > 0, nb, 0)
    n = np.abs(rel)
    nf = np.maximum(n, 1).astype(np.float32)
    large = exact + (np.log(nf / np.float32(exact)) / np.float32(math.log(REL_MAX_DIST / exact))
                     * np.float32(nb - exact)).astype(np.int32)
    large = np.minimum(large, nb - 1)
    return ret + np.where(n < exact, n, large)


def _branch_bucket_idx(tq, dil):
    kw = tq + 2 * DIL_HALF
    rel = np.arange(kw)[None, :] - DIL_HALF - np.arange(tq)[:, None]
    idx = _t5_bucket_np(rel * dil)
    return np.where(np.abs(rel) <= DIL_HALF, idx, -1).astype(np.int32)


def _rope_tables(S):
    inv = ROPE_THETA ** (-jnp.arange(0, 32, 2, dtype=F32) / 32)
    t = jnp.arange(S)
    pos = t.astype(F32)
    row = (t // GRID_W).astype(F32)
    col = (t % GRID_W).astype(F32)
    lane = np.arange(LANES)
    wm = lane - 64
    is_rope = (lane >= 64) & (lane < 96)
    ang = pos[:, None] * inv[np.where(is_rope, wm % 16, 0)][None, :]
    cm = jnp.where(is_rope[None], jnp.cos(ang), 1.0)
    smm = jnp.where((is_rope & (wm < 16))[None], -jnp.sin(ang), 0.0)
    spm = jnp.where((is_rope & (wm >= 16))[None], jnp.sin(ang), 0.0)
    g = lane % 64
    w = g % 32
    angg = jnp.where((g < 32)[None], row[:, None], col[:, None]) * inv[w % 16][None, :]
    cg = jnp.cos(angg)
    smg = jnp.where((w < 16)[None], -jnp.sin(angg), 0.0)
    spg = jnp.where((w >= 16)[None], jnp.sin(angg), 0.0)
    return (cm, smm, spm), (cg, smg, spg)


def _lanes(t, width):
    return t if width == LANES else jnp.concatenate([t] * (width // LANES), axis=1)


def _rope(x, tabs):
    c, sm, sp = (_lanes(t, x.shape[1]) for t in tabs)
    w = x.shape[1]
    return x * c + pltpu.roll(x, w - 16, 1) * sm + pltpu.roll(x, 16, 1) * sp


def _rope_t(dy, tabs):
    c, sm, sp = (_lanes(t, dy.shape[1]) for t in tabs)
    w = dy.shape[1]
    return dy * c + pltpu.roll(dy * sm, 16, 1) + pltpu.roll(dy * sp, w - 16, 1)


def _head_ones(width):
    i = np.arange(width)
    return jnp.asarray((i[:, None] // HEAD_DIM == i[None, :] // HEAD_DIM).astype(np.float32))


def _headsum(x, j):
    return jnp.dot(x, j, preferred_element_type=F32, precision=lax.Precision.HIGHEST)


def _mm(a, b, *, ta=False, tb=False, ga=False, gb=False, go=False, out_dtype=F32, add=None, name):
    G = a.shape[0] if ga else (b.shape[0] if gb else 1)
    a2 = a.shape[1:] if ga else a.shape
    b2 = b.shape[1:] if gb else b.shape
    K, M = a2 if ta else a2[::-1]
    N = b2[0] if tb else b2[1]
    assert (b2[1] if tb else b2[0]) == K
    tm, tn, tk = _pick(M, 1024), _pick(N, 1024), _pick(K, 1024)
    if tm * tn > 1024 * 1152:
        tm = _pick(M, 512)
    nk = K // tk
    steps = nk if (go or G == 1) else G * nk
    dn = (((0 if ta else 1,), (1 if tb else 0,)), ((), ()))

    def body(a_ref, b_ref, *rest):
        rest = list(rest)
        add_ref = rest.pop(0) if add is not None else None
        o_ref = rest.pop(0)
        part = lax.dot_general(a_ref[...], b_ref[...], dn, preferred_element_type=F32)
        if steps == 1:
            if add_ref is not None:
                part = part + add_ref[...]
            o_ref[...] = part.astype(o_ref.dtype)
            return
        acc_ref, = rest
        s = pl.program_id(3)

        @pl.when(s == 0)
        def _():
            acc_ref[...] = part if add_ref is None else part + add_ref[...]

        @pl.when(s > 0)
        def _():
            acc_ref[...] += part

        @pl.when(s == steps - 1)
        def _():
            o_ref[...] = acc_ref[...].astype(o_ref.dtype)

    def grp(g, s):
        return g if go else s // nk

    def kk(s):
        return s if steps == nk else s % nk

    def spec(grouped, block, index):
        if grouped:
            return pl.BlockSpec((None,) + block, lambda g, i, j, s: (grp(g, s),) + index(i, j, s))
        return pl.BlockSpec(block, lambda g, i, j, s: index(i, j, s))

    a_spec = (spec(ga, (tk, tm), lambda i, j, s: (kk(s), i)) if ta else spec(ga, (tm, tk), lambda i, j, s: (i, kk(s))))
    b_spec = (spec(gb, (tn, tk), lambda i, j, s: (j, kk(s))) if tb else spec(gb, (tk, tn), lambda i, j, s: (kk(s), j)))
    o_spec = spec(go, (tm, tn), lambda i, j, s: (i, j))
    return pl.pallas_call(
        body, name=name, grid=(G if go else 1, M // tm, N // tn, steps),
        in_specs=[a_spec, b_spec] + ([o_spec] if add is not None else []), out_specs=o_spec,
        out_shape=_sds(((G,) if go else ()) + (M, N), out_dtype),
        scratch_shapes=[pltpu.VMEM((tm, tn), F32)] if steps > 1 else [],
        compiler_params=_cparams(4),
    )(*([a, b] + ([add] if add is not None else [])))


def _row(ts, w, cb=0):
    return pl.BlockSpec((ts, w), lambda i: (i, cb))


def _full(shape):
    nd = len(shape)
    return pl.BlockSpec(tuple(shape), lambda i: (0,) * nd)


def _rms_fwd(x, g, eps=1e-6):
    r = lax.rsqrt(jnp.mean(x * x, axis=-1, keepdims=True) + eps)
    return x * r * g


def _rms_bwd(x, g, dy, eps=1e-6):
    r = lax.rsqrt(jnp.mean(x * x, axis=-1, keepdims=True) + eps)
    gdy = g * dy
    dx = r * gdy - x * (r * r * r) * jnp.mean(x * gdy, axis=-1, keepdims=True)
    return dx, x * r * dy


def _rms_head_fwd(x, g, j, eps=1e-6):
    r = lax.rsqrt(_headsum(x * x, j) * (1.0 / HEAD_DIM) + eps)
    return x * r * g


def _rms_head_bwd(x, g, dy, j, eps=1e-6):
    r = lax.rsqrt(_headsum(x * x, j) * (1.0 / HEAD_DIM) + eps)
    gdy = g * dy
    dx = r * gdy - x * (r * r * r) * (_headsum(x * gdy, j) * (1.0 / HEAD_DIM))
    return dx, x * r * dy


DIL_STRIDES = tuple(d for _, d in DIL_BRANCHES if d > 1)
DIL_W = DIL_HEADS * HEAD_DIM


def _res_spec(d, n, pad_blocks=0):
    return pl.BlockSpec((d, n, DIL_W), lambda i: (0, i + pad_blocks, 0))


def _prep_fwd(h, gq, gkv, ggq, ggk, tm, tg, j256):
    S = h.shape[0]
    ts = min(256, S)
    scale = HEAD_DIM ** -0.5
    nres = len(DIL_STRIDES)

    def body(*refs):
        (h_ref, gq_ref, gkv_ref, ggq_ref, ggk_ref, cm, smm, spm, cg, smg, spg, j_ref), refs = refs[:12], refs[12:]
        refs = refs[2 * nres:]
        (cq_o, ckv_o, kr_o, dq_o, dk_o, dv_o, gq_o, gk_o, gv_o), res_o = refs[:9], refs[9:-1]
        st = refs[-1]
        tabm = (cm[...], smm[...], spm[...])
        tabg = (cg[...], smg[...], spg[...])
        cq_o[...] = _rms_fwd(h_ref[:, C_CQ:C_CQ + 256], gq_ref[...]).astype(CDT)
        ckv_o[...] = _rms_fwd(h_ref[:, C_CKV:C_CKV + 128], gkv_ref[...]).astype(CDT)
        kr_o[...] = _rope(h_ref[:, C_KR:C_KR + 128], tabm).astype(CDT)
        dq_o[...] = (h_ref[:, C_DQ:C_DQ + 384] * scale).astype(CDT)
        dk_o[...] = h_ref[:, C_DK:C_DK + 384].astype(CDT)
        dv_o[...] = h_ref[:, C_DV:C_DV + 384].astype(CDT)
        for j, lanes in _lane_blocks(3 * DIL_W):
            st[j] = h_ref[:, C_DQ + lanes.start:C_DQ + lanes.stop] * (scale if j < 3 else 1.0)
        for bi, d in enumerate(DIL_STRIDES):
            for c in range(d):
                rows = pl.ds(c, ts // d, stride=d)
                for j, lanes in _lane_blocks(3 * DIL_W):
                    res_o[3 * bi + j // 3][c, :, (j % 3) * LANES:(j % 3 + 1) * LANES] = st.at[j][rows, :].astype(CDT)
        qn = _rms_head_fwd(h_ref[:, C_GQ:C_GQ + 256], ggq_ref[...], j_ref[...])
        gq_o[...] = (_rope(qn, tabg) * scale).astype(CDT)
        kn = _rms_head_fwd(h_ref[:, C_GK:C_GK + 128], ggk_ref[...], j_ref[0:128, 0:128])
        gk_o[...] = _rope(kn, tabg).astype(CDT)
        gv_o[...] = h_ref[:, C_GV:C_GV + 128].astype(CDT)

    widths = (256, 128, 128, 384, 384, 384, 256, 128, 128)
    out_specs = [_row(ts, w) for w in widths]
    out_shape = [_sds((S, w), CDT) for w in widths]
    zeros, aliases = [], {}
    for d in DIL_STRIDES:
        n, L = ts // d, S // d
        out_specs += [_res_spec(d, n), _res_spec(d, n, DIL_HALF // n), _res_spec(d, n, DIL_HALF // n)]
        out_shape += [_sds((d, L, DIL_W), CDT)] + [_sds((d, L + 2 * DIL_HALF, DIL_W), CDT)] * 2
        for t in range(2):
            aliases[12 + len(zeros)] = len(out_shape) - 2 + t
            zeros.append(jnp.zeros((d, L + 2 * DIL_HALF, DIL_W), CDT))
    return pl.pallas_call(
        body, name="prep_fwd", grid=(S // ts,),
        in_specs=[_row(ts, IN_P), _full(gq.shape), _full(gkv.shape), _full(ggq.shape), _full(ggk.shape)]
        + [_row(ts, LANES)] * 6 + [_full(j256.shape)] + [pl.BlockSpec(memory_space=pl.ANY)] * len(zeros),
        out_specs=out_specs, out_shape=out_shape, input_output_aliases=aliases,
        scratch_shapes=[pltpu.VMEM((3 * DIL_W // LANES, ts, LANES), F32)],
        compiler_params=_cparams(1),
    )(h, gq, gkv, ggq, ggk, *tm, *tg, j256, *zeros)


def _prep_bwd(h, dcq, dckv, dkr, ddq, ddk, ddv, dgq, dgk, dgv, gq, gkv, ggq, ggk, tg, j256):
    S = h.shape[0]
    ts = min(256, S)
    scale = HEAD_DIM ** -0.5

    def body(h_ref, dcq_r, dckv_r, dkr_r, q1, q2, q3, k1, k2, k3, v1, v2, v3, dgq_r, dgk_r, dgv_r,
             gq_ref, gkv_ref, ggq_ref, ggk_ref, cg, smg, spg, j_ref,
             dh_o, ngq_o, ngkv_o, nggq_o, nggk_o, *scr):
        tabg = (cg[...], smg[...], spg[...])
        first = pl.program_id(0) == 0
        scr, = scr
        d2, d3 = DIL_STRIDES
        dq = q1[...] + _by_token(q2, scr, d2) + _by_token(q3, scr, d3)
        dk = k1[...] + _by_token(k2, scr, d2) + _by_token(k3, scr, d3)
        dv = v1[...] + _by_token(v2, scr, d2) + _by_token(v3, scr, d3)

        def acc(o_ref, val):
            s = jnp.sum(val, axis=0, keepdims=True)

            @pl.when(first)
            def _():
                o_ref[...] = s

            @pl.when(jnp.logical_not(first))
            def _():
                o_ref[...] += s

        dx, dg = _rms_bwd(h_ref[:, C_CQ:C_CQ + 256], gq_ref[...], dcq_r[...])
        dh_o[:, C_CQ:C_CQ + 256] = dx.astype(CDT)
        acc(ngq_o, dg)
        dx, dg = _rms_bwd(h_ref[:, C_CKV:C_CKV + 128], gkv_ref[...], dckv_r[...])
        dh_o[:, C_CKV:C_CKV + 128] = dx.astype(CDT)
        acc(ngkv_o, dg)
        dh_o[:, C_KR:C_KR + 128] = dkr_r[...].astype(CDT)
        dh_o[:, C_DQ:C_DQ + 384] = (dq * scale).astype(CDT)
        dh_o[:, C_DK:C_DK + 384] = dk.astype(CDT)
        dh_o[:, C_DV:C_DV + 384] = dv.astype(CDT)
        dqn = _rope_t(dgq_r[...] * scale, tabg)
        dx, dg = _rms_head_bwd(h_ref[:, C_GQ:C_GQ + 256], ggq_ref[...], dqn, j_ref[...])
        dh_o[:, C_GQ:C_GQ + 256] = dx.astype(CDT)
        acc(nggq_o, dg)
        dkn = _rope_t(dgk_r[...], tabg)
        dx, dg = _rms_head_bwd(h_ref[:, C_GK:C_GK + 128], ggk_ref[...], dkn, j_ref[0:128, 0:128])
        dh_o[:, C_GK:C_GK + 128] = dx.astype(CDT)
        acc(nggk_o, dg)
        dh_o[:, C_GV:C_GV + 128] = dgv_r[...].astype(CDT)

    d2, d3 = DIL_STRIDES
    n2, n3 = ts // d2, ts // d3
    tok = _row(ts, DIL_W)
    return pl.pallas_call(
        body, name="prep_bwd", grid=(S // ts,),
        in_specs=[_row(ts, IN_P), _row(ts, 256), _row(ts, 128), _row(ts, 128)]
        + [tok, _res_spec(d2, n2), _res_spec(d3, n3)]
        + [tok, _res_spec(d2, n2, DIL_HALF // n2), _res_spec(d3, n3, DIL_HALF // n3)] * 2
        + [_row(ts, 256), _row(ts, 128), _row(ts, 128)]
        + [_full(gq.shape), _full(gkv.shape), _full(ggq.shape), _full(ggk.shape)] + [_row(ts, LANES)] * 3
        + [_full(j256.shape)],
        out_specs=[_row(ts, IN_P), _full((1, 256)), _full((1, 128)), _full((1, 256)), _full((1, 128))],
        out_shape=[_sds((S, IN_P), CDT), _sds((1, 256), F32), _sds((1, 128), F32), _sds((1, 256), F32),
                   _sds((1, 128), F32)],
        scratch_shapes=[_TOKEN_SCRATCH(ts)],
        compiler_params=_cparams(1),
    )(h, dcq, dckv, dkr, *ddq, *ddk, *ddv, dgq, dgk, dgv, gq, gkv, ggq, ggk, *tg, j256)


def _lane_blocks(width):
    return [(j, slice(j * LANES, (j + 1) * LANES)) for j in range(width // LANES)]


_TOKEN_SCRATCH = lambda ts: pltpu.VMEM((DIL_W // LANES, ts, LANES), F32)


def _by_token(res_ref, scr_ref, d):
    n = res_ref.shape[1]
    if d == 1:
        return res_ref[0].astype(F32)
    for c in range(d):
        for j, lanes in _lane_blocks(res_ref.shape[2]):
            scr_ref.at[j][pl.ds(c, n, stride=d), :] = res_ref[c, :, lanes].astype(F32)
    return jnp.concatenate([scr_ref[j] for j, _ in _lane_blocks(res_ref.shape[2])], axis=1)


def _by_residue(val, scr_ref, out_ref, d):
    n = out_ref.shape[1]
    if d == 1:
        out_ref[0] = val.astype(out_ref.dtype)
        return
    for j, lanes in _lane_blocks(out_ref.shape[2]):
        scr_ref[j] = val[:, lanes]
    for c in range(d):
        for j, lanes in _lane_blocks(out_ref.shape[2]):
            out_ref[c, :, lanes] = scr_ref.at[j][pl.ds(c, n, stride=d), :].astype(out_ref.dtype)


def _mla_prep_fwd(qa, kvp, kr, tm, scale):
    S = qa.shape[0]
    ts = min(256, S)

    def body(qa_ref, kv_ref, kr_ref, cm, smm, spm, q_o, k_o):
        tabm = (cm[...], smm[...], spm[...])
        q_o[...] = (_rope(qa_ref[...], tabm) * scale).astype(CDT)
        k_o[...] = kv_ref[:, 0:768] + _lanes(kr_ref[...], 768)

    return pl.pallas_call(
        body, name="mla_prep_fwd", grid=(S // ts,),
        in_specs=[_row(ts, 768), _row(ts, 1152), _row(ts, 128)] + [_row(ts, LANES)] * 3,
        out_specs=[_row(ts, 768)] * 2, out_shape=[_sds((S, 768), CDT)] * 2,
        compiler_params=_cparams(1),
    )(qa, kvp, kr, *tm)


def _mla_prep_bwd(dq, dk, dv, tm, scale):
    S = dq.shape[0]
    ts = min(256, S)

    def body(dq_ref, dk_ref, dv_ref, cm, smm, spm, dqa_o, dkv_o, dkr_o):
        tabm = (cm[...], smm[...], spm[...])
        lane = lax.broadcasted_iota(jnp.int32, (1, LANES), 1)
        dqa_o[...] = _rope_t(dq_ref[...] * scale, tabm).astype(CDT)
        dkr = jnp.zeros((ts, LANES), F32)
        for hd in range(MLA_HEADS):
            blk = dk_ref[:, hd * 128:(hd + 1) * 128]
            dkv_o[:, hd * 128:(hd + 1) * 128] = jnp.where(lane < 64, blk, 0.0).astype(CDT)
            dkr = dkr + jnp.where((lane >= 64) & (lane < 96), blk, 0.0)
        dkv_o[:, 768:1152] = dv_ref[...].astype(CDT)
        dkr_o[...] = jnp.where((lane >= 64) & (lane < 96), _rope_t(dkr, tabm), 0.0)

    return pl.pallas_call(
        body, name="mla_prep_bwd", grid=(S // ts,),
        in_specs=[_row(ts, 768), _row(ts, 768), _row(ts, 384)] + [_row(ts, LANES)] * 3,
        out_specs=[_row(ts, 768), _row(ts, 1152), _row(ts, 128)],
        out_shape=[_sds((S, 768), CDT), _sds((S, 1152), CDT), _sds((S, 128), F32)],
        compiler_params=_cparams(1),
    )(dq, dk, dv, *tm)


def _ln_fwd(xa, xb, g, b, alpha, name):
    S, D = xa.shape
    ts = min(256, S)

    def body(xa_ref, xb_ref, g_ref, b_ref, y_o, yb_o, z_o):
        z = alpha * xa_ref[...] + xb_ref[...]
        mu = jnp.mean(z, axis=-1, keepdims=True)
        zc = z - mu
        var = jnp.mean(zc * zc, axis=-1, keepdims=True)
        y = zc * lax.rsqrt(var + 1e-5) * g_ref[...] + b_ref[...]
        y_o[...] = y
        yb_o[...] = y.astype(CDT)
        z_o[...] = z

    return pl.pallas_call(
        body, name=name, grid=(S // ts,),
        in_specs=[_row(ts, D), _row(ts, D), _full(g.shape), _full(b.shape)],
        out_specs=[_row(ts, D)] * 3, out_shape=[_sds((S, D), F32), _sds((S, D), CDT), _sds((S, D), F32)],
        compiler_params=_cparams(1),
    )(xa, xb, g, b)


def _ln_bwd(dya, dyb, z, g, alpha, name):
    S, D = z.shape
    ts = min(256, S)
    two = dyb is not None

    def body(*refs):
        if two:
            dya_ref, dyb_ref, z_ref, g_ref, dz_o, dzb_o, dg_o, db_o = refs
            dy = dya_ref[...] + alpha * dyb_ref[...]
        else:
            dya_ref, z_ref, g_ref, dz_o, dzb_o, dg_o, db_o = refs
            dy = dya_ref[...]
        z = z_ref[...]
        mu = jnp.mean(z, axis=-1, keepdims=True)
        zc = z - mu
        r = lax.rsqrt(jnp.mean(zc * zc, axis=-1, keepdims=True) + 1e-5)
        xh = zc * r
        dxh = dy * g_ref[...]
        dz = r * (dxh - jnp.mean(dxh, axis=-1, keepdims=True) - xh * jnp.mean(dxh * xh, axis=-1, keepdims=True))
        dz_o[...] = dz
        dzb_o[...] = dz.astype(CDT)
        sg = jnp.sum(dy * xh, axis=0, keepdims=True)
        sb = jnp.sum(dy, axis=0, keepdims=True)
        first = pl.program_id(0) == 0

        @pl.when(first)
        def _():
            dg_o[...] = sg
            db_o[...] = sb

        @pl.when(jnp.logical_not(first))
        def _():
            dg_o[...] += sg
            db_o[...] += sb

    ins = [dya] + ([dyb] if two else []) + [z, g]
    return pl.pallas_call(
        body, name=name, grid=(S // ts,),
        in_specs=[_row(ts, D)] * (3 if two else 2) + [_full(g.shape)],
        out_specs=[_row(ts, D), _row(ts, D), _full((1, D)), _full((1, D))],
        out_shape=[_sds((S, D), F32), _sds((S, D), CDT), _sds((1, D), F32), _sds((1, D), F32)],
        compiler_params=_cparams(1),
    )(*ins)


def _grp_spec(ts, w):
    return pl.BlockSpec((None, ts, w), lambda k, i: (k, i, 0))


def _ffn_up(xb, wg3, wu3):
    S, D = xb.shape
    G, Fc, _ = wg3.shape
    tm = _pick(S, 1024)
    wspec = pl.BlockSpec((None, Fc, D), lambda k, i: (k, 0, 0))

    def body(x_ref, wg_ref, wu_ref, g_o, u_o, a_o):
        x = x_ref[...]
        g = lax.dot_general(x, wg_ref[...], _NT, preferred_element_type=F32)
        u = lax.dot_general(x, wu_ref[...], _NT, preferred_element_type=F32)
        g_o[...] = g
        u_o[...] = u
        a_o[...] = (g / (1.0 + jnp.exp(-g)) * u).astype(CDT)

    return pl.pallas_call(
        body, name="ffn_up", grid=(G, S // tm),
        in_specs=[pl.BlockSpec((tm, D), lambda k, i: (i, 0)), wspec, wspec], out_specs=[_grp_spec(tm, Fc)] * 3,
        out_shape=[_sds((G, S, Fc), F32), _sds((G, S, Fc), F32), _sds((G, S, Fc), CDT)], compiler_params=_cparams(2),
    )(xb, wg3, wu3)


def _ffn_up_dx(dg3, du3, wg3, wu3):
    G, S, Fc = dg3.shape
    D = wg3.shape[2]
    tm = _pick(S, 1024)
    wspec = pl.BlockSpec((None, Fc, D), lambda i, k: (k, 0, 0))
    aspec = pl.BlockSpec((None, tm, Fc), lambda i, k: (k, i, 0))

    def body(dg_ref, du_ref, wg_ref, wu_ref, o_ref):
        part = (jnp.dot(dg_ref[...], wg_ref[...], preferred_element_type=F32)
                + jnp.dot(du_ref[...], wu_ref[...], preferred_element_type=F32))
        k = pl.program_id(1)

        @pl.when(k == 0)
        def _():
            o_ref[...] = part

        @pl.when(k > 0)
        def _():
            o_ref[...] += part

    return pl.pallas_call(
        body, name="ffn_up_dx", grid=(S // tm, G), in_specs=[aspec, aspec, wspec, wspec],
        out_specs=pl.BlockSpec((tm, D), lambda i, k: (i, 0)), out_shape=_sds((S, D), F32), compiler_params=_cparams(2),
    )(dg3, du3, wg3, wu3)


def _ffn_down_dx(dzb, wd3, g3, u3):
    S, D = dzb.shape
    G, Fc, _ = wd3.shape
    tm = _pick(S, 1024)

    def body(dz_ref, wd_ref, g_ref, u_ref, dg_o, du_o):
        da = lax.dot_general(dz_ref[...], wd_ref[...], _NT, preferred_element_type=F32)
        g = g_ref[...]
        sg = 1.0 / (1.0 + jnp.exp(-g))
        dg_o[...] = (da * u_ref[...] * (sg * (1.0 + g * (1.0 - sg)))).astype(CDT)
        du_o[...] = (da * (g * sg)).astype(CDT)

    return pl.pallas_call(
        body, name="ffn_down_dx", grid=(G, S // tm),
        in_specs=[pl.BlockSpec((tm, D), lambda k, i: (i, 0)), pl.BlockSpec((None, Fc, D), lambda k, i: (k, 0, 0)),
                  _grp_spec(tm, Fc), _grp_spec(tm, Fc)],
        out_specs=[_grp_spec(tm, Fc)] * 2, out_shape=[_sds((G, S, Fc), CDT)] * 2, compiler_params=_cparams(2),
    )(dzb, wd3, g3, u3)


def _loss_kernel(y, target):
    S, D = y.shape
    ts = min(256, S)

    def body(y_ref, t_ref, dy_o, loss_o):
        e = y_ref[...] - t_ref[...]
        dy_o[...] = e * (1.0 / D)
        part = jnp.sum(jnp.sum(e * e, axis=1, keepdims=True), axis=0, keepdims=True) * (0.5 / D)
        first = pl.program_id(0) == 0

        @pl.when(first)
        def _():
            loss_o[...] = part

        @pl.when(jnp.logical_not(first))
        def _():
            loss_o[...] += part

    return pl.pallas_call(
        body, name="loss", grid=(S // ts,), in_specs=[_row(ts, D)] * 2,
        out_specs=[_row(ts, D), _full((1, 1))], out_shape=[_sds((S, D), F32), _sds((1, 1), F32)],
        compiler_params=_cparams(1),
    )(y, target)


def _axpy(a, b, alpha, name):
    S, D = a.shape
    ts = min(256, S)

    def body(a_ref, b_ref, o_ref):
        o_ref[...] = a_ref[...] + alpha * b_ref[...]

    return pl.pallas_call(
        body, name=name, grid=(S // ts,), in_specs=[_row(ts, D)] * 2, out_specs=_row(ts, D),
        out_shape=_sds((S, D), F32), compiler_params=_cparams(1),
    )(a, b)


def _pair_masks():
    lane = lax.broadcasted_iota(jnp.int32, (1, LANES), 1)
    first = lane < HEAD_DIM
    return first, jnp.logical_not(first)


def _head_scalar(x, m):
    return jnp.max(jnp.where(m, x, -jnp.inf), axis=-1, keepdims=True)


_NT = (((1,), (1,)), ((), ()))
_TN = (((0,), (0,)), ((), ()))


def _attn_fwd(q, k, v, *, split, npairs, kblk, vblk, name):
    S = q.shape[0]
    qw = 256 if split else LANES
    tq = min(256, S)

    def body(q_ref, k_ref, v_ref, o_ref, lse_ref):
        masks = _pair_masks()
        outs, lses = [], []
        for hd in range(2):
            if split:
                qh = q_ref[:, hd * LANES:(hd + 1) * LANES]
                kh = k_ref[:, hd * LANES:(hd + 1) * LANES]
            else:
                qh = jnp.where(masks[hd], q_ref[...], jnp.zeros_like(q_ref[...]))
                kh = k_ref[...]
            s = lax.dot_general(qh, kh, _NT, preferred_element_type=F32)
            mx = jnp.max(s, axis=-1, keepdims=True)
            p = jnp.exp(s - mx)
            l = jnp.sum(p, axis=-1, keepdims=True)
            o = jnp.dot(p.astype(CDT), v_ref[...], preferred_element_type=F32)
            outs.append(o / l)
            lses.append(jnp.broadcast_to(mx + jnp.log(l), (tq, LANES)))
        o_ref[...] = jnp.where(masks[0], outs[0], outs[1]).astype(o_ref.dtype)
        lse_ref[...] = jnp.where(masks[0], lses[0], lses[1])

    return pl.pallas_call(
        body, name=name, grid=(npairs, S // tq),
        in_specs=[pl.BlockSpec((tq, qw), lambda p, i: (i, p)),
                  pl.BlockSpec((S, qw), lambda p, i: (0, kblk(p))),
                  pl.BlockSpec((S, LANES), lambda p, i: (0, vblk(p)))],
        out_specs=[pl.BlockSpec((tq, LANES), lambda p, i: (i, p))] * 2,
        out_shape=[_sds((S, LANES * npairs), CDT), _sds((S, LANES * npairs), F32)],
        compiler_params=_cparams(2),
    )(q, k, v)


def _attn_bwd(q, k, v, do, o, lse, *, split, npairs, kblk, vblk, doblk, shared_kv, name):
    S = q.shape[0]
    qw = 256 if split else LANES
    tq = min(256, S)
    nkv = 1 if shared_kv else npairs

    def body(q_ref, k_ref, v_ref, do_ref, o_ref, lse_ref, dq_ref, dk_ref, dv_ref):
        masks = _pair_masks()
        p_id, i_id = pl.program_id(0), pl.program_id(1)
        first = (i_id == 0) & ((p_id == 0) if shared_kv else True)
        do = do_ref[...]
        o = o_ref[...].astype(F32)
        lse = lse_ref[...]
        v = v_ref[...]
        dqs, dks, dvs = [], [], []
        for hd in range(2):
            m = masks[hd]
            if split:
                qh = q_ref[:, hd * LANES:(hd + 1) * LANES]
                kh = k_ref[:, hd * LANES:(hd + 1) * LANES]
            else:
                qh = jnp.where(m, q_ref[...], jnp.zeros_like(q_ref[...]))
                kh = k_ref[...]
            doh = jnp.where(m, do, 0.0)
            s = lax.dot_general(qh, kh, _NT, preferred_element_type=F32)
            p = jnp.exp(s - _head_scalar(lse, m))
            delta = jnp.sum(doh * o, axis=-1, keepdims=True)
            dohb = doh.astype(CDT)
            dp = lax.dot_general(dohb, v, _NT, preferred_element_type=F32)
            ds = (p * (dp - delta)).astype(CDT)
            dq = jnp.dot(ds, kh, preferred_element_type=F32)
            dqs.append(dq if split else jnp.where(m, dq, 0.0))
            dks.append(lax.dot_general(ds, qh, _TN, preferred_element_type=F32))
            dvs.append(lax.dot_general(p.astype(CDT), dohb, _TN, preferred_element_type=F32))
        if split:
            dq_ref[:, 0:LANES] = dqs[0]
            dq_ref[:, LANES:2 * LANES] = dqs[1]
        else:
            dq_ref[...] = dqs[0] + dqs[1]
        dv = dvs[0] + dvs[1]

        @pl.when(first)
        def _():
            if split:
                dk_ref[:, 0:LANES] = dks[0]
                dk_ref[:, LANES:2 * LANES] = dks[1]
            else:
                dk_ref[...] = dks[0] + dks[1]
            dv_ref[...] = dv

        @pl.when(jnp.logical_not(first))
        def _():
            if split:
                dk_ref[:, 0:LANES] += dks[0]
                dk_ref[:, LANES:2 * LANES] += dks[1]
            else:
                dk_ref[...] += dks[0] + dks[1]
            dv_ref[...] += dv

    kvo = (lambda p, i: (0, 0)) if shared_kv else (lambda p, i: (0, p))
    return pl.pallas_call(
        body, name=name, grid=(npairs, S // tq),
        in_specs=[pl.BlockSpec((tq, qw), lambda p, i: (i, p)),
                  pl.BlockSpec((S, qw), lambda p, i: (0, kblk(p))),
                  pl.BlockSpec((S, LANES), lambda p, i: (0, vblk(p))),
                  pl.BlockSpec((tq, LANES), lambda p, i: (i, doblk(p))),
                  pl.BlockSpec((tq, LANES), lambda p, i: (i, p)),
                  pl.BlockSpec((tq, LANES), lambda p, i: (i, p))],
        out_specs=[pl.BlockSpec((tq, qw), lambda p, i: (i, p)),
                   pl.BlockSpec((S, qw), kvo), pl.BlockSpec((S, LANES), kvo)],
        out_shape=[_sds((S, qw * npairs), F32), _sds((S, qw * nkv), F32), _sds((S, LANES * nkv), F32)],
        compiler_params=_cparams(2),
    )(q, k, v, do, o, lse)


def _bias_expand(idx, rel_bias, name):
    tq, kw = idx.shape

    def body(idx_ref, rb_ref, o_ref):
        idx = idx_ref[...]
        for hd in range(DIL_HEADS):
            acc = jnp.full((tq, kw), NEG_INF, F32)
            for u in range(REL_BUCKETS):
                acc = jnp.where(idx == u, rb_ref[u, hd], acc)
            o_ref[hd] = acc

    return pl.pallas_call(
        body, name=name,
        in_specs=[pl.BlockSpec(memory_space=pltpu.VMEM), pl.BlockSpec(memory_space=pltpu.SMEM)],
        out_specs=pl.BlockSpec(memory_space=pltpu.VMEM),
        out_shape=_sds((DIL_HEADS, tq, kw), F32),
    )(idx, rel_bias)


def _bias_reduce(idx, dtab, name):
    tq, kw = idx.shape

    def body(idx_ref, d_ref, o_ref):
        idx = idx_ref[...]
        rowid = lax.broadcasted_iota(jnp.int32, (REL_BUCKETS, kw), 0)
        for hd in range(DIL_HEADS):
            d = d_ref[hd]
            acc = jnp.zeros((REL_BUCKETS, kw), F32)
            for u in range(REL_BUCKETS):
                r = jnp.sum(jnp.where(idx == u, d, 0.0), axis=0, keepdims=True)
                acc = jnp.where(rowid == u, r, acc)
            o_ref[hd] = jnp.sum(acc, axis=1, keepdims=True)

    return pl.pallas_call(
        body, name=name,
        in_specs=[pl.BlockSpec(memory_space=pltpu.VMEM)] * 2, out_specs=pl.BlockSpec(memory_space=pltpu.VMEM),
        out_shape=_sds((DIL_HEADS, REL_BUCKETS, 1), F32),
    )(idx, dtab)


def _dil_window(i, tq, kw, L):
    start = pl.multiple_of(i * tq, DIL_HALF)
    key = start + lax.broadcasted_iota(jnp.int32, (1, kw), 1) - DIL_HALF
    return start, (key >= 0) & (key < L)


def _dil_fwd(qv, kv, vv, tab, *, dil, L, tq, name):
    kw = tq + 2 * DIL_HALF
    npair = DIL_HEADS // 2

    def body(q_ref, k_ref, v_ref, t_ref, o_ref, lse_ref):
        masks = _pair_masks()
        start, valid = _dil_window(pl.program_id(2), tq, kw, L)
        kwin = k_ref[pl.ds(start, kw), :]
        vwin = v_ref[pl.ds(start, kw), :]
        outs, lses = [], []
        for hd in range(2):
            qh = jnp.where(masks[hd], q_ref[...], jnp.zeros_like(q_ref[...]))
            s = lax.dot_general(qh, kwin, _NT, preferred_element_type=F32) + t_ref[hd]
            s = jnp.where(valid, s, NEG_INF)
            mx = jnp.max(s, axis=-1, keepdims=True)
            p = jnp.exp(s - mx)
            l = jnp.sum(p, axis=-1, keepdims=True)
            outs.append(jnp.dot(p.astype(CDT), vwin, preferred_element_type=F32) / l)
            lses.append(jnp.broadcast_to(mx + jnp.log(l), (tq, LANES)))
        o_ref[...] = jnp.where(masks[0], outs[0], outs[1])
        lse_ref[...] = jnp.where(masks[0], lses[0], lses[1])

    blk = pl.BlockSpec((None, tq, LANES), lambda p, c, i: (c, i, p))
    res = pl.BlockSpec((None, L + 2 * DIL_HALF, LANES), lambda p, c, i: (c, 0, p))
    return pl.pallas_call(
        body, name=name, grid=(npair, dil, L // tq),
        in_specs=[blk, res, res, pl.BlockSpec((2, tq, kw), lambda p, c, i: (p, 0, 0))],
        out_specs=[blk] * 2, out_shape=[_sds(qv.shape, F32)] * 2,
        compiler_params=_cparams(3),
    )(qv, kv, vv, tab)


def _dil_bwd(qv, kv, vv, tab, dov, lsev, deltav, *, dil, L, tq, name):
    kw = tq + 2 * DIL_HALF
    npair = DIL_HEADS // 2

    def body(q_ref, k_ref, v_ref, t_ref, do_ref, lse_ref, dl_ref, dq_ref, dk_ref, dv_ref, dt_ref):
        masks = _pair_masks()
        c_id, i_id = pl.program_id(1), pl.program_id(2)
        start, valid = _dil_window(i_id, tq, kw, L)
        kwin = k_ref[pl.ds(start, kw), :]
        vwin = v_ref[pl.ds(start, kw), :]

        @pl.when(i_id == 0)
        def _():
            dk_ref[...] = jnp.zeros_like(dk_ref)
            dv_ref[...] = jnp.zeros_like(dv_ref)

        @pl.when((i_id == 0) & (c_id == 0))
        def _():
            dt_ref[...] = jnp.zeros_like(dt_ref)

        do = do_ref[...]
        dq = jnp.zeros((tq, LANES), F32)
        dk = jnp.zeros((kw, LANES), F32)
        dv = jnp.zeros((kw, LANES), F32)
        for hd in range(2):
            m = masks[hd]
            qh = jnp.where(m, q_ref[...], jnp.zeros_like(q_ref[...]))
            doh = jnp.where(m, do, jnp.zeros_like(do))
            s = lax.dot_general(qh, kwin, _NT, preferred_element_type=F32) + t_ref[hd]
            s = jnp.where(valid, s, NEG_INF)
            p = jnp.exp(s - _head_scalar(lse_ref[...], m))
            dp = lax.dot_general(doh, vwin, _NT, preferred_element_type=F32)
            ds = p * (dp - _head_scalar(dl_ref[...], m))
            dt_ref[hd] += ds
            dsb = ds.astype(CDT)
            dq = dq + jnp.where(m, jnp.dot(dsb, kwin, preferred_element_type=F32), 0.0)
            dk = dk + lax.dot_general(dsb, qh, _TN, preferred_element_type=F32)
            dv = dv + lax.dot_general(p.astype(CDT), doh, _TN, preferred_element_type=F32)
        dq_ref[...] = dq
        dk_ref[pl.ds(start, kw), :] += dk
        dv_ref[pl.ds(start, kw), :] += dv

    blk = pl.BlockSpec((None, tq, LANES), lambda p, c, i: (c, i, p))
    res = pl.BlockSpec((None, L + 2 * DIL_HALF, LANES), lambda p, c, i: (c, 0, p))
    tsp = pl.BlockSpec((2, tq, kw), lambda p, c, i: (p, 0, 0))
    return pl.pallas_call(
        body, name=name, grid=(npair, dil, L // tq),
        in_specs=[blk, res, res, tsp, blk, blk, blk], out_specs=[blk, res, res, tsp],
        out_shape=[_sds(qv.shape, F32), _sds(kv.shape, F32), _sds(kv.shape, F32), _sds(tab.shape, F32)],
        compiler_params=_cparams(3),
    )(qv, kv, vv, tab, dov, lsev, deltav)


def _mix_weights(l1, l2, l3):
    mx = jnp.maximum(jnp.maximum(l1, l2), l3)
    e1, e2, e3 = jnp.exp(l1 - mx), jnp.exp(l2 - mx), jnp.exp(l3 - mx)
    inv = 1.0 / (e1 + e2 + e3)
    return e1 * inv, e2 * inv, e3 * inv


def _branch_specs(S, ts):
    dils = [d for _, d in DIL_BRANCHES]
    return dils, [_res_spec(d, ts // d) for d in dils], [(d, S // d, DIL_W) for d in dils]


def _dil_mix_fwd(os, ls):
    S = os[0].shape[0] * os[0].shape[1]
    ts = min(256, S)
    dils, specs, _ = _branch_specs(S, ts)

    def body(o1, o2, o3, l1, l2, l3, out, scr):
        o1, o2, o3, l1, l2, l3 = [_by_token(r, scr, d) for r, d in zip((o1, o2, o3, l1, l2, l3), dils + dils)]
        w1, w2, w3 = _mix_weights(l1, l2, l3)
        out[...] = (w1 * o1 + w2 * o2 + w3 * o3).astype(CDT)

    return pl.pallas_call(
        body, name="dil_mix_fwd", grid=(S // ts,), in_specs=specs + specs, out_specs=_row(ts, DIL_W),
        out_shape=_sds((S, DIL_W), CDT), scratch_shapes=[_TOKEN_SCRATCH(ts)],
        compiler_params=_cparams(1),
    )(*os, *ls)


def _dil_mix_bwd(dcat, os, ls, j384):
    S = os[0].shape[0] * os[0].shape[1]
    ts = min(256, S)
    dils, specs, shapes = _branch_specs(S, ts)

    def body(do_ref, o1, o2, o3, l1, l2, l3, j_ref, d1, d2, d3, e1, e2, e3, scr):
        o1, o2, o3, l1, l2, l3 = [_by_token(r, scr, d) for r, d in zip((o1, o2, o3, l1, l2, l3), dils + dils)]
        ws = _mix_weights(l1, l2, l3)
        do = do_ref[...]
        o = ws[0] * o1 + ws[1] * o2 + ws[2] * o3
        dot = _headsum(do * o, j_ref[...])
        for w, d, d_o, e_o in zip(ws, dils, (d1, d2, d3), (e1, e2, e3)):
            _by_residue(w * do, scr, d_o, d)
            _by_residue(w * dot, scr, e_o, d)

    return pl.pallas_call(
        body, name="dil_mix_bwd", grid=(S // ts,),
        in_specs=[_row(ts, DIL_W, 1)] + specs + specs + [_full(j384.shape)],
        out_specs=specs + specs,
        out_shape=[_sds(s, CDT) for s in shapes] + [_sds(s, F32) for s in shapes],
        scratch_shapes=[_TOKEN_SCRATCH(ts)],
        compiler_params=_cparams(1),
    )(dcat, *os, *ls, j384)


def _adamw_math(w, g, m, v):
    m = ADAM_B1 * m + (1.0 - ADAM_B1) * g
    v = ADAM_B2 * v + (1.0 - ADAM_B2) * (g * g)
    m_hat = m / (1.0 - ADAM_B1 ** ADAM_STEP)
    v_hat = v / (1.0 - ADAM_B2 ** ADAM_STEP)
    delta = -ADAM_LR * (m_hat / (jnp.sqrt(v_hat) + ADAM_EPS) + ADAM_WD * w)
    return delta, m, v


def _pick8(n, target):
    best = None
    for t in range(16, min(n, target) + 1, 16):
        if n % t == 0:
            best = t
    return best if best is not None else n


_ELEMS_PER_BLOCK = 256 * 1024


def _lead_spec(a, b):
    ta = _pick8(a, max(16, _ELEMS_PER_BLOCK // b))
    return ta, pl.BlockSpec((None, ta, b), lambda l, i: (l, i, 0))


def _adamw(w, reds, sibs, m, v, owner, name):
    L, a, b = w.shape
    ta, spec = _lead_spec(a, b)
    gspec = pl.BlockSpec((ta, b), lambda l, i: (i, 0))

    def body(w_ref, r0_ref, r1_ref, s0_ref, s1_ref, m_ref, v_ref, g_o, d_o, m_o, v_o):
        mine = lax.axis_index("c") == owner
        g0 = jnp.where(mine, r0_ref[...], s0_ref[...])
        g1 = jnp.where(mine, r1_ref[...], s1_ref[...])
        g = jnp.where(pl.program_id(0) == 0, g0, g1)
        d, mm, vv = _adamw_math(w_ref[...], g, m_ref[...], v_ref[...])
        g_o[...] = g
        d_o[...] = d
        m_o[...] = mm
        v_o[...] = vv

    return pl.pallas_call(
        body, name=name, grid=(L, a // ta), in_specs=[spec] + [gspec] * 4 + [spec, spec], out_specs=[spec] * 4,
        out_shape=[_sds(w.shape, F32)] * 4, compiler_params=_cparams(2),
    )(w, *reds, *sibs, m, v)


def _adamw_small(w, gall, m, v):
    R = w.shape[0]

    def body(w_ref, g_ref, m_ref, v_ref, g_o, d_o, m_o, v_o):
        g = g_ref[0]
        for k in range(1, 8):
            g = g + g_ref[k]
        d, mm, vv = _adamw_math(w_ref[...], g, m_ref[...], v_ref[...])
        g_o[...] = g
        d_o[...] = d
        m_o[...] = mm
        v_o[...] = vv

    vm = pl.BlockSpec(memory_space=pltpu.VMEM)
    return pl.pallas_call(
        body, name="adamw_small", in_specs=[vm] * 4, out_specs=[vm] * 4, out_shape=[_sds((R, LANES), F32)] * 4,
    )(w, gall, m, v)


def _sum_pair(g, t, owner, name):
    n, a, b = t.shape
    ta, spec = _lead_spec(a, b)

    def body(g_ref, t_ref, o_ref):
        @pl.when(lax.axis_index("c") == owner)
        def _():
            o_ref[...] = (g_ref[...] + t_ref[...]).astype(o_ref.dtype)

    return pl.pallas_call(
        body, name=name, grid=(n, a // ta), in_specs=[spec] * 2, out_specs=spec, out_shape=_sds(t.shape, WIRE),
        compiler_params=_cparams(2),
    )(g, t)


def _sum_chips(pair, t, owner, name):
    _, a, b = t.shape
    ta = _pick8(a, max(16, _ELEMS_PER_BLOCK // b))

    def body(p_ref, t_ref, o_ref):
        @pl.when(lax.axis_index("c") == owner)
        def _():
            me = 2 * lax.axis_index("x") + lax.axis_index("y")
            acc = p_ref[me].astype(F32)
            for k in range(3):
                acc = acc + t_ref[k].astype(F32)
            o_ref[...] = acc

    return pl.pallas_call(
        body, name=name, grid=(a // ta,),
        in_specs=[pl.BlockSpec((4, ta, b), lambda i: (0, i, 0)), pl.BlockSpec((3, ta, b), lambda i: (0, i, 0))],
        out_specs=pl.BlockSpec((ta, b), lambda i: (i, 0)), out_shape=_sds((a, b), F32), compiler_params=_cparams(1),
    )(pair, t)


_HBM = pl.BlockSpec(memory_space=pltpu.HBM)


def _place():
    x, y, c = lax.axis_index("x"), lax.axis_index("y"), lax.axis_index("c")
    chips = [(1 - x, y), (x, 1 - y), (1 - x, 1 - y)]
    return x, y, c, chips


def _remote(src, dst, ssem, rsem, to):
    return pltpu.make_async_remote_copy(src_ref=src, dst_ref=dst, send_sem=ssem, recv_sem=rsem, device_id=to,
                                        device_id_type=MESH_ID)


def _dma_sems(n):
    return pltpu.SemaphoreType.DMA((n,))


_SEM = pl.BlockSpec(memory_space=pltpu.SEMAPHORE)
_ANY = pl.BlockSpec(memory_space=pl.ANY)
_EFFECT = pltpu.SideEffectType.DATAFLOW_SIDE_EFFECTING
_GROUPS = ((1, 2, 4, 5), (0, 3, 6))
_OWNER = (1, 0, 0, 1, 0, 0, 1)


def _hbm(a):
    return pltpu.with_memory_space_constraint(a, pltpu.HBM)


def _per_core(c, fn):
    for g in range(2):
        pl.when(c == g)(functools.partial(fn, _GROUPS[g], _GROUPS[1 - g]))


def _token_spec():
    return pl.BlockSpec(memory_space=pltpu.VMEM), _sds((8, LANES), F32)


def _gather_start(shards, layer):
    n = len(shards)
    lands = [_hbm(lax.empty((4,) + s.shape[1:], s.dtype)) for s in shards]

    def body(*refs):
        w_refs, l_refs = refs[:n], refs[n:2 * n]
        ssem, rsem, token = refs[2 * n], refs[2 * n + 1], refs[-1]
        x, y, c, chips = _place()
        me = 2 * x + y

        def send(mine, _):
            for i in mine:
                for j, (cx, cy) in enumerate(chips):
                    _remote(w_refs[i].at[layer], l_refs[i].at[me], ssem.at[3 * i + j], rsem.at[3 * i + j],
                            (cx, cy, c)).start()

        _per_core(c, send)
        token[...] = jnp.zeros_like(token)

    tspec, tshape = _token_spec()
    out = pl.pallas_call(
        body, name=f"gather_start_{layer}", in_specs=[_HBM] * (2 * n),
        out_specs=[_SEM, _SEM] + [_HBM] * n + [tspec],
        out_shape=[_dma_sems(3 * n), _dma_sems(3 * n)] + [pltpu.HBM(l.shape, l.dtype) for l in lands] + [tshape],
        input_output_aliases={n + i: 2 + i for i in range(n)},
        compiler_params=pltpu.CompilerParams(has_side_effects=_EFFECT),
    )(*[_hbm(s) for s in shards], *lands)
    return out[0], out[1], list(out[2:2 + n]), out[-1]


def _gather_wait(ssem, rsem, shards, lands, after, layer):
    n = len(shards)

    def body(*refs):
        w_refs, l_refs = refs[:n], refs[n:2 * n]
        ssem, rsem = refs[2 * n], refs[2 * n + 1]
        x, y, c, chips = _place()

        def wait(mine, _):
            for i in mine:
                for j, (cx, cy) in enumerate(chips):
                    cp = _remote(w_refs[i].at[layer], l_refs[i].at[2 * cx + cy], ssem.at[3 * i + j], rsem.at[3 * i + j],
                                 (cx, cy, c))
                    cp.wait_send()
                    cp.wait_recv()

        _per_core(c, wait)

    return list(pl.pallas_call(
        body, name=f"gather_wait_{layer}", in_specs=[_HBM] * (2 * n) + [_SEM, _SEM, _ANY], out_specs=[_HBM] * n,
        out_shape=[pltpu.HBM(l.shape, l.dtype) for l in lands],
        input_output_aliases={n + i: i for i in range(n)},
        compiler_params=pltpu.CompilerParams(has_side_effects=_EFFECT),
    )(*[_hbm(s) for s in shards], *lands, ssem, rsem, after))


def _gather_finish(shards, lands, layer):
    n = len(shards)

    def body(*refs):
        w_refs, g_refs = refs[:n], refs[2 * n:3 * n]
        ssem, rsem = refs[3 * n:]
        x, y, c, chips = _place()
        me = 2 * x + y
        sib = (x, y, 1 - c)
        owns = [_remote(w.at[layer], g.at[me], ssem.at[i], rsem.at[i], sib) for i, (w, g) in enumerate(zip(w_refs, g_refs))]
        for cp in owns:
            cp.start()

        def forward(mine, theirs):
            def blk(i, j):
                b = g_refs[i].at[2 * chips[j][0] + chips[j][1]]
                return _remote(b, b, ssem.at[n + 3 * i + j], rsem.at[n + 3 * i + j], sib)

            for i in mine:
                for j in range(3):
                    blk(i, j).start()
            for i in theirs:
                for j in range(3):
                    blk(i, j).wait_recv()
            for i in mine:
                for j in range(3):
                    blk(i, j).wait_send()

        _per_core(c, forward)
        for cp in owns:
            cp.wait_recv()
            cp.wait_send()

    return list(pl.pallas_call(
        body, name=f"gather_finish_{layer}", in_specs=[_HBM] * (2 * n), out_specs=[_HBM] * n,
        out_shape=[_sds(l.shape, l.dtype) for l in lands], input_output_aliases={n + i: i for i in range(n)},
        scratch_shapes=[_dma_sems(4 * n), _dma_sems(4 * n)],
    )(*shards, *lands))


def _rs_to_owner(grads):
    n = len(grads)

    def body(*refs):
        g_refs, t_refs = refs[:n], refs[n:2 * n]
        ssem, rsem = refs[2 * n:]
        x, y, c, _ = _place()

        def swap(mine, theirs):
            cps = [_remote(g_refs[i], t_refs[i], ssem.at[i], rsem.at[i], (x, y, 1 - c)) for i in theirs]
            for cp in cps:
                cp.start()
            for i in mine:
                _remote(g_refs[i], t_refs[i], ssem.at[i], rsem.at[i], (x, y, 1 - c)).wait_recv()
            for cp in cps:
                cp.wait_send()

        _per_core(c, swap)

    return list(pl.pallas_call(
        body, name="rs_to_owner", in_specs=[_HBM] * n, out_specs=[_HBM] * n,
        out_shape=[_sds(g.shape, g.dtype) for g in grads], scratch_shapes=[_dma_sems(n), _dma_sems(n)],
    )(*grads))


def _a2a_start(pairs, layer):
    n = len(pairs)
    lands = [_hbm(lax.empty((3,) + p.shape[1:], p.dtype)) for p in pairs]

    def body(*refs):
        a_refs, t_refs = refs[:n], refs[n:2 * n]
        ssem, rsem, token = refs[2 * n], refs[2 * n + 1], refs[-1]
        x, y, c, chips = _place()

        def send(mine, _):
            for i in mine:
                for j, (cx, cy) in enumerate(chips):
                    _remote(a_refs[i].at[2 * cx + cy], t_refs[i].at[j], ssem.at[3 * i + j], rsem.at[3 * i + j],
                            (cx, cy, c)).start()

        _per_core(c, send)
        token[...] = jnp.zeros_like(token)

    tspec, tshape = _token_spec()
    out = pl.pallas_call(
        body, name=f"rs_a2a_start_{layer}", in_specs=[_HBM] * (2 * n),
        out_specs=[_SEM, _SEM] + [_HBM] * n + [tspec],
        out_shape=[_dma_sems(3 * n), _dma_sems(3 * n)] + [pltpu.HBM(l.shape, l.dtype) for l in lands] + [tshape],
        input_output_aliases={n + i: 2 + i for i in range(n)},
        compiler_params=pltpu.CompilerParams(has_side_effects=_EFFECT),
    )(*[_hbm(p) for p in pairs], *lands)
    return out[0], out[1], list(out[2:2 + n]), out[-1]


def _a2a_wait(ssem, rsem, pairs, lands, after, layer):
    n = len(pairs)

    def body(*refs):
        a_refs, t_refs = refs[:n], refs[n:2 * n]
        ssem, rsem = refs[2 * n], refs[2 * n + 1]
        x, y, c, chips = _place()

        def wait(mine, _):
            for i in mine:
                for j, (cx, cy) in enumerate(chips):
                    cp = _remote(a_refs[i].at[2 * cx + cy], t_refs[i].at[j], ssem.at[3 * i + j], rsem.at[3 * i + j],
                                 (cx, cy, c))
                    cp.wait_send()
                    cp.wait_recv()

        _per_core(c, wait)

    return list(pl.pallas_call(
        body, name=f"rs_a2a_wait_{layer}", in_specs=[_HBM] * (2 * n) + [_SEM, _SEM, _ANY], out_specs=[_HBM] * n,
        out_shape=[pltpu.HBM(l.shape, l.dtype) for l in lands],
        input_output_aliases={n + i: i for i in range(n)},
        compiler_params=pltpu.CompilerParams(has_side_effects=_EFFECT),
    )(*[_hbm(p) for p in pairs], *lands, ssem, rsem, after))


def _rs_from_owner(reds):
    n = len(reds)
    per_layer = len(_BIG)

    def body(*refs):
        q_refs, o_refs = refs[:n], refs[n:2 * n]
        ssem, rsem = refs[2 * n:]
        x, y, c, _ = _place()

        def swap(mine, theirs):
            cps = [_remote(q_refs[k], o_refs[k], ssem.at[k], rsem.at[k], (x, y, 1 - c))
                   for k in range(n) if k % per_layer in mine]
            for cp in cps:
                cp.start()
            for k in range(n):
                if k % per_layer in theirs:
                    _remote(q_refs[k], o_refs[k], ssem.at[k], rsem.at[k], (x, y, 1 - c)).wait_recv()
            for cp in cps:
                cp.wait_send()

        _per_core(c, swap)

    return list(pl.pallas_call(
        body, name="rs_from_owner", in_specs=[_HBM] * n, out_specs=[_HBM] * n,
        out_shape=[_sds(q.shape, q.dtype) for q in reds], scratch_shapes=[_dma_sems(n), _dma_sems(n)],
    )(*reds))


def _gather_weights(shards):
    n = len(shards)

    def body(*refs):
        w_refs, g_refs = refs[:n], refs[n:2 * n]
        ssem, rsem = refs[2 * n:]
        x, y, c, chips = _place()
        me = 2 * x + y
        sib = (x, y, 1 - c)
        owns = [_remote(w.at[l], g.at[l, me], ssem.at[6 * n + 2 * i + l], rsem.at[6 * n + 2 * i + l], sib)
                for i, (w, g) in enumerate(zip(w_refs, g_refs)) for l in range(2)]
        first = [_remote(w.at[c], g.at[c, me], ssem.at[3 * i + j], rsem.at[3 * i + j], (cx, cy, c))
                 for j, (cx, cy) in enumerate(chips) for i, (w, g) in enumerate(zip(w_refs, g_refs))]
        for cp in first + owns:
            cp.start()
        passed = []
        for j, (cx, cy) in enumerate(chips):
            for i, g in enumerate(g_refs):
                blk = g.at[c, 2 * cx + cy]
                _remote(blk, blk, ssem.at[3 * i + j], rsem.at[3 * i + j], sib).wait_recv()
                fw = _remote(blk, blk, ssem.at[3 * n + 3 * i + j], rsem.at[3 * n + 3 * i + j], sib)
                fw.start()
                passed.append(fw)
        for j, (cx, cy) in enumerate(chips):
            for i, g in enumerate(g_refs):
                blk = g.at[1 - c, 2 * cx + cy]
                _remote(blk, blk, ssem.at[3 * n + 3 * i + j], rsem.at[3 * n + 3 * i + j], sib).wait_recv()
        for cp in owns:
            cp.wait_recv()
        for cp in first + passed + owns:
            cp.wait_send()

    return pl.pallas_call(
        body, name="gather_weights", in_specs=[_HBM] * n, out_specs=[_HBM] * n,
        out_shape=[_sds((2, 4) + s.shape[1:], s.dtype) for s in shards],
        scratch_shapes=[_dma_sems(8 * n), _dma_sems(8 * n)],
    )(*shards)


def _sibling_halves(g0s, g1s):
    n = len(g0s)

    def body(*refs):
        g0_refs, g1_refs, t_refs = refs[:n], refs[n:2 * n], refs[2 * n:3 * n]
        ssem, rsem = refs[3 * n:]
        x, y, c, _ = _place()

        def swap(srcs):
            cps = [_remote(s, t, ssem.at[i], rsem.at[i], (x, y, 1 - c)) for i, (s, t) in enumerate(zip(srcs, t_refs))]
            for cp in cps:
                cp.start()
            for cp in cps:
                cp.wait()

        @pl.when(c == 0)
        def _():
            swap(g1_refs)

        @pl.when(c == 1)
        def _():
            swap(g0_refs)

    return pl.pallas_call(
        body, name="rs_sibling_halves", in_specs=[_HBM] * (2 * n), out_specs=[_HBM] * n,
        out_shape=[_sds(g.shape, g.dtype) for g in g0s], scratch_shapes=[_dma_sems(n), _dma_sems(n)],
    )(*g0s, *g1s)


def _chip_all_to_all(parts):
    n = len(parts)

    def body(*refs):
        a_refs, t_refs = refs[:n], refs[n:2 * n]
        ssem, rsem = refs[2 * n:]
        x, y, c, chips = _place()
        sends = [_remote(a.at[2 * cx + cy], t.at[j], ssem.at[3 * i + j], rsem.at[3 * i + j], (cx, cy, c))
                 for j, (cx, cy) in enumerate(chips) for i, (a, t) in enumerate(zip(a_refs, t_refs))]
        for cp in sends:
            cp.start()
        for cp in sends:
            cp.wait_recv()
        for cp in sends:
            cp.wait_send()

    return pl.pallas_call(
        body, name="rs_chip_all_to_all", in_specs=[_HBM] * n, out_specs=[_HBM] * n,
        out_shape=[_sds((3,) + p.shape[1:], p.dtype) for p in parts],
        scratch_shapes=[_dma_sems(3 * n), _dma_sems(3 * n)],
    )(*parts)


def _sibling_swap(reds):
    n = len(reds)

    def body(*refs):
        q_refs, o_refs = refs[:n], refs[n:2 * n]
        ssem, rsem = refs[2 * n:]
        x, y, c, _ = _place()
        cps = [_remote(q, o, ssem.at[i], rsem.at[i], (x, y, 1 - c)) for i, (q, o) in enumerate(zip(q_refs, o_refs))]
        for cp in cps:
            cp.start()
        for cp in cps:
            cp.wait()

    return pl.pallas_call(
        body, name="rs_sibling_swap", in_specs=[_HBM] * n, out_specs=[_HBM] * n,
        out_shape=[_sds(q.shape, q.dtype) for q in reds],
        scratch_shapes=[_dma_sems(n), _dma_sems(n)],
    )(*reds)


def _gather_small(s):
    R, _ = s.shape

    def body(s_ref, o_ref, ssem, rsem, lsem):
        x, y, c, _ = _place()
        me = 4 * x + 2 * y + c
        own = pltpu.make_async_copy(s_ref, o_ref.at[me], lsem)
        own.start()
        sends = []
        for k in range(1, 8):
            px, py, pc = x ^ (k >> 2), y ^ ((k >> 1) & 1), c ^ (k & 1)
            cp = _remote(s_ref, o_ref.at[me], ssem.at[k - 1], rsem.at[k - 1], (px, py, pc))
            cp.start()
            sends.append(cp)
        for k in range(1, 8):
            px, py, pc = x ^ (k >> 2), y ^ ((k >> 1) & 1), c ^ (k & 1)
            blk = o_ref.at[4 * px + 2 * py + pc]
            _remote(blk, blk, ssem.at[k - 1], rsem.at[k - 1], (px, py, pc)).wait_recv()
        for cp in sends:
            cp.wait_send()
        own.wait()

    vm = pl.BlockSpec(memory_space=pltpu.VMEM)
    return pl.pallas_call(
        body, name="gather_small", in_specs=[vm], out_specs=vm, out_shape=_sds((8, R, LANES), s.dtype),
        scratch_shapes=[pltpu.SemaphoreType.DMA((7,)), pltpu.SemaphoreType.DMA((7,)), pltpu.SemaphoreType.DMA],
    )(s)


_BIG = ("w_in", "mla_w_uq", "mla_w_ukv", "w_out", "ffn_w_gate", "ffn_w_up", "ffn_w_down")
_COL_SHARDED = ("w_in", "mla_w_uq", "mla_w_ukv", "ffn_w_gate", "ffn_w_up")
_SMALL = ("mla_q_norm", "mla_kv_norm", "gqa_q_norm", "gqa_k_norm", "rel_bias", "ln1_g", "ln1_b", "ln2_g", "ln2_b")


def _pack_flat(arrs, align):
    flat = jnp.concatenate([a.reshape(-1) for a in arrs])
    pad = (-flat.shape[0]) % align
    return jnp.pad(flat, (0, pad)) if pad else flat


def _unpack_flat(flat, shapes):
    out, off = [], 0
    for s in shapes:
        n = int(np.prod(s))
        out.append(flat[off:off + n].reshape(s))
        off += n
    return out


def _perm_gqa_rows(w):
    return jnp.concatenate([w[:832], w[896:960], w[832:896], w[960:]], axis=0)


def _local_step(x, target, small, depth, weights_of_layer, grads_done):
    S, D = x.shape
    alpha = (2.0 * depth) ** 0.25
    in_idx, uq_idx, ukv_idx = _in_cols(), _uq_cols(), _ukv_cols()
    win, wuq, wukv, wout, wg, wu, wdn = ([None] * depth for _ in range(7))

    tm, tg = _rope_tables(S)
    j256, j384 = _head_ones(256), _head_ones(384)
    mla_scale = (64 + MLA_ROPE_DIM) ** -0.5
    branches = []
    for (_, dil) in DIL_BRANCHES:
        L = S // dil
        tq = min(256, L)
        idx = jnp.asarray(_branch_bucket_idx(tq, dil))
        branches.append((dil, L, tq, idx))
    tabs = [_bias_expand(idx, small["rel_bias"], name=f"bias_expand_{b}") for b, (_, _, _, idx) in enumerate(branches)]

    def padded(a):
        z = jnp.zeros((DIL_HALF, a.shape[1]), a.dtype)
        return jnp.concatenate([z, a, z], axis=0)[None]

    saved = []
    xf, xb = x, x.astype(CDT)
    for l in range(depth):
        W, token = weights_of_layer(l, xb)
        win[l] = _rows_from_shards(W["w_in"], in_idx)
        wuq[l] = _rows_from_shards(W["mla_w_uq"], uq_idx)
        wukv[l] = _rows_from_shards(W["mla_w_ukv"], ukv_idx)
        wout[l] = _perm_gqa_rows(W["w_out"].reshape(-1, D))
        wg[l], wu[l], wdn[l] = W["ffn_w_gate"], W["ffn_w_up"], W["ffn_w_down"]
        gq, gkv = small["mla_q_norm"][l][None], small["mla_kv_norm"][l][None]
        if token is not None:
            gq = gq + token[0, 0]
        ggq = jnp.tile(small["gqa_q_norm"][l], 4)[None]
        ggk = jnp.tile(small["gqa_k_norm"][l], 2)[None]
        h = _mm(xb, win[l], tb=True, name="mm_in")
        cq, ckv, kr, qd, kd, vd, qg, kg, vg, *strided = _prep_fwd(h, gq, gkv, ggq, ggk, tm, tg, j256)
        qkv = [(qd[None], padded(kd), padded(vd))] + [tuple(strided[3 * b:3 * b + 3]) for b in range(len(DIL_STRIDES))]
        qa = _mm(cq, wuq[l], tb=True, name="mm_uq")
        kvp = _mm(ckv, wukv[l], tb=True, out_dtype=CDT, name="mm_ukv")
        qm, km = _mla_prep_fwd(qa, kvp, kr, tm, mla_scale)
        oa, lsa = _attn_fwd(qm, km, kvp, split=True, npairs=3, kblk=lambda p: p, vblk=lambda p: 6 + p,
                            name="mla_attn_fwd")
        oc, lsc = _attn_fwd(qg, kg, vg, split=False, npairs=2, kblk=lambda p: 0, vblk=lambda p: 0,
                            name="gqa_attn_fwd")
        obs, lbs = [], []
        for b, (dil, L, tq, _) in enumerate(branches):
            o_b, l_b = _dil_fwd(*qkv[b], tabs[b], dil=dil, L=L, tq=tq, name=f"dil_fwd_{b}")
            obs.append(o_b)
            lbs.append(l_b)
        ob = _dil_mix_fwd(obs, lbs)
        cat = jnp.concatenate([oa, ob, oc], axis=1)
        mix = _mm(cat, wout[l], name="mm_out")
        x1, x1b, z1 = _ln_fwd(xf, mix, small["ln1_g"][l][None], small["ln1_b"][l][None], alpha, name="ln1_fwd")
        g3, u3, act = _ffn_up(x1b, wg[l], wu[l])
        ff = _mm(act, wdn[l], ga=True, gb=True, name="mm_down")
        x2, x2b, z2 = _ln_fwd(x1, ff, small["ln2_g"][l][None], small["ln2_b"][l][None], alpha, name="ln2_fwd")
        saved.append(dict(xb=xb, h=h, cq=cq, ckv=ckv, qg=qg, kg=kg, vg=vg, kvp=kvp, qm=qm, km=km, oa=oa, lsa=lsa,
                          oc=oc, lsc=lsc, obs=obs, lbs=lbs, qkv=qkv, cat=cat, z1=z1, x1b=x1b, g3=g3, u3=u3, act=act, z2=z2,
                          gq=gq, gkv=gkv, ggq=ggq, ggk=ggk))
        xf, xb = x2, x2b

    dy, loss = _loss_kernel(xf, target)

    gW = {k: [None] * depth for k in _BIG}
    gS = {k: [None] * depth for k in ("mla_q_norm", "mla_kv_norm", "gqa_q_norm", "gqa_k_norm", "ln1_g", "ln1_b", "ln2_g",
                                      "ln2_b")}
    g_rel = None
    dya, dyb = dy, None
    token = None
    for l in reversed(range(depth)):
        sv = saved[l]
        ln2_g = small["ln2_g"][l][None]
        if token is not None:
            ln2_g = ln2_g + token[0, 0]
        dz2, dz2b, gS["ln2_g"][l], gS["ln2_b"][l] = _ln_bwd(dya, dyb, sv["z2"], ln2_g, alpha,
                                                             name="ln2_bwd" if dyb is not None else "ln2_bwd_last")
        gW["ffn_w_down"][l] = _mm(sv["act"], dz2b, ta=True, ga=True, go=True, name="mm_down_dw")
        dg3, du3 = _ffn_down_dx(dz2b, wdn[l], sv["g3"], sv["u3"])
        gW["ffn_w_gate"][l] = _mm(dg3, sv["x1b"], ta=True, ga=True, go=True, name="mm_gate_dw")
        gW["ffn_w_up"][l] = _mm(du3, sv["x1b"], ta=True, ga=True, go=True, name="mm_up_dw")
        dx1 = _ffn_up_dx(dg3, du3, wg[l], wu[l])
        dz1, dz1b, gS["ln1_g"][l], gS["ln1_b"][l] = _ln_bwd(dx1, dz2, sv["z1"], small["ln1_g"][l][None], alpha,
                                                             name="ln1_bwd")
        gW["w_out"][l] = _perm_gqa_rows(_mm(sv["cat"], dz1b, ta=True, name="mm_out_dw")).reshape(4, -1, D)
        dcat = _mm(dz1b, wout[l], tb=True, name="mm_out_dx")
        dqg, dkg, dvg = _attn_bwd(sv["qg"], sv["kg"], sv["vg"], dcat, sv["oc"], sv["lsc"], split=False, npairs=2,
                                  kblk=lambda p: 0, vblk=lambda p: 0, doblk=lambda p: 6 + p, shared_kv=True,
                                  name="gqa_attn_bwd")
        dqm, dkm, dvm = _attn_bwd(sv["qm"], sv["km"], sv["kvp"], dcat, sv["oa"], sv["lsa"], split=True, npairs=3,
                                  kblk=lambda p: p, vblk=lambda p: 6 + p, doblk=lambda p: p, shared_kv=False,
                                  name="mla_attn_bwd")
        dqa, dkvp, dkr = _mla_prep_bwd(dqm, dkm, dvm, tm, mla_scale)
        gW["mla_w_uq"][l] = _rows_to_shards(_mm(dqa, sv["cq"], ta=True, name="mm_uq_dw"), uq_idx, MLA_HEADS * 96)
        dcq = _mm(dqa, wuq[l], name="mm_uq_dx")
        gW["mla_w_ukv"][l] = _rows_to_shards(_mm(dkvp, sv["ckv"], ta=True, name="mm_ukv_dw"), ukv_idx, MLA_HEADS * 128)
        dckv = _mm(dkvp, wukv[l], name="mm_ukv_dx")
        mixb = _dil_mix_bwd(dcat, sv["obs"], sv["lbs"], j384)
        ddq, ddk, ddv = [], [], []
        for b, (dil, L, tq, idx) in enumerate(branches):
            dq_b, dk_b, dv_b, dtab = _dil_bwd(*sv["qkv"][b], tabs[b], mixb[b], sv["lbs"][b], mixb[3 + b], dil=dil, L=L,
                                              tq=tq, name=f"dil_bwd_{b}")
            if dil == 1:
                dq_b, dk_b, dv_b = dq_b[0], dk_b[0, DIL_HALF:DIL_HALF + S], dv_b[0, DIL_HALF:DIL_HALF + S]
            ddq.append(dq_b)
            ddk.append(dk_b)
            ddv.append(dv_b)
            g_b = _bias_reduce(idx, dtab, name=f"bias_reduce_{b}")[:, :, 0].T
            g_rel = g_b if g_rel is None else g_rel + g_b
        dh, n1, n2, n3, n4 = _prep_bwd(sv["h"], dcq, dckv, dkr, ddq, ddk, ddv, dqg, dkg, dvg, sv["gq"], sv["gkv"],
                                       sv["ggq"], sv["ggk"], tg, j256)
        gS["mla_q_norm"][l], gS["mla_kv_norm"][l] = n1[0], n2[0]
        gS["gqa_q_norm"][l] = n3[0].reshape(4, 64).sum(0)
        gS["gqa_k_norm"][l] = n4[0].reshape(2, 64).sum(0)
        gW["w_in"][l] = _rows_to_shards(_mm(dh, sv["xb"], ta=True, name="mm_in_dw"), in_idx, IN_W)
        dya = _mm(dh, win[l], name="mm_in_dx")
        dyb = dz1
        token = grads_done(l, {n: gW[n][l] for n in _BIG})
    grad_x = _axpy(dya, dyb, alpha, name="grad_x")

    gsmall = {k: jnp.stack([a.reshape(-1) for a in v]) for k, v in gS.items()}
    gsmall["rel_bias"] = g_rel
    return loss, grad_x, gsmall


_ORDER = ("w_in", "mla_q_norm", "mla_kv_norm", "mla_w_uq", "mla_w_ukv", "gqa_q_norm", "gqa_k_norm", "rel_bias", "w_out",
          "ln1_g", "ln1_b", "ffn_w_gate", "ffn_w_up", "ffn_w_down", "ln2_g", "ln2_b")


def kernel(x, w_in, mla_q_norm, mla_kv_norm, mla_w_uq, mla_w_ukv, gqa_q_norm, gqa_k_norm, rel_bias, w_out, ln1_g, ln1_b, ffn_w_gate, ffn_w_up, ffn_w_down, ln2_g, ln2_b, loss_target, m_w_in, m_mla_q_norm, m_mla_kv_norm, m_mla_w_uq, m_mla_w_ukv, m_gqa_q_norm, m_gqa_k_norm, m_rel_bias, m_w_out, m_ln1_g, m_ln1_b, m_ffn_w_gate, m_ffn_w_up, m_ffn_w_down, m_ln2_g, m_ln2_b, v_w_in, v_mla_q_norm, v_mla_kv_norm, v_mla_w_uq, v_mla_w_ukv, v_gqa_q_norm, v_gqa_k_norm, v_rel_bias, v_w_out, v_ln1_g, v_ln1_b, v_ffn_w_gate, v_ffn_w_up, v_ffn_w_down, v_ln2_g, v_ln2_b):
    wts = dict(zip(_ORDER, (w_in, mla_q_norm, mla_kv_norm, mla_w_uq, mla_w_ukv, gqa_q_norm, gqa_k_norm, rel_bias, w_out,
                            ln1_g, ln1_b, ffn_w_gate, ffn_w_up, ffn_w_down, ln2_g, ln2_b)))
    mom = dict(zip(_ORDER, (m_w_in, m_mla_q_norm, m_mla_kv_norm, m_mla_w_uq, m_mla_w_ukv, m_gqa_q_norm, m_gqa_k_norm,
                            m_rel_bias, m_w_out, m_ln1_g, m_ln1_b, m_ffn_w_gate, m_ffn_w_up, m_ffn_w_down, m_ln2_g,
                            m_ln2_b)))
    var = dict(zip(_ORDER, (v_w_in, v_mla_q_norm, v_mla_kv_norm, v_mla_w_uq, v_mla_w_ukv, v_gqa_q_norm, v_gqa_k_norm,
                            v_rel_bias, v_w_out, v_ln1_g, v_ln1_b, v_ffn_w_gate, v_ffn_w_up, v_ffn_w_down, v_ln2_g,
                            v_ln2_b)))
    small_shapes = [wts[n].shape for n in _SMALL]
    for d in (wts, mom, var):
        for n in _COL_SHARDED:
            d[n] = d[n].transpose(0, 2, 1)

    depth = 2
    shards = [wts[n].astype(WIRE) for n in _BIG]
    flying = {}

    def weights_of_layer(l, after):
        token = None
        if l == 0:
            ssem, rsem, lands, _ = _gather_start(shards, 0)
            got = _gather_finish(shards, _gather_wait(ssem, rsem, shards, lands, after, 0), 0)
            flying["gather"] = _gather_start(shards, 1)
            token = flying["gather"][3]
        else:
            ssem, rsem, lands, _ = flying.pop("gather")
            got = _gather_finish(shards, _gather_wait(ssem, rsem, shards, lands, after, l), l)
        return {n: g.astype(CDT) for n, g in zip(_BIG, got)}, token

    def grads_done(l, grads):
        grads = [grads[n] for n in _BIG]
        theirs = _rs_to_owner(grads)
        pairs = [_sum_pair(g, t, own, name=f"rs_pair_sum_{n}") for n, g, t, own in zip(_BIG, grads, theirs, _OWNER)]
        ssem, rsem, lands, token = _a2a_start(pairs, l)
        flying[l] = (ssem, rsem, pairs, lands)
        return token

    small = {n: wts[n] for n in _SMALL}
    loss, grad_x, gsmall = _local_step(x[0], loss_target[0], small, depth, weights_of_layer, grads_done)

    reds = {}
    for l in reversed(range(depth)):
        ssem, rsem, pairs, lands = flying.pop(l)
        got = _a2a_wait(ssem, rsem, pairs, lands, grad_x, l)
        reds[l] = [_sum_chips(p, t, own, name=f"rs_sum_chips_{n}") for n, p, t, own in zip(_BIG, pairs, got, _OWNER)]
    sibs = _rs_from_owner(reds[0] + reds[1])

    sflat = _pack_flat([gsmall[n].reshape(-1) for n in _SMALL], 8 * LANES)
    rs = sflat.shape[0] // LANES
    sall = _gather_small(sflat.reshape(rs, LANES))

    def packed(d):
        return _pack_flat([d[n] for n in _SMALL], 8 * LANES).reshape(rs, LANES)

    outs = {tag: {} for tag in ("grad", "delta", "new_m", "new_v")}
    for i, n in enumerate(_BIG):
        res = _adamw(wts[n], [reds[0][i], reds[1][i]], [sibs[i], sibs[len(_BIG) + i]], mom[n], var[n], _OWNER[i],
                     name=f"adamw_{n}")
        for tag, r in zip(("grad", "delta", "new_m", "new_v"), res):
            outs[tag][n] = r.transpose(0, 2, 1) if n in _COL_SHARDED else r
    for tag, smallflat in zip(("grad", "delta", "new_m", "new_v"), _adamw_small(packed(wts), sall, packed(mom), packed(var))):
        outs[tag].update(zip(_SMALL, _unpack_flat(smallflat.reshape(-1), small_shapes)))

    total = lax.psum(loss[0, 0], ("x", "y", "c"))
    return (total, grad_x[None], *[outs["grad"][n] for n in _ORDER], *[outs["delta"][n] for n in _ORDER],
            *[outs["new_m"][n] for n in _ORDER], *[outs["new_v"][n] for n in _ORDER])
```

```python
import functools
import math

import numpy as np
import jax
import jax.numpy as jnp
from jax import lax
from jax.experimental import pallas as pl
from jax.experimental.pallas import tpu as pltpu

F32 = jnp.float32
CDT = jnp.bfloat16
WIRE = jnp.bfloat16

HEAD_DIM = 64
GRID_W = 64
ROPE_THETA = 10000.0
MLA_HEADS = 6
MLA_Q_RANK = 256
MLA_KV_RANK = 128
MLA_ROPE_DIM = 32
DIL_HEADS = 6
DIL_BRANCHES = ((128, 1), (512, 4), (2048, 16))
DIL_HALF = 64
GQA_Q_HEADS = 4
REL_BUCKETS = 32
REL_MAX_DIST = 1024
NEG_INF = -1e30
LANES = 128
VMEM_LIMIT = 56 * 1024 * 1024

ADAM_LR, ADAM_B1, ADAM_B2, ADAM_EPS, ADAM_WD, ADAM_STEP = 0.001, 0.9, 0.999, 1e-08, 0.01, 10

C_CQ, C_CKV, C_KR, C_DQ, C_DK, C_DV, C_GQ, C_GK, C_GV, IN_P = 0, 256, 384, 512, 896, 1280, 1664, 1920, 2048, 2176
IN_W = 2080
MESH_ID = pl.DeviceIdType.MESH


def _cparams(n_axes, vmem=VMEM_LIMIT):
    return pltpu.CompilerParams(dimension_semantics=("arbitrary",) * n_axes, vmem_limit_bytes=vmem)


MAX_WHOLE_DIM = 2304


def _pick(n, target):
    best = None
    for t in range(LANES, min(n, target) + 1, LANES):
        if n % t == 0:
            best = t
    if best is not None and (2 * best >= target or n > MAX_WHOLE_DIM):
        return best
    return n


def _sds(shape, dtype):
    return jax.ShapeDtypeStruct(tuple(shape), dtype)


def _in_cols():
    idx = -np.ones((IN_P,), np.int64)
    idx[C_CQ:C_CQ + 256] = np.arange(0, 256)
    idx[C_CKV:C_CKV + 128] = np.arange(256, 384)
    idx[C_KR + 64:C_KR + 96] = np.arange(384, 416)
    idx[C_DQ:C_DQ + 1152] = np.arange(416, 1568)
    gq = 1568 + (np.array([0, 2, 1, 3])[:, None] * 64 + np.arange(64)[None, :]).reshape(-1)
    idx[C_GQ:C_GQ + 256] = gq
    idx[C_GK:C_GK + 256] = np.arange(1824, 2080)
    return idx


def _uq_cols():
    idx = -np.ones((MLA_HEADS * 128,), np.int64)
    for h in range(MLA_HEADS):
        idx[h * 128:h * 128 + 96] = np.arange(96 * h, 96 * h + 96)
    return idx


def _ukv_cols():
    idx = -np.ones((MLA_HEADS * 128 + MLA_HEADS * 64,), np.int64)
    for h in range(MLA_HEADS):
        idx[h * 128:h * 128 + 64] = np.arange(128 * h, 128 * h + 64)
        idx[768 + h * 64:768 + h * 64 + 64] = np.arange(128 * h + 64, 128 * h + 128)
    return idx


def _out_rows():
    idx = np.arange(1024)
    idx[768:1024] = 768 + (np.array([0, 2, 1, 3])[:, None] * 64 + np.arange(64)[None, :]).reshape(-1)
    return idx


def _runs(idx):
    out, i = [], 0
    while i < len(idx):
        j = i + 1
        while j < len(idx) and ((idx[i] < 0 and idx[j] < 0) or (idx[i] >= 0 and idx[j] == idx[j - 1] + 1)):
            j += 1
        out.append((int(idx[i]), j - i))
        i = j
    return out


def _rows_from_shards(sh, idx):
    _, cs, r = sh.shape
    pieces = []
    for first, ln in _runs(idx):
        if first < 0:
            pieces.append(jnp.zeros((ln, r), sh.dtype))
            continue
        while ln > 0:
            k, off = divmod(first, cs)
            take = min(ln, cs - off)
            pieces.append(sh[k, off:off + take, :])
            first, ln = first + take, ln - take
    return jnp.concatenate(pieces, axis=0)


def _rows_to_shards(wp, idx, n):
    inv = np.zeros((n,), np.int64)
    pos = np.nonzero(idx >= 0)[0]
    inv[idx[pos]] = pos
    cs = n // 4
    shards = []
    for k in range(4):
        pieces = [wp[first:first + ln, :] for first, ln in _runs(inv[k * cs:(k + 1) * cs])]
        shards.append(jnp.concatenate(pieces, axis=0))
    return jnp.stack(shards)


def _t5_bucket_np(rel):
    nb = REL_BUCKETS // 2
    exact = nb // 2
    ret = np.where(rel > 0, nb, 0)
    n = np.abs(rel)
    nf = np.maximum(n, 1).astype(np.float32)
    large = exact + (np.log(nf / np.float32(exact)) / np.float32(math.log(REL_MAX_DIST / exact))
                     * np.float32(nb - exact)).astype(np.int32)
    large = np.minimum(large, nb - 1)
    return ret + np.where(n < exact, n, large)


def _branch_bucket_idx(tq, dil):
    kw = tq + 2 * DIL_HALF
    rel = np.arange(kw)[None, :] - DIL_HALF - np.arange(tq)[:, None]
    idx = _t5_bucket_np(rel * dil)
    return np.where(np.abs(rel) <= DIL_HALF, idx, -1).astype(np.int32)


def _rope_tables(S):
    inv = ROPE_THETA ** (-jnp.arange(0, 32, 2, dtype=F32) / 32)
    t = jnp.arange(S)
    pos = t.astype(F32)
    row = (t // GRID_W).astype(F32)
    col = (t % GRID_W).astype(F32)
    lane = np.arange(LANES)
    wm = lane - 64
    is_rope = (lane >= 64) & (lane < 96)
    ang = pos[:, None] * inv[np.where(is_rope, wm % 16, 0)][None, :]
    cm = jnp.where(is_rope[None], jnp.cos(ang), 1.0)
    smm = jnp.where((is_rope & (wm < 16))[None], -jnp.sin(ang), 0.0)
    spm = jnp.where((is_rope & (wm >= 16))[None], jnp.sin(ang), 0.0)
    g = lane % 64
    w = g % 32
    angg = jnp.where((g < 32)[None], row[:, None], col[:, None]) * inv[w % 16][None, :]
    cg = jnp.cos(angg)
    smg = jnp.where((w < 16)[None], -jnp.sin(angg), 0.0)
    spg = jnp.where((w >= 16)[None], jnp.sin(angg), 0.0)
    return (cm, smm, spm), (cg, smg, spg)


def _lanes(t, width):
    return t if width == LANES else jnp.concatenate([t] * (width // LANES), axis=1)


def _rope(x, tabs):
    c, sm, sp = (_lanes(t, x.shape[1]) for t in tabs)
    w = x.shape[1]
    return x * c + pltpu.roll(x, w - 16, 1) * sm + pltpu.roll(x, 16, 1) * sp


def _rope_t(dy, tabs):
    c, sm, sp = (_lanes(t, dy.shape[1]) for t in tabs)
    w = dy.shape[1]
    return dy * c + pltpu.roll(dy * sm, 16, 1) + pltpu.roll(dy * sp, w - 16, 1)


def _head_ones(width):
    i = np.arange(width)
    return jnp.asarray((i[:, None] // HEAD_DIM == i[None, :] // HEAD_DIM).astype(np.float32))


def _headsum(x, j):
    return jnp.dot(x, j, preferred_element_type=F32, precision=lax.Precision.HIGHEST)


def _mm(a, b, *, ta=False, tb=False, ga=False, gb=False, go=False, out_dtype=F32, add=None, name):
    G = a.shape[0] if ga else (b.shape[0] if gb else 1)
    a2 = a.shape[1:] if ga else a.shape
    b2 = b.shape[1:] if gb else b.shape
    K, M = a2 if ta else a2[::-1]
    N = b2[0] if tb else b2[1]
    assert (b2[1] if tb else b2[0]) == K
    tm, tn, tk = _pick(M, 1024), _pick(N, 1024), _pick(K, 1024)
    if tm * tn > 1024 * 1152:
        tm = _pick(M, 512)
    nk = K // tk
    steps = nk if (go or G == 1) else G * nk
    dn = (((0 if ta else 1,), (1 if tb else 0,)), ((), ()))

    def body(a_ref, b_ref, *rest):
        rest = list(rest)
        add_ref = rest.pop(0) if add is not None else None
        o_ref = rest.pop(0)
        part = lax.dot_general(a_ref[...], b_ref[...], dn, preferred_element_type=F32)
        if steps == 1:
            if add_ref is not None:
                part = part + add_ref[...]
            o_ref[...] = part.astype(o_ref.dtype)
            return
        acc_ref, = rest
        s = pl.program_id(3)

        @pl.when(s == 0)
        def _():
            acc_ref[...] = part if add_ref is None else part + add_ref[...]

        @pl.when(s > 0)
        def _():
            acc_ref[...] += part

        @pl.when(s == steps - 1)
        def _():
            o_ref[...] = acc_ref[...].astype(o_ref.dtype)

    def grp(g, s):
        return g if go else s // nk

    def kk(s):
        return s if steps == nk else s % nk

    def spec(grouped, block, index):
        if grouped:
            return pl.BlockSpec((None,) + block, lambda g, i, j, s: (grp(g, s),) + index(i, j, s))
        return pl.BlockSpec(block, lambda g, i, j, s: index(i, j, s))

    a_spec = (spec(ga, (tk, tm), lambda i, j, s: (kk(s), i)) if ta else spec(ga, (tm, tk), lambda i, j, s: (i, kk(s))))
    b_spec = (spec(gb, (tn, tk), lambda i, j, s: (j, kk(s))) if tb else spec(gb, (tk, tn), lambda i, j, s: (kk(s), j)))
    o_spec = spec(go, (tm, tn), lambda i, j, s: (i, j))
    return pl.pallas_call(
        body, name=name, grid=(G if go else 1, M // tm, N // tn, steps),
        in_specs=[a_spec, b_spec] + ([o_spec] if add is not None else []), out_specs=o_spec,
        out_shape=_sds(((G,) if go else ()) + (M, N), out_dtype),
        scratch_shapes=[pltpu.VMEM((tm, tn), F32)] if steps > 1 else [],
        compiler_params=_cparams(4),
    )(*([a, b] + ([add] if add is not None else [])))


def _row(ts, w, cb=0):
    return pl.BlockSpec((ts, w), lambda i: (i, cb))


def _full(shape):
    nd = len(shape)
    return pl.BlockSpec(tuple(shape), lambda i: (0,) * nd)


def _rms_fwd(x, g, eps=1e-6):
    r = lax.rsqrt(jnp.mean(x * x, axis=-1, keepdims=True) + eps)
    return x * r * g


def _rms_bwd(x, g, dy, eps=1e-6):
    r = lax.rsqrt(jnp.mean(x * x, axis=-1, keepdims=True) + eps)
    gdy = g * dy
    dx = r * gdy - x * (r * r * r) * jnp.mean(x * gdy, axis=-1, keepdims=True)
    return dx, x * r * dy


def _rms_head_fwd(x, g, j, eps=1e-6):
    r = lax.rsqrt(_headsum(x * x, j) * (1.0 / HEAD_DIM) + eps)
    return x * r * g


def _rms_head_bwd(x, g, dy, j, eps=1e-6):
    r = lax.rsqrt(_headsum(x * x, j) * (1.0 / HEAD_DIM) + eps)
    gdy = g * dy
    dx = r * gdy - x * (r * r * r) * (_headsum(x * gdy, j) * (1.0 / HEAD_DIM))
    return dx, x * r * dy


DIL_STRIDES = tuple(d for _, d in DIL_BRANCHES if d > 1)
DIL_W = DIL_HEADS * HEAD_DIM


def _res_spec(d, n, pad_blocks=0):
    return pl.BlockSpec((d, n, DIL_W), lambda i: (0, i + pad_blocks, 0))


def _prep_fwd(h, gq, gkv, ggq, ggk, tm, tg, j256):
    S = h.shape[0]
    ts = min(256, S)
    scale = HEAD_DIM ** -0.5
    nres = len(DIL_STRIDES)

    def body(*refs):
        (h_ref, gq_ref, gkv_ref, ggq_ref, ggk_ref, cm, smm, spm, cg, smg, spg, j_ref), refs = refs[:12], refs[12:]
        refs = refs[2 * nres:]
        (cq_o, ckv_o, kr_o, dq_o, dk_o, dv_o, gq_o, gk_o, gv_o), res_o = refs[:9], refs[9:-1]
        st = refs[-1]
        tabm = (cm[...], smm[...], spm[...])
        tabg = (cg[...], smg[...], spg[...])
        cq_o[...] = _rms_fwd(h_ref[:, C_CQ:C_CQ + 256], gq_ref[...]).astype(CDT)
        ckv_o[...] = _rms_fwd(h_ref[:, C_CKV:C_CKV + 128], gkv_ref[...]).astype(CDT)
        kr_o[...] = _rope(h_ref[:, C_KR:C_KR + 128], tabm).astype(CDT)
        dq_o[...] = (h_ref[:, C_DQ:C_DQ + 384] * scale).astype(CDT)
        dk_o[...] = h_ref[:, C_DK:C_DK + 384].astype(CDT)
        dv_o[...] = h_ref[:, C_DV:C_DV + 384].astype(CDT)
        for j, lanes in _lane_blocks(3 * DIL_W):
            st[j] = h_ref[:, C_DQ + lanes.start:C_DQ + lanes.stop] * (scale if j < 3 else 1.0)
        for bi, d in enumerate(DIL_STRIDES):
            for c in range(d):
                rows = pl.ds(c, ts // d, stride=d)
                for j, lanes in _lane_blocks(3 * DIL_W):
                    res_o[3 * bi + j // 3][c, :, (j % 3) * LANES:(j % 3 + 1) * LANES] = st.at[j][rows, :].astype(CDT)
        qn = _rms_head_fwd(h_ref[:, C_GQ:C_GQ + 256], ggq_ref[...], j_ref[...])
        gq_o[...] = (_rope(qn, tabg) * scale).astype(CDT)
        kn = _rms_head_fwd(h_ref[:, C_GK:C_GK + 128], ggk_ref[...], j_ref[0:128, 0:128])
        gk_o[...] = _rope(kn, tabg).astype(CDT)
        gv_o[...] = h_ref[:, C_GV:C_GV + 128].astype(CDT)

    widths = (256, 128, 128, 384, 384, 384, 256, 128, 128)
    out_specs = [_row(ts, w) for w in widths]
    out_shape = [_sds((S, w), CDT) for w in widths]
    zeros, aliases = [], {}
    for d in DIL_STRIDES:
        n, L = ts // d, S // d
        out_specs += [_res_spec(d, n), _res_spec(d, n, DIL_HALF // n), _res_spec(d, n, DIL_HALF // n)]
        out_shape += [_sds((d, L, DIL_W), CDT)] + [_sds((d, L + 2 * DIL_HALF, DIL_W), CDT)] * 2
        for t in range(2):
            aliases[12 + len(zeros)] = len(out_shape) - 2 + t
            zeros.append(jnp.zeros((d, L + 2 * DIL_HALF, DIL_W), CDT))
    return pl.pallas_call(
        body, name="prep_fwd", grid=(S // ts,),
        in_specs=[_row(ts, IN_P), _full(gq.shape), _full(gkv.shape), _full(ggq.shape), _full(ggk.shape)]
        + [_row(ts, LANES)] * 6 + [_full(j256.shape)] + [pl.BlockSpec(memory_space=pl.ANY)] * len(zeros),
        out_specs=out_specs, out_shape=out_shape, input_output_aliases=aliases,
        scratch_shapes=[pltpu.VMEM((3 * DIL_W // LANES, ts, LANES), F32)],
        compiler_params=_cparams(1),
    )(h, gq, gkv, ggq, ggk, *tm, *tg, j256, *zeros)


def _prep_bwd(h, dcq, dckv, dkr, ddq, ddk, ddv, dgq, dgk, dgv, gq, gkv, ggq, ggk, tg, j256):
    S = h.shape[0]
    ts = min(256, S)
    scale = HEAD_DIM ** -0.5

    def body(h_ref, dcq_r, dckv_r, dkr_r, q1, q2, q3, k1, k2, k3, v1, v2, v3, dgq_r, dgk_r, dgv_r,
             gq_ref, gkv_ref, ggq_ref, ggk_ref, cg, smg, spg, j_ref,
             dh_o, ngq_o, ngkv_o, nggq_o, nggk_o, *scr):
        tabg = (cg[...], smg[...], spg[...])
        first = pl.program_id(0) == 0
        scr, = scr
        d2, d3 = DIL_STRIDES
        dq = q1[...] + _by_token(q2, scr, d2) + _by_token(q3, scr, d3)
        dk = k1[...] + _by_token(k2, scr, d2) + _by_token(k3, scr, d3)
        dv = v1[...] + _by_token(v2, scr, d2) + _by_token(v3, scr, d3)

        def acc(o_ref, val):
            s = jnp.sum(val, axis=0, keepdims=True)

            @pl.when(first)
            def _():
                o_ref[...] = s

            @pl.when(jnp.logical_not(first))
            def _():
                o_ref[...] += s

        dx, dg = _rms_bwd(h_ref[:, C_CQ:C_CQ + 256], gq_ref[...], dcq_r[...])
        dh_o[:, C_CQ:C_CQ + 256] = dx.astype(CDT)
        acc(ngq_o, dg)
        dx, dg = _rms_bwd(h_ref[:, C_CKV:C_CKV + 128], gkv_ref[...], dckv_r[...])
        dh_o[:, C_CKV:C_CKV + 128] = dx.astype(CDT)
        acc(ngkv_o, dg)
        dh_o[:, C_KR:C_KR + 128] = dkr_r[...].astype(CDT)
        dh_o[:, C_DQ:C_DQ + 384] = (dq * scale).astype(CDT)
        dh_o[:, C_DK:C_DK + 384] = dk.astype(CDT)
        dh_o[:, C_DV:C_DV + 384] = dv.astype(CDT)
        dqn = _rope_t(dgq_r[...] * scale, tabg)
        dx, dg = _rms_head_bwd(h_ref[:, C_GQ:C_GQ + 256], ggq_ref[...], dqn, j_ref[...])
        dh_o[:, C_GQ:C_GQ + 256] = dx.astype(CDT)
        acc(nggq_o, dg)
        dkn = _rope_t(dgk_r[...], tabg)
        dx, dg = _rms_head_bwd(h_ref[:, C_GK:C_GK + 128], ggk_ref[...], dkn, j_ref[0:128, 0:128])
        dh_o[:, C_GK:C_GK + 128] = dx.astype(CDT)
        acc(nggk_o, dg)
        dh_o[:, C_GV:C_GV + 128] = dgv_r[...].astype(CDT)

    d2, d3 = DIL_STRIDES
    n2, n3 = ts // d2, ts // d3
    tok = _row(ts, DIL_W)
    return pl.pallas_call(
        body, name="prep_bwd", grid=(S // ts,),
        in_specs=[_row(ts, IN_P), _row(ts, 256), _row(ts, 128), _row(ts, 128)]
        + [tok, _res_spec(d2, n2), _res_spec(d3, n3)]
        + [tok, _res_spec(d2, n2, DIL_HALF // n2), _res_spec(d3, n3, DIL_HALF // n3)] * 2
        + [_row(ts, 256), _row(ts, 128), _row(ts, 128)]
        + [_full(gq.shape), _full(gkv.shape), _full(ggq.shape), _full(ggk.shape)] + [_row(ts, LANES)] * 3
        + [_full(j256.shape)],
        out_specs=[_row(ts, IN_P), _full((1, 256)), _full((1, 128)), _full((1, 256)), _full((1, 128))],
        out_shape=[_sds((S, IN_P), CDT), _sds((1, 256), F32), _sds((1, 128), F32), _sds((1, 256), F32),
                   _sds((1, 128), F32)],
        scratch_shapes=[_TOKEN_SCRATCH(ts)],
        compiler_params=_cparams(1),
    )(h, dcq, dckv, dkr, *ddq, *ddk, *ddv, dgq, dgk, dgv, gq, gkv, ggq, ggk, *tg, j256)


def _lane_blocks(width):
    return [(j, slice(j * LANES, (j + 1) * LANES)) for j in range(width // LANES)]


_TOKEN_SCRATCH = lambda ts: pltpu.VMEM((DIL_W // LANES, ts, LANES), F32)


def _by_token(res_ref, scr_ref, d):
    n = res_ref.shape[1]
    if d == 1:
        return res_ref[0].astype(F32)
    for c in range(d):
        for j, lanes in _lane_blocks(res_ref.shape[2]):
            scr_ref.at[j][pl.ds(c, n, stride=d), :] = res_ref[c, :, lanes].astype(F32)
    return jnp.concatenate([scr_ref[j] for j, _ in _lane_blocks(res_ref.shape[2])], axis=1)


def _by_residue(val, scr_ref, out_ref, d):
    n = out_ref.shape[1]
    if d == 1:
        out_ref[0] = val.astype(out_ref.dtype)
        return
    for j, lanes in _lane_blocks(out_ref.shape[2]):
        scr_ref[j] = val[:, lanes]
    for c in range(d):
        for j, lanes in _lane_blocks(out_ref.shape[2]):
            out_ref[c, :, lanes] = scr_ref.at[j][pl.ds(c, n, stride=d), :].astype(out_ref.dtype)


def _mla_prep_fwd(qa, kvp, kr, tm, scale):
    S = qa.shape[0]
    ts = min(256, S)

    def body(qa_ref, kv_ref, kr_ref, cm, smm, spm, q_o, k_o):
        tabm = (cm[...], smm[...], spm[...])
        q_o[...] = (_rope(qa_ref[...], tabm) * scale).astype(CDT)
        k_o[...] = kv_ref[:, 0:768] + _lanes(kr_ref[...], 768)

    return pl.pallas_call(
        body, name="mla_prep_fwd", grid=(S // ts,),
        in_specs=[_row(ts, 768), _row(ts, 1152), _row(ts, 128)] + [_row(ts, LANES)] * 3,
        out_specs=[_row(ts, 768)] * 2, out_shape=[_sds((S, 768), CDT)] * 2,
        compiler_params=_cparams(1),
    )(qa, kvp, kr, *tm)


def _mla_prep_bwd(dq, dk, dv, tm, scale):
    S = dq.shape[0]
    ts = min(256, S)

    def body(dq_ref, dk_ref, dv_ref, cm, smm, spm, dqa_o, dkv_o, dkr_o):
        tabm = (cm[...], smm[...], spm[...])
        lane = lax.broadcasted_iota(jnp.int32, (1, LANES), 1)
        dqa_o[...] = _rope_t(dq_ref[...] * scale, tabm).astype(CDT)
        dkr = jnp.zeros((ts, LANES), F32)
        for hd in range(MLA_HEADS):
            blk = dk_ref[:, hd * 128:(hd + 1) * 128]
            dkv_o[:, hd * 128:(hd + 1) * 128] = jnp.where(lane < 64, blk, 0.0).astype(CDT)
            dkr = dkr + jnp.where((lane >= 64) & (lane < 96), blk, 0.0)
        dkv_o[:, 768:1152] = dv_ref[...].astype(CDT)
        dkr_o[...] = jnp.where((lane >= 64) & (lane < 96), _rope_t(dkr, tabm), 0.0)

    return pl.pallas_call(
        body, name="mla_prep_bwd", grid=(S // ts,),
        in_specs=[_row(ts, 768), _row(ts, 768), _row(ts, 384)] + [_row(ts, LANES)] * 3,
        out_specs=[_row(ts, 768), _row(ts, 1152), _row(ts, 128)],
        out_shape=[_sds((S, 768), CDT), _sds((S, 1152), CDT), _sds((S, 128), F32)],
        compiler_params=_cparams(1),
    )(dq, dk, dv, *tm)


def _ln_fwd(xa, xb, g, b, alpha, name):
    S, D = xa.shape
    ts = min(256, S)

    def body(xa_ref, xb_ref, g_ref, b_ref, y_o, yb_o, z_o):
        z = alpha * xa_ref[...] + xb_ref[...]
        mu = jnp.mean(z, axis=-1, keepdims=True)
        zc = z - mu
        var = jnp.mean(zc * zc, axis=-1, keepdims=True)
        y = zc * lax.rsqrt(var + 1e-5) * g_ref[...] + b_ref[...]
        y_o[...] = y
        yb_o[...] = y.astype(CDT)
        z_o[...] = z

    return pl.pallas_call(
        body, name=name, grid=(S // ts,),
        in_specs=[_row(ts, D), _row(ts, D), _full(g.shape), _full(b.shape)],
        out_specs=[_row(ts, D)] * 3, out_shape=[_sds((S, D), F32), _sds((S, D), CDT), _sds((S, D), F32)],
        compiler_params=_cparams(1),
    )(xa, xb, g, b)


def _ln_bwd(dya, dyb, z, g, alpha, name):
    S, D = z.shape
    ts = min(256, S)
    two = dyb is not None

    def body(*refs):
        if two:
            dya_ref, dyb_ref, z_ref, g_ref, dz_o, dzb_o, dg_o, db_o = refs
            dy = dya_ref[...] + alpha * dyb_ref[...]
        else:
            dya_ref, z_ref, g_ref, dz_o, dzb_o, dg_o, db_o = refs
            dy = dya_ref[...]
        z = z_ref[...]
        mu = jnp.mean(z, axis=-1, keepdims=True)
        zc = z - mu
        r = lax.rsqrt(jnp.mean(zc * zc, axis=-1, keepdims=True) + 1e-5)
        xh = zc * r
        dxh = dy * g_ref[...]
        dz = r * (dxh - jnp.mean(dxh, axis=-1, keepdims=True) - xh * jnp.mean(dxh * xh, axis=-1, keepdims=True))
        dz_o[...] = dz
        dzb_o[...] = dz.astype(CDT)
        sg = jnp.sum(dy * xh, axis=0, keepdims=True)
        sb = jnp.sum(dy, axis=0, keepdims=True)
        first = pl.program_id(0) == 0

        @pl.when(first)
        def _():
            dg_o[...] = sg
            db_o[...] = sb

        @pl.when(jnp.logical_not(first))
        def _():
            dg_o[...] += sg
            db_o[...] += sb

    ins = [dya] + ([dyb] if two else []) + [z, g]
    return pl.pallas_call(
        body, name=name, grid=(S // ts,),
        in_specs=[_row(ts, D)] * (3 if two else 2) + [_full(g.shape)],
        out_specs=[_row(ts, D), _row(ts, D), _full((1, D)), _full((1, D))],
        out_shape=[_sds((S, D), F32), _sds((S, D), CDT), _sds((1, D), F32), _sds((1, D), F32)],
        compiler_params=_cparams(1),
    )(*ins)


def _grp_spec(ts, w):
    return pl.BlockSpec((None, ts, w), lambda k, i: (k, i, 0))


def _ffn_up(xb, wg3, wu3):
    S, D = xb.shape
    G, Fc, _ = wg3.shape
    tm = _pick(S, 1024)
    wspec = pl.BlockSpec((None, Fc, D), lambda k, i: (k, 0, 0))

    def body(x_ref, wg_ref, wu_ref, g_o, u_o, a_o):
        x = x_ref[...]
        g = lax.dot_general(x, wg_ref[...], _NT, preferred_element_type=F32)
        u = lax.dot_general(x, wu_ref[...], _NT, preferred_element_type=F32)
        g_o[...] = g
        u_o[...] = u
        a_o[...] = (g / (1.0 + jnp.exp(-g)) * u).astype(CDT)

    return pl.pallas_call(
        body, name="ffn_up", grid=(G, S // tm),
        in_specs=[pl.BlockSpec((tm, D), lambda k, i: (i, 0)), wspec, wspec], out_specs=[_grp_spec(tm, Fc)] * 3,
        out_shape=[_sds((G, S, Fc), F32), _sds((G, S, Fc), F32), _sds((G, S, Fc), CDT)], compiler_params=_cparams(2),
    )(xb, wg3, wu3)


def _ffn_up_dx(dg3, du3, wg3, wu3):
    G, S, Fc = dg3.shape
    D = wg3.shape[2]
    tm = _pick(S, 1024)
    wspec = pl.BlockSpec((None, Fc, D), lambda i, k: (k, 0, 0))
    aspec = pl.BlockSpec((None, tm, Fc), lambda i, k: (k, i, 0))

    def body(dg_ref, du_ref, wg_ref, wu_ref, o_ref):
        part = (jnp.dot(dg_ref[...], wg_ref[...], preferred_element_type=F32)
                + jnp.dot(du_ref[...], wu_ref[...], preferred_element_type=F32))
        k = pl.program_id(1)

        @pl.when(k == 0)
        def _():
            o_ref[...] = part

        @pl.when(k > 0)
        def _():
            o_ref[...] += part

    return pl.pallas_call(
        body, name="ffn_up_dx", grid=(S // tm, G), in_specs=[aspec, aspec, wspec, wspec],
        out_specs=pl.BlockSpec((tm, D), lambda i, k: (i, 0)), out_shape=_sds((S, D), F32), compiler_params=_cparams(2),
    )(dg3, du3, wg3, wu3)


def _ffn_down_dx(dzb, wd3, g3, u3):
    S, D = dzb.shape
    G, Fc, _ = wd3.shape
    tm = _pick(S, 1024)

    def body(dz_ref, wd_ref, g_ref, u_ref, dg_o, du_o):
        da = lax.dot_general(dz_ref[...], wd_ref[...], _NT, preferred_element_type=F32)
        g = g_ref[...]
        sg = 1.0 / (1.0 + jnp.exp(-g))
        dg_o[...] = (da * u_ref[...] * (sg * (1.0 + g * (1.0 - sg)))).astype(CDT)
        du_o[...] = (da * (g * sg)).astype(CDT)

    return pl.pallas_call(
        body, name="ffn_down_dx", grid=(G, S // tm),
        in_specs=[pl.BlockSpec((tm, D), lambda k, i: (i, 0)), pl.BlockSpec((None, Fc, D), lambda k, i: (k, 0, 0)),
                  _grp_spec(tm, Fc), _grp_spec(tm, Fc)],
        out_specs=[_grp_spec(tm, Fc)] * 2, out_shape=[_sds((G, S, Fc), CDT)] * 2, compiler_params=_cparams(2),
    )(dzb, wd3, g3, u3)


def _loss_kernel(y, target):
    S, D = y.shape
    ts = min(256, S)

    def body(y_ref, t_ref, dy_o, loss_o):
        e = y_ref[...] - t_ref[...]
        dy_o[...] = e * (1.0 / D)
        part = jnp.sum(jnp.sum(e * e, axis=1, keepdims=True), axis=0, keepdims=True) * (0.5 / D)
        first = pl.program_id(0) == 0

        @pl.when(first)
        def _():
            loss_o[...] = part

        @pl.when(jnp.logical_not(first))
        def _():
            loss_o[...] += part

    return pl.pallas_call(
        body, name="loss", grid=(S // ts,), in_specs=[_row(ts, D)] * 2,
        out_specs=[_row(ts, D), _full((1, 1))], out_shape=[_sds((S, D), F32), _sds((1, 1), F32)],
        compiler_params=_cparams(1),
    )(y, target)


def _axpy(a, b, alpha, name):
    S, D = a.shape
    ts = min(256, S)

    def body(a_ref, b_ref, o_ref):
        o_ref[...] = a_ref[...] + alpha * b_ref[...]

    return pl.pallas_call(
        body, name=name, grid=(S // ts,), in_specs=[_row(ts, D)] * 2, out_specs=_row(ts, D),
        out_shape=_sds((S, D), F32), compiler_params=_cparams(1),
    )(a, b)


def _pair_masks():
    lane = lax.broadcasted_iota(jnp.int32, (1, LANES), 1)
    first = lane < HEAD_DIM
    return first, jnp.logical_not(first)


def _head_scalar(x, m):
    return jnp.max(jnp.where(m, x, -jnp.inf), axis=-1, keepdims=True)


_NT = (((1,), (1,)), ((), ()))
_TN = (((0,), (0,)), ((), ()))


def _attn_fwd(q, k, v, *, split, npairs, kblk, vblk, name):
    S = q.shape[0]
    qw = 256 if split else LANES
    tq = min(256, S)

    def body(q_ref, k_ref, v_ref, o_ref, lse_ref):
        masks = _pair_masks()
        outs, lses = [], []
        for hd in range(2):
            if split:
                qh = q_ref[:, hd * LANES:(hd + 1) * LANES]
                kh = k_ref[:, hd * LANES:(hd + 1) * LANES]
            else:
                qh = jnp.where(masks[hd], q_ref[...], jnp.zeros_like(q_ref[...]))
                kh = k_ref[...]
            s = lax.dot_general(qh, kh, _NT, preferred_element_type=F32)
            mx = jnp.max(s, axis=-1, keepdims=True)
            p = jnp.exp(s - mx)
            l = jnp.sum(p, axis=-1, keepdims=True)
            o = jnp.dot(p.astype(CDT), v_ref[...], preferred_element_type=F32)
            outs.append(o / l)
            lses.append(jnp.broadcast_to(mx + jnp.log(l), (tq, LANES)))
        o_ref[...] = jnp.where(masks[0], outs[0], outs[1]).astype(o_ref.dtype)
        lse_ref[...] = jnp.where(masks[0], lses[0], lses[1])

    return pl.pallas_call(
        body, name=name, grid=(npairs, S // tq),
        in_specs=[pl.BlockSpec((tq, qw), lambda p, i: (i, p)),
                  pl.BlockSpec((S, qw), lambda p, i: (0, kblk(p))),
                  pl.BlockSpec((S, LANES), lambda p, i: (0, vblk(p)))],
        out_specs=[pl.BlockSpec((tq, LANES), lambda p, i: (i, p))] * 2,
        out_shape=[_sds((S, LANES * npairs), CDT), _sds((S, LANES * npairs), F32)],
        compiler_params=_cparams(2),
    )(q, k, v)


def _attn_bwd(q, k, v, do, o, lse, *, split, npairs, kblk, vblk, doblk, shared_kv, name):
    S = q.shape[0]
    qw = 256 if split else LANES
    tq = min(256, S)
    nkv = 1 if shared_kv else npairs

    def body(q_ref, k_ref, v_ref, do_ref, o_ref, lse_ref, dq_ref, dk_ref, dv_ref):
        masks = _pair_masks()
        p_id, i_id = pl.program_id(0), pl.program_id(1)
        first = (i_id == 0) & ((p_id == 0) if shared_kv else True)
        do = do_ref[...]
        o = o_ref[...].astype(F32)
        lse = lse_ref[...]
        v = v_ref[...]
        dqs, dks, dvs = [], [], []
        for hd in range(2):
            m = masks[hd]
            if split:
                qh = q_ref[:, hd * LANES:(hd + 1) * LANES]
                kh = k_ref[:, hd * LANES:(hd + 1) * LANES]
            else:
                qh = jnp.where(m, q_ref[...], jnp.zeros_like(q_ref[...]))
                kh = k_ref[...]
            doh = jnp.where(m, do, 0.0)
            s = lax.dot_general(qh, kh, _NT, preferred_element_type=F32)
            p = jnp.exp(s - _head_scalar(lse, m))
            delta = jnp.sum(doh * o, axis=-1, keepdims=True)
            dohb = doh.astype(CDT)
            dp = lax.dot_general(dohb, v, _NT, preferred_element_type=F32)
            ds = (p * (dp - delta)).astype(CDT)
            dq = jnp.dot(ds, kh, preferred_element_type=F32)
            dqs.append(dq if split else jnp.where(m, dq, 0.0))
            dks.append(lax.dot_general(ds, qh, _TN, preferred_element_type=F32))
            dvs.append(lax.dot_general(p.astype(CDT), dohb, _TN, preferred_element_type=F32))
        if split:
            dq_ref[:, 0:LANES] = dqs[0]
            dq_ref[:, LANES:2 * LANES] = dqs[1]
        else:
            dq_ref[...] = dqs[0] + dqs[1]
        dv = dvs[0] + dvs[1]

        @pl.when(first)
        def _():
            if split:
                dk_ref[:, 0:LANES] = dks[0]
                dk_ref[:, LANES:2 * LANES] = dks[1]
            else:
                dk_ref[...] = dks[0] + dks[1]
            dv_ref[...] = dv

        @pl.when(jnp.logical_not(first))
        def _():
            if split:
                dk_ref[:, 0:LANES] += dks[0]
                dk_ref[:, LANES:2 * LANES] += dks[1]
            else:
                dk_ref[...] += dks[0] + dks[1]
            dv_ref[...] += dv

    kvo = (lambda p, i: (0, 0)) if shared_kv else (lambda p, i: (0, p))
    return pl.pallas_call(
        body, name=name, grid=(npairs, S // tq),
        in_specs=[pl.BlockSpec((tq, qw), lambda p, i: (i, p)),
                  pl.BlockSpec((S, qw), lambda p, i: (0, kblk(p))),
                  pl.BlockSpec((S, LANES), lambda p, i: (0, vblk(p))),
                  pl.BlockSpec((tq, LANES), lambda p, i: (i, doblk(p))),
                  pl.BlockSpec((tq, LANES), lambda p, i: (i, p)),
                  pl.BlockSpec((tq, LANES), lambda p, i: (i, p))],
        out_specs=[pl.BlockSpec((tq, qw), lambda p, i: (i, p)),
                   pl.BlockSpec((S, qw), kvo), pl.BlockSpec((S, LANES), kvo)],
        out_shape=[_sds((S, qw * npairs), F32), _sds((S, qw * nkv), F32), _sds((S, LANES * nkv), F32)],
        compiler_params=_cparams(2),
    )(q, k, v, do, o, lse)


def _bias_expand(idx, rel_bias, name):
    tq, kw = idx.shape

    def body(idx_ref, rb_ref, o_ref):
        idx = idx_ref[...]
        for hd in range(DIL_HEADS):
            acc = jnp.full((tq, kw), NEG_INF, F32)
            for u in range(REL_BUCKETS):
                acc = jnp.where(idx == u, rb_ref[u, hd], acc)
            o_ref[hd] = acc

    return pl.pallas_call(
        body, name=name,
        in_specs=[pl.BlockSpec(memory_space=pltpu.VMEM), pl.BlockSpec(memory_space=pltpu.SMEM)],
        out_specs=pl.BlockSpec(memory_space=pltpu.VMEM),
        out_shape=_sds((DIL_HEADS, tq, kw), F32),
    )(idx, rel_bias)


def _bias_reduce(idx, dtab, name):
    tq, kw = idx.shape

    def body(idx_ref, d_ref, o_ref):
        idx = idx_ref[...]
        rowid = lax.broadcasted_iota(jnp.int32, (REL_BUCKETS, kw), 0)
        for hd in range(DIL_HEADS):
            d = d_ref[hd]
            acc = jnp.zeros((REL_BUCKETS, kw), F32)
            for u in range(REL_BUCKETS):
                r = jnp.sum(jnp.where(idx == u, d, 0.0), axis=0, keepdims=True)
                acc = jnp.where(rowid == u, r, acc)
            o_ref[hd] = jnp.sum(acc, axis=1, keepdims=True)

    return pl.pallas_call(
        body, name=name,
        in_specs=[pl.BlockSpec(memory_space=pltpu.VMEM)] * 2, out_specs=pl.BlockSpec(memory_space=pltpu.VMEM),
        out_shape=_sds((DIL_HEADS, REL_BUCKETS, 1), F32),
    )(idx, dtab)


def _dil_window(i, tq, kw, L):
    start = pl.multiple_of(i * tq, DIL_HALF)
    key = start + lax.broadcasted_iota(jnp.int32, (1, kw), 1) - DIL_HALF
    return start, (key >= 0) & (key < L)


def _dil_fwd(qv, kv, vv, tab, *, dil, L, tq, name):
    kw = tq + 2 * DIL_HALF
    npair = DIL_HEADS // 2

    def body(q_ref, k_ref, v_ref, t_ref, o_ref, lse_ref):
        masks = _pair_masks()
        start, valid = _dil_window(pl.program_id(2), tq, kw, L)
        kwin = k_ref[pl.ds(start, kw), :]
        vwin = v_ref[pl.ds(start, kw), :]
        outs, lses = [], []
        for hd in range(2):
            qh = jnp.where(masks[hd], q_ref[...], jnp.zeros_like(q_ref[...]))
            s = lax.dot_general(qh, kwin, _NT, preferred_element_type=F32) + t_ref[hd]
            s = jnp.where(valid, s, NEG_INF)
            mx = jnp.max(s, axis=-1, keepdims=True)
            p = jnp.exp(s - mx)
            l = jnp.sum(p, axis=-1, keepdims=True)
            outs.append(jnp.dot(p.astype(CDT), vwin, preferred_element_type=F32) / l)
            lses.append(jnp.broadcast_to(mx + jnp.log(l), (tq, LANES)))
        o_ref[...] = jnp.where(masks[0], outs[0], outs[1])
        lse_ref[...] = jnp.where(masks[0], lses[0], lses[1])

    blk = pl.BlockSpec((None, tq, LANES), lambda p, c, i: (c, i, p))
    res = pl.BlockSpec((None, L + 2 * DIL_HALF, LANES), lambda p, c, i: (c, 0, p))
    return pl.pallas_call(
        body, name=name, grid=(npair, dil, L // tq),
        in_specs=[blk, res, res, pl.BlockSpec((2, tq, kw), lambda p, c, i: (p, 0, 0))],
        out_specs=[blk] * 2, out_shape=[_sds(qv.shape, F32)] * 2,
        compiler_params=_cparams(3),
    )(qv, kv, vv, tab)


def _dil_bwd(qv, kv, vv, tab, dov, lsev, deltav, *, dil, L, tq, name):
    kw = tq + 2 * DIL_HALF
    npair = DIL_HEADS // 2

    def body(q_ref, k_ref, v_ref, t_ref, do_ref, lse_ref, dl_ref, dq_ref, dk_ref, dv_ref, dt_ref):
        masks = _pair_masks()
        c_id, i_id = pl.program_id(1), pl.program_id(2)
        start, valid = _dil_window(i_id, tq, kw, L)
        kwin = k_ref[pl.ds(start, kw), :]
        vwin = v_ref[pl.ds(start, kw), :]

        @pl.when(i_id == 0)
        def _():
            dk_ref[...] = jnp.zeros_like(dk_ref)
            dv_ref[...] = jnp.zeros_like(dv_ref)

        @pl.when((i_id == 0) & (c_id == 0))
        def _():
            dt_ref[...] = jnp.zeros_like(dt_ref)

        do = do_ref[...]
        dq = jnp.zeros((tq, LANES), F32)
        dk = jnp.zeros((kw, LANES), F32)
        dv = jnp.zeros((kw, LANES), F32)
        for hd in range(2):
            m = masks[hd]
            qh = jnp.where(m, q_ref[...], jnp.zeros_like(q_ref[...]))
            doh = jnp.where(m, do, jnp.zeros_like(do))
            s = lax.dot_general(qh, kwin, _NT, preferred_element_type=F32) + t_ref[hd]
            s = jnp.where(valid, s, NEG_INF)
            p = jnp.exp(s - _head_scalar(lse_ref[...], m))
            dp = lax.dot_general(doh, vwin, _NT, preferred_element_type=F32)
            ds = p * (dp - _head_scalar(dl_ref[...], m))
            dt_ref[hd] += ds
            dsb = ds.astype(CDT)
            dq = dq + jnp.where(m, jnp.dot(dsb, kwin, preferred_element_type=F32), 0.0)
            dk = dk + lax.dot_general(dsb, qh, _TN, preferred_element_type=F32)
            dv = dv + lax.dot_general(p.astype(CDT), doh, _TN, preferred_element_type=F32)
        dq_ref[...] = dq
        dk_ref[pl.ds(start, kw), :] += dk
        dv_ref[pl.ds(start, kw), :] += dv

    blk = pl.BlockSpec((None, tq, LANES), lambda p, c, i: (c, i, p))
    res = pl.BlockSpec((None, L + 2 * DIL_HALF, LANES), lambda p, c, i: (c, 0, p))
    tsp = pl.BlockSpec((2, tq, kw), lambda p, c, i: (p, 0, 0))
    return pl.pallas_call(
        body, name=name, grid=(npair, dil, L // tq),
        in_specs=[blk, res, res, tsp, blk, blk, blk], out_specs=[blk, res, res, tsp],
        out_shape=[_sds(qv.shape, F32), _sds(kv.shape, F32), _sds(kv.shape, F32), _sds(tab.shape, F32)],
        compiler_params=_cparams(3),
    )(qv, kv, vv, tab, dov, lsev, deltav)


def _mix_weights(l1, l2, l3):
    mx = jnp.maximum(jnp.maximum(l1, l2), l3)
    e1, e2, e3 = jnp.exp(l1 - mx), jnp.exp(l2 - mx), jnp.exp(l3 - mx)
    inv = 1.0 / (e1 + e2 + e3)
    return e1 * inv, e2 * inv, e3 * inv


def _branch_specs(S, ts):
    dils = [d for _, d in DIL_BRANCHES]
    return dils, [_res_spec(d, ts // d) for d in dils], [(d, S // d, DIL_W) for d in dils]


def _dil_mix_fwd(os, ls):
    S = os[0].shape[0] * os[0].shape[1]
    ts = min(256, S)
    dils, specs, _ = _branch_specs(S, ts)

    def body(o1, o2, o3, l1, l2, l3, out, scr):
        o1, o2, o3, l1, l2, l3 = [_by_token(r, scr, d) for r, d in zip((o1, o2, o3, l1, l2, l3), dils + dils)]
        w1, w2, w3 = _mix_weights(l1, l2, l3)
        out[...] = (w1 * o1 + w2 * o2 + w3 * o3).astype(CDT)

    return pl.pallas_call(
        body, name="dil_mix_fwd", grid=(S // ts,), in_specs=specs + specs, out_specs=_row(ts, DIL_W),
        out_shape=_sds((S, DIL_W), CDT), scratch_shapes=[_TOKEN_SCRATCH(ts)],
        compiler_params=_cparams(1),
    )(*os, *ls)


def _dil_mix_bwd(dcat, os, ls, j384):
    S = os[0].shape[0] * os[0].shape[1]
    ts = min(256, S)
    dils, specs, shapes = _branch_specs(S, ts)

    def body(do_ref, o1, o2, o3, l1, l2, l3, j_ref, d1, d2, d3, e1, e2, e3, scr):
        o1, o2, o3, l1, l2, l3 = [_by_token(r, scr, d) for r, d in zip((o1, o2, o3, l1, l2, l3), dils + dils)]
        ws = _mix_weights(l1, l2, l3)
        do = do_ref[...]
        o = ws[0] * o1 + ws[1] * o2 + ws[2] * o3
        dot = _headsum(do * o, j_ref[...])
        for w, d, d_o, e_o in zip(ws, dils, (d1, d2, d3), (e1, e2, e3)):
            _by_residue(w * do, scr, d_o, d)
            _by_residue(w * dot, scr, e_o, d)

    return pl.pallas_call(
        body, name="dil_mix_bwd", grid=(S // ts,),
        in_specs=[_row(ts, DIL_W, 1)] + specs + specs + [_full(j384.shape)],
        out_specs=specs + specs,
        out_shape=[_sds(s, CDT) for s in shapes] + [_sds(s, F32) for s in shapes],
        scratch_shapes=[_TOKEN_SCRATCH(ts)],
        compiler_params=_cparams(1),
    )(dcat, *os, *ls, j384)


def _adamw_math(w, g, m, v):
    m = ADAM_B1 * m + (1.0 - ADAM_B1) * g
    v = ADAM_B2 * v + (1.0 - ADAM_B2) * (g * g)
    m_hat = m / (1.0 - ADAM_B1 ** ADAM_STEP)
    v_hat = v / (1.0 - ADAM_B2 ** ADAM_STEP)
    delta = -ADAM_LR * (m_hat / (jnp.sqrt(v_hat) + ADAM_EPS) + ADAM_WD * w)
    return delta, m, v


def _pick8(n, target):
    best = None
    for t in range(16, min(n, target) + 1, 16):
        if n % t == 0:
            best = t
    return best if best is not None else n


_ELEMS_PER_BLOCK = 256 * 1024


def _lead_spec(a, b):
    ta = _pick8(a, max(16, _ELEMS_PER_BLOCK // b))
    return ta, pl.BlockSpec((None, ta, b), lambda l, i: (l, i, 0))


def _adamw(w, reds, sibs, m, v, owner, name):
    L, a, b = w.shape
    ta, spec = _lead_spec(a, b)
    gspec = pl.BlockSpec((ta, b), lambda l, i: (i, 0))

    def body(w_ref, r0_ref, r1_ref, s0_ref, s1_ref, m_ref, v_ref, g_o, d_o, m_o, v_o):
        mine = lax.axis_index("c") == owner
        g0 = jnp.where(mine, r0_ref[...], s0_ref[...])
        g1 = jnp.where(mine, r1_ref[...], s1_ref[...])
        g = jnp.where(pl.program_id(0) == 0, g0, g1)
        d, mm, vv = _adamw_math(w_ref[...], g, m_ref[...], v_ref[...])
        g_o[...] = g
        d_o[...] = d
        m_o[...] = mm
        v_o[...] = vv

    return pl.pallas_call(
        body, name=name, grid=(L, a // ta), in_specs=[spec] + [gspec] * 4 + [spec, spec], out_specs=[spec] * 4,
        out_shape=[_sds(w.shape, F32)] * 4, compiler_params=_cparams(2),
    )(w, *reds, *sibs, m, v)


def _adamw_small(w, gall, m, v):
    R = w.shape[0]

    def body(w_ref, g_ref, m_ref, v_ref, g_o, d_o, m_o, v_o):
        g = g_ref[0]
        for k in range(1, 8):
            g = g + g_ref[k]
        d, mm, vv = _adamw_math(w_ref[...], g, m_ref[...], v_ref[...])
        g_o[...] = g
        d_o[...] = d
        m_o[...] = mm
        v_o[...] = vv

    vm = pl.BlockSpec(memory_space=pltpu.VMEM)
    return pl.pallas_call(
        body, name="adamw_small", in_specs=[vm] * 4, out_specs=[vm] * 4, out_shape=[_sds((R, LANES), F32)] * 4,
    )(w, gall, m, v)


def _sum_pair(g, t, owner, name):
    n, a, b = t.shape
    ta, spec = _lead_spec(a, b)

    def body(g_ref, t_ref, o_ref):
        @pl.when(lax.axis_index("c") == owner)
        def _():
            o_ref[...] = (g_ref[...] + t_ref[...]).astype(o_ref.dtype)

    return pl.pallas_call(
        body, name=name, grid=(n, a // ta), in_specs=[spec] * 2, out_specs=spec, out_shape=_sds(t.shape, WIRE),
        compiler_params=_cparams(2),
    )(g, t)


def _sum_chips(pair, t, owner, name):
    _, a, b = t.shape
    ta = _pick8(a, max(16, _ELEMS_PER_BLOCK // b))

    def body(p_ref, t_ref, o_ref):
        @pl.when(lax.axis_index("c") == owner)
        def _():
            me = 2 * lax.axis_index("x") + lax.axis_index("y")
            acc = p_ref[me].astype(F32)
            for k in range(3):
                acc = acc + t_ref[k].astype(F32)
            o_ref[...] = acc

    return pl.pallas_call(
        body, name=name, grid=(a // ta,),
        in_specs=[pl.BlockSpec((4, ta, b), lambda i: (0, i, 0)), pl.BlockSpec((3, ta, b), lambda i: (0, i, 0))],
        out_specs=pl.BlockSpec((ta, b), lambda i: (i, 0)), out_shape=_sds((a, b), F32), compiler_params=_cparams(1),
    )(pair, t)


_HBM = pl.BlockSpec(memory_space=pltpu.HBM)


def _place():
    x, y, c = lax.axis_index("x"), lax.axis_index("y"), lax.axis_index("c")
    chips = [(1 - x, y), (x, 1 - y), (1 - x, 1 - y)]
    return x, y, c, chips


def _remote(src, dst, ssem, rsem, to):
    return pltpu.make_async_remote_copy(src_ref=src, dst_ref=dst, send_sem=ssem, recv_sem=rsem, device_id=to,
                                        device_id_type=MESH_ID)


def _dma_sems(n):
    return pltpu.SemaphoreType.DMA((n,))


_SEM = pl.BlockSpec(memory_space=pltpu.SEMAPHORE)
_ANY = pl.BlockSpec(memory_space=pl.ANY)
_EFFECT = pltpu.SideEffectType.DATAFLOW_SIDE_EFFECTING
_BIG = ("w_in", "mla_w_uq", "mla_w_ukv", "w_out", "ffn_w_gate", "ffn_w_up", "ffn_w_down")
_OWNER = dict(zip(_BIG, (1, 0, 0, 1, 0, 0, 1)))
_ATTN_WEIGHTS, _FFN_WEIGHTS = _BIG[:4], _BIG[4:]


def _hbm(a):
    return pltpu.with_memory_space_constraint(a, pltpu.HBM)


def _per_core(c, owners, fn):
    for g in range(2):
        mine = tuple(p for p, o in enumerate(owners) if o == g)
        theirs = tuple(p for p, o in enumerate(owners) if o != g)
        pl.when(c == g)(functools.partial(fn, mine, theirs))


def _token_spec():
    return pl.BlockSpec(memory_space=pltpu.VMEM), _sds((8, LANES), F32)


def _gather_start(shards, owners, layer, tag):
    n = len(shards)
    lands = [_hbm(lax.empty((4,) + s.shape[1:], s.dtype)) for s in shards]

    def body(*refs):
        w_refs, l_refs = refs[:n], refs[n:2 * n]
        ssem, rsem, token = refs[2 * n], refs[2 * n + 1], refs[-1]
        x, y, c, chips = _place()
        me = 2 * x + y

        def send(mine, _):
            for i in mine:
                for j, (cx, cy) in enumerate(chips):
                    _remote(w_refs[i].at[layer], l_refs[i].at[me], ssem.at[3 * i + j], rsem.at[3 * i + j],
                            (cx, cy, c)).start()

        _per_core(c, owners, send)
        token[...] = jnp.zeros_like(token)

    tspec, tshape = _token_spec()
    out = pl.pallas_call(
        body, name=f"gather_start_{tag}", in_specs=[_HBM] * (2 * n),
        out_specs=[_SEM, _SEM] + [_HBM] * n + [tspec],
        out_shape=[_dma_sems(3 * n), _dma_sems(3 * n)] + [pltpu.HBM(l.shape, l.dtype) for l in lands] + [tshape],
        input_output_aliases={n + i: 2 + i for i in range(n)},
        compiler_params=pltpu.CompilerParams(has_side_effects=_EFFECT),
    )(*[_hbm(s) for s in shards], *lands)
    return out[0], out[1], list(out[2:2 + n]), out[-1]


def _gather_wait(ssem, rsem, shards, lands, after, owners, layer, tag):
    n = len(shards)

    def body(*refs):
        w_refs, l_refs = refs[:n], refs[n:2 * n]
        ssem, rsem = refs[2 * n], refs[2 * n + 1]
        x, y, c, chips = _place()

        def wait(mine, _):
            for i in mine:
                for j, (cx, cy) in enumerate(chips):
                    cp = _remote(w_refs[i].at[layer], l_refs[i].at[2 * cx + cy], ssem.at[3 * i + j], rsem.at[3 * i + j],
                                 (cx, cy, c))
                    cp.wait_send()
                    cp.wait_recv()

        _per_core(c, owners, wait)

    return list(pl.pallas_call(
        body, name=f"gather_wait_{tag}", in_specs=[_HBM] * (2 * n) + [_SEM, _SEM, _ANY], out_specs=[_HBM] * n,
        out_shape=[pltpu.HBM(l.shape, l.dtype) for l in lands],
        input_output_aliases={n + i: i for i in range(n)},
        compiler_params=pltpu.CompilerParams(has_side_effects=_EFFECT),
    )(*[_hbm(s) for s in shards], *lands, ssem, rsem, after))


def _gather_finish(shards, lands, owners, layer, tag):
    n = len(shards)

    def body(*refs):
        w_refs, g_refs = refs[:n], refs[2 * n:3 * n]
        ssem, rsem = refs[3 * n:]
        x, y, c, chips = _place()
        me = 2 * x + y
        sib = (x, y, 1 - c)
        owns = [_remote(w.at[layer], g.at[me], ssem.at[i], rsem.at[i], sib) for i, (w, g) in enumerate(zip(w_refs, g_refs))]
        for cp in owns:
            cp.start()

        def forward(mine, theirs):
            def blk(i, j):
                b = g_refs[i].at[2 * chips[j][0] + chips[j][1]]
                return _remote(b, b, ssem.at[n + 3 * i + j], rsem.at[n + 3 * i + j], sib)

            for i in mine:
                for j in range(3):
                    blk(i, j).start()
            for i in theirs:
                for j in range(3):
                    blk(i, j).wait_recv()
            for i in mine:
                for j in range(3):
                    blk(i, j).wait_send()

        _per_core(c, owners, forward)
        for cp in owns:
            cp.wait_recv()
            cp.wait_send()

    return list(pl.pallas_call(
        body, name=f"gather_finish_{tag}", in_specs=[_HBM] * (2 * n), out_specs=[_HBM] * n,
        out_shape=[_sds(l.shape, l.dtype) for l in lands], input_output_aliases={n + i: i for i in range(n)},
        scratch_shapes=[_dma_sems(4 * n), _dma_sems(4 * n)],
    )(*shards, *lands))


def _rs_to_owner(grads, owners, tag):
    n = len(grads)

    def body(*refs):
        g_refs, t_refs = refs[:n], refs[n:2 * n]
        ssem, rsem = refs[2 * n:]
        x, y, c, _ = _place()

        def swap(mine, theirs):
            cps = [_remote(g_refs[i], t_refs[i], ssem.at[i], rsem.at[i], (x, y, 1 - c)) for i in theirs]
            for cp in cps:
                cp.start()
            for i in mine:
                _remote(g_refs[i], t_refs[i], ssem.at[i], rsem.at[i], (x, y, 1 - c)).wait_recv()
            for cp in cps:
                cp.wait_send()

        _per_core(c, owners, swap)

    return list(pl.pallas_call(
        body, name=f"rs_to_owner_{tag}", in_specs=[_HBM] * n, out_specs=[_HBM] * n,
        out_shape=[_sds(g.shape, g.dtype) for g in grads], scratch_shapes=[_dma_sems(n), _dma_sems(n)],
    )(*grads))


def _a2a_start(pairs, owners, tag):
    n = len(pairs)
    lands = [_hbm(lax.empty((3,) + p.shape[1:], p.dtype)) for p in pairs]

    def body(*refs):
        a_refs, t_refs = refs[:n], refs[n:2 * n]
        ssem, rsem, token = refs[2 * n], refs[2 * n + 1], refs[-1]
        x, y, c, chips = _place()

        def send(mine, _):
            for i in mine:
                for j, (cx, cy) in enumerate(chips):
                    _remote(a_refs[i].at[2 * cx + cy], t_refs[i].at[j], ssem.at[3 * i + j], rsem.at[3 * i + j],
                            (cx, cy, c)).start()

        _per_core(c, owners, send)
        token[...] = jnp.zeros_like(token)

    tspec, tshape = _token_spec()
    out = pl.pallas_call(
        body, name=f"rs_a2a_start_{tag}", in_specs=[_HBM] * (2 * n),
        out_specs=[_SEM, _SEM] + [_HBM] * n + [tspec],
        out_shape=[_dma_sems(3 * n), _dma_sems(3 * n)] + [pltpu.HBM(l.shape, l.dtype) for l in lands] + [tshape],
        input_output_aliases={n + i: 2 + i for i in range(n)},
        compiler_params=pltpu.CompilerParams(has_side_effects=_EFFECT),
    )(*[_hbm(p) for p in pairs], *lands)
    return out[0], out[1], list(out[2:2 + n]), out[-1]


def _a2a_wait(ssem, rsem, pairs, lands, after, owners, tag):
    n = len(pairs)

    def body(*refs):
        a_refs, t_refs = refs[:n], refs[n:2 * n]
        ssem, rsem = refs[2 * n], refs[2 * n + 1]
        x, y, c, chips = _place()

        def wait(mine, _):
            for i in mine:
                for j, (cx, cy) in enumerate(chips):
                    cp = _remote(a_refs[i].at[2 * cx + cy], t_refs[i].at[j], ssem.at[3 * i + j], rsem.at[3 * i + j],
                                 (cx, cy, c))
                    cp.wait_send()
                    cp.wait_recv()

        _per_core(c, owners, wait)

    return list(pl.pallas_call(
        body, name=f"rs_a2a_wait_{tag}", in_specs=[_HBM] * (2 * n) + [_SEM, _SEM, _ANY], out_specs=[_HBM] * n,
        out_shape=[pltpu.HBM(l.shape, l.dtype) for l in lands],
        input_output_aliases={n + i: i for i in range(n)},
        compiler_params=pltpu.CompilerParams(has_side_effects=_EFFECT),
    )(*[_hbm(p) for p in pairs], *lands, ssem, rsem, after))


def _rs_from_owner(reds, owners):
    n = len(reds)

    def body(*refs):
        q_refs, o_refs = refs[:n], refs[n:2 * n]
        ssem, rsem = refs[2 * n:]
        x, y, c, _ = _place()

        def swap(mine, theirs):
            cps = [_remote(q_refs[k], o_refs[k], ssem.at[k], rsem.at[k], (x, y, 1 - c)) for k in mine]
            for cp in cps:
                cp.start()
            for k in theirs:
                _remote(q_refs[k], o_refs[k], ssem.at[k], rsem.at[k], (x, y, 1 - c)).wait_recv()
            for cp in cps:
                cp.wait_send()

        _per_core(c, owners, swap)

    return list(pl.pallas_call(
        body, name="rs_from_owner", in_specs=[_HBM] * n, out_specs=[_HBM] * n,
        out_shape=[_sds(q.shape, q.dtype) for q in reds], scratch_shapes=[_dma_sems(n), _dma_sems(n)],
    )(*reds))


def _gather_weights(shards):
    n = len(shards)

    def body(*refs):
        w_refs, g_refs = refs[:n], refs[n:2 * n]
        ssem, rsem = refs[2 * n:]
        x, y, c, chips = _place()
        me = 2 * x + y
        sib = (x, y, 1 - c)
        owns = [_remote(w.at[l], g.at[l, me], ssem.at[6 * n + 2 * i + l], rsem.at[6 * n + 2 * i + l], sib)
                for i, (w, g) in enumerate(zip(w_refs, g_refs)) for l in range(2)]
        first = [_remote(w.at[c], g.at[c, me], ssem.at[3 * i + j], rsem.at[3 * i + j], (cx, cy, c))
                 for j, (cx, cy) in enumerate(chips) for i, (w, g) in enumerate(zip(w_refs, g_refs))]
        for cp in first + owns:
            cp.start()
        passed = []
        for j, (cx, cy) in enumerate(chips):
            for i, g in enumerate(g_refs):
                blk = g.at[c, 2 * cx + cy]
                _remote(blk, blk, ssem.at[3 * i + j], rsem.at[3 * i + j], sib).wait_recv()
                fw = _remote(blk, blk, ssem.at[3 * n + 3 * i + j], rsem.at[3 * n + 3 * i + j], sib)
                fw.start()
                passed.append(fw)
        for j, (cx, cy) in enumerate(chips):
            for i, g in enumerate(g_refs):
                blk = g.at[1 - c, 2 * cx + cy]
                _remote(blk, blk, ssem.at[3 * n + 3 * i + j], rsem.at[3 * n + 3 * i + j], sib).wait_recv()
        for cp in owns:
            cp.wait_recv()
        for cp in first + passed + owns:
            cp.wait_send()

    return pl.pallas_call(
        body, name="gather_weights", in_specs=[_HBM] * n, out_specs=[_HBM] * n,
        out_shape=[_sds((2, 4) + s.shape[1:], s.dtype) for s in shards],
        scratch_shapes=[_dma_sems(8 * n), _dma_sems(8 * n)],
    )(*shards)


def _sibling_halves(g0s, g1s):
    n = len(g0s)

    def body(*refs):
        g0_refs, g1_refs, t_refs = refs[:n], refs[n:2 * n], refs[2 * n:3 * n]
        ssem, rsem = refs[3 * n:]
        x, y, c, _ = _place()

        def swap(srcs):
            cps = [_remote(s, t, ssem.at[i], rsem.at[i], (x, y, 1 - c)) for i, (s, t) in enumerate(zip(srcs, t_refs))]
            for cp in cps:
                cp.start()
            for cp in cps:
                cp.wait()

        @pl.when(c == 0)
        def _():
            swap(g1_refs)

        @pl.when(c == 1)
        def _():
            swap(g0_refs)

    return pl.pallas_call(
        body, name="rs_sibling_halves", in_specs=[_HBM] * (2 * n), out_specs=[_HBM] * n,
        out_shape=[_sds(g.shape, g.dtype) for g in g0s], scratch_shapes=[_dma_sems(n), _dma_sems(n)],
    )(*g0s, *g1s)


def _chip_all_to_all(parts):
    n = len(parts)

    def body(*refs):
        a_refs, t_refs = refs[:n], refs[n:2 * n]
        ssem, rsem = refs[2 * n:]
        x, y, c, chips = _place()
        sends = [_remote(a.at[2 * cx + cy], t.at[j], ssem.at[3 * i + j], rsem.at[3 * i + j], (cx, cy, c))
                 for j, (cx, cy) in enumerate(chips) for i, (a, t) in enumerate(zip(a_refs, t_refs))]
        for cp in sends:
            cp.start()
        for cp in sends:
            cp.wait_recv()
        for cp in sends:
            cp.wait_send()

    return pl.pallas_call(
        body, name="rs_chip_all_to_all", in_specs=[_HBM] * n, out_specs=[_HBM] * n,
        out_shape=[_sds((3,) + p.shape[1:], p.dtype) for p in parts],
        scratch_shapes=[_dma_sems(3 * n), _dma_sems(3 * n)],
    )(*parts)


def _sibling_swap(reds):
    n = len(reds)

    def body(*refs):
        q_refs, o_refs = refs[:n], refs[n:2 * n]
        ssem, rsem = refs[2 * n:]
        x, y, c, _ = _place()
        cps = [_remote(q, o, ssem.at[i], rsem.at[i], (x, y, 1 - c)) for i, (q, o) in enumerate(zip(q_refs, o_refs))]
        for cp in cps:
            cp.start()
        for cp in cps:
            cp.wait()

    return pl.pallas_call(
        body, name="rs_sibling_swap", in_specs=[_HBM] * n, out_specs=[_HBM] * n,
        out_shape=[_sds(q.shape, q.dtype) for q in reds],
        scratch_shapes=[_dma_sems(n), _dma_sems(n)],
    )(*reds)


def _gather_small(s):
    R, _ = s.shape

    def body(s_ref, o_ref, ssem, rsem, lsem):
        x, y, c, _ = _place()
        me = 4 * x + 2 * y + c
        own = pltpu.make_async_copy(s_ref, o_ref.at[me], lsem)
        own.start()
        sends = []
        for k in range(1, 8):
            px, py, pc = x ^ (k >> 2), y ^ ((k >> 1) & 1), c ^ (k & 1)
            cp = _remote(s_ref, o_ref.at[me], ssem.at[k - 1], rsem.at[k - 1], (px, py, pc))
            cp.start()
            sends.append(cp)
        for k in range(1, 8):
            px, py, pc = x ^ (k >> 2), y ^ ((k >> 1) & 1), c ^ (k & 1)
            blk = o_ref.at[4 * px + 2 * py + pc]
            _remote(blk, blk, ssem.at[k - 1], rsem.at[k - 1], (px, py, pc)).wait_recv()
        for cp in sends:
            cp.wait_send()
        own.wait()

    vm = pl.BlockSpec(memory_space=pltpu.VMEM)
    return pl.pallas_call(
        body, name="gather_small", in_specs=[vm], out_specs=vm, out_shape=_sds((8, R, LANES), s.dtype),
        scratch_shapes=[pltpu.SemaphoreType.DMA((7,)), pltpu.SemaphoreType.DMA((7,)), pltpu.SemaphoreType.DMA],
    )(s)


_COL_SHARDED =("w_in", "mla_w_uq", "mla_w_ukv", "ffn_w_gate", "ffn_w_up")
_SMALL = ("mla_q_norm", "mla_kv_norm", "gqa_q_norm", "gqa_k_norm", "rel_bias", "ln1_g", "ln1_b", "ln2_g", "ln2_b")


def _pack_flat(arrs, align):
    flat = jnp.concatenate([a.reshape(-1) for a in arrs])
    pad = (-flat.shape[0]) % align
    return jnp.pad(flat, (0, pad)) if pad else flat


def _unpack_flat(flat, shapes):
    out, off = [], 0
    for s in shapes:
        n = int(np.prod(s))
        out.append(flat[off:off + n].reshape(s))
        off += n
    return out


def _perm_gqa_rows(w):
    return jnp.concatenate([w[:832], w[896:960], w[832:896], w[960:]], axis=0)


def _local_step(x, target, small, depth, weights_of_layer, grads_done):
    S, D = x.shape
    alpha = (2.0 * depth) ** 0.25
    in_idx, uq_idx, ukv_idx = _in_cols(), _uq_cols(), _ukv_cols()
    win, wuq, wukv, wout, wg, wu, wdn = ([None] * depth for _ in range(7))

    tm, tg = _rope_tables(S)
    j256, j384 = _head_ones(256), _head_ones(384)
    mla_scale = (64 + MLA_ROPE_DIM) ** -0.5
    branches = []
    for (_, dil) in DIL_BRANCHES:
        L = S // dil
        tq = min(256, L)
        idx = jnp.asarray(_branch_bucket_idx(tq, dil))
        branches.append((dil, L, tq, idx))
    tabs = [_bias_expand(idx, small["rel_bias"], name=f"bias_expand_{b}") for b, (_, _, _, idx) in enumerate(branches)]

    def padded(a):
        z = jnp.zeros((DIL_HALF, a.shape[1]), a.dtype)
        return jnp.concatenate([z, a, z], axis=0)[None]

    saved = []
    xf, xb = x, x.astype(CDT)
    for l in range(depth):
        W, token = weights_of_layer(l, "attn", xb)
        win[l] = _rows_from_shards(W["w_in"], in_idx)
        wuq[l] = _rows_from_shards(W["mla_w_uq"], uq_idx)
        wukv[l] = _rows_from_shards(W["mla_w_ukv"], ukv_idx)
        wout[l] = _perm_gqa_rows(W["w_out"].reshape(-1, D))
        gq, gkv = small["mla_q_norm"][l][None], small["mla_kv_norm"][l][None]
        if token is not None:
            gq = gq + token[0, 0]
        ggq = jnp.tile(small["gqa_q_norm"][l], 4)[None]
        ggk = jnp.tile(small["gqa_k_norm"][l], 2)[None]
        h = _mm(xb, win[l], tb=True, name="mm_in")
        cq, ckv, kr, qd, kd, vd, qg, kg, vg, *strided = _prep_fwd(h, gq, gkv, ggq, ggk, tm, tg, j256)
        qkv = [(qd[None], padded(kd), padded(vd))] + [tuple(strided[3 * b:3 * b + 3]) for b in range(len(DIL_STRIDES))]
        qa = _mm(cq, wuq[l], tb=True, name="mm_uq")
        kvp = _mm(ckv, wukv[l], tb=True, out_dtype=CDT, name="mm_ukv")
        qm, km = _mla_prep_fwd(qa, kvp, kr, tm, mla_scale)
        oa, lsa = _attn_fwd(qm, km, kvp, split=True, npairs=3, kblk=lambda p: p, vblk=lambda p: 6 + p,
                            name="mla_attn_fwd")
        oc, lsc = _attn_fwd(qg, kg, vg, split=False, npairs=2, kblk=lambda p: 0, vblk=lambda p: 0,
                            name="gqa_attn_fwd")
        obs, lbs = [], []
        for b, (dil, L, tq, _) in enumerate(branches):
            o_b, l_b = _dil_fwd(*qkv[b], tabs[b], dil=dil, L=L, tq=tq, name=f"dil_fwd_{b}")
            obs.append(o_b)
            lbs.append(l_b)
        ob = _dil_mix_fwd(obs, lbs)
        cat = jnp.concatenate([oa, ob, oc], axis=1)
        mix = _mm(cat, wout[l], name="mm_out")
        x1, x1b, z1 = _ln_fwd(xf, mix, small["ln1_g"][l][None], small["ln1_b"][l][None], alpha, name="ln1_fwd")
        W, _ = weights_of_layer(l, "ffn", x1b)
        wg[l], wu[l], wdn[l] = W["ffn_w_gate"], W["ffn_w_up"], W["ffn_w_down"]
        g3, u3, act = _ffn_up(x1b, wg[l], wu[l])
        ff = _mm(act, wdn[l], ga=True, gb=True, name="mm_down")
        x2, x2b, z2 = _ln_fwd(x1, ff, small["ln2_g"][l][None], small["ln2_b"][l][None], alpha, name="ln2_fwd")
        saved.append(dict(xb=xb, h=h, cq=cq, ckv=ckv, qg=qg, kg=kg, vg=vg, kvp=kvp, qm=qm, km=km, oa=oa, lsa=lsa,
                          oc=oc, lsc=lsc, obs=obs, lbs=lbs, qkv=qkv, cat=cat, z1=z1, x1b=x1b, g3=g3, u3=u3, act=act, z2=z2,
                          gq=gq, gkv=gkv, ggq=ggq, ggk=ggk))
        xf, xb = x2, x2b

    dy, loss = _loss_kernel(xf, target)

    gW = {k: [None] * depth for k in _BIG}
    gS = {k: [None] * depth for k in ("mla_q_norm", "mla_kv_norm", "gqa_q_norm", "gqa_k_norm", "ln1_g", "ln1_b", "ln2_g",
                                      "ln2_b")}
    g_rel = None
    dya, dyb = dy, None
    token = None
    for l in reversed(range(depth)):
        sv = saved[l]
        ln2_g = small["ln2_g"][l][None]
        if token is not None:
            ln2_g = ln2_g + token[0, 0]
        dz2, dz2b, gS["ln2_g"][l], gS["ln2_b"][l] = _ln_bwd(dya, dyb, sv["z2"], ln2_g, alpha,
                                                             name="ln2_bwd" if dyb is not None else "ln2_bwd_last")
        gW["ffn_w_down"][l] = _mm(sv["act"], dz2b, ta=True, ga=True, go=True, name="mm_down_dw")
        dg3, du3 = _ffn_down_dx(dz2b, wdn[l], sv["g3"], sv["u3"])
        gW["ffn_w_gate"][l] = _mm(dg3, sv["x1b"], ta=True, ga=True, go=True, name="mm_gate_dw")
        gW["ffn_w_up"][l] = _mm(du3, sv["x1b"], ta=True, ga=True, go=True, name="mm_up_dw")
        dx1 = _ffn_up_dx(dg3, du3, wg[l], wu[l])
        token = grads_done(l, "ffn", {n: gW[n][l] for n in _FFN_WEIGHTS})
        ln1_g = small["ln1_g"][l][None]
        if token is not None:
            ln1_g = ln1_g + token[0, 0]
        dz1, dz1b, gS["ln1_g"][l], gS["ln1_b"][l] = _ln_bwd(dx1, dz2, sv["z1"], ln1_g, alpha, name="ln1_bwd")
        gW["w_out"][l] = _perm_gqa_rows(_mm(sv["cat"], dz1b, ta=True, name="mm_out_dw")).reshape(4, -1, D)
        dcat = _mm(dz1b, wout[l], tb=True, name="mm_out_dx")
        dqg, dkg, dvg = _attn_bwd(sv["qg"], sv["kg"], sv["vg"], dcat, sv["oc"], sv["lsc"], split=False, npairs=2,
                                  kblk=lambda p: 0, vblk=lambda p: 0, doblk=lambda p: 6 + p, shared_kv=True,
                                  name="gqa_attn_bwd")
        dqm, dkm, dvm = _attn_bwd(sv["qm"], sv["km"], sv["kvp"], dcat, sv["oa"], sv["lsa"], split=True, npairs=3,
                                  kblk=lambda p: p, vblk=lambda p: 6 + p, doblk=lambda p: p, shared_kv=False,
                                  name="mla_attn_bwd")
        dqa, dkvp, dkr = _mla_prep_bwd(dqm, dkm, dvm, tm, mla_scale)
        gW["mla_w_uq"][l] = _rows_to_shards(_mm(dqa, sv["cq"], ta=True, name="mm_uq_dw"), uq_idx, MLA_HEADS * 96)
        dcq = _mm(dqa, wuq[l], name="mm_uq_dx")
        gW["mla_w_ukv"][l] = _rows_to_shards(_mm(dkvp, sv["ckv"], ta=True, name="mm_ukv_dw"), ukv_idx, MLA_HEADS * 128)
        dckv = _mm(dkvp, wukv[l], name="mm_ukv_dx")
        mixb = _dil_mix_bwd(dcat, sv["obs"], sv["lbs"], j384)
        ddq, ddk, ddv = [], [], []
        for b, (dil, L, tq, idx) in enumerate(branches):
            dq_b, dk_b, dv_b, dtab = _dil_bwd(*sv["qkv"][b], tabs[b], mixb[b], sv["lbs"][b], mixb[3 + b], dil=dil, L=L,
                                              tq=tq, name=f"dil_bwd_{b}")
            if dil == 1:
                dq_b, dk_b, dv_b = dq_b[0], dk_b[0, DIL_HALF:DIL_HALF + S], dv_b[0, DIL_HALF:DIL_HALF + S]
            ddq.append(dq_b)
            ddk.append(dk_b)
            ddv.append(dv_b)
            g_b = _bias_reduce(idx, dtab, name=f"bias_reduce_{b}")[:, :, 0].T
            g_rel = g_b if g_rel is None else g_rel + g_b
        dh, n1, n2, n3, n4 = _prep_bwd(sv["h"], dcq, dckv, dkr, ddq, ddk, ddv, dqg, dkg, dvg, sv["gq"], sv["gkv"],
                                       sv["ggq"], sv["ggk"], tg, j256)
        gS["mla_q_norm"][l], gS["mla_kv_norm"][l] = n1[0], n2[0]
        gS["gqa_q_norm"][l] = n3[0].reshape(4, 64).sum(0)
        gS["gqa_k_norm"][l] = n4[0].reshape(2, 64).sum(0)
        gW["w_in"][l] = _rows_to_shards(_mm(dh, sv["xb"], ta=True, name="mm_in_dw"), in_idx, IN_W)
        dya = _mm(dh, win[l], name="mm_in_dx")
        dyb = dz1
        token = grads_done(l, "attn", {n: gW[n][l] for n in _ATTN_WEIGHTS})
    grad_x = _axpy(dya, dyb, alpha, name="grad_x")

    gsmall = {k: jnp.stack([a.reshape(-1) for a in v]) for k, v in gS.items()}
    gsmall["rel_bias"] = g_rel
    return loss, grad_x, gsmall


_ORDER = ("w_in", "mla_q_norm", "mla_kv_norm", "mla_w_uq", "mla_w_ukv", "gqa_q_norm", "gqa_k_norm", "rel_bias", "w_out",
          "ln1_g", "ln1_b", "ffn_w_gate", "ffn_w_up", "ffn_w_down", "ln2_g", "ln2_b")


def kernel(x, w_in, mla_q_norm, mla_kv_norm, mla_w_uq, mla_w_ukv, gqa_q_norm, gqa_k_norm, rel_bias, w_out, ln1_g, ln1_b, ffn_w_gate, ffn_w_up, ffn_w_down, ln2_g, ln2_b, loss_target, m_w_in, m_mla_q_norm, m_mla_kv_norm, m_mla_w_uq, m_mla_w_ukv, m_gqa_q_norm, m_gqa_k_norm, m_rel_bias, m_w_out, m_ln1_g, m_ln1_b, m_ffn_w_gate, m_ffn_w_up, m_ffn_w_down, m_ln2_g, m_ln2_b, v_w_in, v_mla_q_norm, v_mla_kv_norm, v_mla_w_uq, v_mla_w_ukv, v_gqa_q_norm, v_gqa_k_norm, v_rel_bias, v_w_out, v_ln1_g, v_ln1_b, v_ffn_w_gate, v_ffn_w_up, v_ffn_w_down, v_ln2_g, v_ln2_b):
    wts = dict(zip(_ORDER, (w_in, mla_q_norm, mla_kv_norm, mla_w_uq, mla_w_ukv, gqa_q_norm, gqa_k_norm, rel_bias, w_out,
                            ln1_g, ln1_b, ffn_w_gate, ffn_w_up, ffn_w_down, ln2_g, ln2_b)))
    mom = dict(zip(_ORDER, (m_w_in, m_mla_q_norm, m_mla_kv_norm, m_mla_w_uq, m_mla_w_ukv, m_gqa_q_norm, m_gqa_k_norm,
                            m_rel_bias, m_w_out, m_ln1_g, m_ln1_b, m_ffn_w_gate, m_ffn_w_up, m_ffn_w_down, m_ln2_g,
                            m_ln2_b)))
    var = dict(zip(_ORDER, (v_w_in, v_mla_q_norm, v_mla_kv_norm, v_mla_w_uq, v_mla_w_ukv, v_gqa_q_norm, v_gqa_k_norm,
                            v_rel_bias, v_w_out, v_ln1_g, v_ln1_b, v_ffn_w_gate, v_ffn_w_up, v_ffn_w_down, v_ln2_g,
                            v_ln2_b)))
    small_shapes = [wts[n].shape for n in _SMALL]
    for d in (wts, mom, var):
        for n in _COL_SHARDED:
            d[n] = d[n].transpose(0, 2, 1)

    depth = 2
    shards = {n: wts[n].astype(WIRE) for n in _BIG}
    flying = {}

    def start_gather(names, l, tag):
        own = tuple(_OWNER[n] for n in names)
        sh = [shards[n] for n in names]
        ssem, rsem, lands, token = _gather_start(sh, own, l, tag)
        return (names, own, sh, ssem, rsem, lands, l, tag), token

    def end_gather(flight, after):
        names, own, sh, ssem, rsem, lands, l, tag = flight
        got = _gather_finish(sh, _gather_wait(ssem, rsem, sh, lands, after, own, l, tag), own, l, tag)
        return {n: g.astype(CDT) for n, g in zip(names, got)}

    def weights_of_layer(l, part, after):
        if (l, part) == (0, "attn"):
            got = end_gather(start_gather(_ATTN_WEIGHTS, 0, "attn0")[0], after)
            flying["ffn0"], t0 = start_gather(_FFN_WEIGHTS, 0, "ffn0")
            flying["layer1"], t1 = start_gather(_BIG, 1, "layer1")
            return got, t0 + t1
        if (l, part) == (0, "ffn"):
            return end_gather(flying.pop("ffn0"), after), None
        if part == "attn":
            flying["w1"] = end_gather(flying.pop("layer1"), after)
        return flying["w1"], None

    def grads_done(l, part, grads):
        names = tuple(grads)
        own = tuple(_OWNER[n] for n in names)
        tag = f"{part}{l}"
        gl = [grads[n] for n in names]
        theirs = _rs_to_owner(gl, own, tag)
        pairs = [_sum_pair(g, t, o, name=f"rs_pair_sum_{n}") for n, g, t, o in zip(names, gl, theirs, own)]
        ssem, rsem, lands, token = _a2a_start(pairs, own, tag)
        flying[tag] = (names, own, ssem, rsem, pairs, lands)
        return token

    small = {n: wts[n] for n in _SMALL}
    loss, grad_x, gsmall = _local_step(x[0], loss_target[0], small, depth, weights_of_layer, grads_done)

    reds = {}
    for l in reversed(range(depth)):
        for part in ("ffn", "attn"):
            tag = f"{part}{l}"
            names, own, ssem, rsem, pairs, lands = flying.pop(tag)
            got = _a2a_wait(ssem, rsem, pairs, lands, grad_x, own, tag)
            for n, p, t, o in zip(names, pairs, got, own):
                reds[n, l] = _sum_chips(p, t, o, name=f"rs_sum_chips_{n}")
    order = [(n, l) for l in range(depth) for n in _BIG]
    sibs = dict(zip(order, _rs_from_owner([reds[k] for k in order], tuple(_OWNER[n] for n, _ in order))))

    sflat = _pack_flat([gsmall[n].reshape(-1) for n in _SMALL], 8 * LANES)
    rs = sflat.shape[0] // LANES
    sall = _gather_small(sflat.reshape(rs, LANES))

    def packed(d):
        return _pack_flat([d[n] for n in _SMALL], 8 * LANES).reshape(rs, LANES)

    outs = {tag: {} for tag in ("grad", "delta", "new_m", "new_v")}
    for n in _BIG:
        res = _adamw(wts[n], [reds[n, 0], reds[n, 1]], [sibs[n, 0], sibs[n, 1]], mom[n], var[n], _OWNER[n],
                     name=f"adamw_{n}")
        for tag, r in zip(("grad", "delta", "new_m", "new_v"), res):
            outs[tag][n] = r.transpose(0, 2, 1) if n in _COL_SHARDED else r
    for tag, smallflat in zip(("grad", "delta", "new_m", "new_v"), _adamw_small(packed(wts), sall, packed(mom), packed(var))):
        outs[tag].update(zip(_SMALL, _unpack_flat(smallflat.reshape(-1), small_shapes)))

    total = lax.psum(loss[0, 0], ("x", "y", "c"))
    return (total, grad_x[None], *[outs["grad"][n] for n in _ORDER], *[outs["delta"][n] for n in _ORDER],
            *[outs["new_m"][n] for n in _ORDER], *[outs["new_v"][n] for n in _ORDER])
```

```python
import functools
import math

import numpy as np
import jax
import jax.numpy as jnp
from jax import lax
from jax.experimental import pallas as pl
from jax.experimental.pallas import tpu as pltpu

F32 = jnp.float32
CDT = jnp.bfloat16
WIRE = jnp.bfloat16

HEAD_DIM = 64
GRID_W = 64
ROPE_THETA = 10000.0
MLA_HEADS = 6
MLA_Q_RANK = 256
MLA_KV_RANK = 128
MLA_ROPE_DIM = 32
DIL_HEADS = 6
DIL_BRANCHES = ((128, 1), (512, 4), (2048, 16))
DIL_HALF = 64
GQA_Q_HEADS = 4
REL_BUCKETS = 32
REL_MAX_DIST = 1024
NEG_INF = -1e30
LANES = 128
VMEM_LIMIT = 56 * 1024 * 1024

ADAM_LR, ADAM_B1, ADAM_B2, ADAM_EPS, ADAM_WD, ADAM_STEP = 0.001, 0.9, 0.999, 1e-08, 0.01, 10

C_CQ, C_CKV, C_KR, C_DQ, C_DK, C_DV, C_GQ, C_GK, C_GV, IN_P = 0, 256, 384, 512, 896, 1280, 1664, 1920, 2048, 2176
IN_W = 2080
MESH_ID = pl.DeviceIdType.MESH


def _cparams(n_axes, vmem=VMEM_LIMIT):
    return pltpu.CompilerParams(dimension_semantics=("arbitrary",) * n_axes, vmem_limit_bytes=vmem)


MAX_WHOLE_DIM = 2304


def _pick(n, target):
    best = None
    for t in range(LANES, min(n, target) + 1, LANES):
        if n % t == 0:
            best = t
    if best is not None and (2 * best >= target or n > MAX_WHOLE_DIM):
        return best
    return n


def _sds(shape, dtype):
    return jax.ShapeDtypeStruct(tuple(shape), dtype)


def _in_cols():
    idx = -np.ones((IN_P,), np.int64)
    idx[C_CQ:C_CQ + 256] = np.arange(0, 256)
    idx[C_CKV:C_CKV + 128] = np.arange(256, 384)
    idx[C_KR + 64:C_KR + 96] = np.arange(384, 416)
    idx[C_DQ:C_DQ + 1152] = np.arange(416, 1568)
    gq = 1568 + (np.array([0, 2, 1, 3])[:, None] * 64 + np.arange(64)[None, :]).reshape(-1)
    idx[C_GQ:C_GQ + 256] = gq
    idx[C_GK:C_GK + 256] = np.arange(1824, 2080)
    return idx


def _uq_cols():
    idx = -np.ones((MLA_HEADS * 128,), np.int64)
    for h in range(MLA_HEADS):
        idx[h * 128:h * 128 + 96] = np.arange(96 * h, 96 * h + 96)
    return idx


def _ukv_cols():
    idx = -np.ones((MLA_HEADS * 128 + MLA_HEADS * 64,), np.int64)
    for h in range(MLA_HEADS):
        idx[h * 128:h * 128 + 64] = np.arange(128 * h, 128 * h + 64)
        idx[768 + h * 64:768 + h * 64 + 64] = np.arange(128 * h + 64, 128 * h + 128)
    return idx


def _out_rows():
    idx = np.arange(1024)
    idx[768:1024] = 768 + (np.array([0, 2, 1, 3])[:, None] * 64 + np.arange(64)[None, :]).reshape(-1)
    return idx


def _runs(idx):
    out, i = [], 0
    while i < len(idx):
        j = i + 1
        while j < len(idx) and ((idx[i] < 0 and idx[j] < 0) or (idx[i] >= 0 and idx[j] == idx[j - 1] + 1)):
            j += 1
        out.append((int(idx[i]), j - i))
        i = j
    return out


def _rows_from_shards(sh, idx):
    _, cs, r = sh.shape
    pieces = []
    for first, ln in _runs(idx):
        if first < 0:
            pieces.append(jnp.zeros((ln, r), sh.dtype))
            continue
        while ln > 0:
            k, off = divmod(first, cs)
            take = min(ln, cs - off)
            pieces.append(sh[k, off:off + take, :])
            first, ln = first + take, ln - take
    return jnp.concatenate(pieces, axis=0)


def _rows_to_shards(wp, idx, n):
    inv = np.zeros((n,), np.int64)
    pos = np.nonzero(idx >= 0)[0]
    inv[idx[pos]] = pos
    cs = n // 4
    shards = []
    for k in range(4):
        pieces = [wp[first:first + ln, :] for first, ln in _runs(inv[k * cs:(k + 1) * cs])]
        shards.append(jnp.concatenate(pieces, axis=0))
    return jnp.stack(shards)


def _t5_bucket_np(rel):
    nb = REL_BUCKETS // 2
    exact = nb // 2
    ret = np.where(rel > 0, nb, 0)
    n = np.abs(rel)
    nf = np.maximum(n, 1).astype(np.float32)
    large = exact + (np.log(nf / np.float32(exact)) / np.float32(math.log(REL_MAX_DIST / exact))
                     * np.float32(nb - exact)).astype(np.int32)
    large = np.minimum(large, nb - 1)
    return ret + np.where(n < exact, n, large)


def _branch_bucket_idx(tq, dil):
    kw = tq + 2 * DIL_HALF
    rel = np.arange(kw)[None, :] - DIL_HALF - np.arange(tq)[:, None]
    idx = _t5_bucket_np(rel * dil)
    return np.where(np.abs(rel) <= DIL_HALF, idx, -1).astype(np.int32)


def _rope_tables(S):
    inv = ROPE_THETA ** (-jnp.arange(0, 32, 2, dtype=F32) / 32)
    t = jnp.arange(S)
    pos = t.astype(F32)
    row = (t // GRID_W).astype(F32)
    col = (t % GRID_W).astype(F32)
    lane = np.arange(LANES)
    wm = lane - 64
    is_rope = (lane >= 64) & (lane < 96)
    ang = pos[:, None] * inv[np.where(is_rope, wm % 16, 0)][None, :]
    cm = jnp.where(is_rope[None], jnp.cos(ang), 1.0)
    smm = jnp.where((is_rope & (wm < 16))[None], -jnp.sin(ang), 0.0)
    spm = jnp.where((is_rope & (wm >= 16))[None], jnp.sin(ang), 0.0)
    g = lane % 64
    w = g % 32
    angg = jnp.where((g < 32)[None], row[:, None], col[:, None]) * inv[w % 16][None, :]
    cg = jnp.cos(angg)
    smg = jnp.where((w < 16)[None], -jnp.sin(angg), 0.0)
    spg = jnp.where((w >= 16)[None], jnp.sin(angg), 0.0)
    return (cm, smm, spm), (cg, smg, spg)


def _lanes(t, width):
    return t if width == LANES else jnp.concatenate([t] * (width // LANES), axis=1)


def _rope(x, tabs):
    c, sm, sp = (_lanes(t, x.shape[1]) for t in tabs)
    w = x.shape[1]
    return x * c + pltpu.roll(x, w - 16, 1) * sm + pltpu.roll(x, 16, 1) * sp


def _rope_t(dy, tabs):
    c, sm, sp = (_lanes(t, dy.shape[1]) for t in tabs)
    w = dy.shape[1]
    return dy * c + pltpu.roll(dy * sm, 16, 1) + pltpu.roll(dy * sp, w - 16, 1)


def _head_ones(width):
    i = np.arange(width)
    return jnp.asarray((i[:, None] // HEAD_DIM == i[None, :] // HEAD_DIM).astype(np.float32))


def _headsum(x, j):
    return jnp.dot(x, j, preferred_element_type=F32, precision=lax.Precision.HIGHEST)


def _mm(a, b, *, ta=False, tb=False, ga=False, gb=False, go=False, out_dtype=F32, add=None, name):
    G = a.shape[0] if ga else (b.shape[0] if gb else 1)
    a2 = a.shape[1:] if ga else a.shape
    b2 = b.shape[1:] if gb else b.shape
    K, M = a2 if ta else a2[::-1]
    N = b2[0] if tb else b2[1]
    assert (b2[1] if tb else b2[0]) == K
    tm, tn, tk = _pick(M, 1024), _pick(N, 1024), _pick(K, 1024)
    if tm * tn > 1024 * 1152:
        tm = _pick(M, 512)
    nk = K // tk
    steps = nk if (go or G == 1) else G * nk
    dn = (((0 if ta else 1,), (1 if tb else 0,)), ((), ()))

    def body(a_ref, b_ref, *rest):
        rest = list(rest)
        add_ref = rest.pop(0) if add is not None else None
        o_ref = rest.pop(0)
        part = lax.dot_general(a_ref[...], b_ref[...], dn, preferred_element_type=F32)
        if steps == 1:
            if add_ref is not None:
                part = part + add_ref[...]
            o_ref[...] = part.astype(o_ref.dtype)
            return
        acc_ref, = rest
        s = pl.program_id(3)

        @pl.when(s == 0)
        def _():
            acc_ref[...] = part if add_ref is None else part + add_ref[...]

        @pl.when(s > 0)
        def _():
            acc_ref[...] += part

        @pl.when(s == steps - 1)
        def _():
            o_ref[...] = acc_ref[...].astype(o_ref.dtype)

    def grp(g, s):
        return g if go else s // nk

    def kk(s):
        return s if steps == nk else s % nk

    def spec(grouped, block, index):
        if grouped:
            return pl.BlockSpec((None,) + block, lambda g, i, j, s: (grp(g, s),) + index(i, j, s))
        return pl.BlockSpec(block, lambda g, i, j, s: index(i, j, s))

    a_spec = (spec(ga, (tk, tm), lambda i, j, s: (kk(s), i)) if ta else spec(ga, (tm, tk), lambda i, j, s: (i, kk(s))))
    b_spec = (spec(gb, (tn, tk), lambda i, j, s: (j, kk(s))) if tb else spec(gb, (tk, tn), lambda i, j, s: (kk(s), j)))
    o_spec = spec(go, (tm, tn), lambda i, j, s: (i, j))
    return pl.pallas_call(
        body, name=name, grid=(G if go else 1, M // tm, N // tn, steps),
        in_specs=[a_spec, b_spec] + ([o_spec] if add is not None else []), out_specs=o_spec,
        out_shape=_sds(((G,) if go else ()) + (M, N), out_dtype),
        scratch_shapes=[pltpu.VMEM((tm, tn), F32)] if steps > 1 else [],
        compiler_params=_cparams(4),
    )(*([a, b] + ([add] if add is not None else [])))


def _row(ts, w, cb=0):
    return pl.BlockSpec((ts, w), lambda i: (i, cb))


def _full(shape):
    nd = len(shape)
    return pl.BlockSpec(tuple(shape), lambda i: (0,) * nd)


def _rms_fwd(x, g, eps=1e-6):
    r = lax.rsqrt(jnp.mean(x * x, axis=-1, keepdims=True) + eps)
    return x * r * g


def _rms_bwd(x, g, dy, eps=1e-6):
    r = lax.rsqrt(jnp.mean(x * x, axis=-1, keepdims=True) + eps)
    gdy = g * dy
    dx = r * gdy - x * (r * r * r) * jnp.mean(x * gdy, axis=-1, keepdims=True)
    return dx, x * r * dy


def _rms_head_fwd(x, g, j, eps=1e-6):
    r = lax.rsqrt(_headsum(x * x, j) * (1.0 / HEAD_DIM) + eps)
    return x * r * g


def _rms_head_bwd(x, g, dy, j, eps=1e-6):
    r = lax.rsqrt(_headsum(x * x, j) * (1.0 / HEAD_DIM) + eps)
    gdy = g * dy
    dx = r * gdy - x * (r * r * r) * (_headsum(x * gdy, j) * (1.0 / HEAD_DIM))
    return dx, x * r * dy


DIL_STRIDES = tuple(d for _, d in DIL_BRANCHES if d > 1)
DIL_W = DIL_HEADS * HEAD_DIM


def _res_spec(d, n, pad_blocks=0):
    return pl.BlockSpec((d, n, DIL_W), lambda i: (0, i + pad_blocks, 0))


def _prep_fwd(h, gq, gkv, ggq, ggk, tm, tg, j256):
    S = h.shape[0]
    ts = min(256, S)
    scale = HEAD_DIM ** -0.5
    nres = len(DIL_STRIDES)

    def body(*refs):
        (h_ref, gq_ref, gkv_ref, ggq_ref, ggk_ref, cm, smm, spm, cg, smg, spg, j_ref), refs = refs[:12], refs[12:]
        refs = refs[2 * nres:]
        (cq_o, ckv_o, kr_o, dq_o, dk_o, dv_o, gq_o, gk_o, gv_o), res_o = refs[:9], refs[9:-1]
        st = refs[-1]
        tabm = (cm[...], smm[...], spm[...])
        tabg = (cg[...], smg[...], spg[...])
        cq_o[...] = _rms_fwd(h_ref[:, C_CQ:C_CQ + 256], gq_ref[...]).astype(CDT)
        ckv_o[...] = _rms_fwd(h_ref[:, C_CKV:C_CKV + 128], gkv_ref[...]).astype(CDT)
        kr_o[...] = _rope(h_ref[:, C_KR:C_KR + 128], tabm).astype(CDT)
        dq_o[...] = (h_ref[:, C_DQ:C_DQ + 384] * scale).astype(CDT)
        dk_o[...] = h_ref[:, C_DK:C_DK + 384].astype(CDT)
        dv_o[...] = h_ref[:, C_DV:C_DV + 384].astype(CDT)
        for j, lanes in _lane_blocks(3 * DIL_W):
            st[j] = h_ref[:, C_DQ + lanes.start:C_DQ + lanes.stop] * (scale if j < 3 else 1.0)
        for bi, d in enumerate(DIL_STRIDES):
            for c in range(d):
                rows = pl.ds(c, ts // d, stride=d)
                for j, lanes in _lane_blocks(3 * DIL_W):
                    res_o[3 * bi + j // 3][c, :, (j % 3) * LANES:(j % 3 + 1) * LANES] = st.at[j][rows, :].astype(CDT)
        qn = _rms_head_fwd(h_ref[:, C_GQ:C_GQ + 256], ggq_ref[...], j_ref[...])
        gq_o[...] = (_rope(qn, tabg) * scale).astype(CDT)
        kn = _rms_head_fwd(h_ref[:, C_GK:C_GK + 128], ggk_ref[...], j_ref[0:128, 0:128])
        gk_o[...] = _rope(kn, tabg).astype(CDT)
        gv_o[...] = h_ref[:, C_GV:C_GV + 128].astype(CDT)

    widths = (256, 128, 128, 384, 384, 384, 256, 128, 128)
    out_specs = [_row(ts, w) for w in widths]
    out_shape = [_sds((S, w), CDT) for w in widths]
    zeros, aliases = [], {}
    for d in DIL_STRIDES:
        n, L = ts // d, S // d
        out_specs += [_res_spec(d, n), _res_spec(d, n, DIL_HALF // n), _res_spec(d, n, DIL_HALF // n)]
        out_shape += [_sds((d, L, DIL_W), CDT)] + [_sds((d, L + 2 * DIL_HALF, DIL_W), CDT)] * 2
        for t in range(2):
            aliases[12 + len(zeros)] = len(out_shape) - 2 + t
            zeros.append(jnp.zeros((d, L + 2 * DIL_HALF, DIL_W), CDT))
    return pl.pallas_call(
        body, name="prep_fwd", grid=(S // ts,),
        in_specs=[_row(ts, IN_P), _full(gq.shape), _full(gkv.shape), _full(ggq.shape), _full(ggk.shape)]
        + [_row(ts, LANES)] * 6 + [_full(j256.shape)] + [pl.BlockSpec(memory_space=pl.ANY)] * len(zeros),
        out_specs=out_specs, out_shape=out_shape, input_output_aliases=aliases,
        scratch_shapes=[pltpu.VMEM((3 * DIL_W // LANES, ts, LANES), F32)],
        compiler_params=_cparams(1),
    )(h, gq, gkv, ggq, ggk, *tm, *tg, j256, *zeros)


def _prep_bwd(h, dcq, dckv, dkr, ddq, ddk, ddv, dgq, dgk, dgv, gq, gkv, ggq, ggk, tg, j256):
    S = h.shape[0]
    ts = min(256, S)
    scale = HEAD_DIM ** -0.5

    def body(h_ref, dcq_r, dckv_r, dkr_r, q1, q2, q3, k1, k2, k3, v1, v2, v3, dgq_r, dgk_r, dgv_r,
             gq_ref, gkv_ref, ggq_ref, ggk_ref, cg, smg, spg, j_ref,
             dh_o, ngq_o, ngkv_o, nggq_o, nggk_o, *scr):
        tabg = (cg[...], smg[...], spg[...])
        first = pl.program_id(0) == 0
        scr, = scr
        d2, d3 = DIL_STRIDES
        dq = q1[...] + _by_token(q2, scr, d2) + _by_token(q3, scr, d3)
        dk = k1[...] + _by_token(k2, scr, d2) + _by_token(k3, scr, d3)
        dv = v1[...] + _by_token(v2, scr, d2) + _by_token(v3, scr, d3)

        def acc(o_ref, val):
            s = jnp.sum(val, axis=0, keepdims=True)

            @pl.when(first)
            def _():
                o_ref[...] = s

            @pl.when(jnp.logical_not(first))
            def _():
                o_ref[...] += s

        dx, dg = _rms_bwd(h_ref[:, C_CQ:C_CQ + 256], gq_ref[...], dcq_r[...])
        dh_o[:, C_CQ:C_CQ + 256] = dx.astype(CDT)
        acc(ngq_o, dg)
        dx, dg = _rms_bwd(h_ref[:, C_CKV:C_CKV + 128], gkv_ref[...], dckv_r[...])
        dh_o[:, C_CKV:C_CKV + 128] = dx.astype(CDT)
        acc(ngkv_o, dg)
        dh_o[:, C_KR:C_KR + 128] = dkr_r[...].astype(CDT)
        dh_o[:, C_DQ:C_DQ + 384] = (dq * scale).astype(CDT)
        dh_o[:, C_DK:C_DK + 384] = dk.astype(CDT)
        dh_o[:, C_DV:C_DV + 384] = dv.astype(CDT)
        dqn = _rope_t(dgq_r[...] * scale, tabg)
        dx, dg = _rms_head_bwd(h_ref[:, C_GQ:C_GQ + 256], ggq_ref[...], dqn, j_ref[...])
        dh_o[:, C_GQ:C_GQ + 256] = dx.astype(CDT)
        acc(nggq_o, dg)
        dkn = _rope_t(dgk_r[...], tabg)
        dx, dg = _rms_head_bwd(h_ref[:, C_GK:C_GK + 128], ggk_ref[...], dkn, j_ref[0:128, 0:128])
        dh_o[:, C_GK:C_GK + 128] = dx.astype(CDT)
        acc(nggk_o, dg)
        dh_o[:, C_GV:C_GV + 128] = dgv_r[...].astype(CDT)

    d2, d3 = DIL_STRIDES
    n2, n3 = ts // d2, ts // d3
    tok = _row(ts, DIL_W)
    return pl.pallas_call(
        body, name="prep_bwd", grid=(S // ts,),
        in_specs=[_row(ts, IN_P), _row(ts, 256), _row(ts, 128), _row(ts, 128)]
        + [tok, _res_spec(d2, n2), _res_spec(d3, n3)]
        + [tok, _res_spec(d2, n2, DIL_HALF // n2), _res_spec(d3, n3, DIL_HALF // n3)] * 2
        + [_row(ts, 256), _row(ts, 128), _row(ts, 128)]
        + [_full(gq.shape), _full(gkv.shape), _full(ggq.shape), _full(ggk.shape)] + [_row(ts, LANES)] * 3
        + [_full(j256.shape)],
        out_specs=[_row(ts, IN_P), _full((1, 256)), _full((1, 128)), _full((1, 256)), _full((1, 128))],
        out_shape=[_sds((S, IN_P), CDT), _sds((1, 256), F32), _sds((1, 128), F32), _sds((1, 256), F32),
                   _sds((1, 128), F32)],
        scratch_shapes=[_TOKEN_SCRATCH(ts)],
        compiler_params=_cparams(1),
    )(h, dcq, dckv, dkr, *ddq, *ddk, *ddv, dgq, dgk, dgv, gq, gkv, ggq, ggk, *tg, j256)


def _lane_blocks(width):
    return [(j, slice(j * LANES, (j + 1) * LANES)) for j in range(width // LANES)]


_TOKEN_SCRATCH = lambda ts: pltpu.VMEM((DIL_W // LANES, ts, LANES), F32)


def _by_token(res_ref, scr_ref, d):
    n = res_ref.shape[1]
    if d == 1:
        return res_ref[0].astype(F32)
    for c in range(d):
        for j, lanes in _lane_blocks(res_ref.shape[2]):
            scr_ref.at[j][pl.ds(c, n, stride=d), :] = res_ref[c, :, lanes].astype(F32)
    return jnp.concatenate([scr_ref[j] for j, _ in _lane_blocks(res_ref.shape[2])], axis=1)


def _by_residue(val, scr_ref, out_ref, d):
    n = out_ref.shape[1]
    if d == 1:
        out_ref[0] = val.astype(out_ref.dtype)
        return
    for j, lanes in _lane_blocks(out_ref.shape[2]):
        scr_ref[j] = val[:, lanes]
    for c in range(d):
        for j, lanes in _lane_blocks(out_ref.shape[2]):
            out_ref[c, :, lanes] = scr_ref.at[j][pl.ds(c, n, stride=d), :].astype(out_ref.dtype)


def _mla_prep_fwd(qa, kvp, kr, tm, scale):
    S = qa.shape[0]
    ts = min(256, S)

    def body(qa_ref, kv_ref, kr_ref, cm, smm, spm, q_o, k_o):
        tabm = (cm[...], smm[...], spm[...])
        q_o[...] = (_rope(qa_ref[...], tabm) * scale).astype(CDT)
        k_o[...] = kv_ref[:, 0:768] + _lanes(kr_ref[...], 768)

    return pl.pallas_call(
        body, name="mla_prep_fwd", grid=(S // ts,),
        in_specs=[_row(ts, 768), _row(ts, 1152), _row(ts, 128)] + [_row(ts, LANES)] * 3,
        out_specs=[_row(ts, 768)] * 2, out_shape=[_sds((S, 768), CDT)] * 2,
        compiler_params=_cparams(1),
    )(qa, kvp, kr, *tm)


def _mla_prep_bwd(dq, dk, dv, tm, scale):
    S = dq.shape[0]
    ts = min(256, S)

    def body(dq_ref, dk_ref, dv_ref, cm, smm, spm, dqa_o, dkv_o, dkr_o):
        tabm = (cm[...], smm[...], spm[...])
        lane = lax.broadcasted_iota(jnp.int32, (1, LANES), 1)
        dqa_o[...] = _rope_t(dq_ref[...] * scale, tabm).astype(CDT)
        dkr = jnp.zeros((ts, LANES), F32)
        for hd in range(MLA_HEADS):
            blk = dk_ref[:, hd * 128:(hd + 1) * 128]
            dkv_o[:, hd * 128:(hd + 1) * 128] = jnp.where(lane < 64, blk, 0.0).astype(CDT)
            dkr = dkr + jnp.where((lane >= 64) & (lane < 96), blk, 0.0)
        dkv_o[:, 768:1152] = dv_ref[...].astype(CDT)
        dkr_o[...] = jnp.where((lane >= 64) & (lane < 96), _rope_t(dkr, tabm), 0.0)

    return pl.pallas_call(
        body, name="mla_prep_bwd", grid=(S // ts,),
        in_specs=[_row(ts, 768), _row(ts, 768), _row(ts, 384)] + [_row(ts, LANES)] * 3,
        out_specs=[_row(ts, 768), _row(ts, 1152), _row(ts, 128)],
        out_shape=[_sds((S, 768), CDT), _sds((S, 1152), CDT), _sds((S, 128), F32)],
        compiler_params=_cparams(1),
    )(dq, dk, dv, *tm)


def _ln_fwd(xa, xb, g, b, alpha, name):
    S, D = xa.shape
    ts = min(256, S)

    def body(xa_ref, xb_ref, g_ref, b_ref, y_o, yb_o, z_o):
        z = alpha * xa_ref[...] + xb_ref[...]
        mu = jnp.mean(z, axis=-1, keepdims=True)
        zc = z - mu
        var = jnp.mean(zc * zc, axis=-1, keepdims=True)
        y = zc * lax.rsqrt(var + 1e-5) * g_ref[...] + b_ref[...]
        y_o[...] = y
        yb_o[...] = y.astype(CDT)
        z_o[...] = z

    return pl.pallas_call(
        body, name=name, grid=(S // ts,),
        in_specs=[_row(ts, D), _row(ts, D), _full(g.shape), _full(b.shape)],
        out_specs=[_row(ts, D)] * 3, out_shape=[_sds((S, D), F32), _sds((S, D), CDT), _sds((S, D), F32)],
        compiler_params=_cparams(1),
    )(xa, xb, g, b)


def _ln_bwd(dya, dyb, z, g, alpha, name):
    S, D = z.shape
    ts = min(256, S)
    two = dyb is not None

    def body(*refs):
        if two:
            dya_ref, dyb_ref, z_ref, g_ref, dz_o, dzb_o, dg_o, db_o = refs
            dy = dya_ref[...] + alpha * dyb_ref[...]
        else:
            dya_ref, z_ref, g_ref, dz_o, dzb_o, dg_o, db_o = refs
            dy = dya_ref[...]
        z = z_ref[...]
        mu = jnp.mean(z, axis=-1, keepdims=True)
        zc = z - mu
        r = lax.rsqrt(jnp.mean(zc * zc, axis=-1, keepdims=True) + 1e-5)
        xh = zc * r
        dxh = dy * g_ref[...]
        dz = r * (dxh - jnp.mean(dxh, axis=-1, keepdims=True) - xh * jnp.mean(dxh * xh, axis=-1, keepdims=True))
        dz_o[...] = dz
        dzb_o[...] = dz.astype(CDT)
        sg = jnp.sum(dy * xh, axis=0, keepdims=True)
        sb = jnp.sum(dy, axis=0, keepdims=True)
        first = pl.program_id(0) == 0

        @pl.when(first)
        def _():
            dg_o[...] = sg
            db_o[...] = sb

        @pl.when(jnp.logical_not(first))
        def _():
            dg_o[...] += sg
            db_o[...] += sb

    ins = [dya] + ([dyb] if two else []) + [z, g]
    return pl.pallas_call(
        body, name=name, grid=(S // ts,),
        in_specs=[_row(ts, D)] * (3 if two else 2) + [_full(g.shape)],
        out_specs=[_row(ts, D), _row(ts, D), _full((1, D)), _full((1, D))],
        out_shape=[_sds((S, D), F32), _sds((S, D), CDT), _sds((1, D), F32), _sds((1, D), F32)],
        compiler_params=_cparams(1),
    )(*ins)


def _grp_spec(ts, w):
    return pl.BlockSpec((None, ts, w), lambda k, i: (k, i, 0))


def _ffn_up(xb, wg3, wu3):
    S, D = xb.shape
    G, Fc, _ = wg3.shape
    tm = _pick(S, 1024)
    wspec = pl.BlockSpec((None, Fc, D), lambda k, i: (k, 0, 0))

    def body(x_ref, wg_ref, wu_ref, g_o, u_o, a_o):
        x = x_ref[...]
        g = lax.dot_general(x, wg_ref[...], _NT, preferred_element_type=F32)
        u = lax.dot_general(x, wu_ref[...], _NT, preferred_element_type=F32)
        g_o[...] = g
        u_o[...] = u
        a_o[...] = (g / (1.0 + jnp.exp(-g)) * u).astype(CDT)

    return pl.pallas_call(
        body, name="ffn_up", grid=(G, S // tm),
        in_specs=[pl.BlockSpec((tm, D), lambda k, i: (i, 0)), wspec, wspec], out_specs=[_grp_spec(tm, Fc)] * 3,
        out_shape=[_sds((G, S, Fc), F32), _sds((G, S, Fc), F32), _sds((G, S, Fc), CDT)], compiler_params=_cparams(2),
    )(xb, wg3, wu3)


def _ffn_up_dx(dg3, du3, wg3, wu3):
    G, S, Fc = dg3.shape
    D = wg3.shape[2]
    tm = _pick(S, 1024)
    wspec = pl.BlockSpec((None, Fc, D), lambda i, k: (k, 0, 0))
    aspec = pl.BlockSpec((None, tm, Fc), lambda i, k: (k, i, 0))

    def body(dg_ref, du_ref, wg_ref, wu_ref, o_ref):
        part = (jnp.dot(dg_ref[...], wg_ref[...], preferred_element_type=F32)
                + jnp.dot(du_ref[...], wu_ref[...], preferred_element_type=F32))
        k = pl.program_id(1)

        @pl.when(k == 0)
        def _():
            o_ref[...] = part

        @pl.when(k > 0)
        def _():
            o_ref[...] += part

    return pl.pallas_call(
        body, name="ffn_up_dx", grid=(S // tm, G), in_specs=[aspec, aspec, wspec, wspec],
        out_specs=pl.BlockSpec((tm, D), lambda i, k: (i, 0)), out_shape=_sds((S, D), F32), compiler_params=_cparams(2),
    )(dg3, du3, wg3, wu3)


def _ffn_down_dx(dzb, wd3, g3, u3):
    S, D = dzb.shape
    G, Fc, _ = wd3.shape
    tm = _pick(S, 1024)

    def body(dz_ref, wd_ref, g_ref, u_ref, dg_o, du_o):
        da = lax.dot_general(dz_ref[...], wd_ref[...], _NT, preferred_element_type=F32)
        g = g_ref[...]
        sg = 1.0 / (1.0 + jnp.exp(-g))
        dg_o[...] = (da * u_ref[...] * (sg * (1.0 + g * (1.0 - sg)))).astype(CDT)
        du_o[...] = (da * (g * sg)).astype(CDT)

    return pl.pallas_call(
        body, name="ffn_down_dx", grid=(G, S // tm),
        in_specs=[pl.BlockSpec((tm, D), lambda k, i: (i, 0)), pl.BlockSpec((None, Fc, D), lambda k, i: (k, 0, 0)),
                  _grp_spec(tm, Fc), _grp_spec(tm, Fc)],
        out_specs=[_grp_spec(tm, Fc)] * 2, out_shape=[_sds((G, S, Fc), CDT)] * 2, compiler_params=_cparams(2),
    )(dzb, wd3, g3, u3)


def _loss_kernel(y, target):
    S, D = y.shape
    ts = min(256, S)

    def body(y_ref, t_ref, dy_o, loss_o):
        e = y_ref[...] - t_ref[...]
        dy_o[...] = e * (1.0 / D)
        part = jnp.sum(jnp.sum(e * e, axis=1, keepdims=True), axis=0, keepdims=True) * (0.5 / D)
        first = pl.program_id(0) == 0

        @pl.when(first)
        def _():
            loss_o[...] = part

        @pl.when(jnp.logical_not(first))
        def _():
            loss_o[...] += part

    return pl.pallas_call(
        body, name="loss", grid=(S // ts,), in_specs=[_row(ts, D)] * 2,
        out_specs=[_row(ts, D), _full((1, 1))], out_shape=[_sds((S, D), F32), _sds((1, 1), F32)],
        compiler_params=_cparams(1),
    )(y, target)


def _axpy(a, b, alpha, name):
    S, D = a.shape
    ts = min(256, S)

    def body(a_ref, b_ref, o_ref):
        o_ref[...] = a_ref[...] + alpha * b_ref[...]

    return pl.pallas_call(
        body, name=name, grid=(S // ts,), in_specs=[_row(ts, D)] * 2, out_specs=_row(ts, D),
        out_shape=_sds((S, D), F32), compiler_params=_cparams(1),
    )(a, b)


def _pair_masks():
    lane = lax.broadcasted_iota(jnp.int32, (1, LANES), 1)
    first = lane < HEAD_DIM
    return first, jnp.logical_not(first)


def _head_scalar(x, m):
    return jnp.max(jnp.where(m, x, -jnp.inf), axis=-1, keepdims=True)


_NT = (((1,), (1,)), ((), ()))
_TN = (((0,), (0,)), ((), ()))


def _attn_fwd(q, k, v, *, split, npairs, kblk, vblk, name):
    S = q.shape[0]
    qw = 256 if split else LANES
    tq = min(256, S)

    def body(q_ref, k_ref, v_ref, o_ref, lse_ref):
        masks = _pair_masks()
        outs, lses = [], []
        for hd in range(2):
            if split:
                qh = q_ref[:, hd * LANES:(hd + 1) * LANES]
                kh = k_ref[:, hd * LANES:(hd + 1) * LANES]
            else:
                qh = jnp.where(masks[hd], q_ref[...], jnp.zeros_like(q_ref[...]))
                kh = k_ref[...]
            s = lax.dot_general(qh, kh, _NT, preferred_element_type=F32)
            mx = jnp.max(s, axis=-1, keepdims=True)
            p = jnp.exp(s - mx)
            l = jnp.sum(p, axis=-1, keepdims=True)
            o = jnp.dot(p.astype(CDT), v_ref[...], preferred_element_type=F32)
            outs.append(o / l)
            lses.append(jnp.broadcast_to(mx + jnp.log(l), (tq, LANES)))
        o_ref[...] = jnp.where(masks[0], outs[0], outs[1]).astype(o_ref.dtype)
        lse_ref[...] = jnp.where(masks[0], lses[0], lses[1])

    return pl.pallas_call(
        body, name=name, grid=(npairs, S // tq),
        in_specs=[pl.BlockSpec((tq, qw), lambda p, i: (i, p)),
                  pl.BlockSpec((S, qw), lambda p, i: (0, kblk(p))),
                  pl.BlockSpec((S, LANES), lambda p, i: (0, vblk(p)))],
        out_specs=[pl.BlockSpec((tq, LANES), lambda p, i: (i, p))] * 2,
        out_shape=[_sds((S, LANES * npairs), CDT), _sds((S, LANES * npairs), F32)],
        compiler_params=_cparams(2),
    )(q, k, v)


def _attn_bwd(q, k, v, do, o, lse, *, split, npairs, kblk, vblk, doblk, shared_kv, name):
    S = q.shape[0]
    qw = 256 if split else LANES
    tq = min(256, S)
    nkv = 1 if shared_kv else npairs

    def body(q_ref, k_ref, v_ref, do_ref, o_ref, lse_ref, dq_ref, dk_ref, dv_ref):
        masks = _pair_masks()
        p_id, i_id = pl.program_id(0), pl.program_id(1)
        first = (i_id == 0) & ((p_id == 0) if shared_kv else True)
        do = do_ref[...]
        o = o_ref[...].astype(F32)
        lse = lse_ref[...]
        v = v_ref[...]
        dqs, dks, dvs = [], [], []
        for hd in range(2):
            m = masks[hd]
            if split:
                qh = q_ref[:, hd * LANES:(hd + 1) * LANES]
                kh = k_ref[:, hd * LANES:(hd + 1) * LANES]
            else:
                qh = jnp.where(m, q_ref[...], jnp.zeros_like(q_ref[...]))
                kh = k_ref[...]
            doh = jnp.where(m, do, 0.0)
            s = lax.dot_general(qh, kh, _NT, preferred_element_type=F32)
            p = jnp.exp(s - _head_scalar(lse, m))
            delta = jnp.sum(doh * o, axis=-1, keepdims=True)
            dohb = doh.astype(CDT)
            dp = lax.dot_general(dohb, v, _NT, preferred_element_type=F32)
            ds = (p * (dp - delta)).astype(CDT)
            dq = jnp.dot(ds, kh, preferred_element_type=F32)
            dqs.append(dq if split else jnp.where(m, dq, 0.0))
            dks.append(lax.dot_general(ds, qh, _TN, preferred_element_type=F32))
            dvs.append(lax.dot_general(p.astype(CDT), dohb, _TN, preferred_element_type=F32))
        if split:
            dq_ref[:, 0:LANES] = dqs[0]
            dq_ref[:, LANES:2 * LANES] = dqs[1]
        else:
            dq_ref[...] = dqs[0] + dqs[1]
        dv = dvs[0] + dvs[1]

        @pl.when(first)
        def _():
            if split:
                dk_ref[:, 0:LANES] = dks[0]
                dk_ref[:, LANES:2 * LANES] = dks[1]
            else:
                dk_ref[...] = dks[0] + dks[1]
            dv_ref[...] = dv

        @pl.when(jnp.logical_not(first))
        def _():
            if split:
                dk_ref[:, 0:LANES] += dks[0]
                dk_ref[:, LANES:2 * LANES] += dks[1]
            else:
                dk_ref[...] += dks[0] + dks[1]
            dv_ref[...] += dv

    kvo = (lambda p, i: (0, 0)) if shared_kv else (lambda p, i: (0, p))
    return pl.pallas_call(
        body, name=name, grid=(npairs, S // tq),
        in_specs=[pl.BlockSpec((tq, qw), lambda p, i: (i, p)),
                  pl.BlockSpec((S, qw), lambda p, i: (0, kblk(p))),
                  pl.BlockSpec((S, LANES), lambda p, i: (0, vblk(p))),
                  pl.BlockSpec((tq, LANES), lambda p, i: (i, doblk(p))),
                  pl.BlockSpec((tq, LANES), lambda p, i: (i, p)),
                  pl.BlockSpec((tq, LANES), lambda p, i: (i, p))],
        out_specs=[pl.BlockSpec((tq, qw), lambda p, i: (i, p)),
                   pl.BlockSpec((S, qw), kvo), pl.BlockSpec((S, LANES), kvo)],
        out_shape=[_sds((S, qw * npairs), F32), _sds((S, qw * nkv), F32), _sds((S, LANES * nkv), F32)],
        compiler_params=_cparams(2),
    )(q, k, v, do, o, lse)


def _bias_expand(idx, rel_bias, name):
    tq, kw = idx.shape

    def body(idx_ref, rb_ref, o_ref):
        idx = idx_ref[...]
        for hd in range(DIL_HEADS):
            acc = jnp.full((tq, kw), NEG_INF, F32)
            for u in range(REL_BUCKETS):
                acc = jnp.where(idx == u, rb_ref[u, hd], acc)
            o_ref[hd] = acc

    return pl.pallas_call(
        body, name=name,
        in_specs=[pl.BlockSpec(memory_space=pltpu.VMEM), pl.BlockSpec(memory_space=pltpu.SMEM)],
        out_specs=pl.BlockSpec(memory_space=pltpu.VMEM),
        out_shape=_sds((DIL_HEADS, tq, kw), F32),
    )(idx, rel_bias)


def _bias_reduce(idx, dtab, name):
    tq, kw = idx.shape

    def body(idx_ref, d_ref, o_ref):
        idx = idx_ref[...]
        rowid = lax.broadcasted_iota(jnp.int32, (REL_BUCKETS, kw), 0)
        for hd in range(DIL_HEADS):
            d = d_ref[hd]
            acc = jnp.zeros((REL_BUCKETS, kw), F32)
            for u in range(REL_BUCKETS):
                r = jnp.sum(jnp.where(idx == u, d, 0.0), axis=0, keepdims=True)
                acc = jnp.where(rowid == u, r, acc)
            o_ref[hd] = jnp.sum(acc, axis=1, keepdims=True)

    return pl.pallas_call(
        body, name=name,
        in_specs=[pl.BlockSpec(memory_space=pltpu.VMEM)] * 2, out_specs=pl.BlockSpec(memory_space=pltpu.VMEM),
        out_shape=_sds((DIL_HEADS, REL_BUCKETS, 1), F32),
    )(idx, dtab)


DIL_SUB = DIL_HALF
DIL_KW = DIL_SUB + 2 * DIL_HALF


def _dil_window(i, j, tq, L):
    start = pl.multiple_of(i * tq + j * DIL_SUB, DIL_SUB)
    key = start + lax.broadcasted_iota(jnp.int32, (1, DIL_KW), 1) - DIL_HALF
    return start, (key >= 0) & (key < L)


def _dil_fwd(qv, kv, vv, tab, *, dil, L, tq, name):
    npair = DIL_HEADS // 2

    def body(q_ref, k_ref, v_ref, t_ref, o_ref, lse_ref):
        masks = _pair_masks()
        for j in range(tq // DIL_SUB):
            rows = slice(j * DIL_SUB, (j + 1) * DIL_SUB)
            start, valid = _dil_window(pl.program_id(2), j, tq, L)
            kwin = k_ref[pl.ds(start, DIL_KW), :]
            vwin = v_ref[pl.ds(start, DIL_KW), :]
            q = q_ref[rows, :]
            outs, lses = [], []
            for hd in range(2):
                qh = jnp.where(masks[hd], q, jnp.zeros_like(q))
                s = lax.dot_general(qh, kwin, _NT, preferred_element_type=F32) + t_ref[hd]
                s = jnp.where(valid, s, NEG_INF)
                mx = jnp.max(s, axis=-1, keepdims=True)
                p = jnp.exp(s - mx)
                l = jnp.sum(p, axis=-1, keepdims=True)
                outs.append(jnp.dot(p.astype(CDT), vwin, preferred_element_type=F32) / l)
                lses.append(jnp.broadcast_to(mx + jnp.log(l), (DIL_SUB, LANES)))
            o_ref[rows, :] = jnp.where(masks[0], outs[0], outs[1])
            lse_ref[rows, :] = jnp.where(masks[0], lses[0], lses[1])

    blk = pl.BlockSpec((None, tq, LANES), lambda p, c, i: (c, i, p))
    res = pl.BlockSpec((None, L + 2 * DIL_HALF, LANES), lambda p, c, i: (c, 0, p))
    return pl.pallas_call(
        body, name=name, grid=(npair, dil, L // tq),
        in_specs=[blk, res, res, pl.BlockSpec((2, DIL_SUB, DIL_KW), lambda p, c, i: (p, 0, 0))],
        out_specs=[blk] * 2, out_shape=[_sds(qv.shape, F32)] * 2,
        compiler_params=_cparams(3),
    )(qv, kv, vv, tab)


def _dil_bwd(qv, kv, vv, tab, dov, lsev, deltav, *, dil, L, tq, name):
    npair = DIL_HEADS // 2

    def body(q_ref, k_ref, v_ref, t_ref, do_ref, lse_ref, dl_ref, dq_ref, dk_ref, dv_ref, dt_ref):
        masks = _pair_masks()
        c_id, i_id = pl.program_id(1), pl.program_id(2)

        @pl.when(i_id == 0)
        def _():
            dk_ref[...] = jnp.zeros_like(dk_ref)
            dv_ref[...] = jnp.zeros_like(dv_ref)

        @pl.when((i_id == 0) & (c_id == 0))
        def _():
            dt_ref[...] = jnp.zeros_like(dt_ref)

        for j in range(tq // DIL_SUB):
            rows = slice(j * DIL_SUB, (j + 1) * DIL_SUB)
            start, valid = _dil_window(i_id, j, tq, L)
            kwin = k_ref[pl.ds(start, DIL_KW), :]
            vwin = v_ref[pl.ds(start, DIL_KW), :]
            q, do, lse, dl = q_ref[rows, :], do_ref[rows, :], lse_ref[rows, :], dl_ref[rows, :]
            dq = jnp.zeros((DIL_SUB, LANES), F32)
            dk = jnp.zeros((DIL_KW, LANES), F32)
            dv = jnp.zeros((DIL_KW, LANES), F32)
            for hd in range(2):
                m = masks[hd]
                qh = jnp.where(m, q, jnp.zeros_like(q))
                doh = jnp.where(m, do, jnp.zeros_like(do))
                s = lax.dot_general(qh, kwin, _NT, preferred_element_type=F32) + t_ref[hd]
                s = jnp.where(valid, s, NEG_INF)
                p = jnp.exp(s - _head_scalar(lse, m))
                dp = lax.dot_general(doh, vwin, _NT, preferred_element_type=F32)
                ds = p * (dp - _head_scalar(dl, m))
                dt_ref[hd] += ds
                dsb = ds.astype(CDT)
                dq = dq + jnp.where(m, jnp.dot(dsb, kwin, preferred_element_type=F32), 0.0)
                dk = dk + lax.dot_general(dsb, qh, _TN, preferred_element_type=F32)
                dv = dv + lax.dot_general(p.astype(CDT), doh, _TN, preferred_element_type=F32)
            dq_ref[rows, :] = dq
            dk_ref[pl.ds(start, DIL_KW), :] += dk
            dv_ref[pl.ds(start, DIL_KW), :] += dv

    blk = pl.BlockSpec((None, tq, LANES), lambda p, c, i: (c, i, p))
    res = pl.BlockSpec((None, L + 2 * DIL_HALF, LANES), lambda p, c, i: (c, 0, p))
    tsp = pl.BlockSpec((2, DIL_SUB, DIL_KW), lambda p, c, i: (p, 0, 0))
    return pl.pallas_call(
        body, name=name, grid=(npair, dil, L // tq),
        in_specs=[blk, res, res, tsp, blk, blk, blk], out_specs=[blk, res, res, tsp],
        out_shape=[_sds(qv.shape, F32), _sds(kv.shape, F32), _sds(kv.shape, F32), _sds(tab.shape, F32)],
        compiler_params=_cparams(3),
    )(qv, kv, vv, tab, dov, lsev, deltav)


def _mix_weights(l1, l2, l3):
    mx = jnp.maximum(jnp.maximum(l1, l2), l3)
    e1, e2, e3 = jnp.exp(l1 - mx), jnp.exp(l2 - mx), jnp.exp(l3 - mx)
    inv = 1.0 / (e1 + e2 + e3)
    return e1 * inv, e2 * inv, e3 * inv


def _branch_specs(S, ts):
    dils = [d for _, d in DIL_BRANCHES]
    return dils, [_res_spec(d, ts // d) for d in dils], [(d, S // d, DIL_W) for d in dils]


def _dil_mix_fwd(os, ls):
    S = os[0].shape[0] * os[0].shape[1]
    ts = min(256, S)
    dils, specs, _ = _branch_specs(S, ts)

    def body(o1, o2, o3, l1, l2, l3, out, scr):
        o1, o2, o3, l1, l2, l3 = [_by_token(r, scr, d) for r, d in zip((o1, o2, o3, l1, l2, l3), dils + dils)]
        w1, w2, w3 = _mix_weights(l1, l2, l3)
        out[...] = (w1 * o1 + w2 * o2 + w3 * o3).astype(CDT)

    return pl.pallas_call(
        body, name="dil_mix_fwd", grid=(S // ts,), in_specs=specs + specs, out_specs=_row(ts, DIL_W),
        out_shape=_sds((S, DIL_W), CDT), scratch_shapes=[_TOKEN_SCRATCH(ts)],
        compiler_params=_cparams(1),
    )(*os, *ls)


def _dil_mix_bwd(dcat, os, ls, j384):
    S = os[0].shape[0] * os[0].shape[1]
    ts = min(256, S)
    dils, specs, shapes = _branch_specs(S, ts)

    def body(do_ref, o1, o2, o3, l1, l2, l3, j_ref, d1, d2, d3, e1, e2, e3, scr):
        o1, o2, o3, l1, l2, l3 = [_by_token(r, scr, d) for r, d in zip((o1, o2, o3, l1, l2, l3), dils + dils)]
        ws = _mix_weights(l1, l2, l3)
        do = do_ref[...]
        o = ws[0] * o1 + ws[1] * o2 + ws[2] * o3
        dot = _headsum(do * o, j_ref[...])
        for w, d, d_o, e_o in zip(ws, dils, (d1, d2, d3), (e1, e2, e3)):
            _by_residue(w * do, scr, d_o, d)
            _by_residue(w * dot, scr, e_o, d)

    return pl.pallas_call(
        body, name="dil_mix_bwd", grid=(S // ts,),
        in_specs=[_row(ts, DIL_W, 1)] + specs + specs + [_full(j384.shape)],
        out_specs=specs + specs,
        out_shape=[_sds(s, CDT) for s in shapes] + [_sds(s, F32) for s in shapes],
        scratch_shapes=[_TOKEN_SCRATCH(ts)],
        compiler_params=_cparams(1),
    )(dcat, *os, *ls, j384)


def _adamw_math(w, g, m, v):
    m = ADAM_B1 * m + (1.0 - ADAM_B1) * g
    v = ADAM_B2 * v + (1.0 - ADAM_B2) * (g * g)
    m_hat = m / (1.0 - ADAM_B1 ** ADAM_STEP)
    v_hat = v / (1.0 - ADAM_B2 ** ADAM_STEP)
    delta = -ADAM_LR * (m_hat / (jnp.sqrt(v_hat) + ADAM_EPS) + ADAM_WD * w)
    return delta, m, v


def _pick8(n, target):
    best = None
    for t in range(16, min(n, target) + 1, 16):
        if n % t == 0:
            best = t
    return best if best is not None else n


_ELEMS_PER_BLOCK = 256 * 1024


def _lead_spec(a, b):
    ta = _pick8(a, max(16, _ELEMS_PER_BLOCK // b))
    return ta, pl.BlockSpec((None, ta, b), lambda l, i: (l, i, 0))


def _adamw(w, reds, sibs, m, v, owner, name):
    L, a, b = w.shape
    ta, spec = _lead_spec(a, b)
    gspec = pl.BlockSpec((ta, b), lambda l, i: (i, 0))

    def body(w_ref, r0_ref, r1_ref, s0_ref, s1_ref, m_ref, v_ref, g_o, d_o, m_o, v_o):
        mine = lax.axis_index("c") == owner
        g0 = jnp.where(mine, r0_ref[...], s0_ref[...])
        g1 = jnp.where(mine, r1_ref[...], s1_ref[...])
        g = jnp.where(pl.program_id(0) == 0, g0, g1)
        d, mm, vv = _adamw_math(w_ref[...], g, m_ref[...], v_ref[...])
        g_o[...] = g
        d_o[...] = d
        m_o[...] = mm
        v_o[...] = vv

    return pl.pallas_call(
        body, name=name, grid=(L, a // ta), in_specs=[spec] + [gspec] * 4 + [spec, spec], out_specs=[spec] * 4,
        out_shape=[_sds(w.shape, F32)] * 4, compiler_params=_cparams(2),
    )(w, *reds, *sibs, m, v)


def _adamw_small(w, gall, m, v):
    R = w.shape[0]

    def body(w_ref, g_ref, m_ref, v_ref, g_o, d_o, m_o, v_o):
        g = g_ref[0]
        for k in range(1, 8):
            g = g + g_ref[k]
        d, mm, vv = _adamw_math(w_ref[...], g, m_ref[...], v_ref[...])
        g_o[...] = g
        d_o[...] = d
        m_o[...] = mm
        v_o[...] = vv

    vm = pl.BlockSpec(memory_space=pltpu.VMEM)
    return pl.pallas_call(
        body, name="adamw_small", in_specs=[vm] * 4, out_specs=[vm] * 4, out_shape=[_sds((R, LANES), F32)] * 4,
    )(w, gall, m, v)


def _sum_pair(g, t, owner, name):
    n, a, b = t.shape
    ta, spec = _lead_spec(a, b)

    def body(g_ref, t_ref, o_ref):
        @pl.when(lax.axis_index("c") == owner)
        def _():
            o_ref[...] = (g_ref[...] + t_ref[...]).astype(o_ref.dtype)

    return pl.pallas_call(
        body, name=name, grid=(n, a // ta), in_specs=[spec] * 2, out_specs=spec, out_shape=_sds(t.shape, WIRE),
        compiler_params=_cparams(2),
    )(g, t)


def _sum_chips(pair, t, owner, name):
    _, a, b = t.shape
    ta = _pick8(a, max(16, _ELEMS_PER_BLOCK // b))

    def body(p_ref, t_ref, o_ref):
        @pl.when(lax.axis_index("c") == owner)
        def _():
            me = 2 * lax.axis_index("x") + lax.axis_index("y")
            acc = p_ref[me].astype(F32)
            for k in range(3):
                acc = acc + t_ref[k].astype(F32)
            o_ref[...] = acc

    return pl.pallas_call(
        body, name=name, grid=(a // ta,),
        in_specs=[pl.BlockSpec((4, ta, b), lambda i: (0, i, 0)), pl.BlockSpec((3, ta, b), lambda i: (0, i, 0))],
        out_specs=pl.BlockSpec((ta, b), lambda i: (i, 0)), out_shape=_sds((a, b), F32), compiler_params=_cparams(1),
    )(pair, t)


_HBM = pl.BlockSpec(memory_space=pltpu.HBM)


def _place():
    x, y, c = lax.axis_index("x"), lax.axis_index("y"), lax.axis_index("c")
    chips = [(1 - x, y), (x, 1 - y), (1 - x, 1 - y)]
    return x, y, c, chips


def _remote(src, dst, ssem, rsem, to):
    return pltpu.make_async_remote_copy(src_ref=src, dst_ref=dst, send_sem=ssem, recv_sem=rsem, device_id=to,
                                        device_id_type=MESH_ID)


def _dma_sems(n):
    return pltpu.SemaphoreType.DMA((n,))


_SEM = pl.BlockSpec(memory_space=pltpu.SEMAPHORE)
_ANY = pl.BlockSpec(memory_space=pl.ANY)
_EFFECT = pltpu.SideEffectType.DATAFLOW_SIDE_EFFECTING
_BIG = ("w_in", "mla_w_uq", "mla_w_ukv", "w_out", "ffn_w_gate", "ffn_w_up", "ffn_w_down")
_OWNER = dict(zip(_BIG, (1, 0, 0, 1, 0, 0, 1)))
_ATTN_WEIGHTS, _FFN_WEIGHTS = _BIG[:4], _BIG[4:]


def _hbm(a):
    return pltpu.with_memory_space_constraint(a, pltpu.HBM)


def _per_core(c, owners, fn):
    for g in range(2):
        mine = tuple(p for p, o in enumerate(owners) if o == g)
        theirs = tuple(p for p, o in enumerate(owners) if o != g)
        pl.when(c == g)(functools.partial(fn, mine, theirs))


def _token_spec():
    return pl.BlockSpec(memory_space=pltpu.VMEM), _sds((8, LANES), F32)


def _gather_start(shards, owners, layer, tag):
    n = len(shards)
    lands = [_hbm(lax.empty((4,) + s.shape[1:], s.dtype)) for s in shards]

    def body(*refs):
        w_refs, l_refs = refs[:n], refs[n:2 * n]
        ssem, rsem, token = refs[2 * n], refs[2 * n + 1], refs[-1]
        x, y, c, chips = _place()
        me = 2 * x + y

        def send(mine, _):
            for i in mine:
                for j, (cx, cy) in enumerate(chips):
                    _remote(w_refs[i].at[layer], l_refs[i].at[me], ssem.at[3 * i + j], rsem.at[3 * i + j],
                            (cx, cy, c)).start()

        _per_core(c, owners, send)
        token[...] = jnp.zeros_like(token)

    tspec, tshape = _token_spec()
    out = pl.pallas_call(
        body, name=f"gather_start_{tag}", in_specs=[_HBM] * (2 * n),
        out_specs=[_SEM, _SEM] + [_HBM] * n + [tspec],
        out_shape=[_dma_sems(3 * n), _dma_sems(3 * n)] + [pltpu.HBM(l.shape, l.dtype) for l in lands] + [tshape],
        input_output_aliases={n + i: 2 + i for i in range(n)},
        compiler_params=pltpu.CompilerParams(has_side_effects=_EFFECT),
    )(*[_hbm(s) for s in shards], *lands)
    return out[0], out[1], list(out[2:2 + n]), out[-1]


def _gather_wait(ssem, rsem, shards, lands, after, owners, layer, tag):
    n = len(shards)

    def body(*refs):
        w_refs, l_refs = refs[:n], refs[n:2 * n]
        ssem, rsem = refs[2 * n], refs[2 * n + 1]
        x, y, c, chips = _place()

        def wait(mine, _):
            for i in mine:
                for j, (cx, cy) in enumerate(chips):
                    cp = _remote(w_refs[i].at[layer], l_refs[i].at[2 * cx + cy], ssem.at[3 * i + j], rsem.at[3 * i + j],
                                 (cx, cy, c))
                    cp.wait_send()
                    cp.wait_recv()

        _per_core(c, owners, wait)

    return list(pl.pallas_call(
        body, name=f"gather_wait_{tag}", in_specs=[_HBM] * (2 * n) + [_SEM, _SEM, _ANY], out_specs=[_HBM] * n,
        out_shape=[pltpu.HBM(l.shape, l.dtype) for l in lands],
        input_output_aliases={n + i: i for i in range(n)},
        compiler_params=pltpu.CompilerParams(has_side_effects=_EFFECT),
    )(*[_hbm(s) for s in shards], *lands, ssem, rsem, after))


def _gather_finish(shards, lands, owners, layer, tag):
    n = len(shards)

    def body(*refs):
        w_refs, g_refs = refs[:n], refs[2 * n:3 * n]
        ssem, rsem = refs[3 * n:]
        x, y, c, chips = _place()
        me = 2 * x + y
        sib = (x, y, 1 - c)
        owns = [_remote(w.at[layer], g.at[me], ssem.at[i], rsem.at[i], sib) for i, (w, g) in enumerate(zip(w_refs, g_refs))]
        for cp in owns:
            cp.start()

        def forward(mine, theirs):
            def blk(i, j):
                b = g_refs[i].at[2 * chips[j][0] + chips[j][1]]
                return _remote(b, b, ssem.at[n + 3 * i + j], rsem.at[n + 3 * i + j], sib)

            for i in mine:
                for j in range(3):
                    blk(i, j).start()
            for i in theirs:
                for j in range(3):
                    blk(i, j).wait_recv()
            for i in mine:
                for j in range(3):
                    blk(i, j).wait_send()

        _per_core(c, owners, forward)
        for cp in owns:
            cp.wait_recv()
            cp.wait_send()

    return list(pl.pallas_call(
        body, name=f"gather_finish_{tag}", in_specs=[_HBM] * (2 * n), out_specs=[_HBM] * n,
        out_shape=[_sds(l.shape, l.dtype) for l in lands], input_output_aliases={n + i: i for i in range(n)},
        scratch_shapes=[_dma_sems(4 * n), _dma_sems(4 * n)],
    )(*shards, *lands))


def _rs_to_owner(grads, owners, tag):
    n = len(grads)

    def body(*refs):
        g_refs, t_refs = refs[:n], refs[n:2 * n]
        ssem, rsem = refs[2 * n:]
        x, y, c, _ = _place()

        def swap(mine, theirs):
            cps = [_remote(g_refs[i], t_refs[i], ssem.at[i], rsem.at[i], (x, y, 1 - c)) for i in theirs]
            for cp in cps:
                cp.start()
            for i in mine:
                _remote(g_refs[i], t_refs[i], ssem.at[i], rsem.at[i], (x, y, 1 - c)).wait_recv()
            for cp in cps:
                cp.wait_send()

        _per_core(c, owners, swap)

    return list(pl.pallas_call(
        body, name=f"rs_to_owner_{tag}", in_specs=[_HBM] * n, out_specs=[_HBM] * n,
        out_shape=[_sds(g.shape, g.dtype) for g in grads], scratch_shapes=[_dma_sems(n), _dma_sems(n)],
    )(*grads))


def _a2a_start(pairs, owners, tag):
    n = len(pairs)
    lands = [_hbm(lax.empty((3,) + p.shape[1:], p.dtype)) for p in pairs]

    def body(*refs):
        a_refs, t_refs = refs[:n], refs[n:2 * n]
        ssem, rsem, token = refs[2 * n], refs[2 * n + 1], refs[-1]
        x, y, c, chips = _place()

        def send(mine, _):
            for i in mine:
                for j, (cx, cy) in enumerate(chips):
                    _remote(a_refs[i].at[2 * cx + cy], t_refs[i].at[j], ssem.at[3 * i + j], rsem.at[3 * i + j],
                            (cx, cy, c)).start()

        _per_core(c, owners, send)
        token[...] = jnp.zeros_like(token)

    tspec, tshape = _token_spec()
    out = pl.pallas_call(
        body, name=f"rs_a2a_start_{tag}", in_specs=[_HBM] * (2 * n),
        out_specs=[_SEM, _SEM] + [_HBM] * n + [tspec],
        out_shape=[_dma_sems(3 * n), _dma_sems(3 * n)] + [pltpu.HBM(l.shape, l.dtype) for l in lands] + [tshape],
        input_output_aliases={n + i: 2 + i for i in range(n)},
        compiler_params=pltpu.CompilerParams(has_side_effects=_EFFECT),
    )(*[_hbm(p) for p in pairs], *lands)
    return out[0], out[1], list(out[2:2 + n]), out[-1]


def _a2a_wait(ssem, rsem, pairs, lands, after, owners, tag):
    n = len(pairs)

    def body(*refs):
        a_refs, t_refs = refs[:n], refs[n:2 * n]
        ssem, rsem = refs[2 * n], refs[2 * n + 1]
        x, y, c, chips = _place()

        def wait(mine, _):
            for i in mine:
                for j, (cx, cy) in enumerate(chips):
                    cp = _remote(a_refs[i].at[2 * cx + cy], t_refs[i].at[j], ssem.at[3 * i + j], rsem.at[3 * i + j],
                                 (cx, cy, c))
                    cp.wait_send()
                    cp.wait_recv()

        _per_core(c, owners, wait)

    return list(pl.pallas_call(
        body, name=f"rs_a2a_wait_{tag}", in_specs=[_HBM] * (2 * n) + [_SEM, _SEM, _ANY], out_specs=[_HBM] * n,
        out_shape=[pltpu.HBM(l.shape, l.dtype) for l in lands],
        input_output_aliases={n + i: i for i in range(n)},
        compiler_params=pltpu.CompilerParams(has_side_effects=_EFFECT),
    )(*[_hbm(p) for p in pairs], *lands, ssem, rsem, after))


def _rs_from_owner(reds, owners):
    n = len(reds)

    def body(*refs):
        q_refs, o_refs = refs[:n], refs[n:2 * n]
        ssem, rsem = refs[2 * n:]
        x, y, c, _ = _place()

        def swap(mine, theirs):
            cps = [_remote(q_refs[k], o_refs[k], ssem.at[k], rsem.at[k], (x, y, 1 - c)) for k in mine]
            for cp in cps:
                cp.start()
            for k in theirs:
                _remote(q_refs[k], o_refs[k], ssem.at[k], rsem.at[k], (x, y, 1 - c)).wait_recv()
            for cp in cps:
                cp.wait_send()

        _per_core(c, owners, swap)

    return list(pl.pallas_call(
        body, name="rs_from_owner", in_specs=[_HBM] * n, out_specs=[_HBM] * n,
        out_shape=[_sds(q.shape, q.dtype) for q in reds], scratch_shapes=[_dma_sems(n), _dma_sems(n)],
    )(*reds))


def _gather_weights(shards):
    n = len(shards)

    def body(*refs):
        w_refs, g_refs = refs[:n], refs[n:2 * n]
        ssem, rsem = refs[2 * n:]
        x, y, c, chips = _place()
        me = 2 * x + y
        sib = (x, y, 1 - c)
        owns = [_remote(w.at[l], g.at[l, me], ssem.at[6 * n + 2 * i + l], rsem.at[6 * n + 2 * i + l], sib)
                for i, (w, g) in enumerate(zip(w_refs, g_refs)) for l in range(2)]
        first = [_remote(w.at[c], g.at[c, me], ssem.at[3 * i + j], rsem.at[3 * i + j], (cx, cy, c))
                 for j, (cx, cy) in enumerate(chips) for i, (w, g) in enumerate(zip(w_refs, g_refs))]
        for cp in first + owns:
            cp.start()
        passed = []
        for j, (cx, cy) in enumerate(chips):
            for i, g in enumerate(g_refs):
                blk = g.at[c, 2 * cx + cy]
                _remote(blk, blk, ssem.at[3 * i + j], rsem.at[3 * i + j], sib).wait_recv()
                fw = _remote(blk, blk, ssem.at[3 * n + 3 * i + j], rsem.at[3 * n + 3 * i + j], sib)
                fw.start()
                passed.append(fw)
        for j, (cx, cy) in enumerate(chips):
            for i, g in enumerate(g_refs):
                blk = g.at[1 - c, 2 * cx + cy]
                _remote(blk, blk, ssem.at[3 * n + 3 * i + j], rsem.at[3 * n + 3 * i + j], sib).wait_recv()
        for cp in owns:
            cp.wait_recv()
        for cp in first + passed + owns:
            cp.wait_send()

    return pl.pallas_call(
        body, name="gather_weights", in_specs=[_HBM] * n, out_specs=[_HBM] * n,
        out_shape=[_sds((2, 4) + s.shape[1:], s.dtype) for s in shards],
        scratch_shapes=[_dma_sems(8 * n), _dma_sems(8 * n)],
    )(*shards)


def _sibling_halves(g0s, g1s):
    n = len(g0s)

    def body(*refs):
        g0_refs, g1_refs, t_refs = refs[:n], refs[n:2 * n], refs[2 * n:3 * n]
        ssem, rsem = refs[3 * n:]
        x, y, c, _ = _place()

        def swap(srcs):
            cps = [_remote(s, t, ssem.at[i], rsem.at[i], (x, y, 1 - c)) for i, (s, t) in enumerate(zip(srcs, t_refs))]
            for cp in cps:
                cp.start()
            for cp in cps:
                cp.wait()

        @pl.when(c == 0)
        def _():
            swap(g1_refs)

        @pl.when(c == 1)
        def _():
            swap(g0_refs)

    return pl.pallas_call(
        body, name="rs_sibling_halves", in_specs=[_HBM] * (2 * n), out_specs=[_HBM] * n,
        out_shape=[_sds(g.shape, g.dtype) for g in g0s], scratch_shapes=[_dma_sems(n), _dma_sems(n)],
    )(*g0s, *g1s)


def _chip_all_to_all(parts):
    n = len(parts)

    def body(*refs):
        a_refs, t_refs = refs[:n], refs[n:2 * n]
        ssem, rsem = refs[2 * n:]
        x, y, c, chips = _place()
        sends = [_remote(a.at[2 * cx + cy], t.at[j], ssem.at[3 * i + j], rsem.at[3 * i + j], (cx, cy, c))
                 for j, (cx, cy) in enumerate(chips) for i, (a, t) in enumerate(zip(a_refs, t_refs))]
        for cp in sends:
            cp.start()
        for cp in sends:
            cp.wait_recv()
        for cp in sends:
            cp.wait_send()

    return pl.pallas_call(
        body, name="rs_chip_all_to_all", in_specs=[_HBM] * n, out_specs=[_HBM] * n,
        out_shape=[_sds((3,) + p.shape[1:], p.dtype) for p in parts],
        scratch_shapes=[_dma_sems(3 * n), _dma_sems(3 * n)],
    )(*parts)


def _sibling_swap(reds):
    n = len(reds)

    def body(*refs):
        q_refs, o_refs = refs[:n], refs[n:2 * n]
        ssem, rsem = refs[2 * n:]
        x, y, c, _ = _place()
        cps = [_remote(q, o, ssem.at[i], rsem.at[i], (x, y, 1 - c)) for i, (q, o) in enumerate(zip(q_refs, o_refs))]
        for cp in cps:
            cp.start()
        for cp in cps:
            cp.wait()

    return pl.pallas_call(
        body, name="rs_sibling_swap", in_specs=[_HBM] * n, out_specs=[_HBM] * n,
        out_shape=[_sds(q.shape, q.dtype) for q in reds],
        scratch_shapes=[_dma_sems(n), _dma_sems(n)],
    )(*reds)


def _gather_small(s):
    R, _ = s.shape

    def body(s_ref, o_ref, ssem, rsem, lsem):
        x, y, c, _ = _place()
        me = 4 * x + 2 * y + c
        own = pltpu.make_async_copy(s_ref, o_ref.at[me], lsem)
        own.start()
        sends = []
        for k in range(1, 8):
            px, py, pc = x ^ (k >> 2), y ^ ((k >> 1) & 1), c ^ (k & 1)
            cp = _remote(s_ref, o_ref.at[me], ssem.at[k - 1], rsem.at[k - 1], (px, py, pc))
            cp.start()
            sends.append(cp)
        for k in range(1, 8):
            px, py, pc = x ^ (k >> 2), y ^ ((k >> 1) & 1), c ^ (k & 1)
            blk = o_ref.at[4 * px + 2 * py + pc]
            _remote(blk, blk, ssem.at[k - 1], rsem.at[k - 1], (px, py, pc)).wait_recv()
        for cp in sends:
            cp.wait_send()
        own.wait()

    vm = pl.BlockSpec(memory_space=pltpu.VMEM)
    return pl.pallas_call(
        body, name="gather_small", in_specs=[vm], out_specs=vm, out_shape=_sds((8, R, LANES), s.dtype),
        scratch_shapes=[pltpu.SemaphoreType.DMA((7,)), pltpu.SemaphoreType.DMA((7,)), pltpu.SemaphoreType.DMA],
    )(s)


_COL_SHARDED =("w_in", "mla_w_uq", "mla_w_ukv", "ffn_w_gate", "ffn_w_up")
_SMALL = ("mla_q_norm", "mla_kv_norm", "gqa_q_norm", "gqa_k_norm", "rel_bias", "ln1_g", "ln1_b", "ln2_g", "ln2_b")


def _pack_flat(arrs, align):
    flat = jnp.concatenate([a.reshape(-1) for a in arrs])
    pad = (-flat.shape[0]) % align
    return jnp.pad(flat, (0, pad)) if pad else flat


def _unpack_flat(flat, shapes):
    out, off = [], 0
    for s in shapes:
        n = int(np.prod(s))
        out.append(flat[off:off + n].reshape(s))
        off += n
    return out


def _perm_gqa_rows(w):
    return jnp.concatenate([w[:832], w[896:960], w[832:896], w[960:]], axis=0)


def _local_step(x, target, small, depth, weights_of_layer, grads_done):
    S, D = x.shape
    alpha = (2.0 * depth) ** 0.25
    in_idx, uq_idx, ukv_idx = _in_cols(), _uq_cols(), _ukv_cols()
    win, wuq, wukv, wout, wg, wu, wdn = ([None] * depth for _ in range(7))

    tm, tg = _rope_tables(S)
    j256, j384 = _head_ones(256), _head_ones(384)
    mla_scale = (64 + MLA_ROPE_DIM) ** -0.5
    branches = []
    for (_, dil) in DIL_BRANCHES:
        L = S // dil
        tq = min(256, L)
        idx = jnp.asarray(_branch_bucket_idx(DIL_SUB, dil))
        branches.append((dil, L, tq, idx))
    tabs = [_bias_expand(idx, small["rel_bias"], name=f"bias_expand_{b}") for b, (_, _, _, idx) in enumerate(branches)]

    def padded(a):
        z = jnp.zeros((DIL_HALF, a.shape[1]), a.dtype)
        return jnp.concatenate([z, a, z], axis=0)[None]

    saved = []
    xf, xb = x, x.astype(CDT)
    for l in range(depth):
        W, token = weights_of_layer(l, "attn", xb)
        win[l] = _rows_from_shards(W["w_in"], in_idx)
        wuq[l] = _rows_from_shards(W["mla_w_uq"], uq_idx)
        wukv[l] = _rows_from_shards(W["mla_w_ukv"], ukv_idx)
        wout[l] = _perm_gqa_rows(W["w_out"].reshape(-1, D))
        gq, gkv = small["mla_q_norm"][l][None], small["mla_kv_norm"][l][None]
        if token is not None:
            gq = gq + token[0, 0]
        ggq = jnp.tile(small["gqa_q_norm"][l], 4)[None]
        ggk = jnp.tile(small["gqa_k_norm"][l], 2)[None]
        h = _mm(xb, win[l], tb=True, name="mm_in")
        cq, ckv, kr, qd, kd, vd, qg, kg, vg, *strided = _prep_fwd(h, gq, gkv, ggq, ggk, tm, tg, j256)
        qkv = [(qd[None], padded(kd), padded(vd))] + [tuple(strided[3 * b:3 * b + 3]) for b in range(len(DIL_STRIDES))]
        qa = _mm(cq, wuq[l], tb=True, name="mm_uq")
        kvp = _mm(ckv, wukv[l], tb=True, out_dtype=CDT, name="mm_ukv")
        qm, km = _mla_prep_fwd(qa, kvp, kr, tm, mla_scale)
        oa, lsa = _attn_fwd(qm, km, kvp, split=True, npairs=3, kblk=lambda p: p, vblk=lambda p: 6 + p,
                            name="mla_attn_fwd")
        oc, lsc = _attn_fwd(qg, kg, vg, split=False, npairs=2, kblk=lambda p: 0, vblk=lambda p: 0,
                            name="gqa_attn_fwd")
        obs, lbs = [], []
        for b, (dil, L, tq, _) in enumerate(branches):
            o_b, l_b = _dil_fwd(*qkv[b], tabs[b], dil=dil, L=L, tq=tq, name=f"dil_fwd_{b}")
            obs.append(o_b)
            lbs.append(l_b)
        ob = _dil_mix_fwd(obs, lbs)
        cat = jnp.concatenate([oa, ob, oc], axis=1)
        mix = _mm(cat, wout[l], name="mm_out")
        x1, x1b, z1 = _ln_fwd(xf, mix, small["ln1_g"][l][None], small["ln1_b"][l][None], alpha, name="ln1_fwd")
        W, _ = weights_of_layer(l, "ffn", x1b)
        wg[l], wu[l], wdn[l] = W["ffn_w_gate"], W["ffn_w_up"], W["ffn_w_down"]
        g3, u3, act = _ffn_up(x1b, wg[l], wu[l])
        ff = _mm(act, wdn[l], ga=True, gb=True, name="mm_down")
        x2, x2b, z2 = _ln_fwd(x1, ff, small["ln2_g"][l][None], small["ln2_b"][l][None], alpha, name="ln2_fwd")
        saved.append(dict(xb=xb, h=h, cq=cq, ckv=ckv, qg=qg, kg=kg, vg=vg, kvp=kvp, qm=qm, km=km, oa=oa, lsa=lsa,
                          oc=oc, lsc=lsc, obs=obs, lbs=lbs, qkv=qkv, cat=cat, z1=z1, x1b=x1b, g3=g3, u3=u3, act=act, z2=z2,
                          gq=gq, gkv=gkv, ggq=ggq, ggk=ggk))
        xf, xb = x2, x2b

    dy, loss = _loss_kernel(xf, target)

    gW = {k: [None] * depth for k in _BIG}
    gS = {k: [None] * depth for k in ("mla_q_norm", "mla_kv_norm", "gqa_q_norm", "gqa_k_norm", "ln1_g", "ln1_b", "ln2_g",
                                      "ln2_b")}
    g_rel = None
    dya, dyb = dy, None
    token = None
    for l in reversed(range(depth)):
        sv = saved[l]
        ln2_g = small["ln2_g"][l][None]
        if token is not None:
            ln2_g = ln2_g + token[0, 0]
        dz2, dz2b, gS["ln2_g"][l], gS["ln2_b"][l] = _ln_bwd(dya, dyb, sv["z2"], ln2_g, alpha,
                                                             name="ln2_bwd" if dyb is not None else "ln2_bwd_last")
        gW["ffn_w_down"][l] = _mm(sv["act"], dz2b, ta=True, ga=True, go=True, name="mm_down_dw")
        dg3, du3 = _ffn_down_dx(dz2b, wdn[l], sv["g3"], sv["u3"])
        gW["ffn_w_gate"][l] = _mm(dg3, sv["x1b"], ta=True, ga=True, go=True, name="mm_gate_dw")
        gW["ffn_w_up"][l] = _mm(du3, sv["x1b"], ta=True, ga=True, go=True, name="mm_up_dw")
        dx1 = _ffn_up_dx(dg3, du3, wg[l], wu[l])
        token = grads_done(l, "ffn", {n: gW[n][l] for n in _FFN_WEIGHTS})
        ln1_g = small["ln1_g"][l][None]
        if token is not None:
            ln1_g = ln1_g + token[0, 0]
        dz1, dz1b, gS["ln1_g"][l], gS["ln1_b"][l] = _ln_bwd(dx1, dz2, sv["z1"], ln1_g, alpha, name="ln1_bwd")
        gW["w_out"][l] = _perm_gqa_rows(_mm(sv["cat"], dz1b, ta=True, name="mm_out_dw")).reshape(4, -1, D)
        dcat = _mm(dz1b, wout[l], tb=True, name="mm_out_dx")
        dqg, dkg, dvg = _attn_bwd(sv["qg"], sv["kg"], sv["vg"], dcat, sv["oc"], sv["lsc"], split=False, npairs=2,
                                  kblk=lambda p: 0, vblk=lambda p: 0, doblk=lambda p: 6 + p, shared_kv=True,
                                  name="gqa_attn_bwd")
        dqm, dkm, dvm = _attn_bwd(sv["qm"], sv["km"], sv["kvp"], dcat, sv["oa"], sv["lsa"], split=True, npairs=3,
                                  kblk=lambda p: p, vblk=lambda p: 6 + p, doblk=lambda p: p, shared_kv=False,
                                  name="mla_attn_bwd")
        dqa, dkvp, dkr = _mla_prep_bwd(dqm, dkm, dvm, tm, mla_scale)
        gW["mla_w_uq"][l] = _rows_to_shards(_mm(dqa, sv["cq"], ta=True, name="mm_uq_dw"), uq_idx, MLA_HEADS * 96)
        dcq = _mm(dqa, wuq[l], name="mm_uq_dx")
        gW["mla_w_ukv"][l] = _rows_to_shards(_mm(dkvp, sv["ckv"], ta=True, name="mm_ukv_dw"), ukv_idx, MLA_HEADS * 128)
        dckv = _mm(dkvp, wukv[l], name="mm_ukv_dx")
        mixb = _dil_mix_bwd(dcat, sv["obs"], sv["lbs"], j384)
        ddq, ddk, ddv = [], [], []
        for b, (dil, L, tq, idx) in enumerate(branches):
            dq_b, dk_b, dv_b, dtab = _dil_bwd(*sv["qkv"][b], tabs[b], mixb[b], sv["lbs"][b], mixb[3 + b], dil=dil, L=L,
                                              tq=tq, name=f"dil_bwd_{b}")
            if dil == 1:
                dq_b, dk_b, dv_b = dq_b[0], dk_b[0, DIL_HALF:DIL_HALF + S], dv_b[0, DIL_HALF:DIL_HALF + S]
            ddq.append(dq_b)
            ddk.append(dk_b)
            ddv.append(dv_b)
            g_b = _bias_reduce(idx, dtab, name=f"bias_reduce_{b}")[:, :, 0].T
            g_rel = g_b if g_rel is None else g_rel + g_b
        dh, n1, n2, n3, n4 = _prep_bwd(sv["h"], dcq, dckv, dkr, ddq, ddk, ddv, dqg, dkg, dvg, sv["gq"], sv["gkv"],
                                       sv["ggq"], sv["ggk"], tg, j256)
        gS["mla_q_norm"][l], gS["mla_kv_norm"][l] = n1[0], n2[0]
        gS["gqa_q_norm"][l] = n3[0].reshape(4, 64).sum(0)
        gS["gqa_k_norm"][l] = n4[0].reshape(2, 64).sum(0)
        gW["w_in"][l] = _rows_to_shards(_mm(dh, sv["xb"], ta=True, name="mm_in_dw"), in_idx, IN_W)
        dya = _mm(dh, win[l], name="mm_in_dx")
        dyb = dz1
        token = grads_done(l, "attn", {n: gW[n][l] for n in _ATTN_WEIGHTS})
    grad_x = _axpy(dya, dyb, alpha, name="grad_x")

    gsmall = {k: jnp.stack([a.reshape(-1) for a in v]) for k, v in gS.items()}
    gsmall["rel_bias"] = g_rel
    return loss, grad_x, gsmall


_ORDER = ("w_in", "mla_q_norm", "mla_kv_norm", "mla_w_uq", "mla_w_ukv", "gqa_q_norm", "gqa_k_norm", "rel_bias", "w_out",
          "ln1_g", "ln1_b", "ffn_w_gate", "ffn_w_up", "ffn_w_down", "ln2_g", "ln2_b")


def kernel(x, w_in, mla_q_norm, mla_kv_norm, mla_w_uq, mla_w_ukv, gqa_q_norm, gqa_k_norm, rel_bias, w_out, ln1_g, ln1_b, ffn_w_gate, ffn_w_up, ffn_w_down, ln2_g, ln2_b, loss_target, m_w_in, m_mla_q_norm, m_mla_kv_norm, m_mla_w_uq, m_mla_w_ukv, m_gqa_q_norm, m_gqa_k_norm, m_rel_bias, m_w_out, m_ln1_g, m_ln1_b, m_ffn_w_gate, m_ffn_w_up, m_ffn_w_down, m_ln2_g, m_ln2_b, v_w_in, v_mla_q_norm, v_mla_kv_norm, v_mla_w_uq, v_mla_w_ukv, v_gqa_q_norm, v_gqa_k_norm, v_rel_bias, v_w_out, v_ln1_g, v_ln1_b, v_ffn_w_gate, v_ffn_w_up, v_ffn_w_down, v_ln2_g, v_ln2_b):
    wts = dict(zip(_ORDER, (w_in, mla_q_norm, mla_kv_norm, mla_w_uq, mla_w_ukv, gqa_q_norm, gqa_k_norm, rel_bias, w_out,
                            ln1_g, ln1_b, ffn_w_gate, ffn_w_up, ffn_w_down, ln2_g, ln2_b)))
    mom = dict(zip(_ORDER, (m_w_in, m_mla_q_norm, m_mla_kv_norm, m_mla_w_uq, m_mla_w_ukv, m_gqa_q_norm, m_gqa_k_norm,
                            m_rel_bias, m_w_out, m_ln1_g, m_ln1_b, m_ffn_w_gate, m_ffn_w_up, m_ffn_w_down, m_ln2_g,
                            m_ln2_b)))
    var = dict(zip(_ORDER, (v_w_in, v_mla_q_norm, v_mla_kv_norm, v_mla_w_uq, v_mla_w_ukv, v_gqa_q_norm, v_gqa_k_norm,
                            v_rel_bias, v_w_out, v_ln1_g, v_ln1_b, v_ffn_w_gate, v_ffn_w_up, v_ffn_w_down, v_ln2_g,
                            v_ln2_b)))
    small_shapes = [wts[n].shape for n in _SMALL]
    for d in (wts, mom, var):
        for n in _COL_SHARDED:
            d[n] = d[n].transpose(0, 2, 1)

    depth = 2
    shards = {n: wts[n].astype(WIRE) for n in _BIG}
    flying = {}

    def start_gather(names, l, tag):
        own = tuple(_OWNER[n] for n in names)
        sh = [shards[n] for n in names]
        ssem, rsem, lands, token = _gather_start(sh, own, l, tag)
        return (names, own, sh, ssem, rsem, lands, l, tag), token

    def end_gather(flight, after):
        names, own, sh, ssem, rsem, lands, l, tag = flight
        got = _gather_finish(sh, _gather_wait(ssem, rsem, sh, lands, after, own, l, tag), own, l, tag)
        return {n: g.astype(CDT) for n, g in zip(names, got)}

    def weights_of_layer(l, part, after):
        if (l, part) == (0, "attn"):
            got = end_gather(start_gather(_ATTN_WEIGHTS, 0, "attn0")[0], after)
            flying["ffn0"], t0 = start_gather(_FFN_WEIGHTS, 0, "ffn0")
            flying["layer1"], t1 = start_gather(_BIG, 1, "layer1")
            return got, t0 + t1
        if (l, part) == (0, "ffn"):
            return end_gather(flying.pop("ffn0"), after), None
        if part == "attn":
            flying["w1"] = end_gather(flying.pop("layer1"), after)
        return flying["w1"], None

    def grads_done(l, part, grads):
        names = tuple(grads)
        own = tuple(_OWNER[n] for n in names)
        tag = f"{part}{l}"
        gl = [grads[n] for n in names]
        theirs = _rs_to_owner(gl, own, tag)
        pairs = [_sum_pair(g, t, o, name=f"rs_pair_sum_{n}") for n, g, t, o in zip(names, gl, theirs, own)]
        ssem, rsem, lands, token = _a2a_start(pairs, own, tag)
        flying[tag] = (names, own, ssem, rsem, pairs, lands)
        return token

    small = {n: wts[n] for n in _SMALL}
    loss, grad_x, gsmall = _local_step(x[0], loss_target[0], small, depth, weights_of_layer, grads_done)

    reds = {}
    for l in reversed(range(depth)):
        for part in ("ffn", "attn"):
            tag = f"{part}{l}"
            names, own, ssem, rsem, pairs, lands = flying.pop(tag)
            got = _a2a_wait(ssem, rsem, pairs, lands, grad_x, own, tag)
            for n, p, t, o in zip(names, pairs, got, own):
                reds[n, l] = _sum_chips(p, t, o, name=f"rs_sum_chips_{n}")
    order = [(n, l) for l in range(depth) for n in _BIG]
    sibs = dict(zip(order, _rs_from_owner([reds[k] for k in order], tuple(_OWNER[n] for n, _ in order))))

    sflat = _pack_flat([gsmall[n].reshape(-1) for n in _SMALL], 8 * LANES)
    rs = sflat.shape[0] // LANES
    sall = _gather_small(sflat.reshape(rs, LANES))

    def packed(d):
        return _pack_flat([d[n] for n in _SMALL], 8 * LANES).reshape(rs, LANES)

    outs = {tag: {} for tag in ("grad", "delta", "new_m", "new_v")}
    for n in _BIG:
        res = _adamw(wts[n], [reds[n, 0], reds[n, 1]], [sibs[n, 0], sibs[n, 1]], mom[n], var[n], _OWNER[n],
                     name=f"adamw_{n}")
        for tag, r in zip(("grad", "delta", "new_m", "new_v"), res):
            outs[tag][n] = r.transpose(0, 2, 1) if n in _COL_SHARDED else r
    for tag, smallflat in zip(("grad", "delta", "new_m", "new_v"), _adamw_small(packed(wts), sall, packed(mom), packed(var))):
        outs[tag].update(zip(_SMALL, _unpack_flat(smallflat.reshape(-1), small_shapes)))

    total = lax.psum(loss[0, 0], ("x", "y", "c"))
    return (total, grad_x[None], *[outs["grad"][n] for n in _ORDER], *[outs["delta"][n] for n in _ORDER],
            *[outs["new_m"][n] for n in _ORDER], *[outs["new_v"][n] for n in _ORDER])
```

```python
import functools
import math

import numpy as np
import jax
import jax.numpy as jnp
from jax import lax
from jax.experimental import pallas as pl
from jax.experimental.pallas import tpu as pltpu

F32 = jnp.float32
CDT = jnp.bfloat16
WIRE = jnp.bfloat16

HEAD_DIM = 64
GRID_W = 64
ROPE_THETA = 10000.0
MLA_HEADS = 6
MLA_Q_RANK = 256
MLA_KV_RANK = 128
MLA_ROPE_DIM = 32
DIL_HEADS = 6
DIL_BRANCHES = ((128, 1), (512, 4), (2048, 16))
DIL_HALF = 64
GQA_Q_HEADS = 4
REL_BUCKETS = 32
REL_MAX_DIST = 1024
NEG_INF = -1e30
LANES = 128
VMEM_LIMIT = 56 * 1024 * 1024

ADAM_LR, ADAM_B1, ADAM_B2, ADAM_EPS, ADAM_WD, ADAM_STEP = 0.001, 0.9, 0.999, 1e-08, 0.01, 10

C_CQ, C_CKV, C_KR, C_DQ, C_DK, C_DV, C_GQ, C_GK, C_GV, IN_P = 0, 256, 384, 512, 896, 1280, 1664, 1920, 2048, 2176
IN_W = 2080
MESH_ID = pl.DeviceIdType.MESH


def _cparams(n_axes, vmem=VMEM_LIMIT):
    return pltpu.CompilerParams(dimension_semantics=("arbitrary",) * n_axes, vmem_limit_bytes=vmem)


MAX_WHOLE_DIM = 2304


def _pick(n, target):
    best = None
    for t in range(LANES, min(n, target) + 1, LANES):
        if n % t == 0:
            best = t
    if best is not None and (2 * best >= target or n > MAX_WHOLE_DIM):
        return best
    return n


def _sds(shape, dtype):
    return jax.ShapeDtypeStruct(tuple(shape), dtype)


def _in_cols():
    idx = -np.ones((IN_P,), np.int64)
    idx[C_CQ:C_CQ + 256] = np.arange(0, 256)
    idx[C_CKV:C_CKV + 128] = np.arange(256, 384)
    idx[C_KR + 64:C_KR + 96] = np.arange(384, 416)
    idx[C_DQ:C_DQ + 1152] = np.arange(416, 1568)
    gq = 1568 + (np.array([0, 2, 1, 3])[:, None] * 64 + np.arange(64)[None, :]).reshape(-1)
    idx[C_GQ:C_GQ + 256] = gq
    idx[C_GK:C_GK + 256] = np.arange(1824, 2080)
    return idx


def _uq_cols():
    idx = -np.ones((MLA_HEADS * 128,), np.int64)
    for h in range(MLA_HEADS):
        idx[h * 128:h * 128 + 96] = np.arange(96 * h, 96 * h + 96)
    return idx


def _ukv_cols():
    idx = -np.ones((MLA_HEADS * 128 + MLA_HEADS * 64,), np.int64)
    for h in range(MLA_HEADS):
        idx[h * 128:h * 128 + 64] = np.arange(128 * h, 128 * h + 64)
        idx[768 + h * 64:768 + h * 64 + 64] = np.arange(128 * h + 64, 128 * h + 128)
    return idx


def _out_rows():
    idx = np.arange(1024)
    idx[768:1024] = 768 + (np.array([0, 2, 1, 3])[:, None] * 64 + np.arange(64)[None, :]).reshape(-1)
    return idx


def _runs(idx):
    out, i = [], 0
    while i < len(idx):
        j = i + 1
        while j < len(idx) and ((idx[i] < 0 and idx[j] < 0) or (idx[i] >= 0 and idx[j] == idx[j - 1] + 1)):
            j += 1
        out.append((int(idx[i]), j - i))
        i = j
    return out


def _rows_from_shards(sh, idx):
    _, cs, r = sh.shape
    pieces = []
    for first, ln in _runs(idx):
        if first < 0:
            pieces.append(jnp.zeros((ln, r), sh.dtype))
            continue
        while ln > 0:
            k, off = divmod(first, cs)
            take = min(ln, cs - off)
            pieces.append(sh[k, off:off + take, :])
            first, ln = first + take, ln - take
    return jnp.concatenate(pieces, axis=0)


def _rows_to_shards(wp, idx, n):
    inv = np.zeros((n,), np.int64)
    pos = np.nonzero(idx >= 0)[0]
    inv[idx[pos]] = pos
    cs = n // 4
    shards = []
    for k in range(4):
        pieces = [wp[first:first + ln, :] for first, ln in _runs(inv[k * cs:(k + 1) * cs])]
        shards.append(jnp.concatenate(pieces, axis=0))
    return jnp.stack(shards)


def _t5_bucket_np(rel):
    nb = REL_BUCKETS // 2
    exact = nb // 2
    ret = np.where(rel > 0, nb, 0)
    n = np.abs(rel)
    nf = np.maximum(n, 1).astype(np.float32)
    large = exact + (np.log(nf / np.float32(exact)) / np.float32(math.log(REL_MAX_DIST / exact))
                     * np.float32(nb - exact)).astype(np.int32)
    large = np.minimum(large, nb - 1)
    return ret + np.where(n < exact, n, large)


def _branch_bucket_idx(tq, dil):
    kw = tq + 2 * DIL_HALF
    rel = np.arange(kw)[None, :] - DIL_HALF - np.arange(tq)[:, None]
    idx = _t5_bucket_np(rel * dil)
    return np.where(np.abs(rel) <= DIL_HALF, idx, -1).astype(np.int32)


def _rope_tables(S):
    inv = ROPE_THETA ** (-jnp.arange(0, 32, 2, dtype=F32) / 32)
    t = jnp.arange(S)
    pos = t.astype(F32)
    row = (t // GRID_W).astype(F32)
    col = (t % GRID_W).astype(F32)
    lane = np.arange(LANES)
    wm = lane - 64
    is_rope = (lane >= 64) & (lane < 96)
    ang = pos[:, None] * inv[np.where(is_rope, wm % 16, 0)][None, :]
    cm = jnp.where(is_rope[None], jnp.cos(ang), 1.0)
    smm = jnp.where((is_rope & (wm < 16))[None], -jnp.sin(ang), 0.0)
    spm = jnp.where((is_rope & (wm >= 16))[None], jnp.sin(ang), 0.0)
    g = lane % 64
    w = g % 32
    angg = jnp.where((g < 32)[None], row[:, None], col[:, None]) * inv[w % 16][None, :]
    cg = jnp.cos(angg)
    smg = jnp.where((w < 16)[None], -jnp.sin(angg), 0.0)
    spg = jnp.where((w >= 16)[None], jnp.sin(angg), 0.0)
    return (cm, smm, spm), (cg, smg, spg)


def _lanes(t, width):
    return t if width == LANES else jnp.concatenate([t] * (width // LANES), axis=1)


def _rope(x, tabs):
    c, sm, sp = (_lanes(t, x.shape[1]) for t in tabs)
    w = x.shape[1]
    return x * c + pltpu.roll(x, w - 16, 1) * sm + pltpu.roll(x, 16, 1) * sp


def _rope_t(dy, tabs):
    c, sm, sp = (_lanes(t, dy.shape[1]) for t in tabs)
    w = dy.shape[1]
    return dy * c + pltpu.roll(dy * sm, 16, 1) + pltpu.roll(dy * sp, w - 16, 1)


def _head_ones(width):
    i = np.arange(width)
    return jnp.asarray((i[:, None] // HEAD_DIM == i[None, :] // HEAD_DIM).astype(np.float32))


def _headsum(x, j):
    return jnp.dot(x, j, preferred_element_type=F32, precision=lax.Precision.HIGHEST)


def _mm(a, b, *, ta=False, tb=False, ga=False, gb=False, go=False, out_dtype=F32, add=None, name):
    G = a.shape[0] if ga else (b.shape[0] if gb else 1)
    a2 = a.shape[1:] if ga else a.shape
    b2 = b.shape[1:] if gb else b.shape
    K, M = a2 if ta else a2[::-1]
    N = b2[0] if tb else b2[1]
    assert (b2[1] if tb else b2[0]) == K
    tm, tn, tk = _pick(M, 1024), _pick(N, 1024), _pick(K, 1024)
    if tm * tn > 1024 * 1152:
        tm = _pick(M, 512)
    nk = K // tk
    steps = nk if (go or G == 1) else G * nk
    dn = (((0 if ta else 1,), (1 if tb else 0,)), ((), ()))

    def body(a_ref, b_ref, *rest):
        rest = list(rest)
        add_ref = rest.pop(0) if add is not None else None
        o_ref = rest.pop(0)
        part = lax.dot_general(a_ref[...], b_ref[...], dn, preferred_element_type=F32)
        if steps == 1:
            if add_ref is not None:
                part = part + add_ref[...]
            o_ref[...] = part.astype(o_ref.dtype)
            return
        acc_ref, = rest
        s = pl.program_id(3)

        @pl.when(s == 0)
        def _():
            acc_ref[...] = part if add_ref is None else part + add_ref[...]

        @pl.when(s > 0)
        def _():
            acc_ref[...] += part

        @pl.when(s == steps - 1)
        def _():
            o_ref[...] = acc_ref[...].astype(o_ref.dtype)

    def grp(g, s):
        return g if go else s // nk

    def kk(s):
        return s if steps == nk else s % nk

    def spec(grouped, block, index):
        if grouped:
            return pl.BlockSpec((None,) + block, lambda g, i, j, s: (grp(g, s),) + index(i, j, s))
        return pl.BlockSpec(block, lambda g, i, j, s: index(i, j, s))

    a_spec = (spec(ga, (tk, tm), lambda i, j, s: (kk(s), i)) if ta else spec(ga, (tm, tk), lambda i, j, s: (i, kk(s))))
    b_spec = (spec(gb, (tn, tk), lambda i, j, s: (j, kk(s))) if tb else spec(gb, (tk, tn), lambda i, j, s: (kk(s), j)))
    o_spec = spec(go, (tm, tn), lambda i, j, s: (i, j))
    return pl.pallas_call(
        body, name=name, grid=(G if go else 1, M // tm, N // tn, steps),
        in_specs=[a_spec, b_spec] + ([o_spec] if add is not None else []), out_specs=o_spec,
        out_shape=_sds(((G,) if go else ()) + (M, N), out_dtype),
        scratch_shapes=[pltpu.VMEM((tm, tn), F32)] if steps > 1 else [],
        compiler_params=_cparams(4),
    )(*([a, b] + ([add] if add is not None else [])))


def _row(ts, w, cb=0):
    return pl.BlockSpec((ts, w), lambda i: (i, cb))


def _full(shape):
    nd = len(shape)
    return pl.BlockSpec(tuple(shape), lambda i: (0,) * nd)


def _rms_fwd(x, g, eps=1e-6):
    r = lax.rsqrt(jnp.mean(x * x, axis=-1, keepdims=True) + eps)
    return x * r * g


def _rms_bwd(x, g, dy, eps=1e-6):
    r = lax.rsqrt(jnp.mean(x * x, axis=-1, keepdims=True) + eps)
    gdy = g * dy
    dx = r * gdy - x * (r * r * r) * jnp.mean(x * gdy, axis=-1, keepdims=True)
    return dx, x * r * dy


def _rms_head_fwd(x, g, j, eps=1e-6):
    r = lax.rsqrt(_headsum(x * x, j) * (1.0 / HEAD_DIM) + eps)
    return x * r * g


def _rms_head_bwd(x, g, dy, j, eps=1e-6):
    r = lax.rsqrt(_headsum(x * x, j) * (1.0 / HEAD_DIM) + eps)
    gdy = g * dy
    dx = r * gdy - x * (r * r * r) * (_headsum(x * gdy, j) * (1.0 / HEAD_DIM))
    return dx, x * r * dy


DIL_STRIDES = tuple(d for _, d in DIL_BRANCHES if d > 1)
DIL_W = DIL_HEADS * HEAD_DIM


def _res_spec(d, n, pad_blocks=0):
    return pl.BlockSpec((d, n, DIL_W), lambda i: (0, i + pad_blocks, 0))


def _prep_fwd(h, gq, gkv, ggq, ggk, tm, tg, j256):
    S = h.shape[0]
    ts = min(256, S)
    scale = HEAD_DIM ** -0.5
    nres = len(DIL_STRIDES)

    def body(*refs):
        (h_ref, gq_ref, gkv_ref, ggq_ref, ggk_ref, cm, smm, spm, cg, smg, spg, j_ref), refs = refs[:12], refs[12:]
        refs = refs[2 * nres:]
        (cq_o, ckv_o, kr_o, dq_o, dk_o, dv_o, gq_o, gk_o, gv_o), res_o = refs[:9], refs[9:-1]
        st = refs[-1]
        tabm = (cm[...], smm[...], spm[...])
        tabg = (cg[...], smg[...], spg[...])
        cq_o[...] = _rms_fwd(h_ref[:, C_CQ:C_CQ + 256], gq_ref[...]).astype(CDT)
        ckv_o[...] = _rms_fwd(h_ref[:, C_CKV:C_CKV + 128], gkv_ref[...]).astype(CDT)
        kr_o[...] = _rope(h_ref[:, C_KR:C_KR + 128], tabm).astype(CDT)
        dq_o[...] = (h_ref[:, C_DQ:C_DQ + 384] * scale).astype(CDT)
        dk_o[...] = h_ref[:, C_DK:C_DK + 384].astype(CDT)
        dv_o[...] = h_ref[:, C_DV:C_DV + 384].astype(CDT)
        for j, lanes in _lane_blocks(3 * DIL_W):
            st[j] = h_ref[:, C_DQ + lanes.start:C_DQ + lanes.stop] * (scale if j < 3 else 1.0)
        for bi, d in enumerate(DIL_STRIDES):
            for c in range(d):
                rows = pl.ds(c, ts // d, stride=d)
                for j, lanes in _lane_blocks(3 * DIL_W):
                    res_o[3 * bi + j // 3][c, :, (j % 3) * LANES:(j % 3 + 1) * LANES] = st.at[j][rows, :].astype(CDT)
        qn = _rms_head_fwd(h_ref[:, C_GQ:C_GQ + 256], ggq_ref[...], j_ref[...])
        gq_o[...] = (_rope(qn, tabg) * scale).astype(CDT)
        kn = _rms_head_fwd(h_ref[:, C_GK:C_GK + 128], ggk_ref[...], j_ref[0:128, 0:128])
        gk_o[...] = _rope(kn, tabg).astype(CDT)
        gv_o[...] = h_ref[:, C_GV:C_GV + 128].astype(CDT)

    widths = (256, 128, 128, 384, 384, 384, 256, 128, 128)
    out_specs = [_row(ts, w) for w in widths]
    out_shape = [_sds((S, w), CDT) for w in widths]
    zeros, aliases = [], {}
    for d in DIL_STRIDES:
        n, L = ts // d, S // d
        out_specs += [_res_spec(d, n), _res_spec(d, n, DIL_HALF // n), _res_spec(d, n, DIL_HALF // n)]
        out_shape += [_sds((d, L, DIL_W), CDT)] + [_sds((d, L + 2 * DIL_HALF, DIL_W), CDT)] * 2
        for t in range(2):
            aliases[12 + len(zeros)] = len(out_shape) - 2 + t
            zeros.append(jnp.zeros((d, L + 2 * DIL_HALF, DIL_W), CDT))
    return pl.pallas_call(
        body, name="prep_fwd", grid=(S // ts,),
        in_specs=[_row(ts, IN_P), _full(gq.shape), _full(gkv.shape), _full(ggq.shape), _full(ggk.shape)]
        + [_row(ts, LANES)] * 6 + [_full(j256.shape)] + [pl.BlockSpec(memory_space=pl.ANY)] * len(zeros),
        out_specs=out_specs, out_shape=out_shape, input_output_aliases=aliases,
        scratch_shapes=[pltpu.VMEM((3 * DIL_W // LANES, ts, LANES), F32)],
        compiler_params=_cparams(1),
    )(h, gq, gkv, ggq, ggk, *tm, *tg, j256, *zeros)


def _prep_bwd(h, dcq, dckv, dkr, ddq, ddk, ddv, dgq, dgk, dgv, gq, gkv, ggq, ggk, tg, j256):
    S = h.shape[0]
    ts = min(256, S)
    scale = HEAD_DIM ** -0.5

    def body(h_ref, dcq_r, dckv_r, dkr_r, q1, q2, q3, k1, k2, k3, v1, v2, v3, dgq_r, dgk_r, dgv_r,
             gq_ref, gkv_ref, ggq_ref, ggk_ref, cg, smg, spg, j_ref,
             dh_o, ngq_o, ngkv_o, nggq_o, nggk_o, *scr):
        tabg = (cg[...], smg[...], spg[...])
        first = pl.program_id(0) == 0
        scr, = scr
        d2, d3 = DIL_STRIDES
        dq = q1[...] + _by_token(q2, scr, d2) + _by_token(q3, scr, d3)
        dk = k1[...] + _by_token(k2, scr, d2) + _by_token(k3, scr, d3)
        dv = v1[...] + _by_token(v2, scr, d2) + _by_token(v3, scr, d3)

        def acc(o_ref, val):
            s = jnp.sum(val, axis=0, keepdims=True)

            @pl.when(first)
            def _():
                o_ref[...] = s

            @pl.when(jnp.logical_not(first))
            def _():
                o_ref[...] += s

        dx, dg = _rms_bwd(h_ref[:, C_CQ:C_CQ + 256], gq_ref[...], dcq_r[...])
        dh_o[:, C_CQ:C_CQ + 256] = dx.astype(CDT)
        acc(ngq_o, dg)
        dx, dg = _rms_bwd(h_ref[:, C_CKV:C_CKV + 128], gkv_ref[...], dckv_r[...])
        dh_o[:, C_CKV:C_CKV + 128] = dx.astype(CDT)
        acc(ngkv_o, dg)
        dh_o[:, C_KR:C_KR + 128] = dkr_r[...].astype(CDT)
        dh_o[:, C_DQ:C_DQ + 384] = (dq * scale).astype(CDT)
        dh_o[:, C_DK:C_DK + 384] = dk.astype(CDT)
        dh_o[:, C_DV:C_DV + 384] = dv.astype(CDT)
        dqn = _rope_t(dgq_r[...] * scale, tabg)
        dx, dg = _rms_head_bwd(h_ref[:, C_GQ:C_GQ + 256], ggq_ref[...], dqn, j_ref[...])
        dh_o[:, C_GQ:C_GQ + 256] = dx.astype(CDT)
        acc(nggq_o, dg)
        dkn = _rope_t(dgk_r[...], tabg)
        dx, dg = _rms_head_bwd(h_ref[:, C_GK:C_GK + 128], ggk_ref[...], dkn, j_ref[0:128, 0:128])
        dh_o[:, C_GK:C_GK + 128] = dx.astype(CDT)
        acc(nggk_o, dg)
        dh_o[:, C_GV:C_GV + 128] = dgv_r[...].astype(CDT)

    d2, d3 = DIL_STRIDES
    n2, n3 = ts // d2, ts // d3
    tok = _row(ts, DIL_W)
    return pl.pallas_call(
        body, name="prep_bwd", grid=(S // ts,),
        in_specs=[_row(ts, IN_P), _row(ts, 256), _row(ts, 128), _row(ts, 128)]
        + [tok, _res_spec(d2, n2), _res_spec(d3, n3)]
        + [tok, _res_spec(d2, n2, DIL_HALF // n2), _res_spec(d3, n3, DIL_HALF // n3)] * 2
        + [_row(ts, 256), _row(ts, 128), _row(ts, 128)]
        + [_full(gq.shape), _full(gkv.shape), _full(ggq.shape), _full(ggk.shape)] + [_row(ts, LANES)] * 3
        + [_full(j256.shape)],
        out_specs=[_row(ts, IN_P), _full((1, 256)), _full((1, 128)), _full((1, 256)), _full((1, 128))],
        out_shape=[_sds((S, IN_P), CDT), _sds((1, 256), F32), _sds((1, 128), F32), _sds((1, 256), F32),
                   _sds((1, 128), F32)],
        scratch_shapes=[_TOKEN_SCRATCH(ts)],
        compiler_params=_cparams(1),
    )(h, dcq, dckv, dkr, *ddq, *ddk, *ddv, dgq, dgk, dgv, gq, gkv, ggq, ggk, *tg, j256)


def _lane_blocks(width):
    return [(j, slice(j * LANES, (j + 1) * LANES)) for j in range(width // LANES)]


_TOKEN_SCRATCH = lambda ts: pltpu.VMEM((DIL_W // LANES, ts, LANES), F32)


def _by_token(res_ref, scr_ref, d):
    n = res_ref.shape[1]
    if d == 1:
        return res_ref[0].astype(F32)
    for c in range(d):
        for j, lanes in _lane_blocks(res_ref.shape[2]):
            scr_ref.at[j][pl.ds(c, n, stride=d), :] = res_ref[c, :, lanes].astype(F32)
    return jnp.concatenate([scr_ref[j] for j, _ in _lane_blocks(res_ref.shape[2])], axis=1)


def _by_residue(val, scr_ref, out_ref, d):
    n = out_ref.shape[1]
    if d == 1:
        out_ref[0] = val.astype(out_ref.dtype)
        return
    for j, lanes in _lane_blocks(out_ref.shape[2]):
        scr_ref[j] = val[:, lanes]
    for c in range(d):
        for j, lanes in _lane_blocks(out_ref.shape[2]):
            out_ref[c, :, lanes] = scr_ref.at[j][pl.ds(c, n, stride=d), :].astype(out_ref.dtype)


def _mla_prep_fwd(qa, kvp, kr, tm, scale):
    S = qa.shape[0]
    ts = min(256, S)

    def body(qa_ref, kv_ref, kr_ref, cm, smm, spm, q_o, k_o):
        tabm = (cm[...], smm[...], spm[...])
        q_o[...] = (_rope(qa_ref[...], tabm) * scale).astype(CDT)
        k_o[...] = kv_ref[:, 0:768] + _lanes(kr_ref[...], 768)

    return pl.pallas_call(
        body, name="mla_prep_fwd", grid=(S // ts,),
        in_specs=[_row(ts, 768), _row(ts, 1152), _row(ts, 128)] + [_row(ts, LANES)] * 3,
        out_specs=[_row(ts, 768)] * 2, out_shape=[_sds((S, 768), CDT)] * 2,
        compiler_params=_cparams(1),
    )(qa, kvp, kr, *tm)


def _mla_prep_bwd(dq, dk, dv, tm, scale):
    S = dq.shape[0]
    ts = min(256, S)

    def body(dq_ref, dk_ref, dv_ref, cm, smm, spm, dqa_o, dkv_o, dkr_o):
        tabm = (cm[...], smm[...], spm[...])
        lane = lax.broadcasted_iota(jnp.int32, (1, LANES), 1)
        dqa_o[...] = _rope_t(dq_ref[...] * scale, tabm).astype(CDT)
        dkr = jnp.zeros((ts, LANES), F32)
        for hd in range(MLA_HEADS):
            blk = dk_ref[:, hd * 128:(hd + 1) * 128]
            dkv_o[:, hd * 128:(hd + 1) * 128] = jnp.where(lane < 64, blk, 0.0).astype(CDT)
            dkr = dkr + jnp.where((lane >= 64) & (lane < 96), blk, 0.0)
        dkv_o[:, 768:1152] = dv_ref[...].astype(CDT)
        dkr_o[...] = jnp.where((lane >= 64) & (lane < 96), _rope_t(dkr, tabm), 0.0)

    return pl.pallas_call(
        body, name="mla_prep_bwd", grid=(S // ts,),
        in_specs=[_row(ts, 768), _row(ts, 768), _row(ts, 384)] + [_row(ts, LANES)] * 3,
        out_specs=[_row(ts, 768), _row(ts, 1152), _row(ts, 128)],
        out_shape=[_sds((S, 768), CDT), _sds((S, 1152), CDT), _sds((S, 128), F32)],
        compiler_params=_cparams(1),
    )(dq, dk, dv, *tm)


def _ln_fwd(xa, xb, g, b, alpha, name):
    S, D = xa.shape
    ts = min(256, S)

    def body(xa_ref, xb_ref, g_ref, b_ref, y_o, yb_o, z_o):
        z = alpha * xa_ref[...] + xb_ref[...]
        mu = jnp.mean(z, axis=-1, keepdims=True)
        zc = z - mu
        var = jnp.mean(zc * zc, axis=-1, keepdims=True)
        y = zc * lax.rsqrt(var + 1e-5) * g_ref[...] + b_ref[...]
        y_o[...] = y
        yb_o[...] = y.astype(CDT)
        z_o[...] = z

    return pl.pallas_call(
        body, name=name, grid=(S // ts,),
        in_specs=[_row(ts, D), _row(ts, D), _full(g.shape), _full(b.shape)],
        out_specs=[_row(ts, D)] * 3, out_shape=[_sds((S, D), F32), _sds((S, D), CDT), _sds((S, D), F32)],
        compiler_params=_cparams(1),
    )(xa, xb, g, b)


def _ln_bwd(dya, dyb, z, g, alpha, name):
    S, D = z.shape
    ts = min(256, S)
    two = dyb is not None

    def body(*refs):
        if two:
            dya_ref, dyb_ref, z_ref, g_ref, dz_o, dzb_o, dg_o, db_o = refs
            dy = dya_ref[...] + alpha * dyb_ref[...]
        else:
            dya_ref, z_ref, g_ref, dz_o, dzb_o, dg_o, db_o = refs
            dy = dya_ref[...]
        z = z_ref[...]
        mu = jnp.mean(z, axis=-1, keepdims=True)
        zc = z - mu
        r = lax.rsqrt(jnp.mean(zc * zc, axis=-1, keepdims=True) + 1e-5)
        xh = zc * r
        dxh = dy * g_ref[...]
        dz = r * (dxh - jnp.mean(dxh, axis=-1, keepdims=True) - xh * jnp.mean(dxh * xh, axis=-1, keepdims=True))
        dz_o[...] = dz
        dzb_o[...] = dz.astype(CDT)
        sg = jnp.sum(dy * xh, axis=0, keepdims=True)
        sb = jnp.sum(dy, axis=0, keepdims=True)
        first = pl.program_id(0) == 0

        @pl.when(first)
        def _():
            dg_o[...] = sg
            db_o[...] = sb

        @pl.when(jnp.logical_not(first))
        def _():
            dg_o[...] += sg
            db_o[...] += sb

    ins = [dya] + ([dyb] if two else []) + [z, g]
    return pl.pallas_call(
        body, name=name, grid=(S // ts,),
        in_specs=[_row(ts, D)] * (3 if two else 2) + [_full(g.shape)],
        out_specs=[_row(ts, D), _row(ts, D), _full((1, D)), _full((1, D))],
        out_shape=[_sds((S, D), F32), _sds((S, D), CDT), _sds((1, D), F32), _sds((1, D), F32)],
        compiler_params=_cparams(1),
    )(*ins)


def _grp_spec(ts, w):
    return pl.BlockSpec((None, ts, w), lambda k, i: (k, i, 0))


def _ffn_up(xb, wg3, wu3):
    S, D = xb.shape
    G, Fc, _ = wg3.shape
    tm = _pick(S, 1024)
    wspec = pl.BlockSpec((None, Fc, D), lambda k, i: (k, 0, 0))

    def body(x_ref, wg_ref, wu_ref, g_o, u_o, a_o):
        x = x_ref[...]
        g = lax.dot_general(x, wg_ref[...], _NT, preferred_element_type=F32)
        u = lax.dot_general(x, wu_ref[...], _NT, preferred_element_type=F32)
        g_o[...] = g.astype(CDT)
        u_o[...] = u.astype(CDT)
        a_o[...] = (g / (1.0 + jnp.exp(-g)) * u).astype(CDT)

    return pl.pallas_call(
        body, name="ffn_up", grid=(G, S // tm),
        in_specs=[pl.BlockSpec((tm, D), lambda k, i: (i, 0)), wspec, wspec], out_specs=[_grp_spec(tm, Fc)] * 3,
        out_shape=[_sds((G, S, Fc), CDT)] * 3, compiler_params=_cparams(2),
    )(xb, wg3, wu3)


def _ffn_up_dx(dg3, du3, wg3, wu3):
    G, S, Fc = dg3.shape
    D = wg3.shape[2]
    tm = _pick(S, 1024)
    wspec = pl.BlockSpec((None, Fc, D), lambda i, k: (k, 0, 0))
    aspec = pl.BlockSpec((None, tm, Fc), lambda i, k: (k, i, 0))

    def body(dg_ref, du_ref, wg_ref, wu_ref, o_ref):
        part = (jnp.dot(dg_ref[...], wg_ref[...], preferred_element_type=F32)
                + jnp.dot(du_ref[...], wu_ref[...], preferred_element_type=F32))
        k = pl.program_id(1)

        @pl.when(k == 0)
        def _():
            o_ref[...] = part

        @pl.when(k > 0)
        def _():
            o_ref[...] += part

    return pl.pallas_call(
        body, name="ffn_up_dx", grid=(S // tm, G), in_specs=[aspec, aspec, wspec, wspec],
        out_specs=pl.BlockSpec((tm, D), lambda i, k: (i, 0)), out_shape=_sds((S, D), F32), compiler_params=_cparams(2),
    )(dg3, du3, wg3, wu3)


def _ffn_down_dx(dzb, wd3, g3, u3):
    S, D = dzb.shape
    G, Fc, _ = wd3.shape
    tm = _pick(S, 1024)

    def body(dz_ref, wd_ref, g_ref, u_ref, dg_o, du_o):
        da = lax.dot_general(dz_ref[...], wd_ref[...], _NT, preferred_element_type=F32)
        g = g_ref[...].astype(F32)
        sg = 1.0 / (1.0 + jnp.exp(-g))
        dg_o[...] = (da * u_ref[...].astype(F32) * (sg * (1.0 + g * (1.0 - sg)))).astype(CDT)
        du_o[...] = (da * (g * sg)).astype(CDT)

    return pl.pallas_call(
        body, name="ffn_down_dx", grid=(G, S // tm),
        in_specs=[pl.BlockSpec((tm, D), lambda k, i: (i, 0)), pl.BlockSpec((None, Fc, D), lambda k, i: (k, 0, 0)),
                  _grp_spec(tm, Fc), _grp_spec(tm, Fc)],
        out_specs=[_grp_spec(tm, Fc)] * 2, out_shape=[_sds((G, S, Fc), CDT)] * 2, compiler_params=_cparams(2),
    )(dzb, wd3, g3, u3)


def _loss_kernel(y, target):
    S, D = y.shape
    ts = min(256, S)

    def body(y_ref, t_ref, dy_o, loss_o):
        e = y_ref[...] - t_ref[...]
        dy_o[...] = e * (1.0 / D)
        part = jnp.sum(jnp.sum(e * e, axis=1, keepdims=True), axis=0, keepdims=True) * (0.5 / D)
        first = pl.program_id(0) == 0

        @pl.when(first)
        def _():
            loss_o[...] = part

        @pl.when(jnp.logical_not(first))
        def _():
            loss_o[...] += part

    return pl.pallas_call(
        body, name="loss", grid=(S // ts,), in_specs=[_row(ts, D)] * 2,
        out_specs=[_row(ts, D), _full((1, 1))], out_shape=[_sds((S, D), F32), _sds((1, 1), F32)],
        compiler_params=_cparams(1),
    )(y, target)


def _axpy(a, b, alpha, name):
    S, D = a.shape
    ts = min(256, S)

    def body(a_ref, b_ref, o_ref):
        o_ref[...] = a_ref[...] + alpha * b_ref[...]

    return pl.pallas_call(
        body, name=name, grid=(S // ts,), in_specs=[_row(ts, D)] * 2, out_specs=_row(ts, D),
        out_shape=_sds((S, D), F32), compiler_params=_cparams(1),
    )(a, b)


def _pair_masks():
    lane = lax.broadcasted_iota(jnp.int32, (1, LANES), 1)
    first = lane < HEAD_DIM
    return first, jnp.logical_not(first)


def _head_scalar(x, m):
    return jnp.max(jnp.where(m, x, -jnp.inf), axis=-1, keepdims=True)


_NT = (((1,), (1,)), ((), ()))
_TN = (((0,), (0,)), ((), ()))


def _attn_fwd(q, k, v, *, split, npairs, kblk, vblk, name):
    S = q.shape[0]
    qw = 256 if split else LANES
    tq = min(256, S)

    def body(q_ref, k_ref, v_ref, o_ref, lse_ref):
        masks = _pair_masks()
        outs, lses = [], []
        for hd in range(2):
            if split:
                qh = q_ref[:, hd * LANES:(hd + 1) * LANES]
                kh = k_ref[:, hd * LANES:(hd + 1) * LANES]
            else:
                qh = jnp.where(masks[hd], q_ref[...], jnp.zeros_like(q_ref[...]))
                kh = k_ref[...]
            s = lax.dot_general(qh, kh, _NT, preferred_element_type=F32)
            mx = jnp.max(s, axis=-1, keepdims=True)
            p = jnp.exp(s - mx)
            l = jnp.sum(p, axis=-1, keepdims=True)
            o = jnp.dot(p.astype(CDT), v_ref[...], preferred_element_type=F32)
            outs.append(o / l)
            lses.append(jnp.broadcast_to(mx + jnp.log(l), (tq, LANES)))
        o_ref[...] = jnp.where(masks[0], outs[0], outs[1]).astype(o_ref.dtype)
        lse_ref[...] = jnp.where(masks[0], lses[0], lses[1])

    return pl.pallas_call(
        body, name=name, grid=(npairs, S // tq),
        in_specs=[pl.BlockSpec((tq, qw), lambda p, i: (i, p)),
                  pl.BlockSpec((S, qw), lambda p, i: (0, kblk(p))),
                  pl.BlockSpec((S, LANES), lambda p, i: (0, vblk(p)))],
        out_specs=[pl.BlockSpec((tq, LANES), lambda p, i: (i, p))] * 2,
        out_shape=[_sds((S, LANES * npairs), CDT), _sds((S, LANES * npairs), F32)],
        compiler_params=_cparams(2),
    )(q, k, v)


def _attn_bwd(q, k, v, do, o, lse, *, split, npairs, kblk, vblk, doblk, shared_kv, name):
    S = q.shape[0]
    qw = 256 if split else LANES
    tq = min(256, S)
    nkv = 1 if shared_kv else npairs

    def body(q_ref, k_ref, v_ref, do_ref, o_ref, lse_ref, dq_ref, dk_ref, dv_ref):
        masks = _pair_masks()
        p_id, i_id = pl.program_id(0), pl.program_id(1)
        first = (i_id == 0) & ((p_id == 0) if shared_kv else True)
        do = do_ref[...]
        o = o_ref[...].astype(F32)
        lse = lse_ref[...]
        v = v_ref[...]
        dqs, dks, dvs = [], [], []
        for hd in range(2):
            m = masks[hd]
            if split:
                qh = q_ref[:, hd * LANES:(hd + 1) * LANES]
                kh = k_ref[:, hd * LANES:(hd + 1) * LANES]
            else:
                qh = jnp.where(m, q_ref[...], jnp.zeros_like(q_ref[...]))
                kh = k_ref[...]
            doh = jnp.where(m, do, 0.0)
            s = lax.dot_general(qh, kh, _NT, preferred_element_type=F32)
            p = jnp.exp(s - _head_scalar(lse, m))
            delta = jnp.sum(doh * o, axis=-1, keepdims=True)
            dohb = doh.astype(CDT)
            dp = lax.dot_general(dohb, v, _NT, preferred_element_type=F32)
            ds = (p * (dp - delta)).astype(CDT)
            dq = jnp.dot(ds, kh, preferred_element_type=F32)
            dqs.append(dq if split else jnp.where(m, dq, 0.0))
            dks.append(lax.dot_general(ds, qh, _TN, preferred_element_type=F32))
            dvs.append(lax.dot_general(p.astype(CDT), dohb, _TN, preferred_element_type=F32))
        if split:
            dq_ref[:, 0:LANES] = dqs[0]
            dq_ref[:, LANES:2 * LANES] = dqs[1]
        else:
            dq_ref[...] = dqs[0] + dqs[1]
        dv = dvs[0] + dvs[1]

        @pl.when(first)
        def _():
            if split:
                dk_ref[:, 0:LANES] = dks[0]
                dk_ref[:, LANES:2 * LANES] = dks[1]
            else:
                dk_ref[...] = dks[0] + dks[1]
            dv_ref[...] = dv

        @pl.when(jnp.logical_not(first))
        def _():
            if split:
                dk_ref[:, 0:LANES] += dks[0]
                dk_ref[:, LANES:2 * LANES] += dks[1]
            else:
                dk_ref[...] += dks[0] + dks[1]
            dv_ref[...] += dv

    kvo = (lambda p, i: (0, 0)) if shared_kv else (lambda p, i: (0, p))
    return pl.pallas_call(
        body, name=name, grid=(npairs, S // tq),
        in_specs=[pl.BlockSpec((tq, qw), lambda p, i: (i, p)),
                  pl.BlockSpec((S, qw), lambda p, i: (0, kblk(p))),
                  pl.BlockSpec((S, LANES), lambda p, i: (0, vblk(p))),
                  pl.BlockSpec((tq, LANES), lambda p, i: (i, doblk(p))),
                  pl.BlockSpec((tq, LANES), lambda p, i: (i, p)),
                  pl.BlockSpec((tq, LANES), lambda p, i: (i, p))],
        out_specs=[pl.BlockSpec((tq, qw), lambda p, i: (i, p)),
                   pl.BlockSpec((S, qw), kvo), pl.BlockSpec((S, LANES), kvo)],
        out_shape=[_sds((S, qw * npairs), F32), _sds((S, qw * nkv), F32), _sds((S, LANES * nkv), F32)],
        compiler_params=_cparams(2),
    )(q, k, v, do, o, lse)


def _bias_expand(idx, rel_bias, name):
    tq, kw = idx.shape

    def body(idx_ref, rb_ref, o_ref):
        idx = idx_ref[...]
        for hd in range(DIL_HEADS):
            acc = jnp.full((tq, kw), NEG_INF, F32)
            for u in range(REL_BUCKETS):
                acc = jnp.where(idx == u, rb_ref[u, hd], acc)
            o_ref[hd] = acc

    return pl.pallas_call(
        body, name=name,
        in_specs=[pl.BlockSpec(memory_space=pltpu.VMEM), pl.BlockSpec(memory_space=pltpu.SMEM)],
        out_specs=pl.BlockSpec(memory_space=pltpu.VMEM),
        out_shape=_sds((DIL_HEADS, tq, kw), F32),
    )(idx, rel_bias)


def _bias_reduce(idx, dtab, name):
    tq, kw = idx.shape

    def body(idx_ref, d_ref, o_ref):
        idx = idx_ref[...]
        rowid = lax.broadcasted_iota(jnp.int32, (REL_BUCKETS, kw), 0)
        for hd in range(DIL_HEADS):
            d = d_ref[hd]
            acc = jnp.zeros((REL_BUCKETS, kw), F32)
            for u in range(REL_BUCKETS):
                r = jnp.sum(jnp.where(idx == u, d, 0.0), axis=0, keepdims=True)
                acc = jnp.where(rowid == u, r, acc)
            o_ref[hd] = jnp.sum(acc, axis=1, keepdims=True)

    return pl.pallas_call(
        body, name=name,
        in_specs=[pl.BlockSpec(memory_space=pltpu.VMEM)] * 2, out_specs=pl.BlockSpec(memory_space=pltpu.VMEM),
        out_shape=_sds((DIL_HEADS, REL_BUCKETS, 1), F32),
    )(idx, dtab)


def _dil_window(i, tq, kw, L):
    start = pl.multiple_of(i * tq, DIL_HALF)
    key = start + lax.broadcasted_iota(jnp.int32, (1, kw), 1) - DIL_HALF
    return start, (key >= 0) & (key < L)


def _dil_fwd(qv, kv, vv, tab, *, dil, L, tq, name):
    kw = tq + 2 * DIL_HALF
    npair = DIL_HEADS // 2

    def body(q_ref, k_ref, v_ref, t_ref, o_ref, lse_ref):
        masks = _pair_masks()
        start, valid = _dil_window(pl.program_id(2), tq, kw, L)
        kwin = k_ref[pl.ds(start, kw), :]
        vwin = v_ref[pl.ds(start, kw), :]
        outs, lses = [], []
        for hd in range(2):
            qh = jnp.where(masks[hd], q_ref[...], jnp.zeros_like(q_ref[...]))
            s = lax.dot_general(qh, kwin, _NT, preferred_element_type=F32) + t_ref[hd]
            s = jnp.where(valid, s, NEG_INF)
            mx = jnp.max(s, axis=-1, keepdims=True)
            p = jnp.exp(s - mx)
            l = jnp.sum(p, axis=-1, keepdims=True)
            outs.append(jnp.dot(p.astype(CDT), vwin, preferred_element_type=F32) / l)
            lses.append(jnp.broadcast_to(mx + jnp.log(l), (tq, LANES)))
        o_ref[...] = jnp.where(masks[0], outs[0], outs[1])
        lse_ref[...] = jnp.where(masks[0], lses[0], lses[1])

    blk = pl.BlockSpec((None, tq, LANES), lambda p, c, i: (c, i, p))
    res = pl.BlockSpec((None, L + 2 * DIL_HALF, LANES), lambda p, c, i: (c, 0, p))
    return pl.pallas_call(
        body, name=name, grid=(npair, dil, L // tq),
        in_specs=[blk, res, res, pl.BlockSpec((2, tq, kw), lambda p, c, i: (p, 0, 0))],
        out_specs=[blk] * 2, out_shape=[_sds(qv.shape, F32)] * 2,
        compiler_params=_cparams(3),
    )(qv, kv, vv, tab)


def _dil_bwd(qv, kv, vv, tab, dov, lsev, deltav, *, dil, L, tq, name):
    kw = tq + 2 * DIL_HALF
    npair = DIL_HEADS // 2

    def body(q_ref, k_ref, v_ref, t_ref, do_ref, lse_ref, dl_ref, dq_ref, dk_ref, dv_ref, dt_ref):
        masks = _pair_masks()
        c_id, i_id = pl.program_id(1), pl.program_id(2)
        start, valid = _dil_window(i_id, tq, kw, L)
        kwin = k_ref[pl.ds(start, kw), :]
        vwin = v_ref[pl.ds(start, kw), :]

        @pl.when(i_id == 0)
        def _():
            dk_ref[...] = jnp.zeros_like(dk_ref)
            dv_ref[...] = jnp.zeros_like(dv_ref)

        @pl.when((i_id == 0) & (c_id == 0))
        def _():
            dt_ref[...] = jnp.zeros_like(dt_ref)

        do = do_ref[...]
        dq = jnp.zeros((tq, LANES), F32)
        dk = jnp.zeros((kw, LANES), F32)
        dv = jnp.zeros((kw, LANES), F32)
        for hd in range(2):
            m = masks[hd]
            qh = jnp.where(m, q_ref[...], jnp.zeros_like(q_ref[...]))
            doh = jnp.where(m, do, jnp.zeros_like(do))
            s = lax.dot_general(qh, kwin, _NT, preferred_element_type=F32) + t_ref[hd]
            s = jnp.where(valid, s, NEG_INF)
            p = jnp.exp(s - _head_scalar(lse_ref[...], m))
            dp = lax.dot_general(doh, vwin, _NT, preferred_element_type=F32)
            ds = p * (dp - _head_scalar(dl_ref[...], m))
            dt_ref[hd] += ds
            dsb = ds.astype(CDT)
            dq = dq + jnp.where(m, jnp.dot(dsb, kwin, preferred_element_type=F32), 0.0)
            dk = dk + lax.dot_general(dsb, qh, _TN, preferred_element_type=F32)
            dv = dv + lax.dot_general(p.astype(CDT), doh, _TN, preferred_element_type=F32)
        dq_ref[...] = dq
        dk_ref[pl.ds(start, kw), :] += dk
        dv_ref[pl.ds(start, kw), :] += dv

    blk = pl.BlockSpec((None, tq, LANES), lambda p, c, i: (c, i, p))
    res = pl.BlockSpec((None, L + 2 * DIL_HALF, LANES), lambda p, c, i: (c, 0, p))
    tsp = pl.BlockSpec((2, tq, kw), lambda p, c, i: (p, 0, 0))
    return pl.pallas_call(
        body, name=name, grid=(npair, dil, L // tq),
        in_specs=[blk, res, res, tsp, blk, blk, blk], out_specs=[blk, res, res, tsp],
        out_shape=[_sds(qv.shape, F32), _sds(kv.shape, F32), _sds(kv.shape, F32), _sds(tab.shape, F32)],
        compiler_params=_cparams(3),
    )(qv, kv, vv, tab, dov, lsev, deltav)


def _mix_weights(l1, l2, l3):
    mx = jnp.maximum(jnp.maximum(l1, l2), l3)
    e1, e2, e3 = jnp.exp(l1 - mx), jnp.exp(l2 - mx), jnp.exp(l3 - mx)
    inv = 1.0 / (e1 + e2 + e3)
    return e1 * inv, e2 * inv, e3 * inv


def _branch_specs(S, ts):
    dils = [d for _, d in DIL_BRANCHES]
    return dils, [_res_spec(d, ts // d) for d in dils], [(d, S // d, DIL_W) for d in dils]


def _dil_mix_fwd(os, ls):
    S = os[0].shape[0] * os[0].shape[1]
    ts = min(256, S)
    dils, specs, _ = _branch_specs(S, ts)

    def body(o1, o2, o3, l1, l2, l3, out, scr):
        o1, o2, o3, l1, l2, l3 = [_by_token(r, scr, d) for r, d in zip((o1, o2, o3, l1, l2, l3), dils + dils)]
        w1, w2, w3 = _mix_weights(l1, l2, l3)
        out[...] = (w1 * o1 + w2 * o2 + w3 * o3).astype(CDT)

    return pl.pallas_call(
        body, name="dil_mix_fwd", grid=(S // ts,), in_specs=specs + specs, out_specs=_row(ts, DIL_W),
        out_shape=_sds((S, DIL_W), CDT), scratch_shapes=[_TOKEN_SCRATCH(ts)],
        compiler_params=_cparams(1),
    )(*os, *ls)


def _dil_mix_bwd(dcat, os, ls, j384):
    S = os[0].shape[0] * os[0].shape[1]
    ts = min(256, S)
    dils, specs, shapes = _branch_specs(S, ts)

    def body(do_ref, o1, o2, o3, l1, l2, l3, j_ref, d1, d2, d3, e1, e2, e3, scr):
        o1, o2, o3, l1, l2, l3 = [_by_token(r, scr, d) for r, d in zip((o1, o2, o3, l1, l2, l3), dils + dils)]
        ws = _mix_weights(l1, l2, l3)
        do = do_ref[...]
        o = ws[0] * o1 + ws[1] * o2 + ws[2] * o3
        dot = _headsum(do * o, j_ref[...])
        for w, d, d_o, e_o in zip(ws, dils, (d1, d2, d3), (e1, e2, e3)):
            _by_residue(w * do, scr, d_o, d)
            _by_residue(w * dot, scr, e_o, d)

    return pl.pallas_call(
        body, name="dil_mix_bwd", grid=(S // ts,),
        in_specs=[_row(ts, DIL_W, 1)] + specs + specs + [_full(j384.shape)],
        out_specs=specs + specs,
        out_shape=[_sds(s, CDT) for s in shapes] + [_sds(s, F32) for s in shapes],
        scratch_shapes=[_TOKEN_SCRATCH(ts)],
        compiler_params=_cparams(1),
    )(dcat, *os, *ls, j384)


def _adamw_math(w, g, m, v):
    m = ADAM_B1 * m + (1.0 - ADAM_B1) * g
    v = ADAM_B2 * v + (1.0 - ADAM_B2) * (g * g)
    m_hat = m / (1.0 - ADAM_B1 ** ADAM_STEP)
    v_hat = v / (1.0 - ADAM_B2 ** ADAM_STEP)
    delta = -ADAM_LR * (m_hat / (jnp.sqrt(v_hat) + ADAM_EPS) + ADAM_WD * w)
    return delta, m, v


def _pick8(n, target):
    best = None
    for t in range(16, min(n, target) + 1, 16):
        if n % t == 0:
            best = t
    return best if best is not None else n


_ELEMS_PER_BLOCK = 256 * 1024


def _lead_spec(a, b):
    ta = _pick8(a, max(16, _ELEMS_PER_BLOCK // b))
    return ta, pl.BlockSpec((None, ta, b), lambda l, i: (l, i, 0))


def _adamw(w, reds, sibs, m, v, owner, name):
    L, a, b = w.shape
    ta, spec = _lead_spec(a, b)
    gspec = pl.BlockSpec((ta, b), lambda l, i: (i, 0))

    def body(w_ref, r0_ref, r1_ref, s0_ref, s1_ref, m_ref, v_ref, g_o, d_o, m_o, v_o):
        mine = lax.axis_index("c") == owner
        g0 = jnp.where(mine, r0_ref[...], s0_ref[...])
        g1 = jnp.where(mine, r1_ref[...], s1_ref[...])
        g = jnp.where(pl.program_id(0) == 0, g0, g1)
        d, mm, vv = _adamw_math(w_ref[...], g, m_ref[...], v_ref[...])
        g_o[...] = g
        d_o[...] = d
        m_o[...] = mm
        v_o[...] = vv

    return pl.pallas_call(
        body, name=name, grid=(L, a // ta), in_specs=[spec] + [gspec] * 4 + [spec, spec], out_specs=[spec] * 4,
        out_shape=[_sds(w.shape, F32)] * 4, compiler_params=_cparams(2),
    )(w, *reds, *sibs, m, v)


def _adamw_small(w, gall, m, v):
    R = w.shape[0]

    def body(w_ref, g_ref, m_ref, v_ref, g_o, d_o, m_o, v_o):
        g = g_ref[0]
        for k in range(1, 8):
            g = g + g_ref[k]
        d, mm, vv = _adamw_math(w_ref[...], g, m_ref[...], v_ref[...])
        g_o[...] = g
        d_o[...] = d
        m_o[...] = mm
        v_o[...] = vv

    vm = pl.BlockSpec(memory_space=pltpu.VMEM)
    return pl.pallas_call(
        body, name="adamw_small", in_specs=[vm] * 4, out_specs=[vm] * 4, out_shape=[_sds((R, LANES), F32)] * 4,
    )(w, gall, m, v)


def _sum_pair(g, t, owner, name):
    n, a, b = t.shape
    ta = _pick8(a, max(16, _ELEMS_PER_BLOCK // b))
    spec = pl.BlockSpec((None, ta, b), lambda k, i, own: (k * own[0], i * own[0], 0))

    def body(own_ref, g_ref, t_ref, o_ref):
        @pl.when(own_ref[0] == 1)
        def _():
            o_ref[...] = (g_ref[...].astype(F32) + t_ref[...].astype(F32)).astype(o_ref.dtype)

    return pl.pallas_call(
        body, name=name, out_shape=_sds(t.shape, WIRE),
        grid_spec=pltpu.PrefetchScalarGridSpec(num_scalar_prefetch=1, grid=(n, a // ta), in_specs=[spec] * 2,
                                               out_specs=spec),
        compiler_params=_cparams(2),
    )(_is_core(owner), g, t)


def _sum_chips(pair, t, owner, name):
    _, a, b = t.shape
    ta = _pick8(a, max(16, _ELEMS_PER_BLOCK // b))

    def body(own_ref, p_ref, t_ref, o_ref):
        @pl.when(own_ref[0] == 1)
        def _():
            me = 2 * lax.axis_index("x") + lax.axis_index("y")
            acc = p_ref[me].astype(F32)
            for k in range(3):
                acc = acc + t_ref[k].astype(F32)
            o_ref[...] = acc

    return pl.pallas_call(
        body, name=name, out_shape=_sds((a, b), F32),
        grid_spec=pltpu.PrefetchScalarGridSpec(
            num_scalar_prefetch=1, grid=(a // ta,),
            in_specs=[pl.BlockSpec((4, ta, b), lambda i, own: (0, i * own[0], 0)),
                      pl.BlockSpec((3, ta, b), lambda i, own: (0, i * own[0], 0))],
            out_specs=pl.BlockSpec((ta, b), lambda i, own: (i * own[0], 0))),
        compiler_params=_cparams(1),
    )(_is_core(owner), pair, t)


def _is_core(core):
    return (lax.axis_index("c") == core).astype(jnp.int32).reshape(1)


_HBM = pl.BlockSpec(memory_space=pltpu.HBM)


def _place():
    x, y, c = lax.axis_index("x"), lax.axis_index("y"), lax.axis_index("c")
    chips = [(1 - x, y), (x, 1 - y), (1 - x, 1 - y)]
    return x, y, c, chips


def _remote(src, dst, ssem, rsem, to):
    return pltpu.make_async_remote_copy(src_ref=src, dst_ref=dst, send_sem=ssem, recv_sem=rsem, device_id=to,
                                        device_id_type=MESH_ID)


def _dma_sems(n):
    return pltpu.SemaphoreType.DMA((n,))


_SEM = pl.BlockSpec(memory_space=pltpu.SEMAPHORE)
_ANY = pl.BlockSpec(memory_space=pl.ANY)
_EFFECT = pltpu.SideEffectType.DATAFLOW_SIDE_EFFECTING
_BIG = ("w_in", "mla_w_uq", "mla_w_ukv", "w_out", "ffn_w_gate", "ffn_w_up", "ffn_w_down")
_OWNER = dict(zip(_BIG, (1, 0, 0, 1, 0, 0, 1)))
_ATTN_WEIGHTS, _FFN_WEIGHTS = _BIG[:4], _BIG[4:]


def _hbm(a):
    return pltpu.with_memory_space_constraint(a, pltpu.HBM)


def _per_core(c, owners, fn):
    for g in range(2):
        mine = tuple(p for p, o in enumerate(owners) if o == g)
        theirs = tuple(p for p, o in enumerate(owners) if o != g)
        pl.when(c == g)(functools.partial(fn, mine, theirs))


def _token_spec():
    return pl.BlockSpec(memory_space=pltpu.VMEM), _sds((8, LANES), F32)


def _gather_start(shards, owners, layer, tag):
    n = len(shards)
    lands = [_hbm(lax.empty((4,) + s.shape[1:], s.dtype)) for s in shards]

    def body(*refs):
        w_refs, l_refs = refs[:n], refs[n:2 * n]
        ssem, rsem, token = refs[2 * n], refs[2 * n + 1], refs[-1]
        x, y, c, chips = _place()
        me = 2 * x + y

        def send(mine, _):
            for i in mine:
                for j, (cx, cy) in enumerate(chips):
                    _remote(w_refs[i].at[layer], l_refs[i].at[me], ssem.at[3 * i + j], rsem.at[3 * i + j],
                            (cx, cy, c)).start()

        _per_core(c, owners, send)
        token[...] = jnp.zeros_like(token)

    tspec, tshape = _token_spec()
    out = pl.pallas_call(
        body, name=f"gather_start_{tag}", in_specs=[_HBM] * (2 * n),
        out_specs=[_SEM, _SEM] + [_HBM] * n + [tspec],
        out_shape=[_dma_sems(3 * n), _dma_sems(3 * n)] + [pltpu.HBM(l.shape, l.dtype) for l in lands] + [tshape],
        input_output_aliases={n + i: 2 + i for i in range(n)},
        compiler_params=pltpu.CompilerParams(has_side_effects=_EFFECT),
    )(*[_hbm(s) for s in shards], *lands)
    return out[0], out[1], list(out[2:2 + n]), out[-1]


def _gather_wait(ssem, rsem, shards, lands, after, owners, layer, tag):
    n = len(shards)

    def body(*refs):
        w_refs, l_refs = refs[:n], refs[n:2 * n]
        ssem, rsem = refs[2 * n], refs[2 * n + 1]
        x, y, c, chips = _place()

        def wait(mine, _):
            for i in mine:
                for j, (cx, cy) in enumerate(chips):
                    cp = _remote(w_refs[i].at[layer], l_refs[i].at[2 * cx + cy], ssem.at[3 * i + j], rsem.at[3 * i + j],
                                 (cx, cy, c))
                    cp.wait_send()
                    cp.wait_recv()

        _per_core(c, owners, wait)

    return list(pl.pallas_call(
        body, name=f"gather_wait_{tag}", in_specs=[_HBM] * (2 * n) + [_SEM, _SEM, _ANY], out_specs=[_HBM] * n,
        out_shape=[pltpu.HBM(l.shape, l.dtype) for l in lands],
        input_output_aliases={n + i: i for i in range(n)},
        compiler_params=pltpu.CompilerParams(has_side_effects=_EFFECT),
    )(*[_hbm(s) for s in shards], *lands, ssem, rsem, after))


def _gather_finish(shards, lands, owners, layer, tag):
    n = len(shards)

    def body(*refs):
        w_refs, g_refs = refs[:n], refs[2 * n:3 * n]
        ssem, rsem = refs[3 * n:]
        x, y, c, chips = _place()
        me = 2 * x + y
        sib = (x, y, 1 - c)
        owns = [_remote(w.at[layer], g.at[me], ssem.at[i], rsem.at[i], sib) for i, (w, g) in enumerate(zip(w_refs, g_refs))]
        for cp in owns:
            cp.start()

        def forward(mine, theirs):
            def blk(i, j):
                b = g_refs[i].at[2 * chips[j][0] + chips[j][1]]
                return _remote(b, b, ssem.at[n + 3 * i + j], rsem.at[n + 3 * i + j], sib)

            for i in mine:
                for j in range(3):
                    blk(i, j).start()
            for i in theirs:
                for j in range(3):
                    blk(i, j).wait_recv()
            for i in mine:
                for j in range(3):
                    blk(i, j).wait_send()

        _per_core(c, owners, forward)
        for cp in owns:
            cp.wait_recv()
            cp.wait_send()

    return list(pl.pallas_call(
        body, name=f"gather_finish_{tag}", in_specs=[_HBM] * (2 * n), out_specs=[_HBM] * n,
        out_shape=[_sds(l.shape, l.dtype) for l in lands], input_output_aliases={n + i: i for i in range(n)},
        scratch_shapes=[_dma_sems(4 * n), _dma_sems(4 * n)],
    )(*shards, *lands))


def _rs_to_owner(grads, owners, tag):
    n = len(grads)

    def body(*refs):
        g_refs, t_refs = refs[:n], refs[n:2 * n]
        ssem, rsem = refs[2 * n:]
        x, y, c, _ = _place()

        def swap(mine, theirs):
            cps = [_remote(g_refs[i], t_refs[i], ssem.at[i], rsem.at[i], (x, y, 1 - c)) for i in theirs]
            for cp in cps:
                cp.start()
            for i in mine:
                _remote(g_refs[i], t_refs[i], ssem.at[i], rsem.at[i], (x, y, 1 - c)).wait_recv()
            for cp in cps:
                cp.wait_send()

        _per_core(c, owners, swap)

    return list(pl.pallas_call(
        body, name=f"rs_to_owner_{tag}", in_specs=[_HBM] * n, out_specs=[_HBM] * n,
        out_shape=[_sds(g.shape, g.dtype) for g in grads], scratch_shapes=[_dma_sems(n), _dma_sems(n)],
    )(*grads))


def _a2a_start(pairs, owners, tag):
    n = len(pairs)
    lands = [_hbm(lax.empty((3,) + p.shape[1:], p.dtype)) for p in pairs]

    def body(*refs):
        a_refs, t_refs = refs[:n], refs[n:2 * n]
        ssem, rsem, token = refs[2 * n], refs[2 * n + 1], refs[-1]
        x, y, c, chips = _place()

        def send(mine, _):
            for i in mine:
                for j, (cx, cy) in enumerate(chips):
                    _remote(a_refs[i].at[2 * cx + cy], t_refs[i].at[j], ssem.at[3 * i + j], rsem.at[3 * i + j],
                            (cx, cy, c)).start()

        _per_core(c, owners, send)
        token[...] = jnp.zeros_like(token)

    tspec, tshape = _token_spec()
    out = pl.pallas_call(
        body, name=f"rs_a2a_start_{tag}", in_specs=[_HBM] * (2 * n),
        out_specs=[_SEM, _SEM] + [_HBM] * n + [tspec],
        out_shape=[_dma_sems(3 * n), _dma_sems(3 * n)] + [pltpu.HBM(l.shape, l.dtype) for l in lands] + [tshape],
        input_output_aliases={n + i: 2 + i for i in range(n)},
        compiler_params=pltpu.CompilerParams(has_side_effects=_EFFECT),
    )(*[_hbm(p) for p in pairs], *lands)
    return out[0], out[1], list(out[2:2 + n]), out[-1]


def _a2a_wait(ssem, rsem, pairs, lands, after, owners, tag):
    n = len(pairs)

    def body(*refs):
        a_refs, t_refs = refs[:n], refs[n:2 * n]
        ssem, rsem = refs[2 * n], refs[2 * n + 1]
        x, y, c, chips = _place()

        def wait(mine, _):
            for i in mine:
                for j, (cx, cy) in enumerate(chips):
                    cp = _remote(a_refs[i].at[2 * cx + cy], t_refs[i].at[j], ssem.at[3 * i + j], rsem.at[3 * i + j],
                                 (cx, cy, c))
                    cp.wait_send()
                    cp.wait_recv()

        _per_core(c, owners, wait)

    return list(pl.pallas_call(
        body, name=f"rs_a2a_wait_{tag}", in_specs=[_HBM] * (2 * n) + [_SEM, _SEM, _ANY], out_specs=[_HBM] * n,
        out_shape=[pltpu.HBM(l.shape, l.dtype) for l in lands],
        input_output_aliases={n + i: i for i in range(n)},
        compiler_params=pltpu.CompilerParams(has_side_effects=_EFFECT),
    )(*[_hbm(p) for p in pairs], *lands, ssem, rsem, after))


def _rs_from_owner(reds, owners):
    n = len(reds)

    def body(*refs):
        q_refs, o_refs = refs[:n], refs[n:2 * n]
        ssem, rsem = refs[2 * n:]
        x, y, c, _ = _place()

        def swap(mine, theirs):
            cps = [_remote(q_refs[k], o_refs[k], ssem.at[k], rsem.at[k], (x, y, 1 - c)) for k in mine]
            for cp in cps:
                cp.start()
            for k in theirs:
                _remote(q_refs[k], o_refs[k], ssem.at[k], rsem.at[k], (x, y, 1 - c)).wait_recv()
            for cp in cps:
                cp.wait_send()

        _per_core(c, owners, swap)

    return list(pl.pallas_call(
        body, name="rs_from_owner", in_specs=[_HBM] * n, out_specs=[_HBM] * n,
        out_shape=[_sds(q.shape, q.dtype) for q in reds], scratch_shapes=[_dma_sems(n), _dma_sems(n)],
    )(*reds))


def _gather_weights(shards):
    n = len(shards)

    def body(*refs):
        w_refs, g_refs = refs[:n], refs[n:2 * n]
        ssem, rsem = refs[2 * n:]
        x, y, c, chips = _place()
        me = 2 * x + y
        sib = (x, y, 1 - c)
        owns = [_remote(w.at[l], g.at[l, me], ssem.at[6 * n + 2 * i + l], rsem.at[6 * n + 2 * i + l], sib)
                for i, (w, g) in enumerate(zip(w_refs, g_refs)) for l in range(2)]
        first = [_remote(w.at[c], g.at[c, me], ssem.at[3 * i + j], rsem.at[3 * i + j], (cx, cy, c))
                 for j, (cx, cy) in enumerate(chips) for i, (w, g) in enumerate(zip(w_refs, g_refs))]
        for cp in first + owns:
            cp.start()
        passed = []
        for j, (cx, cy) in enumerate(chips):
            for i, g in enumerate(g_refs):
                blk = g.at[c, 2 * cx + cy]
                _remote(blk, blk, ssem.at[3 * i + j], rsem.at[3 * i + j], sib).wait_recv()
                fw = _remote(blk, blk, ssem.at[3 * n + 3 * i + j], rsem.at[3 * n + 3 * i + j], sib)
                fw.start()
                passed.append(fw)
        for j, (cx, cy) in enumerate(chips):
            for i, g in enumerate(g_refs):
                blk = g.at[1 - c, 2 * cx + cy]
                _remote(blk, blk, ssem.at[3 * n + 3 * i + j], rsem.at[3 * n + 3 * i + j], sib).wait_recv()
        for cp in owns:
            cp.wait_recv()
        for cp in first + passed + owns:
            cp.wait_send()

    return pl.pallas_call(
        body, name="gather_weights", in_specs=[_HBM] * n, out_specs=[_HBM] * n,
        out_shape=[_sds((2, 4) + s.shape[1:], s.dtype) for s in shards],
        scratch_shapes=[_dma_sems(8 * n), _dma_sems(8 * n)],
    )(*shards)


def _sibling_halves(g0s, g1s):
    n = len(g0s)

    def body(*refs):
        g0_refs, g1_refs, t_refs = refs[:n], refs[n:2 * n], refs[2 * n:3 * n]
        ssem, rsem = refs[3 * n:]
        x, y, c, _ = _place()

        def swap(srcs):
            cps = [_remote(s, t, ssem.at[i], rsem.at[i], (x, y, 1 - c)) for i, (s, t) in enumerate(zip(srcs, t_refs))]
            for cp in cps:
                cp.start()
            for cp in cps:
                cp.wait()

        @pl.when(c == 0)
        def _():
            swap(g1_refs)

        @pl.when(c == 1)
        def _():
            swap(g0_refs)

    return pl.pallas_call(
        body, name="rs_sibling_halves", in_specs=[_HBM] * (2 * n), out_specs=[_HBM] * n,
        out_shape=[_sds(g.shape, g.dtype) for g in g0s], scratch_shapes=[_dma_sems(n), _dma_sems(n)],
    )(*g0s, *g1s)


def _chip_all_to_all(parts):
    n = len(parts)

    def body(*refs):
        a_refs, t_refs = refs[:n], refs[n:2 * n]
        ssem, rsem = refs[2 * n:]
        x, y, c, chips = _place()
        sends = [_remote(a.at[2 * cx + cy], t.at[j], ssem.at[3 * i + j], rsem.at[3 * i + j], (cx, cy, c))
                 for j, (cx, cy) in enumerate(chips) for i, (a, t) in enumerate(zip(a_refs, t_refs))]
        for cp in sends:
            cp.start()
        for cp in sends:
            cp.wait_recv()
        for cp in sends:
            cp.wait_send()

    return pl.pallas_call(
        body, name="rs_chip_all_to_all", in_specs=[_HBM] * n, out_specs=[_HBM] * n,
        out_shape=[_sds((3,) + p.shape[1:], p.dtype) for p in parts],
        scratch_shapes=[_dma_sems(3 * n), _dma_sems(3 * n)],
    )(*parts)


def _sibling_swap(reds):
    n = len(reds)

    def body(*refs):
        q_refs, o_refs = refs[:n], refs[n:2 * n]
        ssem, rsem = refs[2 * n:]
        x, y, c, _ = _place()
        cps = [_remote(q, o, ssem.at[i], rsem.at[i], (x, y, 1 - c)) for i, (q, o) in enumerate(zip(q_refs, o_refs))]
        for cp in cps:
            cp.start()
        for cp in cps:
            cp.wait()

    return pl.pallas_call(
        body, name="rs_sibling_swap", in_specs=[_HBM] * n, out_specs=[_HBM] * n,
        out_shape=[_sds(q.shape, q.dtype) for q in reds],
        scratch_shapes=[_dma_sems(n), _dma_sems(n)],
    )(*reds)


def _gather_small(s):
    R, _ = s.shape

    def body(s_ref, o_ref, ssem, rsem, lsem):
        x, y, c, _ = _place()
        me = 4 * x + 2 * y + c
        own = pltpu.make_async_copy(s_ref, o_ref.at[me], lsem)
        own.start()
        sends = []
        for k in range(1, 8):
            px, py, pc = x ^ (k >> 2), y ^ ((k >> 1) & 1), c ^ (k & 1)
            cp = _remote(s_ref, o_ref.at[me], ssem.at[k - 1], rsem.at[k - 1], (px, py, pc))
            cp.start()
            sends.append(cp)
        for k in range(1, 8):
            px, py, pc = x ^ (k >> 2), y ^ ((k >> 1) & 1), c ^ (k & 1)
            blk = o_ref.at[4 * px + 2 * py + pc]
            _remote(blk, blk, ssem.at[k - 1], rsem.at[k - 1], (px, py, pc)).wait_recv()
        for cp in sends:
            cp.wait_send()
        own.wait()

    vm = pl.BlockSpec(memory_space=pltpu.VMEM)
    return pl.pallas_call(
        body, name="gather_small", in_specs=[vm], out_specs=vm, out_shape=_sds((8, R, LANES), s.dtype),
        scratch_shapes=[pltpu.SemaphoreType.DMA((7,)), pltpu.SemaphoreType.DMA((7,)), pltpu.SemaphoreType.DMA],
    )(s)


_COL_SHARDED =("w_in", "mla_w_uq", "mla_w_ukv", "ffn_w_gate", "ffn_w_up")
_SMALL = ("mla_q_norm", "mla_kv_norm", "gqa_q_norm", "gqa_k_norm", "rel_bias", "ln1_g", "ln1_b", "ln2_g", "ln2_b")


def _pack_flat(arrs, align):
    flat = jnp.concatenate([a.reshape(-1) for a in arrs])
    pad = (-flat.shape[0]) % align
    return jnp.pad(flat, (0, pad)) if pad else flat


def _unpack_flat(flat, shapes):
    out, off = [], 0
    for s in shapes:
        n = int(np.prod(s))
        out.append(flat[off:off + n].reshape(s))
        off += n
    return out


def _perm_gqa_rows(w):
    return jnp.concatenate([w[:832], w[896:960], w[832:896], w[960:]], axis=0)


def _local_step(x, target, small, depth, weights_of_layer, grads_done):
    S, D = x.shape
    alpha = (2.0 * depth) ** 0.25
    in_idx, uq_idx, ukv_idx = _in_cols(), _uq_cols(), _ukv_cols()
    win, wuq, wukv, wout, wg, wu, wdn = ([None] * depth for _ in range(7))

    tm, tg = _rope_tables(S)
    j256, j384 = _head_ones(256), _head_ones(384)
    mla_scale = (64 + MLA_ROPE_DIM) ** -0.5
    branches = []
    for (_, dil) in DIL_BRANCHES:
        L = S // dil
        tq = min(256, L)
        idx = jnp.asarray(_branch_bucket_idx(tq, dil))
        branches.append((dil, L, tq, idx))
    tabs = [_bias_expand(idx, small["rel_bias"], name=f"bias_expand_{b}") for b, (_, _, _, idx) in enumerate(branches)]

    def padded(a):
        z = jnp.zeros((DIL_HALF, a.shape[1]), a.dtype)
        return jnp.concatenate([z, a, z], axis=0)[None]

    saved = []
    xf, xb = x, x.astype(CDT)
    for l in range(depth):
        W, token = weights_of_layer(l, "attn", xb)
        win[l] = _rows_from_shards(W["w_in"], in_idx)
        wuq[l] = _rows_from_shards(W["mla_w_uq"], uq_idx)
        wukv[l] = _rows_from_shards(W["mla_w_ukv"], ukv_idx)
        wout[l] = _perm_gqa_rows(W["w_out"].reshape(-1, D))
        gq, gkv = small["mla_q_norm"][l][None], small["mla_kv_norm"][l][None]
        if token is not None:
            gq = gq + token[0, 0]
        ggq = jnp.tile(small["gqa_q_norm"][l], 4)[None]
        ggk = jnp.tile(small["gqa_k_norm"][l], 2)[None]
        h = _mm(xb, win[l], tb=True, name="mm_in")
        cq, ckv, kr, qd, kd, vd, qg, kg, vg, *strided = _prep_fwd(h, gq, gkv, ggq, ggk, tm, tg, j256)
        qkv = [(qd[None], padded(kd), padded(vd))] + [tuple(strided[3 * b:3 * b + 3]) for b in range(len(DIL_STRIDES))]
        qa = _mm(cq, wuq[l], tb=True, name="mm_uq")
        kvp = _mm(ckv, wukv[l], tb=True, out_dtype=CDT, name="mm_ukv")
        qm, km = _mla_prep_fwd(qa, kvp, kr, tm, mla_scale)
        oa, lsa = _attn_fwd(qm, km, kvp, split=True, npairs=3, kblk=lambda p: p, vblk=lambda p: 6 + p,
                            name="mla_attn_fwd")
        oc, lsc = _attn_fwd(qg, kg, vg, split=False, npairs=2, kblk=lambda p: 0, vblk=lambda p: 0,
                            name="gqa_attn_fwd")
        obs, lbs = [], []
        for b, (dil, L, tq, _) in enumerate(branches):
            o_b, l_b = _dil_fwd(*qkv[b], tabs[b], dil=dil, L=L, tq=tq, name=f"dil_fwd_{b}")
            obs.append(o_b)
            lbs.append(l_b)
        ob = _dil_mix_fwd(obs, lbs)
        cat = jnp.concatenate([oa, ob, oc], axis=1)
        mix = _mm(cat, wout[l], name="mm_out")
        x1, x1b, z1 = _ln_fwd(xf, mix, small["ln1_g"][l][None], small["ln1_b"][l][None], alpha, name="ln1_fwd")
        W, _ = weights_of_layer(l, "ffn", x1b)
        wg[l], wu[l], wdn[l] = W["ffn_w_gate"], W["ffn_w_up"], W["ffn_w_down"]
        g3, u3, act = _ffn_up(x1b, wg[l], wu[l])
        ff = _mm(act, wdn[l], ga=True, gb=True, name="mm_down")
        x2, x2b, z2 = _ln_fwd(x1, ff, small["ln2_g"][l][None], small["ln2_b"][l][None], alpha, name="ln2_fwd")
        saved.append(dict(xb=xb, h=h, cq=cq, ckv=ckv, qg=qg, kg=kg, vg=vg, kvp=kvp, qm=qm, km=km, oa=oa, lsa=lsa,
                          oc=oc, lsc=lsc, obs=obs, lbs=lbs, qkv=qkv, cat=cat, z1=z1, x1b=x1b, g3=g3, u3=u3, act=act, z2=z2,
                          gq=gq, gkv=gkv, ggq=ggq, ggk=ggk))
        xf, xb = x2, x2b

    dy, loss = _loss_kernel(xf, target)

    gW = {k: [None] * depth for k in _BIG}
    gS = {k: [None] * depth for k in ("mla_q_norm", "mla_kv_norm", "gqa_q_norm", "gqa_k_norm", "ln1_g", "ln1_b", "ln2_g",
                                      "ln2_b")}
    g_rel = None
    dya, dyb = dy, None
    token = None
    for l in reversed(range(depth)):
        sv = saved[l]
        ln2_g = small["ln2_g"][l][None]
        if token is not None:
            ln2_g = ln2_g + token[0, 0]
        dz2, dz2b, gS["ln2_g"][l], gS["ln2_b"][l] = _ln_bwd(dya, dyb, sv["z2"], ln2_g, alpha,
                                                             name="ln2_bwd" if dyb is not None else "ln2_bwd_last")
        gW["ffn_w_down"][l] = _mm(sv["act"], dz2b, ta=True, ga=True, go=True, out_dtype=WIRE, name="mm_down_dw")
        dg3, du3 = _ffn_down_dx(dz2b, wdn[l], sv["g3"], sv["u3"])
        gW["ffn_w_gate"][l] = _mm(dg3, sv["x1b"], ta=True, ga=True, go=True, out_dtype=WIRE, name="mm_gate_dw")
        gW["ffn_w_up"][l] = _mm(du3, sv["x1b"], ta=True, ga=True, go=True, out_dtype=WIRE, name="mm_up_dw")
        dx1 = _ffn_up_dx(dg3, du3, wg[l], wu[l])
        token = grads_done(l, "ffn", {n: gW[n][l] for n in _FFN_WEIGHTS})
        ln1_g = small["ln1_g"][l][None]
        if token is not None:
            ln1_g = ln1_g + token[0, 0]
        dz1, dz1b, gS["ln1_g"][l], gS["ln1_b"][l] = _ln_bwd(dx1, dz2, sv["z1"], ln1_g, alpha, name="ln1_bwd")
        gW["w_out"][l] = _perm_gqa_rows(_mm(sv["cat"], dz1b, ta=True, out_dtype=WIRE, name="mm_out_dw")).reshape(4, -1, D)
        dcat = _mm(dz1b, wout[l], tb=True, name="mm_out_dx")
        dqg, dkg, dvg = _attn_bwd(sv["qg"], sv["kg"], sv["vg"], dcat, sv["oc"], sv["lsc"], split=False, npairs=2,
                                  kblk=lambda p: 0, vblk=lambda p: 0, doblk=lambda p: 6 + p, shared_kv=True,
                                  name="gqa_attn_bwd")
        dqm, dkm, dvm = _attn_bwd(sv["qm"], sv["km"], sv["kvp"], dcat, sv["oa"], sv["lsa"], split=True, npairs=3,
                                  kblk=lambda p: p, vblk=lambda p: 6 + p, doblk=lambda p: p, shared_kv=False,
                                  name="mla_attn_bwd")
        dqa, dkvp, dkr = _mla_prep_bwd(dqm, dkm, dvm, tm, mla_scale)
        gW["mla_w_uq"][l] = _rows_to_shards(_mm(dqa, sv["cq"], ta=True, out_dtype=WIRE, name="mm_uq_dw"), uq_idx, MLA_HEADS * 96)
        dcq = _mm(dqa, wuq[l], name="mm_uq_dx")
        gW["mla_w_ukv"][l] = _rows_to_shards(_mm(dkvp, sv["ckv"], ta=True, out_dtype=WIRE, name="mm_ukv_dw"), ukv_idx, MLA_HEADS * 128)
        dckv = _mm(dkvp, wukv[l], name="mm_ukv_dx")
        mixb = _dil_mix_bwd(dcat, sv["obs"], sv["lbs"], j384)
        ddq, ddk, ddv = [], [], []
        for b, (dil, L, tq, idx) in enumerate(branches):
            dq_b, dk_b, dv_b, dtab = _dil_bwd(*sv["qkv"][b], tabs[b], mixb[b], sv["lbs"][b], mixb[3 + b], dil=dil, L=L,
                                              tq=tq, name=f"dil_bwd_{b}")
            if dil == 1:
                dq_b, dk_b, dv_b = dq_b[0], dk_b[0, DIL_HALF:DIL_HALF + S], dv_b[0, DIL_HALF:DIL_HALF + S]
            ddq.append(dq_b)
            ddk.append(dk_b)
            ddv.append(dv_b)
            g_b = _bias_reduce(idx, dtab, name=f"bias_reduce_{b}")[:, :, 0].T
            g_rel = g_b if g_rel is None else g_rel + g_b
        dh, n1, n2, n3, n4 = _prep_bwd(sv["h"], dcq, dckv, dkr, ddq, ddk, ddv, dqg, dkg, dvg, sv["gq"], sv["gkv"],
                                       sv["ggq"], sv["ggk"], tg, j256)
        gS["mla_q_norm"][l], gS["mla_kv_norm"][l] = n1[0], n2[0]
        gS["gqa_q_norm"][l] = n3[0].reshape(4, 64).sum(0)
        gS["gqa_k_norm"][l] = n4[0].reshape(2, 64).sum(0)
        gW["w_in"][l] = _rows_to_shards(_mm(dh, sv["xb"], ta=True, out_dtype=WIRE, name="mm_in_dw"), in_idx, IN_W)
        dya = _mm(dh, win[l], name="mm_in_dx")
        dyb = dz1
        token = grads_done(l, "attn", {n: gW[n][l] for n in _ATTN_WEIGHTS})
    grad_x = _axpy(dya, dyb, alpha, name="grad_x")

    gsmall = {k: jnp.stack([a.reshape(-1) for a in v]) for k, v in gS.items()}
    gsmall["rel_bias"] = g_rel
    return loss, grad_x, gsmall


_ORDER = ("w_in", "mla_q_norm", "mla_kv_norm", "mla_w_uq", "mla_w_ukv", "gqa_q_norm", "gqa_k_norm", "rel_bias", "w_out",
          "ln1_g", "ln1_b", "ffn_w_gate", "ffn_w_up", "ffn_w_down", "ln2_g", "ln2_b")


def kernel(x, w_in, mla_q_norm, mla_kv_norm, mla_w_uq, mla_w_ukv, gqa_q_norm, gqa_k_norm, rel_bias, w_out, ln1_g, ln1_b, ffn_w_gate, ffn_w_up, ffn_w_down, ln2_g, ln2_b, loss_target, m_w_in, m_mla_q_norm, m_mla_kv_norm, m_mla_w_uq, m_mla_w_ukv, m_gqa_q_norm, m_gqa_k_norm, m_rel_bias, m_w_out, m_ln1_g, m_ln1_b, m_ffn_w_gate, m_ffn_w_up, m_ffn_w_down, m_ln2_g, m_ln2_b, v_w_in, v_mla_q_norm, v_mla_kv_norm, v_mla_w_uq, v_mla_w_ukv, v_gqa_q_norm, v_gqa_k_norm, v_rel_bias, v_w_out, v_ln1_g, v_ln1_b, v_ffn_w_gate, v_ffn_w_up, v_ffn_w_down, v_ln2_g, v_ln2_b):
    wts = dict(zip(_ORDER, (w_in, mla_q_norm, mla_kv_norm, mla_w_uq, mla_w_ukv, gqa_q_norm, gqa_k_norm, rel_bias, w_out,
                            ln1_g, ln1_b, ffn_w_gate, ffn_w_up, ffn_w_down, ln2_g, ln2_b)))
    mom = dict(zip(_ORDER, (m_w_in, m_mla_q_norm, m_mla_kv_norm, m_mla_w_uq, m_mla_w_ukv, m_gqa_q_norm, m_gqa_k_norm,
                            m_rel_bias, m_w_out, m_ln1_g, m_ln1_b, m_ffn_w_gate, m_ffn_w_up, m_ffn_w_down, m_ln2_g,
                            m_ln2_b)))
    var = dict(zip(_ORDER, (v_w_in, v_mla_q_norm, v_mla_kv_norm, v_mla_w_uq, v_mla_w_ukv, v_gqa_q_norm, v_gqa_k_norm,
                            v_rel_bias, v_w_out, v_ln1_g, v_ln1_b, v_ffn_w_gate, v_ffn_w_up, v_ffn_w_down, v_ln2_g,
                            v_ln2_b)))
    small_shapes = [wts[n].shape for n in _SMALL]
    for d in (wts, mom, var):
        for n in _COL_SHARDED:
            d[n] = d[n].transpose(0, 2, 1)

    depth = 2
    shards = {n: wts[n].astype(WIRE) for n in _BIG}
    flying = {}

    def start_gather(names, l, tag):
        own = tuple(_OWNER[n] for n in names)
        sh = [shards[n] for n in names]
        ssem, rsem, lands, token = _gather_start(sh, own, l, tag)
        return (names, own, sh, ssem, rsem, lands, l, tag), token

    def end_gather(flight, after):
        names, own, sh, ssem, rsem, lands, l, tag = flight
        got = _gather_finish(sh, _gather_wait(ssem, rsem, sh, lands, after, own, l, tag), own, l, tag)
        return {n: g.astype(CDT) for n, g in zip(names, got)}

    def weights_of_layer(l, part, after):
        if (l, part) == (0, "attn"):
            got = end_gather(start_gather(_ATTN_WEIGHTS, 0, "attn0")[0], after)
            flying["ffn0"], t0 = start_gather(_FFN_WEIGHTS, 0, "ffn0")
            flying["layer1"], t1 = start_gather(_BIG, 1, "layer1")
            return got, t0 + t1
        if (l, part) == (0, "ffn"):
            return end_gather(flying.pop("ffn0"), after), None
        if part == "attn":
            flying["w1"] = end_gather(flying.pop("layer1"), after)
        return flying["w1"], None

    def grads_done(l, part, grads):
        names = tuple(grads)
        own = tuple(_OWNER[n] for n in names)
        tag = f"{part}{l}"
        gl = [grads[n] for n in names]
        theirs = _rs_to_owner(gl, own, tag)
        pairs = [_sum_pair(g, t, o, name=f"rs_pair_sum_{n}") for n, g, t, o in zip(names, gl, theirs, own)]
        ssem, rsem, lands, token = _a2a_start(pairs, own, tag)
        flying[tag] = (names, own, ssem, rsem, pairs, lands)
        return token

    small = {n: wts[n] for n in _SMALL}
    loss, grad_x, gsmall = _local_step(x[0], loss_target[0], small, depth, weights_of_layer, grads_done)

    reds = {}
    for l in reversed(range(depth)):
        for part in ("ffn", "attn"):
            tag = f"{part}{l}"
            names, own, ssem, rsem, pairs, lands = flying.pop(tag)
            got = _a2a_wait(ssem, rsem, pairs, lands, grad_x, own, tag)
            for n, p, t, o in zip(names, pairs, got, own):
                reds[n, l] = _sum_chips(p, t, o, name=f"rs_sum_chips_{n}")
    order = [(n, l) for l in range(depth) for n in _BIG]
    sibs = dict(zip(order, _rs_from_owner([reds[k] for k in order], tuple(_OWNER[n] for n, _ in order))))

    sflat = _pack_flat([gsmall[n].reshape(-1) for n in _SMALL], 8 * LANES)
    rs = sflat.shape[0] // LANES
    sall = _gather_small(sflat.reshape(rs, LANES))

    def packed(d):
        return _pack_flat([d[n] for n in _SMALL], 8 * LANES).reshape(rs, LANES)

    outs = {tag: {} for tag in ("grad", "delta", "new_m", "new_v")}
    for n in _BIG:
        res = _adamw(wts[n], [reds[n, 0], reds[n, 1]], [sibs[n, 0], sibs[n, 1]], mom[n], var[n], _OWNER[n],
                     name=f"adamw_{n}")
        for tag, r in zip(("grad", "delta", "new_m", "new_v"), res):
            outs[tag][n] = r.transpose(0, 2, 1) if n in _COL_SHARDED else r
    for tag, smallflat in zip(("grad", "delta", "new_m", "new_v"), _adamw_small(packed(wts), sall, packed(mom), packed(var))):
        outs[tag].update(zip(_SMALL, _unpack_flat(smallflat.reshape(-1), small_shapes)))

    total = lax.psum(loss[0, 0], ("x", "y", "c"))
    return (total, grad_x[None], *[outs["grad"][n] for n in _ORDER], *[outs["delta"][n] for n in _ORDER],
            *[outs["new_m"][n] for n in _ORDER], *[outs["new_v"][n] for n in _ORDER])
```

```python
import functools
import math

import numpy as np
import jax
import jax.numpy as jnp
from jax import lax
from jax.experimental import pallas as pl
from jax.experimental.pallas import tpu as pltpu

F32 = jnp.float32
CDT = jnp.bfloat16
WIRE = jnp.bfloat16

HEAD_DIM = 64
GRID_W = 64
ROPE_THETA = 10000.0
MLA_HEADS = 6
MLA_Q_RANK = 256
MLA_KV_RANK = 128
MLA_ROPE_DIM = 32
DIL_HEADS = 6
DIL_BRANCHES = ((128, 1), (512, 4), (2048, 16))
DIL_HALF = 64
GQA_Q_HEADS = 4
REL_BUCKETS = 32
REL_MAX_DIST = 1024
NEG_INF = -1e30
LANES = 128
VMEM_LIMIT = 56 * 1024 * 1024

ADAM_LR, ADAM_B1, ADAM_B2, ADAM_EPS, ADAM_WD, ADAM_STEP = 0.001, 0.9, 0.999, 1e-08, 0.01, 10

C_CQ, C_CKV, C_KR, C_DQ, C_DK, C_DV, C_GQ, C_GK, C_GV, IN_P = 0, 256, 384, 512, 896, 1280, 1664, 1920, 2048, 2176
IN_W = 2080
MESH_ID = pl.DeviceIdType.MESH


def _cparams(n_axes, vmem=VMEM_LIMIT):
    return pltpu.CompilerParams(dimension_semantics=("arbitrary",) * n_axes, vmem_limit_bytes=vmem)


MAX_WHOLE_DIM = 2304


def _pick(n, target):
    best = None
    for t in range(LANES, min(n, target) + 1, LANES):
        if n % t == 0:
            best = t
    if best is not None and (2 * best >= target or n > MAX_WHOLE_DIM):
        return best
    return n


def _sds(shape, dtype):
    return jax.ShapeDtypeStruct(tuple(shape), dtype)


def _in_cols():
    idx = -np.ones((IN_P,), np.int64)
    idx[C_CQ:C_CQ + 256] = np.arange(0, 256)
    idx[C_CKV:C_CKV + 128] = np.arange(256, 384)
    idx[C_KR + 64:C_KR + 96] = np.arange(384, 416)
    idx[C_DQ:C_DQ + 1152] = np.arange(416, 1568)
    gq = 1568 + (np.array([0, 2, 1, 3])[:, None] * 64 + np.arange(64)[None, :]).reshape(-1)
    idx[C_GQ:C_GQ + 256] = gq
    idx[C_GK:C_GK + 256] = np.arange(1824, 2080)
    return idx


def _uq_cols():
    idx = -np.ones((MLA_HEADS * 128,), np.int64)
    for h in range(MLA_HEADS):
        idx[h * 128:h * 128 + 96] = np.arange(96 * h, 96 * h + 96)
    return idx


def _ukv_cols():
    idx = -np.ones((MLA_HEADS * 128 + MLA_HEADS * 64,), np.int64)
    for h in range(MLA_HEADS):
        idx[h * 128:h * 128 + 64] = np.arange(128 * h, 128 * h + 64)
        idx[768 + h * 64:768 + h * 64 + 64] = np.arange(128 * h + 64, 128 * h + 128)
    return idx


def _runs(idx):
    out, i = [], 0
    while i < len(idx):
        j = i + 1
        while j < len(idx) and ((idx[i] < 0 and idx[j] < 0) or (idx[i] >= 0 and idx[j] == idx[j - 1] + 1)):
            j += 1
        out.append((int(idx[i]), j - i))
        i = j
    return out


def _rows_from_shards(sh, idx):
    _, cs, r = sh.shape
    pieces = []
    for first, ln in _runs(idx):
        if first < 0:
            pieces.append(jnp.zeros((ln, r), sh.dtype))
            continue
        while ln > 0:
            k, off = divmod(first, cs)
            take = min(ln, cs - off)
            pieces.append(sh[k, off:off + take, :])
            first, ln = first + take, ln - take
    return jnp.concatenate(pieces, axis=0)


def _rows_to_shards(wp, idx, n):
    inv = np.zeros((n,), np.int64)
    pos = np.nonzero(idx >= 0)[0]
    inv[idx[pos]] = pos
    cs = n // 4
    shards = []
    for k in range(4):
        pieces = [wp[first:first + ln, :] for first, ln in _runs(inv[k * cs:(k + 1) * cs])]
        shards.append(jnp.concatenate(pieces, axis=0))
    return jnp.stack(shards)


def _t5_bucket_np(rel):
    nb = REL_BUCKETS // 2
    exact = nb // 2
    ret = np.where(rel > 0, nb, 0)
    n = np.abs(rel)
    nf = np.maximum(n, 1).astype(np.float32)
    large = exact + (np.log(nf / np.float32(exact)) / np.float32(math.log(REL_MAX_DIST / exact))
                     * np.float32(nb - exact)).astype(np.int32)
    large = np.minimum(large, nb - 1)
    return ret + np.where(n < exact, n, large)


def _branch_bucket_idx(tq, dil):
    kw = tq + 2 * DIL_HALF
    rel = np.arange(kw)[None, :] - DIL_HALF - np.arange(tq)[:, None]
    idx = _t5_bucket_np(rel * dil)
    return np.where(np.abs(rel) <= DIL_HALF, idx, -1).astype(np.int32)


def _rope_tables(S):
    inv = ROPE_THETA ** (-jnp.arange(0, 32, 2, dtype=F32) / 32)
    t = jnp.arange(S)
    pos = t.astype(F32)
    row = (t // GRID_W).astype(F32)
    col = (t % GRID_W).astype(F32)
    lane = np.arange(LANES)
    wm = lane - 64
    is_rope = (lane >= 64) & (lane < 96)
    ang = pos[:, None] * inv[np.where(is_rope, wm % 16, 0)][None, :]
    cm = jnp.where(is_rope[None], jnp.cos(ang), 1.0)
    smm = jnp.where((is_rope & (wm < 16))[None], -jnp.sin(ang), 0.0)
    spm = jnp.where((is_rope & (wm >= 16))[None], jnp.sin(ang), 0.0)
    g = lane % 64
    w = g % 32
    angg = jnp.where((g < 32)[None], row[:, None], col[:, None]) * inv[w % 16][None, :]
    cg = jnp.cos(angg)
    smg = jnp.where((w < 16)[None], -jnp.sin(angg), 0.0)
    spg = jnp.where((w >= 16)[None], jnp.sin(angg), 0.0)
    return (cm, smm, spm), (cg, smg, spg)


def _lanes(t, width):
    return t if width == LANES else jnp.concatenate([t] * (width // LANES), axis=1)


def _rope(x, tabs):
    c, sm, sp = (_lanes(t, x.shape[1]) for t in tabs)
    w = x.shape[1]
    return x * c + pltpu.roll(x, w - 16, 1) * sm + pltpu.roll(x, 16, 1) * sp


def _rope_t(dy, tabs):
    c, sm, sp = (_lanes(t, dy.shape[1]) for t in tabs)
    w = dy.shape[1]
    return dy * c + pltpu.roll(dy * sm, 16, 1) + pltpu.roll(dy * sp, w - 16, 1)


def _head_ones(width):
    i = np.arange(width)
    return jnp.asarray((i[:, None] // HEAD_DIM == i[None, :] // HEAD_DIM).astype(np.float32))


def _headsum(x, j):
    return jnp.dot(x, j, preferred_element_type=F32, precision=lax.Precision.HIGHEST)


def _mm(a, b, *, ta=False, tb=False, ga=False, gb=False, go=False, out_dtype=F32, name):
    G = a.shape[0] if ga else (b.shape[0] if gb else 1)
    a2 = a.shape[1:] if ga else a.shape
    b2 = b.shape[1:] if gb else b.shape
    K, M = a2 if ta else a2[::-1]
    N = b2[0] if tb else b2[1]
    assert (b2[1] if tb else b2[0]) == K
    tm, tn, tk = _pick(M, 1024), _pick(N, 1024), _pick(K, 1024)
    if tm * tn > 1024 * 1152:
        tm = _pick(M, 512)
    nk = K // tk
    steps = nk if (go or G == 1) else G * nk
    dn = (((0 if ta else 1,), (1 if tb else 0,)), ((), ()))

    def body(a_ref, b_ref, o_ref, *acc):
        part = lax.dot_general(a_ref[...], b_ref[...], dn, preferred_element_type=F32)
        if steps == 1:
            o_ref[...] = part.astype(o_ref.dtype)
            return
        acc_ref, = acc
        s = pl.program_id(3)

        @pl.when(s == 0)
        def _():
            acc_ref[...] = part

        @pl.when(s > 0)
        def _():
            acc_ref[...] += part

        @pl.when(s == steps - 1)
        def _():
            o_ref[...] = acc_ref[...].astype(o_ref.dtype)

    def grp(g, s):
        return g if go else s // nk

    def kk(s):
        return s if steps == nk else s % nk

    def spec(grouped, block, index):
        if grouped:
            return pl.BlockSpec((None,) + block, lambda g, i, j, s: (grp(g, s),) + index(i, j, s))
        return pl.BlockSpec(block, lambda g, i, j, s: index(i, j, s))

    a_spec = (spec(ga, (tk, tm), lambda i, j, s: (kk(s), i)) if ta else spec(ga, (tm, tk), lambda i, j, s: (i, kk(s))))
    b_spec = (spec(gb, (tn, tk), lambda i, j, s: (j, kk(s))) if tb else spec(gb, (tk, tn), lambda i, j, s: (kk(s), j)))
    o_spec = spec(go, (tm, tn), lambda i, j, s: (i, j))
    return pl.pallas_call(
        body, name=name, grid=(G if go else 1, M // tm, N // tn, steps),
        in_specs=[a_spec, b_spec], out_specs=o_spec,
        out_shape=_sds(((G,) if go else ()) + (M, N), out_dtype),
        scratch_shapes=[pltpu.VMEM((tm, tn), F32)] if steps > 1 else [],
        compiler_params=_cparams(4),
    )(a, b)


def _row(ts, w, cb=0):
    return pl.BlockSpec((ts, w), lambda i: (i, cb))


def _full(shape):
    nd = len(shape)
    return pl.BlockSpec(tuple(shape), lambda i: (0,) * nd)


def _rms_fwd(x, g, eps=1e-6):
    r = lax.rsqrt(jnp.mean(x * x, axis=-1, keepdims=True) + eps)
    return x * r * g


def _rms_bwd(x, g, dy, eps=1e-6):
    r = lax.rsqrt(jnp.mean(x * x, axis=-1, keepdims=True) + eps)
    gdy = g * dy
    dx = r * gdy - x * (r * r * r) * jnp.mean(x * gdy, axis=-1, keepdims=True)
    return dx, x * r * dy


def _rms_head_fwd(x, g, j, eps=1e-6):
    r = lax.rsqrt(_headsum(x * x, j) * (1.0 / HEAD_DIM) + eps)
    return x * r * g


def _rms_head_bwd(x, g, dy, j, eps=1e-6):
    r = lax.rsqrt(_headsum(x * x, j) * (1.0 / HEAD_DIM) + eps)
    gdy = g * dy
    dx = r * gdy - x * (r * r * r) * (_headsum(x * gdy, j) * (1.0 / HEAD_DIM))
    return dx, x * r * dy


DIL_STRIDES = tuple(d for _, d in DIL_BRANCHES if d > 1)
DIL_W = DIL_HEADS * HEAD_DIM


def _res_spec(d, n, pad_blocks=0):
    return pl.BlockSpec((d, n, DIL_W), lambda i: (0, i + pad_blocks, 0))


def _prep_fwd(h, gq, gkv, ggq, ggk, tm, tg, j256):
    S = h.shape[0]
    ts = min(256, S)
    scale = HEAD_DIM ** -0.5
    nres = len(DIL_STRIDES)

    def body(*refs):
        (h_ref, gq_ref, gkv_ref, ggq_ref, ggk_ref, cm, smm, spm, cg, smg, spg, j_ref), refs = refs[:12], refs[12:]
        refs = refs[2 * nres:]
        (cq_o, ckv_o, kr_o, dq_o, dk_o, dv_o, gq_o, gk_o, gv_o), res_o = refs[:9], refs[9:-1]
        st = refs[-1]
        tabm = (cm[...], smm[...], spm[...])
        tabg = (cg[...], smg[...], spg[...])
        cq_o[...] = _rms_fwd(h_ref[:, C_CQ:C_CQ + 256], gq_ref[...]).astype(CDT)
        ckv_o[...] = _rms_fwd(h_ref[:, C_CKV:C_CKV + 128], gkv_ref[...]).astype(CDT)
        kr_o[...] = _rope(h_ref[:, C_KR:C_KR + 128], tabm).astype(CDT)
        dq_o[...] = (h_ref[:, C_DQ:C_DQ + 384] * scale).astype(CDT)
        dk_o[...] = h_ref[:, C_DK:C_DK + 384].astype(CDT)
        dv_o[...] = h_ref[:, C_DV:C_DV + 384].astype(CDT)
        for j, lanes in _lane_blocks(3 * DIL_W):
            st[j] = h_ref[:, C_DQ + lanes.start:C_DQ + lanes.stop] * (scale if j < 3 else 1.0)
        for bi, d in enumerate(DIL_STRIDES):
            for c in range(d):
                rows = pl.ds(c, ts // d, stride=d)
                for j, lanes in _lane_blocks(3 * DIL_W):
                    res_o[3 * bi + j // 3][c, :, (j % 3) * LANES:(j % 3 + 1) * LANES] = st.at[j][rows, :].astype(CDT)
        qn = _rms_head_fwd(h_ref[:, C_GQ:C_GQ + 256], ggq_ref[...], j_ref[...])
        gq_o[...] = (_rope(qn, tabg) * scale).astype(CDT)
        kn = _rms_head_fwd(h_ref[:, C_GK:C_GK + 128], ggk_ref[...], j_ref[0:128, 0:128])
        gk_o[...] = _rope(kn, tabg).astype(CDT)
        gv_o[...] = h_ref[:, C_GV:C_GV + 128].astype(CDT)

    widths = (256, 128, 128, 384, 384, 384, 256, 128, 128)
    out_specs = [_row(ts, w) for w in widths]
    out_shape = [_sds((S, w), CDT) for w in widths]
    zeros, aliases = [], {}
    for d in DIL_STRIDES:
        n, L = ts // d, S // d
        out_specs += [_res_spec(d, n), _res_spec(d, n, DIL_HALF // n), _res_spec(d, n, DIL_HALF // n)]
        out_shape += [_sds((d, L, DIL_W), CDT)] + [_sds((d, L + 2 * DIL_HALF, DIL_W), CDT)] * 2
        for t in range(2):
            aliases[12 + len(zeros)] = len(out_shape) - 2 + t
            zeros.append(jnp.zeros((d, L + 2 * DIL_HALF, DIL_W), CDT))
    return pl.pallas_call(
        body, name="prep_fwd", grid=(S // ts,),
        in_specs=[_row(ts, IN_P), _full(gq.shape), _full(gkv.shape), _full(ggq.shape), _full(ggk.shape)]
        + [_row(ts, LANES)] * 6 + [_full(j256.shape)] + [pl.BlockSpec(memory_space=pl.ANY)] * len(zeros),
        out_specs=out_specs, out_shape=out_shape, input_output_aliases=aliases,
        scratch_shapes=[pltpu.VMEM((3 * DIL_W // LANES, ts, LANES), F32)],
        compiler_params=_cparams(1),
    )(h, gq, gkv, ggq, ggk, *tm, *tg, j256, *zeros)


def _prep_bwd(h, dcq, dckv, dkr, ddq, ddk, ddv, dgq, dgk, dgv, gq, gkv, ggq, ggk, tg, j256):
    S = h.shape[0]
    ts = min(256, S)
    scale = HEAD_DIM ** -0.5

    def body(h_ref, dcq_r, dckv_r, dkr_r, q1, q2, q3, k1, k2, k3, v1, v2, v3, dgq_r, dgk_r, dgv_r,
             gq_ref, gkv_ref, ggq_ref, ggk_ref, cg, smg, spg, j_ref,
             dh_o, ngq_o, ngkv_o, nggq_o, nggk_o, *scr):
        tabg = (cg[...], smg[...], spg[...])
        first = pl.program_id(0) == 0
        scr, = scr
        d2, d3 = DIL_STRIDES
        dq = q1[...] + _by_token(q2, scr, d2) + _by_token(q3, scr, d3)
        dk = k1[...] + _by_token(k2, scr, d2) + _by_token(k3, scr, d3)
        dv = v1[...] + _by_token(v2, scr, d2) + _by_token(v3, scr, d3)

        def acc(o_ref, val):
            s = jnp.sum(val, axis=0, keepdims=True)

            @pl.when(first)
            def _():
                o_ref[...] = s

            @pl.when(jnp.logical_not(first))
            def _():
                o_ref[...] += s

        dx, dg = _rms_bwd(h_ref[:, C_CQ:C_CQ + 256], gq_ref[...], dcq_r[...])
        dh_o[:, C_CQ:C_CQ + 256] = dx.astype(CDT)
        acc(ngq_o, dg)
        dx, dg = _rms_bwd(h_ref[:, C_CKV:C_CKV + 128], gkv_ref[...], dckv_r[...])
        dh_o[:, C_CKV:C_CKV + 128] = dx.astype(CDT)
        acc(ngkv_o, dg)
        dh_o[:, C_KR:C_KR + 128] = dkr_r[...].astype(CDT)
        dh_o[:, C_DQ:C_DQ + 384] = (dq * scale).astype(CDT)
        dh_o[:, C_DK:C_DK + 384] = dk.astype(CDT)
        dh_o[:, C_DV:C_DV + 384] = dv.astype(CDT)
        dqn = _rope_t(dgq_r[...] * scale, tabg)
        dx, dg = _rms_head_bwd(h_ref[:, C_GQ:C_GQ + 256], ggq_ref[...], dqn, j_ref[...])
        dh_o[:, C_GQ:C_GQ + 256] = dx.astype(CDT)
        acc(nggq_o, dg)
        dkn = _rope_t(dgk_r[...], tabg)
        dx, dg = _rms_head_bwd(h_ref[:, C_GK:C_GK + 128], ggk_ref[...], dkn, j_ref[0:128, 0:128])
        dh_o[:, C_GK:C_GK + 128] = dx.astype(CDT)
        acc(nggk_o, dg)
        dh_o[:, C_GV:C_GV + 128] = dgv_r[...].astype(CDT)

    d2, d3 = DIL_STRIDES
    n2, n3 = ts // d2, ts // d3
    tok = _row(ts, DIL_W)
    return pl.pallas_call(
        body, name="prep_bwd", grid=(S // ts,),
        in_specs=[_row(ts, IN_P), _row(ts, 256), _row(ts, 128), _row(ts, 128)]
        + [tok, _res_spec(d2, n2), _res_spec(d3, n3)]
        + [tok, _res_spec(d2, n2, DIL_HALF // n2), _res_spec(d3, n3, DIL_HALF // n3)] * 2
        + [_row(ts, 256), _row(ts, 128), _row(ts, 128)]
        + [_full(gq.shape), _full(gkv.shape), _full(ggq.shape), _full(ggk.shape)] + [_row(ts, LANES)] * 3
        + [_full(j256.shape)],
        out_specs=[_row(ts, IN_P), _full((1, 256)), _full((1, 128)), _full((1, 256)), _full((1, 128))],
        out_shape=[_sds((S, IN_P), CDT), _sds((1, 256), F32), _sds((1, 128), F32), _sds((1, 256), F32),
                   _sds((1, 128), F32)],
        scratch_shapes=[_TOKEN_SCRATCH(ts)],
        compiler_params=_cparams(1),
    )(h, dcq, dckv, dkr, *ddq, *ddk, *ddv, dgq, dgk, dgv, gq, gkv, ggq, ggk, *tg, j256)


def _lane_blocks(width):
    return [(j, slice(j * LANES, (j + 1) * LANES)) for j in range(width // LANES)]


_TOKEN_SCRATCH = lambda ts: pltpu.VMEM((DIL_W // LANES, ts, LANES), F32)


def _by_token(res_ref, scr_ref, d):
    n = res_ref.shape[1]
    if d == 1:
        return res_ref[0].astype(F32)
    for c in range(d):
        for j, lanes in _lane_blocks(res_ref.shape[2]):
            scr_ref.at[j][pl.ds(c, n, stride=d), :] = res_ref[c, :, lanes].astype(F32)
    return jnp.concatenate([scr_ref[j] for j, _ in _lane_blocks(res_ref.shape[2])], axis=1)


def _by_residue(val, scr_ref, out_ref, d):
    n = out_ref.shape[1]
    if d == 1:
        out_ref[0] = val.astype(out_ref.dtype)
        return
    for j, lanes in _lane_blocks(out_ref.shape[2]):
        scr_ref[j] = val[:, lanes]
    for c in range(d):
        for j, lanes in _lane_blocks(out_ref.shape[2]):
            out_ref[c, :, lanes] = scr_ref.at[j][pl.ds(c, n, stride=d), :].astype(out_ref.dtype)


def _mla_prep_fwd(qa, kvp, kr, tm, scale):
    S = qa.shape[0]
    ts = min(256, S)

    def body(qa_ref, kv_ref, kr_ref, cm, smm, spm, q_o, k_o):
        tabm = (cm[...], smm[...], spm[...])
        q_o[...] = (_rope(qa_ref[...], tabm) * scale).astype(CDT)
        k_o[...] = kv_ref[:, 0:768] + _lanes(kr_ref[...], 768)

    return pl.pallas_call(
        body, name="mla_prep_fwd", grid=(S // ts,),
        in_specs=[_row(ts, 768), _row(ts, 1152), _row(ts, 128)] + [_row(ts, LANES)] * 3,
        out_specs=[_row(ts, 768)] * 2, out_shape=[_sds((S, 768), CDT)] * 2,
        compiler_params=_cparams(1),
    )(qa, kvp, kr, *tm)


def _mla_prep_bwd(dq, dk, dv, tm, scale):
    S = dq.shape[0]
    ts = min(256, S)

    def body(dq_ref, dk_ref, dv_ref, cm, smm, spm, dqa_o, dkv_o, dkr_o):
        tabm = (cm[...], smm[...], spm[...])
        lane = lax.broadcasted_iota(jnp.int32, (1, LANES), 1)
        dqa_o[...] = _rope_t(dq_ref[...] * scale, tabm).astype(CDT)
        dkr = jnp.zeros((ts, LANES), F32)
        for hd in range(MLA_HEADS):
            blk = dk_ref[:, hd * 128:(hd + 1) * 128]
            dkv_o[:, hd * 128:(hd + 1) * 128] = jnp.where(lane < 64, blk, 0.0).astype(CDT)
            dkr = dkr + jnp.where((lane >= 64) & (lane < 96), blk, 0.0)
        dkv_o[:, 768:1152] = dv_ref[...].astype(CDT)
        dkr_o[...] = jnp.where((lane >= 64) & (lane < 96), _rope_t(dkr, tabm), 0.0)

    return pl.pallas_call(
        body, name="mla_prep_bwd", grid=(S // ts,),
        in_specs=[_row(ts, 768), _row(ts, 768), _row(ts, 384)] + [_row(ts, LANES)] * 3,
        out_specs=[_row(ts, 768), _row(ts, 1152), _row(ts, 128)],
        out_shape=[_sds((S, 768), CDT), _sds((S, 1152), CDT), _sds((S, 128), F32)],
        compiler_params=_cparams(1),
    )(dq, dk, dv, *tm)


def _ln_fwd(xa, xb, g, b, alpha, name):
    S, D = xa.shape
    ts = min(256, S)

    def body(xa_ref, xb_ref, g_ref, b_ref, y_o, yb_o, z_o):
        z = alpha * xa_ref[...] + xb_ref[...]
        mu = jnp.mean(z, axis=-1, keepdims=True)
        zc = z - mu
        var = jnp.mean(zc * zc, axis=-1, keepdims=True)
        y = zc * lax.rsqrt(var + 1e-5) * g_ref[...] + b_ref[...]
        y_o[...] = y
        yb_o[...] = y.astype(CDT)
        z_o[...] = z

    return pl.pallas_call(
        body, name=name, grid=(S // ts,),
        in_specs=[_row(ts, D), _row(ts, D), _full(g.shape), _full(b.shape)],
        out_specs=[_row(ts, D)] * 3, out_shape=[_sds((S, D), F32), _sds((S, D), CDT), _sds((S, D), F32)],
        compiler_params=_cparams(1),
    )(xa, xb, g, b)


def _ln_bwd(dya, dyb, z, g, alpha, name, loss_head=False):
    S, D = z.shape
    ts = min(256, S)

    def body(dya_ref, dyb_ref, z_ref, g_ref, dz_o, dzb_o, dg_o, db_o, *loss_o):
        first = pl.program_id(0) == 0
        if loss_head:
            err = dya_ref[...] - dyb_ref[...]
            dy = err * (1.0 / D)
            part = jnp.sum(jnp.sum(err * err, axis=1, keepdims=True), axis=0, keepdims=True) * (0.5 / D)

            @pl.when(first)
            def _():
                loss_o[0][...] = part

            @pl.when(jnp.logical_not(first))
            def _():
                loss_o[0][...] += part
        else:
            dy = dya_ref[...] + alpha * dyb_ref[...]
        z = z_ref[...]
        mu = jnp.mean(z, axis=-1, keepdims=True)
        zc = z - mu
        r = lax.rsqrt(jnp.mean(zc * zc, axis=-1, keepdims=True) + 1e-5)
        xh = zc * r
        dxh = dy * g_ref[...]
        dz = r * (dxh - jnp.mean(dxh, axis=-1, keepdims=True) - xh * jnp.mean(dxh * xh, axis=-1, keepdims=True))
        dz_o[...] = dz
        dzb_o[...] = dz.astype(CDT)
        sg = jnp.sum(dy * xh, axis=0, keepdims=True)
        sb = jnp.sum(dy, axis=0, keepdims=True)

        @pl.when(first)
        def _():
            dg_o[...] = sg
            db_o[...] = sb

        @pl.when(jnp.logical_not(first))
        def _():
            dg_o[...] += sg
            db_o[...] += sb

    extra = ([_full((1, 1))], [_sds((1, 1), F32)]) if loss_head else ([], [])
    return pl.pallas_call(
        body, name=name, grid=(S // ts,),
        in_specs=[_row(ts, D)] * 3 + [_full(g.shape)],
        out_specs=[_row(ts, D), _row(ts, D), _full((1, D)), _full((1, D))] + extra[0],
        out_shape=[_sds((S, D), F32), _sds((S, D), CDT), _sds((1, D), F32), _sds((1, D), F32)] + extra[1],
        compiler_params=_cparams(1),
    )(dya, dyb, z, g)


def _grp_spec(ts, w):
    return pl.BlockSpec((None, ts, w), lambda k, i: (k, i, 0))


def _ffn_up(xb, wg3, wu3):
    S, D = xb.shape
    G, Fc, _ = wg3.shape
    tm = _pick(S, 1024)
    wspec = pl.BlockSpec((None, Fc, D), lambda k, i: (k, 0, 0))

    def body(x_ref, wg_ref, wu_ref, g_o, u_o, a_o):
        x = x_ref[...]
        g = lax.dot_general(x, wg_ref[...], _NT, preferred_element_type=F32)
        u = lax.dot_general(x, wu_ref[...], _NT, preferred_element_type=F32)
        g_o[...] = g.astype(CDT)
        u_o[...] = u.astype(CDT)
        a_o[...] = (g / (1.0 + jnp.exp(-g)) * u).astype(CDT)

    return pl.pallas_call(
        body, name="ffn_up", grid=(G, S // tm),
        in_specs=[pl.BlockSpec((tm, D), lambda k, i: (i, 0)), wspec, wspec], out_specs=[_grp_spec(tm, Fc)] * 3,
        out_shape=[_sds((G, S, Fc), CDT)] * 3, compiler_params=_cparams(2),
    )(xb, wg3, wu3)


def _ffn_up_dx(dg3, du3, wg3, wu3):
    G, S, Fc = dg3.shape
    D = wg3.shape[2]
    tm = _pick(S, 1024)
    wspec = pl.BlockSpec((None, Fc, D), lambda i, k: (k, 0, 0))
    aspec = pl.BlockSpec((None, tm, Fc), lambda i, k: (k, i, 0))

    def body(dg_ref, du_ref, wg_ref, wu_ref, o_ref):
        part = (jnp.dot(dg_ref[...], wg_ref[...], preferred_element_type=F32)
                + jnp.dot(du_ref[...], wu_ref[...], preferred_element_type=F32))
        k = pl.program_id(1)

        @pl.when(k == 0)
        def _():
            o_ref[...] = part

        @pl.when(k > 0)
        def _():
            o_ref[...] += part

    return pl.pallas_call(
        body, name="ffn_up_dx", grid=(S // tm, G), in_specs=[aspec, aspec, wspec, wspec],
        out_specs=pl.BlockSpec((tm, D), lambda i, k: (i, 0)), out_shape=_sds((S, D), F32), compiler_params=_cparams(2),
    )(dg3, du3, wg3, wu3)


def _ffn_down_dx(dzb, wd3, g3, u3):
    S, D = dzb.shape
    G, Fc, _ = wd3.shape
    tm = _pick(S, 1024)

    def body(dz_ref, wd_ref, g_ref, u_ref, dg_o, du_o):
        da = lax.dot_general(dz_ref[...], wd_ref[...], _NT, preferred_element_type=F32)
        g = g_ref[...].astype(F32)
        sg = 1.0 / (1.0 + jnp.exp(-g))
        dg_o[...] = (da * u_ref[...].astype(F32) * (sg * (1.0 + g * (1.0 - sg)))).astype(CDT)
        du_o[...] = (da * (g * sg)).astype(CDT)

    return pl.pallas_call(
        body, name="ffn_down_dx", grid=(G, S // tm),
        in_specs=[pl.BlockSpec((tm, D), lambda k, i: (i, 0)), pl.BlockSpec((None, Fc, D), lambda k, i: (k, 0, 0)),
                  _grp_spec(tm, Fc), _grp_spec(tm, Fc)],
        out_specs=[_grp_spec(tm, Fc)] * 2, out_shape=[_sds((G, S, Fc), CDT)] * 2, compiler_params=_cparams(2),
    )(dzb, wd3, g3, u3)


def _axpy(a, b, alpha, name):
    S, D = a.shape
    ts = min(256, S)

    def body(a_ref, b_ref, o_ref):
        o_ref[...] = a_ref[...] + alpha * b_ref[...]

    return pl.pallas_call(
        body, name=name, grid=(S // ts,), in_specs=[_row(ts, D)] * 2, out_specs=_row(ts, D),
        out_shape=_sds((S, D), F32), compiler_params=_cparams(1),
    )(a, b)


def _pair_masks():
    lane = lax.broadcasted_iota(jnp.int32, (1, LANES), 1)
    first = lane < HEAD_DIM
    return first, jnp.logical_not(first)


def _head_scalar(x, m):
    return jnp.max(jnp.where(m, x, -jnp.inf), axis=-1, keepdims=True)


_NT = (((1,), (1,)), ((), ()))
_TN = (((0,), (0,)), ((), ()))


def _attn_fwd(q, k, v, *, split, npairs, kblk, vblk, name):
    S = q.shape[0]
    qw = 256 if split else LANES
    tq = min(512, S)

    def body(q_ref, k_ref, v_ref, o_ref, lse_ref):
        masks = _pair_masks()
        outs, lses = [], []
        for hd in range(2):
            if split:
                qh = q_ref[:, hd * LANES:(hd + 1) * LANES]
                kh = k_ref[:, hd * LANES:(hd + 1) * LANES]
            else:
                qh = jnp.where(masks[hd], q_ref[...], jnp.zeros_like(q_ref[...]))
                kh = k_ref[...]
            s = lax.dot_general(qh, kh, _NT, preferred_element_type=F32)
            mx = jnp.max(s, axis=-1, keepdims=True)
            p = jnp.exp(s - mx)
            l = jnp.sum(p, axis=-1, keepdims=True)
            o = jnp.dot(p.astype(CDT), v_ref[...], preferred_element_type=F32)
            outs.append(o / l)
            lses.append(jnp.broadcast_to(mx + jnp.log(l), (tq, LANES)))
        o_ref[...] = jnp.where(masks[0], outs[0], outs[1]).astype(o_ref.dtype)
        lse_ref[...] = jnp.where(masks[0], lses[0], lses[1])

    return pl.pallas_call(
        body, name=name, grid=(npairs, S // tq),
        in_specs=[pl.BlockSpec((tq, qw), lambda p, i: (i, p)),
                  pl.BlockSpec((S, qw), lambda p, i: (0, kblk(p))),
                  pl.BlockSpec((S, LANES), lambda p, i: (0, vblk(p)))],
        out_specs=[pl.BlockSpec((tq, LANES), lambda p, i: (i, p))] * 2,
        out_shape=[_sds((S, LANES * npairs), CDT), _sds((S, LANES * npairs), F32)],
        compiler_params=_cparams(2),
    )(q, k, v)


def _attn_bwd(q, k, v, do, o, lse, *, split, npairs, kblk, vblk, doblk, shared_kv, name):
    S = q.shape[0]
    qw = 256 if split else LANES
    tq = min(256, S)
    nkv = 1 if shared_kv else npairs

    def body(q_ref, k_ref, v_ref, do_ref, o_ref, lse_ref, dq_ref, dk_ref, dv_ref):
        masks = _pair_masks()
        p_id, i_id = pl.program_id(0), pl.program_id(1)
        first = (i_id == 0) & ((p_id == 0) if shared_kv else True)
        do = do_ref[...]
        o = o_ref[...].astype(F32)
        lse = lse_ref[...]
        v = v_ref[...]
        dqs, dks, dvs = [], [], []
        for hd in range(2):
            m = masks[hd]
            if split:
                qh = q_ref[:, hd * LANES:(hd + 1) * LANES]
                kh = k_ref[:, hd * LANES:(hd + 1) * LANES]
            else:
                qh = jnp.where(m, q_ref[...], jnp.zeros_like(q_ref[...]))
                kh = k_ref[...]
            doh = jnp.where(m, do, 0.0)
            s = lax.dot_general(qh, kh, _NT, preferred_element_type=F32)
            p = jnp.exp(s - _head_scalar(lse, m))
            delta = jnp.sum(doh * o, axis=-1, keepdims=True)
            dohb = doh.astype(CDT)
            dp = lax.dot_general(dohb, v, _NT, preferred_element_type=F32)
            ds = (p * (dp - delta)).astype(CDT)
            dq = jnp.dot(ds, kh, preferred_element_type=F32)
            dqs.append(dq if split else jnp.where(m, dq, 0.0))
            dks.append(lax.dot_general(ds, qh, _TN, preferred_element_type=F32))
            dvs.append(lax.dot_general(p.astype(CDT), dohb, _TN, preferred_element_type=F32))
        if split:
            dq_ref[:, 0:LANES] = dqs[0]
            dq_ref[:, LANES:2 * LANES] = dqs[1]
        else:
            dq_ref[...] = dqs[0] + dqs[1]
        dv = dvs[0] + dvs[1]

        @pl.when(first)
        def _():
            if split:
                dk_ref[:, 0:LANES] = dks[0]
                dk_ref[:, LANES:2 * LANES] = dks[1]
            else:
                dk_ref[...] = dks[0] + dks[1]
            dv_ref[...] = dv

        @pl.when(jnp.logical_not(first))
        def _():
            if split:
                dk_ref[:, 0:LANES] += dks[0]
                dk_ref[:, LANES:2 * LANES] += dks[1]
            else:
                dk_ref[...] += dks[0] + dks[1]
            dv_ref[...] += dv

    kvo = (lambda p, i: (0, 0)) if shared_kv else (lambda p, i: (0, p))
    return pl.pallas_call(
        body, name=name, grid=(npairs, S // tq),
        in_specs=[pl.BlockSpec((tq, qw), lambda p, i: (i, p)),
                  pl.BlockSpec((S, qw), lambda p, i: (0, kblk(p))),
                  pl.BlockSpec((S, LANES), lambda p, i: (0, vblk(p))),
                  pl.BlockSpec((tq, LANES), lambda p, i: (i, doblk(p))),
                  pl.BlockSpec((tq, LANES), lambda p, i: (i, p)),
                  pl.BlockSpec((tq, LANES), lambda p, i: (i, p))],
        out_specs=[pl.BlockSpec((tq, qw), lambda p, i: (i, p)),
                   pl.BlockSpec((S, qw), kvo), pl.BlockSpec((S, LANES), kvo)],
        out_shape=[_sds((S, qw * npairs), F32), _sds((S, qw * nkv), F32), _sds((S, LANES * nkv), F32)],
        compiler_params=_cparams(2),
    )(q, k, v, do, o, lse)


def _bias_expand(idx, rel_bias, name):
    tq, kw = idx.shape

    def body(idx_ref, rb_ref, o_ref):
        idx = idx_ref[...]
        for hd in range(DIL_HEADS):
            acc = jnp.full((tq, kw), NEG_INF, F32)
            for u in range(REL_BUCKETS):
                acc = jnp.where(idx == u, rb_ref[u, hd], acc)
            o_ref[hd] = acc

    return pl.pallas_call(
        body, name=name,
        in_specs=[pl.BlockSpec(memory_space=pltpu.VMEM), pl.BlockSpec(memory_space=pltpu.SMEM)],
        out_specs=pl.BlockSpec(memory_space=pltpu.VMEM),
        out_shape=_sds((DIL_HEADS, tq, kw), F32),
    )(idx, rel_bias)


def _bias_reduce(idx, dtab, name):
    tq, kw = idx.shape

    def body(idx_ref, d_ref, o_ref):
        idx = idx_ref[...]
        rowid = lax.broadcasted_iota(jnp.int32, (REL_BUCKETS, kw), 0)
        for hd in range(DIL_HEADS):
            d = d_ref[hd]
            acc = jnp.zeros((REL_BUCKETS, kw), F32)
            for u in range(REL_BUCKETS):
                r = jnp.sum(jnp.where(idx == u, d, 0.0), axis=0, keepdims=True)
                acc = jnp.where(rowid == u, r, acc)
            o_ref[hd] = jnp.sum(acc, axis=1, keepdims=True)

    return pl.pallas_call(
        body, name=name,
        in_specs=[pl.BlockSpec(memory_space=pltpu.VMEM)] * 2, out_specs=pl.BlockSpec(memory_space=pltpu.VMEM),
        out_shape=_sds((DIL_HEADS, REL_BUCKETS, 1), F32),
    )(idx, dtab)


def _dil_window(i, tq, kw, L):
    start = pl.multiple_of(i * tq, DIL_HALF)
    key = start + lax.broadcasted_iota(jnp.int32, (1, kw), 1) - DIL_HALF
    return start, (key >= 0) & (key < L)


def _dil_fwd(qv, kv, vv, tab, *, dil, L, tq, name):
    kw = tq + 2 * DIL_HALF
    npair = DIL_HEADS // 2

    def body(q_ref, k_ref, v_ref, t_ref, o_ref, lse_ref):
        masks = _pair_masks()
        start, valid = _dil_window(pl.program_id(2), tq, kw, L)
        kwin = k_ref[pl.ds(start, kw), :]
        vwin = v_ref[pl.ds(start, kw), :]
        outs, lses = [], []
        for hd in range(2):
            qh = jnp.where(masks[hd], q_ref[...], jnp.zeros_like(q_ref[...]))
            s = lax.dot_general(qh, kwin, _NT, preferred_element_type=F32) + t_ref[hd]
            s = jnp.where(valid, s, NEG_INF)
            mx = jnp.max(s, axis=-1, keepdims=True)
            p = jnp.exp(s - mx)
            l = jnp.sum(p, axis=-1, keepdims=True)
            outs.append(jnp.dot(p.astype(CDT), vwin, preferred_element_type=F32) / l)
            lses.append(jnp.broadcast_to(mx + jnp.log(l), (tq, LANES)))
        o_ref[...] = jnp.where(masks[0], outs[0], outs[1])
        lse_ref[...] = jnp.where(masks[0], lses[0], lses[1])

    blk = pl.BlockSpec((None, tq, LANES), lambda p, c, i: (c, i, p))
    res = pl.BlockSpec((None, L + 2 * DIL_HALF, LANES), lambda p, c, i: (c, 0, p))
    return pl.pallas_call(
        body, name=name, grid=(npair, dil, L // tq),
        in_specs=[blk, res, res, pl.BlockSpec((2, tq, kw), lambda p, c, i: (p, 0, 0))],
        out_specs=[blk] * 2, out_shape=[_sds(qv.shape, F32)] * 2,
        compiler_params=_cparams(3),
    )(qv, kv, vv, tab)


def _dil_bwd(qv, kv, vv, tab, dov, lsev, deltav, *, dil, L, tq, name):
    kw = tq + 2 * DIL_HALF
    npair = DIL_HEADS // 2

    def body(q_ref, k_ref, v_ref, t_ref, do_ref, lse_ref, dl_ref, dq_ref, dk_ref, dv_ref, dt_ref):
        masks = _pair_masks()
        c_id, i_id = pl.program_id(1), pl.program_id(2)
        start, valid = _dil_window(i_id, tq, kw, L)
        kwin = k_ref[pl.ds(start, kw), :]
        vwin = v_ref[pl.ds(start, kw), :]

        @pl.when(i_id == 0)
        def _():
            dk_ref[...] = jnp.zeros_like(dk_ref)
            dv_ref[...] = jnp.zeros_like(dv_ref)

        @pl.when((i_id == 0) & (c_id == 0))
        def _():
            dt_ref[...] = jnp.zeros_like(dt_ref)

        do = do_ref[...]
        dq = jnp.zeros((tq, LANES), F32)
        dk = jnp.zeros((kw, LANES), F32)
        dv = jnp.zeros((kw, LANES), F32)
        for hd in range(2):
            m = masks[hd]
            qh = jnp.where(m, q_ref[...], jnp.zeros_like(q_ref[...]))
            doh = jnp.where(m, do, jnp.zeros_like(do))
            s = lax.dot_general(qh, kwin, _NT, preferred_element_type=F32) + t_ref[hd]
            s = jnp.where(valid, s, NEG_INF)
            p = jnp.exp(s - _head_scalar(lse_ref[...], m))
            dp = lax.dot_general(doh, vwin, _NT, preferred_element_type=F32)
            ds = p * (dp - _head_scalar(dl_ref[...], m))
            dt_ref[hd] += ds
            dsb = ds.astype(CDT)
            dq = dq + jnp.where(m, jnp.dot(dsb, kwin, preferred_element_type=F32), 0.0)
            dk = dk + lax.dot_general(dsb, qh, _TN, preferred_element_type=F32)
            dv = dv + lax.dot_general(p.astype(CDT), doh, _TN, preferred_element_type=F32)
        dq_ref[...] = dq
        dk_ref[pl.ds(start, kw), :] += dk
        dv_ref[pl.ds(start, kw), :] += dv

    blk = pl.BlockSpec((None, tq, LANES), lambda p, c, i: (c, i, p))
    res = pl.BlockSpec((None, L + 2 * DIL_HALF, LANES), lambda p, c, i: (c, 0, p))
    tsp = pl.BlockSpec((2, tq, kw), lambda p, c, i: (p, 0, 0))
    return pl.pallas_call(
        body, name=name, grid=(npair, dil, L // tq),
        in_specs=[blk, res, res, tsp, blk, blk, blk], out_specs=[blk, res, res, tsp],
        out_shape=[_sds(qv.shape, F32), _sds(kv.shape, F32), _sds(kv.shape, F32), _sds(tab.shape, F32)],
        compiler_params=_cparams(3),
    )(qv, kv, vv, tab, dov, lsev, deltav)


def _mix_weights(l1, l2, l3):
    mx = jnp.maximum(jnp.maximum(l1, l2), l3)
    e1, e2, e3 = jnp.exp(l1 - mx), jnp.exp(l2 - mx), jnp.exp(l3 - mx)
    inv = 1.0 / (e1 + e2 + e3)
    return e1 * inv, e2 * inv, e3 * inv


def _branch_specs(S, ts):
    dils = [d for _, d in DIL_BRANCHES]
    return dils, [_res_spec(d, ts // d) for d in dils], [(d, S // d, DIL_W) for d in dils]


def _dil_mix_fwd(os, ls):
    S = os[0].shape[0] * os[0].shape[1]
    ts = min(256, S)
    dils, specs, _ = _branch_specs(S, ts)

    def body(o1, o2, o3, l1, l2, l3, out, scr):
        o1, o2, o3, l1, l2, l3 = [_by_token(r, scr, d) for r, d in zip((o1, o2, o3, l1, l2, l3), dils + dils)]
        w1, w2, w3 = _mix_weights(l1, l2, l3)
        out[...] = (w1 * o1 + w2 * o2 + w3 * o3).astype(CDT)

    return pl.pallas_call(
        body, name="dil_mix_fwd", grid=(S // ts,), in_specs=specs + specs, out_specs=_row(ts, DIL_W),
        out_shape=_sds((S, DIL_W), CDT), scratch_shapes=[_TOKEN_SCRATCH(ts)],
        compiler_params=_cparams(1),
    )(*os, *ls)


def _dil_mix_bwd(dcat, os, ls, j384):
    S = os[0].shape[0] * os[0].shape[1]
    ts = min(256, S)
    dils, specs, shapes = _branch_specs(S, ts)

    def body(do_ref, o1, o2, o3, l1, l2, l3, j_ref, d1, d2, d3, e1, e2, e3, scr):
        o1, o2, o3, l1, l2, l3 = [_by_token(r, scr, d) for r, d in zip((o1, o2, o3, l1, l2, l3), dils + dils)]
        ws = _mix_weights(l1, l2, l3)
        do = do_ref[...]
        o = ws[0] * o1 + ws[1] * o2 + ws[2] * o3
        dot = _headsum(do * o, j_ref[...])
        for w, d, d_o, e_o in zip(ws, dils, (d1, d2, d3), (e1, e2, e3)):
            _by_residue(w * do, scr, d_o, d)
            _by_residue(w * dot, scr, e_o, d)

    return pl.pallas_call(
        body, name="dil_mix_bwd", grid=(S // ts,),
        in_specs=[_row(ts, DIL_W, 1)] + specs + specs + [_full(j384.shape)],
        out_specs=specs + specs,
        out_shape=[_sds(s, CDT) for s in shapes] + [_sds(s, F32) for s in shapes],
        scratch_shapes=[_TOKEN_SCRATCH(ts)],
        compiler_params=_cparams(1),
    )(dcat, *os, *ls, j384)


def _adamw_math(w, g, m, v):
    m = ADAM_B1 * m + (1.0 - ADAM_B1) * g
    v = ADAM_B2 * v + (1.0 - ADAM_B2) * (g * g)
    m_hat = m / (1.0 - ADAM_B1 ** ADAM_STEP)
    v_hat = v / (1.0 - ADAM_B2 ** ADAM_STEP)
    delta = -ADAM_LR * (m_hat / (jnp.sqrt(v_hat) + ADAM_EPS) + ADAM_WD * w)
    return delta, m, v


def _pick8(n, target):
    best = None
    for t in range(16, min(n, target) + 1, 16):
        if n % t == 0:
            best = t
    return best if best is not None else n


_ELEMS_PER_BLOCK = 256 * 1024


def _lead_spec(a, b):
    ta = _pick8(a, max(16, _ELEMS_PER_BLOCK // b))
    return ta, pl.BlockSpec((None, ta, b), lambda l, i: (l, i, 0))


def _adamw(w, reds, sibs, m, v, owner, name):
    L, a, b = w.shape
    ta, spec = _lead_spec(a, b)
    gspec = pl.BlockSpec((ta, b), lambda l, i: (i, 0))

    def body(w_ref, r0_ref, r1_ref, s0_ref, s1_ref, m_ref, v_ref, g_o, d_o, m_o, v_o):
        mine = lax.axis_index("c") == owner
        g0 = jnp.where(mine, r0_ref[...], s0_ref[...])
        g1 = jnp.where(mine, r1_ref[...], s1_ref[...])
        g = jnp.where(pl.program_id(0) == 0, g0, g1)
        d, mm, vv = _adamw_math(w_ref[...], g, m_ref[...], v_ref[...])
        g_o[...] = g
        d_o[...] = d
        m_o[...] = mm
        v_o[...] = vv

    return pl.pallas_call(
        body, name=name, grid=(L, a // ta), in_specs=[spec] + [gspec] * 4 + [spec, spec], out_specs=[spec] * 4,
        out_shape=[_sds(w.shape, F32)] * 4, compiler_params=_cparams(2),
    )(w, *reds, *sibs, m, v)


def _adamw_small(w, gall, m, v):
    R = w.shape[0]

    def body(w_ref, g_ref, m_ref, v_ref, g_o, d_o, m_o, v_o):
        g = g_ref[0]
        for k in range(1, 8):
            g = g + g_ref[k]
        d, mm, vv = _adamw_math(w_ref[...], g, m_ref[...], v_ref[...])
        g_o[...] = g
        d_o[...] = d
        m_o[...] = mm
        v_o[...] = vv

    vm = pl.BlockSpec(memory_space=pltpu.VMEM)
    return pl.pallas_call(
        body, name="adamw_small", in_specs=[vm] * 4, out_specs=[vm] * 4, out_shape=[_sds((R, LANES), F32)] * 4,
    )(w, gall, m, v)


def _sum_pair(g, t, owner, name):
    n, a, b = t.shape
    ta = _pick8(a, max(16, _ELEMS_PER_BLOCK // b))
    spec = pl.BlockSpec((None, ta, b), lambda k, i, own: (k * own[0], i * own[0], 0))

    def body(own_ref, g_ref, t_ref, o_ref):
        @pl.when(own_ref[0] == 1)
        def _():
            o_ref[...] = (g_ref[...].astype(F32) + t_ref[...].astype(F32)).astype(o_ref.dtype)

    return pl.pallas_call(
        body, name=name, out_shape=_sds(t.shape, WIRE),
        grid_spec=pltpu.PrefetchScalarGridSpec(num_scalar_prefetch=1, grid=(n, a // ta), in_specs=[spec] * 2,
                                               out_specs=spec),
        compiler_params=_cparams(2),
    )(_is_core(owner), g, t)


def _sum_chips(pair, t, owner, name):
    _, a, b = t.shape
    ta = _pick8(a, max(16, _ELEMS_PER_BLOCK // b))

    def body(own_ref, p_ref, t_ref, o_ref):
        @pl.when(own_ref[0] == 1)
        def _():
            me = 2 * lax.axis_index("x") + lax.axis_index("y")
            acc = p_ref[me].astype(F32)
            for k in range(3):
                acc = acc + t_ref[k].astype(F32)
            o_ref[...] = acc

    return pl.pallas_call(
        body, name=name, out_shape=_sds((a, b), F32),
        grid_spec=pltpu.PrefetchScalarGridSpec(
            num_scalar_prefetch=1, grid=(a // ta,),
            in_specs=[pl.BlockSpec((4, ta, b), lambda i, own: (0, i * own[0], 0)),
                      pl.BlockSpec((3, ta, b), lambda i, own: (0, i * own[0], 0))],
            out_specs=pl.BlockSpec((ta, b), lambda i, own: (i * own[0], 0))),
        compiler_params=_cparams(1),
    )(_is_core(owner), pair, t)


def _is_core(core):
    return (lax.axis_index("c") == core).astype(jnp.int32).reshape(1)


_HBM = pl.BlockSpec(memory_space=pltpu.HBM)


def _place():
    x, y, c = lax.axis_index("x"), lax.axis_index("y"), lax.axis_index("c")
    chips = [(1 - x, y), (x, 1 - y), (1 - x, 1 - y)]
    return x, y, c, chips


def _remote(src, dst, ssem, rsem, to):
    return pltpu.make_async_remote_copy(src_ref=src, dst_ref=dst, send_sem=ssem, recv_sem=rsem, device_id=to,
                                        device_id_type=MESH_ID)


def _dma_sems(n):
    return pltpu.SemaphoreType.DMA((n,))


_SEM = pl.BlockSpec(memory_space=pltpu.SEMAPHORE)
_ANY = pl.BlockSpec(memory_space=pl.ANY)
_EFFECT = pltpu.SideEffectType.DATAFLOW_SIDE_EFFECTING
_BIG = ("w_in", "mla_w_uq", "mla_w_ukv", "w_out", "ffn_w_gate", "ffn_w_up", "ffn_w_down")
_OWNER = dict(zip(_BIG, (1, 0, 0, 1, 0, 0, 1)))
_ATTN_WEIGHTS, _FFN_WEIGHTS = _BIG[:4], _BIG[4:]


def _hbm(a):
    return pltpu.with_memory_space_constraint(a, pltpu.HBM)


def _per_core(c, owners, fn):
    for g in range(2):
        mine = tuple(p for p, o in enumerate(owners) if o == g)
        theirs = tuple(p for p, o in enumerate(owners) if o != g)
        pl.when(c == g)(functools.partial(fn, mine, theirs))


def _token_spec():
    return pl.BlockSpec(memory_space=pltpu.VMEM), _sds((8, LANES), F32)


def _gather_start(shards, owners, layer, tag):
    n = len(shards)
    lands = [_hbm(lax.empty((4,) + s.shape[1:], s.dtype)) for s in shards]

    def body(*refs):
        w_refs, l_refs = refs[:n], refs[n:2 * n]
        ssem, rsem, token = refs[2 * n], refs[2 * n + 1], refs[-1]
        x, y, c, chips = _place()
        me = 2 * x + y

        def send(mine, _):
            for i in mine:
                for j, (cx, cy) in enumerate(chips):
                    _remote(w_refs[i].at[layer], l_refs[i].at[me], ssem.at[3 * i + j], rsem.at[3 * i + j],
                            (cx, cy, c)).start()

        _per_core(c, owners, send)
        token[...] = jnp.zeros_like(token)

    tspec, tshape = _token_spec()
    out = pl.pallas_call(
        body, name=f"gather_start_{tag}", in_specs=[_HBM] * (2 * n),
        out_specs=[_SEM, _SEM] + [_HBM] * n + [tspec],
        out_shape=[_dma_sems(3 * n), _dma_sems(3 * n)] + [pltpu.HBM(l.shape, l.dtype) for l in lands] + [tshape],
        input_output_aliases={n + i: 2 + i for i in range(n)},
        compiler_params=pltpu.CompilerParams(has_side_effects=_EFFECT),
    )(*[_hbm(s) for s in shards], *lands)
    return out[0], out[1], list(out[2:2 + n]), out[-1]


def _gather_wait(ssem, rsem, shards, lands, after, owners, layer, tag):
    n = len(shards)

    def body(*refs):
        w_refs, l_refs = refs[:n], refs[n:2 * n]
        ssem, rsem = refs[2 * n], refs[2 * n + 1]
        x, y, c, chips = _place()

        def wait(mine, _):
            for i in mine:
                for j, (cx, cy) in enumerate(chips):
                    cp = _remote(w_refs[i].at[layer], l_refs[i].at[2 * cx + cy], ssem.at[3 * i + j], rsem.at[3 * i + j],
                                 (cx, cy, c))
                    cp.wait_send()
                    cp.wait_recv()

        _per_core(c, owners, wait)

    return list(pl.pallas_call(
        body, name=f"gather_wait_{tag}", in_specs=[_HBM] * (2 * n) + [_SEM, _SEM, _ANY], out_specs=[_HBM] * n,
        out_shape=[pltpu.HBM(l.shape, l.dtype) for l in lands],
        input_output_aliases={n + i: i for i in range(n)},
        compiler_params=pltpu.CompilerParams(has_side_effects=_EFFECT),
    )(*[_hbm(s) for s in shards], *lands, ssem, rsem, after))


def _gather_finish(shards, lands, owners, layer, tag):
    n = len(shards)

    def body(*refs):
        w_refs, g_refs = refs[:n], refs[2 * n:3 * n]
        ssem, rsem = refs[3 * n:]
        x, y, c, chips = _place()
        me = 2 * x + y
        sib = (x, y, 1 - c)
        owns = [_remote(w.at[layer], g.at[me], ssem.at[i], rsem.at[i], sib) for i, (w, g) in enumerate(zip(w_refs, g_refs))]
        for cp in owns:
            cp.start()

        def forward(mine, theirs):
            def blk(i, j):
                b = g_refs[i].at[2 * chips[j][0] + chips[j][1]]
                return _remote(b, b, ssem.at[n + 3 * i + j], rsem.at[n + 3 * i + j], sib)

            for i in mine:
                for j in range(3):
                    blk(i, j).start()
            for i in theirs:
                for j in range(3):
                    blk(i, j).wait_recv()
            for i in mine:
                for j in range(3):
                    blk(i, j).wait_send()

        _per_core(c, owners, forward)
        for cp in owns:
            cp.wait_recv()
            cp.wait_send()

    return list(pl.pallas_call(
        body, name=f"gather_finish_{tag}", in_specs=[_HBM] * (2 * n), out_specs=[_HBM] * n,
        out_shape=[_sds(l.shape, l.dtype) for l in lands], input_output_aliases={n + i: i for i in range(n)},
        scratch_shapes=[_dma_sems(4 * n), _dma_sems(4 * n)],
    )(*shards, *lands))


def _rs_to_owner(grads, owners, tag):
    n = len(grads)

    def body(*refs):
        g_refs, t_refs = refs[:n], refs[n:2 * n]
        ssem, rsem = refs[2 * n:]
        x, y, c, _ = _place()

        def swap(mine, theirs):
            cps = [_remote(g_refs[i], t_refs[i], ssem.at[i], rsem.at[i], (x, y, 1 - c)) for i in theirs]
            for cp in cps:
                cp.start()
            for i in mine:
                _remote(g_refs[i], t_refs[i], ssem.at[i], rsem.at[i], (x, y, 1 - c)).wait_recv()
            for cp in cps:
                cp.wait_send()

        _per_core(c, owners, swap)

    return list(pl.pallas_call(
        body, name=f"rs_to_owner_{tag}", in_specs=[_HBM] * n, out_specs=[_HBM] * n,
        out_shape=[_sds(g.shape, g.dtype) for g in grads], scratch_shapes=[_dma_sems(n), _dma_sems(n)],
    )(*grads))


def _a2a_start(pairs, owners, tag):
    n = len(pairs)
    lands = [_hbm(lax.empty((3,) + p.shape[1:], p.dtype)) for p in pairs]

    def body(*refs):
        a_refs, t_refs = refs[:n], refs[n:2 * n]
        ssem, rsem, token = refs[2 * n], refs[2 * n + 1], refs[-1]
        x, y, c, chips = _place()

        def send(mine, _):
            for i in mine:
                for j, (cx, cy) in enumerate(chips):
                    _remote(a_refs[i].at[2 * cx + cy], t_refs[i].at[j], ssem.at[3 * i + j], rsem.at[3 * i + j],
                            (cx, cy, c)).start()

        _per_core(c, owners, send)
        token[...] = jnp.zeros_like(token)

    tspec, tshape = _token_spec()
    out = pl.pallas_call(
        body, name=f"rs_a2a_start_{tag}", in_specs=[_HBM] * (2 * n),
        out_specs=[_SEM, _SEM] + [_HBM] * n + [tspec],
        out_shape=[_dma_sems(3 * n), _dma_sems(3 * n)] + [pltpu.HBM(l.shape, l.dtype) for l in lands] + [tshape],
        input_output_aliases={n + i: 2 + i for i in range(n)},
        compiler_params=pltpu.CompilerParams(has_side_effects=_EFFECT),
    )(*[_hbm(p) for p in pairs], *lands)
    return out[0], out[1], list(out[2:2 + n]), out[-1]


def _a2a_wait(ssem, rsem, pairs, lands, after, owners, tag):
    n = len(pairs)

    def body(*refs):
        a_refs, t_refs = refs[:n], refs[n:2 * n]
        ssem, rsem = refs[2 * n], refs[2 * n + 1]
        x, y, c, chips = _place()

        def wait(mine, _):
            for i in mine:
                for j, (cx, cy) in enumerate(chips):
                    cp = _remote(a_refs[i].at[2 * cx + cy], t_refs[i].at[j], ssem.at[3 * i + j], rsem.at[3 * i + j],
                                 (cx, cy, c))
                    cp.wait_send()
                    cp.wait_recv()

        _per_core(c, owners, wait)

    return list(pl.pallas_call(
        body, name=f"rs_a2a_wait_{tag}", in_specs=[_HBM] * (2 * n) + [_SEM, _SEM, _ANY], out_specs=[_HBM] * n,
        out_shape=[pltpu.HBM(l.shape, l.dtype) for l in lands],
        input_output_aliases={n + i: i for i in range(n)},
        compiler_params=pltpu.CompilerParams(has_side_effects=_EFFECT),
    )(*[_hbm(p) for p in pairs], *lands, ssem, rsem, after))


def _rs_from_owner(reds, owners):
    n = len(reds)

    def body(*refs):
        q_refs, o_refs = refs[:n], refs[n:2 * n]
        ssem, rsem = refs[2 * n:]
        x, y, c, _ = _place()

        def swap(mine, theirs):
            cps = [_remote(q_refs[k], o_refs[k], ssem.at[k], rsem.at[k], (x, y, 1 - c)) for k in mine]
            for cp in cps:
                cp.start()
            for k in theirs:
                _remote(q_refs[k], o_refs[k], ssem.at[k], rsem.at[k], (x, y, 1 - c)).wait_recv()
            for cp in cps:
                cp.wait_send()

        _per_core(c, owners, swap)

    return list(pl.pallas_call(
        body, name="rs_from_owner", in_specs=[_HBM] * n, out_specs=[_HBM] * n,
        out_shape=[_sds(q.shape, q.dtype) for q in reds], scratch_shapes=[_dma_sems(n), _dma_sems(n)],
    )(*reds))


def _gather_small(s):
    R, _ = s.shape

    def body(s_ref, o_ref, ssem, rsem, lsem):
        x, y, c, _ = _place()
        me = 4 * x + 2 * y + c
        own = pltpu.make_async_copy(s_ref, o_ref.at[me], lsem)
        own.start()
        sends = []
        for k in range(1, 8):
            px, py, pc = x ^ (k >> 2), y ^ ((k >> 1) & 1), c ^ (k & 1)
            cp = _remote(s_ref, o_ref.at[me], ssem.at[k - 1], rsem.at[k - 1], (px, py, pc))
            cp.start()
            sends.append(cp)
        for k in range(1, 8):
            px, py, pc = x ^ (k >> 2), y ^ ((k >> 1) & 1), c ^ (k & 1)
            blk = o_ref.at[4 * px + 2 * py + pc]
            _remote(blk, blk, ssem.at[k - 1], rsem.at[k - 1], (px, py, pc)).wait_recv()
        for cp in sends:
            cp.wait_send()
        own.wait()

    vm = pl.BlockSpec(memory_space=pltpu.VMEM)
    return pl.pallas_call(
        body, name="gather_small", in_specs=[vm], out_specs=vm, out_shape=_sds((8, R, LANES), s.dtype),
        scratch_shapes=[pltpu.SemaphoreType.DMA((7,)), pltpu.SemaphoreType.DMA((7,)), pltpu.SemaphoreType.DMA],
    )(s)


_COL_SHARDED =("w_in", "mla_w_uq", "mla_w_ukv", "ffn_w_gate", "ffn_w_up")
_SMALL = ("mla_q_norm", "mla_kv_norm", "gqa_q_norm", "gqa_k_norm", "rel_bias", "ln1_g", "ln1_b", "ln2_g", "ln2_b")


def _pack_flat(arrs, align):
    flat = jnp.concatenate([a.reshape(-1) for a in arrs])
    pad = (-flat.shape[0]) % align
    return jnp.pad(flat, (0, pad)) if pad else flat


def _unpack_flat(flat, shapes):
    out, off = [], 0
    for s in shapes:
        n = int(np.prod(s))
        out.append(flat[off:off + n].reshape(s))
        off += n
    return out


def _perm_gqa_rows(w):
    return jnp.concatenate([w[:832], w[896:960], w[832:896], w[960:]], axis=0)


def _local_step(x, target, small, depth, weights_of_layer, grads_done):
    S, D = x.shape
    alpha = (2.0 * depth) ** 0.25
    in_idx, uq_idx, ukv_idx = _in_cols(), _uq_cols(), _ukv_cols()
    win, wuq, wukv, wout, wg, wu, wdn = ([None] * depth for _ in range(7))

    tm, tg = _rope_tables(S)
    j256, j384 = _head_ones(256), _head_ones(384)
    mla_scale = (64 + MLA_ROPE_DIM) ** -0.5
    branches = []
    for (_, dil) in DIL_BRANCHES:
        L = S // dil
        tq = min(256, L)
        idx = jnp.asarray(_branch_bucket_idx(tq, dil))
        branches.append((dil, L, tq, idx))
    tabs = [_bias_expand(idx, small["rel_bias"], name=f"bias_expand_{b}") for b, (_, _, _, idx) in enumerate(branches)]

    def padded(a):
        z = jnp.zeros((DIL_HALF, a.shape[1]), a.dtype)
        return jnp.concatenate([z, a, z], axis=0)[None]

    saved = []
    xf, xb = x, x.astype(CDT)
    for l in range(depth):
        W, token = weights_of_layer(l, "attn", xb)
        win[l] = _rows_from_shards(W["w_in"], in_idx)
        wuq[l] = _rows_from_shards(W["mla_w_uq"], uq_idx)
        wukv[l] = _rows_from_shards(W["mla_w_ukv"], ukv_idx)
        wout[l] = _perm_gqa_rows(W["w_out"].reshape(-1, D))
        gq, gkv = small["mla_q_norm"][l][None], small["mla_kv_norm"][l][None]
        if token is not None:
            gq = gq + token[0, 0]
        ggq = jnp.tile(small["gqa_q_norm"][l], 4)[None]
        ggk = jnp.tile(small["gqa_k_norm"][l], 2)[None]
        h = _mm(xb, win[l], tb=True, name="mm_in")
        cq, ckv, kr, qd, kd, vd, qg, kg, vg, *strided = _prep_fwd(h, gq, gkv, ggq, ggk, tm, tg, j256)
        qkv = [(qd[None], padded(kd), padded(vd))] + [tuple(strided[3 * b:3 * b + 3]) for b in range(len(DIL_STRIDES))]
        qa = _mm(cq, wuq[l], tb=True, name="mm_uq")
        kvp = _mm(ckv, wukv[l], tb=True, out_dtype=CDT, name="mm_ukv")
        qm, km = _mla_prep_fwd(qa, kvp, kr, tm, mla_scale)
        oa, lsa = _attn_fwd(qm, km, kvp, split=True, npairs=3, kblk=lambda p: p, vblk=lambda p: 6 + p,
                            name="mla_attn_fwd")
        oc, lsc = _attn_fwd(qg, kg, vg, split=False, npairs=2, kblk=lambda p: 0, vblk=lambda p: 0,
                            name="gqa_attn_fwd")
        obs, lbs = [], []
        for b, (dil, L, tq, _) in enumerate(branches):
            o_b, l_b = _dil_fwd(*qkv[b], tabs[b], dil=dil, L=L, tq=tq, name=f"dil_fwd_{b}")
            obs.append(o_b)
            lbs.append(l_b)
        ob = _dil_mix_fwd(obs, lbs)
        cat = jnp.concatenate([oa, ob, oc], axis=1)
        mix = _mm(cat, wout[l], name="mm_out")
        x1, x1b, z1 = _ln_fwd(xf, mix, small["ln1_g"][l][None], small["ln1_b"][l][None], alpha, name="ln1_fwd")
        W, _ = weights_of_layer(l, "ffn", x1b)
        wg[l], wu[l], wdn[l] = W["ffn_w_gate"], W["ffn_w_up"], W["ffn_w_down"]
        g3, u3, act = _ffn_up(x1b, wg[l], wu[l])
        ff = _mm(act, wdn[l], ga=True, gb=True, name="mm_down")
        x2, x2b, z2 = _ln_fwd(x1, ff, small["ln2_g"][l][None], small["ln2_b"][l][None], alpha, name="ln2_fwd")
        saved.append(dict(xb=xb, h=h, cq=cq, ckv=ckv, qg=qg, kg=kg, vg=vg, kvp=kvp, qm=qm, km=km, oa=oa, lsa=lsa,
                          oc=oc, lsc=lsc, obs=obs, lbs=lbs, qkv=qkv, cat=cat, z1=z1, x1b=x1b, g3=g3, u3=u3, act=act, z2=z2,
                          gq=gq, gkv=gkv, ggq=ggq, ggk=ggk))
        xf, xb = x2, x2b

    gW = {k: [None] * depth for k in _BIG}
    gS = {k: [None] * depth for k in ("mla_q_norm", "mla_kv_norm", "gqa_q_norm", "gqa_k_norm", "ln1_g", "ln1_b", "ln2_g",
                                      "ln2_b")}
    g_rel = None
    dya, dyb = xf, target
    token = None
    for l in reversed(range(depth)):
        sv = saved[l]
        ln2_g = small["ln2_g"][l][None]
        if token is not None:
            ln2_g = ln2_g + token[0, 0]
        if l == depth - 1:
            dz2, dz2b, gS["ln2_g"][l], gS["ln2_b"][l], loss = _ln_bwd(dya, dyb, sv["z2"], ln2_g, alpha,
                                                                       name="ln2_bwd_loss", loss_head=True)
        else:
            dz2, dz2b, gS["ln2_g"][l], gS["ln2_b"][l] = _ln_bwd(dya, dyb, sv["z2"], ln2_g, alpha, name="ln2_bwd")
        gW["ffn_w_down"][l] = _mm(sv["act"], dz2b, ta=True, ga=True, go=True, out_dtype=WIRE, name="mm_down_dw")
        dg3, du3 = _ffn_down_dx(dz2b, wdn[l], sv["g3"], sv["u3"])
        gW["ffn_w_gate"][l] = _mm(dg3, sv["x1b"], ta=True, ga=True, go=True, out_dtype=WIRE, name="mm_gate_dw")
        gW["ffn_w_up"][l] = _mm(du3, sv["x1b"], ta=True, ga=True, go=True, out_dtype=WIRE, name="mm_up_dw")
        dx1 = _ffn_up_dx(dg3, du3, wg[l], wu[l])
        token = grads_done(l, "ffn", {n: gW[n][l] for n in _FFN_WEIGHTS})
        ln1_g = small["ln1_g"][l][None]
        if token is not None:
            ln1_g = ln1_g + token[0, 0]
        dz1, dz1b, gS["ln1_g"][l], gS["ln1_b"][l] = _ln_bwd(dx1, dz2, sv["z1"], ln1_g, alpha, name="ln1_bwd")
        gW["w_out"][l] = _perm_gqa_rows(_mm(sv["cat"], dz1b, ta=True, out_dtype=WIRE, name="mm_out_dw")).reshape(4, -1, D)
        dcat = _mm(dz1b, wout[l], tb=True, name="mm_out_dx")
        dqg, dkg, dvg = _attn_bwd(sv["qg"], sv["kg"], sv["vg"], dcat, sv["oc"], sv["lsc"], split=False, npairs=2,
                                  kblk=lambda p: 0, vblk=lambda p: 0, doblk=lambda p: 6 + p, shared_kv=True,
                                  name="gqa_attn_bwd")
        dqm, dkm, dvm = _attn_bwd(sv["qm"], sv["km"], sv["kvp"], dcat, sv["oa"], sv["lsa"], split=True, npairs=3,
                                  kblk=lambda p: p, vblk=lambda p: 6 + p, doblk=lambda p: p, shared_kv=False,
                                  name="mla_attn_bwd")
        dqa, dkvp, dkr = _mla_prep_bwd(dqm, dkm, dvm, tm, mla_scale)
        gW["mla_w_uq"][l] = _rows_to_shards(_mm(dqa, sv["cq"], ta=True, out_dtype=WIRE, name="mm_uq_dw"), uq_idx, MLA_HEADS * 96)
        dcq = _mm(dqa, wuq[l], name="mm_uq_dx")
        gW["mla_w_ukv"][l] = _rows_to_shards(_mm(dkvp, sv["ckv"], ta=True, out_dtype=WIRE, name="mm_ukv_dw"), ukv_idx, MLA_HEADS * 128)
        dckv = _mm(dkvp, wukv[l], name="mm_ukv_dx")
        mixb = _dil_mix_bwd(dcat, sv["obs"], sv["lbs"], j384)
        ddq, ddk, ddv = [], [], []
        for b, (dil, L, tq, idx) in enumerate(branches):
            dq_b, dk_b, dv_b, dtab = _dil_bwd(*sv["qkv"][b], tabs[b], mixb[b], sv["lbs"][b], mixb[3 + b], dil=dil, L=L,
                                              tq=tq, name=f"dil_bwd_{b}")
            if dil == 1:
                dq_b, dk_b, dv_b = dq_b[0], dk_b[0, DIL_HALF:DIL_HALF + S], dv_b[0, DIL_HALF:DIL_HALF + S]
            ddq.append(dq_b)
            ddk.append(dk_b)
            ddv.append(dv_b)
            g_b = _bias_reduce(idx, dtab, name=f"bias_reduce_{b}")[:, :, 0].T
            g_rel = g_b if g_rel is None else g_rel + g_b
        dh, n1, n2, n3, n4 = _prep_bwd(sv["h"], dcq, dckv, dkr, ddq, ddk, ddv, dqg, dkg, dvg, sv["gq"], sv["gkv"],
                                       sv["ggq"], sv["ggk"], tg, j256)
        gS["mla_q_norm"][l], gS["mla_kv_norm"][l] = n1[0], n2[0]
        gS["gqa_q_norm"][l] = n3[0].reshape(4, 64).sum(0)
        gS["gqa_k_norm"][l] = n4[0].reshape(2, 64).sum(0)
        gW["w_in"][l] = _rows_to_shards(_mm(dh, sv["xb"], ta=True, out_dtype=WIRE, name="mm_in_dw"), in_idx, IN_W)
        dya = _mm(dh, win[l], name="mm_in_dx")
        dyb = dz1
        token = grads_done(l, "attn", {n: gW[n][l] for n in _ATTN_WEIGHTS})
    grad_x = _axpy(dya, dyb, alpha, name="grad_x")

    gsmall = {k: jnp.stack([a.reshape(-1) for a in v]) for k, v in gS.items()}
    gsmall["rel_bias"] = g_rel
    return loss, grad_x, gsmall


_ORDER = ("w_in", "mla_q_norm", "mla_kv_norm", "mla_w_uq", "mla_w_ukv", "gqa_q_norm", "gqa_k_norm", "rel_bias", "w_out",
          "ln1_g", "ln1_b", "ffn_w_gate", "ffn_w_up", "ffn_w_down", "ln2_g", "ln2_b")


def kernel(x, w_in, mla_q_norm, mla_kv_norm, mla_w_uq, mla_w_ukv, gqa_q_norm, gqa_k_norm, rel_bias, w_out, ln1_g, ln1_b, ffn_w_gate, ffn_w_up, ffn_w_down, ln2_g, ln2_b, loss_target, m_w_in, m_mla_q_norm, m_mla_kv_norm, m_mla_w_uq, m_mla_w_ukv, m_gqa_q_norm, m_gqa_k_norm, m_rel_bias, m_w_out, m_ln1_g, m_ln1_b, m_ffn_w_gate, m_ffn_w_up, m_ffn_w_down, m_ln2_g, m_ln2_b, v_w_in, v_mla_q_norm, v_mla_kv_norm, v_mla_w_uq, v_mla_w_ukv, v_gqa_q_norm, v_gqa_k_norm, v_rel_bias, v_w_out, v_ln1_g, v_ln1_b, v_ffn_w_gate, v_ffn_w_up, v_ffn_w_down, v_ln2_g, v_ln2_b):
    wts = dict(zip(_ORDER, (w_in, mla_q_norm, mla_kv_norm, mla_w_uq, mla_w_ukv, gqa_q_norm, gqa_k_norm, rel_bias, w_out,
                            ln1_g, ln1_b, ffn_w_gate, ffn_w_up, ffn_w_down, ln2_g, ln2_b)))
    mom = dict(zip(_ORDER, (m_w_in, m_mla_q_norm, m_mla_kv_norm, m_mla_w_uq, m_mla_w_ukv, m_gqa_q_norm, m_gqa_k_norm,
                            m_rel_bias, m_w_out, m_ln1_g, m_ln1_b, m_ffn_w_gate, m_ffn_w_up, m_ffn_w_down, m_ln2_g,
                            m_ln2_b)))
    var = dict(zip(_ORDER, (v_w_in, v_mla_q_norm, v_mla_kv_norm, v_mla_w_uq, v_mla_w_ukv, v_gqa_q_norm, v_gqa_k_norm,
                            v_rel_bias, v_w_out, v_ln1_g, v_ln1_b, v_ffn_w_gate, v_ffn_w_up, v_ffn_w_down, v_ln2_g,
                            v_ln2_b)))
    small_shapes = [wts[n].shape for n in _SMALL]
    for d in (wts, mom, var):
        for n in _COL_SHARDED:
            d[n] = d[n].transpose(0, 2, 1)

    depth = 2
    shards = {n: wts[n].astype(WIRE) for n in _BIG}
    flying = {}

    def start_gather(names, l, tag):
        own = tuple(_OWNER[n] for n in names)
        sh = [shards[n] for n in names]
        ssem, rsem, lands, token = _gather_start(sh, own, l, tag)
        return (names, own, sh, ssem, rsem, lands, l, tag), token

    def end_gather(flight, after):
        names, own, sh, ssem, rsem, lands, l, tag = flight
        got = _gather_finish(sh, _gather_wait(ssem, rsem, sh, lands, after, own, l, tag), own, l, tag)
        return {n: g.astype(CDT) for n, g in zip(names, got)}

    def weights_of_layer(l, part, after):
        if (l, part) == (0, "attn"):
            got = end_gather(start_gather(_ATTN_WEIGHTS, 0, "attn0")[0], after)
            flying["ffn0"], t0 = start_gather(_FFN_WEIGHTS, 0, "ffn0")
            flying["layer1"], t1 = start_gather(_BIG, 1, "layer1")
            return got, t0 + t1
        if (l, part) == (0, "ffn"):
            return end_gather(flying.pop("ffn0"), after), None
        if part == "attn":
            flying["w1"] = end_gather(flying.pop("layer1"), after)
        return flying["w1"], None

    def grads_done(l, part, grads):
        names = tuple(grads)
        own = tuple(_OWNER[n] for n in names)
        tag = f"{part}{l}"
        gl = [grads[n] for n in names]
        theirs = _rs_to_owner(gl, own, tag)
        pairs = [_sum_pair(g, t, o, name=f"rs_pair_sum_{n}") for n, g, t, o in zip(names, gl, theirs, own)]
        ssem, rsem, lands, token = _a2a_start(pairs, own, tag)
        flying[tag] = (names, own, ssem, rsem, pairs, lands)
        return token

    small = {n: wts[n] for n in _SMALL}
    loss, grad_x, gsmall = _local_step(x[0], loss_target[0], small, depth, weights_of_layer, grads_done)

    reds = {}
    for l in reversed(range(depth)):
        for part in ("ffn", "attn"):
            tag = f"{part}{l}"
            names, own, ssem, rsem, pairs, lands = flying.pop(tag)
            got = _a2a_wait(ssem, rsem, pairs, lands, grad_x, own, tag)
            for n, p, t, o in zip(names, pairs, got, own):
                reds[n, l] = _sum_chips(p, t, o, name=f"rs_sum_chips_{n}")
    order = [(n, l) for l in range(depth) for n in _BIG]
    sibs = dict(zip(order, _rs_from_owner([reds[k] for k in order], tuple(_OWNER[n] for n, _ in order))))

    sflat = _pack_flat([gsmall[n].reshape(-1) for n in _SMALL], 8 * LANES)
    rs = sflat.shape[0] // LANES
    sall = _gather_small(sflat.reshape(rs, LANES))

    def packed(d):
        return _pack_flat([d[n] for n in _SMALL], 8 * LANES).reshape(rs, LANES)

    outs = {tag: {} for tag in ("grad", "delta", "new_m", "new_v")}
    for n in _BIG:
        res = _adamw(wts[n], [reds[n, 0], reds[n, 1]], [sibs[n, 0], sibs[n, 1]], mom[n], var[n], _OWNER[n],
                     name=f"adamw_{n}")
        for tag, r in zip(("grad", "delta", "new_m", "new_v"), res):
            outs[tag][n] = r.transpose(0, 2, 1) if n in _COL_SHARDED else r
    for tag, smallflat in zip(("grad", "delta", "new_m", "new_v"), _adamw_small(packed(wts), sall, packed(mom), packed(var))):
        outs[tag].update(zip(_SMALL, _unpack_flat(smallflat.reshape(-1), small_shapes)))

    total = lax.psum(loss[0, 0], ("x", "y", "c"))
    return (total, grad_x[None], *[outs["grad"][n] for n in _ORDER], *[outs["delta"][n] for n in _ORDER],
            *[outs["new_m"][n] for n in _ORDER], *[outs["new_v"][n] for n in _ORDER])
```

```python
import functools
import math

import numpy as np
import jax
import jax.numpy as jnp
from jax import lax
from jax.experimental import pallas as pl
from jax.experimental.pallas import tpu as pltpu

F32 = jnp.float32
CDT = jnp.bfloat16
WIRE = jnp.bfloat16

HEAD_DIM = 64
GRID_W = 64
ROPE_THETA = 10000.0
MLA_HEADS = 6
MLA_Q_RANK = 256
MLA_KV_RANK = 128
MLA_ROPE_DIM = 32
DIL_HEADS = 6
DIL_BRANCHES = ((128, 1), (512, 4), (2048, 16))
DIL_HALF = 64
GQA_Q_HEADS = 4
REL_BUCKETS = 32
REL_MAX_DIST = 1024
NEG_INF = -1e30
LANES = 128
VMEM_LIMIT = 56 * 1024 * 1024

ADAM_LR, ADAM_B1, ADAM_B2, ADAM_EPS, ADAM_WD, ADAM_STEP = 0.001, 0.9, 0.999, 1e-08, 0.01, 10

C_CQ, C_CKV, C_KR, C_DQ, C_DK, C_DV, C_GQ, C_GK, C_GV, IN_P = 0, 256, 384, 512, 896, 1280, 1664, 1920, 2048, 2176
IN_W = 2080
MESH_ID = pl.DeviceIdType.MESH


def _cparams(n_axes, vmem=VMEM_LIMIT):
    return pltpu.CompilerParams(dimension_semantics=("arbitrary",) * n_axes, vmem_limit_bytes=vmem)


MAX_WHOLE_DIM = 2304


def _pick(n, target):
    best = None
    for t in range(LANES, min(n, target) + 1, LANES):
        if n % t == 0:
            best = t
    if best is not None and (2 * best >= target or n > MAX_WHOLE_DIM):
        return best
    return n


def _sds(shape, dtype):
    return jax.ShapeDtypeStruct(tuple(shape), dtype)


def _in_cols():
    idx = -np.ones((IN_P,), np.int64)
    idx[C_CQ:C_CQ + 256] = np.arange(0, 256)
    idx[C_CKV:C_CKV + 128] = np.arange(256, 384)
    idx[C_KR + 64:C_KR + 96] = np.arange(384, 416)
    idx[C_DQ:C_DQ + 1152] = np.arange(416, 1568)
    gq = 1568 + (np.array([0, 2, 1, 3])[:, None] * 64 + np.arange(64)[None, :]).reshape(-1)
    idx[C_GQ:C_GQ + 256] = gq
    idx[C_GK:C_GK + 256] = np.arange(1824, 2080)
    return idx


def _uq_cols():
    idx = -np.ones((MLA_HEADS * 128,), np.int64)
    for h in range(MLA_HEADS):
        idx[h * 128:h * 128 + 96] = np.arange(96 * h, 96 * h + 96)
    return idx


def _ukv_cols():
    idx = -np.ones((MLA_HEADS * 128 + MLA_HEADS * 64,), np.int64)
    for h in range(MLA_HEADS):
        idx[h * 128:h * 128 + 64] = np.arange(128 * h, 128 * h + 64)
        idx[768 + h * 64:768 + h * 64 + 64] = np.arange(128 * h + 64, 128 * h + 128)
    return idx


def _runs(idx):
    out, i = [], 0
    while i < len(idx):
        j = i + 1
        while j < len(idx) and ((idx[i] < 0 and idx[j] < 0) or (idx[i] >= 0 and idx[j] == idx[j - 1] + 1)):
            j += 1
        out.append((int(idx[i]), j - i))
        i = j
    return out


def _rows_from_shards(sh, idx):
    _, cs, r = sh.shape
    pieces = []
    for first, ln in _runs(idx):
        if first < 0:
            pieces.append(jnp.zeros((ln, r), sh.dtype))
            continue
        while ln > 0:
            k, off = divmod(first, cs)
            take = min(ln, cs - off)
            pieces.append(sh[k, off:off + take, :])
            first, ln = first + take, ln - take
    return jnp.concatenate(pieces, axis=0)


def _rows_to_shards(wp, idx, n):
    inv = np.zeros((n,), np.int64)
    pos = np.nonzero(idx >= 0)[0]
    inv[idx[pos]] = pos
    cs = n // 4
    shards = []
    for k in range(4):
        pieces = [wp[first:first + ln, :] for first, ln in _runs(inv[k * cs:(k + 1) * cs])]
        shards.append(jnp.concatenate(pieces, axis=0))
    return jnp.stack(shards)


def _t5_bucket_np(rel):
    nb = REL_BUCKETS // 2
    exact = nb // 2
    ret = np.where(rel > 0, nb, 0)
    n = np.abs(rel)
    nf = np.maximum(n, 1).astype(np.float32)
    large = exact + (np.log(nf / np.float32(exact)) / np.float32(math.log(REL_MAX_DIST / exact))
                     * np.float32(nb - exact)).astype(np.int32)
    large = np.minimum(large, nb - 1)
    return ret + np.where(n < exact, n, large)


def _branch_bucket_idx(tq, dil):
    kw = tq + 2 * DIL_HALF
    rel = np.arange(kw)[None, :] - DIL_HALF - np.arange(tq)[:, None]
    idx = _t5_bucket_np(rel * dil)
    return np.where(np.abs(rel) <= DIL_HALF, idx, -1).astype(np.int32)


def _rope_tables(S):
    inv = ROPE_THETA ** (-jnp.arange(0, 32, 2, dtype=F32) / 32)
    t = jnp.arange(S)
    pos = t.astype(F32)
    row = (t // GRID_W).astype(F32)
    col = (t % GRID_W).astype(F32)
    lane = np.arange(LANES)
    wm = lane - 64
    is_rope = (lane >= 64) & (lane < 96)
    ang = pos[:, None] * inv[np.where(is_rope, wm % 16, 0)][None, :]
    cm = jnp.where(is_rope[None], jnp.cos(ang), 1.0)
    smm = jnp.where((is_rope & (wm < 16))[None], -jnp.sin(ang), 0.0)
    spm = jnp.where((is_rope & (wm >= 16))[None], jnp.sin(ang), 0.0)
    g = lane % 64
    w = g % 32
    angg = jnp.where((g < 32)[None], row[:, None], col[:, None]) * inv[w % 16][None, :]
    cg = jnp.cos(angg)
    smg = jnp.where((w < 16)[None], -jnp.sin(angg), 0.0)
    spg = jnp.where((w >= 16)[None], jnp.sin(angg), 0.0)
    return (cm, smm, spm), (cg, smg, spg)


def _lanes(t, width):
    return t if width == LANES else jnp.concatenate([t] * (width // LANES), axis=1)


def _rope(x, tabs):
    c, sm, sp = (_lanes(t, x.shape[1]) for t in tabs)
    w = x.shape[1]
    return x * c + pltpu.roll(x, w - 16, 1) * sm + pltpu.roll(x, 16, 1) * sp


def _rope_t(dy, tabs):
    c, sm, sp = (_lanes(t, dy.shape[1]) for t in tabs)
    w = dy.shape[1]
    return dy * c + pltpu.roll(dy * sm, 16, 1) + pltpu.roll(dy * sp, w - 16, 1)


def _head_ones(width):
    i = np.arange(width)
    return jnp.asarray((i[:, None] // HEAD_DIM == i[None, :] // HEAD_DIM).astype(np.float32))


def _headsum(x, j):
    return jnp.dot(x, j, preferred_element_type=F32, precision=lax.Precision.HIGHEST)


def _mm(a, b, *, ta=False, tb=False, ga=False, gb=False, go=False, out_dtype=F32, name):
    G = a.shape[0] if ga else (b.shape[0] if gb else 1)
    a2 = a.shape[1:] if ga else a.shape
    b2 = b.shape[1:] if gb else b.shape
    K, M = a2 if ta else a2[::-1]
    N = b2[0] if tb else b2[1]
    assert (b2[1] if tb else b2[0]) == K
    tm, tn, tk = _pick(M, 1024), _pick(N, 1024), _pick(K, 1024)
    if tm * tn > 1024 * 1152:
        tm = _pick(M, 512)
    nk = K // tk
    steps = nk if (go or G == 1) else G * nk
    dn = (((0 if ta else 1,), (1 if tb else 0,)), ((), ()))

    def body(a_ref, b_ref, o_ref, *acc):
        part = lax.dot_general(a_ref[...], b_ref[...], dn, preferred_element_type=F32)
        if steps == 1:
            o_ref[...] = part.astype(o_ref.dtype)
            return
        acc_ref, = acc
        s = pl.program_id(3)

        @pl.when(s == 0)
        def _():
            acc_ref[...] = part

        @pl.when(s > 0)
        def _():
            acc_ref[...] += part

        @pl.when(s == steps - 1)
        def _():
            o_ref[...] = acc_ref[...].astype(o_ref.dtype)

    def grp(g, s):
        return g if go else s // nk

    def kk(s):
        return s if steps == nk else s % nk

    def spec(grouped, block, index):
        if grouped:
            return pl.BlockSpec((None,) + block, lambda g, i, j, s: (grp(g, s),) + index(i, j, s))
        return pl.BlockSpec(block, lambda g, i, j, s: index(i, j, s))

    a_spec = (spec(ga, (tk, tm), lambda i, j, s: (kk(s), i)) if ta else spec(ga, (tm, tk), lambda i, j, s: (i, kk(s))))
    b_spec = (spec(gb, (tn, tk), lambda i, j, s: (j, kk(s))) if tb else spec(gb, (tk, tn), lambda i, j, s: (kk(s), j)))
    o_spec = spec(go, (tm, tn), lambda i, j, s: (i, j))
    return pl.pallas_call(
        body, name=name, grid=(G if go else 1, M // tm, N // tn, steps),
        in_specs=[a_spec, b_spec], out_specs=o_spec,
        out_shape=_sds(((G,) if go else ()) + (M, N), out_dtype),
        scratch_shapes=[pltpu.VMEM((tm, tn), F32)] if steps > 1 else [],
        compiler_params=_cparams(4),
    )(a, b)


def _row(ts, w, cb=0):
    return pl.BlockSpec((ts, w), lambda i: (i, cb))


def _full(shape):
    nd = len(shape)
    return pl.BlockSpec(tuple(shape), lambda i: (0,) * nd)


def _rms_fwd(x, g, eps=1e-6):
    r = lax.rsqrt(jnp.mean(x * x, axis=-1, keepdims=True) + eps)
    return x * r * g


def _rms_bwd(x, g, dy, eps=1e-6):
    r = lax.rsqrt(jnp.mean(x * x, axis=-1, keepdims=True) + eps)
    gdy = g * dy
    dx = r * gdy - x * (r * r * r) * jnp.mean(x * gdy, axis=-1, keepdims=True)
    return dx, x * r * dy


def _rms_head_fwd(x, g, j, eps=1e-6):
    r = lax.rsqrt(_headsum(x * x, j) * (1.0 / HEAD_DIM) + eps)
    return x * r * g


def _rms_head_bwd(x, g, dy, j, eps=1e-6):
    r = lax.rsqrt(_headsum(x * x, j) * (1.0 / HEAD_DIM) + eps)
    gdy = g * dy
    dx = r * gdy - x * (r * r * r) * (_headsum(x * gdy, j) * (1.0 / HEAD_DIM))
    return dx, x * r * dy


DIL_STRIDES = tuple(d for _, d in DIL_BRANCHES if d > 1)
DIL_W = DIL_HEADS * HEAD_DIM


def _res_spec(d, n, pad_blocks=0):
    return pl.BlockSpec((d, n, DIL_W), lambda i: (0, i + pad_blocks, 0))


def _prep_fwd(h, gq, gkv, ggq, ggk, tm, tg, j256):
    S = h.shape[0]
    ts = min(256, S)
    scale = HEAD_DIM ** -0.5
    nres = len(DIL_STRIDES)

    def body(*refs):
        (h_ref, gq_ref, gkv_ref, ggq_ref, ggk_ref, cm, smm, spm, cg, smg, spg, j_ref), refs = refs[:12], refs[12:]
        refs = refs[2 * nres:]
        (cq_o, ckv_o, kr_o, dq_o, dk_o, dv_o, gq_o, gk_o, gv_o), res_o = refs[:9], refs[9:-1]
        st = refs[-1]
        tabm = (cm[...], smm[...], spm[...])
        tabg = (cg[...], smg[...], spg[...])
        cq_o[...] = _rms_fwd(h_ref[:, C_CQ:C_CQ + 256], gq_ref[...]).astype(CDT)
        ckv_o[...] = _rms_fwd(h_ref[:, C_CKV:C_CKV + 128], gkv_ref[...]).astype(CDT)
        kr_o[...] = _rope(h_ref[:, C_KR:C_KR + 128], tabm).astype(CDT)
        dq_o[...] = (h_ref[:, C_DQ:C_DQ + 384] * scale).astype(CDT)
        dk_o[...] = h_ref[:, C_DK:C_DK + 384].astype(CDT)
        dv_o[...] = h_ref[:, C_DV:C_DV + 384].astype(CDT)
        for j, lanes in _lane_blocks(3 * DIL_W):
            st[j] = h_ref[:, C_DQ + lanes.start:C_DQ + lanes.stop] * (scale if j < 3 else 1.0)
        for bi, d in enumerate(DIL_STRIDES):
            for c in range(d):
                rows = pl.ds(c, ts // d, stride=d)
                for j, lanes in _lane_blocks(3 * DIL_W):
                    res_o[3 * bi + j // 3][c, :, (j % 3) * LANES:(j % 3 + 1) * LANES] = st.at[j][rows, :].astype(CDT)
        qn = _rms_head_fwd(h_ref[:, C_GQ:C_GQ + 256], ggq_ref[...], j_ref[...])
        gq_o[...] = (_rope(qn, tabg) * scale).astype(CDT)
        kn = _rms_head_fwd(h_ref[:, C_GK:C_GK + 128], ggk_ref[...], j_ref[0:128, 0:128])
        gk_o[...] = _rope(kn, tabg).astype(CDT)
        gv_o[...] = h_ref[:, C_GV:C_GV + 128].astype(CDT)

    widths = (256, 128, 128, 384, 384, 384, 256, 128, 128)
    out_specs = [_row(ts, w) for w in widths]
    out_shape = [_sds((S, w), CDT) for w in widths]
    zeros, aliases = [], {}
    for d in DIL_STRIDES:
        n, L = ts // d, S // d
        out_specs += [_res_spec(d, n), _res_spec(d, n, DIL_HALF // n), _res_spec(d, n, DIL_HALF // n)]
        out_shape += [_sds((d, L, DIL_W), CDT)] + [_sds((d, L + 2 * DIL_HALF, DIL_W), CDT)] * 2
        for t in range(2):
            aliases[12 + len(zeros)] = len(out_shape) - 2 + t
            zeros.append(jnp.zeros((d, L + 2 * DIL_HALF, DIL_W), CDT))
    return pl.pallas_call(
        body, name="prep_fwd", grid=(S // ts,),
        in_specs=[_row(ts, IN_P), _full(gq.shape), _full(gkv.shape), _full(ggq.shape), _full(ggk.shape)]
        + [_row(ts, LANES)] * 6 + [_full(j256.shape)] + [pl.BlockSpec(memory_space=pl.ANY)] * len(zeros),
        out_specs=out_specs, out_shape=out_shape, input_output_aliases=aliases,
        scratch_shapes=[pltpu.VMEM((3 * DIL_W // LANES, ts, LANES), F32)],
        compiler_params=_cparams(1),
    )(h, gq, gkv, ggq, ggk, *tm, *tg, j256, *zeros)


def _prep_bwd(h, dcq, dckv, dkr, ddq, ddk, ddv, dgq, dgk, dgv, gq, gkv, ggq, ggk, tg, j256):
    S = h.shape[0]
    ts = min(256, S)
    scale = HEAD_DIM ** -0.5

    def body(h_ref, dcq_r, dckv_r, dkr_r, q1, q2, q3, k1, k2, k3, v1, v2, v3, dgq_r, dgk_r, dgv_r,
             gq_ref, gkv_ref, ggq_ref, ggk_ref, cg, smg, spg, j_ref,
             dh_o, ngq_o, ngkv_o, nggq_o, nggk_o, *scr):
        tabg = (cg[...], smg[...], spg[...])
        first = pl.program_id(0) == 0
        scr, = scr
        d2, d3 = DIL_STRIDES
        dq = q1[...] + _by_token(q2, scr, d2) + _by_token(q3, scr, d3)
        dk = k1[...] + _by_token(k2, scr, d2) + _by_token(k3, scr, d3)
        dv = v1[...] + _by_token(v2, scr, d2) + _by_token(v3, scr, d3)

        def acc(o_ref, val):
            s = jnp.sum(val, axis=0, keepdims=True)

            @pl.when(first)
            def _():
                o_ref[...] = s

            @pl.when(jnp.logical_not(first))
            def _():
                o_ref[...] += s

        dx, dg = _rms_bwd(h_ref[:, C_CQ:C_CQ + 256], gq_ref[...], dcq_r[...])
        dh_o[:, C_CQ:C_CQ + 256] = dx.astype(CDT)
        acc(ngq_o, dg)
        dx, dg = _rms_bwd(h_ref[:, C_CKV:C_CKV + 128], gkv_ref[...], dckv_r[...])
        dh_o[:, C_CKV:C_CKV + 128] = dx.astype(CDT)
        acc(ngkv_o, dg)
        dh_o[:, C_KR:C_KR + 128] = dkr_r[...].astype(CDT)
        dh_o[:, C_DQ:C_DQ + 384] = (dq * scale).astype(CDT)
        dh_o[:, C_DK:C_DK + 384] = dk.astype(CDT)
        dh_o[:, C_DV:C_DV + 384] = dv.astype(CDT)
        dqn = _rope_t(dgq_r[...] * scale, tabg)
        dx, dg = _rms_head_bwd(h_ref[:, C_GQ:C_GQ + 256], ggq_ref[...], dqn, j_ref[...])
        dh_o[:, C_GQ:C_GQ + 256] = dx.astype(CDT)
        acc(nggq_o, dg)
        dkn = _rope_t(dgk_r[...], tabg)
        dx, dg = _rms_head_bwd(h_ref[:, C_GK:C_GK + 128], ggk_ref[...], dkn, j_ref[0:128, 0:128])
        dh_o[:, C_GK:C_GK + 128] = dx.astype(CDT)
        acc(nggk_o, dg)
        dh_o[:, C_GV:C_GV + 128] = dgv_r[...].astype(CDT)

    d2, d3 = DIL_STRIDES
    n2, n3 = ts // d2, ts // d3
    tok = _row(ts, DIL_W)
    return pl.pallas_call(
        body, name="prep_bwd", grid=(S // ts,),
        in_specs=[_row(ts, IN_P), _row(ts, 256), _row(ts, 128), _row(ts, 128)]
        + [tok, _res_spec(d2, n2), _res_spec(d3, n3)]
        + [tok, _res_spec(d2, n2, DIL_HALF // n2), _res_spec(d3, n3, DIL_HALF // n3)] * 2
        + [_row(ts, 256), _row(ts, 128), _row(ts, 128)]
        + [_full(gq.shape), _full(gkv.shape), _full(ggq.shape), _full(ggk.shape)] + [_row(ts, LANES)] * 3
        + [_full(j256.shape)],
        out_specs=[_row(ts, IN_P), _full((1, 256)), _full((1, 128)), _full((1, 256)), _full((1, 128))],
        out_shape=[_sds((S, IN_P), CDT), _sds((1, 256), F32), _sds((1, 128), F32), _sds((1, 256), F32),
                   _sds((1, 128), F32)],
        scratch_shapes=[_TOKEN_SCRATCH(ts)],
        compiler_params=_cparams(1),
    )(h, dcq, dckv, dkr, *ddq, *ddk, *ddv, dgq, dgk, dgv, gq, gkv, ggq, ggk, *tg, j256)


def _lane_blocks(width):
    return [(j, slice(j * LANES, (j + 1) * LANES)) for j in range(width // LANES)]


_TOKEN_SCRATCH = lambda ts: pltpu.VMEM((DIL_W // LANES, ts, LANES), F32)


def _by_token(res_ref, scr_ref, d):
    n = res_ref.shape[1]
    if d == 1:
        return res_ref[0].astype(F32)
    for c in range(d):
        for j, lanes in _lane_blocks(res_ref.shape[2]):
            scr_ref.at[j][pl.ds(c, n, stride=d), :] = res_ref[c, :, lanes].astype(F32)
    return jnp.concatenate([scr_ref[j] for j, _ in _lane_blocks(res_ref.shape[2])], axis=1)


def _by_residue(val, scr_ref, out_ref, d):
    n = out_ref.shape[1]
    if d == 1:
        out_ref[0] = val.astype(out_ref.dtype)
        return
    for j, lanes in _lane_blocks(out_ref.shape[2]):
        scr_ref[j] = val[:, lanes]
    for c in range(d):
        for j, lanes in _lane_blocks(out_ref.shape[2]):
            out_ref[c, :, lanes] = scr_ref.at[j][pl.ds(c, n, stride=d), :].astype(out_ref.dtype)


def _mla_prep_fwd(qa, kvp, kr, tm, scale):
    S = qa.shape[0]
    ts = min(256, S)

    def body(qa_ref, kv_ref, kr_ref, cm, smm, spm, q_o, k_o):
        tabm = (cm[...], smm[...], spm[...])
        q_o[...] = (_rope(qa_ref[...], tabm) * scale).astype(CDT)
        k_o[...] = kv_ref[:, 0:768] + _lanes(kr_ref[...], 768)

    return pl.pallas_call(
        body, name="mla_prep_fwd", grid=(S // ts,),
        in_specs=[_row(ts, 768), _row(ts, 1152), _row(ts, 128)] + [_row(ts, LANES)] * 3,
        out_specs=[_row(ts, 768)] * 2, out_shape=[_sds((S, 768), CDT)] * 2,
        compiler_params=_cparams(1),
    )(qa, kvp, kr, *tm)


def _mla_prep_bwd(dq, dk, dv, tm, scale):
    S = dq.shape[0]
    ts = min(256, S)

    def body(dq_ref, dk_ref, dv_ref, cm, smm, spm, dqa_o, dkv_o, dkr_o):
        tabm = (cm[...], smm[...], spm[...])
        lane = lax.broadcasted_iota(jnp.int32, (1, LANES), 1)
        dqa_o[...] = _rope_t(dq_ref[...] * scale, tabm).astype(CDT)
        dkr = jnp.zeros((ts, LANES), F32)
        for hd in range(MLA_HEADS):
            blk = dk_ref[:, hd * 128:(hd + 1) * 128]
            dkv_o[:, hd * 128:(hd + 1) * 128] = jnp.where(lane < 64, blk, 0.0).astype(CDT)
            dkr = dkr + jnp.where((lane >= 64) & (lane < 96), blk, 0.0)
        dkv_o[:, 768:1152] = dv_ref[...].astype(CDT)
        dkr_o[...] = jnp.where((lane >= 64) & (lane < 96), _rope_t(dkr, tabm), 0.0)

    return pl.pallas_call(
        body, name="mla_prep_bwd", grid=(S // ts,),
        in_specs=[_row(ts, 768), _row(ts, 768), _row(ts, 384)] + [_row(ts, LANES)] * 3,
        out_specs=[_row(ts, 768), _row(ts, 1152), _row(ts, 128)],
        out_shape=[_sds((S, 768), CDT), _sds((S, 1152), CDT), _sds((S, 128), F32)],
        compiler_params=_cparams(1),
    )(dq, dk, dv, *tm)


def _ln_fwd(xa, xb, g, b, alpha, name):
    S, D = xa.shape
    ts = min(256, S)

    def body(xa_ref, xb_ref, g_ref, b_ref, y_o, yb_o, z_o):
        z = alpha * xa_ref[...] + xb_ref[...]
        mu = jnp.mean(z, axis=-1, keepdims=True)
        zc = z - mu
        var = jnp.mean(zc * zc, axis=-1, keepdims=True)
        y = zc * lax.rsqrt(var + 1e-5) * g_ref[...] + b_ref[...]
        y_o[...] = y
        yb_o[...] = y.astype(CDT)
        z_o[...] = z

    return pl.pallas_call(
        body, name=name, grid=(S // ts,),
        in_specs=[_row(ts, D), _row(ts, D), _full(g.shape), _full(b.shape)],
        out_specs=[_row(ts, D)] * 3, out_shape=[_sds((S, D), F32), _sds((S, D), CDT), _sds((S, D), F32)],
        compiler_params=_cparams(1),
    )(xa, xb, g, b)


def _ln_bwd(dya, dyb, z, g, alpha, name, loss_head=False):
    S, D = z.shape
    ts = min(256, S)

    def body(dya_ref, dyb_ref, z_ref, g_ref, dz_o, dzb_o, dg_o, db_o, *loss_o):
        first = pl.program_id(0) == 0
        if loss_head:
            err = dya_ref[...] - dyb_ref[...]
            dy = err * (1.0 / D)
            part = jnp.sum(jnp.sum(err * err, axis=1, keepdims=True), axis=0, keepdims=True) * (0.5 / D)

            @pl.when(first)
            def _():
                loss_o[0][...] = part

            @pl.when(jnp.logical_not(first))
            def _():
                loss_o[0][...] += part
        else:
            dy = dya_ref[...] + alpha * dyb_ref[...]
        z = z_ref[...]
        mu = jnp.mean(z, axis=-1, keepdims=True)
        zc = z - mu
        r = lax.rsqrt(jnp.mean(zc * zc, axis=-1, keepdims=True) + 1e-5)
        xh = zc * r
        dxh = dy * g_ref[...]
        dz = r * (dxh - jnp.mean(dxh, axis=-1, keepdims=True) - xh * jnp.mean(dxh * xh, axis=-1, keepdims=True))
        dz_o[...] = dz
        dzb_o[...] = dz.astype(CDT)
        sg = jnp.sum(dy * xh, axis=0, keepdims=True)
        sb = jnp.sum(dy, axis=0, keepdims=True)

        @pl.when(first)
        def _():
            dg_o[...] = sg
            db_o[...] = sb

        @pl.when(jnp.logical_not(first))
        def _():
            dg_o[...] += sg
            db_o[...] += sb

    extra = ([_full((1, 1))], [_sds((1, 1), F32)]) if loss_head else ([], [])
    return pl.pallas_call(
        body, name=name, grid=(S // ts,),
        in_specs=[_row(ts, D)] * 3 + [_full(g.shape)],
        out_specs=[_row(ts, D), _row(ts, D), _full((1, D)), _full((1, D))] + extra[0],
        out_shape=[_sds((S, D), F32), _sds((S, D), CDT), _sds((1, D), F32), _sds((1, D), F32)] + extra[1],
        compiler_params=_cparams(1),
    )(dya, dyb, z, g)


def _grp_spec(ts, w):
    return pl.BlockSpec((None, ts, w), lambda k, i: (k, i, 0))


def _ffn_up(xb, wg3, wu3):
    S, D = xb.shape
    G, Fc, _ = wg3.shape
    tm = _pick(S, 1024)
    wspec = pl.BlockSpec((None, Fc, D), lambda k, i: (k, 0, 0))

    def body(x_ref, wg_ref, wu_ref, g_o, u_o, a_o):
        x = x_ref[...]
        g = lax.dot_general(x, wg_ref[...], _NT, preferred_element_type=F32)
        u = lax.dot_general(x, wu_ref[...], _NT, preferred_element_type=F32)
        g_o[...] = g.astype(CDT)
        u_o[...] = u.astype(CDT)
        a_o[...] = (g / (1.0 + jnp.exp(-g)) * u).astype(CDT)

    return pl.pallas_call(
        body, name="ffn_up", grid=(G, S // tm),
        in_specs=[pl.BlockSpec((tm, D), lambda k, i: (i, 0)), wspec, wspec], out_specs=[_grp_spec(tm, Fc)] * 3,
        out_shape=[_sds((G, S, Fc), CDT)] * 3, compiler_params=_cparams(2),
    )(xb, wg3, wu3)


def _ffn_up_dx(dg3, du3, wg3, wu3):
    G, S, Fc = dg3.shape
    D = wg3.shape[2]
    tm = _pick(S, 1024)
    wspec = pl.BlockSpec((None, Fc, D), lambda i, k: (k, 0, 0))
    aspec = pl.BlockSpec((None, tm, Fc), lambda i, k: (k, i, 0))

    def body(dg_ref, du_ref, wg_ref, wu_ref, o_ref):
        part = (jnp.dot(dg_ref[...], wg_ref[...], preferred_element_type=F32)
                + jnp.dot(du_ref[...], wu_ref[...], preferred_element_type=F32))
        k = pl.program_id(1)

        @pl.when(k == 0)
        def _():
            o_ref[...] = part

        @pl.when(k > 0)
        def _():
            o_ref[...] += part

    return pl.pallas_call(
        body, name="ffn_up_dx", grid=(S // tm, G), in_specs=[aspec, aspec, wspec, wspec],
        out_specs=pl.BlockSpec((tm, D), lambda i, k: (i, 0)), out_shape=_sds((S, D), F32), compiler_params=_cparams(2),
    )(dg3, du3, wg3, wu3)


def _ffn_down_dx(dzb, wd3, g3, u3):
    S, D = dzb.shape
    G, Fc, _ = wd3.shape
    tm = _pick(S, 1024)

    def body(dz_ref, wd_ref, g_ref, u_ref, dg_o, du_o):
        da = lax.dot_general(dz_ref[...], wd_ref[...], _NT, preferred_element_type=F32)
        g = g_ref[...].astype(F32)
        sg = 1.0 / (1.0 + jnp.exp(-g))
        dg_o[...] = (da * u_ref[...].astype(F32) * (sg * (1.0 + g * (1.0 - sg)))).astype(CDT)
        du_o[...] = (da * (g * sg)).astype(CDT)

    return pl.pallas_call(
        body, name="ffn_down_dx", grid=(G, S // tm),
        in_specs=[pl.BlockSpec((tm, D), lambda k, i: (i, 0)), pl.BlockSpec((None, Fc, D), lambda k, i: (k, 0, 0)),
                  _grp_spec(tm, Fc), _grp_spec(tm, Fc)],
        out_specs=[_grp_spec(tm, Fc)] * 2, out_shape=[_sds((G, S, Fc), CDT)] * 2, compiler_params=_cparams(2),
    )(dzb, wd3, g3, u3)


def _axpy(a, b, alpha, name):
    S, D = a.shape
    ts = min(256, S)

    def body(a_ref, b_ref, o_ref):
        o_ref[...] = a_ref[...] + alpha * b_ref[...]

    return pl.pallas_call(
        body, name=name, grid=(S // ts,), in_specs=[_row(ts, D)] * 2, out_specs=_row(ts, D),
        out_shape=_sds((S, D), F32), compiler_params=_cparams(1),
    )(a, b)


def _pair_masks():
    lane = lax.broadcasted_iota(jnp.int32, (1, LANES), 1)
    first = lane < HEAD_DIM
    return first, jnp.logical_not(first)


def _head_scalar(x, m):
    return jnp.max(jnp.where(m, x, -jnp.inf), axis=-1, keepdims=True)


_NT = (((1,), (1,)), ((), ()))
_TN = (((0,), (0,)), ((), ()))


def _attn_fwd(q, k, v, *, split, npairs, kblk, vblk, name):
    S = q.shape[0]
    qw = 256 if split else LANES
    tq = min(256, S)

    def body(q_ref, k_ref, v_ref, o_ref, lse_ref):
        masks = _pair_masks()
        outs, lses = [], []
        for hd in range(2):
            if split:
                qh = q_ref[:, hd * LANES:(hd + 1) * LANES]
                kh = k_ref[:, hd * LANES:(hd + 1) * LANES]
            else:
                qh = jnp.where(masks[hd], q_ref[...], jnp.zeros_like(q_ref[...]))
                kh = k_ref[...]
            s = lax.dot_general(qh, kh, _NT, preferred_element_type=F32)
            mx = jnp.max(s, axis=-1, keepdims=True)
            p = jnp.exp(s - mx)
            l = jnp.sum(p, axis=-1, keepdims=True)
            o = jnp.dot(p.astype(CDT), v_ref[...], preferred_element_type=F32)
            outs.append(o / l)
            lses.append(jnp.broadcast_to(mx + jnp.log(l), (tq, LANES)))
        o_ref[...] = jnp.where(masks[0], outs[0], outs[1]).astype(o_ref.dtype)
        lse_ref[...] = jnp.where(masks[0], lses[0], lses[1])

    return pl.pallas_call(
        body, name=name, grid=(npairs, S // tq),
        in_specs=[pl.BlockSpec((tq, qw), lambda p, i: (i, p)),
                  pl.BlockSpec((S, qw), lambda p, i: (0, kblk(p))),
                  pl.BlockSpec((S, LANES), lambda p, i: (0, vblk(p)))],
        out_specs=[pl.BlockSpec((tq, LANES), lambda p, i: (i, p))] * 2,
        out_shape=[_sds((S, LANES * npairs), CDT), _sds((S, LANES * npairs), F32)],
        compiler_params=_cparams(2),
    )(q, k, v)


def _attn_bwd(q, k, v, do, o, lse, *, split, npairs, kblk, vblk, doblk, shared_kv, name):
    S = q.shape[0]
    qw = 256 if split else LANES
    tq = min(256, S)
    nkv = 1 if shared_kv else npairs

    def body(q_ref, k_ref, v_ref, do_ref, o_ref, lse_ref, dq_ref, dk_ref, dv_ref):
        masks = _pair_masks()
        p_id, i_id = pl.program_id(0), pl.program_id(1)
        first = (i_id == 0) & ((p_id == 0) if shared_kv else True)
        do = do_ref[...]
        o = o_ref[...].astype(F32)
        lse = lse_ref[...]
        v = v_ref[...]
        dqs, dks, dvs = [], [], []
        for hd in range(2):
            m = masks[hd]
            if split:
                qh = q_ref[:, hd * LANES:(hd + 1) * LANES]
                kh = k_ref[:, hd * LANES:(hd + 1) * LANES]
            else:
                qh = jnp.where(m, q_ref[...], jnp.zeros_like(q_ref[...]))
                kh = k_ref[...]
            doh = jnp.where(m, do, 0.0)
            s = lax.dot_general(qh, kh, _NT, preferred_element_type=F32)
            p = jnp.exp(s - _head_scalar(lse, m))
            delta = jnp.sum(doh * o, axis=-1, keepdims=True)
            dohb = doh.astype(CDT)
            dp = lax.dot_general(dohb, v, _NT, preferred_element_type=F32)
            ds = (p * (dp - delta)).astype(CDT)
            dq = jnp.dot(ds, kh, preferred_element_type=F32)
            dqs.append(dq if split else jnp.where(m, dq, 0.0))
            dks.append(lax.dot_general(ds, qh, _TN, preferred_element_type=F32))
            dvs.append(lax.dot_general(p.astype(CDT), dohb, _TN, preferred_element_type=F32))
        if split:
            dq_ref[:, 0:LANES] = dqs[0]
            dq_ref[:, LANES:2 * LANES] = dqs[1]
        else:
            dq_ref[...] = dqs[0] + dqs[1]
        dv = dvs[0] + dvs[1]

        @pl.when(first)
        def _():
            if split:
                dk_ref[:, 0:LANES] = dks[0]
                dk_ref[:, LANES:2 * LANES] = dks[1]
            else:
                dk_ref[...] = dks[0] + dks[1]
            dv_ref[...] = dv

        @pl.when(jnp.logical_not(first))
        def _():
            if split:
                dk_ref[:, 0:LANES] += dks[0]
                dk_ref[:, LANES:2 * LANES] += dks[1]
            else:
                dk_ref[...] += dks[0] + dks[1]
            dv_ref[...] += dv

    kvo = (lambda p, i: (0, 0)) if shared_kv else (lambda p, i: (0, p))
    return pl.pallas_call(
        body, name=name, grid=(npairs, S // tq),
        in_specs=[pl.BlockSpec((tq, qw), lambda p, i: (i, p)),
                  pl.BlockSpec((S, qw), lambda p, i: (0, kblk(p))),
                  pl.BlockSpec((S, LANES), lambda p, i: (0, vblk(p))),
                  pl.BlockSpec((tq, LANES), lambda p, i: (i, doblk(p))),
                  pl.BlockSpec((tq, LANES), lambda p, i: (i, p)),
                  pl.BlockSpec((tq, LANES), lambda p, i: (i, p))],
        out_specs=[pl.BlockSpec((tq, qw), lambda p, i: (i, p)),
                   pl.BlockSpec((S, qw), kvo), pl.BlockSpec((S, LANES), kvo)],
        out_shape=[_sds((S, qw * npairs), F32), _sds((S, qw * nkv), F32), _sds((S, LANES * nkv), F32)],
        compiler_params=_cparams(2),
    )(q, k, v, do, o, lse)


def _bias_expand(idx, rel_bias, name):
    tq, kw = idx.shape

    def body(idx_ref, rb_ref, o_ref):
        idx = idx_ref[...]
        for hd in range(DIL_HEADS):
            acc = jnp.full((tq, kw), NEG_INF, F32)
            for u in range(REL_BUCKETS):
                acc = jnp.where(idx == u, rb_ref[u, hd], acc)
            o_ref[hd] = acc

    return pl.pallas_call(
        body, name=name,
        in_specs=[pl.BlockSpec(memory_space=pltpu.VMEM), pl.BlockSpec(memory_space=pltpu.SMEM)],
        out_specs=pl.BlockSpec(memory_space=pltpu.VMEM),
        out_shape=_sds((DIL_HEADS, tq, kw), F32),
    )(idx, rel_bias)


def _bias_reduce(idx, dtab, name):
    tq, kw = idx.shape

    def body(idx_ref, d_ref, o_ref):
        idx = idx_ref[...]
        rowid = lax.broadcasted_iota(jnp.int32, (REL_BUCKETS, kw), 0)
        for hd in range(DIL_HEADS):
            d = d_ref[hd]
            acc = jnp.zeros((REL_BUCKETS, kw), F32)
            for u in range(REL_BUCKETS):
                r = jnp.sum(jnp.where(idx == u, d, 0.0), axis=0, keepdims=True)
                acc = jnp.where(rowid == u, r, acc)
            o_ref[hd] = jnp.sum(acc, axis=1, keepdims=True)

    return pl.pallas_call(
        body, name=name,
        in_specs=[pl.BlockSpec(memory_space=pltpu.VMEM)] * 2, out_specs=pl.BlockSpec(memory_space=pltpu.VMEM),
        out_shape=_sds((DIL_HEADS, REL_BUCKETS, 1), F32),
    )(idx, dtab)


def _dil_window(i, tq, kw, L):
    start = pl.multiple_of(i * tq, DIL_HALF)
    key = start + lax.broadcasted_iota(jnp.int32, (1, kw), 1) - DIL_HALF
    return start, (key >= 0) & (key < L)


def _dil_fwd(qv, kv, vv, tab, *, dil, L, tq, name):
    kw = tq + 2 * DIL_HALF
    npair = DIL_HEADS // 2

    def body(q_ref, k_ref, v_ref, t_ref, o_ref, lse_ref):
        masks = _pair_masks()
        start, valid = _dil_window(pl.program_id(2), tq, kw, L)
        kwin = k_ref[pl.ds(start, kw), :]
        vwin = v_ref[pl.ds(start, kw), :]
        outs, lses = [], []
        for hd in range(2):
            qh = jnp.where(masks[hd], q_ref[...], jnp.zeros_like(q_ref[...]))
            s = lax.dot_general(qh, kwin, _NT, preferred_element_type=F32) + t_ref[hd]
            s = jnp.where(valid, s, NEG_INF)
            mx = jnp.max(s, axis=-1, keepdims=True)
            p = jnp.exp(s - mx)
            l = jnp.sum(p, axis=-1, keepdims=True)
            outs.append(jnp.dot(p.astype(CDT), vwin, preferred_element_type=F32) / l)
            lses.append(jnp.broadcast_to(mx + jnp.log(l), (tq, LANES)))
        o_ref[...] = jnp.where(masks[0], outs[0], outs[1])
        lse_ref[...] = jnp.where(masks[0], lses[0], lses[1])

    blk = pl.BlockSpec((None, tq, LANES), lambda p, c, i: (c, i, p))
    res = pl.BlockSpec((None, L + 2 * DIL_HALF, LANES), lambda p, c, i: (c, 0, p))
    return pl.pallas_call(
        body, name=name, grid=(npair, dil, L // tq),
        in_specs=[blk, res, res, pl.BlockSpec((2, tq, kw), lambda p, c, i: (p, 0, 0))],
        out_specs=[blk] * 2, out_shape=[_sds(qv.shape, F32)] * 2,
        compiler_params=_cparams(3),
    )(qv, kv, vv, tab)


def _dil_bwd(qv, kv, vv, tab, dov, lsev, deltav, *, dil, L, tq, name):
    kw = tq + 2 * DIL_HALF
    npair = DIL_HEADS // 2

    def body(q_ref, k_ref, v_ref, t_ref, do_ref, lse_ref, dl_ref, dq_ref, dk_ref, dv_ref, dt_ref):
        masks = _pair_masks()
        c_id, i_id = pl.program_id(1), pl.program_id(2)
        start, valid = _dil_window(i_id, tq, kw, L)
        kwin = k_ref[pl.ds(start, kw), :]
        vwin = v_ref[pl.ds(start, kw), :]

        @pl.when(i_id == 0)
        def _():
            dk_ref[...] = jnp.zeros_like(dk_ref)
            dv_ref[...] = jnp.zeros_like(dv_ref)

        @pl.when((i_id == 0) & (c_id == 0))
        def _():
            dt_ref[...] = jnp.zeros_like(dt_ref)

        do = do_ref[...]
        dq = jnp.zeros((tq, LANES), F32)
        dk = jnp.zeros((kw, LANES), F32)
        dv = jnp.zeros((kw, LANES), F32)
        for hd in range(2):
            m = masks[hd]
            qh = jnp.where(m, q_ref[...], jnp.zeros_like(q_ref[...]))
            doh = jnp.where(m, do, jnp.zeros_like(do))
            s = lax.dot_general(qh, kwin, _NT, preferred_element_type=F32) + t_ref[hd]
            s = jnp.where(valid, s, NEG_INF)
            p = jnp.exp(s - _head_scalar(lse_ref[...], m))
            dp = lax.dot_general(doh, vwin, _NT, preferred_element_type=F32)
            ds = p * (dp - _head_scalar(dl_ref[...], m))
            dt_ref[hd] += ds
            dsb = ds.astype(CDT)
            dq = dq + jnp.where(m, jnp.dot(dsb, kwin, preferred_element_type=F32), 0.0)
            dk = dk + lax.dot_general(dsb, qh, _TN, preferred_element_type=F32)
            dv = dv + lax.dot_general(p.astype(CDT), doh, _TN, preferred_element_type=F32)
        dq_ref[...] = dq
        dk_ref[pl.ds(start, kw), :] += dk
        dv_ref[pl.ds(start, kw), :] += dv

    blk = pl.BlockSpec((None, tq, LANES), lambda p, c, i: (c, i, p))
    res = pl.BlockSpec((None, L + 2 * DIL_HALF, LANES), lambda p, c, i: (c, 0, p))
    tsp = pl.BlockSpec((2, tq, kw), lambda p, c, i: (p, 0, 0))
    return pl.pallas_call(
        body, name=name, grid=(npair, dil, L // tq),
        in_specs=[blk, res, res, tsp, blk, blk, blk], out_specs=[blk, res, res, tsp],
        out_shape=[_sds(qv.shape, F32), _sds(kv.shape, F32), _sds(kv.shape, F32), _sds(tab.shape, F32)],
        compiler_params=_cparams(3),
    )(qv, kv, vv, tab, dov, lsev, deltav)


def _mix_weights(l1, l2, l3):
    mx = jnp.maximum(jnp.maximum(l1, l2), l3)
    e1, e2, e3 = jnp.exp(l1 - mx), jnp.exp(l2 - mx), jnp.exp(l3 - mx)
    inv = 1.0 / (e1 + e2 + e3)
    return e1 * inv, e2 * inv, e3 * inv


def _branch_specs(S, ts):
    dils = [d for _, d in DIL_BRANCHES]
    return dils, [_res_spec(d, ts // d) for d in dils], [(d, S // d, DIL_W) for d in dils]


def _dil_mix_fwd(os, ls):
    S = os[0].shape[0] * os[0].shape[1]
    ts = min(256, S)
    dils, specs, _ = _branch_specs(S, ts)

    def body(o1, o2, o3, l1, l2, l3, out, scr):
        o1, o2, o3, l1, l2, l3 = [_by_token(r, scr, d) for r, d in zip((o1, o2, o3, l1, l2, l3), dils + dils)]
        w1, w2, w3 = _mix_weights(l1, l2, l3)
        out[...] = (w1 * o1 + w2 * o2 + w3 * o3).astype(CDT)

    return pl.pallas_call(
        body, name="dil_mix_fwd", grid=(S // ts,), in_specs=specs + specs, out_specs=_row(ts, DIL_W),
        out_shape=_sds((S, DIL_W), CDT), scratch_shapes=[_TOKEN_SCRATCH(ts)],
        compiler_params=_cparams(1),
    )(*os, *ls)


def _dil_mix_bwd(dcat, os, ls, j384):
    S = os[0].shape[0] * os[0].shape[1]
    ts = min(256, S)
    dils, specs, shapes = _branch_specs(S, ts)

    def body(do_ref, o1, o2, o3, l1, l2, l3, j_ref, d1, d2, d3, e1, e2, e3, scr):
        o1, o2, o3, l1, l2, l3 = [_by_token(r, scr, d) for r, d in zip((o1, o2, o3, l1, l2, l3), dils + dils)]
        ws = _mix_weights(l1, l2, l3)
        do = do_ref[...]
        o = ws[0] * o1 + ws[1] * o2 + ws[2] * o3
        dot = _headsum(do * o, j_ref[...])
        for w, d, d_o, e_o in zip(ws, dils, (d1, d2, d3), (e1, e2, e3)):
            _by_residue(w * do, scr, d_o, d)
            _by_residue(w * dot, scr, e_o, d)

    return pl.pallas_call(
        body, name="dil_mix_bwd", grid=(S // ts,),
        in_specs=[_row(ts, DIL_W, 1)] + specs + specs + [_full(j384.shape)],
        out_specs=specs + specs,
        out_shape=[_sds(s, CDT) for s in shapes] + [_sds(s, F32) for s in shapes],
        scratch_shapes=[_TOKEN_SCRATCH(ts)],
        compiler_params=_cparams(1),
    )(dcat, *os, *ls, j384)


def _adamw_math(w, g, m, v):
    m = ADAM_B1 * m + (1.0 - ADAM_B1) * g
    v = ADAM_B2 * v + (1.0 - ADAM_B2) * (g * g)
    m_hat = m / (1.0 - ADAM_B1 ** ADAM_STEP)
    v_hat = v / (1.0 - ADAM_B2 ** ADAM_STEP)
    delta = -ADAM_LR * (m_hat / (jnp.sqrt(v_hat) + ADAM_EPS) + ADAM_WD * w)
    return delta, m, v


def _pick8(n, target):
    best = None
    for t in range(16, min(n, target) + 1, 16):
        if n % t == 0:
            best = t
    return best if best is not None else n


_ELEMS_PER_BLOCK = 256 * 1024


def _lead_spec(a, b):
    ta = _pick8(a, max(16, _ELEMS_PER_BLOCK // b))
    return ta, pl.BlockSpec((None, ta, b), lambda l, i: (l, i, 0))


def _adamw(w, reds, sibs, m, v, owner, name):
    L, a, b = w.shape
    ta, spec = _lead_spec(a, b)
    gspec = pl.BlockSpec((ta, b), lambda l, i: (i, 0))

    def body(w_ref, r0_ref, r1_ref, s0_ref, s1_ref, m_ref, v_ref, g_o, d_o, m_o, v_o):
        mine = lax.axis_index("c") == owner
        g0 = jnp.where(mine, r0_ref[...], s0_ref[...])
        g1 = jnp.where(mine, r1_ref[...], s1_ref[...])
        g = jnp.where(pl.program_id(0) == 0, g0, g1)
        d, mm, vv = _adamw_math(w_ref[...], g, m_ref[...], v_ref[...])
        g_o[...] = g
        d_o[...] = d
        m_o[...] = mm
        v_o[...] = vv

    return pl.pallas_call(
        body, name=name, grid=(L, a // ta), in_specs=[spec] + [gspec] * 4 + [spec, spec], out_specs=[spec] * 4,
        out_shape=[_sds(w.shape, F32)] * 4, compiler_params=_cparams(2),
    )(w, *reds, *sibs, m, v)


def _adamw_small(w, gall, m, v):
    R = w.shape[0]

    def body(w_ref, g_ref, m_ref, v_ref, g_o, d_o, m_o, v_o):
        g = g_ref[0]
        for k in range(1, 8):
            g = g + g_ref[k]
        d, mm, vv = _adamw_math(w_ref[...], g, m_ref[...], v_ref[...])
        g_o[...] = g
        d_o[...] = d
        m_o[...] = mm
        v_o[...] = vv

    vm = pl.BlockSpec(memory_space=pltpu.VMEM)
    return pl.pallas_call(
        body, name="adamw_small", in_specs=[vm] * 4, out_specs=[vm] * 4, out_shape=[_sds((R, LANES), F32)] * 4,
    )(w, gall, m, v)


def _sum_pair(g, t, owner, name):
    n, a, b = t.shape
    ta = _pick8(a, max(16, _ELEMS_PER_BLOCK // b))
    spec = pl.BlockSpec((None, ta, b), lambda k, i, own: (k * own[0], i * own[0], 0))

    def body(own_ref, g_ref, t_ref, o_ref):
        @pl.when(own_ref[0] == 1)
        def _():
            o_ref[...] = (g_ref[...].astype(F32) + t_ref[...].astype(F32)).astype(o_ref.dtype)

    return pl.pallas_call(
        body, name=name, out_shape=_sds(t.shape, WIRE),
        grid_spec=pltpu.PrefetchScalarGridSpec(num_scalar_prefetch=1, grid=(n, a // ta), in_specs=[spec] * 2,
                                               out_specs=spec),
        compiler_params=_cparams(2),
    )(_is_core(owner), g, t)


def _sum_chips(pair, t, owner, name):
    _, a, b = t.shape
    ta = _pick8(a, max(16, _ELEMS_PER_BLOCK // b))

    def body(own_ref, p_ref, t_ref, o_ref):
        @pl.when(own_ref[0] == 1)
        def _():
            me = 2 * lax.axis_index("x") + lax.axis_index("y")
            acc = p_ref[me].astype(F32)
            for k in range(3):
                acc = acc + t_ref[k].astype(F32)
            o_ref[...] = acc

    return pl.pallas_call(
        body, name=name, out_shape=_sds((a, b), F32),
        grid_spec=pltpu.PrefetchScalarGridSpec(
            num_scalar_prefetch=1, grid=(a // ta,),
            in_specs=[pl.BlockSpec((4, ta, b), lambda i, own: (0, i * own[0], 0)),
                      pl.BlockSpec((3, ta, b), lambda i, own: (0, i * own[0], 0))],
            out_specs=pl.BlockSpec((ta, b), lambda i, own: (i * own[0], 0))),
        compiler_params=_cparams(1),
    )(_is_core(owner), pair, t)


def _is_core(core):
    return (lax.axis_index("c") == core).astype(jnp.int32).reshape(1)


_HBM = pl.BlockSpec(memory_space=pltpu.HBM)


def _place():
    x, y, c = lax.axis_index("x"), lax.axis_index("y"), lax.axis_index("c")
    chips = [(1 - x, y), (x, 1 - y), (1 - x, 1 - y)]
    return x, y, c, chips


def _remote(src, dst, ssem, rsem, to):
    return pltpu.make_async_remote_copy(src_ref=src, dst_ref=dst, send_sem=ssem, recv_sem=rsem, device_id=to,
                                        device_id_type=MESH_ID)


def _dma_sems(n):
    return pltpu.SemaphoreType.DMA((n,))


_SEM = pl.BlockSpec(memory_space=pltpu.SEMAPHORE)
_ANY = pl.BlockSpec(memory_space=pl.ANY)
_EFFECT = pltpu.SideEffectType.DATAFLOW_SIDE_EFFECTING
_BIG = ("w_in", "mla_w_uq", "mla_w_ukv", "w_out", "ffn_w_gate", "ffn_w_up", "ffn_w_down")
_OWNER = dict(zip(_BIG, (1, 0, 0, 1, 0, 0, 1)))
_ATTN_WEIGHTS, _FFN_WEIGHTS = _BIG[:4], _BIG[4:]


def _hbm(a):
    return pltpu.with_memory_space_constraint(a, pltpu.HBM)


def _per_core(c, owners, fn):
    for g in range(2):
        mine = tuple(p for p, o in enumerate(owners) if o == g)
        theirs = tuple(p for p, o in enumerate(owners) if o != g)
        pl.when(c == g)(functools.partial(fn, mine, theirs))


def _token_spec():
    return pl.BlockSpec(memory_space=pltpu.VMEM), _sds((8, LANES), F32)


def _gather_start(shards, owners, layer, tag):
    n = len(shards)
    lands = [_hbm(lax.empty((4,) + s.shape[1:], s.dtype)) for s in shards]

    def body(*refs):
        w_refs, l_refs = refs[:n], refs[n:2 * n]
        ssem, rsem, token = refs[2 * n], refs[2 * n + 1], refs[-1]
        x, y, c, chips = _place()
        me = 2 * x + y

        def send(mine, _):
            for i in mine:
                for j, (cx, cy) in enumerate(chips):
                    _remote(w_refs[i].at[layer], l_refs[i].at[me], ssem.at[3 * i + j], rsem.at[3 * i + j],
                            (cx, cy, c)).start()

        _per_core(c, owners, send)
        token[...] = jnp.zeros_like(token)

    tspec, tshape = _token_spec()
    out = pl.pallas_call(
        body, name=f"gather_start_{tag}", in_specs=[_HBM] * (2 * n),
        out_specs=[_SEM, _SEM] + [_HBM] * n + [tspec],
        out_shape=[_dma_sems(3 * n), _dma_sems(3 * n)] + [pltpu.HBM(l.shape, l.dtype) for l in lands] + [tshape],
        input_output_aliases={n + i: 2 + i for i in range(n)},
        compiler_params=pltpu.CompilerParams(has_side_effects=_EFFECT),
    )(*[_hbm(s) for s in shards], *lands)
    return out[0], out[1], list(out[2:2 + n]), out[-1]


def _gather_wait(ssem, rsem, shards, lands, after, owners, layer, tag):
    n = len(shards)

    def body(*refs):
        w_refs, l_refs = refs[:n], refs[n:2 * n]
        ssem, rsem = refs[2 * n], refs[2 * n + 1]
        x, y, c, chips = _place()

        def wait(mine, _):
            for i in mine:
                for j, (cx, cy) in enumerate(chips):
                    cp = _remote(w_refs[i].at[layer], l_refs[i].at[2 * cx + cy], ssem.at[3 * i + j], rsem.at[3 * i + j],
                                 (cx, cy, c))
                    cp.wait_send()
                    cp.wait_recv()

        _per_core(c, owners, wait)

    return list(pl.pallas_call(
        body, name=f"gather_wait_{tag}", in_specs=[_HBM] * (2 * n) + [_SEM, _SEM, _ANY], out_specs=[_HBM] * n,
        out_shape=[pltpu.HBM(l.shape, l.dtype) for l in lands],
        input_output_aliases={n + i: i for i in range(n)},
        compiler_params=pltpu.CompilerParams(has_side_effects=_EFFECT),
    )(*[_hbm(s) for s in shards], *lands, ssem, rsem, after))


def _gather_finish(shards, lands, owners, layer, tag):
    n = len(shards)

    def body(*refs):
        w_refs, g_refs = refs[:n], refs[2 * n:3 * n]
        ssem, rsem = refs[3 * n:]
        x, y, c, chips = _place()
        me = 2 * x + y
        sib = (x, y, 1 - c)
        owns = [_remote(w.at[layer], g.at[me], ssem.at[i], rsem.at[i], sib) for i, (w, g) in enumerate(zip(w_refs, g_refs))]
        for cp in owns:
            cp.start()

        def forward(mine, theirs):
            def blk(i, j):
                b = g_refs[i].at[2 * chips[j][0] + chips[j][1]]
                return _remote(b, b, ssem.at[n + 3 * i + j], rsem.at[n + 3 * i + j], sib)

            for i in mine:
                for j in range(3):
                    blk(i, j).start()
            for i in theirs:
                for j in range(3):
                    blk(i, j).wait_recv()
            for i in mine:
                for j in range(3):
                    blk(i, j).wait_send()

        _per_core(c, owners, forward)
        for cp in owns:
            cp.wait_recv()
            cp.wait_send()

    return list(pl.pallas_call(
        body, name=f"gather_finish_{tag}", in_specs=[_HBM] * (2 * n), out_specs=[_HBM] * n,
        out_shape=[_sds(l.shape, l.dtype) for l in lands], input_output_aliases={n + i: i for i in range(n)},
        scratch_shapes=[_dma_sems(4 * n), _dma_sems(4 * n)],
    )(*shards, *lands))


def _rs_to_owner(grads, owners, tag):
    n = len(grads)

    def body(*refs):
        g_refs, t_refs = refs[:n], refs[n:2 * n]
        ssem, rsem = refs[2 * n:]
        x, y, c, _ = _place()

        def swap(mine, theirs):
            cps = [_remote(g_refs[i], t_refs[i], ssem.at[i], rsem.at[i], (x, y, 1 - c)) for i in theirs]
            for cp in cps:
                cp.start()
            for i in mine:
                _remote(g_refs[i], t_refs[i], ssem.at[i], rsem.at[i], (x, y, 1 - c)).wait_recv()
            for cp in cps:
                cp.wait_send()

        _per_core(c, owners, swap)

    return list(pl.pallas_call(
        body, name=f"rs_to_owner_{tag}", in_specs=[_HBM] * n, out_specs=[_HBM] * n,
        out_shape=[_sds(g.shape, g.dtype) for g in grads], scratch_shapes=[_dma_sems(n), _dma_sems(n)],
    )(*grads))


def _a2a_start(pairs, owners, tag):
    n = len(pairs)
    lands = [_hbm(lax.empty((3,) + p.shape[1:], p.dtype)) for p in pairs]

    def body(*refs):
        a_refs, t_refs = refs[:n], refs[n:2 * n]
        ssem, rsem, token = refs[2 * n], refs[2 * n + 1], refs[-1]
        x, y, c, chips = _place()

        def send(mine, _):
            for i in mine:
                for j, (cx, cy) in enumerate(chips):
                    _remote(a_refs[i].at[2 * cx + cy], t_refs[i].at[j], ssem.at[3 * i + j], rsem.at[3 * i + j],
                            (cx, cy, c)).start()

        _per_core(c, owners, send)
        token[...] = jnp.zeros_like(token)

    tspec, tshape = _token_spec()
    out = pl.pallas_call(
        body, name=f"rs_a2a_start_{tag}", in_specs=[_HBM] * (2 * n),
        out_specs=[_SEM, _SEM] + [_HBM] * n + [tspec],
        out_shape=[_dma_sems(3 * n), _dma_sems(3 * n)] + [pltpu.HBM(l.shape, l.dtype) for l in lands] + [tshape],
        input_output_aliases={n + i: 2 + i for i in range(n)},
        compiler_params=pltpu.CompilerParams(has_side_effects=_EFFECT),
    )(*[_hbm(p) for p in pairs], *lands)
    return out[0], out[1], list(out[2:2 + n]), out[-1]


def _a2a_wait(ssem, rsem, pairs, lands, after, owners, tag):
    n = len(pairs)

    def body(*refs):
        a_refs, t_refs = refs[:n], refs[n:2 * n]
        ssem, rsem = refs[2 * n], refs[2 * n + 1]
        x, y, c, chips = _place()

        def wait(mine, _):
            for i in mine:
                for j, (cx, cy) in enumerate(chips):
                    cp = _remote(a_refs[i].at[2 * cx + cy], t_refs[i].at[j], ssem.at[3 * i + j], rsem.at[3 * i + j],
                                 (cx, cy, c))
                    cp.wait_send()
                    cp.wait_recv()

        _per_core(c, owners, wait)

    return list(pl.pallas_call(
        body, name=f"rs_a2a_wait_{tag}", in_specs=[_HBM] * (2 * n) + [_SEM, _SEM, _ANY], out_specs=[_HBM] * n,
        out_shape=[pltpu.HBM(l.shape, l.dtype) for l in lands],
        input_output_aliases={n + i: i for i in range(n)},
        compiler_params=pltpu.CompilerParams(has_side_effects=_EFFECT),
    )(*[_hbm(p) for p in pairs], *lands, ssem, rsem, after))


def _rs_from_owner(reds, owners):
    n = len(reds)

    def body(*refs):
        q_refs, o_refs = refs[:n], refs[n:2 * n]
        ssem, rsem = refs[2 * n:]
        x, y, c, _ = _place()

        def swap(mine, theirs):
            cps = [_remote(q_refs[k], o_refs[k], ssem.at[k], rsem.at[k], (x, y, 1 - c)) for k in mine]
            for cp in cps:
                cp.start()
            for k in theirs:
                _remote(q_refs[k], o_refs[k], ssem.at[k], rsem.at[k], (x, y, 1 - c)).wait_recv()
            for cp in cps:
                cp.wait_send()

        _per_core(c, owners, swap)

    return list(pl.pallas_call(
        body, name="rs_from_owner", in_specs=[_HBM] * n, out_specs=[_HBM] * n,
        out_shape=[_sds(q.shape, q.dtype) for q in reds], scratch_shapes=[_dma_sems(n), _dma_sems(n)],
    )(*reds))


def _gather_small(s):
    R, _ = s.shape

    def body(s_ref, o_ref, ssem, rsem, lsem):
        x, y, c, _ = _place()
        me = 4 * x + 2 * y + c
        own = pltpu.make_async_copy(s_ref, o_ref.at[me], lsem)
        own.start()
        sends = []
        for k in range(1, 8):
            px, py, pc = x ^ (k >> 2), y ^ ((k >> 1) & 1), c ^ (k & 1)
            cp = _remote(s_ref, o_ref.at[me], ssem.at[k - 1], rsem.at[k - 1], (px, py, pc))
            cp.start()
            sends.append(cp)
        for k in range(1, 8):
            px, py, pc = x ^ (k >> 2), y ^ ((k >> 1) & 1), c ^ (k & 1)
            blk = o_ref.at[4 * px + 2 * py + pc]
            _remote(blk, blk, ssem.at[k - 1], rsem.at[k - 1], (px, py, pc)).wait_recv()
        for cp in sends:
            cp.wait_send()
        own.wait()

    vm = pl.BlockSpec(memory_space=pltpu.VMEM)
    return pl.pallas_call(
        body, name="gather_small", in_specs=[vm], out_specs=vm, out_shape=_sds((8, R, LANES), s.dtype),
        scratch_shapes=[pltpu.SemaphoreType.DMA((7,)), pltpu.SemaphoreType.DMA((7,)), pltpu.SemaphoreType.DMA],
    )(s)


_COL_SHARDED =("w_in", "mla_w_uq", "mla_w_ukv", "ffn_w_gate", "ffn_w_up")
_SMALL = ("mla_q_norm", "mla_kv_norm", "gqa_q_norm", "gqa_k_norm", "rel_bias", "ln1_g", "ln1_b", "ln2_g", "ln2_b")


def _pack_flat(arrs, align):
    flat = jnp.concatenate([a.reshape(-1) for a in arrs])
    pad = (-flat.shape[0]) % align
    return jnp.pad(flat, (0, pad)) if pad else flat


def _unpack_flat(flat, shapes):
    out, off = [], 0
    for s in shapes:
        n = int(np.prod(s))
        out.append(flat[off:off + n].reshape(s))
        off += n
    return out


def _perm_gqa_rows(w):
    return jnp.concatenate([w[:832], w[896:960], w[832:896], w[960:]], axis=0)


def _local_step(x, target, small, depth, weights_of_layer, grads_done):
    S, D = x.shape
    alpha = (2.0 * depth) ** 0.25
    in_idx, uq_idx, ukv_idx = _in_cols(), _uq_cols(), _ukv_cols()
    win, wuq, wukv, wout, wg, wu, wdn = ([None] * depth for _ in range(7))

    tm, tg = _rope_tables(S)
    j256, j384 = _head_ones(256), _head_ones(384)
    mla_scale = (64 + MLA_ROPE_DIM) ** -0.5
    branches = []
    for (_, dil) in DIL_BRANCHES:
        L = S // dil
        tq = min(256, L)
        idx = jnp.asarray(_branch_bucket_idx(tq, dil))
        branches.append((dil, L, tq, idx))
    tabs = [_bias_expand(idx, small["rel_bias"], name=f"bias_expand_{b}") for b, (_, _, _, idx) in enumerate(branches)]

    def padded(a):
        z = jnp.zeros((DIL_HALF, a.shape[1]), a.dtype)
        return jnp.concatenate([z, a, z], axis=0)[None]

    saved = []
    xf, xb = x, x.astype(CDT)
    for l in range(depth):
        W, token = weights_of_layer(l, "attn", xb)
        win[l] = _rows_from_shards(W["w_in"], in_idx)
        wuq[l] = _rows_from_shards(W["mla_w_uq"], uq_idx)
        wukv[l] = _rows_from_shards(W["mla_w_ukv"], ukv_idx)
        wout[l] = _perm_gqa_rows(W["w_out"].reshape(-1, D))
        gq, gkv = small["mla_q_norm"][l][None], small["mla_kv_norm"][l][None]
        if token is not None:
            gq = gq + token[0, 0]
        ggq = jnp.tile(small["gqa_q_norm"][l], 4)[None]
        ggk = jnp.tile(small["gqa_k_norm"][l], 2)[None]
        h = _mm(xb, win[l], tb=True, name="mm_in")
        cq, ckv, kr, qd, kd, vd, qg, kg, vg, *strided = _prep_fwd(h, gq, gkv, ggq, ggk, tm, tg, j256)
        qkv = [(qd[None], padded(kd), padded(vd))] + [tuple(strided[3 * b:3 * b + 3]) for b in range(len(DIL_STRIDES))]
        qa = _mm(cq, wuq[l], tb=True, name="mm_uq")
        kvp = _mm(ckv, wukv[l], tb=True, out_dtype=CDT, name="mm_ukv")
        qm, km = _mla_prep_fwd(qa, kvp, kr, tm, mla_scale)
        oa, lsa = _attn_fwd(qm, km, kvp, split=True, npairs=3, kblk=lambda p: p, vblk=lambda p: 6 + p,
                            name="mla_attn_fwd")
        oc, lsc = _attn_fwd(qg, kg, vg, split=False, npairs=2, kblk=lambda p: 0, vblk=lambda p: 0,
                            name="gqa_attn_fwd")
        obs, lbs = [], []
        for b, (dil, L, tq, _) in enumerate(branches):
            o_b, l_b = _dil_fwd(*qkv[b], tabs[b], dil=dil, L=L, tq=tq, name=f"dil_fwd_{b}")
            obs.append(o_b)
            lbs.append(l_b)
        ob = _dil_mix_fwd(obs, lbs)
        cat = jnp.concatenate([oa, ob, oc], axis=1)
        mix = _mm(cat, wout[l], name="mm_out")
        x1, x1b, z1 = _ln_fwd(xf, mix, small["ln1_g"][l][None], small["ln1_b"][l][None], alpha, name="ln1_fwd")
        W, _ = weights_of_layer(l, "ffn", x1b)
        wg[l], wu[l], wdn[l] = W["ffn_w_gate"], W["ffn_w_up"], W["ffn_w_down"]
        g3, u3, act = _ffn_up(x1b, wg[l], wu[l])
        ff = _mm(act, wdn[l], ga=True, gb=True, name="mm_down")
        x2, x2b, z2 = _ln_fwd(x1, ff, small["ln2_g"][l][None], small["ln2_b"][l][None], alpha, name="ln2_fwd")
        saved.append(dict(xb=xb, h=h, cq=cq, ckv=ckv, qg=qg, kg=kg, vg=vg, kvp=kvp, qm=qm, km=km, oa=oa, lsa=lsa,
                          oc=oc, lsc=lsc, obs=obs, lbs=lbs, qkv=qkv, cat=cat, z1=z1, x1b=x1b, g3=g3, u3=u3, act=act, z2=z2,
                          gq=gq, gkv=gkv, ggq=ggq, ggk=ggk))
        xf, xb = x2, x2b

    gW = {k: [None] * depth for k in _BIG}
    gS = {k: [None] * depth for k in ("mla_q_norm", "mla_kv_norm", "gqa_q_norm", "gqa_k_norm", "ln1_g", "ln1_b", "ln2_g",
                                      "ln2_b")}
    g_rel = None
    dya, dyb = xf, target
    token = None
    for l in reversed(range(depth)):
        sv = saved[l]
        ln2_g = small["ln2_g"][l][None]
        if token is not None:
            ln2_g = ln2_g + token[0, 0]
        if l == depth - 1:
            dz2, dz2b, gS["ln2_g"][l], gS["ln2_b"][l], loss = _ln_bwd(dya, dyb, sv["z2"], ln2_g, alpha,
                                                                       name="ln2_bwd_loss", loss_head=True)
        else:
            dz2, dz2b, gS["ln2_g"][l], gS["ln2_b"][l] = _ln_bwd(dya, dyb, sv["z2"], ln2_g, alpha, name="ln2_bwd")
        gW["ffn_w_down"][l] = _mm(sv["act"], dz2b, ta=True, ga=True, go=True, out_dtype=WIRE, name="mm_down_dw")
        dg3, du3 = _ffn_down_dx(dz2b, wdn[l], sv["g3"], sv["u3"])
        gW["ffn_w_gate"][l] = _mm(dg3, sv["x1b"], ta=True, ga=True, go=True, out_dtype=WIRE, name="mm_gate_dw")
        gW["ffn_w_up"][l] = _mm(du3, sv["x1b"], ta=True, ga=True, go=True, out_dtype=WIRE, name="mm_up_dw")
        dx1 = _ffn_up_dx(dg3, du3, wg[l], wu[l])
        token = grads_done(l, "ffn", {n: gW[n][l] for n in _FFN_WEIGHTS})
        ln1_g = small["ln1_g"][l][None]
        if token is not None:
            ln1_g = ln1_g + token[0, 0]
        dz1, dz1b, gS["ln1_g"][l], gS["ln1_b"][l] = _ln_bwd(dx1, dz2, sv["z1"], ln1_g, alpha, name="ln1_bwd")
        gW["w_out"][l] = _perm_gqa_rows(_mm(sv["cat"], dz1b, ta=True, out_dtype=WIRE, name="mm_out_dw")).reshape(4, -1, D)
        dcat = _mm(dz1b, wout[l], tb=True, name="mm_out_dx")
        dqg, dkg, dvg = _attn_bwd(sv["qg"], sv["kg"], sv["vg"], dcat, sv["oc"], sv["lsc"], split=False, npairs=2,
                                  kblk=lambda p: 0, vblk=lambda p: 0, doblk=lambda p: 6 + p, shared_kv=True,
                                  name="gqa_attn_bwd")
        dqm, dkm, dvm = _attn_bwd(sv["qm"], sv["km"], sv["kvp"], dcat, sv["oa"], sv["lsa"], split=True, npairs=3,
                                  kblk=lambda p: p, vblk=lambda p: 6 + p, doblk=lambda p: p, shared_kv=False,
                                  name="mla_attn_bwd")
        dqa, dkvp, dkr = _mla_prep_bwd(dqm, dkm, dvm, tm, mla_scale)
        gW["mla_w_uq"][l] = _rows_to_shards(_mm(dqa, sv["cq"], ta=True, out_dtype=WIRE, name="mm_uq_dw"), uq_idx, MLA_HEADS * 96)
        dcq = _mm(dqa, wuq[l], name="mm_uq_dx")
        gW["mla_w_ukv"][l] = _rows_to_shards(_mm(dkvp, sv["ckv"], ta=True, out_dtype=WIRE, name="mm_ukv_dw"), ukv_idx, MLA_HEADS * 128)
        dckv = _mm(dkvp, wukv[l], name="mm_ukv_dx")
        mixb = _dil_mix_bwd(dcat, sv["obs"], sv["lbs"], j384)
        ddq, ddk, ddv = [], [], []
        for b, (dil, L, tq, idx) in enumerate(branches):
            dq_b, dk_b, dv_b, dtab = _dil_bwd(*sv["qkv"][b], tabs[b], mixb[b], sv["lbs"][b], mixb[3 + b], dil=dil, L=L,
                                              tq=tq, name=f"dil_bwd_{b}")
            if dil == 1:
                dq_b, dk_b, dv_b = dq_b[0], dk_b[0, DIL_HALF:DIL_HALF + S], dv_b[0, DIL_HALF:DIL_HALF + S]
            ddq.append(dq_b)
            ddk.append(dk_b)
            ddv.append(dv_b)
            g_b = _bias_reduce(idx, dtab, name=f"bias_reduce_{b}")[:, :, 0].T
            g_rel = g_b if g_rel is None else g_rel + g_b
        dh, n1, n2, n3, n4 = _prep_bwd(sv["h"], dcq, dckv, dkr, ddq, ddk, ddv, dqg, dkg, dvg, sv["gq"], sv["gkv"],
                                       sv["ggq"], sv["ggk"], tg, j256)
        gS["mla_q_norm"][l], gS["mla_kv_norm"][l] = n1[0], n2[0]
        gS["gqa_q_norm"][l] = n3[0].reshape(4, 64).sum(0)
        gS["gqa_k_norm"][l] = n4[0].reshape(2, 64).sum(0)
        gW["w_in"][l] = _rows_to_shards(_mm(dh, sv["xb"], ta=True, out_dtype=WIRE, name="mm_in_dw"), in_idx, IN_W)
        dya = _mm(dh, win[l], name="mm_in_dx")
        dyb = dz1
        token = grads_done(l, "attn", {n: gW[n][l] for n in _ATTN_WEIGHTS})
    grad_x = _axpy(dya, dyb, alpha, name="grad_x")

    gsmall = {k: jnp.stack([a.reshape(-1) for a in v]) for k, v in gS.items()}
    gsmall["rel_bias"] = g_rel
    return loss, grad_x, gsmall


_ORDER = ("w_in", "mla_q_norm", "mla_kv_norm", "mla_w_uq", "mla_w_ukv", "gqa_q_norm", "gqa_k_norm", "rel_bias", "w_out",
          "ln1_g", "ln1_b", "ffn_w_gate", "ffn_w_up", "ffn_w_down", "ln2_g", "ln2_b")


def kernel(x, w_in, mla_q_norm, mla_kv_norm, mla_w_uq, mla_w_ukv, gqa_q_norm, gqa_k_norm, rel_bias, w_out, ln1_g, ln1_b, ffn_w_gate, ffn_w_up, ffn_w_down, ln2_g, ln2_b, loss_target, m_w_in, m_mla_q_norm, m_mla_kv_norm, m_mla_w_uq, m_mla_w_ukv, m_gqa_q_norm, m_gqa_k_norm, m_rel_bias, m_w_out, m_ln1_g, m_ln1_b, m_ffn_w_gate, m_ffn_w_up, m_ffn_w_down, m_ln2_g, m_ln2_b, v_w_in, v_mla_q_norm, v_mla_kv_norm, v_mla_w_uq, v_mla_w_ukv, v_gqa_q_norm, v_gqa_k_norm, v_rel_bias, v_w_out, v_ln1_g, v_ln1_b, v_ffn_w_gate, v_ffn_w_up, v_ffn_w_down, v_ln2_g, v_ln2_b):
    wts = dict(zip(_ORDER, (w_in, mla_q_norm, mla_kv_norm, mla_w_uq, mla_w_ukv, gqa_q_norm, gqa_k_norm, rel_bias, w_out,
                            ln1_g, ln1_b, ffn_w_gate, ffn_w_up, ffn_w_down, ln2_g, ln2_b)))
    mom = dict(zip(_ORDER, (m_w_in, m_mla_q_norm, m_mla_kv_norm, m_mla_w_uq, m_mla_w_ukv, m_gqa_q_norm, m_gqa_k_norm,
                            m_rel_bias, m_w_out, m_ln1_g, m_ln1_b, m_ffn_w_gate, m_ffn_w_up, m_ffn_w_down, m_ln2_g,
                            m_ln2_b)))
    var = dict(zip(_ORDER, (v_w_in, v_mla_q_norm, v_mla_kv_norm, v_mla_w_uq, v_mla_w_ukv, v_gqa_q_norm, v_gqa_k_norm,
                            v_rel_bias, v_w_out, v_ln1_g, v_ln1_b, v_ffn_w_gate, v_ffn_w_up, v_ffn_w_down, v_ln2_g,
                            v_ln2_b)))
    small_shapes = [wts[n].shape for n in _SMALL]
    for d in (wts, mom, var):
        for n in _COL_SHARDED:
            d[n] = d[n].transpose(0, 2, 1)

    depth = 2
    shards = {n: wts[n].astype(WIRE) for n in _BIG}
    flying = {}

    def start_gather(names, l, tag):
        own = tuple(_OWNER[n] for n in names)
        sh = [shards[n] for n in names]
        ssem, rsem, lands, token = _gather_start(sh, own, l, tag)
        return (names, own, sh, ssem, rsem, lands, l, tag), token

    def end_gather(flight, after):
        names, own, sh, ssem, rsem, lands, l, tag = flight
        got = _gather_finish(sh, _gather_wait(ssem, rsem, sh, lands, after, own, l, tag), own, l, tag)
        return {n: g.astype(CDT) for n, g in zip(names, got)}

    def weights_of_layer(l, part, after):
        if (l, part) == (0, "attn"):
            got = end_gather(start_gather(_ATTN_WEIGHTS, 0, "attn0")[0], after)
            flying["ffn0"], t0 = start_gather(_FFN_WEIGHTS, 0, "ffn0")
            flying["layer1"], t1 = start_gather(_BIG, 1, "layer1")
            return got, t0 + t1
        if (l, part) == (0, "ffn"):
            return end_gather(flying.pop("ffn0"), after), None
        if part == "attn":
            flying["w1"] = end_gather(flying.pop("layer1"), after)
        return flying["w1"], None

    def grads_done(l, part, grads):
        names = tuple(grads)
        own = tuple(_OWNER[n] for n in names)
        tag = f"{part}{l}"
        gl = [grads[n] for n in names]
        theirs = _rs_to_owner(gl, own, tag)
        pairs = [_sum_pair(g, t, o, name=f"rs_pair_sum_{n}") for n, g, t, o in zip(names, gl, theirs, own)]
        ssem, rsem, lands, token = _a2a_start(pairs, own, tag)
        flying[tag] = (names, own, ssem, rsem, pairs, lands)
        return token

    small = {n: wts[n] for n in _SMALL}
    loss, grad_x, gsmall = _local_step(x[0], loss_target[0], small, depth, weights_of_layer, grads_done)

    reds = {}
    for l in reversed(range(depth)):
        for part in ("ffn", "attn"):
            tag = f"{part}{l}"
            names, own, ssem, rsem, pairs, lands = flying.pop(tag)
            got = _a2a_wait(ssem, rsem, pairs, lands, grad_x, own, tag)
            for n, p, t, o in zip(names, pairs, got, own):
                reds[n, l] = _sum_chips(p, t, o, name=f"rs_sum_chips_{n}")
    order = [(n, l) for l in range(depth) for n in _BIG]
    sibs = dict(zip(order, _rs_from_owner([reds[k] for k in order], tuple(_OWNER[n] for n, _ in order))))

    sflat = _pack_flat([gsmall[n].reshape(-1) for n in _SMALL], 8 * LANES)
    rs = sflat.shape[0] // LANES
    sall = _gather_small(sflat.reshape(rs, LANES))

    def packed(d):
        return _pack_flat([d[n] for n in _SMALL], 8 * LANES).reshape(rs, LANES)

    outs = {tag: {} for tag in ("grad", "delta", "new_m", "new_v")}
    for n in _BIG:
        res = _adamw(wts[n], [reds[n, 0], reds[n, 1]], [sibs[n, 0], sibs[n, 1]], mom[n], var[n], _OWNER[n],
                     name=f"adamw_{n}")
        for tag, r in zip(("grad", "delta", "new_m", "new_v"), res):
            outs[tag][n] = r.transpose(0, 2, 1) if n in _COL_SHARDED else r
    for tag, smallflat in zip(("grad", "delta", "new_m", "new_v"), _adamw_small(packed(wts), sall, packed(mom), packed(var))):
        outs[tag].update(zip(_SMALL, _unpack_flat(smallflat.reshape(-1), small_shapes)))

    total = lax.psum(loss[0, 0], ("x", "y", "c"))
    return (total, grad_x[None], *[outs["grad"][n] for n in _ORDER], *[outs["delta"][n] for n in _ORDER],
            *[outs["new_m"][n] for n in _ORDER], *[outs["new_v"][n] for n in _ORDER])
```

```python
import functools
import math

import numpy as np
import jax
import jax.numpy as jnp
from jax import lax
from jax.experimental import pallas as pl
from jax.experimental.pallas import tpu as pltpu

F32 = jnp.float32
CDT = jnp.bfloat16
WIRE = jnp.bfloat16

HEAD_DIM = 64
GRID_W = 64
ROPE_THETA = 10000.0
MLA_HEADS = 6
MLA_Q_RANK = 256
MLA_KV_RANK = 128
MLA_ROPE_DIM = 32
DIL_HEADS = 6
DIL_BRANCHES = ((128, 1), (512, 4), (2048, 16))
DIL_HALF = 64
GQA_Q_HEADS = 4
REL_BUCKETS = 32
REL_MAX_DIST = 1024
NEG_INF = -1e30
LANES = 128
VMEM_LIMIT = 56 * 1024 * 1024

ADAM_LR, ADAM_B1, ADAM_B2, ADAM_EPS, ADAM_WD, ADAM_STEP = 0.001, 0.9, 0.999, 1e-08, 0.01, 10

C_CQ, C_CKV, C_KR, C_DQ, C_DK, C_DV, C_GQ, C_GK, C_GV, IN_P = 0, 256, 384, 512, 896, 1280, 1664, 1920, 2048, 2176
IN_W = 2080
MESH_ID = pl.DeviceIdType.MESH


def _cparams(n_axes, vmem=VMEM_LIMIT):
    return pltpu.CompilerParams(dimension_semantics=("arbitrary",) * n_axes, vmem_limit_bytes=vmem)


MAX_WHOLE_DIM = 2304


def _pick(n, target):
    best = None
    for t in range(LANES, min(n, target) + 1, LANES):
        if n % t == 0:
            best = t
    if best is not None and (2 * best >= target or n > MAX_WHOLE_DIM):
        return best
    return n


def _sds(shape, dtype):
    return jax.ShapeDtypeStruct(tuple(shape), dtype)


def _in_cols():
    idx = -np.ones((IN_P,), np.int64)
    idx[C_CQ:C_CQ + 256] = np.arange(0, 256)
    idx[C_CKV:C_CKV + 128] = np.arange(256, 384)
    idx[C_KR + 64:C_KR + 96] = np.arange(384, 416)
    idx[C_DQ:C_DQ + 1152] = np.arange(416, 1568)
    gq = 1568 + (np.array([0, 2, 1, 3])[:, None] * 64 + np.arange(64)[None, :]).reshape(-1)
    idx[C_GQ:C_GQ + 256] = gq
    idx[C_GK:C_GK + 256] = np.arange(1824, 2080)
    return idx


def _uq_cols():
    idx = -np.ones((MLA_HEADS * 128,), np.int64)
    for h in range(MLA_HEADS):
        idx[h * 128:h * 128 + 96] = np.arange(96 * h, 96 * h + 96)
    return idx


def _ukv_cols():
    idx = -np.ones((MLA_HEADS * 128 + MLA_HEADS * 64,), np.int64)
    for h in range(MLA_HEADS):
        idx[h * 128:h * 128 + 64] = np.arange(128 * h, 128 * h + 64)
        idx[768 + h * 64:768 + h * 64 + 64] = np.arange(128 * h + 64, 128 * h + 128)
    return idx


def _runs(idx):
    out, i = [], 0
    while i < len(idx):
        j = i + 1
        while j < len(idx) and ((idx[i] < 0 and idx[j] < 0) or (idx[i] >= 0 and idx[j] == idx[j - 1] + 1)):
            j += 1
        out.append((int(idx[i]), j - i))
        i = j
    return out


def _rows_from_shards(sh, idx):
    _, cs, r = sh.shape
    pieces = []
    for first, ln in _runs(idx):
        if first < 0:
            pieces.append(jnp.zeros((ln, r), sh.dtype))
            continue
        while ln > 0:
            k, off = divmod(first, cs)
            take = min(ln, cs - off)
            pieces.append(sh[k, off:off + take, :])
            first, ln = first + take, ln - take
    return jnp.concatenate(pieces, axis=0)


def _rows_to_shards(wp, idx, n):
    inv = np.zeros((n,), np.int64)
    pos = np.nonzero(idx >= 0)[0]
    inv[idx[pos]] = pos
    cs = n // 4
    shards = []
    for k in range(4):
        pieces = [wp[first:first + ln, :] for first, ln in _runs(inv[k * cs:(k + 1) * cs])]
        shards.append(jnp.concatenate(pieces, axis=0))
    return jnp.stack(shards)


def _t5_bucket_np(rel):
    nb = REL_BUCKETS // 2
    exact = nb // 2
    ret = np.where(rel > 0, nb, 0)
    n = np.abs(rel)
    nf = np.maximum(n, 1).astype(np.float32)
    large = exact + (np.log(nf / np.float32(exact)) / np.float32(math.log(REL_MAX_DIST / exact))
                     * np.float32(nb - exact)).astype(np.int32)
    large = np.minimum(large, nb - 1)
    return ret + np.where(n < exact, n, large)


def _branch_bucket_idx(tq, dil):
    kw = tq + 2 * DIL_HALF
    rel = np.arange(kw)[None, :] - DIL_HALF - np.arange(tq)[:, None]
    idx = _t5_bucket_np(rel * dil)
    return np.where(np.abs(rel) <= DIL_HALF, idx, -1).astype(np.int32)


def _rope_tables(S):
    inv = ROPE_THETA ** (-jnp.arange(0, 32, 2, dtype=F32) / 32)
    t = jnp.arange(S)
    pos = t.astype(F32)
    row = (t // GRID_W).astype(F32)
    col = (t % GRID_W).astype(F32)
    lane = np.arange(LANES)
    wm = lane - 64
    is_rope = (lane >= 64) & (lane < 96)
    ang = pos[:, None] * inv[np.where(is_rope, wm % 16, 0)][None, :]
    cm = jnp.where(is_rope[None], jnp.cos(ang), 1.0)
    smm = jnp.where((is_rope & (wm < 16))[None], -jnp.sin(ang), 0.0)
    spm = jnp.where((is_rope & (wm >= 16))[None], jnp.sin(ang), 0.0)
    g = lane % 64
    w = g % 32
    angg = jnp.where((g < 32)[None], row[:, None], col[:, None]) * inv[w % 16][None, :]
    cg = jnp.cos(angg)
    smg = jnp.where((w < 16)[None], -jnp.sin(angg), 0.0)
    spg = jnp.where((w >= 16)[None], jnp.sin(angg), 0.0)
    return (cm, smm, spm), (cg, smg, spg)


def _lanes(t, width):
    return t if width == LANES else jnp.concatenate([t] * (width // LANES), axis=1)


def _rope(x, tabs):
    c, sm, sp = (_lanes(t, x.shape[1]) for t in tabs)
    w = x.shape[1]
    return x * c + pltpu.roll(x, w - 16, 1) * sm + pltpu.roll(x, 16, 1) * sp


def _rope_t(dy, tabs):
    c, sm, sp = (_lanes(t, dy.shape[1]) for t in tabs)
    w = dy.shape[1]
    return dy * c + pltpu.roll(dy * sm, 16, 1) + pltpu.roll(dy * sp, w - 16, 1)


def _head_ones(width):
    i = np.arange(width)
    return jnp.asarray((i[:, None] // HEAD_DIM == i[None, :] // HEAD_DIM).astype(np.float32))


def _headsum(x, j):
    return jnp.dot(x, j, preferred_element_type=F32, precision=lax.Precision.HIGHEST)


def _mm(a, b, *, ta=False, tb=False, ga=False, gb=False, go=False, out_dtype=F32, name):
    G = a.shape[0] if ga else (b.shape[0] if gb else 1)
    a2 = a.shape[1:] if ga else a.shape
    b2 = b.shape[1:] if gb else b.shape
    K, M = a2 if ta else a2[::-1]
    N = b2[0] if tb else b2[1]
    assert (b2[1] if tb else b2[0]) == K
    tm, tn, tk = _pick(M, 1024), _pick(N, 1024), _pick(K, 1024)
    if tm * tn > 1024 * 1152:
        tm = _pick(M, 512)
    nk = K // tk
    steps = nk if (go or G == 1) else G * nk
    dn = (((0 if ta else 1,), (1 if tb else 0,)), ((), ()))

    def body(a_ref, b_ref, o_ref, *acc):
        part = lax.dot_general(a_ref[...], b_ref[...], dn, preferred_element_type=F32)
        if steps == 1:
            o_ref[...] = part.astype(o_ref.dtype)
            return
        acc_ref, = acc
        s = pl.program_id(3)

        @pl.when(s == 0)
        def _():
            acc_ref[...] = part

        @pl.when(s > 0)
        def _():
            acc_ref[...] += part

        @pl.when(s == steps - 1)
        def _():
            o_ref[...] = acc_ref[...].astype(o_ref.dtype)

    def grp(g, s):
        return g if go else s // nk

    def kk(s):
        return s if steps == nk else s % nk

    def spec(grouped, block, index):
        if grouped:
            return pl.BlockSpec((None,) + block, lambda g, i, j, s: (grp(g, s),) + index(i, j, s))
        return pl.BlockSpec(block, lambda g, i, j, s: index(i, j, s))

    a_spec = (spec(ga, (tk, tm), lambda i, j, s: (kk(s), i)) if ta else spec(ga, (tm, tk), lambda i, j, s: (i, kk(s))))
    b_spec = (spec(gb, (tn, tk), lambda i, j, s: (j, kk(s))) if tb else spec(gb, (tk, tn), lambda i, j, s: (kk(s), j)))
    o_spec = spec(go, (tm, tn), lambda i, j, s: (i, j))
    return pl.pallas_call(
        body, name=name, grid=(G if go else 1, M // tm, N // tn, steps),
        in_specs=[a_spec, b_spec], out_specs=o_spec,
        out_shape=_sds(((G,) if go else ()) + (M, N), out_dtype),
        scratch_shapes=[pltpu.VMEM((tm, tn), F32)] if steps > 1 else [],
        compiler_params=_cparams(4),
    )(a, b)


def _row(ts, w, cb=0):
    return pl.BlockSpec((ts, w), lambda i: (i, cb))


def _full(shape):
    nd = len(shape)
    return pl.BlockSpec(tuple(shape), lambda i: (0,) * nd)


def _rms_fwd(x, g, eps=1e-6):
    r = lax.rsqrt(jnp.mean(x * x, axis=-1, keepdims=True) + eps)
    return x * r * g


def _rms_bwd(x, g, dy, eps=1e-6):
    r = lax.rsqrt(jnp.mean(x * x, axis=-1, keepdims=True) + eps)
    gdy = g * dy
    dx = r * gdy - x * (r * r * r) * jnp.mean(x * gdy, axis=-1, keepdims=True)
    return dx, x * r * dy


def _rms_head_fwd(x, g, j, eps=1e-6):
    r = lax.rsqrt(_headsum(x * x, j) * (1.0 / HEAD_DIM) + eps)
    return x * r * g


def _rms_head_bwd(x, g, dy, j, eps=1e-6):
    r = lax.rsqrt(_headsum(x * x, j) * (1.0 / HEAD_DIM) + eps)
    gdy = g * dy
    dx = r * gdy - x * (r * r * r) * (_headsum(x * gdy, j) * (1.0 / HEAD_DIM))
    return dx, x * r * dy


DIL_STRIDES = tuple(d for _, d in DIL_BRANCHES if d > 1)
DIL_W = DIL_HEADS * HEAD_DIM


def _res_spec(d, n, pad_blocks=0):
    return pl.BlockSpec((d, n, DIL_W), lambda i: (0, i + pad_blocks, 0))


def _prep_fwd(h, gq, gkv, ggq, ggk, tm, tg, j256):
    S = h.shape[0]
    ts = min(256, S)
    scale = HEAD_DIM ** -0.5
    nres = len(DIL_STRIDES)

    def body(*refs):
        (h_ref, gq_ref, gkv_ref, ggq_ref, ggk_ref, cm, smm, spm, cg, smg, spg, j_ref), refs = refs[:12], refs[12:]
        refs = refs[2 * nres:]
        (cq_o, ckv_o, kr_o, dq_o, dk_o, dv_o, gq_o, gk_o, gv_o), res_o = refs[:9], refs[9:-1]
        st = refs[-1]
        tabm = (cm[...], smm[...], spm[...])
        tabg = (cg[...], smg[...], spg[...])
        cq_o[...] = _rms_fwd(h_ref[:, C_CQ:C_CQ + 256], gq_ref[...]).astype(CDT)
        ckv_o[...] = _rms_fwd(h_ref[:, C_CKV:C_CKV + 128], gkv_ref[...]).astype(CDT)
        kr_o[...] = _rope(h_ref[:, C_KR:C_KR + 128], tabm).astype(CDT)
        dq_o[...] = (h_ref[:, C_DQ:C_DQ + 384] * scale).astype(CDT)
        dk_o[...] = h_ref[:, C_DK:C_DK + 384].astype(CDT)
        dv_o[...] = h_ref[:, C_DV:C_DV + 384].astype(CDT)
        for j, lanes in _lane_blocks(3 * DIL_W):
            st[j] = h_ref[:, C_DQ + lanes.start:C_DQ + lanes.stop] * (scale if j < 3 else 1.0)
        for bi, d in enumerate(DIL_STRIDES):
            for c in range(d):
                rows = pl.ds(c, ts // d, stride=d)
                for j, lanes in _lane_blocks(3 * DIL_W):
                    res_o[3 * bi + j // 3][c, :, (j % 3) * LANES:(j % 3 + 1) * LANES] = st.at[j][rows, :].astype(CDT)
        qn = _rms_head_fwd(h_ref[:, C_GQ:C_GQ + 256], ggq_ref[...], j_ref[...])
        gq_o[...] = (_rope(qn, tabg) * scale).astype(CDT)
        kn = _rms_head_fwd(h_ref[:, C_GK:C_GK + 128], ggk_ref[...], j_ref[0:128, 0:128])
        gk_o[...] = _rope(kn, tabg).astype(CDT)
        gv_o[...] = h_ref[:, C_GV:C_GV + 128].astype(CDT)

    widths = (256, 128, 128, 384, 384, 384, 256, 128, 128)
    out_specs = [_row(ts, w) for w in widths]
    out_shape = [_sds((S, w), CDT) for w in widths]
    zeros, aliases = [], {}
    for d in DIL_STRIDES:
        n, L = ts // d, S // d
        out_specs += [_res_spec(d, n), _res_spec(d, n, DIL_HALF // n), _res_spec(d, n, DIL_HALF // n)]
        out_shape += [_sds((d, L, DIL_W), CDT)] + [_sds((d, L + 2 * DIL_HALF, DIL_W), CDT)] * 2
        for t in range(2):
            aliases[12 + len(zeros)] = len(out_shape) - 2 + t
            zeros.append(jnp.zeros((d, L + 2 * DIL_HALF, DIL_W), CDT))
    return pl.pallas_call(
        body, name="prep_fwd", grid=(S // ts,),
        in_specs=[_row(ts, IN_P), _full(gq.shape), _full(gkv.shape), _full(ggq.shape), _full(ggk.shape)]
        + [_row(ts, LANES)] * 6 + [_full(j256.shape)] + [pl.BlockSpec(memory_space=pl.ANY)] * len(zeros),
        out_specs=out_specs, out_shape=out_shape, input_output_aliases=aliases,
        scratch_shapes=[pltpu.VMEM((3 * DIL_W // LANES, ts, LANES), F32)],
        compiler_params=_cparams(1),
    )(h, gq, gkv, ggq, ggk, *tm, *tg, j256, *zeros)


def _prep_bwd(h, dcq, dckv, dkr, ddq, ddk, ddv, dgq, dgk, dgv, gq, gkv, ggq, ggk, tg, j256):
    S = h.shape[0]
    ts = min(256, S)
    scale = HEAD_DIM ** -0.5

    def body(h_ref, dcq_r, dckv_r, dkr_r, q1, q2, q3, k1, k2, k3, v1, v2, v3, dgq_r, dgk_r, dgv_r,
             gq_ref, gkv_ref, ggq_ref, ggk_ref, cg, smg, spg, j_ref,
             dh_o, ngq_o, ngkv_o, nggq_o, nggk_o, *scr):
        tabg = (cg[...], smg[...], spg[...])
        first = pl.program_id(0) == 0
        scr, = scr
        d2, d3 = DIL_STRIDES
        dq = q1[...] + _by_token(q2, scr, d2) + _by_token(q3, scr, d3)
        dk = k1[...] + _by_token(k2, scr, d2) + _by_token(k3, scr, d3)
        dv = v1[...] + _by_token(v2, scr, d2) + _by_token(v3, scr, d3)

        def acc(o_ref, val):
            s = jnp.sum(val, axis=0, keepdims=True)

            @pl.when(first)
            def _():
                o_ref[...] = s

            @pl.when(jnp.logical_not(first))
            def _():
                o_ref[...] += s

        dx, dg = _rms_bwd(h_ref[:, C_CQ:C_CQ + 256], gq_ref[...], dcq_r[...])
        dh_o[:, C_CQ:C_CQ + 256] = dx.astype(CDT)
        acc(ngq_o, dg)
        dx, dg = _rms_bwd(h_ref[:, C_CKV:C_CKV + 128], gkv_ref[...], dckv_r[...])
        dh_o[:, C_CKV:C_CKV + 128] = dx.astype(CDT)
        acc(ngkv_o, dg)
        dh_o[:, C_KR:C_KR + 128] = dkr_r[...].astype(CDT)
        dh_o[:, C_DQ:C_DQ + 384] = (dq * scale).astype(CDT)
        dh_o[:, C_DK:C_DK + 384] = dk.astype(CDT)
        dh_o[:, C_DV:C_DV + 384] = dv.astype(CDT)
        dqn = _rope_t(dgq_r[...] * scale, tabg)
        dx, dg = _rms_head_bwd(h_ref[:, C_GQ:C_GQ + 256], ggq_ref[...], dqn, j_ref[...])
        dh_o[:, C_GQ:C_GQ + 256] = dx.astype(CDT)
        acc(nggq_o, dg)
        dkn = _rope_t(dgk_r[...], tabg)
        dx, dg = _rms_head_bwd(h_ref[:, C_GK:C_GK + 128], ggk_ref[...], dkn, j_ref[0:128, 0:128])
        dh_o[:, C_GK:C_GK + 128] = dx.astype(CDT)
        acc(nggk_o, dg)
        dh_o[:, C_GV:C_GV + 128] = dgv_r[...].astype(CDT)

    d2, d3 = DIL_STRIDES
    n2, n3 = ts // d2, ts // d3
    tok = _row(ts, DIL_W)
    return pl.pallas_call(
        body, name="prep_bwd", grid=(S // ts,),
        in_specs=[_row(ts, IN_P), _row(ts, 256), _row(ts, 128), _row(ts, 128)]
        + [tok, _res_spec(d2, n2), _res_spec(d3, n3)]
        + [tok, _res_spec(d2, n2, DIL_HALF // n2), _res_spec(d3, n3, DIL_HALF // n3)] * 2
        + [_row(ts, 256), _row(ts, 128), _row(ts, 128)]
        + [_full(gq.shape), _full(gkv.shape), _full(ggq.shape), _full(ggk.shape)] + [_row(ts, LANES)] * 3
        + [_full(j256.shape)],
        out_specs=[_row(ts, IN_P), _full((1, 256)), _full((1, 128)), _full((1, 256)), _full((1, 128))],
        out_shape=[_sds((S, IN_P), CDT), _sds((1, 256), F32), _sds((1, 128), F32), _sds((1, 256), F32),
                   _sds((1, 128), F32)],
        scratch_shapes=[_TOKEN_SCRATCH(ts)],
        compiler_params=_cparams(1),
    )(h, dcq, dckv, dkr, *ddq, *ddk, *ddv, dgq, dgk, dgv, gq, gkv, ggq, ggk, *tg, j256)


def _lane_blocks(width):
    return [(j, slice(j * LANES, (j + 1) * LANES)) for j in range(width // LANES)]


_TOKEN_SCRATCH = lambda ts: pltpu.VMEM((DIL_W // LANES, ts, LANES), F32)


def _by_token(res_ref, scr_ref, d):
    n = res_ref.shape[1]
    if d == 1:
        return res_ref[0].astype(F32)
    for c in range(d):
        for j, lanes in _lane_blocks(res_ref.shape[2]):
            scr_ref.at[j][pl.ds(c, n, stride=d), :] = res_ref[c, :, lanes].astype(F32)
    return jnp.concatenate([scr_ref[j] for j, _ in _lane_blocks(res_ref.shape[2])], axis=1)


def _by_residue(val, scr_ref, out_ref, d):
    n = out_ref.shape[1]
    if d == 1:
        out_ref[0] = val.astype(out_ref.dtype)
        return
    for j, lanes in _lane_blocks(out_ref.shape[2]):
        scr_ref[j] = val[:, lanes]
    for c in range(d):
        for j, lanes in _lane_blocks(out_ref.shape[2]):
            out_ref[c, :, lanes] = scr_ref.at[j][pl.ds(c, n, stride=d), :].astype(out_ref.dtype)


def _mla_prep_fwd(qa, kvp, kr, tm, scale):
    S = qa.shape[0]
    ts = min(256, S)

    def body(qa_ref, kv_ref, kr_ref, cm, smm, spm, q_o, k_o):
        tabm = (cm[...], smm[...], spm[...])
        q_o[...] = (_rope(qa_ref[...], tabm) * scale).astype(CDT)
        k_o[...] = kv_ref[:, 0:768] + _lanes(kr_ref[...], 768)

    return pl.pallas_call(
        body, name="mla_prep_fwd", grid=(S // ts,),
        in_specs=[_row(ts, 768), _row(ts, 1152), _row(ts, 128)] + [_row(ts, LANES)] * 3,
        out_specs=[_row(ts, 768)] * 2, out_shape=[_sds((S, 768), CDT)] * 2,
        compiler_params=_cparams(1),
    )(qa, kvp, kr, *tm)


def _mla_prep_bwd(dq, dk, dv, tm, scale):
    S = dq.shape[0]
    ts = min(256, S)

    def body(dq_ref, dk_ref, dv_ref, cm, smm, spm, dqa_o, dkv_o, dkr_o):
        tabm = (cm[...], smm[...], spm[...])
        lane = lax.broadcasted_iota(jnp.int32, (1, LANES), 1)
        dqa_o[...] = _rope_t(dq_ref[...] * scale, tabm).astype(CDT)
        dkr = jnp.zeros((ts, LANES), F32)
        for hd in range(MLA_HEADS):
            blk = dk_ref[:, hd * 128:(hd + 1) * 128]
            dkv_o[:, hd * 128:(hd + 1) * 128] = jnp.where(lane < 64, blk, 0.0).astype(CDT)
            dkr = dkr + jnp.where((lane >= 64) & (lane < 96), blk, 0.0)
        dkv_o[:, 768:1152] = dv_ref[...].astype(CDT)
        dkr_o[...] = jnp.where((lane >= 64) & (lane < 96), _rope_t(dkr, tabm), 0.0)

    return pl.pallas_call(
        body, name="mla_prep_bwd", grid=(S // ts,),
        in_specs=[_row(ts, 768), _row(ts, 768), _row(ts, 384)] + [_row(ts, LANES)] * 3,
        out_specs=[_row(ts, 768), _row(ts, 1152), _row(ts, 128)],
        out_shape=[_sds((S, 768), CDT), _sds((S, 1152), CDT), _sds((S, 128), F32)],
        compiler_params=_cparams(1),
    )(dq, dk, dv, *tm)


def _ln_fwd(xa, xb, g, b, alpha, name):
    S, D = xa.shape
    ts = min(256, S)

    def body(xa_ref, xb_ref, g_ref, b_ref, y_o, yb_o, z_o):
        z = alpha * xa_ref[...] + xb_ref[...]
        mu = jnp.mean(z, axis=-1, keepdims=True)
        zc = z - mu
        var = jnp.mean(zc * zc, axis=-1, keepdims=True)
        y = zc * lax.rsqrt(var + 1e-5) * g_ref[...] + b_ref[...]
        y_o[...] = y
        yb_o[...] = y.astype(CDT)
        z_o[...] = z

    return pl.pallas_call(
        body, name=name, grid=(S // ts,),
        in_specs=[_row(ts, D), _row(ts, D), _full(g.shape), _full(b.shape)],
        out_specs=[_row(ts, D)] * 3, out_shape=[_sds((S, D), F32), _sds((S, D), CDT), _sds((S, D), F32)],
        compiler_params=_cparams(1),
    )(xa, xb, g, b)


def _ln_bwd(dya, dyb, z, g, alpha, name, loss_head=False):
    S, D = z.shape
    ts = min(256, S)

    def body(dya_ref, dyb_ref, z_ref, g_ref, dz_o, dzb_o, dg_o, db_o, *loss_o):
        first = pl.program_id(0) == 0
        if loss_head:
            err = dya_ref[...] - dyb_ref[...]
            dy = err * (1.0 / D)
            part = jnp.sum(jnp.sum(err * err, axis=1, keepdims=True), axis=0, keepdims=True) * (0.5 / D)

            @pl.when(first)
            def _():
                loss_o[0][...] = part

            @pl.when(jnp.logical_not(first))
            def _():
                loss_o[0][...] += part
        else:
            dy = dya_ref[...] + alpha * dyb_ref[...]
        z = z_ref[...]
        mu = jnp.mean(z, axis=-1, keepdims=True)
        zc = z - mu
        r = lax.rsqrt(jnp.mean(zc * zc, axis=-1, keepdims=True) + 1e-5)
        xh = zc * r
        dxh = dy * g_ref[...]
        dz = r * (dxh - jnp.mean(dxh, axis=-1, keepdims=True) - xh * jnp.mean(dxh * xh, axis=-1, keepdims=True))
        dz_o[...] = dz
        dzb_o[...] = dz.astype(CDT)
        sg = jnp.sum(dy * xh, axis=0, keepdims=True)
        sb = jnp.sum(dy, axis=0, keepdims=True)

        @pl.when(first)
        def _():
            dg_o[...] = sg
            db_o[...] = sb

        @pl.when(jnp.logical_not(first))
        def _():
            dg_o[...] += sg
            db_o[...] += sb

    extra = ([_full((1, 1))], [_sds((1, 1), F32)]) if loss_head else ([], [])
    return pl.pallas_call(
        body, name=name, grid=(S // ts,),
        in_specs=[_row(ts, D)] * 3 + [_full(g.shape)],
        out_specs=[_row(ts, D), _row(ts, D), _full((1, D)), _full((1, D))] + extra[0],
        out_shape=[_sds((S, D), F32), _sds((S, D), CDT), _sds((1, D), F32), _sds((1, D), F32)] + extra[1],
        compiler_params=_cparams(1),
    )(dya, dyb, z, g)


def _grp_spec(ts, w):
    return pl.BlockSpec((None, ts, w), lambda k, i: (k, i, 0))


def _ffn_up(xb, wg3, wu3):
    S, D = xb.shape
    G, Fc, _ = wg3.shape
    tm = _pick(S, 1024)
    wspec = pl.BlockSpec((None, Fc, D), lambda k, i: (k, 0, 0))

    def body(x_ref, wg_ref, wu_ref, g_o, u_o, a_o):
        x = x_ref[...]
        g = lax.dot_general(x, wg_ref[...], _NT, preferred_element_type=F32)
        u = lax.dot_general(x, wu_ref[...], _NT, preferred_element_type=F32)
        g_o[...] = g.astype(CDT)
        u_o[...] = u.astype(CDT)
        a_o[...] = (g / (1.0 + jnp.exp(-g)) * u).astype(CDT)

    return pl.pallas_call(
        body, name="ffn_up", grid=(G, S // tm),
        in_specs=[pl.BlockSpec((tm, D), lambda k, i: (i, 0)), wspec, wspec], out_specs=[_grp_spec(tm, Fc)] * 3,
        out_shape=[_sds((G, S, Fc), CDT)] * 3, compiler_params=_cparams(2),
    )(xb, wg3, wu3)


def _ffn_up_dx(dg3, du3, wg3, wu3):
    G, S, Fc = dg3.shape
    D = wg3.shape[2]
    tm = _pick(S, 1024)
    wspec = pl.BlockSpec((None, Fc, D), lambda i, k: (k, 0, 0))
    aspec = pl.BlockSpec((None, tm, Fc), lambda i, k: (k, i, 0))

    def body(dg_ref, du_ref, wg_ref, wu_ref, o_ref):
        part = (jnp.dot(dg_ref[...], wg_ref[...], preferred_element_type=F32)
                + jnp.dot(du_ref[...], wu_ref[...], preferred_element_type=F32))
        k = pl.program_id(1)

        @pl.when(k == 0)
        def _():
            o_ref[...] = part

        @pl.when(k > 0)
        def _():
            o_ref[...] += part

    return pl.pallas_call(
        body, name="ffn_up_dx", grid=(S // tm, G), in_specs=[aspec, aspec, wspec, wspec],
        out_specs=pl.BlockSpec((tm, D), lambda i, k: (i, 0)), out_shape=_sds((S, D), F32), compiler_params=_cparams(2),
    )(dg3, du3, wg3, wu3)


def _ffn_down_dx(dzb, wd3, g3, u3):
    S, D = dzb.shape
    G, Fc, _ = wd3.shape
    tm = _pick(S, 1024)

    def body(dz_ref, wd_ref, g_ref, u_ref, dg_o, du_o):
        da = lax.dot_general(dz_ref[...], wd_ref[...], _NT, preferred_element_type=F32)
        g = g_ref[...].astype(F32)
        sg = 1.0 / (1.0 + jnp.exp(-g))
        dg_o[...] = (da * u_ref[...].astype(F32) * (sg * (1.0 + g * (1.0 - sg)))).astype(CDT)
        du_o[...] = (da * (g * sg)).astype(CDT)

    return pl.pallas_call(
        body, name="ffn_down_dx", grid=(G, S // tm),
        in_specs=[pl.BlockSpec((tm, D), lambda k, i: (i, 0)), pl.BlockSpec((None, Fc, D), lambda k, i: (k, 0, 0)),
                  _grp_spec(tm, Fc), _grp_spec(tm, Fc)],
        out_specs=[_grp_spec(tm, Fc)] * 2, out_shape=[_sds((G, S, Fc), CDT)] * 2, compiler_params=_cparams(2),
    )(dzb, wd3, g3, u3)


def _axpy(a, b, alpha, name):
    S, D = a.shape
    ts = min(256, S)

    def body(a_ref, b_ref, o_ref):
        o_ref[...] = a_ref[...] + alpha * b_ref[...]

    return pl.pallas_call(
        body, name=name, grid=(S // ts,), in_specs=[_row(ts, D)] * 2, out_specs=_row(ts, D),
        out_shape=_sds((S, D), F32), compiler_params=_cparams(1),
    )(a, b)


def _pair_masks():
    lane = lax.broadcasted_iota(jnp.int32, (1, LANES), 1)
    first = lane < HEAD_DIM
    return first, jnp.logical_not(first)


def _head_scalar(x, m):
    return jnp.max(jnp.where(m, x, -jnp.inf), axis=-1, keepdims=True)


_NT = (((1,), (1,)), ((), ()))
_TN = (((0,), (0,)), ((), ()))
ATTN_BWD_KEY_CHUNK = 1024


def _attn_fwd(q, k, v, *, split, npairs, kblk, vblk, name):
    S = q.shape[0]
    qw = 256 if split else LANES
    tq = min(256, S)

    def body(q_ref, k_ref, v_ref, o_ref, lse_ref):
        masks = _pair_masks()
        outs, lses = [], []
        for hd in range(2):
            if split:
                qh = q_ref[:, hd * LANES:(hd + 1) * LANES]
                kh = k_ref[:, hd * LANES:(hd + 1) * LANES]
            else:
                qh = jnp.where(masks[hd], q_ref[...], jnp.zeros_like(q_ref[...]))
                kh = k_ref[...]
            s = lax.dot_general(qh, kh, _NT, preferred_element_type=F32)
            mx = jnp.max(s, axis=-1, keepdims=True)
            p = jnp.exp(s - mx)
            l = jnp.sum(p, axis=-1, keepdims=True)
            o = jnp.dot(p.astype(CDT), v_ref[...], preferred_element_type=F32)
            outs.append(o / l)
            lses.append(jnp.broadcast_to(mx + jnp.log(l), (tq, LANES)))
        o_ref[...] = jnp.where(masks[0], outs[0], outs[1]).astype(o_ref.dtype)
        lse_ref[...] = jnp.where(masks[0], lses[0], lses[1])

    return pl.pallas_call(
        body, name=name, grid=(npairs, S // tq),
        in_specs=[pl.BlockSpec((tq, qw), lambda p, i: (i, p)),
                  pl.BlockSpec((S, qw), lambda p, i: (0, kblk(p))),
                  pl.BlockSpec((S, LANES), lambda p, i: (0, vblk(p)))],
        out_specs=[pl.BlockSpec((tq, LANES), lambda p, i: (i, p))] * 2,
        out_shape=[_sds((S, LANES * npairs), CDT), _sds((S, LANES * npairs), F32)],
        compiler_params=_cparams(2),
    )(q, k, v)


def _attn_bwd(q, k, v, do, o, lse, *, split, npairs, kblk, vblk, doblk, shared_kv, name):
    S = q.shape[0]
    qw = 256 if split else LANES
    tq = min(256, S)
    tkv = min(ATTN_BWD_KEY_CHUNK, S)
    nkv = 1 if shared_kv else npairs

    def body(q_ref, k_ref, v_ref, do_ref, o_ref, lse_ref, dq_ref, dk_ref, dv_ref):
        masks = _pair_masks()
        p_id, i_id = pl.program_id(0), pl.program_id(1)
        first = (i_id == 0) & ((p_id == 0) if shared_kv else True)
        @pl.when(first)
        def _():
            dk_ref[...] = jnp.zeros_like(dk_ref)
            dv_ref[...] = jnp.zeros_like(dv_ref)

        do = do_ref[...]
        o = o_ref[...].astype(F32)
        lse = lse_ref[...]
        heads = []
        for hd in range(2):
            m = masks[hd]
            cols = slice(hd * LANES, (hd + 1) * LANES) if split else slice(None)
            qh = q_ref[:, cols] if split else jnp.where(m, q_ref[...], jnp.zeros_like(q_ref[...]))
            doh = jnp.where(m, do, 0.0)
            heads.append((m, cols, qh, doh.astype(CDT), _head_scalar(lse, m), jnp.sum(doh * o, axis=-1, keepdims=True)))
        dqs = [jnp.zeros((tq, LANES), F32), jnp.zeros((tq, LANES), F32)]
        for ck in range(S // tkv):
            rows = slice(ck * tkv, (ck + 1) * tkv)
            v = v_ref[rows, :]
            dv = jnp.zeros((tkv, LANES), F32)
            for hd, (m, cols, qh, dohb, lse_h, delta) in enumerate(heads):
                kh = k_ref[rows, cols]
                s = lax.dot_general(qh, kh, _NT, preferred_element_type=F32)
                p = jnp.exp(s - lse_h)
                dp = lax.dot_general(dohb, v, _NT, preferred_element_type=F32)
                ds = (p * (dp - delta)).astype(CDT)
                dq = jnp.dot(ds, kh, preferred_element_type=F32)
                dqs[hd] = dqs[hd] + (dq if split else jnp.where(m, dq, 0.0))
                dk_ref[rows, cols] += lax.dot_general(ds, qh, _TN, preferred_element_type=F32)
                dv = dv + lax.dot_general(p.astype(CDT), dohb, _TN, preferred_element_type=F32)
            dv_ref[rows, :] += dv
        if split:
            dq_ref[:, 0:LANES] = dqs[0]
            dq_ref[:, LANES:2 * LANES] = dqs[1]
        else:
            dq_ref[...] = dqs[0] + dqs[1]

    kvo = (lambda p, i: (0, 0)) if shared_kv else (lambda p, i: (0, p))
    return pl.pallas_call(
        body, name=name, grid=(npairs, S // tq),
        in_specs=[pl.BlockSpec((tq, qw), lambda p, i: (i, p)),
                  pl.BlockSpec((S, qw), lambda p, i: (0, kblk(p))),
                  pl.BlockSpec((S, LANES), lambda p, i: (0, vblk(p))),
                  pl.BlockSpec((tq, LANES), lambda p, i: (i, doblk(p))),
                  pl.BlockSpec((tq, LANES), lambda p, i: (i, p)),
                  pl.BlockSpec((tq, LANES), lambda p, i: (i, p))],
        out_specs=[pl.BlockSpec((tq, qw), lambda p, i: (i, p)),
                   pl.BlockSpec((S, qw), kvo), pl.BlockSpec((S, LANES), kvo)],
        out_shape=[_sds((S, qw * npairs), F32), _sds((S, qw * nkv), F32), _sds((S, LANES * nkv), F32)],
        compiler_params=_cparams(2),
    )(q, k, v, do, o, lse)


def _bias_expand(idx, rel_bias, name):
    tq, kw = idx.shape

    def body(idx_ref, rb_ref, o_ref):
        idx = idx_ref[...]
        for hd in range(DIL_HEADS):
            acc = jnp.full((tq, kw), NEG_INF, F32)
            for u in range(REL_BUCKETS):
                acc = jnp.where(idx == u, rb_ref[u, hd], acc)
            o_ref[hd] = acc

    return pl.pallas_call(
        body, name=name,
        in_specs=[pl.BlockSpec(memory_space=pltpu.VMEM), pl.BlockSpec(memory_space=pltpu.SMEM)],
        out_specs=pl.BlockSpec(memory_space=pltpu.VMEM),
        out_shape=_sds((DIL_HEADS, tq, kw), F32),
    )(idx, rel_bias)


def _bias_reduce(idx, dtab, name):
    tq, kw = idx.shape

    def body(idx_ref, d_ref, o_ref):
        idx = idx_ref[...]
        rowid = lax.broadcasted_iota(jnp.int32, (REL_BUCKETS, kw), 0)
        for hd in range(DIL_HEADS):
            d = d_ref[hd]
            acc = jnp.zeros((REL_BUCKETS, kw), F32)
            for u in range(REL_BUCKETS):
                r = jnp.sum(jnp.where(idx == u, d, 0.0), axis=0, keepdims=True)
                acc = jnp.where(rowid == u, r, acc)
            o_ref[hd] = jnp.sum(acc, axis=1, keepdims=True)

    return pl.pallas_call(
        body, name=name,
        in_specs=[pl.BlockSpec(memory_space=pltpu.VMEM)] * 2, out_specs=pl.BlockSpec(memory_space=pltpu.VMEM),
        out_shape=_sds((DIL_HEADS, REL_BUCKETS, 1), F32),
    )(idx, dtab)


def _dil_window(i, tq, kw, L):
    start = pl.multiple_of(i * tq, DIL_HALF)
    key = start + lax.broadcasted_iota(jnp.int32, (1, kw), 1) - DIL_HALF
    return start, (key >= 0) & (key < L)


def _dil_fwd(qv, kv, vv, tab, *, dil, L, tq, name):
    kw = tq + 2 * DIL_HALF
    npair = DIL_HEADS // 2

    def body(q_ref, k_ref, v_ref, t_ref, o_ref, lse_ref):
        masks = _pair_masks()
        start, valid = _dil_window(pl.program_id(2), tq, kw, L)
        kwin = k_ref[pl.ds(start, kw), :]
        vwin = v_ref[pl.ds(start, kw), :]
        outs, lses = [], []
        for hd in range(2):
            qh = jnp.where(masks[hd], q_ref[...], jnp.zeros_like(q_ref[...]))
            s = lax.dot_general(qh, kwin, _NT, preferred_element_type=F32) + t_ref[hd]
            s = jnp.where(valid, s, NEG_INF)
            mx = jnp.max(s, axis=-1, keepdims=True)
            p = jnp.exp(s - mx)
            l = jnp.sum(p, axis=-1, keepdims=True)
            outs.append(jnp.dot(p.astype(CDT), vwin, preferred_element_type=F32) / l)
            lses.append(jnp.broadcast_to(mx + jnp.log(l), (tq, LANES)))
        o_ref[...] = jnp.where(masks[0], outs[0], outs[1])
        lse_ref[...] = jnp.where(masks[0], lses[0], lses[1])

    blk = pl.BlockSpec((None, tq, LANES), lambda p, c, i: (c, i, p))
    res = pl.BlockSpec((None, L + 2 * DIL_HALF, LANES), lambda p, c, i: (c, 0, p))
    return pl.pallas_call(
        body, name=name, grid=(npair, dil, L // tq),
        in_specs=[blk, res, res, pl.BlockSpec((2, tq, kw), lambda p, c, i: (p, 0, 0))],
        out_specs=[blk] * 2, out_shape=[_sds(qv.shape, F32)] * 2,
        compiler_params=_cparams(3),
    )(qv, kv, vv, tab)


def _dil_bwd(qv, kv, vv, tab, dov, lsev, deltav, *, dil, L, tq, name):
    kw = tq + 2 * DIL_HALF
    npair = DIL_HEADS // 2

    def body(q_ref, k_ref, v_ref, t_ref, do_ref, lse_ref, dl_ref, dq_ref, dk_ref, dv_ref, dt_ref):
        masks = _pair_masks()
        c_id, i_id = pl.program_id(1), pl.program_id(2)
        start, valid = _dil_window(i_id, tq, kw, L)
        kwin = k_ref[pl.ds(start, kw), :]
        vwin = v_ref[pl.ds(start, kw), :]

        @pl.when(i_id == 0)
        def _():
            dk_ref[...] = jnp.zeros_like(dk_ref)
            dv_ref[...] = jnp.zeros_like(dv_ref)

        @pl.when((i_id == 0) & (c_id == 0))
        def _():
            dt_ref[...] = jnp.zeros_like(dt_ref)

        do = do_ref[...]
        dq = jnp.zeros((tq, LANES), F32)
        dk = jnp.zeros((kw, LANES), F32)
        dv = jnp.zeros((kw, LANES), F32)
        for hd in range(2):
            m = masks[hd]
            qh = jnp.where(m, q_ref[...], jnp.zeros_like(q_ref[...]))
            doh = jnp.where(m, do, jnp.zeros_like(do))
            s = lax.dot_general(qh, kwin, _NT, preferred_element_type=F32) + t_ref[hd]
            s = jnp.where(valid, s, NEG_INF)
            p = jnp.exp(s - _head_scalar(lse_ref[...], m))
            dp = lax.dot_general(doh, vwin, _NT, preferred_element_type=F32)
            ds = p * (dp - _head_scalar(dl_ref[...], m))
            dt_ref[hd] += ds
            dsb = ds.astype(CDT)
            dq = dq + jnp.where(m, jnp.dot(dsb, kwin, preferred_element_type=F32), 0.0)
            dk = dk + lax.dot_general(dsb, qh, _TN, preferred_element_type=F32)
            dv = dv + lax.dot_general(p.astype(CDT), doh, _TN, preferred_element_type=F32)
        dq_ref[...] = dq
        dk_ref[pl.ds(start, kw), :] += dk
        dv_ref[pl.ds(start, kw), :] += dv

    blk = pl.BlockSpec((None, tq, LANES), lambda p, c, i: (c, i, p))
    res = pl.BlockSpec((None, L + 2 * DIL_HALF, LANES), lambda p, c, i: (c, 0, p))
    tsp = pl.BlockSpec((2, tq, kw), lambda p, c, i: (p, 0, 0))
    return pl.pallas_call(
        body, name=name, grid=(npair, dil, L // tq),
        in_specs=[blk, res, res, tsp, blk, blk, blk], out_specs=[blk, res, res, tsp],
        out_shape=[_sds(qv.shape, F32), _sds(kv.shape, F32), _sds(kv.shape, F32), _sds(tab.shape, F32)],
        compiler_params=_cparams(3),
    )(qv, kv, vv, tab, dov, lsev, deltav)


def _mix_weights(l1, l2, l3):
    mx = jnp.maximum(jnp.maximum(l1, l2), l3)
    e1, e2, e3 = jnp.exp(l1 - mx), jnp.exp(l2 - mx), jnp.exp(l3 - mx)
    inv = 1.0 / (e1 + e2 + e3)
    return e1 * inv, e2 * inv, e3 * inv


def _branch_specs(S, ts):
    dils = [d for _, d in DIL_BRANCHES]
    return dils, [_res_spec(d, ts // d) for d in dils], [(d, S // d, DIL_W) for d in dils]


def _dil_mix_fwd(os, ls):
    S = os[0].shape[0] * os[0].shape[1]
    ts = min(256, S)
    dils, specs, _ = _branch_specs(S, ts)

    def body(o1, o2, o3, l1, l2, l3, out, scr):
        o1, o2, o3, l1, l2, l3 = [_by_token(r, scr, d) for r, d in zip((o1, o2, o3, l1, l2, l3), dils + dils)]
        w1, w2, w3 = _mix_weights(l1, l2, l3)
        out[...] = (w1 * o1 + w2 * o2 + w3 * o3).astype(CDT)

    return pl.pallas_call(
        body, name="dil_mix_fwd", grid=(S // ts,), in_specs=specs + specs, out_specs=_row(ts, DIL_W),
        out_shape=_sds((S, DIL_W), CDT), scratch_shapes=[_TOKEN_SCRATCH(ts)],
        compiler_params=_cparams(1),
    )(*os, *ls)


def _dil_mix_bwd(dcat, os, ls, j384):
    S = os[0].shape[0] * os[0].shape[1]
    ts = min(256, S)
    dils, specs, shapes = _branch_specs(S, ts)

    def body(do_ref, o1, o2, o3, l1, l2, l3, j_ref, d1, d2, d3, e1, e2, e3, scr):
        o1, o2, o3, l1, l2, l3 = [_by_token(r, scr, d) for r, d in zip((o1, o2, o3, l1, l2, l3), dils + dils)]
        ws = _mix_weights(l1, l2, l3)
        do = do_ref[...]
        o = ws[0] * o1 + ws[1] * o2 + ws[2] * o3
        dot = _headsum(do * o, j_ref[...])
        for w, d, d_o, e_o in zip(ws, dils, (d1, d2, d3), (e1, e2, e3)):
            _by_residue(w * do, scr, d_o, d)
            _by_residue(w * dot, scr, e_o, d)

    return pl.pallas_call(
        body, name="dil_mix_bwd", grid=(S // ts,),
        in_specs=[_row(ts, DIL_W, 1)] + specs + specs + [_full(j384.shape)],
        out_specs=specs + specs,
        out_shape=[_sds(s, CDT) for s in shapes] + [_sds(s, F32) for s in shapes],
        scratch_shapes=[_TOKEN_SCRATCH(ts)],
        compiler_params=_cparams(1),
    )(dcat, *os, *ls, j384)


def _adamw_math(w, g, m, v):
    m = ADAM_B1 * m + (1.0 - ADAM_B1) * g
    v = ADAM_B2 * v + (1.0 - ADAM_B2) * (g * g)
    m_hat = m / (1.0 - ADAM_B1 ** ADAM_STEP)
    v_hat = v / (1.0 - ADAM_B2 ** ADAM_STEP)
    delta = -ADAM_LR * (m_hat / (jnp.sqrt(v_hat) + ADAM_EPS) + ADAM_WD * w)
    return delta, m, v


def _pick8(n, target):
    best = None
    for t in range(16, min(n, target) + 1, 16):
        if n % t == 0:
            best = t
    return best if best is not None else n


_ELEMS_PER_BLOCK = 256 * 1024


def _adamw(w, reds, sibs, m, v, owner, name):
    L, a, b = w.shape
    ta = _pick8(a, max(16, _ELEMS_PER_BLOCK // b))
    spec = pl.BlockSpec((None, ta, b), lambda l, i, own: (l, i, 0))
    def gspec(mine, layer):
        def index(l, i, own):
            use = (own[0] if mine else 1 - own[0]) * (l if layer else 1 - l)
            return i * use, 0
        return pl.BlockSpec((ta, b), index)

    def body(own_ref, w_ref, r0_ref, r1_ref, s0_ref, s1_ref, m_ref, v_ref, g_o, d_o, m_o, v_o):
        mine = own_ref[0] == 1
        g0 = jnp.where(mine, r0_ref[...], s0_ref[...])
        g1 = jnp.where(mine, r1_ref[...], s1_ref[...])
        g = jnp.where(pl.program_id(0) == 0, g0, g1)
        d, mm, vv = _adamw_math(w_ref[...], g, m_ref[...], v_ref[...])
        g_o[...] = g
        d_o[...] = d
        m_o[...] = mm
        v_o[...] = vv

    return pl.pallas_call(
        body, name=name, out_shape=[_sds(w.shape, F32)] * 4,
        grid_spec=pltpu.PrefetchScalarGridSpec(
            num_scalar_prefetch=1, grid=(L, a // ta),
            in_specs=[spec, gspec(True, 0), gspec(True, 1), gspec(False, 0), gspec(False, 1), spec, spec],
            out_specs=[spec] * 4),
        compiler_params=_cparams(2),
    )(_is_core(owner), w, *reds, *sibs, m, v)


def _adamw_small(w, gall, m, v):
    R = w.shape[0]

    def body(w_ref, g_ref, m_ref, v_ref, g_o, d_o, m_o, v_o):
        g = g_ref[0]
        for k in range(1, 8):
            g = g + g_ref[k]
        d, mm, vv = _adamw_math(w_ref[...], g, m_ref[...], v_ref[...])
        g_o[...] = g
        d_o[...] = d
        m_o[...] = mm
        v_o[...] = vv

    vm = pl.BlockSpec(memory_space=pltpu.VMEM)
    return pl.pallas_call(
        body, name="adamw_small", in_specs=[vm] * 4, out_specs=[vm] * 4, out_shape=[_sds((R, LANES), F32)] * 4,
    )(w, gall, m, v)


def _sum_pair(g, t, owner, name):
    n, a, b = t.shape
    ta = _pick8(a, max(16, _ELEMS_PER_BLOCK // b))
    spec = pl.BlockSpec((None, ta, b), lambda k, i, own: (k * own[0], i * own[0], 0))

    def body(own_ref, g_ref, t_ref, o_ref):
        @pl.when(own_ref[0] == 1)
        def _():
            o_ref[...] = (g_ref[...].astype(F32) + t_ref[...].astype(F32)).astype(o_ref.dtype)

    return pl.pallas_call(
        body, name=name, out_shape=_sds(t.shape, WIRE),
        grid_spec=pltpu.PrefetchScalarGridSpec(num_scalar_prefetch=1, grid=(n, a // ta), in_specs=[spec] * 2,
                                               out_specs=spec),
        compiler_params=_cparams(2),
    )(_is_core(owner), g, t)


def _sum_chips(pair, t, owner, name):
    _, a, b = t.shape
    ta = _pick8(a, max(16, _ELEMS_PER_BLOCK // b))

    def body(own_ref, p_ref, t_ref, o_ref):
        @pl.when(own_ref[0] == 1)
        def _():
            me = 2 * lax.axis_index("x") + lax.axis_index("y")
            acc = p_ref[me].astype(F32)
            for k in range(3):
                acc = acc + t_ref[k].astype(F32)
            o_ref[...] = acc

    return pl.pallas_call(
        body, name=name, out_shape=_sds((a, b), F32),
        grid_spec=pltpu.PrefetchScalarGridSpec(
            num_scalar_prefetch=1, grid=(a // ta,),
            in_specs=[pl.BlockSpec((4, ta, b), lambda i, own: (0, i * own[0], 0)),
                      pl.BlockSpec((3, ta, b), lambda i, own: (0, i * own[0], 0))],
            out_specs=pl.BlockSpec((ta, b), lambda i, own: (i * own[0], 0))),
        compiler_params=_cparams(1),
    )(_is_core(owner), pair, t)


def _is_core(core):
    return (lax.axis_index("c") == core).astype(jnp.int32).reshape(1)


_HBM = pl.BlockSpec(memory_space=pltpu.HBM)


def _place():
    x, y, c = lax.axis_index("x"), lax.axis_index("y"), lax.axis_index("c")
    chips = [(1 - x, y), (x, 1 - y), (1 - x, 1 - y)]
    return x, y, c, chips


def _remote(src, dst, ssem, rsem, to):
    return pltpu.make_async_remote_copy(src_ref=src, dst_ref=dst, send_sem=ssem, recv_sem=rsem, device_id=to,
                                        device_id_type=MESH_ID)


def _dma_sems(n):
    return pltpu.SemaphoreType.DMA((n,))


_SEM = pl.BlockSpec(memory_space=pltpu.SEMAPHORE)
_ANY = pl.BlockSpec(memory_space=pl.ANY)
_EFFECT = pltpu.SideEffectType.DATAFLOW_SIDE_EFFECTING
_BIG = ("w_in", "mla_w_uq", "mla_w_ukv", "w_out", "ffn_w_gate", "ffn_w_up", "ffn_w_down")
_OWNER = dict(zip(_BIG, (1, 0, 0, 1, 0, 0, 1)))
_ATTN_WEIGHTS, _FFN_WEIGHTS = _BIG[:4], _BIG[4:]


def _hbm(a):
    return pltpu.with_memory_space_constraint(a, pltpu.HBM)


def _per_core(c, owners, fn):
    for g in range(2):
        mine = tuple(p for p, o in enumerate(owners) if o == g)
        theirs = tuple(p for p, o in enumerate(owners) if o != g)
        pl.when(c == g)(functools.partial(fn, mine, theirs))


def _token_spec():
    return pl.BlockSpec(memory_space=pltpu.VMEM), _sds((8, LANES), F32)


def _gather_start(shards, owners, layer, tag):
    n = len(shards)
    lands = [_hbm(lax.empty((4,) + s.shape[1:], s.dtype)) for s in shards]

    def body(*refs):
        w_refs, l_refs = refs[:n], refs[n:2 * n]
        ssem, rsem, token = refs[2 * n], refs[2 * n + 1], refs[-1]
        x, y, c, chips = _place()
        me = 2 * x + y

        def send(mine, _):
            for i in mine:
                for j, (cx, cy) in enumerate(chips):
                    _remote(w_refs[i].at[layer], l_refs[i].at[me], ssem.at[3 * i + j], rsem.at[3 * i + j],
                            (cx, cy, c)).start()

        _per_core(c, owners, send)
        token[...] = jnp.zeros_like(token)

    tspec, tshape = _token_spec()
    out = pl.pallas_call(
        body, name=f"gather_start_{tag}", in_specs=[_HBM] * (2 * n),
        out_specs=[_SEM, _SEM] + [_HBM] * n + [tspec],
        out_shape=[_dma_sems(3 * n), _dma_sems(3 * n)] + [pltpu.HBM(l.shape, l.dtype) for l in lands] + [tshape],
        input_output_aliases={n + i: 2 + i for i in range(n)},
        compiler_params=pltpu.CompilerParams(has_side_effects=_EFFECT),
    )(*[_hbm(s) for s in shards], *lands)
    return out[0], out[1], list(out[2:2 + n]), out[-1]


def _gather_wait(ssem, rsem, shards, lands, after, owners, layer, tag):
    n = len(shards)

    def body(*refs):
        w_refs, l_refs = refs[:n], refs[n:2 * n]
        ssem, rsem = refs[2 * n], refs[2 * n + 1]
        x, y, c, chips = _place()

        def wait(mine, _):
            for i in mine:
                for j, (cx, cy) in enumerate(chips):
                    cp = _remote(w_refs[i].at[layer], l_refs[i].at[2 * cx + cy], ssem.at[3 * i + j], rsem.at[3 * i + j],
                                 (cx, cy, c))
                    cp.wait_send()
                    cp.wait_recv()

        _per_core(c, owners, wait)

    return list(pl.pallas_call(
        body, name=f"gather_wait_{tag}", in_specs=[_HBM] * (2 * n) + [_SEM, _SEM, _ANY], out_specs=[_HBM] * n,
        out_shape=[pltpu.HBM(l.shape, l.dtype) for l in lands],
        input_output_aliases={n + i: i for i in range(n)},
        compiler_params=pltpu.CompilerParams(has_side_effects=_EFFECT),
    )(*[_hbm(s) for s in shards], *lands, ssem, rsem, after))


def _gather_finish(shards, lands, owners, layer, tag):
    n = len(shards)

    def body(*refs):
        w_refs, g_refs = refs[:n], refs[2 * n:3 * n]
        ssem, rsem = refs[3 * n:]
        x, y, c, chips = _place()
        me = 2 * x + y
        sib = (x, y, 1 - c)
        owns = [_remote(w.at[layer], g.at[me], ssem.at[i], rsem.at[i], sib) for i, (w, g) in enumerate(zip(w_refs, g_refs))]
        for cp in owns:
            cp.start()

        def forward(mine, theirs):
            def blk(i, j):
                b = g_refs[i].at[2 * chips[j][0] + chips[j][1]]
                return _remote(b, b, ssem.at[n + 3 * i + j], rsem.at[n + 3 * i + j], sib)

            for i in mine:
                for j in range(3):
                    blk(i, j).start()
            for i in theirs:
                for j in range(3):
                    blk(i, j).wait_recv()
            for i in mine:
                for j in range(3):
                    blk(i, j).wait_send()

        _per_core(c, owners, forward)
        for cp in owns:
            cp.wait_recv()
            cp.wait_send()

    return list(pl.pallas_call(
        body, name=f"gather_finish_{tag}", in_specs=[_HBM] * (2 * n), out_specs=[_HBM] * n,
        out_shape=[_sds(l.shape, l.dtype) for l in lands], input_output_aliases={n + i: i for i in range(n)},
        scratch_shapes=[_dma_sems(4 * n), _dma_sems(4 * n)],
    )(*shards, *lands))


def _rs_to_owner(grads, owners, tag):
    n = len(grads)

    def body(*refs):
        g_refs, t_refs = refs[:n], refs[n:2 * n]
        ssem, rsem = refs[2 * n:]
        x, y, c, _ = _place()

        def swap(mine, theirs):
            cps = [_remote(g_refs[i], t_refs[i], ssem.at[i], rsem.at[i], (x, y, 1 - c)) for i in theirs]
            for cp in cps:
                cp.start()
            for i in mine:
                _remote(g_refs[i], t_refs[i], ssem.at[i], rsem.at[i], (x, y, 1 - c)).wait_recv()
            for cp in cps:
                cp.wait_send()

        _per_core(c, owners, swap)

    return list(pl.pallas_call(
        body, name=f"rs_to_owner_{tag}", in_specs=[_HBM] * n, out_specs=[_HBM] * n,
        out_shape=[_sds(g.shape, g.dtype) for g in grads], scratch_shapes=[_dma_sems(n), _dma_sems(n)],
    )(*grads))


def _a2a_start(pairs, owners, tag):
    n = len(pairs)
    lands = [_hbm(lax.empty((3,) + p.shape[1:], p.dtype)) for p in pairs]

    def body(*refs):
        a_refs, t_refs = refs[:n], refs[n:2 * n]
        ssem, rsem, token = refs[2 * n], refs[2 * n + 1], refs[-1]
        x, y, c, chips = _place()

        def send(mine, _):
            for i in mine:
                for j, (cx, cy) in enumerate(chips):
                    _remote(a_refs[i].at[2 * cx + cy], t_refs[i].at[j], ssem.at[3 * i + j], rsem.at[3 * i + j],
                            (cx, cy, c)).start()

        _per_core(c, owners, send)
        token[...] = jnp.zeros_like(token)

    tspec, tshape = _token_spec()
    out = pl.pallas_call(
        body, name=f"rs_a2a_start_{tag}", in_specs=[_HBM] * (2 * n),
        out_specs=[_SEM, _SEM] + [_HBM] * n + [tspec],
        out_shape=[_dma_sems(3 * n), _dma_sems(3 * n)] + [pltpu.HBM(l.shape, l.dtype) for l in lands] + [tshape],
        input_output_aliases={n + i: 2 + i for i in range(n)},
        compiler_params=pltpu.CompilerParams(has_side_effects=_EFFECT),
    )(*[_hbm(p) for p in pairs], *lands)
    return out[0], out[1], list(out[2:2 + n]), out[-1]


def _a2a_wait(ssem, rsem, pairs, lands, after, owners, tag):
    n = len(pairs)

    def body(*refs):
        a_refs, t_refs = refs[:n], refs[n:2 * n]
        ssem, rsem = refs[2 * n], refs[2 * n + 1]
        x, y, c, chips = _place()

        def wait(mine, _):
            for i in mine:
                for j, (cx, cy) in enumerate(chips):
                    cp = _remote(a_refs[i].at[2 * cx + cy], t_refs[i].at[j], ssem.at[3 * i + j], rsem.at[3 * i + j],
                                 (cx, cy, c))
                    cp.wait_send()
                    cp.wait_recv()

        _per_core(c, owners, wait)

    return list(pl.pallas_call(
        body, name=f"rs_a2a_wait_{tag}", in_specs=[_HBM] * (2 * n) + [_SEM, _SEM, _ANY], out_specs=[_HBM] * n,
        out_shape=[pltpu.HBM(l.shape, l.dtype) for l in lands],
        input_output_aliases={n + i: i for i in range(n)},
        compiler_params=pltpu.CompilerParams(has_side_effects=_EFFECT),
    )(*[_hbm(p) for p in pairs], *lands, ssem, rsem, after))


def _rs_from_owner(reds, owners):
    n = len(reds)

    def body(*refs):
        q_refs, o_refs = refs[:n], refs[n:2 * n]
        ssem, rsem = refs[2 * n:]
        x, y, c, _ = _place()

        def swap(mine, theirs):
            cps = [_remote(q_refs[k], o_refs[k], ssem.at[k], rsem.at[k], (x, y, 1 - c)) for k in mine]
            for cp in cps:
                cp.start()
            for k in theirs:
                _remote(q_refs[k], o_refs[k], ssem.at[k], rsem.at[k], (x, y, 1 - c)).wait_recv()
            for cp in cps:
                cp.wait_send()

        _per_core(c, owners, swap)

    return list(pl.pallas_call(
        body, name="rs_from_owner", in_specs=[_HBM] * n, out_specs=[_HBM] * n,
        out_shape=[_sds(q.shape, q.dtype) for q in reds], scratch_shapes=[_dma_sems(n), _dma_sems(n)],
    )(*reds))


def _gather_small(s):
    R, _ = s.shape

    def body(s_ref, o_ref, ssem, rsem, lsem):
        x, y, c, _ = _place()
        me = 4 * x + 2 * y + c
        own = pltpu.make_async_copy(s_ref, o_ref.at[me], lsem)
        own.start()
        sends = []
        for k in range(1, 8):
            px, py, pc = x ^ (k >> 2), y ^ ((k >> 1) & 1), c ^ (k & 1)
            cp = _remote(s_ref, o_ref.at[me], ssem.at[k - 1], rsem.at[k - 1], (px, py, pc))
            cp.start()
            sends.append(cp)
        for k in range(1, 8):
            px, py, pc = x ^ (k >> 2), y ^ ((k >> 1) & 1), c ^ (k & 1)
            blk = o_ref.at[4 * px + 2 * py + pc]
            _remote(blk, blk, ssem.at[k - 1], rsem.at[k - 1], (px, py, pc)).wait_recv()
        for cp in sends:
            cp.wait_send()
        own.wait()

    vm = pl.BlockSpec(memory_space=pltpu.VMEM)
    return pl.pallas_call(
        body, name="gather_small", in_specs=[vm], out_specs=vm, out_shape=_sds((8, R, LANES), s.dtype),
        scratch_shapes=[pltpu.SemaphoreType.DMA((7,)), pltpu.SemaphoreType.DMA((7,)), pltpu.SemaphoreType.DMA],
    )(s)


_COL_SHARDED =("w_in", "mla_w_uq", "mla_w_ukv", "ffn_w_gate", "ffn_w_up")
_SMALL = ("mla_q_norm", "mla_kv_norm", "gqa_q_norm", "gqa_k_norm", "rel_bias", "ln1_g", "ln1_b", "ln2_g", "ln2_b")


def _pack_flat(arrs, align):
    flat = jnp.concatenate([a.reshape(-1) for a in arrs])
    pad = (-flat.shape[0]) % align
    return jnp.pad(flat, (0, pad)) if pad else flat


def _unpack_flat(flat, shapes):
    out, off = [], 0
    for s in shapes:
        n = int(np.prod(s))
        out.append(flat[off:off + n].reshape(s))
        off += n
    return out


def _perm_gqa_rows(w):
    return jnp.concatenate([w[:832], w[896:960], w[832:896], w[960:]], axis=0)


def _local_step(x, target, small, depth, weights_of_layer, grads_done):
    S, D = x.shape
    alpha = (2.0 * depth) ** 0.25
    in_idx, uq_idx, ukv_idx = _in_cols(), _uq_cols(), _ukv_cols()
    win, wuq, wukv, wout, wg, wu, wdn = ([None] * depth for _ in range(7))

    tm, tg = _rope_tables(S)
    j256, j384 = _head_ones(256), _head_ones(384)
    mla_scale = (64 + MLA_ROPE_DIM) ** -0.5
    branches = []
    for (_, dil) in DIL_BRANCHES:
        L = S // dil
        tq = min(256, L)
        idx = jnp.asarray(_branch_bucket_idx(tq, dil))
        branches.append((dil, L, tq, idx))
    tabs = [_bias_expand(idx, small["rel_bias"], name=f"bias_expand_{b}") for b, (_, _, _, idx) in enumerate(branches)]

    def padded(a):
        z = jnp.zeros((DIL_HALF, a.shape[1]), a.dtype)
        return jnp.concatenate([z, a, z], axis=0)[None]

    saved = []
    xf, xb = x, x.astype(CDT)
    for l in range(depth):
        W, token = weights_of_layer(l, "attn", xb)
        win[l] = _rows_from_shards(W["w_in"], in_idx)
        wuq[l] = _rows_from_shards(W["mla_w_uq"], uq_idx)
        wukv[l] = _rows_from_shards(W["mla_w_ukv"], ukv_idx)
        wout[l] = _perm_gqa_rows(W["w_out"].reshape(-1, D))
        gq, gkv = small["mla_q_norm"][l][None], small["mla_kv_norm"][l][None]
        if token is not None:
            gq = gq + token[0, 0]
        ggq = jnp.tile(small["gqa_q_norm"][l], 4)[None]
        ggk = jnp.tile(small["gqa_k_norm"][l], 2)[None]
        h = _mm(xb, win[l], tb=True, name="mm_in")
        cq, ckv, kr, qd, kd, vd, qg, kg, vg, *strided = _prep_fwd(h, gq, gkv, ggq, ggk, tm, tg, j256)
        qkv = [(qd[None], padded(kd), padded(vd))] + [tuple(strided[3 * b:3 * b + 3]) for b in range(len(DIL_STRIDES))]
        qa = _mm(cq, wuq[l], tb=True, name="mm_uq")
        kvp = _mm(ckv, wukv[l], tb=True, out_dtype=CDT, name="mm_ukv")
        qm, km = _mla_prep_fwd(qa, kvp, kr, tm, mla_scale)
        oa, lsa = _attn_fwd(qm, km, kvp, split=True, npairs=3, kblk=lambda p: p, vblk=lambda p: 6 + p,
                            name="mla_attn_fwd")
        oc, lsc = _attn_fwd(qg, kg, vg, split=False, npairs=2, kblk=lambda p: 0, vblk=lambda p: 0,
                            name="gqa_attn_fwd")
        obs, lbs = [], []
        for b, (dil, L, tq, _) in enumerate(branches):
            o_b, l_b = _dil_fwd(*qkv[b], tabs[b], dil=dil, L=L, tq=tq, name=f"dil_fwd_{b}")
            obs.append(o_b)
            lbs.append(l_b)
        ob = _dil_mix_fwd(obs, lbs)
        cat = jnp.concatenate([oa, ob, oc], axis=1)
        mix = _mm(cat, wout[l], name="mm_out")
        x1, x1b, z1 = _ln_fwd(xf, mix, small["ln1_g"][l][None], small["ln1_b"][l][None], alpha, name="ln1_fwd")
        W, _ = weights_of_layer(l, "ffn", x1b)
        wg[l], wu[l], wdn[l] = W["ffn_w_gate"], W["ffn_w_up"], W["ffn_w_down"]
        g3, u3, act = _ffn_up(x1b, wg[l], wu[l])
        ff = _mm(act, wdn[l], ga=True, gb=True, name="mm_down")
        x2, x2b, z2 = _ln_fwd(x1, ff, small["ln2_g"][l][None], small["ln2_b"][l][None], alpha, name="ln2_fwd")
        saved.append(dict(xb=xb, h=h, cq=cq, ckv=ckv, qg=qg, kg=kg, vg=vg, kvp=kvp, qm=qm, km=km, oa=oa, lsa=lsa,
                          oc=oc, lsc=lsc, obs=obs, lbs=lbs, qkv=qkv, cat=cat, z1=z1, x1b=x1b, g3=g3, u3=u3, act=act, z2=z2,
                          gq=gq, gkv=gkv, ggq=ggq, ggk=ggk))
        xf, xb = x2, x2b

    gW = {k: [None] * depth for k in _BIG}
    gS = {k: [None] * depth for k in ("mla_q_norm", "mla_kv_norm", "gqa_q_norm", "gqa_k_norm", "ln1_g", "ln1_b", "ln2_g",
                                      "ln2_b")}
    g_rel = None
    dya, dyb = xf, target
    token = None
    for l in reversed(range(depth)):
        sv = saved[l]
        ln2_g = small["ln2_g"][l][None]
        if token is not None:
            ln2_g = ln2_g + token[0, 0]
        if l == depth - 1:
            dz2, dz2b, gS["ln2_g"][l], gS["ln2_b"][l], loss = _ln_bwd(dya, dyb, sv["z2"], ln2_g, alpha,
                                                                       name="ln2_bwd_loss", loss_head=True)
        else:
            dz2, dz2b, gS["ln2_g"][l], gS["ln2_b"][l] = _ln_bwd(dya, dyb, sv["z2"], ln2_g, alpha, name="ln2_bwd")
        gW["ffn_w_down"][l] = _mm(sv["act"], dz2b, ta=True, ga=True, go=True, out_dtype=WIRE, name="mm_down_dw")
        dg3, du3 = _ffn_down_dx(dz2b, wdn[l], sv["g3"], sv["u3"])
        gW["ffn_w_gate"][l] = _mm(dg3, sv["x1b"], ta=True, ga=True, go=True, out_dtype=WIRE, name="mm_gate_dw")
        gW["ffn_w_up"][l] = _mm(du3, sv["x1b"], ta=True, ga=True, go=True, out_dtype=WIRE, name="mm_up_dw")
        dx1 = _ffn_up_dx(dg3, du3, wg[l], wu[l])
        token = grads_done(l, "ffn", {n: gW[n][l] for n in _FFN_WEIGHTS})
        ln1_g = small["ln1_g"][l][None]
        if token is not None:
            ln1_g = ln1_g + token[0, 0]
        dz1, dz1b, gS["ln1_g"][l], gS["ln1_b"][l] = _ln_bwd(dx1, dz2, sv["z1"], ln1_g, alpha, name="ln1_bwd")
        gW["w_out"][l] = _perm_gqa_rows(_mm(sv["cat"], dz1b, ta=True, out_dtype=WIRE, name="mm_out_dw")).reshape(4, -1, D)
        dcat = _mm(dz1b, wout[l], tb=True, name="mm_out_dx")
        dqg, dkg, dvg = _attn_bwd(sv["qg"], sv["kg"], sv["vg"], dcat, sv["oc"], sv["lsc"], split=False, npairs=2,
                                  kblk=lambda p: 0, vblk=lambda p: 0, doblk=lambda p: 6 + p, shared_kv=True,
                                  name="gqa_attn_bwd")
        dqm, dkm, dvm = _attn_bwd(sv["qm"], sv["km"], sv["kvp"], dcat, sv["oa"], sv["lsa"], split=True, npairs=3,
                                  kblk=lambda p: p, vblk=lambda p: 6 + p, doblk=lambda p: p, shared_kv=False,
                                  name="mla_attn_bwd")
        dqa, dkvp, dkr = _mla_prep_bwd(dqm, dkm, dvm, tm, mla_scale)
        gW["mla_w_uq"][l] = _rows_to_shards(_mm(dqa, sv["cq"], ta=True, out_dtype=WIRE, name="mm_uq_dw"), uq_idx, MLA_HEADS * 96)
        dcq = _mm(dqa, wuq[l], name="mm_uq_dx")
        gW["mla_w_ukv"][l] = _rows_to_shards(_mm(dkvp, sv["ckv"], ta=True, out_dtype=WIRE, name="mm_ukv_dw"), ukv_idx, MLA_HEADS * 128)
        dckv = _mm(dkvp, wukv[l], name="mm_ukv_dx")
        mixb = _dil_mix_bwd(dcat, sv["obs"], sv["lbs"], j384)
        ddq, ddk, ddv = [], [], []
        for b, (dil, L, tq, idx) in enumerate(branches):
            dq_b, dk_b, dv_b, dtab = _dil_bwd(*sv["qkv"][b], tabs[b], mixb[b], sv["lbs"][b], mixb[3 + b], dil=dil, L=L,
                                              tq=tq, name=f"dil_bwd_{b}")
            if dil == 1:
                dq_b, dk_b, dv_b = dq_b[0], dk_b[0, DIL_HALF:DIL_HALF + S], dv_b[0, DIL_HALF:DIL_HALF + S]
            ddq.append(dq_b)
            ddk.append(dk_b)
            ddv.append(dv_b)
            g_b = _bias_reduce(idx, dtab, name=f"bias_reduce_{b}")[:, :, 0].T
            g_rel = g_b if g_rel is None else g_rel + g_b
        dh, n1, n2, n3, n4 = _prep_bwd(sv["h"], dcq, dckv, dkr, ddq, ddk, ddv, dqg, dkg, dvg, sv["gq"], sv["gkv"],
                                       sv["ggq"], sv["ggk"], tg, j256)
        gS["mla_q_norm"][l], gS["mla_kv_norm"][l] = n1[0], n2[0]
        gS["gqa_q_norm"][l] = n3[0].reshape(4, 64).sum(0)
        gS["gqa_k_norm"][l] = n4[0].reshape(2, 64).sum(0)
        gW["w_in"][l] = _rows_to_shards(_mm(dh, sv["xb"], ta=True, out_dtype=WIRE, name="mm_in_dw"), in_idx, IN_W)
        dya = _mm(dh, win[l], name="mm_in_dx")
        dyb = dz1
        token = grads_done(l, "attn", {n: gW[n][l] for n in _ATTN_WEIGHTS})
    grad_x = _axpy(dya, dyb, alpha, name="grad_x")

    gsmall = {k: jnp.stack([a.reshape(-1) for a in v]) for k, v in gS.items()}
    gsmall["rel_bias"] = g_rel
    return loss, grad_x, gsmall


_ORDER = ("w_in", "mla_q_norm", "mla_kv_norm", "mla_w_uq", "mla_w_ukv", "gqa_q_norm", "gqa_k_norm", "rel_bias", "w_out",
          "ln1_g", "ln1_b", "ffn_w_gate", "ffn_w_up", "ffn_w_down", "ln2_g", "ln2_b")


def kernel(x, w_in, mla_q_norm, mla_kv_norm, mla_w_uq, mla_w_ukv, gqa_q_norm, gqa_k_norm, rel_bias, w_out, ln1_g, ln1_b, ffn_w_gate, ffn_w_up, ffn_w_down, ln2_g, ln2_b, loss_target, m_w_in, m_mla_q_norm, m_mla_kv_norm, m_mla_w_uq, m_mla_w_ukv, m_gqa_q_norm, m_gqa_k_norm, m_rel_bias, m_w_out, m_ln1_g, m_ln1_b, m_ffn_w_gate, m_ffn_w_up, m_ffn_w_down, m_ln2_g, m_ln2_b, v_w_in, v_mla_q_norm, v_mla_kv_norm, v_mla_w_uq, v_mla_w_ukv, v_gqa_q_norm, v_gqa_k_norm, v_rel_bias, v_w_out, v_ln1_g, v_ln1_b, v_ffn_w_gate, v_ffn_w_up, v_ffn_w_down, v_ln2_g, v_ln2_b):
    wts = dict(zip(_ORDER, (w_in, mla_q_norm, mla_kv_norm, mla_w_uq, mla_w_ukv, gqa_q_norm, gqa_k_norm, rel_bias, w_out,
                            ln1_g, ln1_b, ffn_w_gate, ffn_w_up, ffn_w_down, ln2_g, ln2_b)))
    mom = dict(zip(_ORDER, (m_w_in, m_mla_q_norm, m_mla_kv_norm, m_mla_w_uq, m_mla_w_ukv, m_gqa_q_norm, m_gqa_k_norm,
                            m_rel_bias, m_w_out, m_ln1_g, m_ln1_b, m_ffn_w_gate, m_ffn_w_up, m_ffn_w_down, m_ln2_g,
                            m_ln2_b)))
    var = dict(zip(_ORDER, (v_w_in, v_mla_q_norm, v_mla_kv_norm, v_mla_w_uq, v_mla_w_ukv, v_gqa_q_norm, v_gqa_k_norm,
                            v_rel_bias, v_w_out, v_ln1_g, v_ln1_b, v_ffn_w_gate, v_ffn_w_up, v_ffn_w_down, v_ln2_g,
                            v_ln2_b)))
    small_shapes = [wts[n].shape for n in _SMALL]
    for d in (wts, mom, var):
        for n in _COL_SHARDED:
            d[n] = d[n].transpose(0, 2, 1)

    depth = 2
    shards = {n: wts[n].astype(WIRE) for n in _BIG}
    flying = {}

    def start_gather(names, l, tag):
        own = tuple(_OWNER[n] for n in names)
        sh = [shards[n] for n in names]
        ssem, rsem, lands, token = _gather_start(sh, own, l, tag)
        return (names, own, sh, ssem, rsem, lands, l, tag), token

    def end_gather(flight, after):
        names, own, sh, ssem, rsem, lands, l, tag = flight
        got = _gather_finish(sh, _gather_wait(ssem, rsem, sh, lands, after, own, l, tag), own, l, tag)
        return {n: g.astype(CDT) for n, g in zip(names, got)}

    def weights_of_layer(l, part, after):
        if (l, part) == (0, "attn"):
            got = end_gather(start_gather(_ATTN_WEIGHTS, 0, "attn0")[0], after)
            flying["ffn0"], t0 = start_gather(_FFN_WEIGHTS, 0, "ffn0")
            flying["layer1"], t1 = start_gather(_BIG, 1, "layer1")
            return got, t0 + t1
        if (l, part) == (0, "ffn"):
            return end_gather(flying.pop("ffn0"), after), None
        if part == "attn":
            flying["w1"] = end_gather(flying.pop("layer1"), after)
        return flying["w1"], None

    def grads_done(l, part, grads):
        names = tuple(grads)
        own = tuple(_OWNER[n] for n in names)
        tag = f"{part}{l}"
        gl = [grads[n] for n in names]
        theirs = _rs_to_owner(gl, own, tag)
        pairs = [_sum_pair(g, t, o, name=f"rs_pair_sum_{n}") for n, g, t, o in zip(names, gl, theirs, own)]
        ssem, rsem, lands, token = _a2a_start(pairs, own, tag)
        flying[tag] = (names, own, ssem, rsem, pairs, lands)
        return token

    small = {n: wts[n] for n in _SMALL}
    loss, grad_x, gsmall = _local_step(x[0], loss_target[0], small, depth, weights_of_layer, grads_done)

    reds = {}
    for l in reversed(range(depth)):
        for part in ("ffn", "attn"):
            tag = f"{part}{l}"
            names, own, ssem, rsem, pairs, lands = flying.pop(tag)
            got = _a2a_wait(ssem, rsem, pairs, lands, grad_x, own, tag)
            for n, p, t, o in zip(names, pairs, got, own):
                reds[n, l] = _sum_chips(p, t, o, name=f"rs_sum_chips_{n}")
    order = [(n, l) for l in range(depth) for n in _BIG]
    sibs = dict(zip(order, _rs_from_owner([reds[k] for k in order], tuple(_OWNER[n] for n, _ in order))))

    sflat = _pack_flat([gsmall[n].reshape(-1) for n in _SMALL], 8 * LANES)
    rs = sflat.shape[0] // LANES
    sall = _gather_small(sflat.reshape(rs, LANES))

    def packed(d):
        return _pack_flat([d[n] for n in _SMALL], 8 * LANES).reshape(rs, LANES)

    outs = {tag: {} for tag in ("grad", "delta", "new_m", "new_v")}
    for n in _BIG:
        res = _adamw(wts[n], [reds[n, 0], reds[n, 1]], [sibs[n, 0], sibs[n, 1]], mom[n], var[n], _OWNER[n],
                     name=f"adamw_{n}")
        for tag, r in zip(("grad", "delta", "new_m", "new_v"), res):
            outs[tag][n] = r.transpose(0, 2, 1) if n in _COL_SHARDED else r
    for tag, smallflat in zip(("grad", "delta", "new_m", "new_v"), _adamw_small(packed(wts), sall, packed(mom), packed(var))):
        outs[tag].update(zip(_SMALL, _unpack_flat(smallflat.reshape(-1), small_shapes)))

    total = lax.psum(loss[0, 0], ("x", "y", "c"))
    return (total, grad_x[None], *[outs["grad"][n] for n in _ORDER], *[outs["delta"][n] for n in _ORDER],
            *[outs["new_m"][n] for n in _ORDER], *[outs["new_v"][n] for n in _ORDER])
```

```python
import functools
import math

import numpy as np
import jax
import jax.numpy as jnp
from jax import lax
from jax.experimental import pallas as pl
from jax.experimental.pallas import tpu as pltpu

F32 = jnp.float32
CDT = jnp.bfloat16
WIRE = jnp.bfloat16

HEAD_DIM = 64
GRID_W = 64
ROPE_THETA = 10000.0
MLA_HEADS = 6
MLA_Q_RANK = 256
MLA_KV_RANK = 128
MLA_ROPE_DIM = 32
DIL_HEADS = 6
DIL_BRANCHES = ((128, 1), (512, 4), (2048, 16))
DIL_HALF = 64
GQA_Q_HEADS = 4
REL_BUCKETS = 32
REL_MAX_DIST = 1024
NEG_INF = -1e30
LANES = 128
VMEM_LIMIT = 56 * 1024 * 1024

ADAM_LR, ADAM_B1, ADAM_B2, ADAM_EPS, ADAM_WD, ADAM_STEP = 0.001, 0.9, 0.999, 1e-08, 0.01, 10

C_CQ, C_CKV, C_KR, C_DQ, C_DK, C_DV, C_GQ, C_GK, C_GV, IN_P = 0, 256, 384, 512, 896, 1280, 1664, 1920, 2048, 2176
IN_W = 2080
MESH_ID = pl.DeviceIdType.MESH


def _cparams(n_axes, vmem=VMEM_LIMIT):
    return pltpu.CompilerParams(dimension_semantics=("arbitrary",) * n_axes, vmem_limit_bytes=vmem)


MAX_WHOLE_DIM = 2304


def _pick(n, target):
    best = None
    for t in range(LANES, min(n, target) + 1, LANES):
        if n % t == 0:
            best = t
    if best is not None and (2 * best >= target or n > MAX_WHOLE_DIM):
        return best
    return n


def _sds(shape, dtype):
    return jax.ShapeDtypeStruct(tuple(shape), dtype)


def _in_cols():
    idx = -np.ones((IN_P,), np.int64)
    idx[C_CQ:C_CQ + 256] = np.arange(0, 256)
    idx[C_CKV:C_CKV + 128] = np.arange(256, 384)
    idx[C_KR + 64:C_KR + 96] = np.arange(384, 416)
    idx[C_DQ:C_DQ + 1152] = np.arange(416, 1568)
    gq = 1568 + (np.array([0, 2, 1, 3])[:, None] * 64 + np.arange(64)[None, :]).reshape(-1)
    idx[C_GQ:C_GQ + 256] = gq
    idx[C_GK:C_GK + 256] = np.arange(1824, 2080)
    return idx


def _uq_cols():
    idx = -np.ones((MLA_HEADS * 128,), np.int64)
    for h in range(MLA_HEADS):
        idx[h * 128:h * 128 + 96] = np.arange(96 * h, 96 * h + 96)
    return idx


def _ukv_cols():
    idx = -np.ones((MLA_HEADS * 128 + MLA_HEADS * 64,), np.int64)
    for h in range(MLA_HEADS):
        idx[h * 128:h * 128 + 64] = np.arange(128 * h, 128 * h + 64)
        idx[768 + h * 64:768 + h * 64 + 64] = np.arange(128 * h + 64, 128 * h + 128)
    return idx


def _runs(idx):
    out, i = [], 0
    while i < len(idx):
        j = i + 1
        while j < len(idx) and ((idx[i] < 0 and idx[j] < 0) or (idx[i] >= 0 and idx[j] == idx[j - 1] + 1)):
            j += 1
        out.append((int(idx[i]), j - i))
        i = j
    return out


def _rows_from_shards(sh, idx):
    _, cs, r = sh.shape
    pieces = []
    for first, ln in _runs(idx):
        if first < 0:
            pieces.append(jnp.zeros((ln, r), sh.dtype))
            continue
        while ln > 0:
            k, off = divmod(first, cs)
            take = min(ln, cs - off)
            pieces.append(sh[k, off:off + take, :])
            first, ln = first + take, ln - take
    return jnp.concatenate(pieces, axis=0)


def _rows_to_shards(wp, idx, n):
    inv = np.zeros((n,), np.int64)
    pos = np.nonzero(idx >= 0)[0]
    inv[idx[pos]] = pos
    cs = n // 4
    shards = []
    for k in range(4):
        pieces = [wp[first:first + ln, :] for first, ln in _runs(inv[k * cs:(k + 1) * cs])]
        shards.append(jnp.concatenate(pieces, axis=0))
    return jnp.stack(shards)


def _t5_bucket_np(rel):
    nb = REL_BUCKETS // 2
    exact = nb // 2
    ret = np.where(rel > 0, nb, 0)
    n = np.abs(rel)
    nf = np.maximum(n, 1).astype(np.float32)
    large = exact + (np.log(nf / np.float32(exact)) / np.float32(math.log(REL_MAX_DIST / exact))
                     * np.float32(nb - exact)).astype(np.int32)
    large = np.minimum(large, nb - 1)
    return ret + np.where(n < exact, n, large)


def _branch_bucket_idx(tq, dil):
    kw = tq + 2 * DIL_HALF
    rel = np.arange(kw)[None, :] - DIL_HALF - np.arange(tq)[:, None]
    idx = _t5_bucket_np(rel * dil)
    return np.where(np.abs(rel) <= DIL_HALF, idx, -1).astype(np.int32)


def _rope_tables(S):
    inv = ROPE_THETA ** (-jnp.arange(0, 32, 2, dtype=F32) / 32)
    t = jnp.arange(S)
    pos = t.astype(F32)
    row = (t // GRID_W).astype(F32)
    col = (t % GRID_W).astype(F32)
    lane = np.arange(LANES)
    wm = lane - 64
    is_rope = (lane >= 64) & (lane < 96)
    ang = pos[:, None] * inv[np.where(is_rope, wm % 16, 0)][None, :]
    cm = jnp.where(is_rope[None], jnp.cos(ang), 1.0)
    smm = jnp.where((is_rope & (wm < 16))[None], -jnp.sin(ang), 0.0)
    spm = jnp.where((is_rope & (wm >= 16))[None], jnp.sin(ang), 0.0)
    g = lane % 64
    w = g % 32
    angg = jnp.where((g < 32)[None], row[:, None], col[:, None]) * inv[w % 16][None, :]
    cg = jnp.cos(angg)
    smg = jnp.where((w < 16)[None], -jnp.sin(angg), 0.0)
    spg = jnp.where((w >= 16)[None], jnp.sin(angg), 0.0)
    return (cm, smm, spm), (cg, smg, spg)


def _lanes(t, width):
    return t if width == LANES else jnp.concatenate([t] * (width // LANES), axis=1)


def _rope(x, tabs):
    c, sm, sp = (_lanes(t, x.shape[1]) for t in tabs)
    w = x.shape[1]
    return x * c + pltpu.roll(x, w - 16, 1) * sm + pltpu.roll(x, 16, 1) * sp


def _rope_t(dy, tabs):
    c, sm, sp = (_lanes(t, dy.shape[1]) for t in tabs)
    w = dy.shape[1]
    return dy * c + pltpu.roll(dy * sm, 16, 1) + pltpu.roll(dy * sp, w - 16, 1)


def _head_ones(width):
    i = np.arange(width)
    return jnp.asarray((i[:, None] // HEAD_DIM == i[None, :] // HEAD_DIM).astype(np.float32))


def _headsum(x, j):
    return jnp.dot(x, j, preferred_element_type=F32, precision=lax.Precision.HIGHEST)


def _mm(a, b, *, ta=False, tb=False, ga=False, gb=False, go=False, out_dtype=F32, name):
    G = a.shape[0] if ga else (b.shape[0] if gb else 1)
    a2 = a.shape[1:] if ga else a.shape
    b2 = b.shape[1:] if gb else b.shape
    K, M = a2 if ta else a2[::-1]
    N = b2[0] if tb else b2[1]
    assert (b2[1] if tb else b2[0]) == K
    tm, tn, tk = _pick(M, 1024), _pick(N, 1024), _pick(K, 2048)
    if tm * tn > 1024 * 1152:
        tm = _pick(M, 512)
    nk = K // tk
    steps = nk if (go or G == 1) else G * nk
    dn = (((0 if ta else 1,), (1 if tb else 0,)), ((), ()))

    def body(a_ref, b_ref, o_ref, *acc):
        part = lax.dot_general(a_ref[...], b_ref[...], dn, preferred_element_type=F32)
        if steps == 1:
            o_ref[...] = part.astype(o_ref.dtype)
            return
        acc_ref, = acc
        s = pl.program_id(3)

        @pl.when(s == 0)
        def _():
            acc_ref[...] = part

        @pl.when(s > 0)
        def _():
            acc_ref[...] += part

        @pl.when(s == steps - 1)
        def _():
            o_ref[...] = acc_ref[...].astype(o_ref.dtype)

    def grp(g, s):
        return g if go else s // nk

    def kk(s):
        return s if steps == nk else s % nk

    def spec(grouped, block, index):
        if grouped:
            return pl.BlockSpec((None,) + block, lambda g, i, j, s: (grp(g, s),) + index(i, j, s))
        return pl.BlockSpec(block, lambda g, i, j, s: index(i, j, s))

    a_spec = (spec(ga, (tk, tm), lambda i, j, s: (kk(s), i)) if ta else spec(ga, (tm, tk), lambda i, j, s: (i, kk(s))))
    b_spec = (spec(gb, (tn, tk), lambda i, j, s: (j, kk(s))) if tb else spec(gb, (tk, tn), lambda i, j, s: (kk(s), j)))
    o_spec = spec(go, (tm, tn), lambda i, j, s: (i, j))
    return pl.pallas_call(
        body, name=name, grid=(G if go else 1, M // tm, N // tn, steps),
        in_specs=[a_spec, b_spec], out_specs=o_spec,
        out_shape=_sds(((G,) if go else ()) + (M, N), out_dtype),
        scratch_shapes=[pltpu.VMEM((tm, tn), F32)] if steps > 1 else [],
        compiler_params=_cparams(4),
    )(a, b)


def _row(ts, w, cb=0):
    return pl.BlockSpec((ts, w), lambda i: (i, cb))


def _full(shape):
    nd = len(shape)
    return pl.BlockSpec(tuple(shape), lambda i: (0,) * nd)


def _rms_fwd(x, g, eps=1e-6):
    r = lax.rsqrt(jnp.mean(x * x, axis=-1, keepdims=True) + eps)
    return x * r * g


def _rms_bwd(x, g, dy, eps=1e-6):
    r = lax.rsqrt(jnp.mean(x * x, axis=-1, keepdims=True) + eps)
    gdy = g * dy
    dx = r * gdy - x * (r * r * r) * jnp.mean(x * gdy, axis=-1, keepdims=True)
    return dx, x * r * dy


def _rms_head_fwd(x, g, j, eps=1e-6):
    r = lax.rsqrt(_headsum(x * x, j) * (1.0 / HEAD_DIM) + eps)
    return x * r * g


def _rms_head_bwd(x, g, dy, j, eps=1e-6):
    r = lax.rsqrt(_headsum(x * x, j) * (1.0 / HEAD_DIM) + eps)
    gdy = g * dy
    dx = r * gdy - x * (r * r * r) * (_headsum(x * gdy, j) * (1.0 / HEAD_DIM))
    return dx, x * r * dy


DIL_STRIDES = tuple(d for _, d in DIL_BRANCHES if d > 1)
DIL_W = DIL_HEADS * HEAD_DIM


def _res_spec(d, n, pad_blocks=0):
    return pl.BlockSpec((d, n, DIL_W), lambda i: (0, i + pad_blocks, 0))


def _prep_fwd(h, gq, gkv, ggq, ggk, tm, tg, j256):
    S = h.shape[0]
    ts = min(256, S)
    scale = HEAD_DIM ** -0.5
    nres = len(DIL_STRIDES)

    def body(*refs):
        (h_ref, gq_ref, gkv_ref, ggq_ref, ggk_ref, cm, smm, spm, cg, smg, spg, j_ref), refs = refs[:12], refs[12:]
        refs = refs[2 * nres:]
        (cq_o, ckv_o, kr_o, dq_o, dk_o, dv_o, gq_o, gk_o, gv_o), res_o = refs[:9], refs[9:-1]
        st = refs[-1]
        tabm = (cm[...], smm[...], spm[...])
        tabg = (cg[...], smg[...], spg[...])
        cq_o[...] = _rms_fwd(h_ref[:, C_CQ:C_CQ + 256], gq_ref[...]).astype(CDT)
        ckv_o[...] = _rms_fwd(h_ref[:, C_CKV:C_CKV + 128], gkv_ref[...]).astype(CDT)
        kr_o[...] = _rope(h_ref[:, C_KR:C_KR + 128], tabm).astype(CDT)
        dq_o[...] = (h_ref[:, C_DQ:C_DQ + 384] * scale).astype(CDT)
        dk_o[...] = h_ref[:, C_DK:C_DK + 384].astype(CDT)
        dv_o[...] = h_ref[:, C_DV:C_DV + 384].astype(CDT)
        for j, lanes in _lane_blocks(3 * DIL_W):
            st[j] = h_ref[:, C_DQ + lanes.start:C_DQ + lanes.stop] * (scale if j < 3 else 1.0)
        for bi, d in enumerate(DIL_STRIDES):
            for c in range(d):
                rows = pl.ds(c, ts // d, stride=d)
                for j, lanes in _lane_blocks(3 * DIL_W):
                    res_o[3 * bi + j // 3][c, :, (j % 3) * LANES:(j % 3 + 1) * LANES] = st.at[j][rows, :].astype(CDT)
        qn = _rms_head_fwd(h_ref[:, C_GQ:C_GQ + 256], ggq_ref[...], j_ref[...])
        gq_o[...] = (_rope(qn, tabg) * scale).astype(CDT)
        kn = _rms_head_fwd(h_ref[:, C_GK:C_GK + 128], ggk_ref[...], j_ref[0:128, 0:128])
        gk_o[...] = _rope(kn, tabg).astype(CDT)
        gv_o[...] = h_ref[:, C_GV:C_GV + 128].astype(CDT)

    widths = (256, 128, 128, 384, 384, 384, 256, 128, 128)
    out_specs = [_row(ts, w) for w in widths]
    out_shape = [_sds((S, w), CDT) for w in widths]
    zeros, aliases = [], {}
    for d in DIL_STRIDES:
        n, L = ts // d, S // d
        out_specs += [_res_spec(d, n), _res_spec(d, n, DIL_HALF // n), _res_spec(d, n, DIL_HALF // n)]
        out_shape += [_sds((d, L, DIL_W), CDT)] + [_sds((d, L + 2 * DIL_HALF, DIL_W), CDT)] * 2
        for t in range(2):
            aliases[12 + len(zeros)] = len(out_shape) - 2 + t
            zeros.append(jnp.zeros((d, L + 2 * DIL_HALF, DIL_W), CDT))
    return pl.pallas_call(
        body, name="prep_fwd", grid=(S // ts,),
        in_specs=[_row(ts, IN_P), _full(gq.shape), _full(gkv.shape), _full(ggq.shape), _full(ggk.shape)]
        + [_row(ts, LANES)] * 6 + [_full(j256.shape)] + [pl.BlockSpec(memory_space=pl.ANY)] * len(zeros),
        out_specs=out_specs, out_shape=out_shape, input_output_aliases=aliases,
        scratch_shapes=[pltpu.VMEM((3 * DIL_W // LANES, ts, LANES), F32)],
        compiler_params=_cparams(1),
    )(h, gq, gkv, ggq, ggk, *tm, *tg, j256, *zeros)


def _prep_bwd(h, dcq, dckv, dkr, ddq, ddk, ddv, dgq, dgk, dgv, gq, gkv, ggq, ggk, tg, j256):
    S = h.shape[0]
    ts = min(256, S)
    scale = HEAD_DIM ** -0.5

    def body(h_ref, dcq_r, dckv_r, dkr_r, q1, q2, q3, k1, k2, k3, v1, v2, v3, dgq_r, dgk_r, dgv_r,
             gq_ref, gkv_ref, ggq_ref, ggk_ref, cg, smg, spg, j_ref,
             dh_o, ngq_o, ngkv_o, nggq_o, nggk_o, *scr):
        tabg = (cg[...], smg[...], spg[...])
        first = pl.program_id(0) == 0
        scr, = scr
        d2, d3 = DIL_STRIDES
        dq = q1[...] + _by_token(q2, scr, d2) + _by_token(q3, scr, d3)
        dk = k1[...] + _by_token(k2, scr, d2) + _by_token(k3, scr, d3)
        dv = v1[...] + _by_token(v2, scr, d2) + _by_token(v3, scr, d3)

        def acc(o_ref, val):
            s = jnp.sum(val, axis=0, keepdims=True)

            @pl.when(first)
            def _():
                o_ref[...] = s

            @pl.when(jnp.logical_not(first))
            def _():
                o_ref[...] += s

        dx, dg = _rms_bwd(h_ref[:, C_CQ:C_CQ + 256], gq_ref[...], dcq_r[...])
        dh_o[:, C_CQ:C_CQ + 256] = dx.astype(CDT)
        acc(ngq_o, dg)
        dx, dg = _rms_bwd(h_ref[:, C_CKV:C_CKV + 128], gkv_ref[...], dckv_r[...])
        dh_o[:, C_CKV:C_CKV + 128] = dx.astype(CDT)
        acc(ngkv_o, dg)
        dh_o[:, C_KR:C_KR + 128] = dkr_r[...].astype(CDT)
        dh_o[:, C_DQ:C_DQ + 384] = (dq * scale).astype(CDT)
        dh_o[:, C_DK:C_DK + 384] = dk.astype(CDT)
        dh_o[:, C_DV:C_DV + 384] = dv.astype(CDT)
        dqn = _rope_t(dgq_r[...] * scale, tabg)
        dx, dg = _rms_head_bwd(h_ref[:, C_GQ:C_GQ + 256], ggq_ref[...], dqn, j_ref[...])
        dh_o[:, C_GQ:C_GQ + 256] = dx.astype(CDT)
        acc(nggq_o, dg)
        dkn = _rope_t(dgk_r[...], tabg)
        dx, dg = _rms_head_bwd(h_ref[:, C_GK:C_GK + 128], ggk_ref[...], dkn, j_ref[0:128, 0:128])
        dh_o[:, C_GK:C_GK + 128] = dx.astype(CDT)
        acc(nggk_o, dg)
        dh_o[:, C_GV:C_GV + 128] = dgv_r[...].astype(CDT)

    d2, d3 = DIL_STRIDES
    n2, n3 = ts // d2, ts // d3
    tok = _row(ts, DIL_W)
    return pl.pallas_call(
        body, name="prep_bwd", grid=(S // ts,),
        in_specs=[_row(ts, IN_P), _row(ts, 256), _row(ts, 128), _row(ts, 128)]
        + [tok, _res_spec(d2, n2), _res_spec(d3, n3)]
        + [tok, _res_spec(d2, n2, DIL_HALF // n2), _res_spec(d3, n3, DIL_HALF // n3)] * 2
        + [_row(ts, 256), _row(ts, 128), _row(ts, 128)]
        + [_full(gq.shape), _full(gkv.shape), _full(ggq.shape), _full(ggk.shape)] + [_row(ts, LANES)] * 3
        + [_full(j256.shape)],
        out_specs=[_row(ts, IN_P), _full((1, 256)), _full((1, 128)), _full((1, 256)), _full((1, 128))],
        out_shape=[_sds((S, IN_P), CDT), _sds((1, 256), F32), _sds((1, 128), F32), _sds((1, 256), F32),
                   _sds((1, 128), F32)],
        scratch_shapes=[_TOKEN_SCRATCH(ts)],
        compiler_params=_cparams(1),
    )(h, dcq, dckv, dkr, *ddq, *ddk, *ddv, dgq, dgk, dgv, gq, gkv, ggq, ggk, *tg, j256)


def _lane_blocks(width):
    return [(j, slice(j * LANES, (j + 1) * LANES)) for j in range(width // LANES)]


_TOKEN_SCRATCH = lambda ts: pltpu.VMEM((DIL_W // LANES, ts, LANES), F32)


def _by_token(res_ref, scr_ref, d):
    n = res_ref.shape[1]
    if d == 1:
        return res_ref[0].astype(F32)
    for c in range(d):
        for j, lanes in _lane_blocks(res_ref.shape[2]):
            scr_ref.at[j][pl.ds(c, n, stride=d), :] = res_ref[c, :, lanes].astype(F32)
    return jnp.concatenate([scr_ref[j] for j, _ in _lane_blocks(res_ref.shape[2])], axis=1)


def _by_residue(val, scr_ref, out_ref, d):
    n = out_ref.shape[1]
    if d == 1:
        out_ref[0] = val.astype(out_ref.dtype)
        return
    for j, lanes in _lane_blocks(out_ref.shape[2]):
        scr_ref[j] = val[:, lanes]
    for c in range(d):
        for j, lanes in _lane_blocks(out_ref.shape[2]):
            out_ref[c, :, lanes] = scr_ref.at[j][pl.ds(c, n, stride=d), :].astype(out_ref.dtype)


def _mla_prep_fwd(qa, kvp, kr, tm, scale):
    S = qa.shape[0]
    ts = min(256, S)

    def body(qa_ref, kv_ref, kr_ref, cm, smm, spm, q_o, k_o):
        tabm = (cm[...], smm[...], spm[...])
        q_o[...] = (_rope(qa_ref[...], tabm) * scale).astype(CDT)
        k_o[...] = kv_ref[:, 0:768] + _lanes(kr_ref[...], 768)

    return pl.pallas_call(
        body, name="mla_prep_fwd", grid=(S // ts,),
        in_specs=[_row(ts, 768), _row(ts, 1152), _row(ts, 128)] + [_row(ts, LANES)] * 3,
        out_specs=[_row(ts, 768)] * 2, out_shape=[_sds((S, 768), CDT)] * 2,
        compiler_params=_cparams(1),
    )(qa, kvp, kr, *tm)


def _mla_prep_bwd(dq, dk, dv, tm, scale):
    S = dq.shape[0]
    ts = min(256, S)

    def body(dq_ref, dk_ref, dv_ref, cm, smm, spm, dqa_o, dkv_o, dkr_o):
        tabm = (cm[...], smm[...], spm[...])
        lane = lax.broadcasted_iota(jnp.int32, (1, LANES), 1)
        dqa_o[...] = _rope_t(dq_ref[...] * scale, tabm).astype(CDT)
        dkr = jnp.zeros((ts, LANES), F32)
        for hd in range(MLA_HEADS):
            blk = dk_ref[:, hd * 128:(hd + 1) * 128]
            dkv_o[:, hd * 128:(hd + 1) * 128] = jnp.where(lane < 64, blk, 0.0).astype(CDT)
            dkr = dkr + jnp.where((lane >= 64) & (lane < 96), blk, 0.0)
        dkv_o[:, 768:1152] = dv_ref[...].astype(CDT)
        dkr_o[...] = jnp.where((lane >= 64) & (lane < 96), _rope_t(dkr, tabm), 0.0)

    return pl.pallas_call(
        body, name="mla_prep_bwd", grid=(S // ts,),
        in_specs=[_row(ts, 768), _row(ts, 768), _row(ts, 384)] + [_row(ts, LANES)] * 3,
        out_specs=[_row(ts, 768), _row(ts, 1152), _row(ts, 128)],
        out_shape=[_sds((S, 768), CDT), _sds((S, 1152), CDT), _sds((S, 128), F32)],
        compiler_params=_cparams(1),
    )(dq, dk, dv, *tm)


def _ln_fwd(xa, xb, g, b, alpha, name):
    S, D = xa.shape
    ts = min(256, S)

    def body(xa_ref, xb_ref, g_ref, b_ref, y_o, yb_o, z_o):
        z = alpha * xa_ref[...] + xb_ref[...]
        mu = jnp.mean(z, axis=-1, keepdims=True)
        zc = z - mu
        var = jnp.mean(zc * zc, axis=-1, keepdims=True)
        y = zc * lax.rsqrt(var + 1e-5) * g_ref[...] + b_ref[...]
        y_o[...] = y
        yb_o[...] = y.astype(CDT)
        z_o[...] = z

    return pl.pallas_call(
        body, name=name, grid=(S // ts,),
        in_specs=[_row(ts, D), _row(ts, D), _full(g.shape), _full(b.shape)],
        out_specs=[_row(ts, D)] * 3, out_shape=[_sds((S, D), F32), _sds((S, D), CDT), _sds((S, D), F32)],
        compiler_params=_cparams(1),
    )(xa, xb, g, b)


def _ln_bwd(dya, dyb, z, g, alpha, name, loss_head=False):
    S, D = z.shape
    ts = min(256, S)

    def body(dya_ref, dyb_ref, z_ref, g_ref, dz_o, dzb_o, dg_o, db_o, *loss_o):
        first = pl.program_id(0) == 0
        if loss_head:
            err = dya_ref[...] - dyb_ref[...]
            dy = err * (1.0 / D)
            part = jnp.sum(jnp.sum(err * err, axis=1, keepdims=True), axis=0, keepdims=True) * (0.5 / D)

            @pl.when(first)
            def _():
                loss_o[0][...] = part

            @pl.when(jnp.logical_not(first))
            def _():
                loss_o[0][...] += part
        else:
            dy = dya_ref[...] + alpha * dyb_ref[...]
        z = z_ref[...]
        mu = jnp.mean(z, axis=-1, keepdims=True)
        zc = z - mu
        r = lax.rsqrt(jnp.mean(zc * zc, axis=-1, keepdims=True) + 1e-5)
        xh = zc * r
        dxh = dy * g_ref[...]
        dz = r * (dxh - jnp.mean(dxh, axis=-1, keepdims=True) - xh * jnp.mean(dxh * xh, axis=-1, keepdims=True))
        dz_o[...] = dz
        dzb_o[...] = dz.astype(CDT)
        sg = jnp.sum(dy * xh, axis=0, keepdims=True)
        sb = jnp.sum(dy, axis=0, keepdims=True)

        @pl.when(first)
        def _():
            dg_o[...] = sg
            db_o[...] = sb

        @pl.when(jnp.logical_not(first))
        def _():
            dg_o[...] += sg
            db_o[...] += sb

    extra = ([_full((1, 1))], [_sds((1, 1), F32)]) if loss_head else ([], [])
    return pl.pallas_call(
        body, name=name, grid=(S // ts,),
        in_specs=[_row(ts, D)] * 3 + [_full(g.shape)],
        out_specs=[_row(ts, D), _row(ts, D), _full((1, D)), _full((1, D))] + extra[0],
        out_shape=[_sds((S, D), F32), _sds((S, D), CDT), _sds((1, D), F32), _sds((1, D), F32)] + extra[1],
        compiler_params=_cparams(1),
    )(dya, dyb, z, g)


def _grp_spec(ts, w):
    return pl.BlockSpec((None, ts, w), lambda k, i: (k, i, 0))


def _ffn_up(xb, wg3, wu3):
    S, D = xb.shape
    G, Fc, _ = wg3.shape
    tm = _pick(S, 1024)
    wspec = pl.BlockSpec((None, Fc, D), lambda k, i: (k, 0, 0))

    def body(x_ref, wg_ref, wu_ref, g_o, u_o, a_o):
        x = x_ref[...]
        g = lax.dot_general(x, wg_ref[...], _NT, preferred_element_type=F32)
        u = lax.dot_general(x, wu_ref[...], _NT, preferred_element_type=F32)
        g_o[...] = g.astype(CDT)
        u_o[...] = u.astype(CDT)
        a_o[...] = (g / (1.0 + jnp.exp(-g)) * u).astype(CDT)

    return pl.pallas_call(
        body, name="ffn_up", grid=(G, S // tm),
        in_specs=[pl.BlockSpec((tm, D), lambda k, i: (i, 0)), wspec, wspec], out_specs=[_grp_spec(tm, Fc)] * 3,
        out_shape=[_sds((G, S, Fc), CDT)] * 3, compiler_params=_cparams(2),
    )(xb, wg3, wu3)


def _ffn_up_dx(dg3, du3, wg3, wu3):
    G, S, Fc = dg3.shape
    D = wg3.shape[2]
    tm = _pick(S, 1024)
    wspec = pl.BlockSpec((None, Fc, D), lambda i, k: (k, 0, 0))
    aspec = pl.BlockSpec((None, tm, Fc), lambda i, k: (k, i, 0))

    def body(dg_ref, du_ref, wg_ref, wu_ref, o_ref):
        part = (jnp.dot(dg_ref[...], wg_ref[...], preferred_element_type=F32)
                + jnp.dot(du_ref[...], wu_ref[...], preferred_element_type=F32))
        k = pl.program_id(1)

        @pl.when(k == 0)
        def _():
            o_ref[...] = part

        @pl.when(k > 0)
        def _():
            o_ref[...] += part

    return pl.pallas_call(
        body, name="ffn_up_dx", grid=(S // tm, G), in_specs=[aspec, aspec, wspec, wspec],
        out_specs=pl.BlockSpec((tm, D), lambda i, k: (i, 0)), out_shape=_sds((S, D), F32), compiler_params=_cparams(2),
    )(dg3, du3, wg3, wu3)


def _ffn_down_dx(dzb, wd3, g3, u3):
    S, D = dzb.shape
    G, Fc, _ = wd3.shape
    tm = _pick(S, 1024)

    def body(dz_ref, wd_ref, g_ref, u_ref, dg_o, du_o):
        da = lax.dot_general(dz_ref[...], wd_ref[...], _NT, preferred_element_type=F32)
        g = g_ref[...].astype(F32)
        sg = 1.0 / (1.0 + jnp.exp(-g))
        dg_o[...] = (da * u_ref[...].astype(F32) * (sg * (1.0 + g * (1.0 - sg)))).astype(CDT)
        du_o[...] = (da * (g * sg)).astype(CDT)

    return pl.pallas_call(
        body, name="ffn_down_dx", grid=(G, S // tm),
        in_specs=[pl.BlockSpec((tm, D), lambda k, i: (i, 0)), pl.BlockSpec((None, Fc, D), lambda k, i: (k, 0, 0)),
                  _grp_spec(tm, Fc), _grp_spec(tm, Fc)],
        out_specs=[_grp_spec(tm, Fc)] * 2, out_shape=[_sds((G, S, Fc), CDT)] * 2, compiler_params=_cparams(2),
    )(dzb, wd3, g3, u3)


def _axpy(a, b, alpha, name):
    S, D = a.shape
    ts = min(256, S)

    def body(a_ref, b_ref, o_ref):
        o_ref[...] = a_ref[...] + alpha * b_ref[...]

    return pl.pallas_call(
        body, name=name, grid=(S // ts,), in_specs=[_row(ts, D)] * 2, out_specs=_row(ts, D),
        out_shape=_sds((S, D), F32), compiler_params=_cparams(1),
    )(a, b)


def _pair_masks():
    lane = lax.broadcasted_iota(jnp.int32, (1, LANES), 1)
    first = lane < HEAD_DIM
    return first, jnp.logical_not(first)


def _head_scalar(x, m):
    return jnp.max(jnp.where(m, x, -jnp.inf), axis=-1, keepdims=True)


_NT = (((1,), (1,)), ((), ()))
_TN = (((0,), (0,)), ((), ()))
ATTN_KEY_CHUNK = 1024


def _attn_fwd(q, k, v, *, split, npairs, kblk, vblk, name):
    S = q.shape[0]
    qw = 256 if split else LANES
    tq = min(256, S)

    def body(q_ref, k_ref, v_ref, o_ref, lse_ref):
        masks = _pair_masks()
        outs, lses = [], []
        for hd in range(2):
            if split:
                qh = q_ref[:, hd * LANES:(hd + 1) * LANES]
                kh = k_ref[:, hd * LANES:(hd + 1) * LANES]
            else:
                qh = jnp.where(masks[hd], q_ref[...], jnp.zeros_like(q_ref[...]))
                kh = k_ref[...]
            s = lax.dot_general(qh, kh, _NT, preferred_element_type=F32)
            mx = jnp.max(s, axis=-1, keepdims=True)
            p = jnp.exp(s - mx)
            l = jnp.sum(p, axis=-1, keepdims=True)
            o = jnp.dot(p.astype(CDT), v_ref[...], preferred_element_type=F32)
            outs.append(o / l)
            lses.append(jnp.broadcast_to(mx + jnp.log(l), (tq, LANES)))
        o_ref[...] = jnp.where(masks[0], outs[0], outs[1]).astype(o_ref.dtype)
        lse_ref[...] = jnp.where(masks[0], lses[0], lses[1])

    return pl.pallas_call(
        body, name=name, grid=(npairs, S // tq),
        in_specs=[pl.BlockSpec((tq, qw), lambda p, i: (i, p)),
                  pl.BlockSpec((S, qw), lambda p, i: (0, kblk(p))),
                  pl.BlockSpec((S, LANES), lambda p, i: (0, vblk(p)))],
        out_specs=[pl.BlockSpec((tq, LANES), lambda p, i: (i, p))] * 2,
        out_shape=[_sds((S, LANES * npairs), CDT), _sds((S, LANES * npairs), F32)],
        compiler_params=_cparams(2),
    )(q, k, v)


def _attn_bwd(q, k, v, do, o, lse, *, split, npairs, kblk, vblk, doblk, shared_kv, name):
    S = q.shape[0]
    qw = 256 if split else LANES
    tq = min(512, S)
    tkv = min(ATTN_KEY_CHUNK, S)
    nkv = 1 if shared_kv else npairs

    def body(q_ref, k_ref, v_ref, do_ref, o_ref, lse_ref, dq_ref, dk_ref, dv_ref):
        masks = _pair_masks()
        p_id, i_id = pl.program_id(0), pl.program_id(1)
        first = (i_id == 0) & ((p_id == 0) if shared_kv else True)
        @pl.when(first)
        def _():
            dk_ref[...] = jnp.zeros_like(dk_ref)
            dv_ref[...] = jnp.zeros_like(dv_ref)

        do = do_ref[...]
        o = o_ref[...].astype(F32)
        lse = lse_ref[...]
        heads = []
        for hd in range(2):
            m = masks[hd]
            cols = slice(hd * LANES, (hd + 1) * LANES) if split else slice(None)
            qh = q_ref[:, cols] if split else jnp.where(m, q_ref[...], jnp.zeros_like(q_ref[...]))
            doh = jnp.where(m, do, 0.0)
            heads.append((m, cols, qh, doh.astype(CDT), _head_scalar(lse, m), jnp.sum(doh * o, axis=-1, keepdims=True)))
        dqs = [jnp.zeros((tq, LANES), F32), jnp.zeros((tq, LANES), F32)]
        for ck in range(S // tkv):
            rows = slice(ck * tkv, (ck + 1) * tkv)
            v = v_ref[rows, :]
            dv = jnp.zeros((tkv, LANES), F32)
            for hd, (m, cols, qh, dohb, lse_h, delta) in enumerate(heads):
                kh = k_ref[rows, cols]
                s = lax.dot_general(qh, kh, _NT, preferred_element_type=F32)
                p = jnp.exp(s - lse_h)
                dp = lax.dot_general(dohb, v, _NT, preferred_element_type=F32)
                ds = (p * (dp - delta)).astype(CDT)
                dq = jnp.dot(ds, kh, preferred_element_type=F32)
                dqs[hd] = dqs[hd] + (dq if split else jnp.where(m, dq, 0.0))
                dk_ref[rows, cols] += lax.dot_general(ds, qh, _TN, preferred_element_type=F32)
                dv = dv + lax.dot_general(p.astype(CDT), dohb, _TN, preferred_element_type=F32)
            dv_ref[rows, :] += dv
        if split:
            dq_ref[:, 0:LANES] = dqs[0]
            dq_ref[:, LANES:2 * LANES] = dqs[1]
        else:
            dq_ref[...] = dqs[0] + dqs[1]

    kvo = (lambda p, i: (0, 0)) if shared_kv else (lambda p, i: (0, p))
    return pl.pallas_call(
        body, name=name, grid=(npairs, S // tq),
        in_specs=[pl.BlockSpec((tq, qw), lambda p, i: (i, p)),
                  pl.BlockSpec((S, qw), lambda p, i: (0, kblk(p))),
                  pl.BlockSpec((S, LANES), lambda p, i: (0, vblk(p))),
                  pl.BlockSpec((tq, LANES), lambda p, i: (i, doblk(p))),
                  pl.BlockSpec((tq, LANES), lambda p, i: (i, p)),
                  pl.BlockSpec((tq, LANES), lambda p, i: (i, p))],
        out_specs=[pl.BlockSpec((tq, qw), lambda p, i: (i, p)),
                   pl.BlockSpec((S, qw), kvo), pl.BlockSpec((S, LANES), kvo)],
        out_shape=[_sds((S, qw * npairs), F32), _sds((S, qw * nkv), F32), _sds((S, LANES * nkv), F32)],
        compiler_params=_cparams(2),
    )(q, k, v, do, o, lse)


def _bias_expand(idx, rel_bias, name):
    tq, kw = idx.shape

    def body(idx_ref, rb_ref, o_ref):
        idx = idx_ref[...]
        for hd in range(DIL_HEADS):
            acc = jnp.full((tq, kw), NEG_INF, F32)
            for u in range(REL_BUCKETS):
                acc = jnp.where(idx == u, rb_ref[u, hd], acc)
            o_ref[hd] = acc

    return pl.pallas_call(
        body, name=name,
        in_specs=[pl.BlockSpec(memory_space=pltpu.VMEM), pl.BlockSpec(memory_space=pltpu.SMEM)],
        out_specs=pl.BlockSpec(memory_space=pltpu.VMEM),
        out_shape=_sds((DIL_HEADS, tq, kw), F32),
    )(idx, rel_bias)


def _bias_reduce(idx, dtab, name):
    tq, kw = idx.shape

    def body(idx_ref, d_ref, o_ref):
        idx = idx_ref[...]
        rowid = lax.broadcasted_iota(jnp.int32, (REL_BUCKETS, kw), 0)
        for hd in range(DIL_HEADS):
            d = d_ref[hd]
            acc = jnp.zeros((REL_BUCKETS, kw), F32)
            for u in range(REL_BUCKETS):
                r = jnp.sum(jnp.where(idx == u, d, 0.0), axis=0, keepdims=True)
                acc = jnp.where(rowid == u, r, acc)
            o_ref[hd] = jnp.sum(acc, axis=1, keepdims=True)

    return pl.pallas_call(
        body, name=name,
        in_specs=[pl.BlockSpec(memory_space=pltpu.VMEM)] * 2, out_specs=pl.BlockSpec(memory_space=pltpu.VMEM),
        out_shape=_sds((DIL_HEADS, REL_BUCKETS, 1), F32),
    )(idx, dtab)


def _dil_window(i, tq, kw, L):
    start = pl.multiple_of(i * tq, DIL_HALF)
    key = start + lax.broadcasted_iota(jnp.int32, (1, kw), 1) - DIL_HALF
    return start, (key >= 0) & (key < L)


def _dil_fwd(qv, kv, vv, tab, *, dil, L, tq, name):
    kw = tq + 2 * DIL_HALF
    npair = DIL_HEADS // 2

    def body(q_ref, k_ref, v_ref, t_ref, o_ref, lse_ref):
        masks = _pair_masks()
        start, valid = _dil_window(pl.program_id(2), tq, kw, L)
        kwin = k_ref[pl.ds(start, kw), :]
        vwin = v_ref[pl.ds(start, kw), :]
        outs, lses = [], []
        for hd in range(2):
            qh = jnp.where(masks[hd], q_ref[...], jnp.zeros_like(q_ref[...]))
            s = lax.dot_general(qh, kwin, _NT, preferred_element_type=F32) + t_ref[hd]
            s = jnp.where(valid, s, NEG_INF)
            mx = jnp.max(s, axis=-1, keepdims=True)
            p = jnp.exp(s - mx)
            l = jnp.sum(p, axis=-1, keepdims=True)
            outs.append(jnp.dot(p.astype(CDT), vwin, preferred_element_type=F32) / l)
            lses.append(jnp.broadcast_to(mx + jnp.log(l), (tq, LANES)))
        o_ref[...] = jnp.where(masks[0], outs[0], outs[1])
        lse_ref[...] = jnp.where(masks[0], lses[0], lses[1])

    blk = pl.BlockSpec((None, tq, LANES), lambda p, c, i: (c, i, p))
    res = pl.BlockSpec((None, L + 2 * DIL_HALF, LANES), lambda p, c, i: (c, 0, p))
    return pl.pallas_call(
        body, name=name, grid=(npair, dil, L // tq),
        in_specs=[blk, res, res, pl.BlockSpec((2, tq, kw), lambda p, c, i: (p, 0, 0))],
        out_specs=[blk] * 2, out_shape=[_sds(qv.shape, F32)] * 2,
        compiler_params=_cparams(3),
    )(qv, kv, vv, tab)


def _dil_bwd(qv, kv, vv, tab, dov, lsev, deltav, *, dil, L, tq, name):
    kw = tq + 2 * DIL_HALF
    npair = DIL_HEADS // 2

    def body(q_ref, k_ref, v_ref, t_ref, do_ref, lse_ref, dl_ref, dq_ref, dk_ref, dv_ref, dt_ref):
        masks = _pair_masks()
        c_id, i_id = pl.program_id(1), pl.program_id(2)
        start, valid = _dil_window(i_id, tq, kw, L)
        kwin = k_ref[pl.ds(start, kw), :]
        vwin = v_ref[pl.ds(start, kw), :]

        @pl.when(i_id == 0)
        def _():
            dk_ref[...] = jnp.zeros_like(dk_ref)
            dv_ref[...] = jnp.zeros_like(dv_ref)

        @pl.when((i_id == 0) & (c_id == 0))
        def _():
            dt_ref[...] = jnp.zeros_like(dt_ref)

        do = do_ref[...]
        dq = jnp.zeros((tq, LANES), F32)
        dk = jnp.zeros((kw, LANES), F32)
        dv = jnp.zeros((kw, LANES), F32)
        for hd in range(2):
            m = masks[hd]
            qh = jnp.where(m, q_ref[...], jnp.zeros_like(q_ref[...]))
            doh = jnp.where(m, do, jnp.zeros_like(do))
            s = lax.dot_general(qh, kwin, _NT, preferred_element_type=F32) + t_ref[hd]
            s = jnp.where(valid, s, NEG_INF)
            p = jnp.exp(s - _head_scalar(lse_ref[...], m))
            dp = lax.dot_general(doh, vwin, _NT, preferred_element_type=F32)
            ds = p * (dp - _head_scalar(dl_ref[...], m))
            dt_ref[hd] += ds
            dsb = ds.astype(CDT)
            dq = dq + jnp.where(m, jnp.dot(dsb, kwin, preferred_element_type=F32), 0.0)
            dk = dk + lax.dot_general(dsb, qh, _TN, preferred_element_type=F32)
            dv = dv + lax.dot_general(p.astype(CDT), doh, _TN, preferred_element_type=F32)
        dq_ref[...] = dq
        dk_ref[pl.ds(start, kw), :] += dk
        dv_ref[pl.ds(start, kw), :] += dv

    blk = pl.BlockSpec((None, tq, LANES), lambda p, c, i: (c, i, p))
    res = pl.BlockSpec((None, L + 2 * DIL_HALF, LANES), lambda p, c, i: (c, 0, p))
    tsp = pl.BlockSpec((2, tq, kw), lambda p, c, i: (p, 0, 0))
    return pl.pallas_call(
        body, name=name, grid=(npair, dil, L // tq),
        in_specs=[blk, res, res, tsp, blk, blk, blk], out_specs=[blk, res, res, tsp],
        out_shape=[_sds(qv.shape, F32), _sds(kv.shape, F32), _sds(kv.shape, F32), _sds(tab.shape, F32)],
        compiler_params=_cparams(3),
    )(qv, kv, vv, tab, dov, lsev, deltav)


def _mix_weights(l1, l2, l3):
    mx = jnp.maximum(jnp.maximum(l1, l2), l3)
    e1, e2, e3 = jnp.exp(l1 - mx), jnp.exp(l2 - mx), jnp.exp(l3 - mx)
    inv = 1.0 / (e1 + e2 + e3)
    return e1 * inv, e2 * inv, e3 * inv


def _branch_specs(S, ts):
    dils = [d for _, d in DIL_BRANCHES]
    return dils, [_res_spec(d, ts // d) for d in dils], [(d, S // d, DIL_W) for d in dils]


def _dil_mix_fwd(os, ls):
    S = os[0].shape[0] * os[0].shape[1]
    ts = min(256, S)
    dils, specs, _ = _branch_specs(S, ts)

    def body(o1, o2, o3, l1, l2, l3, out, scr):
        o1, o2, o3, l1, l2, l3 = [_by_token(r, scr, d) for r, d in zip((o1, o2, o3, l1, l2, l3), dils + dils)]
        w1, w2, w3 = _mix_weights(l1, l2, l3)
        out[...] = (w1 * o1 + w2 * o2 + w3 * o3).astype(CDT)

    return pl.pallas_call(
        body, name="dil_mix_fwd", grid=(S // ts,), in_specs=specs + specs, out_specs=_row(ts, DIL_W),
        out_shape=_sds((S, DIL_W), CDT), scratch_shapes=[_TOKEN_SCRATCH(ts)],
        compiler_params=_cparams(1),
    )(*os, *ls)


def _dil_mix_bwd(dcat, os, ls, j384):
    S = os[0].shape[0] * os[0].shape[1]
    ts = min(256, S)
    dils, specs, shapes = _branch_specs(S, ts)

    def body(do_ref, o1, o2, o3, l1, l2, l3, j_ref, d1, d2, d3, e1, e2, e3, scr):
        o1, o2, o3, l1, l2, l3 = [_by_token(r, scr, d) for r, d in zip((o1, o2, o3, l1, l2, l3), dils + dils)]
        ws = _mix_weights(l1, l2, l3)
        do = do_ref[...]
        o = ws[0] * o1 + ws[1] * o2 + ws[2] * o3
        dot = _headsum(do * o, j_ref[...])
        for w, d, d_o, e_o in zip(ws, dils, (d1, d2, d3), (e1, e2, e3)):
            _by_residue(w * do, scr, d_o, d)
            _by_residue(w * dot, scr, e_o, d)

    return pl.pallas_call(
        body, name="dil_mix_bwd", grid=(S // ts,),
        in_specs=[_row(ts, DIL_W, 1)] + specs + specs + [_full(j384.shape)],
        out_specs=specs + specs,
        out_shape=[_sds(s, CDT) for s in shapes] + [_sds(s, F32) for s in shapes],
        scratch_shapes=[_TOKEN_SCRATCH(ts)],
        compiler_params=_cparams(1),
    )(dcat, *os, *ls, j384)


def _adamw_math(w, g, m, v):
    m = ADAM_B1 * m + (1.0 - ADAM_B1) * g
    v = ADAM_B2 * v + (1.0 - ADAM_B2) * (g * g)
    m_hat = m / (1.0 - ADAM_B1 ** ADAM_STEP)
    v_hat = v / (1.0 - ADAM_B2 ** ADAM_STEP)
    delta = -ADAM_LR * (m_hat / (jnp.sqrt(v_hat) + ADAM_EPS) + ADAM_WD * w)
    return delta, m, v


def _pick8(n, target):
    best = None
    for t in range(16, min(n, target) + 1, 16):
        if n % t == 0:
            best = t
    return best if best is not None else n


_ELEMS_PER_BLOCK = 256 * 1024


def _adamw(w, reds, sibs, m, v, owner, name):
    L, a, b = w.shape
    ta = _pick8(a, max(16, _ELEMS_PER_BLOCK // b))
    spec = pl.BlockSpec((None, ta, b), lambda l, i, own: (l, i, 0))
    def gspec(mine, layer):
        def index(l, i, own):
            use = (own[0] if mine else 1 - own[0]) * (l if layer else 1 - l)
            return i * use, 0
        return pl.BlockSpec((ta, b), index)

    def body(own_ref, w_ref, r0_ref, r1_ref, s0_ref, s1_ref, m_ref, v_ref, g_o, d_o, m_o, v_o):
        mine = own_ref[0] == 1
        g0 = jnp.where(mine, r0_ref[...], s0_ref[...])
        g1 = jnp.where(mine, r1_ref[...], s1_ref[...])
        g = jnp.where(pl.program_id(0) == 0, g0, g1)
        d, mm, vv = _adamw_math(w_ref[...], g, m_ref[...], v_ref[...])
        g_o[...] = g
        d_o[...] = d
        m_o[...] = mm
        v_o[...] = vv

    return pl.pallas_call(
        body, name=name, out_shape=[_sds(w.shape, F32)] * 4,
        grid_spec=pltpu.PrefetchScalarGridSpec(
            num_scalar_prefetch=1, grid=(L, a // ta),
            in_specs=[spec, gspec(True, 0), gspec(True, 1), gspec(False, 0), gspec(False, 1), spec, spec],
            out_specs=[spec] * 4),
        compiler_params=_cparams(2),
    )(_is_core(owner), w, *reds, *sibs, m, v)


def _adamw_small(w, gall, m, v):
    R = w.shape[0]

    def body(w_ref, g_ref, m_ref, v_ref, g_o, d_o, m_o, v_o):
        g = g_ref[0]
        for k in range(1, 8):
            g = g + g_ref[k]
        d, mm, vv = _adamw_math(w_ref[...], g, m_ref[...], v_ref[...])
        g_o[...] = g
        d_o[...] = d
        m_o[...] = mm
        v_o[...] = vv

    vm = pl.BlockSpec(memory_space=pltpu.VMEM)
    return pl.pallas_call(
        body, name="adamw_small", in_specs=[vm] * 4, out_specs=[vm] * 4, out_shape=[_sds((R, LANES), F32)] * 4,
    )(w, gall, m, v)


def _sum_pair(g, t, owner, name):
    n, a, b = t.shape
    ta = _pick8(a, max(16, _ELEMS_PER_BLOCK // b))
    spec = pl.BlockSpec((None, ta, b), lambda k, i, own: (k * own[0], i * own[0], 0))

    def body(own_ref, g_ref, t_ref, o_ref):
        @pl.when(own_ref[0] == 1)
        def _():
            o_ref[...] = (g_ref[...].astype(F32) + t_ref[...].astype(F32)).astype(o_ref.dtype)

    return pl.pallas_call(
        body, name=name, out_shape=_sds(t.shape, WIRE),
        grid_spec=pltpu.PrefetchScalarGridSpec(num_scalar_prefetch=1, grid=(n, a // ta), in_specs=[spec] * 2,
                                               out_specs=spec),
        compiler_params=_cparams(2),
    )(_is_core(owner), g, t)


def _sum_chips(pair, t, owner, name):
    _, a, b = t.shape
    ta = _pick8(a, max(16, _ELEMS_PER_BLOCK // b))

    def body(own_ref, p_ref, t_ref, o_ref):
        @pl.when(own_ref[0] == 1)
        def _():
            me = 2 * lax.axis_index("x") + lax.axis_index("y")
            acc = p_ref[me].astype(F32)
            for k in range(3):
                acc = acc + t_ref[k].astype(F32)
            o_ref[...] = acc

    return pl.pallas_call(
        body, name=name, out_shape=_sds((a, b), F32),
        grid_spec=pltpu.PrefetchScalarGridSpec(
            num_scalar_prefetch=1, grid=(a // ta,),
            in_specs=[pl.BlockSpec((4, ta, b), lambda i, own: (0, i * own[0], 0)),
                      pl.BlockSpec((3, ta, b), lambda i, own: (0, i * own[0], 0))],
            out_specs=pl.BlockSpec((ta, b), lambda i, own: (i * own[0], 0))),
        compiler_params=_cparams(1),
    )(_is_core(owner), pair, t)


def _is_core(core):
    return (lax.axis_index("c") == core).astype(jnp.int32).reshape(1)


_HBM = pl.BlockSpec(memory_space=pltpu.HBM)


def _place():
    x, y, c = lax.axis_index("x"), lax.axis_index("y"), lax.axis_index("c")
    chips = [(1 - x, y), (x, 1 - y), (1 - x, 1 - y)]
    return x, y, c, chips


def _remote(src, dst, ssem, rsem, to):
    return pltpu.make_async_remote_copy(src_ref=src, dst_ref=dst, send_sem=ssem, recv_sem=rsem, device_id=to,
                                        device_id_type=MESH_ID)


def _dma_sems(n):
    return pltpu.SemaphoreType.DMA((n,))


_SEM = pl.BlockSpec(memory_space=pltpu.SEMAPHORE)
_ANY = pl.BlockSpec(memory_space=pl.ANY)
_EFFECT = pltpu.SideEffectType.DATAFLOW_SIDE_EFFECTING
_BIG = ("w_in", "mla_w_uq", "mla_w_ukv", "w_out", "ffn_w_gate", "ffn_w_up", "ffn_w_down")
_OWNER = dict(zip(_BIG, (1, 0, 0, 1, 0, 0, 1)))
_ATTN_WEIGHTS, _FFN_WEIGHTS = _BIG[:4], _BIG[4:]


def _hbm(a):
    return pltpu.with_memory_space_constraint(a, pltpu.HBM)


def _per_core(c, owners, fn):
    for g in range(2):
        mine = tuple(p for p, o in enumerate(owners) if o == g)
        theirs = tuple(p for p, o in enumerate(owners) if o != g)
        pl.when(c == g)(functools.partial(fn, mine, theirs))


def _token_spec():
    return pl.BlockSpec(memory_space=pltpu.VMEM), _sds((8, LANES), F32)


def _gather_start(shards, owners, layer, tag):
    n = len(shards)
    lands = [_hbm(lax.empty((4,) + s.shape[1:], s.dtype)) for s in shards]

    def body(*refs):
        w_refs, l_refs = refs[:n], refs[n:2 * n]
        ssem, rsem, token = refs[2 * n], refs[2 * n + 1], refs[-1]
        x, y, c, chips = _place()
        me = 2 * x + y

        def send(mine, _):
            for i in mine:
                for j, (cx, cy) in enumerate(chips):
                    _remote(w_refs[i].at[layer], l_refs[i].at[me], ssem.at[3 * i + j], rsem.at[3 * i + j],
                            (cx, cy, c)).start()

        _per_core(c, owners, send)
        token[...] = jnp.zeros_like(token)

    tspec, tshape = _token_spec()
    out = pl.pallas_call(
        body, name=f"gather_start_{tag}", in_specs=[_HBM] * (2 * n),
        out_specs=[_SEM, _SEM] + [_HBM] * n + [tspec],
        out_shape=[_dma_sems(3 * n), _dma_sems(3 * n)] + [pltpu.HBM(l.shape, l.dtype) for l in lands] + [tshape],
        input_output_aliases={n + i: 2 + i for i in range(n)},
        compiler_params=pltpu.CompilerParams(has_side_effects=_EFFECT),
    )(*[_hbm(s) for s in shards], *lands)
    return out[0], out[1], list(out[2:2 + n]), out[-1]


def _gather_wait(ssem, rsem, shards, lands, after, owners, layer, tag):
    n = len(shards)

    def body(*refs):
        w_refs, l_refs = refs[:n], refs[n:2 * n]
        ssem, rsem = refs[2 * n], refs[2 * n + 1]
        x, y, c, chips = _place()

        def wait(mine, _):
            for i in mine:
                for j, (cx, cy) in enumerate(chips):
                    cp = _remote(w_refs[i].at[layer], l_refs[i].at[2 * cx + cy], ssem.at[3 * i + j], rsem.at[3 * i + j],
                                 (cx, cy, c))
                    cp.wait_send()
                    cp.wait_recv()

        _per_core(c, owners, wait)

    return list(pl.pallas_call(
        body, name=f"gather_wait_{tag}", in_specs=[_HBM] * (2 * n) + [_SEM, _SEM, _ANY], out_specs=[_HBM] * n,
        out_shape=[pltpu.HBM(l.shape, l.dtype) for l in lands],
        input_output_aliases={n + i: i for i in range(n)},
        compiler_params=pltpu.CompilerParams(has_side_effects=_EFFECT),
    )(*[_hbm(s) for s in shards], *lands, ssem, rsem, after))


def _gather_finish(shards, lands, owners, layer, tag):
    n = len(shards)

    def body(*refs):
        w_refs, g_refs = refs[:n], refs[2 * n:3 * n]
        ssem, rsem = refs[3 * n:]
        x, y, c, chips = _place()
        me = 2 * x + y
        sib = (x, y, 1 - c)
        owns = [_remote(w.at[layer], g.at[me], ssem.at[i], rsem.at[i], sib) for i, (w, g) in enumerate(zip(w_refs, g_refs))]
        for cp in owns:
            cp.start()

        def forward(mine, theirs):
            def blk(i, j):
                b = g_refs[i].at[2 * chips[j][0] + chips[j][1]]
                return _remote(b, b, ssem.at[n + 3 * i + j], rsem.at[n + 3 * i + j], sib)

            for i in mine:
                for j in range(3):
                    blk(i, j).start()
            for i in theirs:
                for j in range(3):
                    blk(i, j).wait_recv()
            for i in mine:
                for j in range(3):
                    blk(i, j).wait_send()

        _per_core(c, owners, forward)
        for cp in owns:
            cp.wait_recv()
            cp.wait_send()

    return list(pl.pallas_call(
        body, name=f"gather_finish_{tag}", in_specs=[_HBM] * (2 * n), out_specs=[_HBM] * n,
        out_shape=[_sds(l.shape, l.dtype) for l in lands], input_output_aliases={n + i: i for i in range(n)},
        scratch_shapes=[_dma_sems(4 * n), _dma_sems(4 * n)],
    )(*shards, *lands))


def _rs_to_owner(grads, owners, tag):
    n = len(grads)

    def body(*refs):
        g_refs, t_refs = refs[:n], refs[n:2 * n]
        ssem, rsem = refs[2 * n:]
        x, y, c, _ = _place()

        def swap(mine, theirs):
            cps = [_remote(g_refs[i], t_refs[i], ssem.at[i], rsem.at[i], (x, y, 1 - c)) for i in theirs]
            for cp in cps:
                cp.start()
            for i in mine:
                _remote(g_refs[i], t_refs[i], ssem.at[i], rsem.at[i], (x, y, 1 - c)).wait_recv()
            for cp in cps:
                cp.wait_send()

        _per_core(c, owners, swap)

    return list(pl.pallas_call(
        body, name=f"rs_to_owner_{tag}", in_specs=[_HBM] * n, out_specs=[_HBM] * n,
        out_shape=[_sds(g.shape, g.dtype) for g in grads], scratch_shapes=[_dma_sems(n), _dma_sems(n)],
    )(*grads))


def _a2a_start(pairs, owners, tag):
    n = len(pairs)
    lands = [_hbm(lax.empty((3,) + p.shape[1:], p.dtype)) for p in pairs]

    def body(*refs):
        a_refs, t_refs = refs[:n], refs[n:2 * n]
        ssem, rsem, token = refs[2 * n], refs[2 * n + 1], refs[-1]
        x, y, c, chips = _place()

        def send(mine, _):
            for i in mine:
                for j, (cx, cy) in enumerate(chips):
                    _remote(a_refs[i].at[2 * cx + cy], t_refs[i].at[j], ssem.at[3 * i + j], rsem.at[3 * i + j],
                            (cx, cy, c)).start()

        _per_core(c, owners, send)
        token[...] = jnp.zeros_like(token)

    tspec, tshape = _token_spec()
    out = pl.pallas_call(
        body, name=f"rs_a2a_start_{tag}", in_specs=[_HBM] * (2 * n),
        out_specs=[_SEM, _SEM] + [_HBM] * n + [tspec],
        out_shape=[_dma_sems(3 * n), _dma_sems(3 * n)] + [pltpu.HBM(l.shape, l.dtype) for l in lands] + [tshape],
        input_output_aliases={n + i: 2 + i for i in range(n)},
        compiler_params=pltpu.CompilerParams(has_side_effects=_EFFECT),
    )(*[_hbm(p) for p in pairs], *lands)
    return out[0], out[1], list(out[2:2 + n]), out[-1]


def _a2a_wait(ssem, rsem, pairs, lands, after, owners, tag):
    n = len(pairs)

    def body(*refs):
        a_refs, t_refs = refs[:n], refs[n:2 * n]
        ssem, rsem = refs[2 * n], refs[2 * n + 1]
        x, y, c, chips = _place()

        def wait(mine, _):
            for i in mine:
                for j, (cx, cy) in enumerate(chips):
                    cp = _remote(a_refs[i].at[2 * cx + cy], t_refs[i].at[j], ssem.at[3 * i + j], rsem.at[3 * i + j],
                                 (cx, cy, c))
                    cp.wait_send()
                    cp.wait_recv()

        _per_core(c, owners, wait)

    return list(pl.pallas_call(
        body, name=f"rs_a2a_wait_{tag}", in_specs=[_HBM] * (2 * n) + [_SEM, _SEM, _ANY], out_specs=[_HBM] * n,
        out_shape=[pltpu.HBM(l.shape, l.dtype) for l in lands],
        input_output_aliases={n + i: i for i in range(n)},
        compiler_params=pltpu.CompilerParams(has_side_effects=_EFFECT),
    )(*[_hbm(p) for p in pairs], *lands, ssem, rsem, after))


def _rs_from_owner(reds, owners):
    n = len(reds)

    def body(*refs):
        q_refs, o_refs = refs[:n], refs[n:2 * n]
        ssem, rsem = refs[2 * n:]
        x, y, c, _ = _place()

        def swap(mine, theirs):
            cps = [_remote(q_refs[k], o_refs[k], ssem.at[k], rsem.at[k], (x, y, 1 - c)) for k in mine]
            for cp in cps:
                cp.start()
            for k in theirs:
                _remote(q_refs[k], o_refs[k], ssem.at[k], rsem.at[k], (x, y, 1 - c)).wait_recv()
            for cp in cps:
                cp.wait_send()

        _per_core(c, owners, swap)

    return list(pl.pallas_call(
        body, name="rs_from_owner", in_specs=[_HBM] * n, out_specs=[_HBM] * n,
        out_shape=[_sds(q.shape, q.dtype) for q in reds], scratch_shapes=[_dma_sems(n), _dma_sems(n)],
    )(*reds))


def _gather_small(s):
    R, _ = s.shape

    def body(s_ref, o_ref, ssem, rsem, lsem):
        x, y, c, _ = _place()
        me = 4 * x + 2 * y + c
        own = pltpu.make_async_copy(s_ref, o_ref.at[me], lsem)
        own.start()
        sends = []
        for k in range(1, 8):
            px, py, pc = x ^ (k >> 2), y ^ ((k >> 1) & 1), c ^ (k & 1)
            cp = _remote(s_ref, o_ref.at[me], ssem.at[k - 1], rsem.at[k - 1], (px, py, pc))
            cp.start()
            sends.append(cp)
        for k in range(1, 8):
            px, py, pc = x ^ (k >> 2), y ^ ((k >> 1) & 1), c ^ (k & 1)
            blk = o_ref.at[4 * px + 2 * py + pc]
            _remote(blk, blk, ssem.at[k - 1], rsem.at[k - 1], (px, py, pc)).wait_recv()
        for cp in sends:
            cp.wait_send()
        own.wait()

    vm = pl.BlockSpec(memory_space=pltpu.VMEM)
    return pl.pallas_call(
        body, name="gather_small", in_specs=[vm], out_specs=vm, out_shape=_sds((8, R, LANES), s.dtype),
        scratch_shapes=[pltpu.SemaphoreType.DMA((7,)), pltpu.SemaphoreType.DMA((7,)), pltpu.SemaphoreType.DMA],
    )(s)


_COL_SHARDED =("w_in", "mla_w_uq", "mla_w_ukv", "ffn_w_gate", "ffn_w_up")
_SMALL = ("mla_q_norm", "mla_kv_norm", "gqa_q_norm", "gqa_k_norm", "rel_bias", "ln1_g", "ln1_b", "ln2_g", "ln2_b")


def _pack_flat(arrs, align):
    flat = jnp.concatenate([a.reshape(-1) for a in arrs])
    pad = (-flat.shape[0]) % align
    return jnp.pad(flat, (0, pad)) if pad else flat


def _unpack_flat(flat, shapes):
    out, off = [], 0
    for s in shapes:
        n = int(np.prod(s))
        out.append(flat[off:off + n].reshape(s))
        off += n
    return out


def _perm_gqa_rows(w):
    return jnp.concatenate([w[:832], w[896:960], w[832:896], w[960:]], axis=0)


def _local_step(x, target, small, depth, weights_of_layer, grads_done):
    S, D = x.shape
    alpha = (2.0 * depth) ** 0.25
    in_idx, uq_idx, ukv_idx = _in_cols(), _uq_cols(), _ukv_cols()
    win, wuq, wukv, wout, wg, wu, wdn = ([None] * depth for _ in range(7))

    tm, tg = _rope_tables(S)
    j256, j384 = _head_ones(256), _head_ones(384)
    mla_scale = (64 + MLA_ROPE_DIM) ** -0.5
    branches = []
    for (_, dil) in DIL_BRANCHES:
        L = S // dil
        tq = min(256, L)
        idx = jnp.asarray(_branch_bucket_idx(tq, dil))
        branches.append((dil, L, tq, idx))
    tabs = [_bias_expand(idx, small["rel_bias"], name=f"bias_expand_{b}") for b, (_, _, _, idx) in enumerate(branches)]

    def padded(a):
        z = jnp.zeros((DIL_HALF, a.shape[1]), a.dtype)
        return jnp.concatenate([z, a, z], axis=0)[None]

    saved = []
    xf, xb = x, x.astype(CDT)
    for l in range(depth):
        W, token = weights_of_layer(l, "attn", xb)
        win[l] = _rows_from_shards(W["w_in"], in_idx)
        wuq[l] = _rows_from_shards(W["mla_w_uq"], uq_idx)
        wukv[l] = _rows_from_shards(W["mla_w_ukv"], ukv_idx)
        wout[l] = _perm_gqa_rows(W["w_out"].reshape(-1, D))
        gq, gkv = small["mla_q_norm"][l][None], small["mla_kv_norm"][l][None]
        if token is not None:
            gq = gq + token[0, 0]
        ggq = jnp.tile(small["gqa_q_norm"][l], 4)[None]
        ggk = jnp.tile(small["gqa_k_norm"][l], 2)[None]
        h = _mm(xb, win[l], tb=True, name="mm_in")
        cq, ckv, kr, qd, kd, vd, qg, kg, vg, *strided = _prep_fwd(h, gq, gkv, ggq, ggk, tm, tg, j256)
        qkv = [(qd[None], padded(kd), padded(vd))] + [tuple(strided[3 * b:3 * b + 3]) for b in range(len(DIL_STRIDES))]
        qa = _mm(cq, wuq[l], tb=True, name="mm_uq")
        kvp = _mm(ckv, wukv[l], tb=True, out_dtype=CDT, name="mm_ukv")
        qm, km = _mla_prep_fwd(qa, kvp, kr, tm, mla_scale)
        oa, lsa = _attn_fwd(qm, km, kvp, split=True, npairs=3, kblk=lambda p: p, vblk=lambda p: 6 + p,
                            name="mla_attn_fwd")
        oc, lsc = _attn_fwd(qg, kg, vg, split=False, npairs=2, kblk=lambda p: 0, vblk=lambda p: 0,
                            name="gqa_attn_fwd")
        obs, lbs = [], []
        for b, (dil, L, tq, _) in enumerate(branches):
            o_b, l_b = _dil_fwd(*qkv[b], tabs[b], dil=dil, L=L, tq=tq, name=f"dil_fwd_{b}")
            obs.append(o_b)
            lbs.append(l_b)
        ob = _dil_mix_fwd(obs, lbs)
        cat = jnp.concatenate([oa, ob, oc], axis=1)
        mix = _mm(cat, wout[l], name="mm_out")
        x1, x1b, z1 = _ln_fwd(xf, mix, small["ln1_g"][l][None], small["ln1_b"][l][None], alpha, name="ln1_fwd")
        W, _ = weights_of_layer(l, "ffn", x1b)
        wg[l], wu[l], wdn[l] = W["ffn_w_gate"], W["ffn_w_up"], W["ffn_w_down"]
        g3, u3, act = _ffn_up(x1b, wg[l], wu[l])
        ff = _mm(act, wdn[l], ga=True, gb=True, name="mm_down")
        x2, x2b, z2 = _ln_fwd(x1, ff, small["ln2_g"][l][None], small["ln2_b"][l][None], alpha, name="ln2_fwd")
        saved.append(dict(xb=xb, h=h, cq=cq, ckv=ckv, qg=qg, kg=kg, vg=vg, kvp=kvp, qm=qm, km=km, oa=oa, lsa=lsa,
                          oc=oc, lsc=lsc, obs=obs, lbs=lbs, qkv=qkv, cat=cat, z1=z1, x1b=x1b, g3=g3, u3=u3, act=act, z2=z2,
                          gq=gq, gkv=gkv, ggq=ggq, ggk=ggk))
        xf, xb = x2, x2b

    gW = {k: [None] * depth for k in _BIG}
    gS = {k: [None] * depth for k in ("mla_q_norm", "mla_kv_norm", "gqa_q_norm", "gqa_k_norm", "ln1_g", "ln1_b", "ln2_g",
                                      "ln2_b")}
    g_rel = None
    dya, dyb = xf, target
    token = None
    for l in reversed(range(depth)):
        sv = saved[l]
        ln2_g = small["ln2_g"][l][None]
        if token is not None:
            ln2_g = ln2_g + token[0, 0]
        if l == depth - 1:
            dz2, dz2b, gS["ln2_g"][l], gS["ln2_b"][l], loss = _ln_bwd(dya, dyb, sv["z2"], ln2_g, alpha,
                                                                       name="ln2_bwd_loss", loss_head=True)
        else:
            dz2, dz2b, gS["ln2_g"][l], gS["ln2_b"][l] = _ln_bwd(dya, dyb, sv["z2"], ln2_g, alpha, name="ln2_bwd")
        gW["ffn_w_down"][l] = _mm(sv["act"], dz2b, ta=True, ga=True, go=True, out_dtype=WIRE, name="mm_down_dw")
        dg3, du3 = _ffn_down_dx(dz2b, wdn[l], sv["g3"], sv["u3"])
        gW["ffn_w_gate"][l] = _mm(dg3, sv["x1b"], ta=True, ga=True, go=True, out_dtype=WIRE, name="mm_gate_dw")
        gW["ffn_w_up"][l] = _mm(du3, sv["x1b"], ta=True, ga=True, go=True, out_dtype=WIRE, name="mm_up_dw")
        dx1 = _ffn_up_dx(dg3, du3, wg[l], wu[l])
        token = grads_done(l, "ffn", {n: gW[n][l] for n in _FFN_WEIGHTS})
        ln1_g = small["ln1_g"][l][None]
        if token is not None:
            ln1_g = ln1_g + token[0, 0]
        dz1, dz1b, gS["ln1_g"][l], gS["ln1_b"][l] = _ln_bwd(dx1, dz2, sv["z1"], ln1_g, alpha, name="ln1_bwd")
        gW["w_out"][l] = _perm_gqa_rows(_mm(sv["cat"], dz1b, ta=True, out_dtype=WIRE, name="mm_out_dw")).reshape(4, -1, D)
        dcat = _mm(dz1b, wout[l], tb=True, name="mm_out_dx")
        dqg, dkg, dvg = _attn_bwd(sv["qg"], sv["kg"], sv["vg"], dcat, sv["oc"], sv["lsc"], split=False, npairs=2,
                                  kblk=lambda p: 0, vblk=lambda p: 0, doblk=lambda p: 6 + p, shared_kv=True,
                                  name="gqa_attn_bwd")
        dqm, dkm, dvm = _attn_bwd(sv["qm"], sv["km"], sv["kvp"], dcat, sv["oa"], sv["lsa"], split=True, npairs=3,
                                  kblk=lambda p: p, vblk=lambda p: 6 + p, doblk=lambda p: p, shared_kv=False,
                                  name="mla_attn_bwd")
        dqa, dkvp, dkr = _mla_prep_bwd(dqm, dkm, dvm, tm, mla_scale)
        gW["mla_w_uq"][l] = _rows_to_shards(_mm(dqa, sv["cq"], ta=True, out_dtype=WIRE, name="mm_uq_dw"), uq_idx, MLA_HEADS * 96)
        dcq = _mm(dqa, wuq[l], name="mm_uq_dx")
        gW["mla_w_ukv"][l] = _rows_to_shards(_mm(dkvp, sv["ckv"], ta=True, out_dtype=WIRE, name="mm_ukv_dw"), ukv_idx, MLA_HEADS * 128)
        dckv = _mm(dkvp, wukv[l], name="mm_ukv_dx")
        mixb = _dil_mix_bwd(dcat, sv["obs"], sv["lbs"], j384)
        ddq, ddk, ddv = [], [], []
        for b, (dil, L, tq, idx) in enumerate(branches):
            dq_b, dk_b, dv_b, dtab = _dil_bwd(*sv["qkv"][b], tabs[b], mixb[b], sv["lbs"][b], mixb[3 + b], dil=dil, L=L,
                                              tq=tq, name=f"dil_bwd_{b}")
            if dil == 1:
                dq_b, dk_b, dv_b = dq_b[0], dk_b[0, DIL_HALF:DIL_HALF + S], dv_b[0, DIL_HALF:DIL_HALF + S]
            ddq.append(dq_b)
            ddk.append(dk_b)
            ddv.append(dv_b)
            g_b = _bias_reduce(idx, dtab, name=f"bias_reduce_{b}")[:, :, 0].T
            g_rel = g_b if g_rel is None else g_rel + g_b
        dh, n1, n2, n3, n4 = _prep_bwd(sv["h"], dcq, dckv, dkr, ddq, ddk, ddv, dqg, dkg, dvg, sv["gq"], sv["gkv"],
                                       sv["ggq"], sv["ggk"], tg, j256)
        gS["mla_q_norm"][l], gS["mla_kv_norm"][l] = n1[0], n2[0]
        gS["gqa_q_norm"][l] = n3[0].reshape(4, 64).sum(0)
        gS["gqa_k_norm"][l] = n4[0].reshape(2, 64).sum(0)
        gW["w_in"][l] = _rows_to_shards(_mm(dh, sv["xb"], ta=True, out_dtype=WIRE, name="mm_in_dw"), in_idx, IN_W)
        dya = _mm(dh, win[l], name="mm_in_dx")
        dyb = dz1
        token = grads_done(l, "attn", {n: gW[n][l] for n in _ATTN_WEIGHTS})
    grad_x = _axpy(dya, dyb, alpha, name="grad_x")

    gsmall = {k: jnp.stack([a.reshape(-1) for a in v]) for k, v in gS.items()}
    gsmall["rel_bias"] = g_rel
    return loss, grad_x, gsmall


_ORDER = ("w_in", "mla_q_norm", "mla_kv_norm", "mla_w_uq", "mla_w_ukv", "gqa_q_norm", "gqa_k_norm", "rel_bias", "w_out",
          "ln1_g", "ln1_b", "ffn_w_gate", "ffn_w_up", "ffn_w_down", "ln2_g", "ln2_b")


def kernel(x, w_in, mla_q_norm, mla_kv_norm, mla_w_uq, mla_w_ukv, gqa_q_norm, gqa_k_norm, rel_bias, w_out, ln1_g, ln1_b, ffn_w_gate, ffn_w_up, ffn_w_down, ln2_g, ln2_b, loss_target, m_w_in, m_mla_q_norm, m_mla_kv_norm, m_mla_w_uq, m_mla_w_ukv, m_gqa_q_norm, m_gqa_k_norm, m_rel_bias, m_w_out, m_ln1_g, m_ln1_b, m_ffn_w_gate, m_ffn_w_up, m_ffn_w_down, m_ln2_g, m_ln2_b, v_w_in, v_mla_q_norm, v_mla_kv_norm, v_mla_w_uq, v_mla_w_ukv, v_gqa_q_norm, v_gqa_k_norm, v_rel_bias, v_w_out, v_ln1_g, v_ln1_b, v_ffn_w_gate, v_ffn_w_up, v_ffn_w_down, v_ln2_g, v_ln2_b):
    wts = dict(zip(_ORDER, (w_in, mla_q_norm, mla_kv_norm, mla_w_uq, mla_w_ukv, gqa_q_norm, gqa_k_norm, rel_bias, w_out,
                            ln1_g, ln1_b, ffn_w_gate, ffn_w_up, ffn_w_down, ln2_g, ln2_b)))
    mom = dict(zip(_ORDER, (m_w_in, m_mla_q_norm, m_mla_kv_norm, m_mla_w_uq, m_mla_w_ukv, m_gqa_q_norm, m_gqa_k_norm,
                            m_rel_bias, m_w_out, m_ln1_g, m_ln1_b, m_ffn_w_gate, m_ffn_w_up, m_ffn_w_down, m_ln2_g,
                            m_ln2_b)))
    var = dict(zip(_ORDER, (v_w_in, v_mla_q_norm, v_mla_kv_norm, v_mla_w_uq, v_mla_w_ukv, v_gqa_q_norm, v_gqa_k_norm,
                            v_rel_bias, v_w_out, v_ln1_g, v_ln1_b, v_ffn_w_gate, v_ffn_w_up, v_ffn_w_down, v_ln2_g,
                            v_ln2_b)))
    small_shapes = [wts[n].shape for n in _SMALL]
    for d in (wts, mom, var):
        for n in _COL_SHARDED:
            d[n] = d[n].transpose(0, 2, 1)

    depth = 2
    shards = {n: wts[n].astype(WIRE) for n in _BIG}
    flying = {}

    def start_gather(names, l, tag):
        own = tuple(_OWNER[n] for n in names)
        sh = [shards[n] for n in names]
        ssem, rsem, lands, token = _gather_start(sh, own, l, tag)
        return (names, own, sh, ssem, rsem, lands, l, tag), token

    def end_gather(flight, after):
        names, own, sh, ssem, rsem, lands, l, tag = flight
        got = _gather_finish(sh, _gather_wait(ssem, rsem, sh, lands, after, own, l, tag), own, l, tag)
        return {n: g.astype(CDT) for n, g in zip(names, got)}

    def weights_of_layer(l, part, after):
        if (l, part) == (0, "attn"):
            got = end_gather(start_gather(_ATTN_WEIGHTS, 0, "attn0")[0], after)
            flying["ffn0"], t0 = start_gather(_FFN_WEIGHTS, 0, "ffn0")
            flying["layer1"], t1 = start_gather(_BIG, 1, "layer1")
            return got, t0 + t1
        if (l, part) == (0, "ffn"):
            return end_gather(flying.pop("ffn0"), after), None
        if part == "attn":
            flying["w1"] = end_gather(flying.pop("layer1"), after)
        return flying["w1"], None

    def grads_done(l, part, grads):
        names = tuple(grads)
        own = tuple(_OWNER[n] for n in names)
        tag = f"{part}{l}"
        gl = [grads[n] for n in names]
        theirs = _rs_to_owner(gl, own, tag)
        pairs = [_sum_pair(g, t, o, name=f"rs_pair_sum_{n}") for n, g, t, o in zip(names, gl, theirs, own)]
        ssem, rsem, lands, token = _a2a_start(pairs, own, tag)
        flying[tag] = (names, own, ssem, rsem, pairs, lands)
        return token

    small = {n: wts[n] for n in _SMALL}
    loss, grad_x, gsmall = _local_step(x[0], loss_target[0], small, depth, weights_of_layer, grads_done)

    reds = {}
    for l in reversed(range(depth)):
        for part in ("ffn", "attn"):
            tag = f"{part}{l}"
            names, own, ssem, rsem, pairs, lands = flying.pop(tag)
            got = _a2a_wait(ssem, rsem, pairs, lands, grad_x, own, tag)
            for n, p, t, o in zip(names, pairs, got, own):
                reds[n, l] = _sum_chips(p, t, o, name=f"rs_sum_chips_{n}")
    order = [(n, l) for l in range(depth) for n in _BIG]
    sibs = dict(zip(order, _rs_from_owner([reds[k] for k in order], tuple(_OWNER[n] for n, _ in order))))

    sflat = _pack_flat([gsmall[n].reshape(-1) for n in _SMALL], 8 * LANES)
    rs = sflat.shape[0] // LANES
    sall = _gather_small(sflat.reshape(rs, LANES))

    def packed(d):
        return _pack_flat([d[n] for n in _SMALL], 8 * LANES).reshape(rs, LANES)

    outs = {tag: {} for tag in ("grad", "delta", "new_m", "new_v")}
    for n in _BIG:
        res = _adamw(wts[n], [reds[n, 0], reds[n, 1]], [sibs[n, 0], sibs[n, 1]], mom[n], var[n], _OWNER[n],
                     name=f"adamw_{n}")
        for tag, r in zip(("grad", "delta", "new_m", "new_v"), res):
            outs[tag][n] = r.transpose(0, 2, 1) if n in _COL_SHARDED else r
    for tag, smallflat in zip(("grad", "delta", "new_m", "new_v"), _adamw_small(packed(wts), sall, packed(mom), packed(var))):
        outs[tag].update(zip(_SMALL, _unpack_flat(smallflat.reshape(-1), small_shapes)))

    total = lax.psum(loss[0, 0], ("x", "y", "c"))
    return (total, grad_x[None], *[outs["grad"][n] for n in _ORDER], *[outs["delta"][n] for n in _ORDER],
            *[outs["new_m"][n] for n in _ORDER], *[outs["new_v"][n] for n in _ORDER])
```

```python
import functools
import math

import numpy as np
import jax
import jax.numpy as jnp
from jax import lax
from jax.experimental import pallas as pl
from jax.experimental.pallas import tpu as pltpu

F32 = jnp.float32
CDT = jnp.bfloat16
WIRE = jnp.bfloat16

HEAD_DIM = 64
GRID_W = 64
ROPE_THETA = 10000.0
MLA_HEADS = 6
MLA_Q_RANK = 256
MLA_KV_RANK = 128
MLA_ROPE_DIM = 32
DIL_HEADS = 6
DIL_BRANCHES = ((128, 1), (512, 4), (2048, 16))
DIL_HALF = 64
GQA_Q_HEADS = 4
REL_BUCKETS = 32
REL_MAX_DIST = 1024
NEG_INF = -1e30
LANES = 128
VMEM_LIMIT = 56 * 1024 * 1024

ADAM_LR, ADAM_B1, ADAM_B2, ADAM_EPS, ADAM_WD, ADAM_STEP = 0.001, 0.9, 0.999, 1e-08, 0.01, 10

C_CQ, C_CKV, C_KR, C_DQ, C_DK, C_DV, C_GQ, C_GK, C_GV, IN_P = 0, 256, 384, 512, 896, 1280, 1664, 1920, 2048, 2176
IN_W = 2080
MESH_ID = pl.DeviceIdType.MESH


def _cparams(n_axes, vmem=VMEM_LIMIT):
    return pltpu.CompilerParams(dimension_semantics=("arbitrary",) * n_axes, vmem_limit_bytes=vmem)


MAX_WHOLE_DIM = 2304
WHOLE_K_WINDOW_BYTES = 36 * 1024 * 1024


def _pick(n, target):
    best = None
    for t in range(LANES, min(n, target) + 1, LANES):
        if n % t == 0:
            best = t
    if best is not None and (2 * best >= target or n > MAX_WHOLE_DIM):
        return best
    return n


def _sds(shape, dtype):
    return jax.ShapeDtypeStruct(tuple(shape), dtype)


def _in_cols():
    idx = -np.ones((IN_P,), np.int64)
    idx[C_CQ:C_CQ + 256] = np.arange(0, 256)
    idx[C_CKV:C_CKV + 128] = np.arange(256, 384)
    idx[C_KR + 64:C_KR + 96] = np.arange(384, 416)
    idx[C_DQ:C_DQ + 1152] = np.arange(416, 1568)
    gq = 1568 + (np.array([0, 2, 1, 3])[:, None] * 64 + np.arange(64)[None, :]).reshape(-1)
    idx[C_GQ:C_GQ + 256] = gq
    idx[C_GK:C_GK + 256] = np.arange(1824, 2080)
    return idx


def _uq_cols():
    idx = -np.ones((MLA_HEADS * 128,), np.int64)
    for h in range(MLA_HEADS):
        idx[h * 128:h * 128 + 96] = np.arange(96 * h, 96 * h + 96)
    return idx


def _ukv_cols():
    idx = -np.ones((MLA_HEADS * 128 + MLA_HEADS * 64,), np.int64)
    for h in range(MLA_HEADS):
        idx[h * 128:h * 128 + 64] = np.arange(128 * h, 128 * h + 64)
        idx[768 + h * 64:768 + h * 64 + 64] = np.arange(128 * h + 64, 128 * h + 128)
    return idx


def _runs(idx):
    out, i = [], 0
    while i < len(idx):
        j = i + 1
        while j < len(idx) and ((idx[i] < 0 and idx[j] < 0) or (idx[i] >= 0 and idx[j] == idx[j - 1] + 1)):
            j += 1
        out.append((int(idx[i]), j - i))
        i = j
    return out


def _rows_from_shards(sh, idx):
    _, cs, r = sh.shape
    pieces = []
    for first, ln in _runs(idx):
        if first < 0:
            pieces.append(jnp.zeros((ln, r), sh.dtype))
            continue
        while ln > 0:
            k, off = divmod(first, cs)
            take = min(ln, cs - off)
            pieces.append(sh[k, off:off + take, :])
            first, ln = first + take, ln - take
    return jnp.concatenate(pieces, axis=0)


def _rows_to_shards(wp, idx, n):
    inv = np.zeros((n,), np.int64)
    pos = np.nonzero(idx >= 0)[0]
    inv[idx[pos]] = pos
    cs = n // 4
    shards = []
    for k in range(4):
        pieces = [wp[first:first + ln, :] for first, ln in _runs(inv[k * cs:(k + 1) * cs])]
        shards.append(jnp.concatenate(pieces, axis=0))
    return jnp.stack(shards)


def _t5_bucket_np(rel):
    nb = REL_BUCKETS // 2
    exact = nb // 2
    ret = np.where(rel > 0, nb, 0)
    n = np.abs(rel)
    nf = np.maximum(n, 1).astype(np.float32)
    large = exact + (np.log(nf / np.float32(exact)) / np.float32(math.log(REL_MAX_DIST / exact))
                     * np.float32(nb - exact)).astype(np.int32)
    large = np.minimum(large, nb - 1)
    return ret + np.where(n < exact, n, large)


def _branch_bucket_idx(tq, dil):
    kw = tq + 2 * DIL_HALF
    rel = np.arange(kw)[None, :] - DIL_HALF - np.arange(tq)[:, None]
    idx = _t5_bucket_np(rel * dil)
    return np.where(np.abs(rel) <= DIL_HALF, idx, -1).astype(np.int32)


def _rope_tables(S):
    inv = ROPE_THETA ** (-jnp.arange(0, 32, 2, dtype=F32) / 32)
    t = jnp.arange(S)
    pos = t.astype(F32)
    row = (t // GRID_W).astype(F32)
    col = (t % GRID_W).astype(F32)
    lane = np.arange(LANES)
    wm = lane - 64
    is_rope = (lane >= 64) & (lane < 96)
    ang = pos[:, None] * inv[np.where(is_rope, wm % 16, 0)][None, :]
    cm = jnp.where(is_rope[None], jnp.cos(ang), 1.0)
    smm = jnp.where((is_rope & (wm < 16))[None], -jnp.sin(ang), 0.0)
    spm = jnp.where((is_rope & (wm >= 16))[None], jnp.sin(ang), 0.0)
    g = lane % 64
    w = g % 32
    angg = jnp.where((g < 32)[None], row[:, None], col[:, None]) * inv[w % 16][None, :]
    cg = jnp.cos(angg)
    smg = jnp.where((w < 16)[None], -jnp.sin(angg), 0.0)
    spg = jnp.where((w >= 16)[None], jnp.sin(angg), 0.0)
    return (cm, smm, spm), (cg, smg, spg)


def _lanes(t, width):
    return t if width == LANES else jnp.concatenate([t] * (width // LANES), axis=1)


def _rope(x, tabs):
    c, sm, sp = (_lanes(t, x.shape[1]) for t in tabs)
    w = x.shape[1]
    return x * c + pltpu.roll(x, w - 16, 1) * sm + pltpu.roll(x, 16, 1) * sp


def _rope_t(dy, tabs):
    c, sm, sp = (_lanes(t, dy.shape[1]) for t in tabs)
    w = dy.shape[1]
    return dy * c + pltpu.roll(dy * sm, 16, 1) + pltpu.roll(dy * sp, w - 16, 1)


def _head_ones(width):
    i = np.arange(width)
    return jnp.asarray((i[:, None] // HEAD_DIM == i[None, :] // HEAD_DIM).astype(np.float32))


def _headsum(x, j):
    return jnp.dot(x, j, preferred_element_type=F32, precision=lax.Precision.HIGHEST)


def _mm(a, b, *, ta=False, tb=False, ga=False, gb=False, go=False, out_dtype=F32, name):
    G = a.shape[0] if ga else (b.shape[0] if gb else 1)
    a2 = a.shape[1:] if ga else a.shape
    b2 = b.shape[1:] if gb else b.shape
    K, M = a2 if ta else a2[::-1]
    N = b2[0] if tb else b2[1]
    assert (b2[1] if tb else b2[0]) == K
    tm, tn, tk = _pick(M, 1024), _pick(N, 1024), _pick(K, 2048)
    if tm * tn > 1024 * 1152:
        tm = _pick(M, 512)
    if 2 * K * (tm + tn) * jnp.dtype(CDT).itemsize <= WHOLE_K_WINDOW_BYTES:
        tk = K
    nk = K // tk
    steps = nk if (go or G == 1) else G * nk
    dn = (((0 if ta else 1,), (1 if tb else 0,)), ((), ()))

    def body(a_ref, b_ref, o_ref, *acc):
        part = lax.dot_general(a_ref[...], b_ref[...], dn, preferred_element_type=F32)
        if steps == 1:
            o_ref[...] = part.astype(o_ref.dtype)
            return
        acc_ref, = acc
        s = pl.program_id(3)

        @pl.when(s == 0)
        def _():
            acc_ref[...] = part

        @pl.when(s > 0)
        def _():
            acc_ref[...] += part

        @pl.when(s == steps - 1)
        def _():
            o_ref[...] = acc_ref[...].astype(o_ref.dtype)

    def grp(g, s):
        return g if go else s // nk

    def kk(s):
        return s if steps == nk else s % nk

    def spec(grouped, block, index):
        if grouped:
            return pl.BlockSpec((None,) + block, lambda g, i, j, s: (grp(g, s),) + index(i, j, s))
        return pl.BlockSpec(block, lambda g, i, j, s: index(i, j, s))

    a_spec = (spec(ga, (tk, tm), lambda i, j, s: (kk(s), i)) if ta else spec(ga, (tm, tk), lambda i, j, s: (i, kk(s))))
    b_spec = (spec(gb, (tn, tk), lambda i, j, s: (j, kk(s))) if tb else spec(gb, (tk, tn), lambda i, j, s: (kk(s), j)))
    o_spec = spec(go, (tm, tn), lambda i, j, s: (i, j))
    return pl.pallas_call(
        body, name=name, grid=(G if go else 1, M // tm, N // tn, steps),
        in_specs=[a_spec, b_spec], out_specs=o_spec,
        out_shape=_sds(((G,) if go else ()) + (M, N), out_dtype),
        scratch_shapes=[pltpu.VMEM((tm, tn), F32)] if steps > 1 else [],
        compiler_params=_cparams(4),
    )(a, b)


def _row(ts, w, cb=0):
    return pl.BlockSpec((ts, w), lambda i: (i, cb))


def _full(shape):
    nd = len(shape)
    return pl.BlockSpec(tuple(shape), lambda i: (0,) * nd)


def _rms_fwd(x, g, eps=1e-6):
    r = lax.rsqrt(jnp.mean(x * x, axis=-1, keepdims=True) + eps)
    return x * r * g


def _rms_bwd(x, g, dy, eps=1e-6):
    r = lax.rsqrt(jnp.mean(x * x, axis=-1, keepdims=True) + eps)
    gdy = g * dy
    dx = r * gdy - x * (r * r * r) * jnp.mean(x * gdy, axis=-1, keepdims=True)
    return dx, x * r * dy


def _rms_head_fwd(x, g, j, eps=1e-6):
    r = lax.rsqrt(_headsum(x * x, j) * (1.0 / HEAD_DIM) + eps)
    return x * r * g


def _rms_head_bwd(x, g, dy, j, eps=1e-6):
    r = lax.rsqrt(_headsum(x * x, j) * (1.0 / HEAD_DIM) + eps)
    gdy = g * dy
    dx = r * gdy - x * (r * r * r) * (_headsum(x * gdy, j) * (1.0 / HEAD_DIM))
    return dx, x * r * dy


DIL_STRIDES = tuple(d for _, d in DIL_BRANCHES if d > 1)
DIL_W = DIL_HEADS * HEAD_DIM


def _res_spec(d, n, pad_blocks=0):
    return pl.BlockSpec((d, n, DIL_W), lambda i: (0, i + pad_blocks, 0))


def _prep_fwd(h, gq, gkv, ggq, ggk, tm, tg, j256):
    S = h.shape[0]
    ts = min(256, S)
    scale = HEAD_DIM ** -0.5
    nres = len(DIL_STRIDES)

    def body(*refs):
        (h_ref, gq_ref, gkv_ref, ggq_ref, ggk_ref, cm, smm, spm, cg, smg, spg, j_ref), refs = refs[:12], refs[12:]
        refs = refs[2 * nres:]
        (cq_o, ckv_o, kr_o, dq_o, dk_o, dv_o, gq_o, gk_o, gv_o), res_o = refs[:9], refs[9:-1]
        st = refs[-1]
        tabm = (cm[...], smm[...], spm[...])
        tabg = (cg[...], smg[...], spg[...])
        cq_o[...] = _rms_fwd(h_ref[:, C_CQ:C_CQ + 256], gq_ref[...]).astype(CDT)
        ckv_o[...] = _rms_fwd(h_ref[:, C_CKV:C_CKV + 128], gkv_ref[...]).astype(CDT)
        kr_o[...] = _rope(h_ref[:, C_KR:C_KR + 128], tabm).astype(CDT)
        dq_o[...] = (h_ref[:, C_DQ:C_DQ + 384] * scale).astype(CDT)
        dk_o[...] = h_ref[:, C_DK:C_DK + 384].astype(CDT)
        dv_o[...] = h_ref[:, C_DV:C_DV + 384].astype(CDT)
        for j, lanes in _lane_blocks(3 * DIL_W):
            st[j] = h_ref[:, C_DQ + lanes.start:C_DQ + lanes.stop] * (scale if j < 3 else 1.0)
        for bi, d in enumerate(DIL_STRIDES):
            for c in range(d):
                rows = pl.ds(c, ts // d, stride=d)
                for j, lanes in _lane_blocks(3 * DIL_W):
                    res_o[3 * bi + j // 3][c, :, (j % 3) * LANES:(j % 3 + 1) * LANES] = st.at[j][rows, :].astype(CDT)
        qn = _rms_head_fwd(h_ref[:, C_GQ:C_GQ + 256], ggq_ref[...], j_ref[...])
        gq_o[...] = (_rope(qn, tabg) * scale).astype(CDT)
        kn = _rms_head_fwd(h_ref[:, C_GK:C_GK + 128], ggk_ref[...], j_ref[0:128, 0:128])
        gk_o[...] = _rope(kn, tabg).astype(CDT)
        gv_o[...] = h_ref[:, C_GV:C_GV + 128].astype(CDT)

    widths = (256, 128, 128, 384, 384, 384, 256, 128, 128)
    out_specs = [_row(ts, w) for w in widths]
    out_shape = [_sds((S, w), CDT) for w in widths]
    zeros, aliases = [], {}
    for d in DIL_STRIDES:
        n, L = ts // d, S // d
        out_specs += [_res_spec(d, n), _res_spec(d, n, DIL_HALF // n), _res_spec(d, n, DIL_HALF // n)]
        out_shape += [_sds((d, L, DIL_W), CDT)] + [_sds((d, L + 2 * DIL_HALF, DIL_W), CDT)] * 2
        for t in range(2):
            aliases[12 + len(zeros)] = len(out_shape) - 2 + t
            zeros.append(jnp.zeros((d, L + 2 * DIL_HALF, DIL_W), CDT))
    return pl.pallas_call(
        body, name="prep_fwd", grid=(S // ts,),
        in_specs=[_row(ts, IN_P), _full(gq.shape), _full(gkv.shape), _full(ggq.shape), _full(ggk.shape)]
        + [_row(ts, LANES)] * 6 + [_full(j256.shape)] + [pl.BlockSpec(memory_space=pl.ANY)] * len(zeros),
        out_specs=out_specs, out_shape=out_shape, input_output_aliases=aliases,
        scratch_shapes=[pltpu.VMEM((3 * DIL_W // LANES, ts, LANES), F32)],
        compiler_params=_cparams(1),
    )(h, gq, gkv, ggq, ggk, *tm, *tg, j256, *zeros)


def _prep_bwd(h, dcq, dckv, dkr, ddq, ddk, ddv, dgq, dgk, dgv, gq, gkv, ggq, ggk, tg, j256):
    S = h.shape[0]
    ts = min(256, S)
    scale = HEAD_DIM ** -0.5

    def body(h_ref, dcq_r, dckv_r, dkr_r, q1, q2, q3, k1, k2, k3, v1, v2, v3, dgq_r, dgk_r, dgv_r,
             gq_ref, gkv_ref, ggq_ref, ggk_ref, cg, smg, spg, j_ref,
             dh_o, ngq_o, ngkv_o, nggq_o, nggk_o, *scr):
        tabg = (cg[...], smg[...], spg[...])
        first = pl.program_id(0) == 0
        scr, = scr
        d2, d3 = DIL_STRIDES
        dq = q1[...] + _by_token(q2, scr, d2) + _by_token(q3, scr, d3)
        dk = k1[...] + _by_token(k2, scr, d2) + _by_token(k3, scr, d3)
        dv = v1[...] + _by_token(v2, scr, d2) + _by_token(v3, scr, d3)

        def acc(o_ref, val):
            s = jnp.sum(val, axis=0, keepdims=True)

            @pl.when(first)
            def _():
                o_ref[...] = s

            @pl.when(jnp.logical_not(first))
            def _():
                o_ref[...] += s

        dx, dg = _rms_bwd(h_ref[:, C_CQ:C_CQ + 256], gq_ref[...], dcq_r[...])
        dh_o[:, C_CQ:C_CQ + 256] = dx.astype(CDT)
        acc(ngq_o, dg)
        dx, dg = _rms_bwd(h_ref[:, C_CKV:C_CKV + 128], gkv_ref[...], dckv_r[...])
        dh_o[:, C_CKV:C_CKV + 128] = dx.astype(CDT)
        acc(ngkv_o, dg)
        dh_o[:, C_KR:C_KR + 128] = dkr_r[...].astype(CDT)
        dh_o[:, C_DQ:C_DQ + 384] = (dq * scale).astype(CDT)
        dh_o[:, C_DK:C_DK + 384] = dk.astype(CDT)
        dh_o[:, C_DV:C_DV + 384] = dv.astype(CDT)
        dqn = _rope_t(dgq_r[...] * scale, tabg)
        dx, dg = _rms_head_bwd(h_ref[:, C_GQ:C_GQ + 256], ggq_ref[...], dqn, j_ref[...])
        dh_o[:, C_GQ:C_GQ + 256] = dx.astype(CDT)
        acc(nggq_o, dg)
        dkn = _rope_t(dgk_r[...], tabg)
        dx, dg = _rms_head_bwd(h_ref[:, C_GK:C_GK + 128], ggk_ref[...], dkn, j_ref[0:128, 0:128])
        dh_o[:, C_GK:C_GK + 128] = dx.astype(CDT)
        acc(nggk_o, dg)
        dh_o[:, C_GV:C_GV + 128] = dgv_r[...].astype(CDT)

    d2, d3 = DIL_STRIDES
    n2, n3 = ts // d2, ts // d3
    tok = _row(ts, DIL_W)
    return pl.pallas_call(
        body, name="prep_bwd", grid=(S // ts,),
        in_specs=[_row(ts, IN_P), _row(ts, 256), _row(ts, 128), _row(ts, 128)]
        + [tok, _res_spec(d2, n2), _res_spec(d3, n3)]
        + [tok, _res_spec(d2, n2, DIL_HALF // n2), _res_spec(d3, n3, DIL_HALF // n3)] * 2
        + [_row(ts, 256), _row(ts, 128), _row(ts, 128)]
        + [_full(gq.shape), _full(gkv.shape), _full(ggq.shape), _full(ggk.shape)] + [_row(ts, LANES)] * 3
        + [_full(j256.shape)],
        out_specs=[_row(ts, IN_P), _full((1, 256)), _full((1, 128)), _full((1, 256)), _full((1, 128))],
        out_shape=[_sds((S, IN_P), CDT), _sds((1, 256), F32), _sds((1, 128), F32), _sds((1, 256), F32),
                   _sds((1, 128), F32)],
        scratch_shapes=[_TOKEN_SCRATCH(ts)],
        compiler_params=_cparams(1),
    )(h, dcq, dckv, dkr, *ddq, *ddk, *ddv, dgq, dgk, dgv, gq, gkv, ggq, ggk, *tg, j256)


def _lane_blocks(width):
    return [(j, slice(j * LANES, (j + 1) * LANES)) for j in range(width // LANES)]


_TOKEN_SCRATCH = lambda ts: pltpu.VMEM((DIL_W // LANES, ts, LANES), F32)


def _by_token(res_ref, scr_ref, d):
    n = res_ref.shape[1]
    if d == 1:
        return res_ref[0].astype(F32)
    for c in range(d):
        for j, lanes in _lane_blocks(res_ref.shape[2]):
            scr_ref.at[j][pl.ds(c, n, stride=d), :] = res_ref[c, :, lanes].astype(F32)
    return jnp.concatenate([scr_ref[j] for j, _ in _lane_blocks(res_ref.shape[2])], axis=1)


def _by_residue(val, scr_ref, out_ref, d):
    n = out_ref.shape[1]
    if d == 1:
        out_ref[0] = val.astype(out_ref.dtype)
        return
    for j, lanes in _lane_blocks(out_ref.shape[2]):
        scr_ref[j] = val[:, lanes]
    for c in range(d):
        for j, lanes in _lane_blocks(out_ref.shape[2]):
            out_ref[c, :, lanes] = scr_ref.at[j][pl.ds(c, n, stride=d), :].astype(out_ref.dtype)


def _mla_prep_fwd(qa, kvp, kr, tm, scale):
    S = qa.shape[0]
    ts = min(256, S)

    def body(qa_ref, kv_ref, kr_ref, cm, smm, spm, q_o, k_o):
        tabm = (cm[...], smm[...], spm[...])
        q_o[...] = (_rope(qa_ref[...], tabm) * scale).astype(CDT)
        k_o[...] = kv_ref[:, 0:768] + _lanes(kr_ref[...], 768)

    return pl.pallas_call(
        body, name="mla_prep_fwd", grid=(S // ts,),
        in_specs=[_row(ts, 768), _row(ts, 1152), _row(ts, 128)] + [_row(ts, LANES)] * 3,
        out_specs=[_row(ts, 768)] * 2, out_shape=[_sds((S, 768), CDT)] * 2,
        compiler_params=_cparams(1),
    )(qa, kvp, kr, *tm)


def _mla_prep_bwd(dq, dk, dv, tm, scale):
    S = dq.shape[0]
    ts = min(256, S)

    def body(dq_ref, dk_ref, dv_ref, cm, smm, spm, dqa_o, dkv_o, dkr_o):
        tabm = (cm[...], smm[...], spm[...])
        lane = lax.broadcasted_iota(jnp.int32, (1, LANES), 1)
        dqa_o[...] = _rope_t(dq_ref[...] * scale, tabm).astype(CDT)
        dkr = jnp.zeros((ts, LANES), F32)
        for hd in range(MLA_HEADS):
            blk = dk_ref[:, hd * 128:(hd + 1) * 128]
            dkv_o[:, hd * 128:(hd + 1) * 128] = jnp.where(lane < 64, blk, 0.0).astype(CDT)
            dkr = dkr + jnp.where((lane >= 64) & (lane < 96), blk, 0.0)
        dkv_o[:, 768:1152] = dv_ref[...].astype(CDT)
        dkr_o[...] = jnp.where((lane >= 64) & (lane < 96), _rope_t(dkr, tabm), 0.0)

    return pl.pallas_call(
        body, name="mla_prep_bwd", grid=(S // ts,),
        in_specs=[_row(ts, 768), _row(ts, 768), _row(ts, 384)] + [_row(ts, LANES)] * 3,
        out_specs=[_row(ts, 768), _row(ts, 1152), _row(ts, 128)],
        out_shape=[_sds((S, 768), CDT), _sds((S, 1152), CDT), _sds((S, 128), F32)],
        compiler_params=_cparams(1),
    )(dq, dk, dv, *tm)


def _ln_fwd(xa, xb, g, b, alpha, name):
    S, D = xa.shape
    ts = min(512, S)

    def body(xa_ref, xb_ref, g_ref, b_ref, y_o, yb_o, z_o):
        z = alpha * xa_ref[...] + xb_ref[...]
        mu = jnp.mean(z, axis=-1, keepdims=True)
        zc = z - mu
        var = jnp.mean(zc * zc, axis=-1, keepdims=True)
        y = zc * lax.rsqrt(var + 1e-5) * g_ref[...] + b_ref[...]
        y_o[...] = y
        yb_o[...] = y.astype(CDT)
        z_o[...] = z

    return pl.pallas_call(
        body, name=name, grid=(S // ts,),
        in_specs=[_row(ts, D), _row(ts, D), _full(g.shape), _full(b.shape)],
        out_specs=[_row(ts, D)] * 3, out_shape=[_sds((S, D), F32), _sds((S, D), CDT), _sds((S, D), F32)],
        compiler_params=_cparams(1),
    )(xa, xb, g, b)


def _ln_bwd(dya, dyb, z, g, alpha, name, loss_head=False):
    S, D = z.shape
    ts = min(512, S)

    def body(dya_ref, dyb_ref, z_ref, g_ref, dz_o, dzb_o, dg_o, db_o, *loss_o):
        first = pl.program_id(0) == 0
        if loss_head:
            err = dya_ref[...] - dyb_ref[...]
            dy = err * (1.0 / D)
            part = jnp.sum(jnp.sum(err * err, axis=1, keepdims=True), axis=0, keepdims=True) * (0.5 / D)

            @pl.when(first)
            def _():
                loss_o[0][...] = part

            @pl.when(jnp.logical_not(first))
            def _():
                loss_o[0][...] += part
        else:
            dy = dya_ref[...] + alpha * dyb_ref[...]
        z = z_ref[...]
        mu = jnp.mean(z, axis=-1, keepdims=True)
        zc = z - mu
        r = lax.rsqrt(jnp.mean(zc * zc, axis=-1, keepdims=True) + 1e-5)
        xh = zc * r
        dxh = dy * g_ref[...]
        dz = r * (dxh - jnp.mean(dxh, axis=-1, keepdims=True) - xh * jnp.mean(dxh * xh, axis=-1, keepdims=True))
        dz_o[...] = dz
        dzb_o[...] = dz.astype(CDT)
        sg = jnp.sum(dy * xh, axis=0, keepdims=True)
        sb = jnp.sum(dy, axis=0, keepdims=True)

        @pl.when(first)
        def _():
            dg_o[...] = sg
            db_o[...] = sb

        @pl.when(jnp.logical_not(first))
        def _():
            dg_o[...] += sg
            db_o[...] += sb

    extra = ([_full((1, 1))], [_sds((1, 1), F32)]) if loss_head else ([], [])
    return pl.pallas_call(
        body, name=name, grid=(S // ts,),
        in_specs=[_row(ts, D)] * 3 + [_full(g.shape)],
        out_specs=[_row(ts, D), _row(ts, D), _full((1, D)), _full((1, D))] + extra[0],
        out_shape=[_sds((S, D), F32), _sds((S, D), CDT), _sds((1, D), F32), _sds((1, D), F32)] + extra[1],
        compiler_params=_cparams(1),
    )(dya, dyb, z, g)


def _grp_spec(ts, w):
    return pl.BlockSpec((None, ts, w), lambda k, i: (k, i, 0))


def _ffn_up(xb, wg3, wu3):
    S, D = xb.shape
    G, Fc, _ = wg3.shape
    tm = _pick(S, 1024)
    wspec = pl.BlockSpec((None, Fc, D), lambda k, i: (k, 0, 0))

    def body(x_ref, wg_ref, wu_ref, g_o, u_o, a_o):
        x = x_ref[...]
        g = lax.dot_general(x, wg_ref[...], _NT, preferred_element_type=F32)
        u = lax.dot_general(x, wu_ref[...], _NT, preferred_element_type=F32)
        g_o[...] = g.astype(CDT)
        u_o[...] = u.astype(CDT)
        a_o[...] = (g / (1.0 + jnp.exp(-g)) * u).astype(CDT)

    return pl.pallas_call(
        body, name="ffn_up", grid=(G, S // tm),
        in_specs=[pl.BlockSpec((tm, D), lambda k, i: (i, 0)), wspec, wspec], out_specs=[_grp_spec(tm, Fc)] * 3,
        out_shape=[_sds((G, S, Fc), CDT)] * 3, compiler_params=_cparams(2),
    )(xb, wg3, wu3)


def _ffn_up_dx(dg3, du3, wg3, wu3):
    G, S, Fc = dg3.shape
    D = wg3.shape[2]
    tm = _pick(S, 512)
    wspec = pl.BlockSpec((G, Fc, D), lambda i: (0, 0, 0))
    aspec = pl.BlockSpec((G, tm, Fc), lambda i: (0, i, 0))

    def body(dg_ref, du_ref, wg_ref, wu_ref, o_ref):
        acc = None
        for k in range(G):
            for a_ref, w_ref in ((dg_ref, wg_ref), (du_ref, wu_ref)):
                part = jnp.dot(a_ref[k], w_ref[k], preferred_element_type=F32)
                acc = part if acc is None else acc + part
        o_ref[...] = acc

    return pl.pallas_call(
        body, name="ffn_up_dx", grid=(S // tm,), in_specs=[aspec, aspec, wspec, wspec],
        out_specs=pl.BlockSpec((tm, D), lambda i: (i, 0)), out_shape=_sds((S, D), F32), compiler_params=_cparams(1),
    )(dg3, du3, wg3, wu3)


def _ffn_down(act3, wd3):
    G, S, Fc = act3.shape
    D = wd3.shape[2]
    tm = _pick(S, 1024)

    def body(a_ref, w_ref, o_ref):
        acc = jnp.dot(a_ref[0], w_ref[0], preferred_element_type=F32)
        for k in range(1, G):
            acc = acc + jnp.dot(a_ref[k], w_ref[k], preferred_element_type=F32)
        o_ref[...] = acc

    return pl.pallas_call(
        body, name="ffn_down", grid=(S // tm,),
        in_specs=[pl.BlockSpec((G, tm, Fc), lambda i: (0, i, 0)), pl.BlockSpec((G, Fc, D), lambda i: (0, 0, 0))],
        out_specs=pl.BlockSpec((tm, D), lambda i: (i, 0)), out_shape=_sds((S, D), F32), compiler_params=_cparams(1),
    )(act3, wd3)


def _ffn_down_dx(dzb, wd3, g3, u3):
    S, D = dzb.shape
    G, Fc, _ = wd3.shape
    tm = _pick(S, 1024)

    def body(dz_ref, wd_ref, g_ref, u_ref, dg_o, du_o):
        da = lax.dot_general(dz_ref[...], wd_ref[...], _NT, preferred_element_type=F32)
        g = g_ref[...].astype(F32)
        sg = 1.0 / (1.0 + jnp.exp(-g))
        dg_o[...] = (da * u_ref[...].astype(F32) * (sg * (1.0 + g * (1.0 - sg)))).astype(CDT)
        du_o[...] = (da * (g * sg)).astype(CDT)

    return pl.pallas_call(
        body, name="ffn_down_dx", grid=(G, S // tm),
        in_specs=[pl.BlockSpec((tm, D), lambda k, i: (i, 0)), pl.BlockSpec((None, Fc, D), lambda k, i: (k, 0, 0)),
                  _grp_spec(tm, Fc), _grp_spec(tm, Fc)],
        out_specs=[_grp_spec(tm, Fc)] * 2, out_shape=[_sds((G, S, Fc), CDT)] * 2, compiler_params=_cparams(2),
    )(dzb, wd3, g3, u3)


def _axpy(a, b, alpha, name):
    S, D = a.shape
    ts = min(512, S)

    def body(a_ref, b_ref, o_ref):
        o_ref[...] = a_ref[...] + alpha * b_ref[...]

    return pl.pallas_call(
        body, name=name, grid=(S // ts,), in_specs=[_row(ts, D)] * 2, out_specs=_row(ts, D),
        out_shape=_sds((S, D), F32), compiler_params=_cparams(1),
    )(a, b)


def _pair_masks():
    lane = lax.broadcasted_iota(jnp.int32, (1, LANES), 1)
    first = lane < HEAD_DIM
    return first, jnp.logical_not(first)


def _head_scalar(x, m):
    return jnp.max(jnp.where(m, x, -jnp.inf), axis=-1, keepdims=True)


_NT = (((1,), (1,)), ((), ()))
_TN = (((0,), (0,)), ((), ()))
ATTN_KEY_CHUNK = 1024


def _attn_fwd(q, k, v, *, split, npairs, kblk, vblk, name):
    S = q.shape[0]
    qw = 256 if split else LANES
    tq = min(256, S)

    def body(q_ref, k_ref, v_ref, o_ref, lse_ref):
        masks = _pair_masks()
        outs, lses = [], []
        for hd in range(2):
            if split:
                qh = q_ref[:, hd * LANES:(hd + 1) * LANES]
                kh = k_ref[:, hd * LANES:(hd + 1) * LANES]
            else:
                qh = jnp.where(masks[hd], q_ref[...], jnp.zeros_like(q_ref[...]))
                kh = k_ref[...]
            s = lax.dot_general(qh, kh, _NT, preferred_element_type=F32)
            mx = jnp.max(s, axis=-1, keepdims=True)
            p = jnp.exp(s - mx)
            l = jnp.sum(p, axis=-1, keepdims=True)
            o = jnp.dot(p.astype(CDT), v_ref[...], preferred_element_type=F32)
            outs.append(o / l)
            lses.append(jnp.broadcast_to(mx + jnp.log(l), (tq, LANES)))
        o_ref[...] = jnp.where(masks[0], outs[0], outs[1]).astype(o_ref.dtype)
        lse_ref[...] = jnp.where(masks[0], lses[0], lses[1])

    return pl.pallas_call(
        body, name=name, grid=(npairs, S // tq),
        in_specs=[pl.BlockSpec((tq, qw), lambda p, i: (i, p)),
                  pl.BlockSpec((S, qw), lambda p, i: (0, kblk(p))),
                  pl.BlockSpec((S, LANES), lambda p, i: (0, vblk(p)))],
        out_specs=[pl.BlockSpec((tq, LANES), lambda p, i: (i, p))] * 2,
        out_shape=[_sds((S, LANES * npairs), CDT), _sds((S, LANES * npairs), F32)],
        compiler_params=_cparams(2),
    )(q, k, v)


def _attn_bwd(q, k, v, do, o, lse, *, split, npairs, kblk, vblk, doblk, shared_kv, name):
    S = q.shape[0]
    qw = 256 if split else LANES
    tq = min(512, S)
    tkv = min(ATTN_KEY_CHUNK, S)
    nkv = 1 if shared_kv else npairs

    def body(q_ref, k_ref, v_ref, do_ref, o_ref, lse_ref, dq_ref, dk_ref, dv_ref):
        masks = _pair_masks()
        p_id, i_id = pl.program_id(0), pl.program_id(1)
        first = (i_id == 0) & ((p_id == 0) if shared_kv else True)
        @pl.when(first)
        def _():
            dk_ref[...] = jnp.zeros_like(dk_ref)
            dv_ref[...] = jnp.zeros_like(dv_ref)

        do = do_ref[...]
        o = o_ref[...].astype(F32)
        lse = lse_ref[...]
        heads = []
        for hd in range(2):
            m = masks[hd]
            cols = slice(hd * LANES, (hd + 1) * LANES) if split else slice(None)
            qh = q_ref[:, cols] if split else jnp.where(m, q_ref[...], jnp.zeros_like(q_ref[...]))
            doh = jnp.where(m, do, 0.0)
            heads.append((m, cols, qh, doh.astype(CDT), _head_scalar(lse, m), jnp.sum(doh * o, axis=-1, keepdims=True)))
        dqs = [jnp.zeros((tq, LANES), F32), jnp.zeros((tq, LANES), F32)]
        for ck in range(S // tkv):
            rows = slice(ck * tkv, (ck + 1) * tkv)
            v = v_ref[rows, :]
            dv = jnp.zeros((tkv, LANES), F32)
            for hd, (m, cols, qh, dohb, lse_h, delta) in enumerate(heads):
                kh = k_ref[rows, cols]
                s = lax.dot_general(qh, kh, _NT, preferred_element_type=F32)
                p = jnp.exp(s - lse_h)
                dp = lax.dot_general(dohb, v, _NT, preferred_element_type=F32)
                ds = (p * (dp - delta)).astype(CDT)
                dq = jnp.dot(ds, kh, preferred_element_type=F32)
                dqs[hd] = dqs[hd] + (dq if split else jnp.where(m, dq, 0.0))
                dk_ref[rows, cols] += lax.dot_general(ds, qh, _TN, preferred_element_type=F32)
                dv = dv + lax.dot_general(p.astype(CDT), dohb, _TN, preferred_element_type=F32)
            dv_ref[rows, :] += dv
        if split:
            dq_ref[:, 0:LANES] = dqs[0]
            dq_ref[:, LANES:2 * LANES] = dqs[1]
        else:
            dq_ref[...] = dqs[0] + dqs[1]

    kvo = (lambda p, i: (0, 0)) if shared_kv else (lambda p, i: (0, p))
    return pl.pallas_call(
        body, name=name, grid=(npairs, S // tq),
        in_specs=[pl.BlockSpec((tq, qw), lambda p, i: (i, p)),
                  pl.BlockSpec((S, qw), lambda p, i: (0, kblk(p))),
                  pl.BlockSpec((S, LANES), lambda p, i: (0, vblk(p))),
                  pl.BlockSpec((tq, LANES), lambda p, i: (i, doblk(p))),
                  pl.BlockSpec((tq, LANES), lambda p, i: (i, p)),
                  pl.BlockSpec((tq, LANES), lambda p, i: (i, p))],
        out_specs=[pl.BlockSpec((tq, qw), lambda p, i: (i, p)),
                   pl.BlockSpec((S, qw), kvo), pl.BlockSpec((S, LANES), kvo)],
        out_shape=[_sds((S, qw * npairs), F32), _sds((S, qw * nkv), F32), _sds((S, LANES * nkv), F32)],
        compiler_params=_cparams(2),
    )(q, k, v, do, o, lse)


def _bias_expand(idx, rel_bias, name):
    tq, kw = idx.shape

    def body(idx_ref, rb_ref, o_ref):
        idx = idx_ref[...]
        for hd in range(DIL_HEADS):
            acc = jnp.full((tq, kw), NEG_INF, F32)
            for u in range(REL_BUCKETS):
                acc = jnp.where(idx == u, rb_ref[u, hd], acc)
            o_ref[hd] = acc

    return pl.pallas_call(
        body, name=name,
        in_specs=[pl.BlockSpec(memory_space=pltpu.VMEM), pl.BlockSpec(memory_space=pltpu.SMEM)],
        out_specs=pl.BlockSpec(memory_space=pltpu.VMEM),
        out_shape=_sds((DIL_HEADS, tq, kw), F32),
    )(idx, rel_bias)


def _bias_reduce(idx, dtab, name):
    tq, kw = idx.shape

    def body(idx_ref, d_ref, o_ref):
        idx = idx_ref[...]
        rowid = lax.broadcasted_iota(jnp.int32, (REL_BUCKETS, kw), 0)
        for hd in range(DIL_HEADS):
            d = d_ref[hd]
            acc = jnp.zeros((REL_BUCKETS, kw), F32)
            for u in range(REL_BUCKETS):
                r = jnp.sum(jnp.where(idx == u, d, 0.0), axis=0, keepdims=True)
                acc = jnp.where(rowid == u, r, acc)
            o_ref[hd] = jnp.sum(acc, axis=1, keepdims=True)

    return pl.pallas_call(
        body, name=name,
        in_specs=[pl.BlockSpec(memory_space=pltpu.VMEM)] * 2, out_specs=pl.BlockSpec(memory_space=pltpu.VMEM),
        out_shape=_sds((DIL_HEADS, REL_BUCKETS, 1), F32),
    )(idx, dtab)


def _dil_window(i, tq, kw, L):
    start = pl.multiple_of(i * tq, DIL_HALF)
    key = start + lax.broadcasted_iota(jnp.int32, (1, kw), 1) - DIL_HALF
    return start, (key >= 0) & (key < L)


def _dil_fwd(qv, kv, vv, tab, *, dil, L, tq, name):
    kw = tq + 2 * DIL_HALF
    npair = DIL_HEADS // 2

    def body(q_ref, k_ref, v_ref, t_ref, o_ref, lse_ref):
        masks = _pair_masks()
        start, valid = _dil_window(pl.program_id(2), tq, kw, L)
        kwin = k_ref[pl.ds(start, kw), :]
        vwin = v_ref[pl.ds(start, kw), :]
        outs, lses = [], []
        for hd in range(2):
            qh = jnp.where(masks[hd], q_ref[...], jnp.zeros_like(q_ref[...]))
            s = lax.dot_general(qh, kwin, _NT, preferred_element_type=F32) + t_ref[hd]
            s = jnp.where(valid, s, NEG_INF)
            mx = jnp.max(s, axis=-1, keepdims=True)
            p = jnp.exp(s - mx)
            l = jnp.sum(p, axis=-1, keepdims=True)
            outs.append(jnp.dot(p.astype(CDT), vwin, preferred_element_type=F32) / l)
            lses.append(jnp.broadcast_to(mx + jnp.log(l), (tq, LANES)))
        o_ref[...] = jnp.where(masks[0], outs[0], outs[1])
        lse_ref[...] = jnp.where(masks[0], lses[0], lses[1])

    blk = pl.BlockSpec((None, tq, LANES), lambda p, c, i: (c, i, p))
    res = pl.BlockSpec((None, L + 2 * DIL_HALF, LANES), lambda p, c, i: (c, 0, p))
    return pl.pallas_call(
        body, name=name, grid=(npair, dil, L // tq),
        in_specs=[blk, res, res, pl.BlockSpec((2, tq, kw), lambda p, c, i: (p, 0, 0))],
        out_specs=[blk] * 2, out_shape=[_sds(qv.shape, F32)] * 2,
        compiler_params=_cparams(3),
    )(qv, kv, vv, tab)


def _dil_bwd(qv, kv, vv, tab, dov, lsev, deltav, *, dil, L, tq, name):
    kw = tq + 2 * DIL_HALF
    npair = DIL_HEADS // 2

    def body(q_ref, k_ref, v_ref, t_ref, do_ref, lse_ref, dl_ref, dq_ref, dk_ref, dv_ref, dt_ref):
        masks = _pair_masks()
        c_id, i_id = pl.program_id(1), pl.program_id(2)
        start, valid = _dil_window(i_id, tq, kw, L)
        kwin = k_ref[pl.ds(start, kw), :]
        vwin = v_ref[pl.ds(start, kw), :]

        @pl.when(i_id == 0)
        def _():
            dk_ref[...] = jnp.zeros_like(dk_ref)
            dv_ref[...] = jnp.zeros_like(dv_ref)

        @pl.when((i_id == 0) & (c_id == 0))
        def _():
            dt_ref[...] = jnp.zeros_like(dt_ref)

        do = do_ref[...]
        dq = jnp.zeros((tq, LANES), F32)
        dk = jnp.zeros((kw, LANES), F32)
        dv = jnp.zeros((kw, LANES), F32)
        for hd in range(2):
            m = masks[hd]
            qh = jnp.where(m, q_ref[...], jnp.zeros_like(q_ref[...]))
            doh = jnp.where(m, do, jnp.zeros_like(do))
            s = lax.dot_general(qh, kwin, _NT, preferred_element_type=F32) + t_ref[hd]
            s = jnp.where(valid, s, NEG_INF)
            p = jnp.exp(s - _head_scalar(lse_ref[...], m))
            dp = lax.dot_general(doh, vwin, _NT, preferred_element_type=F32)
            ds = p * (dp - _head_scalar(dl_ref[...], m))
            dt_ref[hd] += ds
            dsb = ds.astype(CDT)
            dq = dq + jnp.where(m, jnp.dot(dsb, kwin, preferred_element_type=F32), 0.0)
            dk = dk + lax.dot_general(dsb, qh, _TN, preferred_element_type=F32)
            dv = dv + lax.dot_general(p.astype(CDT), doh, _TN, preferred_element_type=F32)
        dq_ref[...] = dq
        dk_ref[pl.ds(start, kw), :] += dk
        dv_ref[pl.ds(start, kw), :] += dv

    blk = pl.BlockSpec((None, tq, LANES), lambda p, c, i: (c, i, p))
    res = pl.BlockSpec((None, L + 2 * DIL_HALF, LANES), lambda p, c, i: (c, 0, p))
    tsp = pl.BlockSpec((2, tq, kw), lambda p, c, i: (p, 0, 0))
    return pl.pallas_call(
        body, name=name, grid=(npair, dil, L // tq),
        in_specs=[blk, res, res, tsp, blk, blk, blk], out_specs=[blk, res, res, tsp],
        out_shape=[_sds(qv.shape, F32), _sds(kv.shape, F32), _sds(kv.shape, F32), _sds(tab.shape, F32)],
        compiler_params=_cparams(3),
    )(qv, kv, vv, tab, dov, lsev, deltav)


def _mix_weights(l1, l2, l3):
    mx = jnp.maximum(jnp.maximum(l1, l2), l3)
    e1, e2, e3 = jnp.exp(l1 - mx), jnp.exp(l2 - mx), jnp.exp(l3 - mx)
    inv = 1.0 / (e1 + e2 + e3)
    return e1 * inv, e2 * inv, e3 * inv


def _branch_specs(S, ts):
    dils = [d for _, d in DIL_BRANCHES]
    return dils, [_res_spec(d, ts // d) for d in dils], [(d, S // d, DIL_W) for d in dils]


def _dil_mix_fwd(os, ls):
    S = os[0].shape[0] * os[0].shape[1]
    ts = min(256, S)
    dils, specs, _ = _branch_specs(S, ts)

    def body(o1, o2, o3, l1, l2, l3, out, scr):
        o1, o2, o3, l1, l2, l3 = [_by_token(r, scr, d) for r, d in zip((o1, o2, o3, l1, l2, l3), dils + dils)]
        w1, w2, w3 = _mix_weights(l1, l2, l3)
        out[...] = (w1 * o1 + w2 * o2 + w3 * o3).astype(CDT)

    return pl.pallas_call(
        body, name="dil_mix_fwd", grid=(S // ts,), in_specs=specs + specs, out_specs=_row(ts, DIL_W),
        out_shape=_sds((S, DIL_W), CDT), scratch_shapes=[_TOKEN_SCRATCH(ts)],
        compiler_params=_cparams(1),
    )(*os, *ls)


def _dil_mix_bwd(dcat, os, ls, j384):
    S = os[0].shape[0] * os[0].shape[1]
    ts = min(256, S)
    dils, specs, shapes = _branch_specs(S, ts)

    def body(do_ref, o1, o2, o3, l1, l2, l3, j_ref, d1, d2, d3, e1, e2, e3, scr):
        o1, o2, o3, l1, l2, l3 = [_by_token(r, scr, d) for r, d in zip((o1, o2, o3, l1, l2, l3), dils + dils)]
        ws = _mix_weights(l1, l2, l3)
        do = do_ref[...]
        o = ws[0] * o1 + ws[1] * o2 + ws[2] * o3
        dot = _headsum(do * o, j_ref[...])
        for w, d, d_o, e_o in zip(ws, dils, (d1, d2, d3), (e1, e2, e3)):
            _by_residue(w * do, scr, d_o, d)
            _by_residue(w * dot, scr, e_o, d)

    return pl.pallas_call(
        body, name="dil_mix_bwd", grid=(S // ts,),
        in_specs=[_row(ts, DIL_W, 1)] + specs + specs + [_full(j384.shape)],
        out_specs=specs + specs,
        out_shape=[_sds(s, CDT) for s in shapes] + [_sds(s, F32) for s in shapes],
        scratch_shapes=[_TOKEN_SCRATCH(ts)],
        compiler_params=_cparams(1),
    )(dcat, *os, *ls, j384)


def _adamw_math(w, g, m, v):
    m = ADAM_B1 * m + (1.0 - ADAM_B1) * g
    v = ADAM_B2 * v + (1.0 - ADAM_B2) * (g * g)
    m_hat = m / (1.0 - ADAM_B1 ** ADAM_STEP)
    v_hat = v / (1.0 - ADAM_B2 ** ADAM_STEP)
    delta = -ADAM_LR * (m_hat / (jnp.sqrt(v_hat) + ADAM_EPS) + ADAM_WD * w)
    return delta, m, v


def _pick8(n, target):
    best = None
    for t in range(16, min(n, target) + 1, 16):
        if n % t == 0:
            best = t
    return best if best is not None else n


_ELEMS_PER_BLOCK = 256 * 1024


def _adamw(w, reds, sibs, m, v, owner, name):
    L, a, b = w.shape
    ta = _pick8(a, max(16, _ELEMS_PER_BLOCK // b))
    spec = pl.BlockSpec((None, ta, b), lambda l, i, own: (l, i, 0))
    def gspec(mine, layer):
        def index(l, i, own):
            use = (own[0] if mine else 1 - own[0]) * (l if layer else 1 - l)
            return i * use, 0
        return pl.BlockSpec((ta, b), index)

    def body(own_ref, w_ref, r0_ref, r1_ref, s0_ref, s1_ref, m_ref, v_ref, g_o, d_o, m_o, v_o):
        mine = own_ref[0] == 1
        g0 = jnp.where(mine, r0_ref[...], s0_ref[...])
        g1 = jnp.where(mine, r1_ref[...], s1_ref[...])
        g = jnp.where(pl.program_id(0) == 0, g0, g1)
        d, mm, vv = _adamw_math(w_ref[...], g, m_ref[...], v_ref[...])
        g_o[...] = g
        d_o[...] = d
        m_o[...] = mm
        v_o[...] = vv

    return pl.pallas_call(
        body, name=name, out_shape=[_sds(w.shape, F32)] * 4,
        grid_spec=pltpu.PrefetchScalarGridSpec(
            num_scalar_prefetch=1, grid=(L, a // ta),
            in_specs=[spec, gspec(True, 0), gspec(True, 1), gspec(False, 0), gspec(False, 1), spec, spec],
            out_specs=[spec] * 4),
        compiler_params=_cparams(2),
    )(_is_core(owner), w, *reds, *sibs, m, v)


def _adamw_small(w, gall, m, v):
    R = w.shape[0]

    def body(w_ref, g_ref, m_ref, v_ref, g_o, d_o, m_o, v_o):
        g = g_ref[0]
        for k in range(1, 8):
            g = g + g_ref[k]
        d, mm, vv = _adamw_math(w_ref[...], g, m_ref[...], v_ref[...])
        g_o[...] = g
        d_o[...] = d
        m_o[...] = mm
        v_o[...] = vv

    vm = pl.BlockSpec(memory_space=pltpu.VMEM)
    return pl.pallas_call(
        body, name="adamw_small", in_specs=[vm] * 4, out_specs=[vm] * 4, out_shape=[_sds((R, LANES), F32)] * 4,
    )(w, gall, m, v)


def _sum_pair(g, t, owner, name):
    n, a, b = t.shape
    ta = _pick8(a, max(16, _ELEMS_PER_BLOCK // b))
    spec = pl.BlockSpec((None, ta, b), lambda k, i, own: (k * own[0], i * own[0], 0))

    def body(own_ref, g_ref, t_ref, o_ref):
        @pl.when(own_ref[0] == 1)
        def _():
            o_ref[...] = (g_ref[...].astype(F32) + t_ref[...].astype(F32)).astype(o_ref.dtype)

    return pl.pallas_call(
        body, name=name, out_shape=_sds(t.shape, WIRE),
        grid_spec=pltpu.PrefetchScalarGridSpec(num_scalar_prefetch=1, grid=(n, a // ta), in_specs=[spec] * 2,
                                               out_specs=spec),
        compiler_params=_cparams(2),
    )(_is_core(owner), g, t)


def _sum_chips(pair, t, owner, name):
    _, a, b = t.shape
    ta = _pick8(a, max(16, _ELEMS_PER_BLOCK // b))

    def body(own_ref, p_ref, t_ref, o_ref):
        @pl.when(own_ref[0] == 1)
        def _():
            me = 2 * lax.axis_index("x") + lax.axis_index("y")
            acc = p_ref[me].astype(F32)
            for k in range(3):
                acc = acc + t_ref[k].astype(F32)
            o_ref[...] = acc

    return pl.pallas_call(
        body, name=name, out_shape=_sds((a, b), F32),
        grid_spec=pltpu.PrefetchScalarGridSpec(
            num_scalar_prefetch=1, grid=(a // ta,),
            in_specs=[pl.BlockSpec((4, ta, b), lambda i, own: (0, i * own[0], 0)),
                      pl.BlockSpec((3, ta, b), lambda i, own: (0, i * own[0], 0))],
            out_specs=pl.BlockSpec((ta, b), lambda i, own: (i * own[0], 0))),
        compiler_params=_cparams(1),
    )(_is_core(owner), pair, t)


def _is_core(core):
    return (lax.axis_index("c") == core).astype(jnp.int32).reshape(1)


_HBM = pl.BlockSpec(memory_space=pltpu.HBM)


def _place():
    x, y, c = lax.axis_index("x"), lax.axis_index("y"), lax.axis_index("c")
    chips = [(1 - x, y), (x, 1 - y), (1 - x, 1 - y)]
    return x, y, c, chips


def _remote(src, dst, ssem, rsem, to):
    return pltpu.make_async_remote_copy(src_ref=src, dst_ref=dst, send_sem=ssem, recv_sem=rsem, device_id=to,
                                        device_id_type=MESH_ID)


def _dma_sems(n):
    return pltpu.SemaphoreType.DMA((n,))


_SEM = pl.BlockSpec(memory_space=pltpu.SEMAPHORE)
_ANY = pl.BlockSpec(memory_space=pl.ANY)
_EFFECT = pltpu.SideEffectType.DATAFLOW_SIDE_EFFECTING
_BIG = ("w_in", "mla_w_uq", "mla_w_ukv", "w_out", "ffn_w_gate", "ffn_w_up", "ffn_w_down")
_OWNER = dict(zip(_BIG, (1, 0, 0, 1, 0, 0, 1)))
_ATTN_WEIGHTS, _FFN_WEIGHTS = _BIG[:4], _BIG[4:]


def _hbm(a):
    return pltpu.with_memory_space_constraint(a, pltpu.HBM)


def _per_core(c, owners, fn):
    for g in range(2):
        mine = tuple(p for p, o in enumerate(owners) if o == g)
        theirs = tuple(p for p, o in enumerate(owners) if o != g)
        pl.when(c == g)(functools.partial(fn, mine, theirs))


def _token_spec():
    return pl.BlockSpec(memory_space=pltpu.VMEM), _sds((8, LANES), F32)


def _gather_start(shards, owners, layer, tag):
    n = len(shards)
    lands = [_hbm(lax.empty((4,) + s.shape[1:], s.dtype)) for s in shards]

    def body(*refs):
        w_refs, l_refs = refs[:n], refs[n:2 * n]
        ssem, rsem, token = refs[2 * n], refs[2 * n + 1], refs[-1]
        x, y, c, chips = _place()
        me = 2 * x + y

        def send(mine, _):
            for i in mine:
                for j, (cx, cy) in enumerate(chips):
                    _remote(w_refs[i].at[layer], l_refs[i].at[me], ssem.at[3 * i + j], rsem.at[3 * i + j],
                            (cx, cy, c)).start()

        _per_core(c, owners, send)
        token[...] = jnp.zeros_like(token)

    tspec, tshape = _token_spec()
    out = pl.pallas_call(
        body, name=f"gather_start_{tag}", in_specs=[_HBM] * (2 * n),
        out_specs=[_SEM, _SEM] + [_HBM] * n + [tspec],
        out_shape=[_dma_sems(3 * n), _dma_sems(3 * n)] + [pltpu.HBM(l.shape, l.dtype) for l in lands] + [tshape],
        input_output_aliases={n + i: 2 + i for i in range(n)},
        compiler_params=pltpu.CompilerParams(has_side_effects=_EFFECT),
    )(*[_hbm(s) for s in shards], *lands)
    return out[0], out[1], list(out[2:2 + n]), out[-1]


def _gather_wait(ssem, rsem, shards, lands, after, owners, layer, tag):
    n = len(shards)

    def body(*refs):
        w_refs, l_refs = refs[:n], refs[n:2 * n]
        ssem, rsem = refs[2 * n], refs[2 * n + 1]
        x, y, c, chips = _place()

        def wait(mine, _):
            for i in mine:
                for j, (cx, cy) in enumerate(chips):
                    cp = _remote(w_refs[i].at[layer], l_refs[i].at[2 * cx + cy], ssem.at[3 * i + j], rsem.at[3 * i + j],
                                 (cx, cy, c))
                    cp.wait_send()
                    cp.wait_recv()

        _per_core(c, owners, wait)

    return list(pl.pallas_call(
        body, name=f"gather_wait_{tag}", in_specs=[_HBM] * (2 * n) + [_SEM, _SEM, _ANY], out_specs=[_HBM] * n,
        out_shape=[pltpu.HBM(l.shape, l.dtype) for l in lands],
        input_output_aliases={n + i: i for i in range(n)},
        compiler_params=pltpu.CompilerParams(has_side_effects=_EFFECT),
    )(*[_hbm(s) for s in shards], *lands, ssem, rsem, after))


def _gather_finish(shards, lands, owners, layer, tag):
    n = len(shards)

    def body(*refs):
        w_refs, g_refs = refs[:n], refs[2 * n:3 * n]
        ssem, rsem = refs[3 * n:]
        x, y, c, chips = _place()
        me = 2 * x + y
        sib = (x, y, 1 - c)
        owns = [_remote(w.at[layer], g.at[me], ssem.at[i], rsem.at[i], sib) for i, (w, g) in enumerate(zip(w_refs, g_refs))]
        for cp in owns:
            cp.start()

        def forward(mine, theirs):
            def blk(i, j):
                b = g_refs[i].at[2 * chips[j][0] + chips[j][1]]
                return _remote(b, b, ssem.at[n + 3 * i + j], rsem.at[n + 3 * i + j], sib)

            for i in mine:
                for j in range(3):
                    blk(i, j).start()
            for i in theirs:
                for j in range(3):
                    blk(i, j).wait_recv()
            for i in mine:
                for j in range(3):
                    blk(i, j).wait_send()

        _per_core(c, owners, forward)
        for cp in owns:
            cp.wait_recv()
            cp.wait_send()

    return list(pl.pallas_call(
        body, name=f"gather_finish_{tag}", in_specs=[_HBM] * (2 * n), out_specs=[_HBM] * n,
        out_shape=[_sds(l.shape, l.dtype) for l in lands], input_output_aliases={n + i: i for i in range(n)},
        scratch_shapes=[_dma_sems(4 * n), _dma_sems(4 * n)],
    )(*shards, *lands))


def _rs_to_owner(grads, owners, tag):
    n = len(grads)

    def body(*refs):
        g_refs, t_refs = refs[:n], refs[n:2 * n]
        ssem, rsem = refs[2 * n:]
        x, y, c, _ = _place()

        def swap(mine, theirs):
            cps = [_remote(g_refs[i], t_refs[i], ssem.at[i], rsem.at[i], (x, y, 1 - c)) for i in theirs]
            for cp in cps:
                cp.start()
            for i in mine:
                _remote(g_refs[i], t_refs[i], ssem.at[i], rsem.at[i], (x, y, 1 - c)).wait_recv()
            for cp in cps:
                cp.wait_send()

        _per_core(c, owners, swap)

    return list(pl.pallas_call(
        body, name=f"rs_to_owner_{tag}", in_specs=[_HBM] * n, out_specs=[_HBM] * n,
        out_shape=[_sds(g.shape, g.dtype) for g in grads], scratch_shapes=[_dma_sems(n), _dma_sems(n)],
    )(*grads))


def _a2a_start(pairs, owners, tag):
    n = len(pairs)
    lands = [_hbm(lax.empty((3,) + p.shape[1:], p.dtype)) for p in pairs]

    def body(*refs):
        a_refs, t_refs = refs[:n], refs[n:2 * n]
        ssem, rsem, token = refs[2 * n], refs[2 * n + 1], refs[-1]
        x, y, c, chips = _place()

        def send(mine, _):
            for i in mine:
                for j, (cx, cy) in enumerate(chips):
                    _remote(a_refs[i].at[2 * cx + cy], t_refs[i].at[j], ssem.at[3 * i + j], rsem.at[3 * i + j],
                            (cx, cy, c)).start()

        _per_core(c, owners, send)
        token[...] = jnp.zeros_like(token)

    tspec, tshape = _token_spec()
    out = pl.pallas_call(
        body, name=f"rs_a2a_start_{tag}", in_specs=[_HBM] * (2 * n),
        out_specs=[_SEM, _SEM] + [_HBM] * n + [tspec],
        out_shape=[_dma_sems(3 * n), _dma_sems(3 * n)] + [pltpu.HBM(l.shape, l.dtype) for l in lands] + [tshape],
        input_output_aliases={n + i: 2 + i for i in range(n)},
        compiler_params=pltpu.CompilerParams(has_side_effects=_EFFECT),
    )(*[_hbm(p) for p in pairs], *lands)
    return out[0], out[1], list(out[2:2 + n]), out[-1]


def _a2a_wait(ssem, rsem, pairs, lands, after, owners, tag):
    n = len(pairs)

    def body(*refs):
        a_refs, t_refs = refs[:n], refs[n:2 * n]
        ssem, rsem = refs[2 * n], refs[2 * n + 1]
        x, y, c, chips = _place()

        def wait(mine, _):
            for i in mine:
                for j, (cx, cy) in enumerate(chips):
                    cp = _remote(a_refs[i].at[2 * cx + cy], t_refs[i].at[j], ssem.at[3 * i + j], rsem.at[3 * i + j],
                                 (cx, cy, c))
                    cp.wait_send()
                    cp.wait_recv()

        _per_core(c, owners, wait)

    return list(pl.pallas_call(
        body, name=f"rs_a2a_wait_{tag}", in_specs=[_HBM] * (2 * n) + [_SEM, _SEM, _ANY], out_specs=[_HBM] * n,
        out_shape=[pltpu.HBM(l.shape, l.dtype) for l in lands],
        input_output_aliases={n + i: i for i in range(n)},
        compiler_params=pltpu.CompilerParams(has_side_effects=_EFFECT),
    )(*[_hbm(p) for p in pairs], *lands, ssem, rsem, after))


def _rs_from_owner(reds, owners):
    n = len(reds)

    def body(*refs):
        q_refs, o_refs = refs[:n], refs[n:2 * n]
        ssem, rsem = refs[2 * n:]
        x, y, c, _ = _place()

        def swap(mine, theirs):
            cps = [_remote(q_refs[k], o_refs[k], ssem.at[k], rsem.at[k], (x, y, 1 - c)) for k in mine]
            for cp in cps:
                cp.start()
            for k in theirs:
                _remote(q_refs[k], o_refs[k], ssem.at[k], rsem.at[k], (x, y, 1 - c)).wait_recv()
            for cp in cps:
                cp.wait_send()

        _per_core(c, owners, swap)

    return list(pl.pallas_call(
        body, name="rs_from_owner", in_specs=[_HBM] * n, out_specs=[_HBM] * n,
        out_shape=[_sds(q.shape, q.dtype) for q in reds], scratch_shapes=[_dma_sems(n), _dma_sems(n)],
    )(*reds))


def _gather_small(s):
    R, _ = s.shape

    def body(s_ref, o_ref, ssem, rsem, lsem):
        x, y, c, _ = _place()
        me = 4 * x + 2 * y + c
        own = pltpu.make_async_copy(s_ref, o_ref.at[me], lsem)
        own.start()
        sends = []
        for k in range(1, 8):
            px, py, pc = x ^ (k >> 2), y ^ ((k >> 1) & 1), c ^ (k & 1)
            cp = _remote(s_ref, o_ref.at[me], ssem.at[k - 1], rsem.at[k - 1], (px, py, pc))
            cp.start()
            sends.append(cp)
        for k in range(1, 8):
            px, py, pc = x ^ (k >> 2), y ^ ((k >> 1) & 1), c ^ (k & 1)
            blk = o_ref.at[4 * px + 2 * py + pc]
            _remote(blk, blk, ssem.at[k - 1], rsem.at[k - 1], (px, py, pc)).wait_recv()
        for cp in sends:
            cp.wait_send()
        own.wait()

    vm = pl.BlockSpec(memory_space=pltpu.VMEM)
    return pl.pallas_call(
        body, name="gather_small", in_specs=[vm], out_specs=vm, out_shape=_sds((8, R, LANES), s.dtype),
        scratch_shapes=[pltpu.SemaphoreType.DMA((7,)), pltpu.SemaphoreType.DMA((7,)), pltpu.SemaphoreType.DMA],
    )(s)


_COL_SHARDED =("w_in", "mla_w_uq", "mla_w_ukv", "ffn_w_gate", "ffn_w_up")
_SMALL = ("mla_q_norm", "mla_kv_norm", "gqa_q_norm", "gqa_k_norm", "rel_bias", "ln1_g", "ln1_b", "ln2_g", "ln2_b")


def _pack_flat(arrs, align):
    flat = jnp.concatenate([a.reshape(-1) for a in arrs])
    pad = (-flat.shape[0]) % align
    return jnp.pad(flat, (0, pad)) if pad else flat


def _unpack_flat(flat, shapes):
    out, off = [], 0
    for s in shapes:
        n = int(np.prod(s))
        out.append(flat[off:off + n].reshape(s))
        off += n
    return out


def _perm_gqa_rows(w):
    return jnp.concatenate([w[:832], w[896:960], w[832:896], w[960:]], axis=0)


def _local_step(x, target, small, depth, weights_of_layer, grads_done):
    S, D = x.shape
    alpha = (2.0 * depth) ** 0.25
    in_idx, uq_idx, ukv_idx = _in_cols(), _uq_cols(), _ukv_cols()
    win, wuq, wukv, wout, wg, wu, wdn = ([None] * depth for _ in range(7))

    tm, tg = _rope_tables(S)
    j256, j384 = _head_ones(256), _head_ones(384)
    mla_scale = (64 + MLA_ROPE_DIM) ** -0.5
    branches = []
    for (_, dil) in DIL_BRANCHES:
        L = S // dil
        tq = min(256, L)
        idx = jnp.asarray(_branch_bucket_idx(tq, dil))
        branches.append((dil, L, tq, idx))
    tabs = [_bias_expand(idx, small["rel_bias"], name=f"bias_expand_{b}") for b, (_, _, _, idx) in enumerate(branches)]

    def padded(a):
        z = jnp.zeros((DIL_HALF, a.shape[1]), a.dtype)
        return jnp.concatenate([z, a, z], axis=0)[None]

    saved = []
    xf, xb = x, x.astype(CDT)
    for l in range(depth):
        W, token = weights_of_layer(l, "attn", xb)
        win[l] = _rows_from_shards(W["w_in"], in_idx)
        wuq[l] = _rows_from_shards(W["mla_w_uq"], uq_idx)
        wukv[l] = _rows_from_shards(W["mla_w_ukv"], ukv_idx)
        wout[l] = _perm_gqa_rows(W["w_out"].reshape(-1, D))
        gq, gkv = small["mla_q_norm"][l][None], small["mla_kv_norm"][l][None]
        if token is not None:
            gq = gq + token[0, 0]
        ggq = jnp.tile(small["gqa_q_norm"][l], 4)[None]
        ggk = jnp.tile(small["gqa_k_norm"][l], 2)[None]
        h = _mm(xb, win[l], tb=True, name="mm_in")
        cq, ckv, kr, qd, kd, vd, qg, kg, vg, *strided = _prep_fwd(h, gq, gkv, ggq, ggk, tm, tg, j256)
        qkv = [(qd[None], padded(kd), padded(vd))] + [tuple(strided[3 * b:3 * b + 3]) for b in range(len(DIL_STRIDES))]
        qa = _mm(cq, wuq[l], tb=True, name="mm_uq")
        kvp = _mm(ckv, wukv[l], tb=True, out_dtype=CDT, name="mm_ukv")
        qm, km = _mla_prep_fwd(qa, kvp, kr, tm, mla_scale)
        oa, lsa = _attn_fwd(qm, km, kvp, split=True, npairs=3, kblk=lambda p: p, vblk=lambda p: 6 + p,
                            name="mla_attn_fwd")
        oc, lsc = _attn_fwd(qg, kg, vg, split=False, npairs=2, kblk=lambda p: 0, vblk=lambda p: 0,
                            name="gqa_attn_fwd")
        obs, lbs = [], []
        for b, (dil, L, tq, _) in enumerate(branches):
            o_b, l_b = _dil_fwd(*qkv[b], tabs[b], dil=dil, L=L, tq=tq, name=f"dil_fwd_{b}")
            obs.append(o_b)
            lbs.append(l_b)
        ob = _dil_mix_fwd(obs, lbs)
        cat = jnp.concatenate([oa, ob, oc], axis=1)
        mix = _mm(cat, wout[l], name="mm_out")
        x1, x1b, z1 = _ln_fwd(xf, mix, small["ln1_g"][l][None], small["ln1_b"][l][None], alpha, name="ln1_fwd")
        W, _ = weights_of_layer(l, "ffn", x1b)
        wg[l], wu[l], wdn[l] = W["ffn_w_gate"], W["ffn_w_up"], W["ffn_w_down"]
        g3, u3, act = _ffn_up(x1b, wg[l], wu[l])
        ff = _ffn_down(act, wdn[l])
        x2, x2b, z2 = _ln_fwd(x1, ff, small["ln2_g"][l][None], small["ln2_b"][l][None], alpha, name="ln2_fwd")
        saved.append(dict(xb=xb, h=h, cq=cq, ckv=ckv, qg=qg, kg=kg, vg=vg, kvp=kvp, qm=qm, km=km, oa=oa, lsa=lsa,
                          oc=oc, lsc=lsc, obs=obs, lbs=lbs, qkv=qkv, cat=cat, z1=z1, x1b=x1b, g3=g3, u3=u3, act=act, z2=z2,
                          gq=gq, gkv=gkv, ggq=ggq, ggk=ggk))
        xf, xb = x2, x2b

    gW = {k: [None] * depth for k in _BIG}
    gS = {k: [None] * depth for k in ("mla_q_norm", "mla_kv_norm", "gqa_q_norm", "gqa_k_norm", "ln1_g", "ln1_b", "ln2_g",
                                      "ln2_b")}
    g_rel = None
    dya, dyb = xf, target
    token = None
    for l in reversed(range(depth)):
        sv = saved[l]
        ln2_g = small["ln2_g"][l][None]
        if token is not None:
            ln2_g = ln2_g + token[0, 0]
        if l == depth - 1:
            dz2, dz2b, gS["ln2_g"][l], gS["ln2_b"][l], loss = _ln_bwd(dya, dyb, sv["z2"], ln2_g, alpha,
                                                                       name="ln2_bwd_loss", loss_head=True)
        else:
            dz2, dz2b, gS["ln2_g"][l], gS["ln2_b"][l] = _ln_bwd(dya, dyb, sv["z2"], ln2_g, alpha, name="ln2_bwd")
        gW["ffn_w_down"][l] = _mm(sv["act"], dz2b, ta=True, ga=True, go=True, out_dtype=WIRE, name="mm_down_dw")
        dg3, du3 = _ffn_down_dx(dz2b, wdn[l], sv["g3"], sv["u3"])
        gW["ffn_w_gate"][l] = _mm(dg3, sv["x1b"], ta=True, ga=True, go=True, out_dtype=WIRE, name="mm_gate_dw")
        gW["ffn_w_up"][l] = _mm(du3, sv["x1b"], ta=True, ga=True, go=True, out_dtype=WIRE, name="mm_up_dw")
        dx1 = _ffn_up_dx(dg3, du3, wg[l], wu[l])
        token = grads_done(l, "ffn", {n: gW[n][l] for n in _FFN_WEIGHTS})
        ln1_g = small["ln1_g"][l][None]
        if token is not None:
            ln1_g = ln1_g + token[0, 0]
        dz1, dz1b, gS["ln1_g"][l], gS["ln1_b"][l] = _ln_bwd(dx1, dz2, sv["z1"], ln1_g, alpha, name="ln1_bwd")
        gW["w_out"][l] = _perm_gqa_rows(_mm(sv["cat"], dz1b, ta=True, out_dtype=WIRE, name="mm_out_dw")).reshape(4, -1, D)
        dcat = _mm(dz1b, wout[l], tb=True, name="mm_out_dx")
        dqg, dkg, dvg = _attn_bwd(sv["qg"], sv["kg"], sv["vg"], dcat, sv["oc"], sv["lsc"], split=False, npairs=2,
                                  kblk=lambda p: 0, vblk=lambda p: 0, doblk=lambda p: 6 + p, shared_kv=True,
                                  name="gqa_attn_bwd")
        dqm, dkm, dvm = _attn_bwd(sv["qm"], sv["km"], sv["kvp"], dcat, sv["oa"], sv["lsa"], split=True, npairs=3,
                                  kblk=lambda p: p, vblk=lambda p: 6 + p, doblk=lambda p: p, shared_kv=False,
                                  name="mla_attn_bwd")
        dqa, dkvp, dkr = _mla_prep_bwd(dqm, dkm, dvm, tm, mla_scale)
        gW["mla_w_uq"][l] = _rows_to_shards(_mm(dqa, sv["cq"], ta=True, out_dtype=WIRE, name="mm_uq_dw"), uq_idx, MLA_HEADS * 96)
        dcq = _mm(dqa, wuq[l], name="mm_uq_dx")
        gW["mla_w_ukv"][l] = _rows_to_shards(_mm(dkvp, sv["ckv"], ta=True, out_dtype=WIRE, name="mm_ukv_dw"), ukv_idx, MLA_HEADS * 128)
        dckv = _mm(dkvp, wukv[l], name="mm_ukv_dx")
        mixb = _dil_mix_bwd(dcat, sv["obs"], sv["lbs"], j384)
        ddq, ddk, ddv = [], [], []
        for b, (dil, L, tq, idx) in enumerate(branches):
            dq_b, dk_b, dv_b, dtab = _dil_bwd(*sv["qkv"][b], tabs[b], mixb[b], sv["lbs"][b], mixb[3 + b], dil=dil, L=L,
                                              tq=tq, name=f"dil_bwd_{b}")
            if dil == 1:
                dq_b, dk_b, dv_b = dq_b[0], dk_b[0, DIL_HALF:DIL_HALF + S], dv_b[0, DIL_HALF:DIL_HALF + S]
            ddq.append(dq_b)
            ddk.append(dk_b)
            ddv.append(dv_b)
            g_b = _bias_reduce(idx, dtab, name=f"bias_reduce_{b}")[:, :, 0].T
            g_rel = g_b if g_rel is None else g_rel + g_b
        dh, n1, n2, n3, n4 = _prep_bwd(sv["h"], dcq, dckv, dkr, ddq, ddk, ddv, dqg, dkg, dvg, sv["gq"], sv["gkv"],
                                       sv["ggq"], sv["ggk"], tg, j256)
        gS["mla_q_norm"][l], gS["mla_kv_norm"][l] = n1[0], n2[0]
        gS["gqa_q_norm"][l] = n3[0].reshape(4, 64).sum(0)
        gS["gqa_k_norm"][l] = n4[0].reshape(2, 64).sum(0)
        gW["w_in"][l] = _rows_to_shards(_mm(dh, sv["xb"], ta=True, out_dtype=WIRE, name="mm_in_dw"), in_idx, IN_W)
        dya = _mm(dh, win[l], name="mm_in_dx")
        dyb = dz1
        token = grads_done(l, "attn", {n: gW[n][l] for n in _ATTN_WEIGHTS})
    grad_x = _axpy(dya, dyb, alpha, name="grad_x")

    gsmall = {k: jnp.stack([a.reshape(-1) for a in v]) for k, v in gS.items()}
    gsmall["rel_bias"] = g_rel
    return loss, grad_x, gsmall


_ORDER = ("w_in", "mla_q_norm", "mla_kv_norm", "mla_w_uq", "mla_w_ukv", "gqa_q_norm", "gqa_k_norm", "rel_bias", "w_out",
          "ln1_g", "ln1_b", "ffn_w_gate", "ffn_w_up", "ffn_w_down", "ln2_g", "ln2_b")


def kernel(x, w_in, mla_q_norm, mla_kv_norm, mla_w_uq, mla_w_ukv, gqa_q_norm, gqa_k_norm, rel_bias, w_out, ln1_g, ln1_b, ffn_w_gate, ffn_w_up, ffn_w_down, ln2_g, ln2_b, loss_target, m_w_in, m_mla_q_norm, m_mla_kv_norm, m_mla_w_uq, m_mla_w_ukv, m_gqa_q_norm, m_gqa_k_norm, m_rel_bias, m_w_out, m_ln1_g, m_ln1_b, m_ffn_w_gate, m_ffn_w_up, m_ffn_w_down, m_ln2_g, m_ln2_b, v_w_in, v_mla_q_norm, v_mla_kv_norm, v_mla_w_uq, v_mla_w_ukv, v_gqa_q_norm, v_gqa_k_norm, v_rel_bias, v_w_out, v_ln1_g, v_ln1_b, v_ffn_w_gate, v_ffn_w_up, v_ffn_w_down, v_ln2_g, v_ln2_b):
    wts = dict(zip(_ORDER, (w_in, mla_q_norm, mla_kv_norm, mla_w_uq, mla_w_ukv, gqa_q_norm, gqa_k_norm, rel_bias, w_out,
                            ln1_g, ln1_b, ffn_w_gate, ffn_w_up, ffn_w_down, ln2_g, ln2_b)))
    mom = dict(zip(_ORDER, (m_w_in, m_mla_q_norm, m_mla_kv_norm, m_mla_w_uq, m_mla_w_ukv, m_gqa_q_norm, m_gqa_k_norm,
                            m_rel_bias, m_w_out, m_ln1_g, m_ln1_b, m_ffn_w_gate, m_ffn_w_up, m_ffn_w_down, m_ln2_g,
                            m_ln2_b)))
    var = dict(zip(_ORDER, (v_w_in, v_mla_q_norm, v_mla_kv_norm, v_mla_w_uq, v_mla_w_ukv, v_gqa_q_norm, v_gqa_k_norm,
                            v_rel_bias, v_w_out, v_ln1_g, v_ln1_b, v_ffn_w_gate, v_ffn_w_up, v_ffn_w_down, v_ln2_g,
                            v_ln2_b)))
    small_shapes = [wts[n].shape for n in _SMALL]
    for d in (wts, mom, var):
        for n in _COL_SHARDED:
            d[n] = d[n].transpose(0, 2, 1)

    depth = 2
    shards = {n: wts[n].astype(WIRE) for n in _BIG}
    flying = {}

    def start_gather(names, l, tag):
        own = tuple(_OWNER[n] for n in names)
        sh = [shards[n] for n in names]
        ssem, rsem, lands, token = _gather_start(sh, own, l, tag)
        return (names, own, sh, ssem, rsem, lands, l, tag), token

    def end_gather(flight, after):
        names, own, sh, ssem, rsem, lands, l, tag = flight
        got = _gather_finish(sh, _gather_wait(ssem, rsem, sh, lands, after, own, l, tag), own, l, tag)
        return {n: g.astype(CDT) for n, g in zip(names, got)}

    def weights_of_layer(l, part, after):
        if (l, part) == (0, "attn"):
            got = end_gather(start_gather(_ATTN_WEIGHTS, 0, "attn0")[0], after)
            flying["ffn0"], t0 = start_gather(_FFN_WEIGHTS, 0, "ffn0")
            flying["layer1"], t1 = start_gather(_BIG, 1, "layer1")
            return got, t0 + t1
        if (l, part) == (0, "ffn"):
            return end_gather(flying.pop("ffn0"), after), None
        if part == "attn":
            flying["w1"] = end_gather(flying.pop("layer1"), after)
        return flying["w1"], None

    def grads_done(l, part, grads):
        names = tuple(grads)
        own = tuple(_OWNER[n] for n in names)
        tag = f"{part}{l}"
        gl = [grads[n] for n in names]
        theirs = _rs_to_owner(gl, own, tag)
        pairs = [_sum_pair(g, t, o, name=f"rs_pair_sum_{n}") for n, g, t, o in zip(names, gl, theirs, own)]
        ssem, rsem, lands, token = _a2a_start(pairs, own, tag)
        flying[tag] = (names, own, ssem, rsem, pairs, lands)
        return token

    small = {n: wts[n] for n in _SMALL}
    loss, grad_x, gsmall = _local_step(x[0], loss_target[0], small, depth, weights_of_layer, grads_done)

    reds = {}
    for l in reversed(range(depth)):
        for part in ("ffn", "attn"):
            tag = f"{part}{l}"
            names, own, ssem, rsem, pairs, lands = flying.pop(tag)
            got = _a2a_wait(ssem, rsem, pairs, lands, grad_x, own, tag)
            for n, p, t, o in zip(names, pairs, got, own):
                reds[n, l] = _sum_chips(p, t, o, name=f"rs_sum_chips_{n}")
    order = [(n, l) for l in range(depth) for n in _BIG]
    sibs = dict(zip(order, _rs_from_owner([reds[k] for k in order], tuple(_OWNER[n] for n, _ in order))))

    sflat = _pack_flat([gsmall[n].reshape(-1) for n in _SMALL], 8 * LANES)
    rs = sflat.shape[0] // LANES
    sall = _gather_small(sflat.reshape(rs, LANES))

    def packed(d):
        return _pack_flat([d[n] for n in _SMALL], 8 * LANES).reshape(rs, LANES)

    outs = {tag: {} for tag in ("grad", "delta", "new_m", "new_v")}
    for n in _BIG:
        res = _adamw(wts[n], [reds[n, 0], reds[n, 1]], [sibs[n, 0], sibs[n, 1]], mom[n], var[n], _OWNER[n],
                     name=f"adamw_{n}")
        for tag, r in zip(("grad", "delta", "new_m", "new_v"), res):
            outs[tag][n] = r.transpose(0, 2, 1) if n in _COL_SHARDED else r
    for tag, smallflat in zip(("grad", "delta", "new_m", "new_v"), _adamw_small(packed(wts), sall, packed(mom), packed(var))):
        outs[tag].update(zip(_SMALL, _unpack_flat(smallflat.reshape(-1), small_shapes)))

    total = lax.psum(loss[0, 0], ("x", "y", "c"))
    return (total, grad_x[None], *[outs["grad"][n] for n in _ORDER], *[outs["delta"][n] for n in _ORDER],
            *[outs["new_m"][n] for n in _ORDER], *[outs["new_v"][n] for n in _ORDER])
```

```python
import functools
import math

import numpy as np
import jax
import jax.numpy as jnp
from jax import lax
from jax.experimental import pallas as pl
from jax.experimental.pallas import tpu as pltpu

F32 = jnp.float32
CDT = jnp.bfloat16
WIRE = jnp.bfloat16

HEAD_DIM = 64
GRID_W = 64
ROPE_THETA = 10000.0
MLA_HEADS = 6
MLA_Q_RANK = 256
MLA_KV_RANK = 128
MLA_ROPE_DIM = 32
DIL_HEADS = 6
DIL_BRANCHES = ((128, 1), (512, 4), (2048, 16))
DIL_HALF = 64
GQA_Q_HEADS = 4
REL_BUCKETS = 32
REL_MAX_DIST = 1024
NEG_INF = -1e30
LANES = 128
VMEM_LIMIT = 56 * 1024 * 1024

ADAM_LR, ADAM_B1, ADAM_B2, ADAM_EPS, ADAM_WD, ADAM_STEP = 0.001, 0.9, 0.999, 1e-08, 0.01, 10

C_CQ, C_CKV, C_KR, C_DQ, C_DK, C_DV, C_GQ, C_GK, C_GV, IN_P = 0, 256, 384, 512, 896, 1280, 1664, 1920, 2048, 2176
IN_W = 2080
MESH_ID = pl.DeviceIdType.MESH


def _cparams(n_axes, vmem=VMEM_LIMIT):
    return pltpu.CompilerParams(dimension_semantics=("arbitrary",) * n_axes, vmem_limit_bytes=vmem)


MAX_WHOLE_DIM = 2304
WHOLE_K_WINDOW_BYTES = 36 * 1024 * 1024


def _pick(n, target):
    best = None
    for t in range(LANES, min(n, target) + 1, LANES):
        if n % t == 0:
            best = t
    if best is not None and (2 * best >= target or n > MAX_WHOLE_DIM):
        return best
    return n


def _sds(shape, dtype):
    return jax.ShapeDtypeStruct(tuple(shape), dtype)


def _in_cols():
    idx = -np.ones((IN_P,), np.int64)
    idx[C_CQ:C_CQ + 256] = np.arange(0, 256)
    idx[C_CKV:C_CKV + 128] = np.arange(256, 384)
    idx[C_KR + 64:C_KR + 96] = np.arange(384, 416)
    idx[C_DQ:C_DQ + 1152] = np.arange(416, 1568)
    gq = 1568 + (np.array([0, 2, 1, 3])[:, None] * 64 + np.arange(64)[None, :]).reshape(-1)
    idx[C_GQ:C_GQ + 256] = gq
    idx[C_GK:C_GK + 256] = np.arange(1824, 2080)
    return idx


def _uq_cols():
    idx = -np.ones((MLA_HEADS * 128,), np.int64)
    for h in range(MLA_HEADS):
        idx[h * 128:h * 128 + 96] = np.arange(96 * h, 96 * h + 96)
    return idx


def _ukv_cols():
    idx = -np.ones((MLA_HEADS * 128 + MLA_HEADS * 64,), np.int64)
    for h in range(MLA_HEADS):
        idx[h * 128:h * 128 + 64] = np.arange(128 * h, 128 * h + 64)
        idx[768 + h * 64:768 + h * 64 + 64] = np.arange(128 * h + 64, 128 * h + 128)
    return idx


def _runs(idx):
    out, i = [], 0
    while i < len(idx):
        j = i + 1
        while j < len(idx) and ((idx[i] < 0 and idx[j] < 0) or (idx[i] >= 0 and idx[j] == idx[j - 1] + 1)):
            j += 1
        out.append((int(idx[i]), j - i))
        i = j
    return out


def _rows_from_shards(sh, idx):
    _, cs, r = sh.shape
    pieces = []
    for first, ln in _runs(idx):
        if first < 0:
            pieces.append(jnp.zeros((ln, r), sh.dtype))
            continue
        while ln > 0:
            k, off = divmod(first, cs)
            take = min(ln, cs - off)
            pieces.append(sh[k, off:off + take, :])
            first, ln = first + take, ln - take
    return jnp.concatenate(pieces, axis=0)


def _rows_to_shards(wp, idx, n):
    inv = np.zeros((n,), np.int64)
    pos = np.nonzero(idx >= 0)[0]
    inv[idx[pos]] = pos
    cs = n // 4
    shards = []
    for k in range(4):
        pieces = [wp[first:first + ln, :] for first, ln in _runs(inv[k * cs:(k + 1) * cs])]
        shards.append(jnp.concatenate(pieces, axis=0))
    return jnp.stack(shards)


def _t5_bucket_np(rel):
    nb = REL_BUCKETS // 2
    exact = nb // 2
    ret = np.where(rel > 0, nb, 0)
    n = np.abs(rel)
    nf = np.maximum(n, 1).astype(np.float32)
    large = exact + (np.log(nf / np.float32(exact)) / np.float32(math.log(REL_MAX_DIST / exact))
                     * np.float32(nb - exact)).astype(np.int32)
    large = np.minimum(large, nb - 1)
    return ret + np.where(n < exact, n, large)


def _branch_bucket_idx(tq, dil):
    kw = tq + 2 * DIL_HALF
    rel = np.arange(kw)[None, :] - DIL_HALF - np.arange(tq)[:, None]
    idx = _t5_bucket_np(rel * dil)
    return np.where(np.abs(rel) <= DIL_HALF, idx, -1).astype(np.int32)


def _rope_tables(S):
    inv = ROPE_THETA ** (-jnp.arange(0, 32, 2, dtype=F32) / 32)
    t = jnp.arange(S)
    pos = t.astype(F32)
    row = (t // GRID_W).astype(F32)
    col = (t % GRID_W).astype(F32)
    lane = np.arange(LANES)
    wm = lane - 64
    is_rope = (lane >= 64) & (lane < 96)
    ang = pos[:, None] * inv[np.where(is_rope, wm % 16, 0)][None, :]
    cm = jnp.where(is_rope[None], jnp.cos(ang), 1.0)
    smm = jnp.where((is_rope & (wm < 16))[None], -jnp.sin(ang), 0.0)
    spm = jnp.where((is_rope & (wm >= 16))[None], jnp.sin(ang), 0.0)
    g = lane % 64
    w = g % 32
    angg = jnp.where((g < 32)[None], row[:, None], col[:, None]) * inv[w % 16][None, :]
    cg = jnp.cos(angg)
    smg = jnp.where((w < 16)[None], -jnp.sin(angg), 0.0)
    spg = jnp.where((w >= 16)[None], jnp.sin(angg), 0.0)
    return (cm, smm, spm), (cg, smg, spg)


def _lanes(t, width):
    return t if width == LANES else jnp.concatenate([t] * (width // LANES), axis=1)


def _rope(x, tabs):
    c, sm, sp = (_lanes(t, x.shape[1]) for t in tabs)
    w = x.shape[1]
    return x * c + pltpu.roll(x, w - 16, 1) * sm + pltpu.roll(x, 16, 1) * sp


def _rope_t(dy, tabs):
    c, sm, sp = (_lanes(t, dy.shape[1]) for t in tabs)
    w = dy.shape[1]
    return dy * c + pltpu.roll(dy * sm, 16, 1) + pltpu.roll(dy * sp, w - 16, 1)


def _head_ones(width):
    i = np.arange(width)
    return jnp.asarray((i[:, None] // HEAD_DIM == i[None, :] // HEAD_DIM).astype(np.float32))


def _headsum(x, j):
    return jnp.dot(x, j, preferred_element_type=F32, precision=lax.Precision.HIGHEST)


def _mm(a, b, *, ta=False, tb=False, ga=False, gb=False, go=False, out_dtype=F32, name):
    G = a.shape[0] if ga else (b.shape[0] if gb else 1)
    a2 = a.shape[1:] if ga else a.shape
    b2 = b.shape[1:] if gb else b.shape
    K, M = a2 if ta else a2[::-1]
    N = b2[0] if tb else b2[1]
    assert (b2[1] if tb else b2[0]) == K
    tm, tn, tk = _pick(M, 1024), _pick(N, 1024), _pick(K, 2048)
    if tm * tn > 1024 * 1152:
        tm = _pick(M, 512)
    if 2 * K * (tm + tn) * jnp.dtype(CDT).itemsize <= WHOLE_K_WINDOW_BYTES:
        tk = K
    nk = K // tk
    steps = nk if (go or G == 1) else G * nk
    dn = (((0 if ta else 1,), (1 if tb else 0,)), ((), ()))

    def body(a_ref, b_ref, o_ref, *acc):
        part = lax.dot_general(a_ref[...], b_ref[...], dn, preferred_element_type=F32)
        if steps == 1:
            o_ref[...] = part.astype(o_ref.dtype)
            return
        acc_ref, = acc
        s = pl.program_id(3)

        @pl.when(s == 0)
        def _():
            acc_ref[...] = part

        @pl.when(s > 0)
        def _():
            acc_ref[...] += part

        @pl.when(s == steps - 1)
        def _():
            o_ref[...] = acc_ref[...].astype(o_ref.dtype)

    def grp(g, s):
        return g if go else s // nk

    def kk(s):
        return s if steps == nk else s % nk

    def spec(grouped, block, index):
        if grouped:
            return pl.BlockSpec((None,) + block, lambda g, i, j, s: (grp(g, s),) + index(i, j, s))
        return pl.BlockSpec(block, lambda g, i, j, s: index(i, j, s))

    a_spec = (spec(ga, (tk, tm), lambda i, j, s: (kk(s), i)) if ta else spec(ga, (tm, tk), lambda i, j, s: (i, kk(s))))
    b_spec = (spec(gb, (tn, tk), lambda i, j, s: (j, kk(s))) if tb else spec(gb, (tk, tn), lambda i, j, s: (kk(s), j)))
    o_spec = spec(go, (tm, tn), lambda i, j, s: (i, j))
    return pl.pallas_call(
        body, name=name, grid=(G if go else 1, M // tm, N // tn, steps),
        in_specs=[a_spec, b_spec], out_specs=o_spec,
        out_shape=_sds(((G,) if go else ()) + (M, N), out_dtype),
        scratch_shapes=[pltpu.VMEM((tm, tn), F32)] if steps > 1 else [],
        compiler_params=_cparams(4),
    )(a, b)


def _row(ts, w, cb=0):
    return pl.BlockSpec((ts, w), lambda i: (i, cb))


def _full(shape):
    nd = len(shape)
    return pl.BlockSpec(tuple(shape), lambda i: (0,) * nd)


def _rms_fwd(x, g, eps=1e-6):
    r = lax.rsqrt(jnp.mean(x * x, axis=-1, keepdims=True) + eps)
    return x * r * g


def _rms_bwd(x, g, dy, eps=1e-6):
    r = lax.rsqrt(jnp.mean(x * x, axis=-1, keepdims=True) + eps)
    gdy = g * dy
    dx = r * gdy - x * (r * r * r) * jnp.mean(x * gdy, axis=-1, keepdims=True)
    return dx, x * r * dy


def _rms_head_fwd(x, g, j, eps=1e-6):
    r = lax.rsqrt(_headsum(x * x, j) * (1.0 / HEAD_DIM) + eps)
    return x * r * g


def _rms_head_bwd(x, g, dy, j, eps=1e-6):
    r = lax.rsqrt(_headsum(x * x, j) * (1.0 / HEAD_DIM) + eps)
    gdy = g * dy
    dx = r * gdy - x * (r * r * r) * (_headsum(x * gdy, j) * (1.0 / HEAD_DIM))
    return dx, x * r * dy


DIL_STRIDES = tuple(d for _, d in DIL_BRANCHES if d > 1)
DIL_W = DIL_HEADS * HEAD_DIM


def _res_spec(d, n, pad_blocks=0):
    return pl.BlockSpec((d, n, DIL_W), lambda i: (0, i + pad_blocks, 0))


def _prep_fwd(h, gq, gkv, ggq, ggk, tm, tg, j256):
    S = h.shape[0]
    ts = min(256, S)
    scale = HEAD_DIM ** -0.5
    nres = len(DIL_STRIDES)

    def body(*refs):
        (h_ref, gq_ref, gkv_ref, ggq_ref, ggk_ref, cm, smm, spm, cg, smg, spg, j_ref), refs = refs[:12], refs[12:]
        refs = refs[2 * nres:]
        (cq_o, ckv_o, kr_o, dq_o, dk_o, dv_o, gq_o, gk_o, gv_o), res_o = refs[:9], refs[9:-1]
        st = refs[-1]
        tabm = (cm[...], smm[...], spm[...])
        tabg = (cg[...], smg[...], spg[...])
        cq_o[...] = _rms_fwd(h_ref[:, C_CQ:C_CQ + 256], gq_ref[...]).astype(CDT)
        ckv_o[...] = _rms_fwd(h_ref[:, C_CKV:C_CKV + 128], gkv_ref[...]).astype(CDT)
        kr_o[...] = _rope(h_ref[:, C_KR:C_KR + 128], tabm).astype(CDT)
        dq_o[...] = (h_ref[:, C_DQ:C_DQ + 384] * scale).astype(CDT)
        dk_o[...] = h_ref[:, C_DK:C_DK + 384].astype(CDT)
        dv_o[...] = h_ref[:, C_DV:C_DV + 384].astype(CDT)
        for j, lanes in _lane_blocks(3 * DIL_W):
            st[j] = h_ref[:, C_DQ + lanes.start:C_DQ + lanes.stop] * (scale if j < 3 else 1.0)
        for bi, d in enumerate(DIL_STRIDES):
            for c in range(d):
                rows = pl.ds(c, ts // d, stride=d)
                for j, lanes in _lane_blocks(3 * DIL_W):
                    res_o[3 * bi + j // 3][c, :, (j % 3) * LANES:(j % 3 + 1) * LANES] = st.at[j][rows, :].astype(CDT)
        qn = _rms_head_fwd(h_ref[:, C_GQ:C_GQ + 256], ggq_ref[...], j_ref[...])
        gq_o[...] = (_rope(qn, tabg) * scale).astype(CDT)
        kn = _rms_head_fwd(h_ref[:, C_GK:C_GK + 128], ggk_ref[...], j_ref[0:128, 0:128])
        gk_o[...] = _rope(kn, tabg).astype(CDT)
        gv_o[...] = h_ref[:, C_GV:C_GV + 128].astype(CDT)

    widths = (256, 128, 128, 384, 384, 384, 256, 128, 128)
    out_specs = [_row(ts, w) for w in widths]
    out_shape = [_sds((S, w), CDT) for w in widths]
    zeros, aliases = [], {}
    for d in DIL_STRIDES:
        n, L = ts // d, S // d
        out_specs += [_res_spec(d, n), _res_spec(d, n, DIL_HALF // n), _res_spec(d, n, DIL_HALF // n)]
        out_shape += [_sds((d, L, DIL_W), CDT)] + [_sds((d, L + 2 * DIL_HALF, DIL_W), CDT)] * 2
        for t in range(2):
            aliases[12 + len(zeros)] = len(out_shape) - 2 + t
            zeros.append(jnp.zeros((d, L + 2 * DIL_HALF, DIL_W), CDT))
    return pl.pallas_call(
        body, name="prep_fwd", grid=(S // ts,),
        in_specs=[_row(ts, IN_P), _full(gq.shape), _full(gkv.shape), _full(ggq.shape), _full(ggk.shape)]
        + [_row(ts, LANES)] * 6 + [_full(j256.shape)] + [pl.BlockSpec(memory_space=pl.ANY)] * len(zeros),
        out_specs=out_specs, out_shape=out_shape, input_output_aliases=aliases,
        scratch_shapes=[pltpu.VMEM((3 * DIL_W // LANES, ts, LANES), F32)],
        compiler_params=_cparams(1),
    )(h, gq, gkv, ggq, ggk, *tm, *tg, j256, *zeros)


def _prep_bwd(h, dcq, dckv, dkr, ddq, ddk, ddv, dgq, dgk, dgv, gq, gkv, ggq, ggk, tg, j256):
    S = h.shape[0]
    ts = min(256, S)
    scale = HEAD_DIM ** -0.5

    def body(h_ref, dcq_r, dckv_r, dkr_r, q1, q2, q3, k1, k2, k3, v1, v2, v3, dgq_r, dgk_r, dgv_r,
             gq_ref, gkv_ref, ggq_ref, ggk_ref, cg, smg, spg, j_ref,
             dh_o, ngq_o, ngkv_o, nggq_o, nggk_o, *scr):
        tabg = (cg[...], smg[...], spg[...])
        first = pl.program_id(0) == 0
        scr, = scr
        d2, d3 = DIL_STRIDES
        dq = q1[...] + _by_token(q2, scr, d2) + _by_token(q3, scr, d3)
        dk = k1[...] + _by_token(k2, scr, d2) + _by_token(k3, scr, d3)
        dv = v1[...] + _by_token(v2, scr, d2) + _by_token(v3, scr, d3)

        def acc(o_ref, val):
            s = jnp.sum(val, axis=0, keepdims=True)

            @pl.when(first)
            def _():
                o_ref[...] = s

            @pl.when(jnp.logical_not(first))
            def _():
                o_ref[...] += s

        dx, dg = _rms_bwd(h_ref[:, C_CQ:C_CQ + 256], gq_ref[...], dcq_r[...])
        dh_o[:, C_CQ:C_CQ + 256] = dx.astype(CDT)
        acc(ngq_o, dg)
        dx, dg = _rms_bwd(h_ref[:, C_CKV:C_CKV + 128], gkv_ref[...], dckv_r[...])
        dh_o[:, C_CKV:C_CKV + 128] = dx.astype(CDT)
        acc(ngkv_o, dg)
        dh_o[:, C_KR:C_KR + 128] = dkr_r[...].astype(CDT)
        dh_o[:, C_DQ:C_DQ + 384] = (dq * scale).astype(CDT)
        dh_o[:, C_DK:C_DK + 384] = dk.astype(CDT)
        dh_o[:, C_DV:C_DV + 384] = dv.astype(CDT)
        dqn = _rope_t(dgq_r[...] * scale, tabg)
        dx, dg = _rms_head_bwd(h_ref[:, C_GQ:C_GQ + 256], ggq_ref[...], dqn, j_ref[...])
        dh_o[:, C_GQ:C_GQ + 256] = dx.astype(CDT)
        acc(nggq_o, dg)
        dkn = _rope_t(dgk_r[...], tabg)
        dx, dg = _rms_head_bwd(h_ref[:, C_GK:C_GK + 128], ggk_ref[...], dkn, j_ref[0:128, 0:128])
        dh_o[:, C_GK:C_GK + 128] = dx.astype(CDT)
        acc(nggk_o, dg)
        dh_o[:, C_GV:C_GV + 128] = dgv_r[...].astype(CDT)

    d2, d3 = DIL_STRIDES
    n2, n3 = ts // d2, ts // d3
    tok = _row(ts, DIL_W)
    return pl.pallas_call(
        body, name="prep_bwd", grid=(S // ts,),
        in_specs=[_row(ts, IN_P), _row(ts, 256), _row(ts, 128), _row(ts, 128)]
        + [tok, _res_spec(d2, n2), _res_spec(d3, n3)]
        + [tok, _res_spec(d2, n2, DIL_HALF // n2), _res_spec(d3, n3, DIL_HALF // n3)] * 2
        + [_row(ts, 256), _row(ts, 128), _row(ts, 128)]
        + [_full(gq.shape), _full(gkv.shape), _full(ggq.shape), _full(ggk.shape)] + [_row(ts, LANES)] * 3
        + [_full(j256.shape)],
        out_specs=[_row(ts, IN_P), _full((1, 256)), _full((1, 128)), _full((1, 256)), _full((1, 128))],
        out_shape=[_sds((S, IN_P), CDT), _sds((1, 256), F32), _sds((1, 128), F32), _sds((1, 256), F32),
                   _sds((1, 128), F32)],
        scratch_shapes=[_TOKEN_SCRATCH(ts)],
        compiler_params=_cparams(1),
    )(h, dcq, dckv, dkr, *ddq, *ddk, *ddv, dgq, dgk, dgv, gq, gkv, ggq, ggk, *tg, j256)


def _lane_blocks(width):
    return [(j, slice(j * LANES, (j + 1) * LANES)) for j in range(width // LANES)]


_TOKEN_SCRATCH = lambda ts: pltpu.VMEM((DIL_W // LANES, ts, LANES), F32)


def _by_token(res_ref, scr_ref, d):
    n = res_ref.shape[1]
    if d == 1:
        return res_ref[0].astype(F32)
    for c in range(d):
        for j, lanes in _lane_blocks(res_ref.shape[2]):
            scr_ref.at[j][pl.ds(c, n, stride=d), :] = res_ref[c, :, lanes].astype(F32)
    return jnp.concatenate([scr_ref[j] for j, _ in _lane_blocks(res_ref.shape[2])], axis=1)


def _by_residue(val, scr_ref, out_ref, d):
    n = out_ref.shape[1]
    if d == 1:
        out_ref[0] = val.astype(out_ref.dtype)
        return
    for j, lanes in _lane_blocks(out_ref.shape[2]):
        scr_ref[j] = val[:, lanes]
    for c in range(d):
        for j, lanes in _lane_blocks(out_ref.shape[2]):
            out_ref[c, :, lanes] = scr_ref.at[j][pl.ds(c, n, stride=d), :].astype(out_ref.dtype)


def _mla_prep_fwd(qa, kvp, kr, tm, scale):
    S = qa.shape[0]
    ts = min(256, S)

    def body(qa_ref, kv_ref, kr_ref, cm, smm, spm, q_o, k_o):
        tabm = (cm[...], smm[...], spm[...])
        q_o[...] = (_rope(qa_ref[...], tabm) * scale).astype(CDT)
        k_o[...] = kv_ref[:, 0:768] + _lanes(kr_ref[...], 768)

    return pl.pallas_call(
        body, name="mla_prep_fwd", grid=(S // ts,),
        in_specs=[_row(ts, 768), _row(ts, 1152), _row(ts, 128)] + [_row(ts, LANES)] * 3,
        out_specs=[_row(ts, 768)] * 2, out_shape=[_sds((S, 768), CDT)] * 2,
        compiler_params=_cparams(1),
    )(qa, kvp, kr, *tm)


def _mla_prep_bwd(dq, dk, dv, tm, scale):
    S = dq.shape[0]
    ts = min(256, S)

    def body(dq_ref, dk_ref, dv_ref, cm, smm, spm, dqa_o, dkv_o, dkr_o):
        tabm = (cm[...], smm[...], spm[...])
        lane = lax.broadcasted_iota(jnp.int32, (1, LANES), 1)
        dqa_o[...] = _rope_t(dq_ref[...] * scale, tabm).astype(CDT)
        dkr = jnp.zeros((ts, LANES), F32)
        for hd in range(MLA_HEADS):
            blk = dk_ref[:, hd * 128:(hd + 1) * 128]
            dkv_o[:, hd * 128:(hd + 1) * 128] = jnp.where(lane < 64, blk, 0.0).astype(CDT)
            dkr = dkr + jnp.where((lane >= 64) & (lane < 96), blk, 0.0)
        dkv_o[:, 768:1152] = dv_ref[...].astype(CDT)
        dkr_o[...] = jnp.where((lane >= 64) & (lane < 96), _rope_t(dkr, tabm), 0.0)

    return pl.pallas_call(
        body, name="mla_prep_bwd", grid=(S // ts,),
        in_specs=[_row(ts, 768), _row(ts, 768), _row(ts, 384)] + [_row(ts, LANES)] * 3,
        out_specs=[_row(ts, 768), _row(ts, 1152), _row(ts, 128)],
        out_shape=[_sds((S, 768), CDT), _sds((S, 1152), CDT), _sds((S, 128), F32)],
        compiler_params=_cparams(1),
    )(dq, dk, dv, *tm)


def _ln_fwd(xa, xb, g, b, alpha, name):
    S, D = xa.shape
    ts = min(512, S)

    def body(xa_ref, xb_ref, g_ref, b_ref, y_o, yb_o, z_o):
        z = alpha * xa_ref[...] + xb_ref[...]
        mu = jnp.mean(z, axis=-1, keepdims=True)
        zc = z - mu
        var = jnp.mean(zc * zc, axis=-1, keepdims=True)
        y = zc * lax.rsqrt(var + 1e-5) * g_ref[...] + b_ref[...]
        y_o[...] = y
        yb_o[...] = y.astype(CDT)
        z_o[...] = z

    return pl.pallas_call(
        body, name=name, grid=(S // ts,),
        in_specs=[_row(ts, D), _row(ts, D), _full(g.shape), _full(b.shape)],
        out_specs=[_row(ts, D)] * 3, out_shape=[_sds((S, D), F32), _sds((S, D), CDT), _sds((S, D), F32)],
        compiler_params=_cparams(1),
    )(xa, xb, g, b)


def _ln_bwd(dya, dyb, z, g, alpha, name, loss_head=False):
    S, D = z.shape
    ts = min(512, S)

    def body(dya_ref, dyb_ref, z_ref, g_ref, dz_o, dzb_o, dg_o, db_o, *loss_o):
        first = pl.program_id(0) == 0
        if loss_head:
            err = dya_ref[...] - dyb_ref[...]
            dy = err * (1.0 / D)
            part = jnp.sum(jnp.sum(err * err, axis=1, keepdims=True), axis=0, keepdims=True) * (0.5 / D)

            @pl.when(first)
            def _():
                loss_o[0][...] = part

            @pl.when(jnp.logical_not(first))
            def _():
                loss_o[0][...] += part
        else:
            dy = dya_ref[...] + alpha * dyb_ref[...]
        z = z_ref[...]
        mu = jnp.mean(z, axis=-1, keepdims=True)
        zc = z - mu
        r = lax.rsqrt(jnp.mean(zc * zc, axis=-1, keepdims=True) + 1e-5)
        xh = zc * r
        dxh = dy * g_ref[...]
        dz = r * (dxh - jnp.mean(dxh, axis=-1, keepdims=True) - xh * jnp.mean(dxh * xh, axis=-1, keepdims=True))
        dz_o[...] = dz
        dzb_o[...] = dz.astype(CDT)
        sg = jnp.sum(dy * xh, axis=0, keepdims=True)
        sb = jnp.sum(dy, axis=0, keepdims=True)

        @pl.when(first)
        def _():
            dg_o[...] = sg
            db_o[...] = sb

        @pl.when(jnp.logical_not(first))
        def _():
            dg_o[...] += sg
            db_o[...] += sb

    extra = ([_full((1, 1))], [_sds((1, 1), F32)]) if loss_head else ([], [])
    return pl.pallas_call(
        body, name=name, grid=(S // ts,),
        in_specs=[_row(ts, D)] * 3 + [_full(g.shape)],
        out_specs=[_row(ts, D), _row(ts, D), _full((1, D)), _full((1, D))] + extra[0],
        out_shape=[_sds((S, D), F32), _sds((S, D), CDT), _sds((1, D), F32), _sds((1, D), F32)] + extra[1],
        compiler_params=_cparams(1),
    )(dya, dyb, z, g)


def _grp_spec(ts, w):
    return pl.BlockSpec((None, ts, w), lambda k, i: (k, i, 0))


def _ffn_up(xb, wg3, wu3):
    S, D = xb.shape
    G, Fc, _ = wg3.shape
    tm = _pick(S, 1024)
    wspec = pl.BlockSpec((None, Fc, D), lambda k, i: (k, 0, 0))

    def body(x_ref, wg_ref, wu_ref, g_o, u_o, a_o):
        x = x_ref[...]
        g = lax.dot_general(x, wg_ref[...], _NT, preferred_element_type=F32)
        u = lax.dot_general(x, wu_ref[...], _NT, preferred_element_type=F32)
        g_o[...] = g.astype(CDT)
        u_o[...] = u.astype(CDT)
        a_o[...] = (g / (1.0 + jnp.exp(-g)) * u).astype(CDT)

    return pl.pallas_call(
        body, name="ffn_up", grid=(G, S // tm),
        in_specs=[pl.BlockSpec((tm, D), lambda k, i: (i, 0)), wspec, wspec], out_specs=[_grp_spec(tm, Fc)] * 3,
        out_shape=[_sds((G, S, Fc), CDT)] * 3, compiler_params=_cparams(2),
    )(xb, wg3, wu3)


def _ffn_up_dx(dg3, du3, wg3, wu3):
    G, S, Fc = dg3.shape
    D = wg3.shape[2]
    tm = _pick(S, 512)
    wspec = pl.BlockSpec((G, Fc, D), lambda i: (0, 0, 0))
    aspec = pl.BlockSpec((G, tm, Fc), lambda i: (0, i, 0))

    def body(dg_ref, du_ref, wg_ref, wu_ref, o_ref):
        acc = None
        for k in range(G):
            for a_ref, w_ref in ((dg_ref, wg_ref), (du_ref, wu_ref)):
                part = jnp.dot(a_ref[k], w_ref[k], preferred_element_type=F32)
                acc = part if acc is None else acc + part
        o_ref[...] = acc

    return pl.pallas_call(
        body, name="ffn_up_dx", grid=(S // tm,), in_specs=[aspec, aspec, wspec, wspec],
        out_specs=pl.BlockSpec((tm, D), lambda i: (i, 0)), out_shape=_sds((S, D), F32), compiler_params=_cparams(1),
    )(dg3, du3, wg3, wu3)


def _ffn_down(act3, wd3):
    G, S, Fc = act3.shape
    D = wd3.shape[2]
    tm = _pick(S, 1024)

    def body(a_ref, w_ref, o_ref):
        acc = jnp.dot(a_ref[0], w_ref[0], preferred_element_type=F32)
        for k in range(1, G):
            acc = acc + jnp.dot(a_ref[k], w_ref[k], preferred_element_type=F32)
        o_ref[...] = acc

    return pl.pallas_call(
        body, name="ffn_down", grid=(S // tm,),
        in_specs=[pl.BlockSpec((G, tm, Fc), lambda i: (0, i, 0)), pl.BlockSpec((G, Fc, D), lambda i: (0, 0, 0))],
        out_specs=pl.BlockSpec((tm, D), lambda i: (i, 0)), out_shape=_sds((S, D), F32), compiler_params=_cparams(1),
    )(act3, wd3)


def _ffn_down_dx(dzb, wd3, g3, u3):
    S, D = dzb.shape
    G, Fc, _ = wd3.shape
    tm = _pick(S, 1024)

    def body(dz_ref, wd_ref, g_ref, u_ref, dg_o, du_o):
        da = lax.dot_general(dz_ref[...], wd_ref[...], _NT, preferred_element_type=F32)
        g = g_ref[...].astype(F32)
        sg = 1.0 / (1.0 + jnp.exp(-g))
        dg_o[...] = (da * u_ref[...].astype(F32) * (sg * (1.0 + g * (1.0 - sg)))).astype(CDT)
        du_o[...] = (da * (g * sg)).astype(CDT)

    return pl.pallas_call(
        body, name="ffn_down_dx", grid=(G, S // tm),
        in_specs=[pl.BlockSpec((tm, D), lambda k, i: (i, 0)), pl.BlockSpec((None, Fc, D), lambda k, i: (k, 0, 0)),
                  _grp_spec(tm, Fc), _grp_spec(tm, Fc)],
        out_specs=[_grp_spec(tm, Fc)] * 2, out_shape=[_sds((G, S, Fc), CDT)] * 2, compiler_params=_cparams(2),
    )(dzb, wd3, g3, u3)


def _axpy(a, b, alpha, name):
    S, D = a.shape
    ts = min(512, S)

    def body(a_ref, b_ref, o_ref):
        o_ref[...] = a_ref[...] + alpha * b_ref[...]

    return pl.pallas_call(
        body, name=name, grid=(S // ts,), in_specs=[_row(ts, D)] * 2, out_specs=_row(ts, D),
        out_shape=_sds((S, D), F32), compiler_params=_cparams(1),
    )(a, b)


def _pair_masks():
    lane = lax.broadcasted_iota(jnp.int32, (1, LANES), 1)
    first = lane < HEAD_DIM
    return first, jnp.logical_not(first)


def _head_scalar(x, m):
    return jnp.max(jnp.where(m, x, -jnp.inf), axis=-1, keepdims=True)


_NT = (((1,), (1,)), ((), ()))
_TN = (((0,), (0,)), ((), ()))
ATTN_KEY_CHUNK = 1024


def _attn_fwd(q, k, v, *, split, npairs, kblk, vblk, name):
    S = q.shape[0]
    qw = 256 if split else LANES
    tq = min(256, S)

    def body(q_ref, k_ref, v_ref, o_ref, lse_ref):
        masks = _pair_masks()
        outs, lses = [], []
        for hd in range(2):
            if split:
                qh = q_ref[:, hd * LANES:(hd + 1) * LANES]
                kh = k_ref[:, hd * LANES:(hd + 1) * LANES]
            else:
                qh = jnp.where(masks[hd], q_ref[...], jnp.zeros_like(q_ref[...]))
                kh = k_ref[...]
            s = lax.dot_general(qh, kh, _NT, preferred_element_type=F32)
            mx = jnp.max(s, axis=-1, keepdims=True)
            p = jnp.exp(s - mx)
            l = jnp.sum(p, axis=-1, keepdims=True)
            o = jnp.dot(p.astype(CDT), v_ref[...], preferred_element_type=F32)
            outs.append(o / l)
            lses.append(jnp.broadcast_to(mx + jnp.log(l), (tq, LANES)))
        o_ref[...] = jnp.where(masks[0], outs[0], outs[1]).astype(o_ref.dtype)
        lse_ref[...] = jnp.where(masks[0], lses[0], lses[1])

    return pl.pallas_call(
        body, name=name, grid=(npairs, S // tq),
        in_specs=[pl.BlockSpec((tq, qw), lambda p, i: (i, p)),
                  pl.BlockSpec((S, qw), lambda p, i: (0, kblk(p))),
                  pl.BlockSpec((S, LANES), lambda p, i: (0, vblk(p)))],
        out_specs=[pl.BlockSpec((tq, LANES), lambda p, i: (i, p))] * 2,
        out_shape=[_sds((S, LANES * npairs), CDT), _sds((S, LANES * npairs), F32)],
        compiler_params=_cparams(2),
    )(q, k, v)


def _attn_bwd(q, k, v, do, o, lse, *, split, npairs, kblk, vblk, doblk, shared_kv, name):
    S = q.shape[0]
    qw = 256 if split else LANES
    tq = min(512, S)
    tkv = min(ATTN_KEY_CHUNK, S)
    nkv = 1 if shared_kv else npairs

    def body(q_ref, k_ref, v_ref, do_ref, o_ref, lse_ref, dq_ref, dk_ref, dv_ref):
        masks = _pair_masks()
        p_id, i_id = pl.program_id(0), pl.program_id(1)
        first = (i_id == 0) & ((p_id == 0) if shared_kv else True)
        @pl.when(first)
        def _():
            dk_ref[...] = jnp.zeros_like(dk_ref)
            dv_ref[...] = jnp.zeros_like(dv_ref)

        do = do_ref[...]
        o = o_ref[...].astype(F32)
        lse = lse_ref[...]
        heads = []
        for hd in range(2):
            m = masks[hd]
            cols = slice(hd * LANES, (hd + 1) * LANES) if split else slice(None)
            qh = q_ref[:, cols] if split else jnp.where(m, q_ref[...], jnp.zeros_like(q_ref[...]))
            doh = jnp.where(m, do, 0.0)
            heads.append((m, cols, qh, doh.astype(CDT), _head_scalar(lse, m), jnp.sum(doh * o, axis=-1, keepdims=True)))
        dqs = [jnp.zeros((tq, LANES), F32), jnp.zeros((tq, LANES), F32)]
        for ck in range(S // tkv):
            rows = slice(ck * tkv, (ck + 1) * tkv)
            v = v_ref[rows, :]
            dv = jnp.zeros((tkv, LANES), F32)
            for hd, (m, cols, qh, dohb, lse_h, delta) in enumerate(heads):
                kh = k_ref[rows, cols]
                s = lax.dot_general(qh, kh, _NT, preferred_element_type=F32)
                p = jnp.exp(s - lse_h)
                dp = lax.dot_general(dohb, v, _NT, preferred_element_type=F32)
                ds = (p * (dp - delta)).astype(CDT)
                dq = jnp.dot(ds, kh, preferred_element_type=F32)
                dqs[hd] = dqs[hd] + (dq if split else jnp.where(m, dq, 0.0))
                dk_ref[rows, cols] += lax.dot_general(ds, qh, _TN, preferred_element_type=F32)
                dv = dv + lax.dot_general(p.astype(CDT), dohb, _TN, preferred_element_type=F32)
            dv_ref[rows, :] += dv
        if split:
            dq_ref[:, 0:LANES] = dqs[0]
            dq_ref[:, LANES:2 * LANES] = dqs[1]
        else:
            dq_ref[...] = dqs[0] + dqs[1]

    kvo = (lambda p, i: (0, 0)) if shared_kv else (lambda p, i: (0, p))
    return pl.pallas_call(
        body, name=name, grid=(npairs, S // tq),
        in_specs=[pl.BlockSpec((tq, qw), lambda p, i: (i, p)),
                  pl.BlockSpec((S, qw), lambda p, i: (0, kblk(p))),
                  pl.BlockSpec((S, LANES), lambda p, i: (0, vblk(p))),
                  pl.BlockSpec((tq, LANES), lambda p, i: (i, doblk(p))),
                  pl.BlockSpec((tq, LANES), lambda p, i: (i, p)),
                  pl.BlockSpec((tq, LANES), lambda p, i: (i, p))],
        out_specs=[pl.BlockSpec((tq, qw), lambda p, i: (i, p)),
                   pl.BlockSpec((S, qw), kvo), pl.BlockSpec((S, LANES), kvo)],
        out_shape=[_sds((S, qw * npairs), F32), _sds((S, qw * nkv), F32), _sds((S, LANES * nkv), F32)],
        compiler_params=_cparams(2),
    )(q, k, v, do, o, lse)


def _bias_expand(idx, rel_bias, name):
    tq, kw = idx.shape

    def body(idx_ref, rb_ref, o_ref):
        idx = idx_ref[...]
        for hd in range(DIL_HEADS):
            acc = jnp.full((tq, kw), NEG_INF, F32)
            for u in range(REL_BUCKETS):
                acc = jnp.where(idx == u, rb_ref[u, hd], acc)
            o_ref[hd] = acc

    return pl.pallas_call(
        body, name=name,
        in_specs=[pl.BlockSpec(memory_space=pltpu.VMEM), pl.BlockSpec(memory_space=pltpu.SMEM)],
        out_specs=pl.BlockSpec(memory_space=pltpu.VMEM),
        out_shape=_sds((DIL_HEADS, tq, kw), F32),
    )(idx, rel_bias)


def _bias_reduce(idx, dtab, name):
    tq, kw = idx.shape

    def body(idx_ref, d_ref, o_ref):
        idx = idx_ref[...]
        rowid = lax.broadcasted_iota(jnp.int32, (REL_BUCKETS, kw), 0)
        for hd in range(DIL_HEADS):
            d = d_ref[hd]
            acc = jnp.zeros((REL_BUCKETS, kw), F32)
            for u in range(REL_BUCKETS):
                r = jnp.sum(jnp.where(idx == u, d, 0.0), axis=0, keepdims=True)
                acc = jnp.where(rowid == u, r, acc)
            o_ref[hd] = jnp.sum(acc, axis=1, keepdims=True)

    return pl.pallas_call(
        body, name=name,
        in_specs=[pl.BlockSpec(memory_space=pltpu.VMEM)] * 2, out_specs=pl.BlockSpec(memory_space=pltpu.VMEM),
        out_shape=_sds((DIL_HEADS, REL_BUCKETS, 1), F32),
    )(idx, dtab)


DIL_TILES_PER_STEP = 4


def _dil_tiles_per_step(L, tq):
    return DIL_TILES_PER_STEP if L % (DIL_TILES_PER_STEP * tq) == 0 else 1


def _dil_window(i, tq, kw, L):
    start = pl.multiple_of(i * tq, DIL_HALF)
    key = start + lax.broadcasted_iota(jnp.int32, (1, kw), 1) - DIL_HALF
    return start, (key >= 0) & (key < L)


def _dil_fwd(qv, kv, vv, tab, *, dil, L, tq, name):
    kw = tq + 2 * DIL_HALF
    npair = DIL_HEADS // 2
    nsub = _dil_tiles_per_step(L, tq)

    def body(q_ref, k_ref, v_ref, t_ref, o_ref, lse_ref):
        masks = _pair_masks()
        for j in range(nsub):
            rows = slice(j * tq, (j + 1) * tq)
            start, valid = _dil_window(pl.program_id(2) * nsub + j, tq, kw, L)
            kwin = k_ref[pl.ds(start, kw), :]
            vwin = v_ref[pl.ds(start, kw), :]
            q = q_ref[rows, :]
            outs, lses = [], []
            for hd in range(2):
                qh = jnp.where(masks[hd], q, jnp.zeros_like(q))
                s = lax.dot_general(qh, kwin, _NT, preferred_element_type=F32) + t_ref[hd]
                s = jnp.where(valid, s, NEG_INF)
                mx = jnp.max(s, axis=-1, keepdims=True)
                p = jnp.exp(s - mx)
                l = jnp.sum(p, axis=-1, keepdims=True)
                outs.append(jnp.dot(p.astype(CDT), vwin, preferred_element_type=F32) / l)
                lses.append(jnp.broadcast_to(mx + jnp.log(l), (tq, LANES)))
            o_ref[rows, :] = jnp.where(masks[0], outs[0], outs[1])
            lse_ref[rows, :] = jnp.where(masks[0], lses[0], lses[1])

    blk = pl.BlockSpec((None, nsub * tq, LANES), lambda p, c, i: (c, i, p))
    res = pl.BlockSpec((None, L + 2 * DIL_HALF, LANES), lambda p, c, i: (c, 0, p))
    return pl.pallas_call(
        body, name=name, grid=(npair, dil, L // (nsub * tq)),
        in_specs=[blk, res, res, pl.BlockSpec((2, tq, kw), lambda p, c, i: (p, 0, 0))],
        out_specs=[blk] * 2, out_shape=[_sds(qv.shape, F32)] * 2,
        compiler_params=_cparams(3),
    )(qv, kv, vv, tab)


def _dil_bwd(qv, kv, vv, tab, dov, lsev, deltav, *, dil, L, tq, name):
    kw = tq + 2 * DIL_HALF
    npair = DIL_HEADS // 2
    nsub = _dil_tiles_per_step(L, tq)

    def body(q_ref, k_ref, v_ref, t_ref, do_ref, lse_ref, dl_ref, dq_ref, dk_ref, dv_ref, dt_ref):
        masks = _pair_masks()
        c_id, i_id = pl.program_id(1), pl.program_id(2)

        @pl.when(i_id == 0)
        def _():
            dk_ref[...] = jnp.zeros_like(dk_ref)
            dv_ref[...] = jnp.zeros_like(dv_ref)

        @pl.when((i_id == 0) & (c_id == 0))
        def _():
            dt_ref[...] = jnp.zeros_like(dt_ref)

        dts = [jnp.zeros((tq, kw), F32), jnp.zeros((tq, kw), F32)]
        for j in range(nsub):
            rows = slice(j * tq, (j + 1) * tq)
            start, valid = _dil_window(i_id * nsub + j, tq, kw, L)
            kwin = k_ref[pl.ds(start, kw), :]
            vwin = v_ref[pl.ds(start, kw), :]
            q, do, lse, dl = q_ref[rows, :], do_ref[rows, :], lse_ref[rows, :], dl_ref[rows, :]
            dq = jnp.zeros((tq, LANES), F32)
            dk = jnp.zeros((kw, LANES), F32)
            dv = jnp.zeros((kw, LANES), F32)
            for hd in range(2):
                m = masks[hd]
                qh = jnp.where(m, q, jnp.zeros_like(q))
                doh = jnp.where(m, do, jnp.zeros_like(do))
                s = lax.dot_general(qh, kwin, _NT, preferred_element_type=F32) + t_ref[hd]
                s = jnp.where(valid, s, NEG_INF)
                p = jnp.exp(s - _head_scalar(lse, m))
                dp = lax.dot_general(doh, vwin, _NT, preferred_element_type=F32)
                ds = p * (dp - _head_scalar(dl, m))
                dts[hd] = dts[hd] + ds
                dsb = ds.astype(CDT)
                dq = dq + jnp.where(m, jnp.dot(dsb, kwin, preferred_element_type=F32), 0.0)
                dk = dk + lax.dot_general(dsb, qh, _TN, preferred_element_type=F32)
                dv = dv + lax.dot_general(p.astype(CDT), doh, _TN, preferred_element_type=F32)
            dq_ref[rows, :] = dq
            dk_ref[pl.ds(start, kw), :] += dk
            dv_ref[pl.ds(start, kw), :] += dv
        for hd in range(2):
            dt_ref[hd] += dts[hd]

    blk = pl.BlockSpec((None, nsub * tq, LANES), lambda p, c, i: (c, i, p))
    res = pl.BlockSpec((None, L + 2 * DIL_HALF, LANES), lambda p, c, i: (c, 0, p))
    tsp = pl.BlockSpec((2, tq, kw), lambda p, c, i: (p, 0, 0))
    return pl.pallas_call(
        body, name=name, grid=(npair, dil, L // (nsub * tq)),
        in_specs=[blk, res, res, tsp, blk, blk, blk], out_specs=[blk, res, res, tsp],
        out_shape=[_sds(qv.shape, F32), _sds(kv.shape, F32), _sds(kv.shape, F32), _sds(tab.shape, F32)],
        compiler_params=_cparams(3),
    )(qv, kv, vv, tab, dov, lsev, deltav)


def _mix_weights(l1, l2, l3):
    mx = jnp.maximum(jnp.maximum(l1, l2), l3)
    e1, e2, e3 = jnp.exp(l1 - mx), jnp.exp(l2 - mx), jnp.exp(l3 - mx)
    inv = 1.0 / (e1 + e2 + e3)
    return e1 * inv, e2 * inv, e3 * inv


def _branch_specs(S, ts):
    dils = [d for _, d in DIL_BRANCHES]
    return dils, [_res_spec(d, ts // d) for d in dils], [(d, S // d, DIL_W) for d in dils]


def _dil_mix_fwd(os, ls):
    S = os[0].shape[0] * os[0].shape[1]
    ts = min(256, S)
    dils, specs, _ = _branch_specs(S, ts)

    def body(o1, o2, o3, l1, l2, l3, out, scr):
        o1, o2, o3, l1, l2, l3 = [_by_token(r, scr, d) for r, d in zip((o1, o2, o3, l1, l2, l3), dils + dils)]
        w1, w2, w3 = _mix_weights(l1, l2, l3)
        out[...] = (w1 * o1 + w2 * o2 + w3 * o3).astype(CDT)

    return pl.pallas_call(
        body, name="dil_mix_fwd", grid=(S // ts,), in_specs=specs + specs, out_specs=_row(ts, DIL_W),
        out_shape=_sds((S, DIL_W), CDT), scratch_shapes=[_TOKEN_SCRATCH(ts)],
        compiler_params=_cparams(1),
    )(*os, *ls)


def _dil_mix_bwd(dcat, os, ls, j384):
    S = os[0].shape[0] * os[0].shape[1]
    ts = min(256, S)
    dils, specs, shapes = _branch_specs(S, ts)

    def body(do_ref, o1, o2, o3, l1, l2, l3, j_ref, d1, d2, d3, e1, e2, e3, scr):
        o1, o2, o3, l1, l2, l3 = [_by_token(r, scr, d) for r, d in zip((o1, o2, o3, l1, l2, l3), dils + dils)]
        ws = _mix_weights(l1, l2, l3)
        do = do_ref[...]
        o = ws[0] * o1 + ws[1] * o2 + ws[2] * o3
        dot = _headsum(do * o, j_ref[...])
        for w, d, d_o, e_o in zip(ws, dils, (d1, d2, d3), (e1, e2, e3)):
            _by_residue(w * do, scr, d_o, d)
            _by_residue(w * dot, scr, e_o, d)

    return pl.pallas_call(
        body, name="dil_mix_bwd", grid=(S // ts,),
        in_specs=[_row(ts, DIL_W, 1)] + specs + specs + [_full(j384.shape)],
        out_specs=specs + specs,
        out_shape=[_sds(s, CDT) for s in shapes] + [_sds(s, F32) for s in shapes],
        scratch_shapes=[_TOKEN_SCRATCH(ts)],
        compiler_params=_cparams(1),
    )(dcat, *os, *ls, j384)


def _adamw_math(w, g, m, v):
    m = ADAM_B1 * m + (1.0 - ADAM_B1) * g
    v = ADAM_B2 * v + (1.0 - ADAM_B2) * (g * g)
    m_hat = m / (1.0 - ADAM_B1 ** ADAM_STEP)
    v_hat = v / (1.0 - ADAM_B2 ** ADAM_STEP)
    delta = -ADAM_LR * (m_hat / (jnp.sqrt(v_hat) + ADAM_EPS) + ADAM_WD * w)
    return delta, m, v


def _pick8(n, target):
    best = None
    for t in range(16, min(n, target) + 1, 16):
        if n % t == 0:
            best = t
    return best if best is not None else n


_ELEMS_PER_BLOCK = 256 * 1024


def _adamw(w, reds, sibs, m, v, owner, name):
    L, a, b = w.shape
    ta = _pick8(a, max(16, _ELEMS_PER_BLOCK // b))
    spec = pl.BlockSpec((None, ta, b), lambda l, i, own: (l, i, 0))
    def gspec(mine, layer):
        def index(l, i, own):
            use = (own[0] if mine else 1 - own[0]) * (l if layer else 1 - l)
            return i * use, 0
        return pl.BlockSpec((ta, b), index)

    def body(own_ref, w_ref, r0_ref, r1_ref, s0_ref, s1_ref, m_ref, v_ref, g_o, d_o, m_o, v_o):
        mine = own_ref[0] == 1
        g0 = jnp.where(mine, r0_ref[...], s0_ref[...])
        g1 = jnp.where(mine, r1_ref[...], s1_ref[...])
        g = jnp.where(pl.program_id(0) == 0, g0, g1)
        d, mm, vv = _adamw_math(w_ref[...], g, m_ref[...], v_ref[...])
        g_o[...] = g
        d_o[...] = d
        m_o[...] = mm
        v_o[...] = vv

    return pl.pallas_call(
        body, name=name, out_shape=[_sds(w.shape, F32)] * 4,
        grid_spec=pltpu.PrefetchScalarGridSpec(
            num_scalar_prefetch=1, grid=(L, a // ta),
            in_specs=[spec, gspec(True, 0), gspec(True, 1), gspec(False, 0), gspec(False, 1), spec, spec],
            out_specs=[spec] * 4),
        compiler_params=_cparams(2),
    )(_is_core(owner), w, *reds, *sibs, m, v)


def _adamw_small(w, gall, m, v):
    R = w.shape[0]

    def body(w_ref, g_ref, m_ref, v_ref, g_o, d_o, m_o, v_o):
        g = g_ref[0]
        for k in range(1, 8):
            g = g + g_ref[k]
        d, mm, vv = _adamw_math(w_ref[...], g, m_ref[...], v_ref[...])
        g_o[...] = g
        d_o[...] = d
        m_o[...] = mm
        v_o[...] = vv

    vm = pl.BlockSpec(memory_space=pltpu.VMEM)
    return pl.pallas_call(
        body, name="adamw_small", in_specs=[vm] * 4, out_specs=[vm] * 4, out_shape=[_sds((R, LANES), F32)] * 4,
    )(w, gall, m, v)


def _sum_pair(g, t, owner, name):
    n, a, b = t.shape
    ta = _pick8(a, max(16, _ELEMS_PER_BLOCK // b))
    spec = pl.BlockSpec((None, ta, b), lambda k, i, own: (k * own[0], i * own[0], 0))

    def body(own_ref, g_ref, t_ref, o_ref):
        @pl.when(own_ref[0] == 1)
        def _():
            o_ref[...] = (g_ref[...].astype(F32) + t_ref[...].astype(F32)).astype(o_ref.dtype)

    return pl.pallas_call(
        body, name=name, out_shape=_sds(t.shape, WIRE),
        grid_spec=pltpu.PrefetchScalarGridSpec(num_scalar_prefetch=1, grid=(n, a // ta), in_specs=[spec] * 2,
                                               out_specs=spec),
        compiler_params=_cparams(2),
    )(_is_core(owner), g, t)


def _sum_chips(pair, t, owner, name):
    _, a, b = t.shape
    ta = _pick8(a, max(16, _ELEMS_PER_BLOCK // b))

    def body(own_ref, p_ref, t_ref, o_ref):
        @pl.when(own_ref[0] == 1)
        def _():
            me = 2 * lax.axis_index("x") + lax.axis_index("y")
            acc = p_ref[me].astype(F32)
            for k in range(3):
                acc = acc + t_ref[k].astype(F32)
            o_ref[...] = acc

    return pl.pallas_call(
        body, name=name, out_shape=_sds((a, b), F32),
        grid_spec=pltpu.PrefetchScalarGridSpec(
            num_scalar_prefetch=1, grid=(a // ta,),
            in_specs=[pl.BlockSpec((4, ta, b), lambda i, own: (0, i * own[0], 0)),
                      pl.BlockSpec((3, ta, b), lambda i, own: (0, i * own[0], 0))],
            out_specs=pl.BlockSpec((ta, b), lambda i, own: (i * own[0], 0))),
        compiler_params=_cparams(1),
    )(_is_core(owner), pair, t)


def _is_core(core):
    return (lax.axis_index("c") == core).astype(jnp.int32).reshape(1)


_HBM = pl.BlockSpec(memory_space=pltpu.HBM)


def _place():
    x, y, c = lax.axis_index("x"), lax.axis_index("y"), lax.axis_index("c")
    chips = [(1 - x, y), (x, 1 - y), (1 - x, 1 - y)]
    return x, y, c, chips


def _remote(src, dst, ssem, rsem, to):
    return pltpu.make_async_remote_copy(src_ref=src, dst_ref=dst, send_sem=ssem, recv_sem=rsem, device_id=to,
                                        device_id_type=MESH_ID)


def _dma_sems(n):
    return pltpu.SemaphoreType.DMA((n,))


_SEM = pl.BlockSpec(memory_space=pltpu.SEMAPHORE)
_ANY = pl.BlockSpec(memory_space=pl.ANY)
_EFFECT = pltpu.SideEffectType.DATAFLOW_SIDE_EFFECTING
_BIG = ("w_in", "mla_w_uq", "mla_w_ukv", "w_out", "ffn_w_gate", "ffn_w_up", "ffn_w_down")
_OWNER = dict(zip(_BIG, (1, 0, 0, 1, 0, 0, 1)))
_ATTN_WEIGHTS, _FFN_WEIGHTS = _BIG[:4], _BIG[4:]


def _hbm(a):
    return pltpu.with_memory_space_constraint(a, pltpu.HBM)


def _per_core(c, owners, fn):
    for g in range(2):
        mine = tuple(p for p, o in enumerate(owners) if o == g)
        theirs = tuple(p for p, o in enumerate(owners) if o != g)
        pl.when(c == g)(functools.partial(fn, mine, theirs))


def _token_spec():
    return pl.BlockSpec(memory_space=pltpu.VMEM), _sds((8, LANES), F32)


def _gather_start(shards, owners, layer, tag):
    n = len(shards)
    lands = [_hbm(lax.empty((4,) + s.shape[1:], s.dtype)) for s in shards]

    def body(*refs):
        w_refs, l_refs = refs[:n], refs[n:2 * n]
        ssem, rsem, token = refs[2 * n], refs[2 * n + 1], refs[-1]
        x, y, c, chips = _place()
        me = 2 * x + y

        def send(mine, _):
            for i in mine:
                for j, (cx, cy) in enumerate(chips):
                    _remote(w_refs[i].at[layer], l_refs[i].at[me], ssem.at[3 * i + j], rsem.at[3 * i + j],
                            (cx, cy, c)).start()

        _per_core(c, owners, send)
        token[...] = jnp.zeros_like(token)

    tspec, tshape = _token_spec()
    out = pl.pallas_call(
        body, name=f"gather_start_{tag}", in_specs=[_HBM] * (2 * n),
        out_specs=[_SEM, _SEM] + [_HBM] * n + [tspec],
        out_shape=[_dma_sems(3 * n), _dma_sems(3 * n)] + [pltpu.HBM(l.shape, l.dtype) for l in lands] + [tshape],
        input_output_aliases={n + i: 2 + i for i in range(n)},
        compiler_params=pltpu.CompilerParams(has_side_effects=_EFFECT),
    )(*[_hbm(s) for s in shards], *lands)
    return out[0], out[1], list(out[2:2 + n]), out[-1]


def _gather_wait(ssem, rsem, shards, lands, after, owners, layer, tag):
    n = len(shards)

    def body(*refs):
        w_refs, l_refs = refs[:n], refs[n:2 * n]
        ssem, rsem = refs[2 * n], refs[2 * n + 1]
        x, y, c, chips = _place()

        def wait(mine, _):
            for i in mine:
                for j, (cx, cy) in enumerate(chips):
                    cp = _remote(w_refs[i].at[layer], l_refs[i].at[2 * cx + cy], ssem.at[3 * i + j], rsem.at[3 * i + j],
                                 (cx, cy, c))
                    cp.wait_send()
                    cp.wait_recv()

        _per_core(c, owners, wait)

    return list(pl.pallas_call(
        body, name=f"gather_wait_{tag}", in_specs=[_HBM] * (2 * n) + [_SEM, _SEM, _ANY], out_specs=[_HBM] * n,
        out_shape=[pltpu.HBM(l.shape, l.dtype) for l in lands],
        input_output_aliases={n + i: i for i in range(n)},
        compiler_params=pltpu.CompilerParams(has_side_effects=_EFFECT),
    )(*[_hbm(s) for s in shards], *lands, ssem, rsem, after))


def _gather_finish(shards, lands, owners, layer, tag):
    n = len(shards)

    def body(*refs):
        w_refs, g_refs = refs[:n], refs[2 * n:3 * n]
        ssem, rsem = refs[3 * n:]
        x, y, c, chips = _place()
        me = 2 * x + y
        sib = (x, y, 1 - c)
        owns = [_remote(w.at[layer], g.at[me], ssem.at[i], rsem.at[i], sib) for i, (w, g) in enumerate(zip(w_refs, g_refs))]
        for cp in owns:
            cp.start()

        def forward(mine, theirs):
            def blk(i, j):
                b = g_refs[i].at[2 * chips[j][0] + chips[j][1]]
                return _remote(b, b, ssem.at[n + 3 * i + j], rsem.at[n + 3 * i + j], sib)

            for i in mine:
                for j in range(3):
                    blk(i, j).start()
            for i in theirs:
                for j in range(3):
                    blk(i, j).wait_recv()
            for i in mine:
                for j in range(3):
                    blk(i, j).wait_send()

        _per_core(c, owners, forward)
        for cp in owns:
            cp.wait_recv()
            cp.wait_send()

    return list(pl.pallas_call(
        body, name=f"gather_finish_{tag}", in_specs=[_HBM] * (2 * n), out_specs=[_HBM] * n,
        out_shape=[_sds(l.shape, l.dtype) for l in lands], input_output_aliases={n + i: i for i in range(n)},
        scratch_shapes=[_dma_sems(4 * n), _dma_sems(4 * n)],
    )(*shards, *lands))


def _rs_to_owner(grads, owners, tag):
    n = len(grads)

    def body(*refs):
        g_refs, t_refs = refs[:n], refs[n:2 * n]
        ssem, rsem = refs[2 * n:]
        x, y, c, _ = _place()

        def swap(mine, theirs):
            cps = [_remote(g_refs[i], t_refs[i], ssem.at[i], rsem.at[i], (x, y, 1 - c)) for i in theirs]
            for cp in cps:
                cp.start()
            for i in mine:
                _remote(g_refs[i], t_refs[i], ssem.at[i], rsem.at[i], (x, y, 1 - c)).wait_recv()
            for cp in cps:
                cp.wait_send()

        _per_core(c, owners, swap)

    return list(pl.pallas_call(
        body, name=f"rs_to_owner_{tag}", in_specs=[_HBM] * n, out_specs=[_HBM] * n,
        out_shape=[_sds(g.shape, g.dtype) for g in grads], scratch_shapes=[_dma_sems(n), _dma_sems(n)],
    )(*grads))


def _a2a_start(pairs, owners, tag):
    n = len(pairs)
    lands = [_hbm(lax.empty((3,) + p.shape[1:], p.dtype)) for p in pairs]

    def body(*refs):
        a_refs, t_refs = refs[:n], refs[n:2 * n]
        ssem, rsem, token = refs[2 * n], refs[2 * n + 1], refs[-1]
        x, y, c, chips = _place()

        def send(mine, _):
            for i in mine:
                for j, (cx, cy) in enumerate(chips):
                    _remote(a_refs[i].at[2 * cx + cy], t_refs[i].at[j], ssem.at[3 * i + j], rsem.at[3 * i + j],
                            (cx, cy, c)).start()

        _per_core(c, owners, send)
        token[...] = jnp.zeros_like(token)

    tspec, tshape = _token_spec()
    out = pl.pallas_call(
        body, name=f"rs_a2a_start_{tag}", in_specs=[_HBM] * (2 * n),
        out_specs=[_SEM, _SEM] + [_HBM] * n + [tspec],
        out_shape=[_dma_sems(3 * n), _dma_sems(3 * n)] + [pltpu.HBM(l.shape, l.dtype) for l in lands] + [tshape],
        input_output_aliases={n + i: 2 + i for i in range(n)},
        compiler_params=pltpu.CompilerParams(has_side_effects=_EFFECT),
    )(*[_hbm(p) for p in pairs], *lands)
    return out[0], out[1], list(out[2:2 + n]), out[-1]


def _a2a_wait(ssem, rsem, pairs, lands, after, owners, tag):
    n = len(pairs)

    def body(*refs):
        a_refs, t_refs = refs[:n], refs[n:2 * n]
        ssem, rsem = refs[2 * n], refs[2 * n + 1]
        x, y, c, chips = _place()

        def wait(mine, _):
            for i in mine:
                for j, (cx, cy) in enumerate(chips):
                    cp = _remote(a_refs[i].at[2 * cx + cy], t_refs[i].at[j], ssem.at[3 * i + j], rsem.at[3 * i + j],
                                 (cx, cy, c))
                    cp.wait_send()
                    cp.wait_recv()

        _per_core(c, owners, wait)

    return list(pl.pallas_call(
        body, name=f"rs_a2a_wait_{tag}", in_specs=[_HBM] * (2 * n) + [_SEM, _SEM, _ANY], out_specs=[_HBM] * n,
        out_shape=[pltpu.HBM(l.shape, l.dtype) for l in lands],
        input_output_aliases={n + i: i for i in range(n)},
        compiler_params=pltpu.CompilerParams(has_side_effects=_EFFECT),
    )(*[_hbm(p) for p in pairs], *lands, ssem, rsem, after))


def _rs_from_owner(reds, owners):
    n = len(reds)

    def body(*refs):
        q_refs, o_refs = refs[:n], refs[n:2 * n]
        ssem, rsem = refs[2 * n:]
        x, y, c, _ = _place()

        def swap(mine, theirs):
            cps = [_remote(q_refs[k], o_refs[k], ssem.at[k], rsem.at[k], (x, y, 1 - c)) for k in mine]
            for cp in cps:
                cp.start()
            for k in theirs:
                _remote(q_refs[k], o_refs[k], ssem.at[k], rsem.at[k], (x, y, 1 - c)).wait_recv()
            for cp in cps:
                cp.wait_send()

        _per_core(c, owners, swap)

    return list(pl.pallas_call(
        body, name="rs_from_owner", in_specs=[_HBM] * n, out_specs=[_HBM] * n,
        out_shape=[_sds(q.shape, q.dtype) for q in reds], scratch_shapes=[_dma_sems(n), _dma_sems(n)],
    )(*reds))


def _gather_small(s):
    R, _ = s.shape

    def body(s_ref, o_ref, ssem, rsem, lsem):
        x, y, c, _ = _place()
        me = 4 * x + 2 * y + c
        own = pltpu.make_async_copy(s_ref, o_ref.at[me], lsem)
        own.start()
        sends = []
        for k in range(1, 8):
            px, py, pc = x ^ (k >> 2), y ^ ((k >> 1) & 1), c ^ (k & 1)
            cp = _remote(s_ref, o_ref.at[me], ssem.at[k - 1], rsem.at[k - 1], (px, py, pc))
            cp.start()
            sends.append(cp)
        for k in range(1, 8):
            px, py, pc = x ^ (k >> 2), y ^ ((k >> 1) & 1), c ^ (k & 1)
            blk = o_ref.at[4 * px + 2 * py + pc]
            _remote(blk, blk, ssem.at[k - 1], rsem.at[k - 1], (px, py, pc)).wait_recv()
        for cp in sends:
            cp.wait_send()
        own.wait()

    vm = pl.BlockSpec(memory_space=pltpu.VMEM)
    return pl.pallas_call(
        body, name="gather_small", in_specs=[vm], out_specs=vm, out_shape=_sds((8, R, LANES), s.dtype),
        scratch_shapes=[pltpu.SemaphoreType.DMA((7,)), pltpu.SemaphoreType.DMA((7,)), pltpu.SemaphoreType.DMA],
    )(s)


_COL_SHARDED =("w_in", "mla_w_uq", "mla_w_ukv", "ffn_w_gate", "ffn_w_up")
_SMALL = ("mla_q_norm", "mla_kv_norm", "gqa_q_norm", "gqa_k_norm", "rel_bias", "ln1_g", "ln1_b", "ln2_g", "ln2_b")


def _pack_flat(arrs, align):
    flat = jnp.concatenate([a.reshape(-1) for a in arrs])
    pad = (-flat.shape[0]) % align
    return jnp.pad(flat, (0, pad)) if pad else flat


def _unpack_flat(flat, shapes):
    out, off = [], 0
    for s in shapes:
        n = int(np.prod(s))
        out.append(flat[off:off + n].reshape(s))
        off += n
    return out


def _perm_gqa_rows(w):
    return jnp.concatenate([w[:832], w[896:960], w[832:896], w[960:]], axis=0)


def _local_step(x, target, small, depth, weights_of_layer, grads_done):
    S, D = x.shape
    alpha = (2.0 * depth) ** 0.25
    in_idx, uq_idx, ukv_idx = _in_cols(), _uq_cols(), _ukv_cols()
    win, wuq, wukv, wout, wg, wu, wdn = ([None] * depth for _ in range(7))

    tm, tg = _rope_tables(S)
    j256, j384 = _head_ones(256), _head_ones(384)
    mla_scale = (64 + MLA_ROPE_DIM) ** -0.5
    branches = []
    for (_, dil) in DIL_BRANCHES:
        L = S // dil
        tq = min(256, L)
        idx = jnp.asarray(_branch_bucket_idx(tq, dil))
        branches.append((dil, L, tq, idx))
    tabs = [_bias_expand(idx, small["rel_bias"], name=f"bias_expand_{b}") for b, (_, _, _, idx) in enumerate(branches)]

    def padded(a):
        z = jnp.zeros((DIL_HALF, a.shape[1]), a.dtype)
        return jnp.concatenate([z, a, z], axis=0)[None]

    saved = []
    xf, xb = x, x.astype(CDT)
    for l in range(depth):
        W, token = weights_of_layer(l, "attn", xb)
        win[l] = _rows_from_shards(W["w_in"], in_idx)
        wuq[l] = _rows_from_shards(W["mla_w_uq"], uq_idx)
        wukv[l] = _rows_from_shards(W["mla_w_ukv"], ukv_idx)
        wout[l] = _perm_gqa_rows(W["w_out"].reshape(-1, D))
        gq, gkv = small["mla_q_norm"][l][None], small["mla_kv_norm"][l][None]
        if token is not None:
            gq = gq + token[0, 0]
        ggq = jnp.tile(small["gqa_q_norm"][l], 4)[None]
        ggk = jnp.tile(small["gqa_k_norm"][l], 2)[None]
        h = _mm(xb, win[l], tb=True, name="mm_in")
        cq, ckv, kr, qd, kd, vd, qg, kg, vg, *strided = _prep_fwd(h, gq, gkv, ggq, ggk, tm, tg, j256)
        qkv = [(qd[None], padded(kd), padded(vd))] + [tuple(strided[3 * b:3 * b + 3]) for b in range(len(DIL_STRIDES))]
        qa = _mm(cq, wuq[l], tb=True, name="mm_uq")
        kvp = _mm(ckv, wukv[l], tb=True, out_dtype=CDT, name="mm_ukv")
        qm, km = _mla_prep_fwd(qa, kvp, kr, tm, mla_scale)
        oa, lsa = _attn_fwd(qm, km, kvp, split=True, npairs=3, kblk=lambda p: p, vblk=lambda p: 6 + p,
                            name="mla_attn_fwd")
        oc, lsc = _attn_fwd(qg, kg, vg, split=False, npairs=2, kblk=lambda p: 0, vblk=lambda p: 0,
                            name="gqa_attn_fwd")
        obs, lbs = [], []
        for b, (dil, L, tq, _) in enumerate(branches):
            o_b, l_b = _dil_fwd(*qkv[b], tabs[b], dil=dil, L=L, tq=tq, name=f"dil_fwd_{b}")
            obs.append(o_b)
            lbs.append(l_b)
        ob = _dil_mix_fwd(obs, lbs)
        cat = jnp.concatenate([oa, ob, oc], axis=1)
        mix = _mm(cat, wout[l], name="mm_out")
        x1, x1b, z1 = _ln_fwd(xf, mix, small["ln1_g"][l][None], small["ln1_b"][l][None], alpha, name="ln1_fwd")
        W, _ = weights_of_layer(l, "ffn", x1b)
        wg[l], wu[l], wdn[l] = W["ffn_w_gate"], W["ffn_w_up"], W["ffn_w_down"]
        g3, u3, act = _ffn_up(x1b, wg[l], wu[l])
        ff = _ffn_down(act, wdn[l])
        x2, x2b, z2 = _ln_fwd(x1, ff, small["ln2_g"][l][None], small["ln2_b"][l][None], alpha, name="ln2_fwd")
        saved.append(dict(xb=xb, h=h, cq=cq, ckv=ckv, qg=qg, kg=kg, vg=vg, kvp=kvp, qm=qm, km=km, oa=oa, lsa=lsa,
                          oc=oc, lsc=lsc, obs=obs, lbs=lbs, qkv=qkv, cat=cat, z1=z1, x1b=x1b, g3=g3, u3=u3, act=act, z2=z2,
                          gq=gq, gkv=gkv, ggq=ggq, ggk=ggk))
        xf, xb = x2, x2b

    gW = {k: [None] * depth for k in _BIG}
    gS = {k: [None] * depth for k in ("mla_q_norm", "mla_kv_norm", "gqa_q_norm", "gqa_k_norm", "ln1_g", "ln1_b", "ln2_g",
                                      "ln2_b")}
    g_rel = None
    dya, dyb = xf, target
    token = None
    for l in reversed(range(depth)):
        sv = saved[l]
        ln2_g = small["ln2_g"][l][None]
        if token is not None:
            ln2_g = ln2_g + token[0, 0]
        if l == depth - 1:
            dz2, dz2b, gS["ln2_g"][l], gS["ln2_b"][l], loss = _ln_bwd(dya, dyb, sv["z2"], ln2_g, alpha,
                                                                       name="ln2_bwd_loss", loss_head=True)
        else:
            dz2, dz2b, gS["ln2_g"][l], gS["ln2_b"][l] = _ln_bwd(dya, dyb, sv["z2"], ln2_g, alpha, name="ln2_bwd")
        gW["ffn_w_down"][l] = _mm(sv["act"], dz2b, ta=True, ga=True, go=True, out_dtype=WIRE, name="mm_down_dw")
        dg3, du3 = _ffn_down_dx(dz2b, wdn[l], sv["g3"], sv["u3"])
        gW["ffn_w_gate"][l] = _mm(dg3, sv["x1b"], ta=True, ga=True, go=True, out_dtype=WIRE, name="mm_gate_dw")
        gW["ffn_w_up"][l] = _mm(du3, sv["x1b"], ta=True, ga=True, go=True, out_dtype=WIRE, name="mm_up_dw")
        dx1 = _ffn_up_dx(dg3, du3, wg[l], wu[l])
        token = grads_done(l, "ffn", {n: gW[n][l] for n in _FFN_WEIGHTS})
        ln1_g = small["ln1_g"][l][None]
        if token is not None:
            ln1_g = ln1_g + token[0, 0]
        dz1, dz1b, gS["ln1_g"][l], gS["ln1_b"][l] = _ln_bwd(dx1, dz2, sv["z1"], ln1_g, alpha, name="ln1_bwd")
        gW["w_out"][l] = _perm_gqa_rows(_mm(sv["cat"], dz1b, ta=True, out_dtype=WIRE, name="mm_out_dw")).reshape(4, -1, D)
        dcat = _mm(dz1b, wout[l], tb=True, name="mm_out_dx")
        dqg, dkg, dvg = _attn_bwd(sv["qg"], sv["kg"], sv["vg"], dcat, sv["oc"], sv["lsc"], split=False, npairs=2,
                                  kblk=lambda p: 0, vblk=lambda p: 0, doblk=lambda p: 6 + p, shared_kv=True,
                                  name="gqa_attn_bwd")
        dqm, dkm, dvm = _attn_bwd(sv["qm"], sv["km"], sv["kvp"], dcat, sv["oa"], sv["lsa"], split=True, npairs=3,
                                  kblk=lambda p: p, vblk=lambda p: 6 + p, doblk=lambda p: p, shared_kv=False,
                                  name="mla_attn_bwd")
        dqa, dkvp, dkr = _mla_prep_bwd(dqm, dkm, dvm, tm, mla_scale)
        gW["mla_w_uq"][l] = _rows_to_shards(_mm(dqa, sv["cq"], ta=True, out_dtype=WIRE, name="mm_uq_dw"), uq_idx, MLA_HEADS * 96)
        dcq = _mm(dqa, wuq[l], name="mm_uq_dx")
        gW["mla_w_ukv"][l] = _rows_to_shards(_mm(dkvp, sv["ckv"], ta=True, out_dtype=WIRE, name="mm_ukv_dw"), ukv_idx, MLA_HEADS * 128)
        dckv = _mm(dkvp, wukv[l], name="mm_ukv_dx")
        mixb = _dil_mix_bwd(dcat, sv["obs"], sv["lbs"], j384)
        ddq, ddk, ddv = [], [], []
        for b, (dil, L, tq, idx) in enumerate(branches):
            dq_b, dk_b, dv_b, dtab = _dil_bwd(*sv["qkv"][b], tabs[b], mixb[b], sv["lbs"][b], mixb[3 + b], dil=dil, L=L,
                                              tq=tq, name=f"dil_bwd_{b}")
            if dil == 1:
                dq_b, dk_b, dv_b = dq_b[0], dk_b[0, DIL_HALF:DIL_HALF + S], dv_b[0, DIL_HALF:DIL_HALF + S]
            ddq.append(dq_b)
            ddk.append(dk_b)
            ddv.append(dv_b)
            g_b = _bias_reduce(idx, dtab, name=f"bias_reduce_{b}")[:, :, 0].T
            g_rel = g_b if g_rel is None else g_rel + g_b
        dh, n1, n2, n3, n4 = _prep_bwd(sv["h"], dcq, dckv, dkr, ddq, ddk, ddv, dqg, dkg, dvg, sv["gq"], sv["gkv"],
                                       sv["ggq"], sv["ggk"], tg, j256)
        gS["mla_q_norm"][l], gS["mla_kv_norm"][l] = n1[0], n2[0]
        gS["gqa_q_norm"][l] = n3[0].reshape(4, 64).sum(0)
        gS["gqa_k_norm"][l] = n4[0].reshape(2, 64).sum(0)
        gW["w_in"][l] = _rows_to_shards(_mm(dh, sv["xb"], ta=True, out_dtype=WIRE, name="mm_in_dw"), in_idx, IN_W)
        dya = _mm(dh, win[l], name="mm_in_dx")
        dyb = dz1
        token = grads_done(l, "attn", {n: gW[n][l] for n in _ATTN_WEIGHTS})
    grad_x = _axpy(dya, dyb, alpha, name="grad_x")

    gsmall = {k: jnp.stack([a.reshape(-1) for a in v]) for k, v in gS.items()}
    gsmall["rel_bias"] = g_rel
    return loss, grad_x, gsmall


_ORDER = ("w_in", "mla_q_norm", "mla_kv_norm", "mla_w_uq", "mla_w_ukv", "gqa_q_norm", "gqa_k_norm", "rel_bias", "w_out",
          "ln1_g", "ln1_b", "ffn_w_gate", "ffn_w_up", "ffn_w_down", "ln2_g", "ln2_b")


def kernel(x, w_in, mla_q_norm, mla_kv_norm, mla_w_uq, mla_w_ukv, gqa_q_norm, gqa_k_norm, rel_bias, w_out, ln1_g, ln1_b, ffn_w_gate, ffn_w_up, ffn_w_down, ln2_g, ln2_b, loss_target, m_w_in, m_mla_q_norm, m_mla_kv_norm, m_mla_w_uq, m_mla_w_ukv, m_gqa_q_norm, m_gqa_k_norm, m_rel_bias, m_w_out, m_ln1_g, m_ln1_b, m_ffn_w_gate, m_ffn_w_up, m_ffn_w_down, m_ln2_g, m_ln2_b, v_w_in, v_mla_q_norm, v_mla_kv_norm, v_mla_w_uq, v_mla_w_ukv, v_gqa_q_norm, v_gqa_k_norm, v_rel_bias, v_w_out, v_ln1_g, v_ln1_b, v_ffn_w_gate, v_ffn_w_up, v_ffn_w_down, v_ln2_g, v_ln2_b):
    wts = dict(zip(_ORDER, (w_in, mla_q_norm, mla_kv_norm, mla_w_uq, mla_w_ukv, gqa_q_norm, gqa_k_norm, rel_bias, w_out,
                            ln1_g, ln1_b, ffn_w_gate, ffn_w_up, ffn_w_down, ln2_g, ln2_b)))
    mom = dict(zip(_ORDER, (m_w_in, m_mla_q_norm, m_mla_kv_norm, m_mla_w_uq, m_mla_w_ukv, m_gqa_q_norm, m_gqa_k_norm,
                            m_rel_bias, m_w_out, m_ln1_g, m_ln1_b, m_ffn_w_gate, m_ffn_w_up, m_ffn_w_down, m_ln2_g,
                            m_ln2_b)))
    var = dict(zip(_ORDER, (v_w_in, v_mla_q_norm, v_mla_kv_norm, v_mla_w_uq, v_mla_w_ukv, v_gqa_q_norm, v_gqa_k_norm,
                            v_rel_bias, v_w_out, v_ln1_g, v_ln1_b, v_ffn_w_gate, v_ffn_w_up, v_ffn_w_down, v_ln2_g,
                            v_ln2_b)))
    small_shapes = [wts[n].shape for n in _SMALL]
    for d in (wts, mom, var):
        for n in _COL_SHARDED:
            d[n] = d[n].transpose(0, 2, 1)

    depth = 2
    shards = {n: wts[n].astype(WIRE) for n in _BIG}
    flying = {}

    def start_gather(names, l, tag):
        own = tuple(_OWNER[n] for n in names)
        sh = [shards[n] for n in names]
        ssem, rsem, lands, token = _gather_start(sh, own, l, tag)
        return (names, own, sh, ssem, rsem, lands, l, tag), token

    def end_gather(flight, after):
        names, own, sh, ssem, rsem, lands, l, tag = flight
        got = _gather_finish(sh, _gather_wait(ssem, rsem, sh, lands, after, own, l, tag), own, l, tag)
        return {n: g.astype(CDT) for n, g in zip(names, got)}

    def weights_of_layer(l, part, after):
        if (l, part) == (0, "attn"):
            got = end_gather(start_gather(_ATTN_WEIGHTS, 0, "attn0")[0], after)
            flying["ffn0"], t0 = start_gather(_FFN_WEIGHTS, 0, "ffn0")
            flying["layer1"], t1 = start_gather(_BIG, 1, "layer1")
            return got, t0 + t1
        if (l, part) == (0, "ffn"):
            return end_gather(flying.pop("ffn0"), after), None
        if part == "attn":
            flying["w1"] = end_gather(flying.pop("layer1"), after)
        return flying["w1"], None

    def grads_done(l, part, grads):
        names = tuple(grads)
        own = tuple(_OWNER[n] for n in names)
        tag = f"{part}{l}"
        gl = [grads[n] for n in names]
        theirs = _rs_to_owner(gl, own, tag)
        pairs = [_sum_pair(g, t, o, name=f"rs_pair_sum_{n}") for n, g, t, o in zip(names, gl, theirs, own)]
        ssem, rsem, lands, token = _a2a_start(pairs, own, tag)
        flying[tag] = (names, own, ssem, rsem, pairs, lands)
        return token

    small = {n: wts[n] for n in _SMALL}
    loss, grad_x, gsmall = _local_step(x[0], loss_target[0], small, depth, weights_of_layer, grads_done)

    reds = {}
    for l in reversed(range(depth)):
        for part in ("ffn", "attn"):
            tag = f"{part}{l}"
            names, own, ssem, rsem, pairs, lands = flying.pop(tag)
            got = _a2a_wait(ssem, rsem, pairs, lands, grad_x, own, tag)
            for n, p, t, o in zip(names, pairs, got, own):
                reds[n, l] = _sum_chips(p, t, o, name=f"rs_sum_chips_{n}")
    order = [(n, l) for l in range(depth) for n in _BIG]
    sibs = dict(zip(order, _rs_from_owner([reds[k] for k in order], tuple(_OWNER[n] for n, _ in order))))

    sflat = _pack_flat([gsmall[n].reshape(-1) for n in _SMALL], 8 * LANES)
    rs = sflat.shape[0] // LANES
    sall = _gather_small(sflat.reshape(rs, LANES))

    def packed(d):
        return _pack_flat([d[n] for n in _SMALL], 8 * LANES).reshape(rs, LANES)

    outs = {tag: {} for tag in ("grad", "delta", "new_m", "new_v")}
    for n in _BIG:
        res = _adamw(wts[n], [reds[n, 0], reds[n, 1]], [sibs[n, 0], sibs[n, 1]], mom[n], var[n], _OWNER[n],
                     name=f"adamw_{n}")
        for tag, r in zip(("grad", "delta", "new_m", "new_v"), res):
            outs[tag][n] = r.transpose(0, 2, 1) if n in _COL_SHARDED else r
    for tag, smallflat in zip(("grad", "delta", "new_m", "new_v"), _adamw_small(packed(wts), sall, packed(mom), packed(var))):
        outs[tag].update(zip(_SMALL, _unpack_flat(smallflat.reshape(-1), small_shapes)))

    total = lax.psum(loss[0, 0], ("x", "y", "c"))
    return (total, grad_x[None], *[outs["grad"][n] for n in _ORDER], *[outs["delta"][n] for n in _ORDER],
            *[outs["new_m"][n] for n in _ORDER], *[outs["new_v"][n] for n in _ORDER])
```

```python
import functools
import math

import numpy as np
import jax
import jax.numpy as jnp
from jax import lax
from jax.experimental import pallas as pl
from jax.experimental.pallas import tpu as pltpu

F32 = jnp.float32
CDT = jnp.bfloat16
WIRE = jnp.bfloat16

HEAD_DIM = 64
GRID_W = 64
ROPE_THETA = 10000.0
MLA_HEADS = 6
MLA_Q_RANK = 256
MLA_KV_RANK = 128
MLA_ROPE_DIM = 32
DIL_HEADS = 6
DIL_BRANCHES = ((128, 1), (512, 4), (2048, 16))
DIL_HALF = 64
GQA_Q_HEADS = 4
REL_BUCKETS = 32
REL_MAX_DIST = 1024
NEG_INF = -1e30
LANES = 128
VMEM_LIMIT = 56 * 1024 * 1024

ADAM_LR, ADAM_B1, ADAM_B2, ADAM_EPS, ADAM_WD, ADAM_STEP = 0.001, 0.9, 0.999, 1e-08, 0.01, 10

C_CQ, C_CKV, C_KR, C_DQ, C_DK, C_DV, C_GQ, C_GK, C_GV, IN_P = 0, 256, 384, 512, 896, 1280, 1664, 1920, 2048, 2176
IN_W = 2080
MESH_ID = pl.DeviceIdType.MESH


def _cparams(n_axes, vmem=VMEM_LIMIT):
    return pltpu.CompilerParams(dimension_semantics=("arbitrary",) * n_axes, vmem_limit_bytes=vmem)


MAX_WHOLE_DIM = 2304
WHOLE_K_WINDOW_BYTES = 36 * 1024 * 1024


def _pick(n, target):
    best = None
    for t in range(LANES, min(n, target) + 1, LANES):
        if n % t == 0:
            best = t
    if best is not None and (2 * best >= target or n > MAX_WHOLE_DIM):
        return best
    return n


def _sds(shape, dtype):
    return jax.ShapeDtypeStruct(tuple(shape), dtype)


def _in_cols():
    idx = -np.ones((IN_P,), np.int64)
    idx[C_CQ:C_CQ + 256] = np.arange(0, 256)
    idx[C_CKV:C_CKV + 128] = np.arange(256, 384)
    idx[C_KR + 64:C_KR + 96] = np.arange(384, 416)
    idx[C_DQ:C_DQ + 1152] = np.arange(416, 1568)
    gq = 1568 + (np.array([0, 2, 1, 3])[:, None] * 64 + np.arange(64)[None, :]).reshape(-1)
    idx[C_GQ:C_GQ + 256] = gq
    idx[C_GK:C_GK + 256] = np.arange(1824, 2080)
    return idx


def _uq_cols():
    idx = -np.ones((MLA_HEADS * 128,), np.int64)
    for h in range(MLA_HEADS):
        idx[h * 128:h * 128 + 96] = np.arange(96 * h, 96 * h + 96)
    return idx


def _ukv_cols():
    idx = -np.ones((MLA_HEADS * 128 + MLA_HEADS * 64,), np.int64)
    for h in range(MLA_HEADS):
        idx[h * 128:h * 128 + 64] = np.arange(128 * h, 128 * h + 64)
        idx[768 + h * 64:768 + h * 64 + 64] = np.arange(128 * h + 64, 128 * h + 128)
    return idx


def _runs(idx):
    out, i = [], 0
    while i < len(idx):
        j = i + 1
        while j < len(idx) and ((idx[i] < 0 and idx[j] < 0) or (idx[i] >= 0 and idx[j] == idx[j - 1] + 1)):
            j += 1
        out.append((int(idx[i]), j - i))
        i = j
    return out


def _rows_from_shards(sh, idx):
    _, cs, r = sh.shape
    pieces = []
    for first, ln in _runs(idx):
        if first < 0:
            pieces.append(jnp.zeros((ln, r), sh.dtype))
            continue
        while ln > 0:
            k, off = divmod(first, cs)
            take = min(ln, cs - off)
            pieces.append(sh[k, off:off + take, :])
            first, ln = first + take, ln - take
    return jnp.concatenate(pieces, axis=0)


def _rows_to_shards(wp, idx, n):
    inv = np.zeros((n,), np.int64)
    pos = np.nonzero(idx >= 0)[0]
    inv[idx[pos]] = pos
    cs = n // 4
    shards = []
    for k in range(4):
        pieces = [wp[first:first + ln, :] for first, ln in _runs(inv[k * cs:(k + 1) * cs])]
        shards.append(jnp.concatenate(pieces, axis=0))
    return jnp.stack(shards)


def _t5_bucket_np(rel):
    nb = REL_BUCKETS // 2
    exact = nb // 2
    ret = np.where(rel > 0, nb, 0)
    n = np.abs(rel)
    nf = np.maximum(n, 1).astype(np.float32)
    large = exact + (np.log(nf / np.float32(exact)) / np.float32(math.log(REL_MAX_DIST / exact))
                     * np.float32(nb - exact)).astype(np.int32)
    large = np.minimum(large, nb - 1)
    return ret + np.where(n < exact, n, large)


def _branch_bucket_idx(tq, dil):
    kw = tq + 2 * DIL_HALF
    rel = np.arange(kw)[None, :] - DIL_HALF - np.arange(tq)[:, None]
    idx = _t5_bucket_np(rel * dil)
    return np.where(np.abs(rel) <= DIL_HALF, idx, -1).astype(np.int32)


def _rope_tables(S):
    inv = ROPE_THETA ** (-jnp.arange(0, 32, 2, dtype=F32) / 32)
    t = jnp.arange(S)
    pos = t.astype(F32)
    row = (t // GRID_W).astype(F32)
    col = (t % GRID_W).astype(F32)
    lane = np.arange(LANES)
    wm = lane - 64
    is_rope = (lane >= 64) & (lane < 96)
    ang = pos[:, None] * inv[np.where(is_rope, wm % 16, 0)][None, :]
    cm = jnp.where(is_rope[None], jnp.cos(ang), 1.0)
    smm = jnp.where((is_rope & (wm < 16))[None], -jnp.sin(ang), 0.0)
    spm = jnp.where((is_rope & (wm >= 16))[None], jnp.sin(ang), 0.0)
    g = lane % 64
    w = g % 32
    angg = jnp.where((g < 32)[None], row[:, None], col[:, None]) * inv[w % 16][None, :]
    cg = jnp.cos(angg)
    smg = jnp.where((w < 16)[None], -jnp.sin(angg), 0.0)
    spg = jnp.where((w >= 16)[None], jnp.sin(angg), 0.0)
    return (cm, smm, spm), (cg, smg, spg)


def _lanes(t, width):
    return t if width == LANES else jnp.concatenate([t] * (width // LANES), axis=1)


def _rope(x, tabs):
    c, sm, sp = (_lanes(t, x.shape[1]) for t in tabs)
    w = x.shape[1]
    return x * c + pltpu.roll(x, w - 16, 1) * sm + pltpu.roll(x, 16, 1) * sp


def _rope_t(dy, tabs):
    c, sm, sp = (_lanes(t, dy.shape[1]) for t in tabs)
    w = dy.shape[1]
    return dy * c + pltpu.roll(dy * sm, 16, 1) + pltpu.roll(dy * sp, w - 16, 1)


def _head_ones(width):
    i = np.arange(width)
    return jnp.asarray((i[:, None] // HEAD_DIM == i[None, :] // HEAD_DIM).astype(np.float32))


def _headsum(x, j):
    return jnp.dot(x, j, preferred_element_type=F32, precision=lax.Precision.HIGHEST)


def _mm(a, b, *, ta=False, tb=False, ga=False, gb=False, go=False, out_dtype=F32, name):
    G = a.shape[0] if ga else (b.shape[0] if gb else 1)
    a2 = a.shape[1:] if ga else a.shape
    b2 = b.shape[1:] if gb else b.shape
    K, M = a2 if ta else a2[::-1]
    N = b2[0] if tb else b2[1]
    assert (b2[1] if tb else b2[0]) == K
    tm, tn, tk = _pick(M, 1024), _pick(N, 1024), _pick(K, 2048)
    if tm * tn > 1024 * 1152:
        tm = _pick(M, 512)
    if 2 * K * (tm + tn) * jnp.dtype(CDT).itemsize <= WHOLE_K_WINDOW_BYTES:
        tk = K
    nk = K // tk
    steps = nk if (go or G == 1) else G * nk
    dn = (((0 if ta else 1,), (1 if tb else 0,)), ((), ()))

    def body(a_ref, b_ref, o_ref, *acc):
        part = lax.dot_general(a_ref[...], b_ref[...], dn, preferred_element_type=F32)
        if steps == 1:
            o_ref[...] = part.astype(o_ref.dtype)
            return
        acc_ref, = acc
        s = pl.program_id(3)

        @pl.when(s == 0)
        def _():
            acc_ref[...] = part

        @pl.when(s > 0)
        def _():
            acc_ref[...] += part

        @pl.when(s == steps - 1)
        def _():
            o_ref[...] = acc_ref[...].astype(o_ref.dtype)

    def grp(g, s):
        return g if go else s // nk

    def kk(s):
        return s if steps == nk else s % nk

    def spec(grouped, block, index):
        if grouped:
            return pl.BlockSpec((None,) + block, lambda g, i, j, s: (grp(g, s),) + index(i, j, s))
        return pl.BlockSpec(block, lambda g, i, j, s: index(i, j, s))

    a_spec = (spec(ga, (tk, tm), lambda i, j, s: (kk(s), i)) if ta else spec(ga, (tm, tk), lambda i, j, s: (i, kk(s))))
    b_spec = (spec(gb, (tn, tk), lambda i, j, s: (j, kk(s))) if tb else spec(gb, (tk, tn), lambda i, j, s: (kk(s), j)))
    o_spec = spec(go, (tm, tn), lambda i, j, s: (i, j))
    return pl.pallas_call(
        body, name=name, grid=(G if go else 1, M // tm, N // tn, steps),
        in_specs=[a_spec, b_spec], out_specs=o_spec,
        out_shape=_sds(((G,) if go else ()) + (M, N), out_dtype),
        scratch_shapes=[pltpu.VMEM((tm, tn), F32)] if steps > 1 else [],
        compiler_params=_cparams(4),
    )(a, b)


def _row(ts, w, cb=0):
    return pl.BlockSpec((ts, w), lambda i: (i, cb))


def _full(shape):
    nd = len(shape)
    return pl.BlockSpec(tuple(shape), lambda i: (0,) * nd)


def _rms_fwd(x, g, eps=1e-6):
    r = lax.rsqrt(jnp.mean(x * x, axis=-1, keepdims=True) + eps)
    return x * r * g


def _rms_bwd(x, g, dy, eps=1e-6):
    r = lax.rsqrt(jnp.mean(x * x, axis=-1, keepdims=True) + eps)
    gdy = g * dy
    dx = r * gdy - x * (r * r * r) * jnp.mean(x * gdy, axis=-1, keepdims=True)
    return dx, x * r * dy


def _rms_head_fwd(x, g, j, eps=1e-6):
    r = lax.rsqrt(_headsum(x * x, j) * (1.0 / HEAD_DIM) + eps)
    return x * r * g


def _rms_head_bwd(x, g, dy, j, eps=1e-6):
    r = lax.rsqrt(_headsum(x * x, j) * (1.0 / HEAD_DIM) + eps)
    gdy = g * dy
    dx = r * gdy - x * (r * r * r) * (_headsum(x * gdy, j) * (1.0 / HEAD_DIM))
    return dx, x * r * dy


DIL_STRIDES = tuple(d for _, d in DIL_BRANCHES if d > 1)
DIL_W = DIL_HEADS * HEAD_DIM


def _res_spec(d, n, pad_blocks=0):
    return pl.BlockSpec((d, n, DIL_W), lambda i: (0, i + pad_blocks, 0))


def _prep_fwd(h, gq, gkv, ggq, ggk, tm, tg, j256):
    S = h.shape[0]
    ts = min(256, S)
    scale = HEAD_DIM ** -0.5
    nres = len(DIL_STRIDES)

    def body(*refs):
        (h_ref, gq_ref, gkv_ref, ggq_ref, ggk_ref, cm, smm, spm, cg, smg, spg, j_ref), refs = refs[:12], refs[12:]
        refs = refs[2 * nres:]
        (cq_o, ckv_o, kr_o, dq_o, dk_o, dv_o, gq_o, gk_o, gv_o), res_o = refs[:9], refs[9:-1]
        st = refs[-1]
        tabm = (cm[...], smm[...], spm[...])
        tabg = (cg[...], smg[...], spg[...])
        cq_o[...] = _rms_fwd(h_ref[:, C_CQ:C_CQ + 256], gq_ref[...]).astype(CDT)
        ckv_o[...] = _rms_fwd(h_ref[:, C_CKV:C_CKV + 128], gkv_ref[...]).astype(CDT)
        kr_o[...] = _rope(h_ref[:, C_KR:C_KR + 128], tabm).astype(CDT)
        dq_o[...] = (h_ref[:, C_DQ:C_DQ + 384] * scale).astype(CDT)
        dk_o[...] = h_ref[:, C_DK:C_DK + 384].astype(CDT)
        dv_o[...] = h_ref[:, C_DV:C_DV + 384].astype(CDT)
        for j, lanes in _lane_blocks(3 * DIL_W):
            st[j] = h_ref[:, C_DQ + lanes.start:C_DQ + lanes.stop] * (scale if j < 3 else 1.0)
        for bi, d in enumerate(DIL_STRIDES):
            for c in range(d):
                rows = pl.ds(c, ts // d, stride=d)
                for j, lanes in _lane_blocks(3 * DIL_W):
                    res_o[3 * bi + j // 3][c, :, (j % 3) * LANES:(j % 3 + 1) * LANES] = st.at[j][rows, :].astype(CDT)
        qn = _rms_head_fwd(h_ref[:, C_GQ:C_GQ + 256], ggq_ref[...], j_ref[...])
        gq_o[...] = (_rope(qn, tabg) * scale).astype(CDT)
        kn = _rms_head_fwd(h_ref[:, C_GK:C_GK + 128], ggk_ref[...], j_ref[0:128, 0:128])
        gk_o[...] = _rope(kn, tabg).astype(CDT)
        gv_o[...] = h_ref[:, C_GV:C_GV + 128].astype(CDT)

    widths = (256, 128, 128, 384, 384, 384, 256, 128, 128)
    out_specs = [_row(ts, w) for w in widths]
    out_shape = [_sds((S, w), CDT) for w in widths]
    zeros, aliases = [], {}
    for d in DIL_STRIDES:
        n, L = ts // d, S // d
        out_specs += [_res_spec(d, n), _res_spec(d, n, DIL_HALF // n), _res_spec(d, n, DIL_HALF // n)]
        out_shape += [_sds((d, L, DIL_W), CDT)] + [_sds((d, L + 2 * DIL_HALF, DIL_W), CDT)] * 2
        for t in range(2):
            aliases[12 + len(zeros)] = len(out_shape) - 2 + t
            zeros.append(jnp.zeros((d, L + 2 * DIL_HALF, DIL_W), CDT))
    return pl.pallas_call(
        body, name="prep_fwd", grid=(S // ts,),
        in_specs=[_row(ts, IN_P), _full(gq.shape), _full(gkv.shape), _full(ggq.shape), _full(ggk.shape)]
        + [_row(ts, LANES)] * 6 + [_full(j256.shape)] + [pl.BlockSpec(memory_space=pl.ANY)] * len(zeros),
        out_specs=out_specs, out_shape=out_shape, input_output_aliases=aliases,
        scratch_shapes=[pltpu.VMEM((3 * DIL_W // LANES, ts, LANES), F32)],
        compiler_params=_cparams(1),
    )(h, gq, gkv, ggq, ggk, *tm, *tg, j256, *zeros)


def _prep_bwd(h, dcq, dckv, dkr, ddq, ddk, ddv, dgq, dgk, dgv, gq, gkv, ggq, ggk, tg, j256):
    S = h.shape[0]
    ts = min(256, S)
    scale = HEAD_DIM ** -0.5

    def body(h_ref, dcq_r, dckv_r, dkr_r, q1, q2, q3, k1, k2, k3, v1, v2, v3, dgq_r, dgk_r, dgv_r,
             gq_ref, gkv_ref, ggq_ref, ggk_ref, cg, smg, spg, j_ref,
             dh_o, ngq_o, ngkv_o, nggq_o, nggk_o, *scr):
        tabg = (cg[...], smg[...], spg[...])
        first = pl.program_id(0) == 0
        scr, = scr
        d2, d3 = DIL_STRIDES
        dq = q1[...] + _by_token(q2, scr, d2) + _by_token(q3, scr, d3)
        dk = k1[...] + _by_token(k2, scr, d2) + _by_token(k3, scr, d3)
        dv = v1[...] + _by_token(v2, scr, d2) + _by_token(v3, scr, d3)

        def acc(o_ref, val):
            s = jnp.sum(val, axis=0, keepdims=True)

            @pl.when(first)
            def _():
                o_ref[...] = s

            @pl.when(jnp.logical_not(first))
            def _():
                o_ref[...] += s

        dx, dg = _rms_bwd(h_ref[:, C_CQ:C_CQ + 256], gq_ref[...], dcq_r[...])
        dh_o[:, C_CQ:C_CQ + 256] = dx.astype(CDT)
        acc(ngq_o, dg)
        dx, dg = _rms_bwd(h_ref[:, C_CKV:C_CKV + 128], gkv_ref[...], dckv_r[...])
        dh_o[:, C_CKV:C_CKV + 128] = dx.astype(CDT)
        acc(ngkv_o, dg)
        dh_o[:, C_KR:C_KR + 128] = dkr_r[...].astype(CDT)
        dh_o[:, C_DQ:C_DQ + 384] = (dq * scale).astype(CDT)
        dh_o[:, C_DK:C_DK + 384] = dk.astype(CDT)
        dh_o[:, C_DV:C_DV + 384] = dv.astype(CDT)
        dqn = _rope_t(dgq_r[...] * scale, tabg)
        dx, dg = _rms_head_bwd(h_ref[:, C_GQ:C_GQ + 256], ggq_ref[...], dqn, j_ref[...])
        dh_o[:, C_GQ:C_GQ + 256] = dx.astype(CDT)
        acc(nggq_o, dg)
        dkn = _rope_t(dgk_r[...], tabg)
        dx, dg = _rms_head_bwd(h_ref[:, C_GK:C_GK + 128], ggk_ref[...], dkn, j_ref[0:128, 0:128])
        dh_o[:, C_GK:C_GK + 128] = dx.astype(CDT)
        acc(nggk_o, dg)
        dh_o[:, C_GV:C_GV + 128] = dgv_r[...].astype(CDT)

    d2, d3 = DIL_STRIDES
    n2, n3 = ts // d2, ts // d3
    tok = _row(ts, DIL_W)
    return pl.pallas_call(
        body, name="prep_bwd", grid=(S // ts,),
        in_specs=[_row(ts, IN_P), _row(ts, 256), _row(ts, 128), _row(ts, 128)]
        + [tok, _res_spec(d2, n2), _res_spec(d3, n3)]
        + [tok, _res_spec(d2, n2, DIL_HALF // n2), _res_spec(d3, n3, DIL_HALF // n3)] * 2
        + [_row(ts, 256), _row(ts, 128), _row(ts, 128)]
        + [_full(gq.shape), _full(gkv.shape), _full(ggq.shape), _full(ggk.shape)] + [_row(ts, LANES)] * 3
        + [_full(j256.shape)],
        out_specs=[_row(ts, IN_P), _full((1, 256)), _full((1, 128)), _full((1, 256)), _full((1, 128))],
        out_shape=[_sds((S, IN_P), CDT), _sds((1, 256), F32), _sds((1, 128), F32), _sds((1, 256), F32),
                   _sds((1, 128), F32)],
        scratch_shapes=[_TOKEN_SCRATCH(ts)],
        compiler_params=_cparams(1),
    )(h, dcq, dckv, dkr, *ddq, *ddk, *ddv, dgq, dgk, dgv, gq, gkv, ggq, ggk, *tg, j256)


def _lane_blocks(width):
    return [(j, slice(j * LANES, (j + 1) * LANES)) for j in range(width // LANES)]


_TOKEN_SCRATCH = lambda ts: pltpu.VMEM((DIL_W // LANES, ts, LANES), F32)


def _by_token(res_ref, scr_ref, d):
    n = res_ref.shape[1]
    if d == 1:
        return res_ref[0].astype(F32)
    for c in range(d):
        for j, lanes in _lane_blocks(res_ref.shape[2]):
            scr_ref.at[j][pl.ds(c, n, stride=d), :] = res_ref[c, :, lanes].astype(F32)
    return jnp.concatenate([scr_ref[j] for j, _ in _lane_blocks(res_ref.shape[2])], axis=1)


def _by_residue(val, scr_ref, out_ref, d):
    n = out_ref.shape[1]
    if d == 1:
        out_ref[0] = val.astype(out_ref.dtype)
        return
    for j, lanes in _lane_blocks(out_ref.shape[2]):
        scr_ref[j] = val[:, lanes]
    for c in range(d):
        for j, lanes in _lane_blocks(out_ref.shape[2]):
            out_ref[c, :, lanes] = scr_ref.at[j][pl.ds(c, n, stride=d), :].astype(out_ref.dtype)


def _mla_prep_fwd(qa, kvp, kr, tm, scale):
    S = qa.shape[0]
    ts = min(256, S)

    def body(qa_ref, kv_ref, kr_ref, cm, smm, spm, q_o, k_o):
        tabm = (cm[...], smm[...], spm[...])
        q_o[...] = (_rope(qa_ref[...], tabm) * scale).astype(CDT)
        k_o[...] = kv_ref[:, 0:768] + _lanes(kr_ref[...], 768)

    return pl.pallas_call(
        body, name="mla_prep_fwd", grid=(S // ts,),
        in_specs=[_row(ts, 768), _row(ts, 1152), _row(ts, 128)] + [_row(ts, LANES)] * 3,
        out_specs=[_row(ts, 768)] * 2, out_shape=[_sds((S, 768), CDT)] * 2,
        compiler_params=_cparams(1),
    )(qa, kvp, kr, *tm)


def _mla_prep_bwd(dq, dk, dv, tm, scale):
    S = dq.shape[0]
    ts = min(256, S)

    def body(dq_ref, dk_ref, dv_ref, cm, smm, spm, dqa_o, dkv_o, dkr_o):
        tabm = (cm[...], smm[...], spm[...])
        lane = lax.broadcasted_iota(jnp.int32, (1, LANES), 1)
        dqa_o[...] = _rope_t(dq_ref[...] * scale, tabm).astype(CDT)
        dkr = jnp.zeros((ts, LANES), F32)
        for hd in range(MLA_HEADS):
            blk = dk_ref[:, hd * 128:(hd + 1) * 128]
            dkv_o[:, hd * 128:(hd + 1) * 128] = jnp.where(lane < 64, blk, 0.0).astype(CDT)
            dkr = dkr + jnp.where((lane >= 64) & (lane < 96), blk, 0.0)
        dkv_o[:, 768:1152] = dv_ref[...].astype(CDT)
        dkr_o[...] = jnp.where((lane >= 64) & (lane < 96), _rope_t(dkr, tabm), 0.0)

    return pl.pallas_call(
        body, name="mla_prep_bwd", grid=(S // ts,),
        in_specs=[_row(ts, 768), _row(ts, 768), _row(ts, 384)] + [_row(ts, LANES)] * 3,
        out_specs=[_row(ts, 768), _row(ts, 1152), _row(ts, 128)],
        out_shape=[_sds((S, 768), CDT), _sds((S, 1152), CDT), _sds((S, 128), F32)],
        compiler_params=_cparams(1),
    )(dq, dk, dv, *tm)


def _ln_fwd(xa, xb, g, b, alpha, name):
    S, D = xa.shape
    ts = min(512, S)

    def body(xa_ref, xb_ref, g_ref, b_ref, y_o, yb_o, z_o):
        z = alpha * xa_ref[...] + xb_ref[...]
        mu = jnp.mean(z, axis=-1, keepdims=True)
        zc = z - mu
        var = jnp.mean(zc * zc, axis=-1, keepdims=True)
        y = zc * lax.rsqrt(var + 1e-5) * g_ref[...] + b_ref[...]
        y_o[...] = y
        yb_o[...] = y.astype(CDT)
        z_o[...] = z

    return pl.pallas_call(
        body, name=name, grid=(S // ts,),
        in_specs=[_row(ts, D), _row(ts, D), _full(g.shape), _full(b.shape)],
        out_specs=[_row(ts, D)] * 3, out_shape=[_sds((S, D), F32), _sds((S, D), CDT), _sds((S, D), F32)],
        compiler_params=_cparams(1),
    )(xa, xb, g, b)


def _ln_bwd(dya, dyb, z, g, alpha, name, loss_head=False):
    S, D = z.shape
    ts = min(512, S)

    def body(dya_ref, dyb_ref, z_ref, g_ref, dz_o, dzb_o, dg_o, db_o, *loss_o):
        first = pl.program_id(0) == 0
        if loss_head:
            err = dya_ref[...] - dyb_ref[...]
            dy = err * (1.0 / D)
            part = jnp.sum(jnp.sum(err * err, axis=1, keepdims=True), axis=0, keepdims=True) * (0.5 / D)

            @pl.when(first)
            def _():
                loss_o[0][...] = part

            @pl.when(jnp.logical_not(first))
            def _():
                loss_o[0][...] += part
        else:
            dy = dya_ref[...] + alpha * dyb_ref[...]
        z = z_ref[...]
        mu = jnp.mean(z, axis=-1, keepdims=True)
        zc = z - mu
        r = lax.rsqrt(jnp.mean(zc * zc, axis=-1, keepdims=True) + 1e-5)
        xh = zc * r
        dxh = dy * g_ref[...]
        dz = r * (dxh - jnp.mean(dxh, axis=-1, keepdims=True) - xh * jnp.mean(dxh * xh, axis=-1, keepdims=True))
        dz_o[...] = dz
        dzb_o[...] = dz.astype(CDT)
        sg = jnp.sum(dy * xh, axis=0, keepdims=True)
        sb = jnp.sum(dy, axis=0, keepdims=True)

        @pl.when(first)
        def _():
            dg_o[...] = sg
            db_o[...] = sb

        @pl.when(jnp.logical_not(first))
        def _():
            dg_o[...] += sg
            db_o[...] += sb

    extra = ([_full((1, 1))], [_sds((1, 1), F32)]) if loss_head else ([], [])
    return pl.pallas_call(
        body, name=name, grid=(S // ts,),
        in_specs=[_row(ts, D)] * 3 + [_full(g.shape)],
        out_specs=[_row(ts, D), _row(ts, D), _full((1, D)), _full((1, D))] + extra[0],
        out_shape=[_sds((S, D), F32), _sds((S, D), CDT), _sds((1, D), F32), _sds((1, D), F32)] + extra[1],
        compiler_params=_cparams(1),
    )(dya, dyb, z, g)


def _grp_spec(ts, w):
    return pl.BlockSpec((None, ts, w), lambda k, i: (k, i, 0))


def _ffn_up(xb, wg3, wu3):
    S, D = xb.shape
    G, Fc, _ = wg3.shape
    tm = _pick(S, 1024)
    wspec = pl.BlockSpec((None, Fc, D), lambda k, i: (k, 0, 0))

    def body(x_ref, wg_ref, wu_ref, g_o, u_o, a_o):
        x = x_ref[...]
        g = lax.dot_general(x, wg_ref[...], _NT, preferred_element_type=F32)
        u = lax.dot_general(x, wu_ref[...], _NT, preferred_element_type=F32)
        g_o[...] = g.astype(CDT)
        u_o[...] = u.astype(CDT)
        a_o[...] = (g / (1.0 + jnp.exp(-g)) * u).astype(CDT)

    return pl.pallas_call(
        body, name="ffn_up", grid=(G, S // tm),
        in_specs=[pl.BlockSpec((tm, D), lambda k, i: (i, 0)), wspec, wspec], out_specs=[_grp_spec(tm, Fc)] * 3,
        out_shape=[_sds((G, S, Fc), CDT)] * 3, compiler_params=_cparams(2),
    )(xb, wg3, wu3)


def _ffn_up_dx(dg3, du3, wg3, wu3):
    G, S, Fc = dg3.shape
    D = wg3.shape[2]
    tm = _pick(S, 512)
    wspec = pl.BlockSpec((G, Fc, D), lambda i: (0, 0, 0))
    aspec = pl.BlockSpec((G, tm, Fc), lambda i: (0, i, 0))

    def body(dg_ref, du_ref, wg_ref, wu_ref, o_ref):
        acc = None
        for k in range(G):
            for a_ref, w_ref in ((dg_ref, wg_ref), (du_ref, wu_ref)):
                part = jnp.dot(a_ref[k], w_ref[k], preferred_element_type=F32)
                acc = part if acc is None else acc + part
        o_ref[...] = acc

    return pl.pallas_call(
        body, name="ffn_up_dx", grid=(S // tm,), in_specs=[aspec, aspec, wspec, wspec],
        out_specs=pl.BlockSpec((tm, D), lambda i: (i, 0)), out_shape=_sds((S, D), F32), compiler_params=_cparams(1),
    )(dg3, du3, wg3, wu3)


def _ffn_down(act3, wd3):
    G, S, Fc = act3.shape
    D = wd3.shape[2]
    tm = _pick(S, 1024)

    def body(a_ref, w_ref, o_ref):
        acc = jnp.dot(a_ref[0], w_ref[0], preferred_element_type=F32)
        for k in range(1, G):
            acc = acc + jnp.dot(a_ref[k], w_ref[k], preferred_element_type=F32)
        o_ref[...] = acc

    return pl.pallas_call(
        body, name="ffn_down", grid=(S // tm,),
        in_specs=[pl.BlockSpec((G, tm, Fc), lambda i: (0, i, 0)), pl.BlockSpec((G, Fc, D), lambda i: (0, 0, 0))],
        out_specs=pl.BlockSpec((tm, D), lambda i: (i, 0)), out_shape=_sds((S, D), F32), compiler_params=_cparams(1),
    )(act3, wd3)


def _ffn_down_dx(dzb, wd3, g3, u3):
    S, D = dzb.shape
    G, Fc, _ = wd3.shape
    tm = _pick(S, 1024)

    def body(dz_ref, wd_ref, g_ref, u_ref, dg_o, du_o):
        da = lax.dot_general(dz_ref[...], wd_ref[...], _NT, preferred_element_type=F32)
        g = g_ref[...].astype(F32)
        sg = 1.0 / (1.0 + jnp.exp(-g))
        dg_o[...] = (da * u_ref[...].astype(F32) * (sg * (1.0 + g * (1.0 - sg)))).astype(CDT)
        du_o[...] = (da * (g * sg)).astype(CDT)

    return pl.pallas_call(
        body, name="ffn_down_dx", grid=(G, S // tm),
        in_specs=[pl.BlockSpec((tm, D), lambda k, i: (i, 0)), pl.BlockSpec((None, Fc, D), lambda k, i: (k, 0, 0)),
                  _grp_spec(tm, Fc), _grp_spec(tm, Fc)],
        out_specs=[_grp_spec(tm, Fc)] * 2, out_shape=[_sds((G, S, Fc), CDT)] * 2, compiler_params=_cparams(2),
    )(dzb, wd3, g3, u3)


def _axpy(a, b, alpha, name):
    S, D = a.shape
    ts = min(512, S)

    def body(a_ref, b_ref, o_ref):
        o_ref[...] = a_ref[...] + alpha * b_ref[...]

    return pl.pallas_call(
        body, name=name, grid=(S // ts,), in_specs=[_row(ts, D)] * 2, out_specs=_row(ts, D),
        out_shape=_sds((S, D), F32), compiler_params=_cparams(1),
    )(a, b)


def _pair_masks():
    lane = lax.broadcasted_iota(jnp.int32, (1, LANES), 1)
    first = lane < HEAD_DIM
    return first, jnp.logical_not(first)


def _head_scalar(x, m):
    return jnp.max(jnp.where(m, x, -jnp.inf), axis=-1, keepdims=True)


_NT = (((1,), (1,)), ((), ()))
_TN = (((0,), (0,)), ((), ()))
ATTN_KEY_CHUNK = 1024
ATTN_FWD_TILES_PER_STEP = 2


def _attn_fwd(q, k, v, *, split, npairs, kblk, vblk, name):
    S = q.shape[0]
    qw = 256 if split else LANES
    tq = min(256, S)
    nsub = ATTN_FWD_TILES_PER_STEP if S % (ATTN_FWD_TILES_PER_STEP * tq) == 0 else 1

    def body(q_ref, k_ref, v_ref, o_ref, lse_ref):
        masks = _pair_masks()
        for j in range(nsub):
            rows = slice(j * tq, (j + 1) * tq)
            outs, lses = [], []
            for hd in range(2):
                if split:
                    qh = q_ref[rows, hd * LANES:(hd + 1) * LANES]
                    kh = k_ref[:, hd * LANES:(hd + 1) * LANES]
                else:
                    qh = jnp.where(masks[hd], q_ref[rows, :], jnp.zeros_like(q_ref[rows, :]))
                    kh = k_ref[...]
                s = lax.dot_general(qh, kh, _NT, preferred_element_type=F32)
                mx = jnp.max(s, axis=-1, keepdims=True)
                p = jnp.exp(s - mx)
                l = jnp.sum(p, axis=-1, keepdims=True)
                o = jnp.dot(p.astype(CDT), v_ref[...], preferred_element_type=F32)
                outs.append(o / l)
                lses.append(jnp.broadcast_to(mx + jnp.log(l), (tq, LANES)))
            o_ref[rows, :] = jnp.where(masks[0], outs[0], outs[1]).astype(o_ref.dtype)
            lse_ref[rows, :] = jnp.where(masks[0], lses[0], lses[1])

    return pl.pallas_call(
        body, name=name, grid=(npairs, S // (nsub * tq)),
        in_specs=[pl.BlockSpec((nsub * tq, qw), lambda p, i: (i, p)),
                  pl.BlockSpec((S, qw), lambda p, i: (0, kblk(p))),
                  pl.BlockSpec((S, LANES), lambda p, i: (0, vblk(p)))],
        out_specs=[pl.BlockSpec((nsub * tq, LANES), lambda p, i: (i, p))] * 2,
        out_shape=[_sds((S, LANES * npairs), CDT), _sds((S, LANES * npairs), F32)],
        compiler_params=_cparams(2),
    )(q, k, v)


def _attn_bwd(q, k, v, do, o, lse, *, split, npairs, kblk, vblk, doblk, shared_kv, name):
    S = q.shape[0]
    qw = 256 if split else LANES
    tq = min(512, S)
    tkv = min(ATTN_KEY_CHUNK, S)
    nkv = 1 if shared_kv else npairs

    def body(q_ref, k_ref, v_ref, do_ref, o_ref, lse_ref, dq_ref, dk_ref, dv_ref):
        masks = _pair_masks()
        p_id, i_id = pl.program_id(0), pl.program_id(1)
        first = (i_id == 0) & ((p_id == 0) if shared_kv else True)
        @pl.when(first)
        def _():
            dk_ref[...] = jnp.zeros_like(dk_ref)
            dv_ref[...] = jnp.zeros_like(dv_ref)

        do = do_ref[...]
        o = o_ref[...].astype(F32)
        lse = lse_ref[...]
        heads = []
        for hd in range(2):
            m = masks[hd]
            cols = slice(hd * LANES, (hd + 1) * LANES) if split else slice(None)
            qh = q_ref[:, cols] if split else jnp.where(m, q_ref[...], jnp.zeros_like(q_ref[...]))
            doh = jnp.where(m, do, 0.0)
            heads.append((m, cols, qh, doh.astype(CDT), _head_scalar(lse, m), jnp.sum(doh * o, axis=-1, keepdims=True)))
        dqs = [jnp.zeros((tq, LANES), F32), jnp.zeros((tq, LANES), F32)]
        for ck in range(S // tkv):
            rows = slice(ck * tkv, (ck + 1) * tkv)
            v = v_ref[rows, :]
            dv = jnp.zeros((tkv, LANES), F32)
            for hd, (m, cols, qh, dohb, lse_h, delta) in enumerate(heads):
                kh = k_ref[rows, cols]
                s = lax.dot_general(qh, kh, _NT, preferred_element_type=F32)
                p = jnp.exp(s - lse_h)
                dp = lax.dot_general(dohb, v, _NT, preferred_element_type=F32)
                ds = (p * (dp - delta)).astype(CDT)
                dq = jnp.dot(ds, kh, preferred_element_type=F32)
                dqs[hd] = dqs[hd] + (dq if split else jnp.where(m, dq, 0.0))
                dk_ref[rows, cols] += lax.dot_general(ds, qh, _TN, preferred_element_type=F32)
                dv = dv + lax.dot_general(p.astype(CDT), dohb, _TN, preferred_element_type=F32)
            dv_ref[rows, :] += dv
        if split:
            dq_ref[:, 0:LANES] = dqs[0]
            dq_ref[:, LANES:2 * LANES] = dqs[1]
        else:
            dq_ref[...] = dqs[0] + dqs[1]

    kvo = (lambda p, i: (0, 0)) if shared_kv else (lambda p, i: (0, p))
    return pl.pallas_call(
        body, name=name, grid=(npairs, S // tq),
        in_specs=[pl.BlockSpec((tq, qw), lambda p, i: (i, p)),
                  pl.BlockSpec((S, qw), lambda p, i: (0, kblk(p))),
                  pl.BlockSpec((S, LANES), lambda p, i: (0, vblk(p))),
                  pl.BlockSpec((tq, LANES), lambda p, i: (i, doblk(p))),
                  pl.BlockSpec((tq, LANES), lambda p, i: (i, p)),
                  pl.BlockSpec((tq, LANES), lambda p, i: (i, p))],
        out_specs=[pl.BlockSpec((tq, qw), lambda p, i: (i, p)),
                   pl.BlockSpec((S, qw), kvo), pl.BlockSpec((S, LANES), kvo)],
        out_shape=[_sds((S, qw * npairs), F32), _sds((S, qw * nkv), F32), _sds((S, LANES * nkv), F32)],
        compiler_params=_cparams(2),
    )(q, k, v, do, o, lse)


def _bias_expand(idx, rel_bias, name):
    tq, kw = idx.shape

    def body(idx_ref, rb_ref, o_ref):
        idx = idx_ref[...]
        for hd in range(DIL_HEADS):
            acc = jnp.full((tq, kw), NEG_INF, F32)
            for u in range(REL_BUCKETS):
                acc = jnp.where(idx == u, rb_ref[u, hd], acc)
            o_ref[hd] = acc

    return pl.pallas_call(
        body, name=name,
        in_specs=[pl.BlockSpec(memory_space=pltpu.VMEM), pl.BlockSpec(memory_space=pltpu.SMEM)],
        out_specs=pl.BlockSpec(memory_space=pltpu.VMEM),
        out_shape=_sds((DIL_HEADS, tq, kw), F32),
    )(idx, rel_bias)


def _bias_reduce(idx, dtab, name):
    tq, kw = idx.shape

    def body(idx_ref, d_ref, o_ref):
        idx = idx_ref[...]
        rowid = lax.broadcasted_iota(jnp.int32, (REL_BUCKETS, kw), 0)
        for hd in range(DIL_HEADS):
            d = d_ref[hd]
            acc = jnp.zeros((REL_BUCKETS, kw), F32)
            for u in range(REL_BUCKETS):
                r = jnp.sum(jnp.where(idx == u, d, 0.0), axis=0, keepdims=True)
                acc = jnp.where(rowid == u, r, acc)
            o_ref[hd] = jnp.sum(acc, axis=1, keepdims=True)

    return pl.pallas_call(
        body, name=name,
        in_specs=[pl.BlockSpec(memory_space=pltpu.VMEM)] * 2, out_specs=pl.BlockSpec(memory_space=pltpu.VMEM),
        out_shape=_sds((DIL_HEADS, REL_BUCKETS, 1), F32),
    )(idx, dtab)


DIL_TILES_PER_STEP = 4


def _dil_tiles_per_step(L, tq):
    return DIL_TILES_PER_STEP if L % (DIL_TILES_PER_STEP * tq) == 0 else 1


def _dil_window(i, tq, kw, L):
    start = pl.multiple_of(i * tq, DIL_HALF)
    key = start + lax.broadcasted_iota(jnp.int32, (1, kw), 1) - DIL_HALF
    return start, (key >= 0) & (key < L)


def _dil_fwd(qv, kv, vv, tab, *, dil, L, tq, name):
    kw = tq + 2 * DIL_HALF
    npair = DIL_HEADS // 2
    nsub = _dil_tiles_per_step(L, tq)

    def body(q_ref, k_ref, v_ref, t_ref, o_ref, lse_ref):
        masks = _pair_masks()
        for j in range(nsub):
            rows = slice(j * tq, (j + 1) * tq)
            start, valid = _dil_window(pl.program_id(2) * nsub + j, tq, kw, L)
            kwin = k_ref[pl.ds(start, kw), :]
            vwin = v_ref[pl.ds(start, kw), :]
            q = q_ref[rows, :]
            outs, lses = [], []
            for hd in range(2):
                qh = jnp.where(masks[hd], q, jnp.zeros_like(q))
                s = lax.dot_general(qh, kwin, _NT, preferred_element_type=F32) + t_ref[hd]
                s = jnp.where(valid, s, NEG_INF)
                mx = jnp.max(s, axis=-1, keepdims=True)
                p = jnp.exp(s - mx)
                l = jnp.sum(p, axis=-1, keepdims=True)
                outs.append(jnp.dot(p.astype(CDT), vwin, preferred_element_type=F32) / l)
                lses.append(jnp.broadcast_to(mx + jnp.log(l), (tq, LANES)))
            o_ref[rows, :] = jnp.where(masks[0], outs[0], outs[1])
            lse_ref[rows, :] = jnp.where(masks[0], lses[0], lses[1])

    blk = pl.BlockSpec((None, nsub * tq, LANES), lambda p, c, i: (c, i, p))
    res = pl.BlockSpec((None, L + 2 * DIL_HALF, LANES), lambda p, c, i: (c, 0, p))
    return pl.pallas_call(
        body, name=name, grid=(npair, dil, L // (nsub * tq)),
        in_specs=[blk, res, res, pl.BlockSpec((2, tq, kw), lambda p, c, i: (p, 0, 0))],
        out_specs=[blk] * 2, out_shape=[_sds(qv.shape, F32)] * 2,
        compiler_params=_cparams(3),
    )(qv, kv, vv, tab)


def _dil_bwd(qv, kv, vv, tab, dov, lsev, deltav, *, dil, L, tq, name):
    kw = tq + 2 * DIL_HALF
    npair = DIL_HEADS // 2
    nsub = _dil_tiles_per_step(L, tq)

    def body(q_ref, k_ref, v_ref, t_ref, do_ref, lse_ref, dl_ref, dq_ref, dk_ref, dv_ref, dt_ref):
        masks = _pair_masks()
        c_id, i_id = pl.program_id(1), pl.program_id(2)

        @pl.when(i_id == 0)
        def _():
            dk_ref[...] = jnp.zeros_like(dk_ref)
            dv_ref[...] = jnp.zeros_like(dv_ref)

        @pl.when((i_id == 0) & (c_id == 0))
        def _():
            dt_ref[...] = jnp.zeros_like(dt_ref)

        dts = [jnp.zeros((tq, kw), F32), jnp.zeros((tq, kw), F32)]
        for j in range(nsub):
            rows = slice(j * tq, (j + 1) * tq)
            start, valid = _dil_window(i_id * nsub + j, tq, kw, L)
            kwin = k_ref[pl.ds(start, kw), :]
            vwin = v_ref[pl.ds(start, kw), :]
            q, do, lse, dl = q_ref[rows, :], do_ref[rows, :], lse_ref[rows, :], dl_ref[rows, :]
            dq = jnp.zeros((tq, LANES), F32)
            dk = jnp.zeros((kw, LANES), F32)
            dv = jnp.zeros((kw, LANES), F32)
            for hd in range(2):
                m = masks[hd]
                qh = jnp.where(m, q, jnp.zeros_like(q))
                doh = jnp.where(m, do, jnp.zeros_like(do))
                s = lax.dot_general(qh, kwin, _NT, preferred_element_type=F32) + t_ref[hd]
                s = jnp.where(valid, s, NEG_INF)
                p = jnp.exp(s - _head_scalar(lse, m))
                dp = lax.dot_general(doh, vwin, _NT, preferred_element_type=F32)
                ds = p * (dp - _head_scalar(dl, m))
                dts[hd] = dts[hd] + ds
                dsb = ds.astype(CDT)
                dq = dq + jnp.where(m, jnp.dot(dsb, kwin, preferred_element_type=F32), 0.0)
                dk = dk + lax.dot_general(dsb, qh, _TN, preferred_element_type=F32)
                dv = dv + lax.dot_general(p.astype(CDT), doh, _TN, preferred_element_type=F32)
            dq_ref[rows, :] = dq
            dk_ref[pl.ds(start, kw), :] += dk
            dv_ref[pl.ds(start, kw), :] += dv
        for hd in range(2):
            dt_ref[hd] += dts[hd]

    blk = pl.BlockSpec((None, nsub * tq, LANES), lambda p, c, i: (c, i, p))
    res = pl.BlockSpec((None, L + 2 * DIL_HALF, LANES), lambda p, c, i: (c, 0, p))
    tsp = pl.BlockSpec((2, tq, kw), lambda p, c, i: (p, 0, 0))
    return pl.pallas_call(
        body, name=name, grid=(npair, dil, L // (nsub * tq)),
        in_specs=[blk, res, res, tsp, blk, blk, blk], out_specs=[blk, res, res, tsp],
        out_shape=[_sds(qv.shape, F32), _sds(kv.shape, F32), _sds(kv.shape, F32), _sds(tab.shape, F32)],
        compiler_params=_cparams(3),
    )(qv, kv, vv, tab, dov, lsev, deltav)


def _mix_weights(l1, l2, l3):
    mx = jnp.maximum(jnp.maximum(l1, l2), l3)
    e1, e2, e3 = jnp.exp(l1 - mx), jnp.exp(l2 - mx), jnp.exp(l3 - mx)
    inv = 1.0 / (e1 + e2 + e3)
    return e1 * inv, e2 * inv, e3 * inv


def _branch_specs(S, ts):
    dils = [d for _, d in DIL_BRANCHES]
    return dils, [_res_spec(d, ts // d) for d in dils], [(d, S // d, DIL_W) for d in dils]


def _dil_mix_fwd(os, ls):
    S = os[0].shape[0] * os[0].shape[1]
    ts = min(256, S)
    dils, specs, _ = _branch_specs(S, ts)

    def body(o1, o2, o3, l1, l2, l3, out, scr):
        o1, o2, o3, l1, l2, l3 = [_by_token(r, scr, d) for r, d in zip((o1, o2, o3, l1, l2, l3), dils + dils)]
        w1, w2, w3 = _mix_weights(l1, l2, l3)
        out[...] = (w1 * o1 + w2 * o2 + w3 * o3).astype(CDT)

    return pl.pallas_call(
        body, name="dil_mix_fwd", grid=(S // ts,), in_specs=specs + specs, out_specs=_row(ts, DIL_W),
        out_shape=_sds((S, DIL_W), CDT), scratch_shapes=[_TOKEN_SCRATCH(ts)],
        compiler_params=_cparams(1),
    )(*os, *ls)


def _dil_mix_bwd(dcat, os, ls, j384):
    S = os[0].shape[0] * os[0].shape[1]
    ts = min(256, S)
    dils, specs, shapes = _branch_specs(S, ts)

    def body(do_ref, o1, o2, o3, l1, l2, l3, j_ref, d1, d2, d3, e1, e2, e3, scr):
        o1, o2, o3, l1, l2, l3 = [_by_token(r, scr, d) for r, d in zip((o1, o2, o3, l1, l2, l3), dils + dils)]
        ws = _mix_weights(l1, l2, l3)
        do = do_ref[...]
        o = ws[0] * o1 + ws[1] * o2 + ws[2] * o3
        dot = _headsum(do * o, j_ref[...])
        for w, d, d_o, e_o in zip(ws, dils, (d1, d2, d3), (e1, e2, e3)):
            _by_residue(w * do, scr, d_o, d)
            _by_residue(w * dot, scr, e_o, d)

    return pl.pallas_call(
        body, name="dil_mix_bwd", grid=(S // ts,),
        in_specs=[_row(ts, DIL_W, 1)] + specs + specs + [_full(j384.shape)],
        out_specs=specs + specs,
        out_shape=[_sds(s, CDT) for s in shapes] + [_sds(s, F32) for s in shapes],
        scratch_shapes=[_TOKEN_SCRATCH(ts)],
        compiler_params=_cparams(1),
    )(dcat, *os, *ls, j384)


def _adamw_math(w, g, m, v):
    m = ADAM_B1 * m + (1.0 - ADAM_B1) * g
    v = ADAM_B2 * v + (1.0 - ADAM_B2) * (g * g)
    m_hat = m / (1.0 - ADAM_B1 ** ADAM_STEP)
    v_hat = v / (1.0 - ADAM_B2 ** ADAM_STEP)
    delta = -ADAM_LR * (m_hat / (jnp.sqrt(v_hat) + ADAM_EPS) + ADAM_WD * w)
    return delta, m, v


def _pick8(n, target):
    best = None
    for t in range(16, min(n, target) + 1, 16):
        if n % t == 0:
            best = t
    return best if best is not None else n


_ELEMS_PER_BLOCK = 256 * 1024


def _adamw(w, reds, sibs, m, v, owner, name):
    L, a, b = w.shape
    ta = _pick8(a, max(16, _ELEMS_PER_BLOCK // b))
    spec = pl.BlockSpec((None, ta, b), lambda l, i, own: (l, i, 0))
    def gspec(mine, layer):
        def index(l, i, own):
            use = (own[0] if mine else 1 - own[0]) * (l if layer else 1 - l)
            return i * use, 0
        return pl.BlockSpec((ta, b), index)

    def body(own_ref, w_ref, r0_ref, r1_ref, s0_ref, s1_ref, m_ref, v_ref, g_o, d_o, m_o, v_o):
        mine = own_ref[0] == 1
        g0 = jnp.where(mine, r0_ref[...], s0_ref[...])
        g1 = jnp.where(mine, r1_ref[...], s1_ref[...])
        g = jnp.where(pl.program_id(0) == 0, g0, g1)
        d, mm, vv = _adamw_math(w_ref[...], g, m_ref[...], v_ref[...])
        g_o[...] = g
        d_o[...] = d
        m_o[...] = mm
        v_o[...] = vv

    return pl.pallas_call(
        body, name=name, out_shape=[_sds(w.shape, F32)] * 4,
        grid_spec=pltpu.PrefetchScalarGridSpec(
            num_scalar_prefetch=1, grid=(L, a // ta),
            in_specs=[spec, gspec(True, 0), gspec(True, 1), gspec(False, 0), gspec(False, 1), spec, spec],
            out_specs=[spec] * 4),
        compiler_params=_cparams(2),
    )(_is_core(owner), w, *reds, *sibs, m, v)


def _adamw_small(w, gall, m, v):
    R = w.shape[0]

    def body(w_ref, g_ref, m_ref, v_ref, g_o, d_o, m_o, v_o):
        g = g_ref[0]
        for k in range(1, 8):
            g = g + g_ref[k]
        d, mm, vv = _adamw_math(w_ref[...], g, m_ref[...], v_ref[...])
        g_o[...] = g
        d_o[...] = d
        m_o[...] = mm
        v_o[...] = vv

    vm = pl.BlockSpec(memory_space=pltpu.VMEM)
    return pl.pallas_call(
        body, name="adamw_small", in_specs=[vm] * 4, out_specs=[vm] * 4, out_shape=[_sds((R, LANES), F32)] * 4,
    )(w, gall, m, v)


def _sum_pair(g, t, owner, name):
    n, a, b = t.shape
    ta = _pick8(a, max(16, _ELEMS_PER_BLOCK // b))
    spec = pl.BlockSpec((None, ta, b), lambda k, i, own: (k * own[0], i * own[0], 0))

    def body(own_ref, g_ref, t_ref, o_ref):
        @pl.when(own_ref[0] == 1)
        def _():
            o_ref[...] = (g_ref[...].astype(F32) + t_ref[...].astype(F32)).astype(o_ref.dtype)

    return pl.pallas_call(
        body, name=name, out_shape=_sds(t.shape, WIRE),
        grid_spec=pltpu.PrefetchScalarGridSpec(num_scalar_prefetch=1, grid=(n, a // ta), in_specs=[spec] * 2,
                                               out_specs=spec),
        compiler_params=_cparams(2),
    )(_is_core(owner), g, t)


def _sum_chips(pair, t, owner, name):
    _, a, b = t.shape
    ta = _pick8(a, max(16, _ELEMS_PER_BLOCK // b))

    def body(own_ref, p_ref, t_ref, o_ref):
        @pl.when(own_ref[0] == 1)
        def _():
            me = 2 * lax.axis_index("x") + lax.axis_index("y")
            acc = p_ref[me].astype(F32)
            for k in range(3):
                acc = acc + t_ref[k].astype(F32)
            o_ref[...] = acc

    return pl.pallas_call(
        body, name=name, out_shape=_sds((a, b), F32),
        grid_spec=pltpu.PrefetchScalarGridSpec(
            num_scalar_prefetch=1, grid=(a // ta,),
            in_specs=[pl.BlockSpec((4, ta, b), lambda i, own: (0, i * own[0], 0)),
                      pl.BlockSpec((3, ta, b), lambda i, own: (0, i * own[0], 0))],
            out_specs=pl.BlockSpec((ta, b), lambda i, own: (i * own[0], 0))),
        compiler_params=_cparams(1),
    )(_is_core(owner), pair, t)


def _is_core(core):
    return (lax.axis_index("c") == core).astype(jnp.int32).reshape(1)


_HBM = pl.BlockSpec(memory_space=pltpu.HBM)


def _place():
    x, y, c = lax.axis_index("x"), lax.axis_index("y"), lax.axis_index("c")
    chips = [(1 - x, y), (x, 1 - y), (1 - x, 1 - y)]
    return x, y, c, chips


def _remote(src, dst, ssem, rsem, to):
    return pltpu.make_async_remote_copy(src_ref=src, dst_ref=dst, send_sem=ssem, recv_sem=rsem, device_id=to,
                                        device_id_type=MESH_ID)


def _dma_sems(n):
    return pltpu.SemaphoreType.DMA((n,))


_SEM = pl.BlockSpec(memory_space=pltpu.SEMAPHORE)
_ANY = pl.BlockSpec(memory_space=pl.ANY)
_EFFECT = pltpu.SideEffectType.DATAFLOW_SIDE_EFFECTING
_BIG = ("w_in", "mla_w_uq", "mla_w_ukv", "w_out", "ffn_w_gate", "ffn_w_up", "ffn_w_down")
_OWNER = dict(zip(_BIG, (1, 0, 0, 1, 0, 0, 1)))
_ATTN_WEIGHTS, _FFN_WEIGHTS = _BIG[:4], _BIG[4:]


def _hbm(a):
    return pltpu.with_memory_space_constraint(a, pltpu.HBM)


def _per_core(c, owners, fn):
    for g in range(2):
        mine = tuple(p for p, o in enumerate(owners) if o == g)
        theirs = tuple(p for p, o in enumerate(owners) if o != g)
        pl.when(c == g)(functools.partial(fn, mine, theirs))


def _token_spec():
    return pl.BlockSpec(memory_space=pltpu.VMEM), _sds((8, LANES), F32)


def _gather_start(shards, owners, layer, tag):
    n = len(shards)
    lands = [_hbm(lax.empty((4,) + s.shape[1:], s.dtype)) for s in shards]

    def body(*refs):
        w_refs, l_refs = refs[:n], refs[n:2 * n]
        ssem, rsem, token = refs[2 * n], refs[2 * n + 1], refs[-1]
        x, y, c, chips = _place()
        me = 2 * x + y

        def send(mine, _):
            for i in mine:
                for j, (cx, cy) in enumerate(chips):
                    _remote(w_refs[i].at[layer], l_refs[i].at[me], ssem.at[3 * i + j], rsem.at[3 * i + j],
                            (cx, cy, c)).start()

        _per_core(c, owners, send)
        token[...] = jnp.zeros_like(token)

    tspec, tshape = _token_spec()
    out = pl.pallas_call(
        body, name=f"gather_start_{tag}", in_specs=[_HBM] * (2 * n),
        out_specs=[_SEM, _SEM] + [_HBM] * n + [tspec],
        out_shape=[_dma_sems(3 * n), _dma_sems(3 * n)] + [pltpu.HBM(l.shape, l.dtype) for l in lands] + [tshape],
        input_output_aliases={n + i: 2 + i for i in range(n)},
        compiler_params=pltpu.CompilerParams(has_side_effects=_EFFECT),
    )(*[_hbm(s) for s in shards], *lands)
    return out[0], out[1], list(out[2:2 + n]), out[-1]


def _gather_wait(ssem, rsem, shards, lands, after, owners, layer, tag):
    n = len(shards)

    def body(*refs):
        w_refs, l_refs = refs[:n], refs[n:2 * n]
        ssem, rsem = refs[2 * n], refs[2 * n + 1]
        x, y, c, chips = _place()

        def wait(mine, _):
            for i in mine:
                for j, (cx, cy) in enumerate(chips):
                    cp = _remote(w_refs[i].at[layer], l_refs[i].at[2 * cx + cy], ssem.at[3 * i + j], rsem.at[3 * i + j],
                                 (cx, cy, c))
                    cp.wait_send()
                    cp.wait_recv()

        _per_core(c, owners, wait)

    return list(pl.pallas_call(
        body, name=f"gather_wait_{tag}", in_specs=[_HBM] * (2 * n) + [_SEM, _SEM, _ANY], out_specs=[_HBM] * n,
        out_shape=[pltpu.HBM(l.shape, l.dtype) for l in lands],
        input_output_aliases={n + i: i for i in range(n)},
        compiler_params=pltpu.CompilerParams(has_side_effects=_EFFECT),
    )(*[_hbm(s) for s in shards], *lands, ssem, rsem, after))


def _gather_finish(shards, lands, owners, layer, tag):
    n = len(shards)

    def body(*refs):
        w_refs, g_refs = refs[:n], refs[2 * n:3 * n]
        ssem, rsem = refs[3 * n:]
        x, y, c, chips = _place()
        me = 2 * x + y
        sib = (x, y, 1 - c)
        owns = [_remote(w.at[layer], g.at[me], ssem.at[i], rsem.at[i], sib) for i, (w, g) in enumerate(zip(w_refs, g_refs))]
        for cp in owns:
            cp.start()

        def forward(mine, theirs):
            def blk(i, j):
                b = g_refs[i].at[2 * chips[j][0] + chips[j][1]]
                return _remote(b, b, ssem.at[n + 3 * i + j], rsem.at[n + 3 * i + j], sib)

            for i in mine:
                for j in range(3):
                    blk(i, j).start()
            for i in theirs:
                for j in range(3):
                    blk(i, j).wait_recv()
            for i in mine:
                for j in range(3):
                    blk(i, j).wait_send()

        _per_core(c, owners, forward)
        for cp in owns:
            cp.wait_recv()
            cp.wait_send()

    return list(pl.pallas_call(
        body, name=f"gather_finish_{tag}", in_specs=[_HBM] * (2 * n), out_specs=[_HBM] * n,
        out_shape=[_sds(l.shape, l.dtype) for l in lands], input_output_aliases={n + i: i for i in range(n)},
        scratch_shapes=[_dma_sems(4 * n), _dma_sems(4 * n)],
    )(*shards, *lands))


def _rs_to_owner(grads, owners, tag):
    n = len(grads)

    def body(*refs):
        g_refs, t_refs = refs[:n], refs[n:2 * n]
        ssem, rsem = refs[2 * n:]
        x, y, c, _ = _place()

        def swap(mine, theirs):
            cps = [_remote(g_refs[i], t_refs[i], ssem.at[i], rsem.at[i], (x, y, 1 - c)) for i in theirs]
            for cp in cps:
                cp.start()
            for i in mine:
                _remote(g_refs[i], t_refs[i], ssem.at[i], rsem.at[i], (x, y, 1 - c)).wait_recv()
            for cp in cps:
                cp.wait_send()

        _per_core(c, owners, swap)

    return list(pl.pallas_call(
        body, name=f"rs_to_owner_{tag}", in_specs=[_HBM] * n, out_specs=[_HBM] * n,
        out_shape=[_sds(g.shape, g.dtype) for g in grads], scratch_shapes=[_dma_sems(n), _dma_sems(n)],
    )(*grads))


def _a2a_start(pairs, owners, tag):
    n = len(pairs)
    lands = [_hbm(lax.empty((3,) + p.shape[1:], p.dtype)) for p in pairs]

    def body(*refs):
        a_refs, t_refs = refs[:n], refs[n:2 * n]
        ssem, rsem, token = refs[2 * n], refs[2 * n + 1], refs[-1]
        x, y, c, chips = _place()

        def send(mine, _):
            for i in mine:
                for j, (cx, cy) in enumerate(chips):
                    _remote(a_refs[i].at[2 * cx + cy], t_refs[i].at[j], ssem.at[3 * i + j], rsem.at[3 * i + j],
                            (cx, cy, c)).start()

        _per_core(c, owners, send)
        token[...] = jnp.zeros_like(token)

    tspec, tshape = _token_spec()
    out = pl.pallas_call(
        body, name=f"rs_a2a_start_{tag}", in_specs=[_HBM] * (2 * n),
        out_specs=[_SEM, _SEM] + [_HBM] * n + [tspec],
        out_shape=[_dma_sems(3 * n), _dma_sems(3 * n)] + [pltpu.HBM(l.shape, l.dtype) for l in lands] + [tshape],
        input_output_aliases={n + i: 2 + i for i in range(n)},
        compiler_params=pltpu.CompilerParams(has_side_effects=_EFFECT),
    )(*[_hbm(p) for p in pairs], *lands)
    return out[0], out[1], list(out[2:2 + n]), out[-1]


def _a2a_wait(ssem, rsem, pairs, lands, after, owners, tag):
    n = len(pairs)

    def body(*refs):
        a_refs, t_refs = refs[:n], refs[n:2 * n]
        ssem, rsem = refs[2 * n], refs[2 * n + 1]
        x, y, c, chips = _place()

        def wait(mine, _):
            for i in mine:
                for j, (cx, cy) in enumerate(chips):
                    cp = _remote(a_refs[i].at[2 * cx + cy], t_refs[i].at[j], ssem.at[3 * i + j], rsem.at[3 * i + j],
                                 (cx, cy, c))
                    cp.wait_send()
                    cp.wait_recv()

        _per_core(c, owners, wait)

    return list(pl.pallas_call(
        body, name=f"rs_a2a_wait_{tag}", in_specs=[_HBM] * (2 * n) + [_SEM, _SEM, _ANY], out_specs=[_HBM] * n,
        out_shape=[pltpu.HBM(l.shape, l.dtype) for l in lands],
        input_output_aliases={n + i: i for i in range(n)},
        compiler_params=pltpu.CompilerParams(has_side_effects=_EFFECT),
    )(*[_hbm(p) for p in pairs], *lands, ssem, rsem, after))


def _rs_from_owner(reds, owners):
    n = len(reds)

    def body(*refs):
        q_refs, o_refs = refs[:n], refs[n:2 * n]
        ssem, rsem = refs[2 * n:]
        x, y, c, _ = _place()

        def swap(mine, theirs):
            cps = [_remote(q_refs[k], o_refs[k], ssem.at[k], rsem.at[k], (x, y, 1 - c)) for k in mine]
            for cp in cps:
                cp.start()
            for k in theirs:
                _remote(q_refs[k], o_refs[k], ssem.at[k], rsem.at[k], (x, y, 1 - c)).wait_recv()
            for cp in cps:
                cp.wait_send()

        _per_core(c, owners, swap)

    return list(pl.pallas_call(
        body, name="rs_from_owner", in_specs=[_HBM] * n, out_specs=[_HBM] * n,
        out_shape=[_sds(q.shape, q.dtype) for q in reds], scratch_shapes=[_dma_sems(n), _dma_sems(n)],
    )(*reds))


def _gather_small(s):
    R, _ = s.shape

    def body(s_ref, o_ref, ssem, rsem, lsem):
        x, y, c, _ = _place()
        me = 4 * x + 2 * y + c
        own = pltpu.make_async_copy(s_ref, o_ref.at[me], lsem)
        own.start()
        sends = []
        for k in range(1, 8):
            px, py, pc = x ^ (k >> 2), y ^ ((k >> 1) & 1), c ^ (k & 1)
            cp = _remote(s_ref, o_ref.at[me], ssem.at[k - 1], rsem.at[k - 1], (px, py, pc))
            cp.start()
            sends.append(cp)
        for k in range(1, 8):
            px, py, pc = x ^ (k >> 2), y ^ ((k >> 1) & 1), c ^ (k & 1)
            blk = o_ref.at[4 * px + 2 * py + pc]
            _remote(blk, blk, ssem.at[k - 1], rsem.at[k - 1], (px, py, pc)).wait_recv()
        for cp in sends:
            cp.wait_send()
        own.wait()

    vm = pl.BlockSpec(memory_space=pltpu.VMEM)
    return pl.pallas_call(
        body, name="gather_small", in_specs=[vm], out_specs=vm, out_shape=_sds((8, R, LANES), s.dtype),
        scratch_shapes=[pltpu.SemaphoreType.DMA((7,)), pltpu.SemaphoreType.DMA((7,)), pltpu.SemaphoreType.DMA],
    )(s)


_COL_SHARDED =("w_in", "mla_w_uq", "mla_w_ukv", "ffn_w_gate", "ffn_w_up")
_SMALL = ("mla_q_norm", "mla_kv_norm", "gqa_q_norm", "gqa_k_norm", "rel_bias", "ln1_g", "ln1_b", "ln2_g", "ln2_b")


def _pack_flat(arrs, align):
    flat = jnp.concatenate([a.reshape(-1) for a in arrs])
    pad = (-flat.shape[0]) % align
    return jnp.pad(flat, (0, pad)) if pad else flat


def _unpack_flat(flat, shapes):
    out, off = [], 0
    for s in shapes:
        n = int(np.prod(s))
        out.append(flat[off:off + n].reshape(s))
        off += n
    return out


def _perm_gqa_rows(w):
    return jnp.concatenate([w[:832], w[896:960], w[832:896], w[960:]], axis=0)


def _local_step(x, target, small, depth, weights_of_layer, grads_done):
    S, D = x.shape
    alpha = (2.0 * depth) ** 0.25
    in_idx, uq_idx, ukv_idx = _in_cols(), _uq_cols(), _ukv_cols()
    win, wuq, wukv, wout, wg, wu, wdn = ([None] * depth for _ in range(7))

    tm, tg = _rope_tables(S)
    j256, j384 = _head_ones(256), _head_ones(384)
    mla_scale = (64 + MLA_ROPE_DIM) ** -0.5
    branches = []
    for (_, dil) in DIL_BRANCHES:
        L = S // dil
        tq = min(256, L)
        idx = jnp.asarray(_branch_bucket_idx(tq, dil))
        branches.append((dil, L, tq, idx))
    tabs = [_bias_expand(idx, small["rel_bias"], name=f"bias_expand_{b}") for b, (_, _, _, idx) in enumerate(branches)]

    def padded(a):
        z = jnp.zeros((DIL_HALF, a.shape[1]), a.dtype)
        return jnp.concatenate([z, a, z], axis=0)[None]

    saved = []
    xf, xb = x, x.astype(CDT)
    for l in range(depth):
        W, token = weights_of_layer(l, "attn", xb)
        win[l] = _rows_from_shards(W["w_in"], in_idx)
        wuq[l] = _rows_from_shards(W["mla_w_uq"], uq_idx)
        wukv[l] = _rows_from_shards(W["mla_w_ukv"], ukv_idx)
        wout[l] = _perm_gqa_rows(W["w_out"].reshape(-1, D))
        gq, gkv = small["mla_q_norm"][l][None], small["mla_kv_norm"][l][None]
        if token is not None:
            gq = gq + token[0, 0]
        ggq = jnp.tile(small["gqa_q_norm"][l], 4)[None]
        ggk = jnp.tile(small["gqa_k_norm"][l], 2)[None]
        h = _mm(xb, win[l], tb=True, name="mm_in")
        cq, ckv, kr, qd, kd, vd, qg, kg, vg, *strided = _prep_fwd(h, gq, gkv, ggq, ggk, tm, tg, j256)
        qkv = [(qd[None], padded(kd), padded(vd))] + [tuple(strided[3 * b:3 * b + 3]) for b in range(len(DIL_STRIDES))]
        qa = _mm(cq, wuq[l], tb=True, name="mm_uq")
        kvp = _mm(ckv, wukv[l], tb=True, out_dtype=CDT, name="mm_ukv")
        qm, km = _mla_prep_fwd(qa, kvp, kr, tm, mla_scale)
        oa, lsa = _attn_fwd(qm, km, kvp, split=True, npairs=3, kblk=lambda p: p, vblk=lambda p: 6 + p,
                            name="mla_attn_fwd")
        oc, lsc = _attn_fwd(qg, kg, vg, split=False, npairs=2, kblk=lambda p: 0, vblk=lambda p: 0,
                            name="gqa_attn_fwd")
        obs, lbs = [], []
        for b, (dil, L, tq, _) in enumerate(branches):
            o_b, l_b = _dil_fwd(*qkv[b], tabs[b], dil=dil, L=L, tq=tq, name=f"dil_fwd_{b}")
            obs.append(o_b)
            lbs.append(l_b)
        ob = _dil_mix_fwd(obs, lbs)
        cat = jnp.concatenate([oa, ob, oc], axis=1)
        mix = _mm(cat, wout[l], name="mm_out")
        x1, x1b, z1 = _ln_fwd(xf, mix, small["ln1_g"][l][None], small["ln1_b"][l][None], alpha, name="ln1_fwd")
        W, _ = weights_of_layer(l, "ffn", x1b)
        wg[l], wu[l], wdn[l] = W["ffn_w_gate"], W["ffn_w_up"], W["ffn_w_down"]
        g3, u3, act = _ffn_up(x1b, wg[l], wu[l])
        ff = _ffn_down(act, wdn[l])
        x2, x2b, z2 = _ln_fwd(x1, ff, small["ln2_g"][l][None], small["ln2_b"][l][None], alpha, name="ln2_fwd")
        saved.append(dict(xb=xb, h=h, cq=cq, ckv=ckv, qg=qg, kg=kg, vg=vg, kvp=kvp, qm=qm, km=km, oa=oa, lsa=lsa,
                          oc=oc, lsc=lsc, obs=obs, lbs=lbs, qkv=qkv, cat=cat, z1=z1, x1b=x1b, g3=g3, u3=u3, act=act, z2=z2,
                          gq=gq, gkv=gkv, ggq=ggq, ggk=ggk))
        xf, xb = x2, x2b

    gW = {k: [None] * depth for k in _BIG}
    gS = {k: [None] * depth for k in ("mla_q_norm", "mla_kv_norm", "gqa_q_norm", "gqa_k_norm", "ln1_g", "ln1_b", "ln2_g",
                                      "ln2_b")}
    g_rel = None
    dya, dyb = xf, target
    token = None
    for l in reversed(range(depth)):
        sv = saved[l]
        ln2_g = small["ln2_g"][l][None]
        if token is not None:
            ln2_g = ln2_g + token[0, 0]
        if l == depth - 1:
            dz2, dz2b, gS["ln2_g"][l], gS["ln2_b"][l], loss = _ln_bwd(dya, dyb, sv["z2"], ln2_g, alpha,
                                                                       name="ln2_bwd_loss", loss_head=True)
        else:
            dz2, dz2b, gS["ln2_g"][l], gS["ln2_b"][l] = _ln_bwd(dya, dyb, sv["z2"], ln2_g, alpha, name="ln2_bwd")
        gW["ffn_w_down"][l] = _mm(sv["act"], dz2b, ta=True, ga=True, go=True, out_dtype=WIRE, name="mm_down_dw")
        dg3, du3 = _ffn_down_dx(dz2b, wdn[l], sv["g3"], sv["u3"])
        gW["ffn_w_gate"][l] = _mm(dg3, sv["x1b"], ta=True, ga=True, go=True, out_dtype=WIRE, name="mm_gate_dw")
        gW["ffn_w_up"][l] = _mm(du3, sv["x1b"], ta=True, ga=True, go=True, out_dtype=WIRE, name="mm_up_dw")
        dx1 = _ffn_up_dx(dg3, du3, wg[l], wu[l])
        token = grads_done(l, "ffn", {n: gW[n][l] for n in _FFN_WEIGHTS})
        ln1_g = small["ln1_g"][l][None]
        if token is not None:
            ln1_g = ln1_g + token[0, 0]
        dz1, dz1b, gS["ln1_g"][l], gS["ln1_b"][l] = _ln_bwd(dx1, dz2, sv["z1"], ln1_g, alpha, name="ln1_bwd")
        gW["w_out"][l] = _perm_gqa_rows(_mm(sv["cat"], dz1b, ta=True, out_dtype=WIRE, name="mm_out_dw")).reshape(4, -1, D)
        dcat = _mm(dz1b, wout[l], tb=True, name="mm_out_dx")
        dqg, dkg, dvg = _attn_bwd(sv["qg"], sv["kg"], sv["vg"], dcat, sv["oc"], sv["lsc"], split=False, npairs=2,
                                  kblk=lambda p: 0, vblk=lambda p: 0, doblk=lambda p: 6 + p, shared_kv=True,
                                  name="gqa_attn_bwd")
        dqm, dkm, dvm = _attn_bwd(sv["qm"], sv["km"], sv["kvp"], dcat, sv["oa"], sv["lsa"], split=True, npairs=3,
                                  kblk=lambda p: p, vblk=lambda p: 6 + p, doblk=lambda p: p, shared_kv=False,
                                  name="mla_attn_bwd")
        dqa, dkvp, dkr = _mla_prep_bwd(dqm, dkm, dvm, tm, mla_scale)
        gW["mla_w_uq"][l] = _rows_to_shards(_mm(dqa, sv["cq"], ta=True, out_dtype=WIRE, name="mm_uq_dw"), uq_idx, MLA_HEADS * 96)
        dcq = _mm(dqa, wuq[l], name="mm_uq_dx")
        gW["mla_w_ukv"][l] = _rows_to_shards(_mm(dkvp, sv["ckv"], ta=True, out_dtype=WIRE, name="mm_ukv_dw"), ukv_idx, MLA_HEADS * 128)
        dckv = _mm(dkvp, wukv[l], name="mm_ukv_dx")
        mixb = _dil_mix_bwd(dcat, sv["obs"], sv["lbs"], j384)
        ddq, ddk, ddv = [], [], []
        for b, (dil, L, tq, idx) in enumerate(branches):
            dq_b, dk_b, dv_b, dtab = _dil_bwd(*sv["qkv"][b], tabs[b], mixb[b], sv["lbs"][b], mixb[3 + b], dil=dil, L=L,
                                              tq=tq, name=f"dil_bwd_{b}")
            if dil == 1:
                dq_b, dk_b, dv_b = dq_b[0], dk_b[0, DIL_HALF:DIL_HALF + S], dv_b[0, DIL_HALF:DIL_HALF + S]
            ddq.append(dq_b)
            ddk.append(dk_b)
            ddv.append(dv_b)
            g_b = _bias_reduce(idx, dtab, name=f"bias_reduce_{b}")[:, :, 0].T
            g_rel = g_b if g_rel is None else g_rel + g_b
        dh, n1, n2, n3, n4 = _prep_bwd(sv["h"], dcq, dckv, dkr, ddq, ddk, ddv, dqg, dkg, dvg, sv["gq"], sv["gkv"],
                                       sv["ggq"], sv["ggk"], tg, j256)
        gS["mla_q_norm"][l], gS["mla_kv_norm"][l] = n1[0], n2[0]
        gS["gqa_q_norm"][l] = n3[0].reshape(4, 64).sum(0)
        gS["gqa_k_norm"][l] = n4[0].reshape(2, 64).sum(0)
        gW["w_in"][l] = _rows_to_shards(_mm(dh, sv["xb"], ta=True, out_dtype=WIRE, name="mm_in_dw"), in_idx, IN_W)
        dya = _mm(dh, win[l], name="mm_in_dx")
        dyb = dz1
        token = grads_done(l, "attn", {n: gW[n][l] for n in _ATTN_WEIGHTS})
    grad_x = _axpy(dya, dyb, alpha, name="grad_x")

    gsmall = {k: jnp.stack([a.reshape(-1) for a in v]) for k, v in gS.items()}
    gsmall["rel_bias"] = g_rel
    return loss, grad_x, gsmall


_ORDER = ("w_in", "mla_q_norm", "mla_kv_norm", "mla_w_uq", "mla_w_ukv", "gqa_q_norm", "gqa_k_norm", "rel_bias", "w_out",
          "ln1_g", "ln1_b", "ffn_w_gate", "ffn_w_up", "ffn_w_down", "ln2_g", "ln2_b")


def kernel(x, w_in, mla_q_norm, mla_kv_norm, mla_w_uq, mla_w_ukv, gqa_q_norm, gqa_k_norm, rel_bias, w_out, ln1_g, ln1_b, ffn_w_gate, ffn_w_up, ffn_w_down, ln2_g, ln2_b, loss_target, m_w_in, m_mla_q_norm, m_mla_kv_norm, m_mla_w_uq, m_mla_w_ukv, m_gqa_q_norm, m_gqa_k_norm, m_rel_bias, m_w_out, m_ln1_g, m_ln1_b, m_ffn_w_gate, m_ffn_w_up, m_ffn_w_down, m_ln2_g, m_ln2_b, v_w_in, v_mla_q_norm, v_mla_kv_norm, v_mla_w_uq, v_mla_w_ukv, v_gqa_q_norm, v_gqa_k_norm, v_rel_bias, v_w_out, v_ln1_g, v_ln1_b, v_ffn_w_gate, v_ffn_w_up, v_ffn_w_down, v_ln2_g, v_ln2_b):
    wts = dict(zip(_ORDER, (w_in, mla_q_norm, mla_kv_norm, mla_w_uq, mla_w_ukv, gqa_q_norm, gqa_k_norm, rel_bias, w_out,
                            ln1_g, ln1_b, ffn_w_gate, ffn_w_up, ffn_w_down, ln2_g, ln2_b)))
    mom = dict(zip(_ORDER, (m_w_in, m_mla_q_norm, m_mla_kv_norm, m_mla_w_uq, m_mla_w_ukv, m_gqa_q_norm, m_gqa_k_norm,
                            m_rel_bias, m_w_out, m_ln1_g, m_ln1_b, m_ffn_w_gate, m_ffn_w_up, m_ffn_w_down, m_ln2_g,
                            m_ln2_b)))
    var = dict(zip(_ORDER, (v_w_in, v_mla_q_norm, v_mla_kv_norm, v_mla_w_uq, v_mla_w_ukv, v_gqa_q_norm, v_gqa_k_norm,
                            v_rel_bias, v_w_out, v_ln1_g, v_ln1_b, v_ffn_w_gate, v_ffn_w_up, v_ffn_w_down, v_ln2_g,
                            v_ln2_b)))
    small_shapes = [wts[n].shape for n in _SMALL]
    for d in (wts, mom, var):
        for n in _COL_SHARDED:
            d[n] = d[n].transpose(0, 2, 1)

    depth = 2
    shards = {n: wts[n].astype(WIRE) for n in _BIG}
    flying = {}

    def start_gather(names, l, tag):
        own = tuple(_OWNER[n] for n in names)
        sh = [shards[n] for n in names]
        ssem, rsem, lands, token = _gather_start(sh, own, l, tag)
        return (names, own, sh, ssem, rsem, lands, l, tag), token

    def end_gather(flight, after):
        names, own, sh, ssem, rsem, lands, l, tag = flight
        got = _gather_finish(sh, _gather_wait(ssem, rsem, sh, lands, after, own, l, tag), own, l, tag)
        return {n: g.astype(CDT) for n, g in zip(names, got)}

    def weights_of_layer(l, part, after):
        if (l, part) == (0, "attn"):
            got = end_gather(start_gather(_ATTN_WEIGHTS, 0, "attn0")[0], after)
            flying["ffn0"], t0 = start_gather(_FFN_WEIGHTS, 0, "ffn0")
            flying["layer1"], t1 = start_gather(_BIG, 1, "layer1")
            return got, t0 + t1
        if (l, part) == (0, "ffn"):
            return end_gather(flying.pop("ffn0"), after), None
        if part == "attn":
            flying["w1"] = end_gather(flying.pop("layer1"), after)
        return flying["w1"], None

    def grads_done(l, part, grads):
        names = tuple(grads)
        own = tuple(_OWNER[n] for n in names)
        tag = f"{part}{l}"
        gl = [grads[n] for n in names]
        theirs = _rs_to_owner(gl, own, tag)
        pairs = [_sum_pair(g, t, o, name=f"rs_pair_sum_{n}") for n, g, t, o in zip(names, gl, theirs, own)]
        ssem, rsem, lands, token = _a2a_start(pairs, own, tag)
        flying[tag] = (names, own, ssem, rsem, pairs, lands)
        return token

    small = {n: wts[n] for n in _SMALL}
    loss, grad_x, gsmall = _local_step(x[0], loss_target[0], small, depth, weights_of_layer, grads_done)

    reds = {}
    for l in reversed(range(depth)):
        for part in ("ffn", "attn"):
            tag = f"{part}{l}"
            names, own, ssem, rsem, pairs, lands = flying.pop(tag)
            got = _a2a_wait(ssem, rsem, pairs, lands, grad_x, own, tag)
            for n, p, t, o in zip(names, pairs, got, own):
                reds[n, l] = _sum_chips(p, t, o, name=f"rs_sum_chips_{n}")
    order = [(n, l) for l in range(depth) for n in _BIG]
    sibs = dict(zip(order, _rs_from_owner([reds[k] for k in order], tuple(_OWNER[n] for n, _ in order))))

    sflat = _pack_flat([gsmall[n].reshape(-1) for n in _SMALL], 8 * LANES)
    rs = sflat.shape[0] // LANES
    sall = _gather_small(sflat.reshape(rs, LANES))

    def packed(d):
        return _pack_flat([d[n] for n in _SMALL], 8 * LANES).reshape(rs, LANES)

    outs = {tag: {} for tag in ("grad", "delta", "new_m", "new_v")}
    for n in _BIG:
        res = _adamw(wts[n], [reds[n, 0], reds[n, 1]], [sibs[n, 0], sibs[n, 1]], mom[n], var[n], _OWNER[n],
                     name=f"adamw_{n}")
        for tag, r in zip(("grad", "delta", "new_m", "new_v"), res):
            outs[tag][n] = r.transpose(0, 2, 1) if n in _COL_SHARDED else r
    for tag, smallflat in zip(("grad", "delta", "new_m", "new_v"), _adamw_small(packed(wts), sall, packed(mom), packed(var))):
        outs[tag].update(zip(_SMALL, _unpack_flat(smallflat.reshape(-1), small_shapes)))

    total = lax.psum(loss[0, 0], ("x", "y", "c"))
    return (total, grad_x[None], *[outs["grad"][n] for n in _ORDER], *[outs["delta"][n] for n in _ORDER],
            *[outs["new_m"][n] for n in _ORDER], *[outs["new_v"][n] for n in _ORDER])
```

```python
import functools
import math

import numpy as np
import jax
import jax.numpy as jnp
from jax import lax
from jax.experimental import pallas as pl
from jax.experimental.pallas import tpu as pltpu

F32 = jnp.float32
CDT = jnp.bfloat16
WIRE = jnp.bfloat16

HEAD_DIM = 64
GRID_W = 64
ROPE_THETA = 10000.0
MLA_HEADS = 6
MLA_Q_RANK = 256
MLA_KV_RANK = 128
MLA_ROPE_DIM = 32
DIL_HEADS = 6
DIL_BRANCHES = ((128, 1), (512, 4), (2048, 16))
DIL_HALF = 64
GQA_Q_HEADS = 4
REL_BUCKETS = 32
REL_MAX_DIST = 1024
NEG_INF = -1e30
LANES = 128
VMEM_LIMIT = 56 * 1024 * 1024

ADAM_LR, ADAM_B1, ADAM_B2, ADAM_EPS, ADAM_WD, ADAM_STEP = 0.001, 0.9, 0.999, 1e-08, 0.01, 10

C_CQ, C_CKV, C_KR, C_DQ, C_DK, C_DV, C_GQ, C_GK, C_GV, IN_P = 0, 256, 384, 512, 896, 1280, 1664, 1920, 2048, 2176
IN_W = 2080
MESH_ID = pl.DeviceIdType.MESH


def _cparams(n_axes, vmem=VMEM_LIMIT):
    return pltpu.CompilerParams(dimension_semantics=("arbitrary",) * n_axes, vmem_limit_bytes=vmem)


MAX_WHOLE_DIM = 2304
WHOLE_K_WINDOW_BYTES = 36 * 1024 * 1024


def _pick(n, target):
    best = None
    for t in range(LANES, min(n, target) + 1, LANES):
        if n % t == 0:
            best = t
    if best is not None and (2 * best >= target or n > MAX_WHOLE_DIM):
        return best
    return n


def _sds(shape, dtype):
    return jax.ShapeDtypeStruct(tuple(shape), dtype)


def _in_cols():
    idx = -np.ones((IN_P,), np.int64)
    idx[C_CQ:C_CQ + 256] = np.arange(0, 256)
    idx[C_CKV:C_CKV + 128] = np.arange(256, 384)
    idx[C_KR + 64:C_KR + 96] = np.arange(384, 416)
    idx[C_DQ:C_DQ + 1152] = np.arange(416, 1568)
    gq = 1568 + (np.array([0, 2, 1, 3])[:, None] * 64 + np.arange(64)[None, :]).reshape(-1)
    idx[C_GQ:C_GQ + 256] = gq
    idx[C_GK:C_GK + 256] = np.arange(1824, 2080)
    return idx


def _uq_cols():
    idx = -np.ones((MLA_HEADS * 128,), np.int64)
    for h in range(MLA_HEADS):
        idx[h * 128:h * 128 + 96] = np.arange(96 * h, 96 * h + 96)
    return idx


def _ukv_cols():
    idx = -np.ones((MLA_HEADS * 128 + MLA_HEADS * 64,), np.int64)
    for h in range(MLA_HEADS):
        idx[h * 128:h * 128 + 64] = np.arange(128 * h, 128 * h + 64)
        idx[768 + h * 64:768 + h * 64 + 64] = np.arange(128 * h + 64, 128 * h + 128)
    return idx


def _runs(idx):
    out, i = [], 0
    while i < len(idx):
        j = i + 1
        while j < len(idx) and ((idx[i] < 0 and idx[j] < 0) or (idx[i] >= 0 and idx[j] == idx[j - 1] + 1)):
            j += 1
        out.append((int(idx[i]), j - i))
        i = j
    return out


def _rows_from_shards(sh, idx):
    _, cs, r = sh.shape
    pieces = []
    for first, ln in _runs(idx):
        if first < 0:
            pieces.append(jnp.zeros((ln, r), sh.dtype))
            continue
        while ln > 0:
            k, off = divmod(first, cs)
            take = min(ln, cs - off)
            pieces.append(sh[k, off:off + take, :])
            first, ln = first + take, ln - take
    return jnp.concatenate(pieces, axis=0)


def _rows_to_shards(wp, idx, n):
    inv = np.zeros((n,), np.int64)
    pos = np.nonzero(idx >= 0)[0]
    inv[idx[pos]] = pos
    cs = n // 4
    shards = []
    for k in range(4):
        pieces = [wp[first:first + ln, :] for first, ln in _runs(inv[k * cs:(k + 1) * cs])]
        shards.append(jnp.concatenate(pieces, axis=0))
    return jnp.stack(shards)


def _t5_bucket_np(rel):
    nb = REL_BUCKETS // 2
    exact = nb // 2
    ret = np.where(rel > 0, nb, 0)
    n = np.abs(rel)
    nf = np.maximum(n, 1).astype(np.float32)
    large = exact + (np.log(nf / np.float32(exact)) / np.float32(math.log(REL_MAX_DIST / exact))
                     * np.float32(nb - exact)).astype(np.int32)
    large = np.minimum(large, nb - 1)
    return ret + np.where(n < exact, n, large)


def _branch_bucket_idx(tq, dil):
    kw = tq + 2 * DIL_HALF
    rel = np.arange(kw)[None, :] - DIL_HALF - np.arange(tq)[:, None]
    idx = _t5_bucket_np(rel * dil)
    return np.where(np.abs(rel) <= DIL_HALF, idx, -1).astype(np.int32)


def _rope_tables(S):
    inv = ROPE_THETA ** (-jnp.arange(0, 32, 2, dtype=F32) / 32)
    t = jnp.arange(S)
    pos = t.astype(F32)
    row = (t // GRID_W).astype(F32)
    col = (t % GRID_W).astype(F32)
    lane = np.arange(LANES)
    wm = lane - 64
    is_rope = (lane >= 64) & (lane < 96)
    ang = pos[:, None] * inv[np.where(is_rope, wm % 16, 0)][None, :]
    cm = jnp.where(is_rope[None], jnp.cos(ang), 1.0)
    smm = jnp.where((is_rope & (wm < 16))[None], -jnp.sin(ang), 0.0)
    spm = jnp.where((is_rope & (wm >= 16))[None], jnp.sin(ang), 0.0)
    g = lane % 64
    w = g % 32
    angg = jnp.where((g < 32)[None], row[:, None], col[:, None]) * inv[w % 16][None, :]
    cg = jnp.cos(angg)
    smg = jnp.where((w < 16)[None], -jnp.sin(angg), 0.0)
    spg = jnp.where((w >= 16)[None], jnp.sin(angg), 0.0)
    return (cm, smm, spm), (cg, smg, spg)


def _lanes(t, width):
    return t if width == LANES else jnp.concatenate([t] * (width // LANES), axis=1)


def _rope(x, tabs):
    c, sm, sp = (_lanes(t, x.shape[1]) for t in tabs)
    w = x.shape[1]
    return x * c + pltpu.roll(x, w - 16, 1) * sm + pltpu.roll(x, 16, 1) * sp


def _rope_t(dy, tabs):
    c, sm, sp = (_lanes(t, dy.shape[1]) for t in tabs)
    w = dy.shape[1]
    return dy * c + pltpu.roll(dy * sm, 16, 1) + pltpu.roll(dy * sp, w - 16, 1)


def _head_ones(width):
    i = np.arange(width)
    return jnp.asarray((i[:, None] // HEAD_DIM == i[None, :] // HEAD_DIM).astype(np.float32))


def _headsum(x, j):
    return jnp.dot(x, j, preferred_element_type=F32, precision=lax.Precision.HIGHEST)


def _mm(a, b, *, ta=False, tb=False, ga=False, gb=False, go=False, out_dtype=F32, name):
    G = a.shape[0] if ga else (b.shape[0] if gb else 1)
    a2 = a.shape[1:] if ga else a.shape
    b2 = b.shape[1:] if gb else b.shape
    K, M = a2 if ta else a2[::-1]
    N = b2[0] if tb else b2[1]
    assert (b2[1] if tb else b2[0]) == K
    tm, tn, tk = _pick(M, 1024), _pick(N, 1024), _pick(K, 2048)
    if tm * tn > 1024 * 1152:
        tm = _pick(M, 512)
    if 2 * K * (tm + tn) * jnp.dtype(CDT).itemsize <= WHOLE_K_WINDOW_BYTES:
        tk = K
    nk = K // tk
    steps = nk if (go or G == 1) else G * nk
    dn = (((0 if ta else 1,), (1 if tb else 0,)), ((), ()))

    def body(a_ref, b_ref, o_ref, *acc):
        part = lax.dot_general(a_ref[...], b_ref[...], dn, preferred_element_type=F32)
        if steps == 1:
            o_ref[...] = part.astype(o_ref.dtype)
            return
        acc_ref, = acc
        s = pl.program_id(3)

        @pl.when(s == 0)
        def _():
            acc_ref[...] = part

        @pl.when(s > 0)
        def _():
            acc_ref[...] += part

        @pl.when(s == steps - 1)
        def _():
            o_ref[...] = acc_ref[...].astype(o_ref.dtype)

    def grp(g, s):
        return g if go else s // nk

    def kk(s):
        return s if steps == nk else s % nk

    def spec(grouped, block, index):
        if grouped:
            return pl.BlockSpec((None,) + block, lambda g, i, j, s: (grp(g, s),) + index(i, j, s))
        return pl.BlockSpec(block, lambda g, i, j, s: index(i, j, s))

    a_spec = (spec(ga, (tk, tm), lambda i, j, s: (kk(s), i)) if ta else spec(ga, (tm, tk), lambda i, j, s: (i, kk(s))))
    b_spec = (spec(gb, (tn, tk), lambda i, j, s: (j, kk(s))) if tb else spec(gb, (tk, tn), lambda i, j, s: (kk(s), j)))
    o_spec = spec(go, (tm, tn), lambda i, j, s: (i, j))
    return pl.pallas_call(
        body, name=name, grid=(G if go else 1, M // tm, N // tn, steps),
        in_specs=[a_spec, b_spec], out_specs=o_spec,
        out_shape=_sds(((G,) if go else ()) + (M, N), out_dtype),
        scratch_shapes=[pltpu.VMEM((tm, tn), F32)] if steps > 1 else [],
        compiler_params=_cparams(4),
    )(a, b)


def _row(ts, w, cb=0):
    return pl.BlockSpec((ts, w), lambda i: (i, cb))


def _full(shape):
    nd = len(shape)
    return pl.BlockSpec(tuple(shape), lambda i: (0,) * nd)


def _rms_fwd(x, g, eps=1e-6):
    r = lax.rsqrt(jnp.mean(x * x, axis=-1, keepdims=True) + eps)
    return x * r * g


def _rms_bwd(x, g, dy, eps=1e-6):
    r = lax.rsqrt(jnp.mean(x * x, axis=-1, keepdims=True) + eps)
    gdy = g * dy
    dx = r * gdy - x * (r * r * r) * jnp.mean(x * gdy, axis=-1, keepdims=True)
    return dx, x * r * dy


def _rms_head_fwd(x, g, j, eps=1e-6):
    r = lax.rsqrt(_headsum(x * x, j) * (1.0 / HEAD_DIM) + eps)
    return x * r * g


def _rms_head_bwd(x, g, dy, j, eps=1e-6):
    r = lax.rsqrt(_headsum(x * x, j) * (1.0 / HEAD_DIM) + eps)
    gdy = g * dy
    dx = r * gdy - x * (r * r * r) * (_headsum(x * gdy, j) * (1.0 / HEAD_DIM))
    return dx, x * r * dy


DIL_STRIDES = tuple(d for _, d in DIL_BRANCHES if d > 1)
DIL_W = DIL_HEADS * HEAD_DIM


def _res_spec(d, n, pad_blocks=0):
    return pl.BlockSpec((d, n, DIL_W), lambda i: (0, i + pad_blocks, 0))


def _prep_fwd(h, gq, gkv, ggq, ggk, tm, tg, j256):
    S = h.shape[0]
    ts = min(256, S)
    scale = HEAD_DIM ** -0.5
    nres = len(DIL_STRIDES)

    def body(*refs):
        (h_ref, gq_ref, gkv_ref, ggq_ref, ggk_ref, cm, smm, spm, cg, smg, spg, j_ref), refs = refs[:12], refs[12:]
        refs = refs[2 * nres:]
        (cq_o, ckv_o, kr_o, dq_o, dk_o, dv_o, gq_o, gk_o, gv_o), res_o = refs[:9], refs[9:-1]
        st = refs[-1]
        tabm = (cm[...], smm[...], spm[...])
        tabg = (cg[...], smg[...], spg[...])
        cq_o[...] = _rms_fwd(h_ref[:, C_CQ:C_CQ + 256], gq_ref[...]).astype(CDT)
        ckv_o[...] = _rms_fwd(h_ref[:, C_CKV:C_CKV + 128], gkv_ref[...]).astype(CDT)
        kr_o[...] = _rope(h_ref[:, C_KR:C_KR + 128], tabm).astype(CDT)
        dq_o[...] = (h_ref[:, C_DQ:C_DQ + 384] * scale).astype(CDT)
        dk_o[...] = h_ref[:, C_DK:C_DK + 384].astype(CDT)
        dv_o[...] = h_ref[:, C_DV:C_DV + 384].astype(CDT)
        for j, lanes in _lane_blocks(3 * DIL_W):
            st[j] = h_ref[:, C_DQ + lanes.start:C_DQ + lanes.stop] * (scale if j < 3 else 1.0)
        for bi, d in enumerate(DIL_STRIDES):
            for c in range(d):
                rows = pl.ds(c, ts // d, stride=d)
                for j, lanes in _lane_blocks(3 * DIL_W):
                    res_o[3 * bi + j // 3][c, :, (j % 3) * LANES:(j % 3 + 1) * LANES] = st.at[j][rows, :].astype(CDT)
        qn = _rms_head_fwd(h_ref[:, C_GQ:C_GQ + 256], ggq_ref[...], j_ref[...])
        gq_o[...] = (_rope(qn, tabg) * scale).astype(CDT)
        kn = _rms_head_fwd(h_ref[:, C_GK:C_GK + 128], ggk_ref[...], j_ref[0:128, 0:128])
        gk_o[...] = _rope(kn, tabg).astype(CDT)
        gv_o[...] = h_ref[:, C_GV:C_GV + 128].astype(CDT)

    widths = (256, 128, 128, 384, 384, 384, 256, 128, 128)
    out_specs = [_row(ts, w) for w in widths]
    out_shape = [_sds((S, w), CDT) for w in widths]
    zeros, aliases = [], {}
    for d in DIL_STRIDES:
        n, L = ts // d, S // d
        out_specs += [_res_spec(d, n), _res_spec(d, n, DIL_HALF // n), _res_spec(d, n, DIL_HALF // n)]
        out_shape += [_sds((d, L, DIL_W), CDT)] + [_sds((d, L + 2 * DIL_HALF, DIL_W), CDT)] * 2
        for t in range(2):
            aliases[12 + len(zeros)] = len(out_shape) - 2 + t
            zeros.append(jnp.zeros((d, L + 2 * DIL_HALF, DIL_W), CDT))
    return pl.pallas_call(
        body, name="prep_fwd", grid=(S // ts,),
        in_specs=[_row(ts, IN_P), _full(gq.shape), _full(gkv.shape), _full(ggq.shape), _full(ggk.shape)]
        + [_row(ts, LANES)] * 6 + [_full(j256.shape)] + [pl.BlockSpec(memory_space=pl.ANY)] * len(zeros),
        out_specs=out_specs, out_shape=out_shape, input_output_aliases=aliases,
        scratch_shapes=[pltpu.VMEM((3 * DIL_W // LANES, ts, LANES), F32)],
        compiler_params=_cparams(1),
    )(h, gq, gkv, ggq, ggk, *tm, *tg, j256, *zeros)


def _prep_bwd(h, dcq, dckv, dkr, ddq, ddk, ddv, dgq, dgk, dgv, gq, gkv, ggq, ggk, tg, j256):
    S = h.shape[0]
    ts = min(256, S)
    scale = HEAD_DIM ** -0.5

    def body(h_ref, dcq_r, dckv_r, dkr_r, q1, q2, q3, k1, k2, k3, v1, v2, v3, dgq_r, dgk_r, dgv_r,
             gq_ref, gkv_ref, ggq_ref, ggk_ref, cg, smg, spg, j_ref,
             dh_o, ngq_o, ngkv_o, nggq_o, nggk_o, *scr):
        tabg = (cg[...], smg[...], spg[...])
        first = pl.program_id(0) == 0
        scr, = scr
        d2, d3 = DIL_STRIDES
        dq = q1[...] + _by_token(q2, scr, d2) + _by_token(q3, scr, d3)
        dk = k1[...] + _by_token(k2, scr, d2) + _by_token(k3, scr, d3)
        dv = v1[...] + _by_token(v2, scr, d2) + _by_token(v3, scr, d3)

        def acc(o_ref, val):
            s = jnp.sum(val, axis=0, keepdims=True)

            @pl.when(first)
            def _():
                o_ref[...] = s

            @pl.when(jnp.logical_not(first))
            def _():
                o_ref[...] += s

        dx, dg = _rms_bwd(h_ref[:, C_CQ:C_CQ + 256], gq_ref[...], dcq_r[...])
        dh_o[:, C_CQ:C_CQ + 256] = dx.astype(CDT)
        acc(ngq_o, dg)
        dx, dg = _rms_bwd(h_ref[:, C_CKV:C_CKV + 128], gkv_ref[...], dckv_r[...])
        dh_o[:, C_CKV:C_CKV + 128] = dx.astype(CDT)
        acc(ngkv_o, dg)
        dh_o[:, C_KR:C_KR + 128] = dkr_r[...].astype(CDT)
        dh_o[:, C_DQ:C_DQ + 384] = (dq * scale).astype(CDT)
        dh_o[:, C_DK:C_DK + 384] = dk.astype(CDT)
        dh_o[:, C_DV:C_DV + 384] = dv.astype(CDT)
        dqn = _rope_t(dgq_r[...] * scale, tabg)
        dx, dg = _rms_head_bwd(h_ref[:, C_GQ:C_GQ + 256], ggq_ref[...], dqn, j_ref[...])
        dh_o[:, C_GQ:C_GQ + 256] = dx.astype(CDT)
        acc(nggq_o, dg)
        dkn = _rope_t(dgk_r[...], tabg)
        dx, dg = _rms_head_bwd(h_ref[:, C_GK:C_GK + 128], ggk_ref[...], dkn, j_ref[0:128, 0:128])
        dh_o[:, C_GK:C_GK + 128] = dx.astype(CDT)
        acc(nggk_o, dg)
        dh_o[:, C_GV:C_GV + 128] = dgv_r[...].astype(CDT)

    d2, d3 = DIL_STRIDES
    n2, n3 = ts // d2, ts // d3
    tok = _row(ts, DIL_W)
    return pl.pallas_call(
        body, name="prep_bwd", grid=(S // ts,),
        in_specs=[_row(ts, IN_P), _row(ts, 256), _row(ts, 128), _row(ts, 128)]
        + [tok, _res_spec(d2, n2), _res_spec(d3, n3)]
        + [tok, _res_spec(d2, n2, DIL_HALF // n2), _res_spec(d3, n3, DIL_HALF // n3)] * 2
        + [_row(ts, 256), _row(ts, 128), _row(ts, 128)]
        + [_full(gq.shape), _full(gkv.shape), _full(ggq.shape), _full(ggk.shape)] + [_row(ts, LANES)] * 3
        + [_full(j256.shape)],
        out_specs=[_row(ts, IN_P), _full((1, 256)), _full((1, 128)), _full((1, 256)), _full((1, 128))],
        out_shape=[_sds((S, IN_P), CDT), _sds((1, 256), F32), _sds((1, 128), F32), _sds((1, 256), F32),
                   _sds((1, 128), F32)],
        scratch_shapes=[_TOKEN_SCRATCH(ts)],
        compiler_params=_cparams(1),
    )(h, dcq, dckv, dkr, *ddq, *ddk, *ddv, dgq, dgk, dgv, gq, gkv, ggq, ggk, *tg, j256)


def _lane_blocks(width):
    return [(j, slice(j * LANES, (j + 1) * LANES)) for j in range(width // LANES)]


_TOKEN_SCRATCH = lambda ts: pltpu.VMEM((DIL_W // LANES, ts, LANES), F32)


def _by_token(res_ref, scr_ref, d):
    n = res_ref.shape[1]
    if d == 1:
        return res_ref[0].astype(F32)
    for c in range(d):
        for j, lanes in _lane_blocks(res_ref.shape[2]):
            scr_ref.at[j][pl.ds(c, n, stride=d), :] = res_ref[c, :, lanes].astype(F32)
    return jnp.concatenate([scr_ref[j] for j, _ in _lane_blocks(res_ref.shape[2])], axis=1)


def _by_residue(val, scr_ref, out_ref, d):
    n = out_ref.shape[1]
    if d == 1:
        out_ref[0] = val.astype(out_ref.dtype)
        return
    for j, lanes in _lane_blocks(out_ref.shape[2]):
        scr_ref[j] = val[:, lanes]
    for c in range(d):
        for j, lanes in _lane_blocks(out_ref.shape[2]):
            out_ref[c, :, lanes] = scr_ref.at[j][pl.ds(c, n, stride=d), :].astype(out_ref.dtype)


def _mla_prep_fwd(qa, kvp, kr, tm, scale):
    S = qa.shape[0]
    ts = min(512, S)

    def body(qa_ref, kv_ref, kr_ref, cm, smm, spm, q_o, k_o):
        tabm = (cm[...], smm[...], spm[...])
        q_o[...] = (_rope(qa_ref[...], tabm) * scale).astype(CDT)
        k_o[...] = kv_ref[:, 0:768] + _lanes(kr_ref[...], 768)

    return pl.pallas_call(
        body, name="mla_prep_fwd", grid=(S // ts,),
        in_specs=[_row(ts, 768), _row(ts, 1152), _row(ts, 128)] + [_row(ts, LANES)] * 3,
        out_specs=[_row(ts, 768)] * 2, out_shape=[_sds((S, 768), CDT)] * 2,
        compiler_params=_cparams(1),
    )(qa, kvp, kr, *tm)


def _mla_prep_bwd(dq, dk, dv, tm, scale):
    S = dq.shape[0]
    ts = min(512, S)

    def body(dq_ref, dk_ref, dv_ref, cm, smm, spm, dqa_o, dkv_o, dkr_o):
        tabm = (cm[...], smm[...], spm[...])
        lane = lax.broadcasted_iota(jnp.int32, (1, LANES), 1)
        dqa_o[...] = _rope_t(dq_ref[...] * scale, tabm).astype(CDT)
        dkr = jnp.zeros((ts, LANES), F32)
        for hd in range(MLA_HEADS):
            blk = dk_ref[:, hd * 128:(hd + 1) * 128]
            dkv_o[:, hd * 128:(hd + 1) * 128] = jnp.where(lane < 64, blk, 0.0).astype(CDT)
            dkr = dkr + jnp.where((lane >= 64) & (lane < 96), blk, 0.0)
        dkv_o[:, 768:1152] = dv_ref[...].astype(CDT)
        dkr_o[...] = jnp.where((lane >= 64) & (lane < 96), _rope_t(dkr, tabm), 0.0)

    return pl.pallas_call(
        body, name="mla_prep_bwd", grid=(S // ts,),
        in_specs=[_row(ts, 768), _row(ts, 768), _row(ts, 384)] + [_row(ts, LANES)] * 3,
        out_specs=[_row(ts, 768), _row(ts, 1152), _row(ts, 128)],
        out_shape=[_sds((S, 768), CDT), _sds((S, 1152), CDT), _sds((S, 128), F32)],
        compiler_params=_cparams(1),
    )(dq, dk, dv, *tm)


def _ln_fwd(xa, xb, g, b, alpha, name):
    S, D = xa.shape
    ts = min(512, S)

    def body(xa_ref, xb_ref, g_ref, b_ref, y_o, yb_o, z_o):
        z = alpha * xa_ref[...] + xb_ref[...]
        mu = jnp.mean(z, axis=-1, keepdims=True)
        zc = z - mu
        var = jnp.mean(zc * zc, axis=-1, keepdims=True)
        y = zc * lax.rsqrt(var + 1e-5) * g_ref[...] + b_ref[...]
        y_o[...] = y
        yb_o[...] = y.astype(CDT)
        z_o[...] = z

    return pl.pallas_call(
        body, name=name, grid=(S // ts,),
        in_specs=[_row(ts, D), _row(ts, D), _full(g.shape), _full(b.shape)],
        out_specs=[_row(ts, D)] * 3, out_shape=[_sds((S, D), F32), _sds((S, D), CDT), _sds((S, D), F32)],
        compiler_params=_cparams(1),
    )(xa, xb, g, b)


def _ln_bwd(dya, dyb, z, g, alpha, name, loss_head=False):
    S, D = z.shape
    ts = min(512, S)

    def body(dya_ref, dyb_ref, z_ref, g_ref, dz_o, dzb_o, dg_o, db_o, *loss_o):
        first = pl.program_id(0) == 0
        if loss_head:
            err = dya_ref[...] - dyb_ref[...]
            dy = err * (1.0 / D)
            part = jnp.sum(jnp.sum(err * err, axis=1, keepdims=True), axis=0, keepdims=True) * (0.5 / D)

            @pl.when(first)
            def _():
                loss_o[0][...] = part

            @pl.when(jnp.logical_not(first))
            def _():
                loss_o[0][...] += part
        else:
            dy = dya_ref[...] + alpha * dyb_ref[...]
        z = z_ref[...]
        mu = jnp.mean(z, axis=-1, keepdims=True)
        zc = z - mu
        r = lax.rsqrt(jnp.mean(zc * zc, axis=-1, keepdims=True) + 1e-5)
        xh = zc * r
        dxh = dy * g_ref[...]
        dz = r * (dxh - jnp.mean(dxh, axis=-1, keepdims=True) - xh * jnp.mean(dxh * xh, axis=-1, keepdims=True))
        dz_o[...] = dz
        dzb_o[...] = dz.astype(CDT)
        sg = jnp.sum(dy * xh, axis=0, keepdims=True)
        sb = jnp.sum(dy, axis=0, keepdims=True)

        @pl.when(first)
        def _():
            dg_o[...] = sg
            db_o[...] = sb

        @pl.when(jnp.logical_not(first))
        def _():
            dg_o[...] += sg
            db_o[...] += sb

    extra = ([_full((1, 1))], [_sds((1, 1), F32)]) if loss_head else ([], [])
    return pl.pallas_call(
        body, name=name, grid=(S // ts,),
        in_specs=[_row(ts, D)] * 3 + [_full(g.shape)],
        out_specs=[_row(ts, D), _row(ts, D), _full((1, D)), _full((1, D))] + extra[0],
        out_shape=[_sds((S, D), F32), _sds((S, D), CDT), _sds((1, D), F32), _sds((1, D), F32)] + extra[1],
        compiler_params=_cparams(1),
    )(dya, dyb, z, g)


def _grp_spec(ts, w):
    return pl.BlockSpec((None, ts, w), lambda k, i: (k, i, 0))


def _ffn_up(xb, wg3, wu3):
    S, D = xb.shape
    G, Fc, _ = wg3.shape
    tm = _pick(S, 1024)
    wspec = pl.BlockSpec((None, Fc, D), lambda k, i: (k, 0, 0))

    def body(x_ref, wg_ref, wu_ref, g_o, u_o, a_o):
        x = x_ref[...]
        g = lax.dot_general(x, wg_ref[...], _NT, preferred_element_type=F32)
        u = lax.dot_general(x, wu_ref[...], _NT, preferred_element_type=F32)
        g_o[...] = g.astype(CDT)
        u_o[...] = u.astype(CDT)
        a_o[...] = (g / (1.0 + jnp.exp(-g)) * u).astype(CDT)

    return pl.pallas_call(
        body, name="ffn_up", grid=(G, S // tm),
        in_specs=[pl.BlockSpec((tm, D), lambda k, i: (i, 0)), wspec, wspec], out_specs=[_grp_spec(tm, Fc)] * 3,
        out_shape=[_sds((G, S, Fc), CDT)] * 3, compiler_params=_cparams(2),
    )(xb, wg3, wu3)


def _ffn_up_dx(dg3, du3, wg3, wu3):
    G, S, Fc = dg3.shape
    D = wg3.shape[2]
    tm = _pick(S, 512)
    wspec = pl.BlockSpec((G, Fc, D), lambda i: (0, 0, 0))
    aspec = pl.BlockSpec((G, tm, Fc), lambda i: (0, i, 0))

    def body(dg_ref, du_ref, wg_ref, wu_ref, o_ref):
        acc = None
        for k in range(G):
            for a_ref, w_ref in ((dg_ref, wg_ref), (du_ref, wu_ref)):
                part = jnp.dot(a_ref[k], w_ref[k], preferred_element_type=F32)
                acc = part if acc is None else acc + part
        o_ref[...] = acc

    return pl.pallas_call(
        body, name="ffn_up_dx", grid=(S // tm,), in_specs=[aspec, aspec, wspec, wspec],
        out_specs=pl.BlockSpec((tm, D), lambda i: (i, 0)), out_shape=_sds((S, D), F32), compiler_params=_cparams(1),
    )(dg3, du3, wg3, wu3)


def _ffn_down(act3, wd3):
    G, S, Fc = act3.shape
    D = wd3.shape[2]
    tm = _pick(S, 1024)

    def body(a_ref, w_ref, o_ref):
        acc = jnp.dot(a_ref[0], w_ref[0], preferred_element_type=F32)
        for k in range(1, G):
            acc = acc + jnp.dot(a_ref[k], w_ref[k], preferred_element_type=F32)
        o_ref[...] = acc

    return pl.pallas_call(
        body, name="ffn_down", grid=(S // tm,),
        in_specs=[pl.BlockSpec((G, tm, Fc), lambda i: (0, i, 0)), pl.BlockSpec((G, Fc, D), lambda i: (0, 0, 0))],
        out_specs=pl.BlockSpec((tm, D), lambda i: (i, 0)), out_shape=_sds((S, D), F32), compiler_params=_cparams(1),
    )(act3, wd3)


def _ffn_down_dx(dzb, wd3, g3, u3):
    S, D = dzb.shape
    G, Fc, _ = wd3.shape
    tm = _pick(S, 1024)

    def body(dz_ref, wd_ref, g_ref, u_ref, dg_o, du_o):
        da = lax.dot_general(dz_ref[...], wd_ref[...], _NT, preferred_element_type=F32)
        g = g_ref[...].astype(F32)
        sg = 1.0 / (1.0 + jnp.exp(-g))
        dg_o[...] = (da * u_ref[...].astype(F32) * (sg * (1.0 + g * (1.0 - sg)))).astype(CDT)
        du_o[...] = (da * (g * sg)).astype(CDT)

    return pl.pallas_call(
        body, name="ffn_down_dx", grid=(G, S // tm),
        in_specs=[pl.BlockSpec((tm, D), lambda k, i: (i, 0)), pl.BlockSpec((None, Fc, D), lambda k, i: (k, 0, 0)),
                  _grp_spec(tm, Fc), _grp_spec(tm, Fc)],
        out_specs=[_grp_spec(tm, Fc)] * 2, out_shape=[_sds((G, S, Fc), CDT)] * 2, compiler_params=_cparams(2),
    )(dzb, wd3, g3, u3)


def _axpy(a, b, alpha, name):
    S, D = a.shape
    ts = min(512, S)

    def body(a_ref, b_ref, o_ref):
        o_ref[...] = a_ref[...] + alpha * b_ref[...]

    return pl.pallas_call(
        body, name=name, grid=(S // ts,), in_specs=[_row(ts, D)] * 2, out_specs=_row(ts, D),
        out_shape=_sds((S, D), F32), compiler_params=_cparams(1),
    )(a, b)


def _pair_masks():
    lane = lax.broadcasted_iota(jnp.int32, (1, LANES), 1)
    first = lane < HEAD_DIM
    return first, jnp.logical_not(first)


def _head_scalar(x, m):
    return jnp.max(jnp.where(m, x, -jnp.inf), axis=-1, keepdims=True)


_NT = (((1,), (1,)), ((), ()))
_TN = (((0,), (0,)), ((), ()))
ATTN_KEY_CHUNK = 1024
ATTN_FWD_TILES_PER_STEP = 4


def _attn_fwd(q, k, v, *, split, npairs, kblk, vblk, name):
    S = q.shape[0]
    qw = 256 if split else LANES
    tq = min(256, S)
    nsub = ATTN_FWD_TILES_PER_STEP if S % (ATTN_FWD_TILES_PER_STEP * tq) == 0 else 1

    def body(q_ref, k_ref, v_ref, o_ref, lse_ref):
        masks = _pair_masks()
        for j in range(nsub):
            rows = slice(j * tq, (j + 1) * tq)
            outs, lses = [], []
            for hd in range(2):
                if split:
                    qh = q_ref[rows, hd * LANES:(hd + 1) * LANES]
                    kh = k_ref[:, hd * LANES:(hd + 1) * LANES]
                else:
                    qh = jnp.where(masks[hd], q_ref[rows, :], jnp.zeros_like(q_ref[rows, :]))
                    kh = k_ref[...]
                s = lax.dot_general(qh, kh, _NT, preferred_element_type=F32)
                mx = jnp.max(s, axis=-1, keepdims=True)
                p = jnp.exp(s - mx)
                l = jnp.sum(p, axis=-1, keepdims=True)
                o = jnp.dot(p.astype(CDT), v_ref[...], preferred_element_type=F32)
                outs.append(o / l)
                lses.append(jnp.broadcast_to(mx + jnp.log(l), (tq, LANES)))
            o_ref[rows, :] = jnp.where(masks[0], outs[0], outs[1]).astype(o_ref.dtype)
            lse_ref[rows, :] = jnp.where(masks[0], lses[0], lses[1])

    return pl.pallas_call(
        body, name=name, grid=(npairs, S // (nsub * tq)),
        in_specs=[pl.BlockSpec((nsub * tq, qw), lambda p, i: (i, p)),
                  pl.BlockSpec((S, qw), lambda p, i: (0, kblk(p))),
                  pl.BlockSpec((S, LANES), lambda p, i: (0, vblk(p)))],
        out_specs=[pl.BlockSpec((nsub * tq, LANES), lambda p, i: (i, p))] * 2,
        out_shape=[_sds((S, LANES * npairs), CDT), _sds((S, LANES * npairs), F32)],
        compiler_params=_cparams(2),
    )(q, k, v)


def _attn_bwd(q, k, v, do, o, lse, *, split, npairs, kblk, vblk, doblk, shared_kv, name):
    S = q.shape[0]
    qw = 256 if split else LANES
    tq = min(512, S)
    tkv = min(ATTN_KEY_CHUNK, S)
    nkv = 1 if shared_kv else npairs

    def body(q_ref, k_ref, v_ref, do_ref, o_ref, lse_ref, dq_ref, dk_ref, dv_ref):
        masks = _pair_masks()
        p_id, i_id = pl.program_id(0), pl.program_id(1)
        first = (i_id == 0) & ((p_id == 0) if shared_kv else True)
        @pl.when(first)
        def _():
            dk_ref[...] = jnp.zeros_like(dk_ref)
            dv_ref[...] = jnp.zeros_like(dv_ref)

        do = do_ref[...]
        o = o_ref[...].astype(F32)
        lse = lse_ref[...]
        heads = []
        for hd in range(2):
            m = masks[hd]
            cols = slice(hd * LANES, (hd + 1) * LANES) if split else slice(None)
            qh = q_ref[:, cols] if split else jnp.where(m, q_ref[...], jnp.zeros_like(q_ref[...]))
            doh = jnp.where(m, do, 0.0)
            heads.append((m, cols, qh, doh.astype(CDT), _head_scalar(lse, m), jnp.sum(doh * o, axis=-1, keepdims=True)))
        dqs = [jnp.zeros((tq, LANES), F32), jnp.zeros((tq, LANES), F32)]
        for ck in range(S // tkv):
            rows = slice(ck * tkv, (ck + 1) * tkv)
            v = v_ref[rows, :]
            dv = jnp.zeros((tkv, LANES), F32)
            for hd, (m, cols, qh, dohb, lse_h, delta) in enumerate(heads):
                kh = k_ref[rows, cols]
                s = lax.dot_general(qh, kh, _NT, preferred_element_type=F32)
                p = jnp.exp(s - lse_h)
                dp = lax.dot_general(dohb, v, _NT, preferred_element_type=F32)
                ds = (p * (dp - delta)).astype(CDT)
                dq = jnp.dot(ds, kh, preferred_element_type=F32)
                dqs[hd] = dqs[hd] + (dq if split else jnp.where(m, dq, 0.0))
                dk_ref[rows, cols] += lax.dot_general(ds, qh, _TN, preferred_element_type=F32)
                dv = dv + lax.dot_general(p.astype(CDT), dohb, _TN, preferred_element_type=F32)
            dv_ref[rows, :] += dv
        if split:
            dq_ref[:, 0:LANES] = dqs[0]
            dq_ref[:, LANES:2 * LANES] = dqs[1]
        else:
            dq_ref[...] = dqs[0] + dqs[1]

    kvo = (lambda p, i: (0, 0)) if shared_kv else (lambda p, i: (0, p))
    return pl.pallas_call(
        body, name=name, grid=(npairs, S // tq),
        in_specs=[pl.BlockSpec((tq, qw), lambda p, i: (i, p)),
                  pl.BlockSpec((S, qw), lambda p, i: (0, kblk(p))),
                  pl.BlockSpec((S, LANES), lambda p, i: (0, vblk(p))),
                  pl.BlockSpec((tq, LANES), lambda p, i: (i, doblk(p))),
                  pl.BlockSpec((tq, LANES), lambda p, i: (i, p)),
                  pl.BlockSpec((tq, LANES), lambda p, i: (i, p))],
        out_specs=[pl.BlockSpec((tq, qw), lambda p, i: (i, p)),
                   pl.BlockSpec((S, qw), kvo), pl.BlockSpec((S, LANES), kvo)],
        out_shape=[_sds((S, qw * npairs), F32), _sds((S, qw * nkv), F32), _sds((S, LANES * nkv), F32)],
        compiler_params=_cparams(2),
    )(q, k, v, do, o, lse)


def _bias_expand(idx, rel_bias, name):
    tq, kw = idx.shape

    def body(idx_ref, rb_ref, o_ref):
        idx = idx_ref[...]
        for hd in range(DIL_HEADS):
            acc = jnp.full((tq, kw), NEG_INF, F32)
            for u in range(REL_BUCKETS):
                acc = jnp.where(idx == u, rb_ref[u, hd], acc)
            o_ref[hd] = acc

    return pl.pallas_call(
        body, name=name,
        in_specs=[pl.BlockSpec(memory_space=pltpu.VMEM), pl.BlockSpec(memory_space=pltpu.SMEM)],
        out_specs=pl.BlockSpec(memory_space=pltpu.VMEM),
        out_shape=_sds((DIL_HEADS, tq, kw), F32),
    )(idx, rel_bias)


def _bias_reduce(idx, dtab, name):
    tq, kw = idx.shape

    def body(idx_ref, d_ref, o_ref):
        idx = idx_ref[...]
        rowid = lax.broadcasted_iota(jnp.int32, (REL_BUCKETS, kw), 0)
        for hd in range(DIL_HEADS):
            d = d_ref[hd]
            acc = jnp.zeros((REL_BUCKETS, kw), F32)
            for u in range(REL_BUCKETS):
                r = jnp.sum(jnp.where(idx == u, d, 0.0), axis=0, keepdims=True)
                acc = jnp.where(rowid == u, r, acc)
            o_ref[hd] = jnp.sum(acc, axis=1, keepdims=True)

    return pl.pallas_call(
        body, name=name,
        in_specs=[pl.BlockSpec(memory_space=pltpu.VMEM)] * 2, out_specs=pl.BlockSpec(memory_space=pltpu.VMEM),
        out_shape=_sds((DIL_HEADS, REL_BUCKETS, 1), F32),
    )(idx, dtab)


DIL_TILES_PER_STEP = 4


def _dil_tiles_per_step(L, tq):
    return DIL_TILES_PER_STEP if L % (DIL_TILES_PER_STEP * tq) == 0 else 1


def _dil_window(i, tq, kw, L):
    start = pl.multiple_of(i * tq, DIL_HALF)
    key = start + lax.broadcasted_iota(jnp.int32, (1, kw), 1) - DIL_HALF
    return start, (key >= 0) & (key < L)


def _dil_fwd(qv, kv, vv, tab, *, dil, L, tq, name):
    kw = tq + 2 * DIL_HALF
    npair = DIL_HEADS // 2
    nsub = _dil_tiles_per_step(L, tq)

    def body(q_ref, k_ref, v_ref, t_ref, o_ref, lse_ref):
        masks = _pair_masks()
        for j in range(nsub):
            rows = slice(j * tq, (j + 1) * tq)
            start, valid = _dil_window(pl.program_id(2) * nsub + j, tq, kw, L)
            kwin = k_ref[pl.ds(start, kw), :]
            vwin = v_ref[pl.ds(start, kw), :]
            q = q_ref[rows, :]
            outs, lses = [], []
            for hd in range(2):
                qh = jnp.where(masks[hd], q, jnp.zeros_like(q))
                s = lax.dot_general(qh, kwin, _NT, preferred_element_type=F32) + t_ref[hd]
                s = jnp.where(valid, s, NEG_INF)
                mx = jnp.max(s, axis=-1, keepdims=True)
                p = jnp.exp(s - mx)
                l = jnp.sum(p, axis=-1, keepdims=True)
                outs.append(jnp.dot(p.astype(CDT), vwin, preferred_element_type=F32) / l)
                lses.append(jnp.broadcast_to(mx + jnp.log(l), (tq, LANES)))
            o_ref[rows, :] = jnp.where(masks[0], outs[0], outs[1])
            lse_ref[rows, :] = jnp.where(masks[0], lses[0], lses[1])

    blk = pl.BlockSpec((None, nsub * tq, LANES), lambda p, c, i: (c, i, p))
    res = pl.BlockSpec((None, L + 2 * DIL_HALF, LANES), lambda p, c, i: (c, 0, p))
    return pl.pallas_call(
        body, name=name, grid=(npair, dil, L // (nsub * tq)),
        in_specs=[blk, res, res, pl.BlockSpec((2, tq, kw), lambda p, c, i: (p, 0, 0))],
        out_specs=[blk] * 2, out_shape=[_sds(qv.shape, F32)] * 2,
        compiler_params=_cparams(3),
    )(qv, kv, vv, tab)


def _dil_bwd(qv, kv, vv, tab, dov, lsev, deltav, *, dil, L, tq, name):
    kw = tq + 2 * DIL_HALF
    npair = DIL_HEADS // 2
    nsub = _dil_tiles_per_step(L, tq)

    def body(q_ref, k_ref, v_ref, t_ref, do_ref, lse_ref, dl_ref, dq_ref, dk_ref, dv_ref, dt_ref):
        masks = _pair_masks()
        c_id, i_id = pl.program_id(1), pl.program_id(2)

        @pl.when(i_id == 0)
        def _():
            dk_ref[...] = jnp.zeros_like(dk_ref)
            dv_ref[...] = jnp.zeros_like(dv_ref)

        @pl.when((i_id == 0) & (c_id == 0))
        def _():
            dt_ref[...] = jnp.zeros_like(dt_ref)

        dts = [jnp.zeros((tq, kw), F32), jnp.zeros((tq, kw), F32)]
        for j in range(nsub):
            rows = slice(j * tq, (j + 1) * tq)
            start, valid = _dil_window(i_id * nsub + j, tq, kw, L)
            kwin = k_ref[pl.ds(start, kw), :]
            vwin = v_ref[pl.ds(start, kw), :]
            q, do, lse, dl = q_ref[rows, :], do_ref[rows, :], lse_ref[rows, :], dl_ref[rows, :]
            dq = jnp.zeros((tq, LANES), F32)
            dk = jnp.zeros((kw, LANES), F32)
            dv = jnp.zeros((kw, LANES), F32)
            for hd in range(2):
                m = masks[hd]
                qh = jnp.where(m, q, jnp.zeros_like(q))
                doh = jnp.where(m, do, jnp.zeros_like(do))
                s = lax.dot_general(qh, kwin, _NT, preferred_element_type=F32) + t_ref[hd]
                s = jnp.where(valid, s, NEG_INF)
                p = jnp.exp(s - _head_scalar(lse, m))
                dp = lax.dot_general(doh, vwin, _NT, preferred_element_type=F32)
                ds = p * (dp - _head_scalar(dl, m))
                dts[hd] = dts[hd] + ds
                dsb = ds.astype(CDT)
                dq = dq + jnp.where(m, jnp.dot(dsb, kwin, preferred_element_type=F32), 0.0)
                dk = dk + lax.dot_general(dsb, qh, _TN, preferred_element_type=F32)
                dv = dv + lax.dot_general(p.astype(CDT), doh, _TN, preferred_element_type=F32)
            dq_ref[rows, :] = dq
            dk_ref[pl.ds(start, kw), :] += dk
            dv_ref[pl.ds(start, kw), :] += dv
        for hd in range(2):
            dt_ref[hd] += dts[hd]

    blk = pl.BlockSpec((None, nsub * tq, LANES), lambda p, c, i: (c, i, p))
    res = pl.BlockSpec((None, L + 2 * DIL_HALF, LANES), lambda p, c, i: (c, 0, p))
    tsp = pl.BlockSpec((2, tq, kw), lambda p, c, i: (p, 0, 0))
    return pl.pallas_call(
        body, name=name, grid=(npair, dil, L // (nsub * tq)),
        in_specs=[blk, res, res, tsp, blk, blk, blk], out_specs=[blk, res, res, tsp],
        out_shape=[_sds(qv.shape, F32), _sds(kv.shape, F32), _sds(kv.shape, F32), _sds(tab.shape, F32)],
        compiler_params=_cparams(3),
    )(qv, kv, vv, tab, dov, lsev, deltav)


def _mix_weights(l1, l2, l3):
    mx = jnp.maximum(jnp.maximum(l1, l2), l3)
    e1, e2, e3 = jnp.exp(l1 - mx), jnp.exp(l2 - mx), jnp.exp(l3 - mx)
    inv = 1.0 / (e1 + e2 + e3)
    return e1 * inv, e2 * inv, e3 * inv


def _branch_specs(S, ts):
    dils = [d for _, d in DIL_BRANCHES]
    return dils, [_res_spec(d, ts // d) for d in dils], [(d, S // d, DIL_W) for d in dils]


def _dil_mix_fwd(os, ls):
    S = os[0].shape[0] * os[0].shape[1]
    ts = min(512, S)
    dils, specs, _ = _branch_specs(S, ts)

    def body(o1, o2, o3, l1, l2, l3, out, scr):
        o1, o2, o3, l1, l2, l3 = [_by_token(r, scr, d) for r, d in zip((o1, o2, o3, l1, l2, l3), dils + dils)]
        w1, w2, w3 = _mix_weights(l1, l2, l3)
        out[...] = (w1 * o1 + w2 * o2 + w3 * o3).astype(CDT)

    return pl.pallas_call(
        body, name="dil_mix_fwd", grid=(S // ts,), in_specs=specs + specs, out_specs=_row(ts, DIL_W),
        out_shape=_sds((S, DIL_W), CDT), scratch_shapes=[_TOKEN_SCRATCH(ts)],
        compiler_params=_cparams(1),
    )(*os, *ls)


def _dil_mix_bwd(dcat, os, ls, j384):
    S = os[0].shape[0] * os[0].shape[1]
    ts = min(512, S)
    dils, specs, shapes = _branch_specs(S, ts)

    def body(do_ref, o1, o2, o3, l1, l2, l3, j_ref, d1, d2, d3, e1, e2, e3, scr):
        o1, o2, o3, l1, l2, l3 = [_by_token(r, scr, d) for r, d in zip((o1, o2, o3, l1, l2, l3), dils + dils)]
        ws = _mix_weights(l1, l2, l3)
        do = do_ref[...]
        o = ws[0] * o1 + ws[1] * o2 + ws[2] * o3
        dot = _headsum(do * o, j_ref[...])
        for w, d, d_o, e_o in zip(ws, dils, (d1, d2, d3), (e1, e2, e3)):
            _by_residue(w * do, scr, d_o, d)
            _by_residue(w * dot, scr, e_o, d)

    return pl.pallas_call(
        body, name="dil_mix_bwd", grid=(S // ts,),
        in_specs=[_row(ts, DIL_W, 1)] + specs + specs + [_full(j384.shape)],
        out_specs=specs + specs,
        out_shape=[_sds(s, CDT) for s in shapes] + [_sds(s, F32) for s in shapes],
        scratch_shapes=[_TOKEN_SCRATCH(ts)],
        compiler_params=_cparams(1),
    )(dcat, *os, *ls, j384)


def _adamw_math(w, g, m, v):
    m = ADAM_B1 * m + (1.0 - ADAM_B1) * g
    v = ADAM_B2 * v + (1.0 - ADAM_B2) * (g * g)
    m_hat = m / (1.0 - ADAM_B1 ** ADAM_STEP)
    v_hat = v / (1.0 - ADAM_B2 ** ADAM_STEP)
    delta = -ADAM_LR * (m_hat / (jnp.sqrt(v_hat) + ADAM_EPS) + ADAM_WD * w)
    return delta, m, v


def _pick8(n, target):
    best = None
    for t in range(16, min(n, target) + 1, 16):
        if n % t == 0:
            best = t
    return best if best is not None else n


_ELEMS_PER_BLOCK = 256 * 1024


def _adamw(w, reds, sibs, m, v, owner, name):
    L, a, b = w.shape
    ta = _pick8(a, max(16, _ELEMS_PER_BLOCK // b))
    spec = pl.BlockSpec((None, ta, b), lambda l, i, own: (l, i, 0))
    def gspec(mine, layer):
        def index(l, i, own):
            use = (own[0] if mine else 1 - own[0]) * (l if layer else 1 - l)
            return i * use, 0
        return pl.BlockSpec((ta, b), index)

    def body(own_ref, w_ref, r0_ref, r1_ref, s0_ref, s1_ref, m_ref, v_ref, g_o, d_o, m_o, v_o):
        mine = own_ref[0] == 1
        g0 = jnp.where(mine, r0_ref[...], s0_ref[...])
        g1 = jnp.where(mine, r1_ref[...], s1_ref[...])
        g = jnp.where(pl.program_id(0) == 0, g0, g1)
        d, mm, vv = _adamw_math(w_ref[...], g, m_ref[...], v_ref[...])
        g_o[...] = g
        d_o[...] = d
        m_o[...] = mm
        v_o[...] = vv

    return pl.pallas_call(
        body, name=name, out_shape=[_sds(w.shape, F32)] * 4,
        grid_spec=pltpu.PrefetchScalarGridSpec(
            num_scalar_prefetch=1, grid=(L, a // ta),
            in_specs=[spec, gspec(True, 0), gspec(True, 1), gspec(False, 0), gspec(False, 1), spec, spec],
            out_specs=[spec] * 4),
        compiler_params=_cparams(2),
    )(_is_core(owner), w, *reds, *sibs, m, v)


def _adamw_small(w, gall, m, v):
    R = w.shape[0]

    def body(w_ref, g_ref, m_ref, v_ref, g_o, d_o, m_o, v_o):
        g = g_ref[0]
        for k in range(1, 8):
            g = g + g_ref[k]
        d, mm, vv = _adamw_math(w_ref[...], g, m_ref[...], v_ref[...])
        g_o[...] = g
        d_o[...] = d
        m_o[...] = mm
        v_o[...] = vv

    vm = pl.BlockSpec(memory_space=pltpu.VMEM)
    return pl.pallas_call(
        body, name="adamw_small", in_specs=[vm] * 4, out_specs=[vm] * 4, out_shape=[_sds((R, LANES), F32)] * 4,
    )(w, gall, m, v)


def _sum_pair(g, t, owner, name):
    n, a, b = t.shape
    ta = _pick8(a, max(16, _ELEMS_PER_BLOCK // b))
    spec = pl.BlockSpec((None, ta, b), lambda k, i, own: (k * own[0], i * own[0], 0))

    def body(own_ref, g_ref, t_ref, o_ref):
        @pl.when(own_ref[0] == 1)
        def _():
            o_ref[...] = (g_ref[...].astype(F32) + t_ref[...].astype(F32)).astype(o_ref.dtype)

    return pl.pallas_call(
        body, name=name, out_shape=_sds(t.shape, WIRE),
        grid_spec=pltpu.PrefetchScalarGridSpec(num_scalar_prefetch=1, grid=(n, a // ta), in_specs=[spec] * 2,
                                               out_specs=spec),
        compiler_params=_cparams(2),
    )(_is_core(owner), g, t)


def _sum_chips(pair, t, owner, name):
    _, a, b = t.shape
    ta = _pick8(a, max(16, _ELEMS_PER_BLOCK // b))

    def body(own_ref, p_ref, t_ref, o_ref):
        @pl.when(own_ref[0] == 1)
        def _():
            me = 2 * lax.axis_index("x") + lax.axis_index("y")
            acc = p_ref[me].astype(F32)
            for k in range(3):
                acc = acc + t_ref[k].astype(F32)
            o_ref[...] = acc

    return pl.pallas_call(
        body, name=name, out_shape=_sds((a, b), F32),
        grid_spec=pltpu.PrefetchScalarGridSpec(
            num_scalar_prefetch=1, grid=(a // ta,),
            in_specs=[pl.BlockSpec((4, ta, b), lambda i, own: (0, i * own[0], 0)),
                      pl.BlockSpec((3, ta, b), lambda i, own: (0, i * own[0], 0))],
            out_specs=pl.BlockSpec((ta, b), lambda i, own: (i * own[0], 0))),
        compiler_params=_cparams(1),
    )(_is_core(owner), pair, t)


def _is_core(core):
    return (lax.axis_index("c") == core).astype(jnp.int32).reshape(1)


_HBM = pl.BlockSpec(memory_space=pltpu.HBM)


def _place():
    x, y, c = lax.axis_index("x"), lax.axis_index("y"), lax.axis_index("c")
    chips = [(1 - x, y), (x, 1 - y), (1 - x, 1 - y)]
    return x, y, c, chips


def _remote(src, dst, ssem, rsem, to):
    return pltpu.make_async_remote_copy(src_ref=src, dst_ref=dst, send_sem=ssem, recv_sem=rsem, device_id=to,
                                        device_id_type=MESH_ID)


def _dma_sems(n):
    return pltpu.SemaphoreType.DMA((n,))


_SEM = pl.BlockSpec(memory_space=pltpu.SEMAPHORE)
_ANY = pl.BlockSpec(memory_space=pl.ANY)
_EFFECT = pltpu.SideEffectType.DATAFLOW_SIDE_EFFECTING
_BIG = ("w_in", "mla_w_uq", "mla_w_ukv", "w_out", "ffn_w_gate", "ffn_w_up", "ffn_w_down")
_OWNER = dict(zip(_BIG, (1, 0, 0, 1, 0, 0, 1)))
_ATTN_WEIGHTS, _FFN_WEIGHTS = _BIG[:4], _BIG[4:]


def _hbm(a):
    return pltpu.with_memory_space_constraint(a, pltpu.HBM)


def _per_core(c, owners, fn):
    for g in range(2):
        mine = tuple(p for p, o in enumerate(owners) if o == g)
        theirs = tuple(p for p, o in enumerate(owners) if o != g)
        pl.when(c == g)(functools.partial(fn, mine, theirs))


def _token_spec():
    return pl.BlockSpec(memory_space=pltpu.VMEM), _sds((8, LANES), F32)


def _gather_start(shards, owners, layer, tag):
    n = len(shards)
    lands = [_hbm(lax.empty((4,) + s.shape[1:], s.dtype)) for s in shards]

    def body(*refs):
        w_refs, l_refs = refs[:n], refs[n:2 * n]
        ssem, rsem, token = refs[2 * n], refs[2 * n + 1], refs[-1]
        x, y, c, chips = _place()
        me = 2 * x + y

        def send(mine, _):
            for i in mine:
                for j, (cx, cy) in enumerate(chips):
                    _remote(w_refs[i].at[layer], l_refs[i].at[me], ssem.at[3 * i + j], rsem.at[3 * i + j],
                            (cx, cy, c)).start()

        _per_core(c, owners, send)
        token[...] = jnp.zeros_like(token)

    tspec, tshape = _token_spec()
    out = pl.pallas_call(
        body, name=f"gather_start_{tag}", in_specs=[_HBM] * (2 * n),
        out_specs=[_SEM, _SEM] + [_HBM] * n + [tspec],
        out_shape=[_dma_sems(3 * n), _dma_sems(3 * n)] + [pltpu.HBM(l.shape, l.dtype) for l in lands] + [tshape],
        input_output_aliases={n + i: 2 + i for i in range(n)},
        compiler_params=pltpu.CompilerParams(has_side_effects=_EFFECT),
    )(*[_hbm(s) for s in shards], *lands)
    return out[0], out[1], list(out[2:2 + n]), out[-1]


def _gather_wait(ssem, rsem, shards, lands, after, owners, layer, tag):
    n = len(shards)

    def body(*refs):
        w_refs, l_refs = refs[:n], refs[n:2 * n]
        ssem, rsem = refs[2 * n], refs[2 * n + 1]
        x, y, c, chips = _place()

        def wait(mine, _):
            for i in mine:
                for j, (cx, cy) in enumerate(chips):
                    cp = _remote(w_refs[i].at[layer], l_refs[i].at[2 * cx + cy], ssem.at[3 * i + j], rsem.at[3 * i + j],
                                 (cx, cy, c))
                    cp.wait_send()
                    cp.wait_recv()

        _per_core(c, owners, wait)

    return list(pl.pallas_call(
        body, name=f"gather_wait_{tag}", in_specs=[_HBM] * (2 * n) + [_SEM, _SEM, _ANY], out_specs=[_HBM] * n,
        out_shape=[pltpu.HBM(l.shape, l.dtype) for l in lands],
        input_output_aliases={n + i: i for i in range(n)},
        compiler_params=pltpu.CompilerParams(has_side_effects=_EFFECT),
    )(*[_hbm(s) for s in shards], *lands, ssem, rsem, after))


def _gather_finish(shards, lands, owners, layer, tag):
    n = len(shards)

    def body(*refs):
        w_refs, g_refs = refs[:n], refs[2 * n:3 * n]
        ssem, rsem = refs[3 * n:]
        x, y, c, chips = _place()
        me = 2 * x + y
        sib = (x, y, 1 - c)
        owns = [_remote(w.at[layer], g.at[me], ssem.at[i], rsem.at[i], sib) for i, (w, g) in enumerate(zip(w_refs, g_refs))]
        for cp in owns:
            cp.start()

        def forward(mine, theirs):
            def blk(i, j):
                b = g_refs[i].at[2 * chips[j][0] + chips[j][1]]
                return _remote(b, b, ssem.at[n + 3 * i + j], rsem.at[n + 3 * i + j], sib)

            for i in mine:
                for j in range(3):
                    blk(i, j).start()
            for i in theirs:
                for j in range(3):
                    blk(i, j).wait_recv()
            for i in mine:
                for j in range(3):
                    blk(i, j).wait_send()

        _per_core(c, owners, forward)
        for cp in owns:
            cp.wait_recv()
            cp.wait_send()

    return list(pl.pallas_call(
        body, name=f"gather_finish_{tag}", in_specs=[_HBM] * (2 * n), out_specs=[_HBM] * n,
        out_shape=[_sds(l.shape, l.dtype) for l in lands], input_output_aliases={n + i: i for i in range(n)},
        scratch_shapes=[_dma_sems(4 * n), _dma_sems(4 * n)],
    )(*shards, *lands))


def _rs_to_owner(grads, owners, tag):
    n = len(grads)

    def body(*refs):
        g_refs, t_refs = refs[:n], refs[n:2 * n]
        ssem, rsem = refs[2 * n:]
        x, y, c, _ = _place()

        def swap(mine, theirs):
            cps = [_remote(g_refs[i], t_refs[i], ssem.at[i], rsem.at[i], (x, y, 1 - c)) for i in theirs]
            for cp in cps:
                cp.start()
            for i in mine:
                _remote(g_refs[i], t_refs[i], ssem.at[i], rsem.at[i], (x, y, 1 - c)).wait_recv()
            for cp in cps:
                cp.wait_send()

        _per_core(c, owners, swap)

    return list(pl.pallas_call(
        body, name=f"rs_to_owner_{tag}", in_specs=[_HBM] * n, out_specs=[_HBM] * n,
        out_shape=[_sds(g.shape, g.dtype) for g in grads], scratch_shapes=[_dma_sems(n), _dma_sems(n)],
    )(*grads))


def _a2a_start(pairs, owners, tag):
    n = len(pairs)
    lands = [_hbm(lax.empty((3,) + p.shape[1:], p.dtype)) for p in pairs]

    def body(*refs):
        a_refs, t_refs = refs[:n], refs[n:2 * n]
        ssem, rsem, token = refs[2 * n], refs[2 * n + 1], refs[-1]
        x, y, c, chips = _place()

        def send(mine, _):
            for i in mine:
                for j, (cx, cy) in enumerate(chips):
                    _remote(a_refs[i].at[2 * cx + cy], t_refs[i].at[j], ssem.at[3 * i + j], rsem.at[3 * i + j],
                            (cx, cy, c)).start()

        _per_core(c, owners, send)
        token[...] = jnp.zeros_like(token)

    tspec, tshape = _token_spec()
    out = pl.pallas_call(
        body, name=f"rs_a2a_start_{tag}", in_specs=[_HBM] * (2 * n),
        out_specs=[_SEM, _SEM] + [_HBM] * n + [tspec],
        out_shape=[_dma_sems(3 * n), _dma_sems(3 * n)] + [pltpu.HBM(l.shape, l.dtype) for l in lands] + [tshape],
        input_output_aliases={n + i: 2 + i for i in range(n)},
        compiler_params=pltpu.CompilerParams(has_side_effects=_EFFECT),
    )(*[_hbm(p) for p in pairs], *lands)
    return out[0], out[1], list(out[2:2 + n]), out[-1]


def _a2a_wait(ssem, rsem, pairs, lands, after, owners, tag):
    n = len(pairs)

    def body(*refs):
        a_refs, t_refs = refs[:n], refs[n:2 * n]
        ssem, rsem = refs[2 * n], refs[2 * n + 1]
        x, y, c, chips = _place()

        def wait(mine, _):
            for i in mine:
                for j, (cx, cy) in enumerate(chips):
                    cp = _remote(a_refs[i].at[2 * cx + cy], t_refs[i].at[j], ssem.at[3 * i + j], rsem.at[3 * i + j],
                                 (cx, cy, c))
                    cp.wait_send()
                    cp.wait_recv()

        _per_core(c, owners, wait)

    return list(pl.pallas_call(
        body, name=f"rs_a2a_wait_{tag}", in_specs=[_HBM] * (2 * n) + [_SEM, _SEM, _ANY], out_specs=[_HBM] * n,
        out_shape=[pltpu.HBM(l.shape, l.dtype) for l in lands],
        input_output_aliases={n + i: i for i in range(n)},
        compiler_params=pltpu.CompilerParams(has_side_effects=_EFFECT),
    )(*[_hbm(p) for p in pairs], *lands, ssem, rsem, after))


def _rs_from_owner(reds, owners):
    n = len(reds)

    def body(*refs):
        q_refs, o_refs = refs[:n], refs[n:2 * n]
        ssem, rsem = refs[2 * n:]
        x, y, c, _ = _place()

        def swap(mine, theirs):
            cps = [_remote(q_refs[k], o_refs[k], ssem.at[k], rsem.at[k], (x, y, 1 - c)) for k in mine]
            for cp in cps:
                cp.start()
            for k in theirs:
                _remote(q_refs[k], o_refs[k], ssem.at[k], rsem.at[k], (x, y, 1 - c)).wait_recv()
            for cp in cps:
                cp.wait_send()

        _per_core(c, owners, swap)

    return list(pl.pallas_call(
        body, name="rs_from_owner", in_specs=[_HBM] * n, out_specs=[_HBM] * n,
        out_shape=[_sds(q.shape, q.dtype) for q in reds], scratch_shapes=[_dma_sems(n), _dma_sems(n)],
    )(*reds))


def _gather_small(s):
    R, _ = s.shape

    def body(s_ref, o_ref, ssem, rsem, lsem):
        x, y, c, _ = _place()
        me = 4 * x + 2 * y + c
        own = pltpu.make_async_copy(s_ref, o_ref.at[me], lsem)
        own.start()
        sends = []
        for k in range(1, 8):
            px, py, pc = x ^ (k >> 2), y ^ ((k >> 1) & 1), c ^ (k & 1)
            cp = _remote(s_ref, o_ref.at[me], ssem.at[k - 1], rsem.at[k - 1], (px, py, pc))
            cp.start()
            sends.append(cp)
        for k in range(1, 8):
            px, py, pc = x ^ (k >> 2), y ^ ((k >> 1) & 1), c ^ (k & 1)
            blk = o_ref.at[4 * px + 2 * py + pc]
            _remote(blk, blk, ssem.at[k - 1], rsem.at[k - 1], (px, py, pc)).wait_recv()
        for cp in sends:
            cp.wait_send()
        own.wait()

    vm = pl.BlockSpec(memory_space=pltpu.VMEM)
    return pl.pallas_call(
        body, name="gather_small", in_specs=[vm], out_specs=vm, out_shape=_sds((8, R, LANES), s.dtype),
        scratch_shapes=[pltpu.SemaphoreType.DMA((7,)), pltpu.SemaphoreType.DMA((7,)), pltpu.SemaphoreType.DMA],
    )(s)


_COL_SHARDED =("w_in", "mla_w_uq", "mla_w_ukv", "ffn_w_gate", "ffn_w_up")
_SMALL = ("mla_q_norm", "mla_kv_norm", "gqa_q_norm", "gqa_k_norm", "rel_bias", "ln1_g", "ln1_b", "ln2_g", "ln2_b")


def _pack_flat(arrs, align):
    flat = jnp.concatenate([a.reshape(-1) for a in arrs])
    pad = (-flat.shape[0]) % align
    return jnp.pad(flat, (0, pad)) if pad else flat


def _unpack_flat(flat, shapes):
    out, off = [], 0
    for s in shapes:
        n = int(np.prod(s))
        out.append(flat[off:off + n].reshape(s))
        off += n
    return out


def _perm_gqa_rows(w):
    return jnp.concatenate([w[:832], w[896:960], w[832:896], w[960:]], axis=0)


def _local_step(x, target, small, depth, weights_of_layer, grads_done):
    S, D = x.shape
    alpha = (2.0 * depth) ** 0.25
    in_idx, uq_idx, ukv_idx = _in_cols(), _uq_cols(), _ukv_cols()
    win, wuq, wukv, wout, wg, wu, wdn = ([None] * depth for _ in range(7))

    tm, tg = _rope_tables(S)
    j256, j384 = _head_ones(256), _head_ones(384)
    mla_scale = (64 + MLA_ROPE_DIM) ** -0.5
    branches = []
    for (_, dil) in DIL_BRANCHES:
        L = S // dil
        tq = min(256, L)
        idx = jnp.asarray(_branch_bucket_idx(tq, dil))
        branches.append((dil, L, tq, idx))
    tabs = [_bias_expand(idx, small["rel_bias"], name=f"bias_expand_{b}") for b, (_, _, _, idx) in enumerate(branches)]

    def padded(a):
        z = jnp.zeros((DIL_HALF, a.shape[1]), a.dtype)
        return jnp.concatenate([z, a, z], axis=0)[None]

    saved = []
    xf, xb = x, x.astype(CDT)
    for l in range(depth):
        W, token = weights_of_layer(l, "attn", xb)
        win[l] = _rows_from_shards(W["w_in"], in_idx)
        wuq[l] = _rows_from_shards(W["mla_w_uq"], uq_idx)
        wukv[l] = _rows_from_shards(W["mla_w_ukv"], ukv_idx)
        wout[l] = _perm_gqa_rows(W["w_out"].reshape(-1, D))
        gq, gkv = small["mla_q_norm"][l][None], small["mla_kv_norm"][l][None]
        if token is not None:
            gq = gq + token[0, 0]
        ggq = jnp.tile(small["gqa_q_norm"][l], 4)[None]
        ggk = jnp.tile(small["gqa_k_norm"][l], 2)[None]
        h = _mm(xb, win[l], tb=True, name="mm_in")
        cq, ckv, kr, qd, kd, vd, qg, kg, vg, *strided = _prep_fwd(h, gq, gkv, ggq, ggk, tm, tg, j256)
        qkv = [(qd[None], padded(kd), padded(vd))] + [tuple(strided[3 * b:3 * b + 3]) for b in range(len(DIL_STRIDES))]
        qa = _mm(cq, wuq[l], tb=True, name="mm_uq")
        kvp = _mm(ckv, wukv[l], tb=True, out_dtype=CDT, name="mm_ukv")
        qm, km = _mla_prep_fwd(qa, kvp, kr, tm, mla_scale)
        oa, lsa = _attn_fwd(qm, km, kvp, split=True, npairs=3, kblk=lambda p: p, vblk=lambda p: 6 + p,
                            name="mla_attn_fwd")
        oc, lsc = _attn_fwd(qg, kg, vg, split=False, npairs=2, kblk=lambda p: 0, vblk=lambda p: 0,
                            name="gqa_attn_fwd")
        obs, lbs = [], []
        for b, (dil, L, tq, _) in enumerate(branches):
            o_b, l_b = _dil_fwd(*qkv[b], tabs[b], dil=dil, L=L, tq=tq, name=f"dil_fwd_{b}")
            obs.append(o_b)
            lbs.append(l_b)
        ob = _dil_mix_fwd(obs, lbs)
        cat = jnp.concatenate([oa, ob, oc], axis=1)
        mix = _mm(cat, wout[l], name="mm_out")
        x1, x1b, z1 = _ln_fwd(xf, mix, small["ln1_g"][l][None], small["ln1_b"][l][None], alpha, name="ln1_fwd")
        W, _ = weights_of_layer(l, "ffn", x1b)
        wg[l], wu[l], wdn[l] = W["ffn_w_gate"], W["ffn_w_up"], W["ffn_w_down"]
        g3, u3, act = _ffn_up(x1b, wg[l], wu[l])
        ff = _ffn_down(act, wdn[l])
        x2, x2b, z2 = _ln_fwd(x1, ff, small["ln2_g"][l][None], small["ln2_b"][l][None], alpha, name="ln2_fwd")
        saved.append(dict(xb=xb, h=h, cq=cq, ckv=ckv, qg=qg, kg=kg, vg=vg, kvp=kvp, qm=qm, km=km, oa=oa, lsa=lsa,
                          oc=oc, lsc=lsc, obs=obs, lbs=lbs, qkv=qkv, cat=cat, z1=z1, x1b=x1b, g3=g3, u3=u3, act=act, z2=z2,
                          gq=gq, gkv=gkv, ggq=ggq, ggk=ggk))
        xf, xb = x2, x2b

    gW = {k: [None] * depth for k in _BIG}
    gS = {k: [None] * depth for k in ("mla_q_norm", "mla_kv_norm", "gqa_q_norm", "gqa_k_norm", "ln1_g", "ln1_b", "ln2_g",
                                      "ln2_b")}
    g_rel = None
    dya, dyb = xf, target
    token = None
    for l in reversed(range(depth)):
        sv = saved[l]
        ln2_g = small["ln2_g"][l][None]
        if token is not None:
            ln2_g = ln2_g + token[0, 0]
        if l == depth - 1:
            dz2, dz2b, gS["ln2_g"][l], gS["ln2_b"][l], loss = _ln_bwd(dya, dyb, sv["z2"], ln2_g, alpha,
                                                                       name="ln2_bwd_loss", loss_head=True)
        else:
            dz2, dz2b, gS["ln2_g"][l], gS["ln2_b"][l] = _ln_bwd(dya, dyb, sv["z2"], ln2_g, alpha, name="ln2_bwd")
        gW["ffn_w_down"][l] = _mm(sv["act"], dz2b, ta=True, ga=True, go=True, out_dtype=WIRE, name="mm_down_dw")
        dg3, du3 = _ffn_down_dx(dz2b, wdn[l], sv["g3"], sv["u3"])
        gW["ffn_w_gate"][l] = _mm(dg3, sv["x1b"], ta=True, ga=True, go=True, out_dtype=WIRE, name="mm_gate_dw")
        gW["ffn_w_up"][l] = _mm(du3, sv["x1b"], ta=True, ga=True, go=True, out_dtype=WIRE, name="mm_up_dw")
        dx1 = _ffn_up_dx(dg3, du3, wg[l], wu[l])
        token = grads_done(l, "ffn", {n: gW[n][l] for n in _FFN_WEIGHTS})
        ln1_g = small["ln1_g"][l][None]
        if token is not None:
            ln1_g = ln1_g + token[0, 0]
        dz1, dz1b, gS["ln1_g"][l], gS["ln1_b"][l] = _ln_bwd(dx1, dz2, sv["z1"], ln1_g, alpha, name="ln1_bwd")
        gW["w_out"][l] = _perm_gqa_rows(_mm(sv["cat"], dz1b, ta=True, out_dtype=WIRE, name="mm_out_dw")).reshape(4, -1, D)
        dcat = _mm(dz1b, wout[l], tb=True, name="mm_out_dx")
        dqg, dkg, dvg = _attn_bwd(sv["qg"], sv["kg"], sv["vg"], dcat, sv["oc"], sv["lsc"], split=False, npairs=2,
                                  kblk=lambda p: 0, vblk=lambda p: 0, doblk=lambda p: 6 + p, shared_kv=True,
                                  name="gqa_attn_bwd")
        dqm, dkm, dvm = _attn_bwd(sv["qm"], sv["km"], sv["kvp"], dcat, sv["oa"], sv["lsa"], split=True, npairs=3,
                                  kblk=lambda p: p, vblk=lambda p: 6 + p, doblk=lambda p: p, shared_kv=False,
                                  name="mla_attn_bwd")
        dqa, dkvp, dkr = _mla_prep_bwd(dqm, dkm, dvm, tm, mla_scale)
        gW["mla_w_uq"][l] = _rows_to_shards(_mm(dqa, sv["cq"], ta=True, out_dtype=WIRE, name="mm_uq_dw"), uq_idx, MLA_HEADS * 96)
        dcq = _mm(dqa, wuq[l], name="mm_uq_dx")
        gW["mla_w_ukv"][l] = _rows_to_shards(_mm(dkvp, sv["ckv"], ta=True, out_dtype=WIRE, name="mm_ukv_dw"), ukv_idx, MLA_HEADS * 128)
        dckv = _mm(dkvp, wukv[l], name="mm_ukv_dx")
        mixb = _dil_mix_bwd(dcat, sv["obs"], sv["lbs"], j384)
        ddq, ddk, ddv = [], [], []
        for b, (dil, L, tq, idx) in enumerate(branches):
            dq_b, dk_b, dv_b, dtab = _dil_bwd(*sv["qkv"][b], tabs[b], mixb[b], sv["lbs"][b], mixb[3 + b], dil=dil, L=L,
                                              tq=tq, name=f"dil_bwd_{b}")
            if dil == 1:
                dq_b, dk_b, dv_b = dq_b[0], dk_b[0, DIL_HALF:DIL_HALF + S], dv_b[0, DIL_HALF:DIL_HALF + S]
            ddq.append(dq_b)
            ddk.append(dk_b)
            ddv.append(dv_b)
            g_b = _bias_reduce(idx, dtab, name=f"bias_reduce_{b}")[:, :, 0].T
            g_rel = g_b if g_rel is None else g_rel + g_b
        dh, n1, n2, n3, n4 = _prep_bwd(sv["h"], dcq, dckv, dkr, ddq, ddk, ddv, dqg, dkg, dvg, sv["gq"], sv["gkv"],
                                       sv["ggq"], sv["ggk"], tg, j256)
        gS["mla_q_norm"][l], gS["mla_kv_norm"][l] = n1[0], n2[0]
        gS["gqa_q_norm"][l] = n3[0].reshape(4, 64).sum(0)
        gS["gqa_k_norm"][l] = n4[0].reshape(2, 64).sum(0)
        gW["w_in"][l] = _rows_to_shards(_mm(dh, sv["xb"], ta=True, out_dtype=WIRE, name="mm_in_dw"), in_idx, IN_W)
        dya = _mm(dh, win[l], name="mm_in_dx")
        dyb = dz1
        token = grads_done(l, "attn", {n: gW[n][l] for n in _ATTN_WEIGHTS})
    grad_x = _axpy(dya, dyb, alpha, name="grad_x")

    gsmall = {k: jnp.stack([a.reshape(-1) for a in v]) for k, v in gS.items()}
    gsmall["rel_bias"] = g_rel
    return loss, grad_x, gsmall


_ORDER = ("w_in", "mla_q_norm", "mla_kv_norm", "mla_w_uq", "mla_w_ukv", "gqa_q_norm", "gqa_k_norm", "rel_bias", "w_out",
          "ln1_g", "ln1_b", "ffn_w_gate", "ffn_w_up", "ffn_w_down", "ln2_g", "ln2_b")


def kernel(x, w_in, mla_q_norm, mla_kv_norm, mla_w_uq, mla_w_ukv, gqa_q_norm, gqa_k_norm, rel_bias, w_out, ln1_g, ln1_b, ffn_w_gate, ffn_w_up, ffn_w_down, ln2_g, ln2_b, loss_target, m_w_in, m_mla_q_norm, m_mla_kv_norm, m_mla_w_uq, m_mla_w_ukv, m_gqa_q_norm, m_gqa_k_norm, m_rel_bias, m_w_out, m_ln1_g, m_ln1_b, m_ffn_w_gate, m_ffn_w_up, m_ffn_w_down, m_ln2_g, m_ln2_b, v_w_in, v_mla_q_norm, v_mla_kv_norm, v_mla_w_uq, v_mla_w_ukv, v_gqa_q_norm, v_gqa_k_norm, v_rel_bias, v_w_out, v_ln1_g, v_ln1_b, v_ffn_w_gate, v_ffn_w_up, v_ffn_w_down, v_ln2_g, v_ln2_b):
    wts = dict(zip(_ORDER, (w_in, mla_q_norm, mla_kv_norm, mla_w_uq, mla_w_ukv, gqa_q_norm, gqa_k_norm, rel_bias, w_out,
                            ln1_g, ln1_b, ffn_w_gate, ffn_w_up, ffn_w_down, ln2_g, ln2_b)))
    mom = dict(zip(_ORDER, (m_w_in, m_mla_q_norm, m_mla_kv_norm, m_mla_w_uq, m_mla_w_ukv, m_gqa_q_norm, m_gqa_k_norm,
                            m_rel_bias, m_w_out, m_ln1_g, m_ln1_b, m_ffn_w_gate, m_ffn_w_up, m_ffn_w_down, m_ln2_g,
                            m_ln2_b)))
    var = dict(zip(_ORDER, (v_w_in, v_mla_q_norm, v_mla_kv_norm, v_mla_w_uq, v_mla_w_ukv, v_gqa_q_norm, v_gqa_k_norm,
                            v_rel_bias, v_w_out, v_ln1_g, v_ln1_b, v_ffn_w_gate, v_ffn_w_up, v_ffn_w_down, v_ln2_g,
                            v_ln2_b)))
    small_shapes = [wts[n].shape for n in _SMALL]
    for d in (wts, mom, var):
        for n in _COL_SHARDED:
            d[n] = d[n].transpose(0, 2, 1)

    depth = 2
    shards = {n: wts[n].astype(WIRE) for n in _BIG}
    flying = {}

    def start_gather(names, l, tag):
        own = tuple(_OWNER[n] for n in names)
        sh = [shards[n] for n in names]
        ssem, rsem, lands, token = _gather_start(sh, own, l, tag)
        return (names, own, sh, ssem, rsem, lands, l, tag), token

    def end_gather(flight, after):
        names, own, sh, ssem, rsem, lands, l, tag = flight
        got = _gather_finish(sh, _gather_wait(ssem, rsem, sh, lands, after, own, l, tag), own, l, tag)
        return {n: g.astype(CDT) for n, g in zip(names, got)}

    def weights_of_layer(l, part, after):
        if (l, part) == (0, "attn"):
            got = end_gather(start_gather(_ATTN_WEIGHTS, 0, "attn0")[0], after)
            flying["ffn0"], t0 = start_gather(_FFN_WEIGHTS, 0, "ffn0")
            flying["layer1"], t1 = start_gather(_BIG, 1, "layer1")
            return got, t0 + t1
        if (l, part) == (0, "ffn"):
            return end_gather(flying.pop("ffn0"), after), None
        if part == "attn":
            flying["w1"] = end_gather(flying.pop("layer1"), after)
        return flying["w1"], None

    def grads_done(l, part, grads):
        names = tuple(grads)
        own = tuple(_OWNER[n] for n in names)
        tag = f"{part}{l}"
        gl = [grads[n] for n in names]
        theirs = _rs_to_owner(gl, own, tag)
        pairs = [_sum_pair(g, t, o, name=f"rs_pair_sum_{n}") for n, g, t, o in zip(names, gl, theirs, own)]
        ssem, rsem, lands, token = _a2a_start(pairs, own, tag)
        flying[tag] = (names, own, ssem, rsem, pairs, lands)
        return token

    small = {n: wts[n] for n in _SMALL}
    loss, grad_x, gsmall = _local_step(x[0], loss_target[0], small, depth, weights_of_layer, grads_done)

    reds = {}
    for l in reversed(range(depth)):
        for part in ("ffn", "attn"):
            tag = f"{part}{l}"
            names, own, ssem, rsem, pairs, lands = flying.pop(tag)
            got = _a2a_wait(ssem, rsem, pairs, lands, grad_x, own, tag)
            for n, p, t, o in zip(names, pairs, got, own):
                reds[n, l] = _sum_chips(p, t, o, name=f"rs_sum_chips_{n}")
    order = [(n, l) for l in range(depth) for n in _BIG]
    sibs = dict(zip(order, _rs_from_owner([reds[k] for k in order], tuple(_OWNER[n] for n, _ in order))))

    sflat = _pack_flat([gsmall[n].reshape(-1) for n in _SMALL], 8 * LANES)
    rs = sflat.shape[0] // LANES
    sall = _gather_small(sflat.reshape(rs, LANES))

    def packed(d):
        return _pack_flat([d[n] for n in _SMALL], 8 * LANES).reshape(rs, LANES)

    outs = {tag: {} for tag in ("grad", "delta", "new_m", "new_v")}
    for n in _BIG:
        res = _adamw(wts[n], [reds[n, 0], reds[n, 1]], [sibs[n, 0], sibs[n, 1]], mom[n], var[n], _OWNER[n],
                     name=f"adamw_{n}")
        for tag, r in zip(("grad", "delta", "new_m", "new_v"), res):
            outs[tag][n] = r.transpose(0, 2, 1) if n in _COL_SHARDED else r
    for tag, smallflat in zip(("grad", "delta", "new_m", "new_v"), _adamw_small(packed(wts), sall, packed(mom), packed(var))):
        outs[tag].update(zip(_SMALL, _unpack_flat(smallflat.reshape(-1), small_shapes)))

    total = lax.psum(loss[0, 0], ("x", "y", "c"))
    return (total, grad_x[None], *[outs["grad"][n] for n in _ORDER], *[outs["delta"][n] for n in _ORDER],
            *[outs["new_m"][n] for n in _ORDER], *[outs["new_v"][n] for n in _ORDER])
```

```python
import functools
import math

import numpy as np
import jax
import jax.numpy as jnp
from jax import lax
from jax.experimental import pallas as pl
from jax.experimental.pallas import tpu as pltpu

F32 = jnp.float32
CDT = jnp.bfloat16
WIRE = jnp.bfloat16

HEAD_DIM = 64
GRID_W = 64
ROPE_THETA = 10000.0
MLA_HEADS = 6
MLA_Q_RANK = 256
MLA_KV_RANK = 128
MLA_ROPE_DIM = 32
DIL_HEADS = 6
DIL_BRANCHES = ((128, 1), (512, 4), (2048, 16))
DIL_HALF = 64
GQA_Q_HEADS = 4
REL_BUCKETS = 32
REL_MAX_DIST = 1024
NEG_INF = -1e30
LANES = 128
VMEM_LIMIT = 56 * 1024 * 1024

ADAM_LR, ADAM_B1, ADAM_B2, ADAM_EPS, ADAM_WD, ADAM_STEP = 0.001, 0.9, 0.999, 1e-08, 0.01, 10

C_CQ, C_CKV, C_KR, C_DQ, C_DK, C_DV, C_GQ, C_GK, C_GV, IN_P = 0, 256, 384, 512, 896, 1280, 1664, 1920, 2048, 2176
IN_W = 2080
MESH_ID = pl.DeviceIdType.MESH


def _cparams(n_axes, vmem=VMEM_LIMIT):
    return pltpu.CompilerParams(dimension_semantics=("arbitrary",) * n_axes, vmem_limit_bytes=vmem)


MAX_WHOLE_DIM = 2304
WHOLE_K_WINDOW_BYTES = 36 * 1024 * 1024


def _pick(n, target):
    best = None
    for t in range(LANES, min(n, target) + 1, LANES):
        if n % t == 0:
            best = t
    if best is not None and (2 * best >= target or n > MAX_WHOLE_DIM):
        return best
    return n


def _sds(shape, dtype):
    return jax.ShapeDtypeStruct(tuple(shape), dtype)


def _in_cols():
    idx = -np.ones((IN_P,), np.int64)
    idx[C_CQ:C_CQ + 256] = np.arange(0, 256)
    idx[C_CKV:C_CKV + 128] = np.arange(256, 384)
    idx[C_KR + 64:C_KR + 96] = np.arange(384, 416)
    idx[C_DQ:C_DQ + 1152] = np.arange(416, 1568)
    gq = 1568 + (np.array([0, 2, 1, 3])[:, None] * 64 + np.arange(64)[None, :]).reshape(-1)
    idx[C_GQ:C_GQ + 256] = gq
    idx[C_GK:C_GK + 256] = np.arange(1824, 2080)
    return idx


def _uq_cols():
    idx = -np.ones((MLA_HEADS * 128,), np.int64)
    for h in range(MLA_HEADS):
        idx[h * 128:h * 128 + 96] = np.arange(96 * h, 96 * h + 96)
    return idx


def _ukv_cols():
    idx = -np.ones((MLA_HEADS * 128 + MLA_HEADS * 64,), np.int64)
    for h in range(MLA_HEADS):
        idx[h * 128:h * 128 + 64] = np.arange(128 * h, 128 * h + 64)
        idx[768 + h * 64:768 + h * 64 + 64] = np.arange(128 * h + 64, 128 * h + 128)
    return idx


def _runs(idx):
    out, i = [], 0
    while i < len(idx):
        j = i + 1
        while j < len(idx) and ((idx[i] < 0 and idx[j] < 0) or (idx[i] >= 0 and idx[j] == idx[j - 1] + 1)):
            j += 1
        out.append((int(idx[i]), j - i))
        i = j
    return out


def _rows_from_shards(sh, idx):
    _, cs, r = sh.shape
    pieces = []
    for first, ln in _runs(idx):
        if first < 0:
            pieces.append(jnp.zeros((ln, r), sh.dtype))
            continue
        while ln > 0:
            k, off = divmod(first, cs)
            take = min(ln, cs - off)
            pieces.append(sh[k, off:off + take, :])
            first, ln = first + take, ln - take
    return jnp.concatenate(pieces, axis=0)


def _rows_to_shards(wp, idx, n):
    inv = np.zeros((n,), np.int64)
    pos = np.nonzero(idx >= 0)[0]
    inv[idx[pos]] = pos
    cs = n // 4
    shards = []
    for k in range(4):
        pieces = [wp[first:first + ln, :] for first, ln in _runs(inv[k * cs:(k + 1) * cs])]
        shards.append(jnp.concatenate(pieces, axis=0))
    return jnp.stack(shards)


def _t5_bucket_np(rel):
    nb = REL_BUCKETS // 2
    exact = nb // 2
    ret = np.where(rel > 0, nb, 0)
    n = np.abs(rel)
    nf = np.maximum(n, 1).astype(np.float32)
    large = exact + (np.log(nf / np.float32(exact)) / np.float32(math.log(REL_MAX_DIST / exact))
                     * np.float32(nb - exact)).astype(np.int32)
    large = np.minimum(large, nb - 1)
    return ret + np.where(n < exact, n, large)


def _branch_bucket_idx(tq, dil):
    kw = tq + 2 * DIL_HALF
    rel = np.arange(kw)[None, :] - DIL_HALF - np.arange(tq)[:, None]
    idx = _t5_bucket_np(rel * dil)
    return np.where(np.abs(rel) <= DIL_HALF, idx, -1).astype(np.int32)


def _rope_tables(S):
    inv = ROPE_THETA ** (-jnp.arange(0, 32, 2, dtype=F32) / 32)
    t = jnp.arange(S)
    pos = t.astype(F32)
    row = (t // GRID_W).astype(F32)
    col = (t % GRID_W).astype(F32)
    lane = np.arange(LANES)
    wm = lane - 64
    is_rope = (lane >= 64) & (lane < 96)
    ang = pos[:, None] * inv[np.where(is_rope, wm % 16, 0)][None, :]
    cm = jnp.where(is_rope[None], jnp.cos(ang), 1.0)
    smm = jnp.where((is_rope & (wm < 16))[None], -jnp.sin(ang), 0.0)
    spm = jnp.where((is_rope & (wm >= 16))[None], jnp.sin(ang), 0.0)
    g = lane % 64
    w = g % 32
    angg = jnp.where((g < 32)[None], row[:, None], col[:, None]) * inv[w % 16][None, :]
    cg = jnp.cos(angg)
    smg = jnp.where((w < 16)[None], -jnp.sin(angg), 0.0)
    spg = jnp.where((w >= 16)[None], jnp.sin(angg), 0.0)
    return (cm, smm, spm), (cg, smg, spg)


def _lanes(t, width):
    return t if width == LANES else jnp.concatenate([t] * (width // LANES), axis=1)


def _rope(x, tabs):
    c, sm, sp = (_lanes(t, x.shape[1]) for t in tabs)
    w = x.shape[1]
    return x * c + pltpu.roll(x, w - 16, 1) * sm + pltpu.roll(x, 16, 1) * sp


def _rope_t(dy, tabs):
    c, sm, sp = (_lanes(t, dy.shape[1]) for t in tabs)
    w = dy.shape[1]
    return dy * c + pltpu.roll(dy * sm, 16, 1) + pltpu.roll(dy * sp, w - 16, 1)


def _head_ones(width):
    i = np.arange(width)
    return jnp.asarray((i[:, None] // HEAD_DIM == i[None, :] // HEAD_DIM).astype(np.float32))


def _headsum(x, j):
    return jnp.dot(x, j, preferred_element_type=F32, precision=lax.Precision.HIGHEST)


def _mm(a, b, *, ta=False, tb=False, ga=False, gb=False, go=False, out_dtype=F32, name):
    G = a.shape[0] if ga else (b.shape[0] if gb else 1)
    a2 = a.shape[1:] if ga else a.shape
    b2 = b.shape[1:] if gb else b.shape
    K, M = a2 if ta else a2[::-1]
    N = b2[0] if tb else b2[1]
    assert (b2[1] if tb else b2[0]) == K
    tm, tn, tk = _pick(M, 1024), _pick(N, 1024), _pick(K, 2048)
    if tm * tn > 1024 * 1152:
        tm = _pick(M, 512)
    if 2 * K * (tm + tn) * jnp.dtype(CDT).itemsize <= WHOLE_K_WINDOW_BYTES:
        tk = K
    nk = K // tk
    steps = nk if (go or G == 1) else G * nk
    dn = (((0 if ta else 1,), (1 if tb else 0,)), ((), ()))

    def body(a_ref, b_ref, o_ref, *acc):
        part = lax.dot_general(a_ref[...], b_ref[...], dn, preferred_element_type=F32)
        if steps == 1:
            o_ref[...] = part.astype(o_ref.dtype)
            return
        acc_ref, = acc
        s = pl.program_id(3)

        @pl.when(s == 0)
        def _():
            acc_ref[...] = part

        @pl.when(s > 0)
        def _():
            acc_ref[...] += part

        @pl.when(s == steps - 1)
        def _():
            o_ref[...] = acc_ref[...].astype(o_ref.dtype)

    def grp(g, s):
        return g if go else s // nk

    def kk(s):
        return s if steps == nk else s % nk

    def spec(grouped, block, index):
        if grouped:
            return pl.BlockSpec((None,) + block, lambda g, i, j, s: (grp(g, s),) + index(i, j, s))
        return pl.BlockSpec(block, lambda g, i, j, s: index(i, j, s))

    a_spec = (spec(ga, (tk, tm), lambda i, j, s: (kk(s), i)) if ta else spec(ga, (tm, tk), lambda i, j, s: (i, kk(s))))
    b_spec = (spec(gb, (tn, tk), lambda i, j, s: (j, kk(s))) if tb else spec(gb, (tk, tn), lambda i, j, s: (kk(s), j)))
    o_spec = spec(go, (tm, tn), lambda i, j, s: (i, j))
    return pl.pallas_call(
        body, name=name, grid=(G if go else 1, M // tm, N // tn, steps),
        in_specs=[a_spec, b_spec], out_specs=o_spec,
        out_shape=_sds(((G,) if go else ()) + (M, N), out_dtype),
        scratch_shapes=[pltpu.VMEM((tm, tn), F32)] if steps > 1 else [],
        compiler_params=_cparams(4),
    )(a, b)


def _row(ts, w, cb=0):
    return pl.BlockSpec((ts, w), lambda i: (i, cb))


def _full(shape):
    nd = len(shape)
    return pl.BlockSpec(tuple(shape), lambda i: (0,) * nd)


def _rms_fwd(x, g, eps=1e-6):
    r = lax.rsqrt(jnp.mean(x * x, axis=-1, keepdims=True) + eps)
    return x * r * g


def _rms_bwd(x, g, dy, eps=1e-6):
    r = lax.rsqrt(jnp.mean(x * x, axis=-1, keepdims=True) + eps)
    gdy = g * dy
    dx = r * gdy - x * (r * r * r) * jnp.mean(x * gdy, axis=-1, keepdims=True)
    return dx, x * r * dy


def _rms_head_fwd(x, g, j, eps=1e-6):
    r = lax.rsqrt(_headsum(x * x, j) * (1.0 / HEAD_DIM) + eps)
    return x * r * g


def _rms_head_bwd(x, g, dy, j, eps=1e-6):
    r = lax.rsqrt(_headsum(x * x, j) * (1.0 / HEAD_DIM) + eps)
    gdy = g * dy
    dx = r * gdy - x * (r * r * r) * (_headsum(x * gdy, j) * (1.0 / HEAD_DIM))
    return dx, x * r * dy


DIL_STRIDES = tuple(d for _, d in DIL_BRANCHES if d > 1)
DIL_W = DIL_HEADS * HEAD_DIM


def _res_spec(d, n, pad_blocks=0):
    return pl.BlockSpec((d, n, DIL_W), lambda i: (0, i + pad_blocks, 0))


def _prep_fwd(h, gq, gkv, ggq, ggk, tm, tg, j256):
    S = h.shape[0]
    ts = min(256, S)
    scale = HEAD_DIM ** -0.5
    nres = len(DIL_STRIDES)

    def body(*refs):
        (h_ref, gq_ref, gkv_ref, ggq_ref, ggk_ref, cm, smm, spm, cg, smg, spg, j_ref), refs = refs[:12], refs[12:]
        refs = refs[2 * nres:]
        (cq_o, ckv_o, kr_o, dq_o, dk_o, dv_o, gq_o, gk_o, gv_o), res_o = refs[:9], refs[9:-1]
        st = refs[-1]
        tabm = (cm[...], smm[...], spm[...])
        tabg = (cg[...], smg[...], spg[...])
        cq_o[...] = _rms_fwd(h_ref[:, C_CQ:C_CQ + 256], gq_ref[...]).astype(CDT)
        ckv_o[...] = _rms_fwd(h_ref[:, C_CKV:C_CKV + 128], gkv_ref[...]).astype(CDT)
        kr_o[...] = _rope(h_ref[:, C_KR:C_KR + 128], tabm).astype(CDT)
        dq_o[...] = (h_ref[:, C_DQ:C_DQ + 384] * scale).astype(CDT)
        dk_o[...] = h_ref[:, C_DK:C_DK + 384].astype(CDT)
        dv_o[...] = h_ref[:, C_DV:C_DV + 384].astype(CDT)
        for j, lanes in _lane_blocks(3 * DIL_W):
            st[j] = h_ref[:, C_DQ + lanes.start:C_DQ + lanes.stop] * (scale if j < 3 else 1.0)
        for bi, d in enumerate(DIL_STRIDES):
            for c in range(d):
                rows = pl.ds(c, ts // d, stride=d)
                for j, lanes in _lane_blocks(3 * DIL_W):
                    res_o[3 * bi + j // 3][c, :, (j % 3) * LANES:(j % 3 + 1) * LANES] = st.at[j][rows, :].astype(CDT)
        qn = _rms_head_fwd(h_ref[:, C_GQ:C_GQ + 256], ggq_ref[...], j_ref[...])
        gq_o[...] = (_rope(qn, tabg) * scale).astype(CDT)
        kn = _rms_head_fwd(h_ref[:, C_GK:C_GK + 128], ggk_ref[...], j_ref[0:128, 0:128])
        gk_o[...] = _rope(kn, tabg).astype(CDT)
        gv_o[...] = h_ref[:, C_GV:C_GV + 128].astype(CDT)

    widths = (256, 128, 128, 384, 384, 384, 256, 128, 128)
    out_specs = [_row(ts, w) for w in widths]
    out_shape = [_sds((S, w), CDT) for w in widths]
    zeros, aliases = [], {}
    for d in DIL_STRIDES:
        n, L = ts // d, S // d
        out_specs += [_res_spec(d, n), _res_spec(d, n, DIL_HALF // n), _res_spec(d, n, DIL_HALF // n)]
        out_shape += [_sds((d, L, DIL_W), CDT)] + [_sds((d, L + 2 * DIL_HALF, DIL_W), CDT)] * 2
        for t in range(2):
            aliases[12 + len(zeros)] = len(out_shape) - 2 + t
            zeros.append(jnp.zeros((d, L + 2 * DIL_HALF, DIL_W), CDT))
    return pl.pallas_call(
        body, name="prep_fwd", grid=(S // ts,),
        in_specs=[_row(ts, IN_P), _full(gq.shape), _full(gkv.shape), _full(ggq.shape), _full(ggk.shape)]
        + [_row(ts, LANES)] * 6 + [_full(j256.shape)] + [pl.BlockSpec(memory_space=pl.ANY)] * len(zeros),
        out_specs=out_specs, out_shape=out_shape, input_output_aliases=aliases,
        scratch_shapes=[pltpu.VMEM((3 * DIL_W // LANES, ts, LANES), F32)],
        compiler_params=_cparams(1),
    )(h, gq, gkv, ggq, ggk, *tm, *tg, j256, *zeros)


def _prep_bwd(h, dcq, dckv, dkr, ddq, ddk, ddv, dgq, dgk, dgv, gq, gkv, ggq, ggk, tg, j256):
    S = h.shape[0]
    ts = min(256, S)
    scale = HEAD_DIM ** -0.5

    def body(h_ref, dcq_r, dckv_r, dkr_r, q1, q2, q3, k1, k2, k3, v1, v2, v3, dgq_r, dgk_r, dgv_r,
             gq_ref, gkv_ref, ggq_ref, ggk_ref, cg, smg, spg, j_ref,
             dh_o, ngq_o, ngkv_o, nggq_o, nggk_o, *scr):
        tabg = (cg[...], smg[...], spg[...])
        first = pl.program_id(0) == 0
        scr, = scr
        d2, d3 = DIL_STRIDES
        dq = q1[...] + _by_token(q2, scr, d2) + _by_token(q3, scr, d3)
        dk = k1[...] + _by_token(k2, scr, d2) + _by_token(k3, scr, d3)
        dv = v1[...] + _by_token(v2, scr, d2) + _by_token(v3, scr, d3)

        def acc(o_ref, val):
            s = jnp.sum(val, axis=0, keepdims=True)

            @pl.when(first)
            def _():
                o_ref[...] = s

            @pl.when(jnp.logical_not(first))
            def _():
                o_ref[...] += s

        dx, dg = _rms_bwd(h_ref[:, C_CQ:C_CQ + 256], gq_ref[...], dcq_r[...])
        dh_o[:, C_CQ:C_CQ + 256] = dx.astype(CDT)
        acc(ngq_o, dg)
        dx, dg = _rms_bwd(h_ref[:, C_CKV:C_CKV + 128], gkv_ref[...], dckv_r[...])
        dh_o[:, C_CKV:C_CKV + 128] = dx.astype(CDT)
        acc(ngkv_o, dg)
        dh_o[:, C_KR:C_KR + 128] = dkr_r[...].astype(CDT)
        dh_o[:, C_DQ:C_DQ + 384] = (dq * scale).astype(CDT)
        dh_o[:, C_DK:C_DK + 384] = dk.astype(CDT)
        dh_o[:, C_DV:C_DV + 384] = dv.astype(CDT)
        dqn = _rope_t(dgq_r[...] * scale, tabg)
        dx, dg = _rms_head_bwd(h_ref[:, C_GQ:C_GQ + 256], ggq_ref[...], dqn, j_ref[...])
        dh_o[:, C_GQ:C_GQ + 256] = dx.astype(CDT)
        acc(nggq_o, dg)
        dkn = _rope_t(dgk_r[...], tabg)
        dx, dg = _rms_head_bwd(h_ref[:, C_GK:C_GK + 128], ggk_ref[...], dkn, j_ref[0:128, 0:128])
        dh_o[:, C_GK:C_GK + 128] = dx.astype(CDT)
        acc(nggk_o, dg)
        dh_o[:, C_GV:C_GV + 128] = dgv_r[...].astype(CDT)

    d2, d3 = DIL_STRIDES
    n2, n3 = ts // d2, ts // d3
    tok = _row(ts, DIL_W)
    return pl.pallas_call(
        body, name="prep_bwd", grid=(S // ts,),
        in_specs=[_row(ts, IN_P), _row(ts, 256), _row(ts, 128), _row(ts, 128)]
        + [tok, _res_spec(d2, n2), _res_spec(d3, n3)]
        + [tok, _res_spec(d2, n2, DIL_HALF // n2), _res_spec(d3, n3, DIL_HALF // n3)] * 2
        + [_row(ts, 256), _row(ts, 128), _row(ts, 128)]
        + [_full(gq.shape), _full(gkv.shape), _full(ggq.shape), _full(ggk.shape)] + [_row(ts, LANES)] * 3
        + [_full(j256.shape)],
        out_specs=[_row(ts, IN_P), _full((1, 256)), _full((1, 128)), _full((1, 256)), _full((1, 128))],
        out_shape=[_sds((S, IN_P), CDT), _sds((1, 256), F32), _sds((1, 128), F32), _sds((1, 256), F32),
                   _sds((1, 128), F32)],
        scratch_shapes=[_TOKEN_SCRATCH(ts)],
        compiler_params=_cparams(1),
    )(h, dcq, dckv, dkr, *ddq, *ddk, *ddv, dgq, dgk, dgv, gq, gkv, ggq, ggk, *tg, j256)


def _lane_blocks(width):
    return [(j, slice(j * LANES, (j + 1) * LANES)) for j in range(width // LANES)]


_TOKEN_SCRATCH = lambda ts: pltpu.VMEM((DIL_W // LANES, ts, LANES), F32)


def _by_token(res_ref, scr_ref, d):
    n = res_ref.shape[1]
    if d == 1:
        return res_ref[0].astype(F32)
    for c in range(d):
        for j, lanes in _lane_blocks(res_ref.shape[2]):
            scr_ref.at[j][pl.ds(c, n, stride=d), :] = res_ref[c, :, lanes].astype(F32)
    return jnp.concatenate([scr_ref[j] for j, _ in _lane_blocks(res_ref.shape[2])], axis=1)


def _by_residue(val, scr_ref, out_ref, d):
    n = out_ref.shape[1]
    if d == 1:
        out_ref[0] = val.astype(out_ref.dtype)
        return
    for j, lanes in _lane_blocks(out_ref.shape[2]):
        scr_ref[j] = val[:, lanes]
    for c in range(d):
        for j, lanes in _lane_blocks(out_ref.shape[2]):
            out_ref[c, :, lanes] = scr_ref.at[j][pl.ds(c, n, stride=d), :].astype(out_ref.dtype)


def _mla_prep_fwd(qa, kvp, kr, tm, scale):
    S = qa.shape[0]
    ts = min(512, S)

    def body(qa_ref, kv_ref, kr_ref, cm, smm, spm, q_o, k_o):
        tabm = (cm[...], smm[...], spm[...])
        q_o[...] = (_rope(qa_ref[...], tabm) * scale).astype(CDT)
        k_o[...] = kv_ref[:, 0:768] + _lanes(kr_ref[...], 768)

    return pl.pallas_call(
        body, name="mla_prep_fwd", grid=(S // ts,),
        in_specs=[_row(ts, 768), _row(ts, 1152), _row(ts, 128)] + [_row(ts, LANES)] * 3,
        out_specs=[_row(ts, 768)] * 2, out_shape=[_sds((S, 768), CDT)] * 2,
        compiler_params=_cparams(1),
    )(qa, kvp, kr, *tm)


def _mla_prep_bwd(dq, dk, dv, tm, scale):
    S = dq.shape[0]
    ts = min(512, S)

    def body(dq_ref, dk_ref, dv_ref, cm, smm, spm, dqa_o, dkv_o, dkr_o):
        tabm = (cm[...], smm[...], spm[...])
        lane = lax.broadcasted_iota(jnp.int32, (1, LANES), 1)
        dqa_o[...] = _rope_t(dq_ref[...] * scale, tabm).astype(CDT)
        dkr = jnp.zeros((ts, LANES), F32)
        for hd in range(MLA_HEADS):
            blk = dk_ref[:, hd * 128:(hd + 1) * 128]
            dkv_o[:, hd * 128:(hd + 1) * 128] = jnp.where(lane < 64, blk, 0.0).astype(CDT)
            dkr = dkr + jnp.where((lane >= 64) & (lane < 96), blk, 0.0)
        dkv_o[:, 768:1152] = dv_ref[...].astype(CDT)
        dkr_o[...] = jnp.where((lane >= 64) & (lane < 96), _rope_t(dkr, tabm), 0.0)

    return pl.pallas_call(
        body, name="mla_prep_bwd", grid=(S // ts,),
        in_specs=[_row(ts, 768), _row(ts, 768), _row(ts, 384)] + [_row(ts, LANES)] * 3,
        out_specs=[_row(ts, 768), _row(ts, 1152), _row(ts, 128)],
        out_shape=[_sds((S, 768), CDT), _sds((S, 1152), CDT), _sds((S, 128), F32)],
        compiler_params=_cparams(1),
    )(dq, dk, dv, *tm)


def _ln_fwd(xa, xb, g, b, alpha, name):
    S, D = xa.shape
    ts = min(512, S)

    def body(xa_ref, xb_ref, g_ref, b_ref, y_o, yb_o, z_o):
        z = alpha * xa_ref[...] + xb_ref[...]
        mu = jnp.mean(z, axis=-1, keepdims=True)
        zc = z - mu
        var = jnp.mean(zc * zc, axis=-1, keepdims=True)
        y = zc * lax.rsqrt(var + 1e-5) * g_ref[...] + b_ref[...]
        y_o[...] = y
        yb_o[...] = y.astype(CDT)
        z_o[...] = z

    return pl.pallas_call(
        body, name=name, grid=(S // ts,),
        in_specs=[_row(ts, D), _row(ts, D), _full(g.shape), _full(b.shape)],
        out_specs=[_row(ts, D)] * 3, out_shape=[_sds((S, D), F32), _sds((S, D), CDT), _sds((S, D), F32)],
        compiler_params=_cparams(1),
    )(xa, xb, g, b)


def _ln_bwd(dya, dyb, z, g, alpha, name, loss_head=False):
    S, D = z.shape
    ts = min(512, S)

    def body(dya_ref, dyb_ref, z_ref, g_ref, dz_o, dzb_o, dg_o, db_o, *loss_o):
        first = pl.program_id(0) == 0
        if loss_head:
            err = dya_ref[...] - dyb_ref[...]
            dy = err * (1.0 / D)
            part = jnp.sum(jnp.sum(err * err, axis=1, keepdims=True), axis=0, keepdims=True) * (0.5 / D)

            @pl.when(first)
            def _():
                loss_o[0][...] = part

            @pl.when(jnp.logical_not(first))
            def _():
                loss_o[0][...] += part
        else:
            dy = dya_ref[...] + alpha * dyb_ref[...]
        z = z_ref[...]
        mu = jnp.mean(z, axis=-1, keepdims=True)
        zc = z - mu
        r = lax.rsqrt(jnp.mean(zc * zc, axis=-1, keepdims=True) + 1e-5)
        xh = zc * r
        dxh = dy * g_ref[...]
        dz = r * (dxh - jnp.mean(dxh, axis=-1, keepdims=True) - xh * jnp.mean(dxh * xh, axis=-1, keepdims=True))
        dz_o[...] = dz
        dzb_o[...] = dz.astype(CDT)
        sg = jnp.sum(dy * xh, axis=0, keepdims=True)
        sb = jnp.sum(dy, axis=0, keepdims=True)

        @pl.when(first)
        def _():
            dg_o[...] = sg
            db_o[...] = sb

        @pl.when(jnp.logical_not(first))
        def _():
            dg_o[...] += sg
            db_o[...] += sb

    extra = ([_full((1, 1))], [_sds((1, 1), F32)]) if loss_head else ([], [])
    return pl.pallas_call(
        body, name=name, grid=(S // ts,),
        in_specs=[_row(ts, D)] * 3 + [_full(g.shape)],
        out_specs=[_row(ts, D), _row(ts, D), _full((1, D)), _full((1, D))] + extra[0],
        out_shape=[_sds((S, D), F32), _sds((S, D), CDT), _sds((1, D), F32), _sds((1, D), F32)] + extra[1],
        compiler_params=_cparams(1),
    )(dya, dyb, z, g)


def _grp_spec(ts, w):
    return pl.BlockSpec((None, ts, w), lambda k, i: (k, i, 0))


def _ffn_up(xb, wg3, wu3):
    S, D = xb.shape
    G, Fc, _ = wg3.shape
    tm = _pick(S, 1024)
    wspec = pl.BlockSpec((None, Fc, D), lambda k, i: (k, 0, 0))

    def body(x_ref, wg_ref, wu_ref, g_o, u_o, a_o):
        x = x_ref[...]
        g = lax.dot_general(x, wg_ref[...], _NT, preferred_element_type=F32)
        u = lax.dot_general(x, wu_ref[...], _NT, preferred_element_type=F32)
        g_o[...] = g.astype(CDT)
        u_o[...] = u.astype(CDT)
        a_o[...] = (g / (1.0 + jnp.exp(-g)) * u).astype(CDT)

    return pl.pallas_call(
        body, name="ffn_up", grid=(G, S // tm),
        in_specs=[pl.BlockSpec((tm, D), lambda k, i: (i, 0)), wspec, wspec], out_specs=[_grp_spec(tm, Fc)] * 3,
        out_shape=[_sds((G, S, Fc), CDT)] * 3, compiler_params=_cparams(2),
    )(xb, wg3, wu3)


def _ffn_up_dx(dg3, du3, wg3, wu3):
    G, S, Fc = dg3.shape
    D = wg3.shape[2]
    tm = _pick(S, 512)
    wspec = pl.BlockSpec((G, Fc, D), lambda i: (0, 0, 0))
    aspec = pl.BlockSpec((G, tm, Fc), lambda i: (0, i, 0))

    def body(dg_ref, du_ref, wg_ref, wu_ref, o_ref):
        acc = None
        for k in range(G):
            for a_ref, w_ref in ((dg_ref, wg_ref), (du_ref, wu_ref)):
                part = jnp.dot(a_ref[k], w_ref[k], preferred_element_type=F32)
                acc = part if acc is None else acc + part
        o_ref[...] = acc

    return pl.pallas_call(
        body, name="ffn_up_dx", grid=(S // tm,), in_specs=[aspec, aspec, wspec, wspec],
        out_specs=pl.BlockSpec((tm, D), lambda i: (i, 0)), out_shape=_sds((S, D), F32), compiler_params=_cparams(1),
    )(dg3, du3, wg3, wu3)


def _ffn_down(act3, wd3):
    G, S, Fc = act3.shape
    D = wd3.shape[2]
    tm = _pick(S, 1024)

    def body(a_ref, w_ref, o_ref):
        acc = jnp.dot(a_ref[0], w_ref[0], preferred_element_type=F32)
        for k in range(1, G):
            acc = acc + jnp.dot(a_ref[k], w_ref[k], preferred_element_type=F32)
        o_ref[...] = acc

    return pl.pallas_call(
        body, name="ffn_down", grid=(S // tm,),
        in_specs=[pl.BlockSpec((G, tm, Fc), lambda i: (0, i, 0)), pl.BlockSpec((G, Fc, D), lambda i: (0, 0, 0))],
        out_specs=pl.BlockSpec((tm, D), lambda i: (i, 0)), out_shape=_sds((S, D), F32), compiler_params=_cparams(1),
    )(act3, wd3)


def _ffn_down_dx(dzb, wd3, g3, u3):
    S, D = dzb.shape
    G, Fc, _ = wd3.shape
    tm = _pick(S, 1024)

    def body(dz_ref, wd_ref, g_ref, u_ref, dg_o, du_o):
        da = lax.dot_general(dz_ref[...], wd_ref[...], _NT, preferred_element_type=F32)
        g = g_ref[...].astype(F32)
        sg = 1.0 / (1.0 + jnp.exp(-g))
        dg_o[...] = (da * u_ref[...].astype(F32) * (sg * (1.0 + g * (1.0 - sg)))).astype(CDT)
        du_o[...] = (da * (g * sg)).astype(CDT)

    return pl.pallas_call(
        body, name="ffn_down_dx", grid=(G, S // tm),
        in_specs=[pl.BlockSpec((tm, D), lambda k, i: (i, 0)), pl.BlockSpec((None, Fc, D), lambda k, i: (k, 0, 0)),
                  _grp_spec(tm, Fc), _grp_spec(tm, Fc)],
        out_specs=[_grp_spec(tm, Fc)] * 2, out_shape=[_sds((G, S, Fc), CDT)] * 2, compiler_params=_cparams(2),
    )(dzb, wd3, g3, u3)


def _axpy(a, b, alpha, name):
    S, D = a.shape
    ts = min(512, S)

    def body(a_ref, b_ref, o_ref):
        o_ref[...] = a_ref[...] + alpha * b_ref[...]

    return pl.pallas_call(
        body, name=name, grid=(S // ts,), in_specs=[_row(ts, D)] * 2, out_specs=_row(ts, D),
        out_shape=_sds((S, D), F32), compiler_params=_cparams(1),
    )(a, b)


def _pair_masks():
    lane = lax.broadcasted_iota(jnp.int32, (1, LANES), 1)
    first = lane < HEAD_DIM
    return first, jnp.logical_not(first)


def _head_scalar(x, m):
    return jnp.max(jnp.where(m, x, -jnp.inf), axis=-1, keepdims=True)


_NT = (((1,), (1,)), ((), ()))
_TN = (((0,), (0,)), ((), ()))
ATTN_KEY_CHUNK = 1024
ATTN_FWD_TILES_PER_STEP = 4
ATTN_BWD_TILES_PER_STEP = 2


def _attn_fwd(q, k, v, *, split, npairs, kblk, vblk, name):
    S = q.shape[0]
    qw = 256 if split else LANES
    tq = min(256, S)
    nsub = ATTN_FWD_TILES_PER_STEP if S % (ATTN_FWD_TILES_PER_STEP * tq) == 0 else 1

    def body(q_ref, k_ref, v_ref, o_ref, lse_ref):
        masks = _pair_masks()
        for j in range(nsub):
            rows = slice(j * tq, (j + 1) * tq)
            outs, lses = [], []
            for hd in range(2):
                if split:
                    qh = q_ref[rows, hd * LANES:(hd + 1) * LANES]
                    kh = k_ref[:, hd * LANES:(hd + 1) * LANES]
                else:
                    qh = jnp.where(masks[hd], q_ref[rows, :], jnp.zeros_like(q_ref[rows, :]))
                    kh = k_ref[...]
                s = lax.dot_general(qh, kh, _NT, preferred_element_type=F32)
                mx = jnp.max(s, axis=-1, keepdims=True)
                p = jnp.exp(s - mx)
                l = jnp.sum(p, axis=-1, keepdims=True)
                o = jnp.dot(p.astype(CDT), v_ref[...], preferred_element_type=F32)
                outs.append(o / l)
                lses.append(jnp.broadcast_to(mx + jnp.log(l), (tq, LANES)))
            o_ref[rows, :] = jnp.where(masks[0], outs[0], outs[1]).astype(o_ref.dtype)
            lse_ref[rows, :] = jnp.where(masks[0], lses[0], lses[1])

    return pl.pallas_call(
        body, name=name, grid=(npairs, S // (nsub * tq)),
        in_specs=[pl.BlockSpec((nsub * tq, qw), lambda p, i: (i, p)),
                  pl.BlockSpec((S, qw), lambda p, i: (0, kblk(p))),
                  pl.BlockSpec((S, LANES), lambda p, i: (0, vblk(p)))],
        out_specs=[pl.BlockSpec((nsub * tq, LANES), lambda p, i: (i, p))] * 2,
        out_shape=[_sds((S, LANES * npairs), CDT), _sds((S, LANES * npairs), F32)],
        compiler_params=_cparams(2),
    )(q, k, v)


def _attn_bwd(q, k, v, do, o, lse, *, split, npairs, kblk, vblk, doblk, shared_kv, name):
    S = q.shape[0]
    qw = 256 if split else LANES
    tq = min(512, S)
    tkv = min(ATTN_KEY_CHUNK, S)
    nsub = ATTN_BWD_TILES_PER_STEP if S % (ATTN_BWD_TILES_PER_STEP * tq) == 0 else 1
    nkv = 1 if shared_kv else npairs

    def body(q_ref, k_ref, v_ref, do_ref, o_ref, lse_ref, dq_ref, dk_ref, dv_ref):
        masks = _pair_masks()
        p_id, i_id = pl.program_id(0), pl.program_id(1)
        first = (i_id == 0) & ((p_id == 0) if shared_kv else True)

        @pl.when(first)
        def _():
            dk_ref[...] = jnp.zeros_like(dk_ref)
            dv_ref[...] = jnp.zeros_like(dv_ref)

        for j in range(nsub):
            qrows = slice(j * tq, (j + 1) * tq)
            do = do_ref[qrows, :]
            o = o_ref[qrows, :].astype(F32)
            lse = lse_ref[qrows, :]
            heads = []
            for hd in range(2):
                m = masks[hd]
                cols = slice(hd * LANES, (hd + 1) * LANES) if split else slice(None)
                qh = q_ref[qrows, cols] if split else jnp.where(m, q_ref[qrows, :], jnp.zeros_like(q_ref[qrows, :]))
                doh = jnp.where(m, do, 0.0)
                heads.append((m, cols, qh, doh.astype(CDT), _head_scalar(lse, m),
                              jnp.sum(doh * o, axis=-1, keepdims=True)))
            dqs = [jnp.zeros((tq, LANES), F32), jnp.zeros((tq, LANES), F32)]
            for ck in range(S // tkv):
                rows = slice(ck * tkv, (ck + 1) * tkv)
                v = v_ref[rows, :]
                dv = jnp.zeros((tkv, LANES), F32)
                for hd, (m, cols, qh, dohb, lse_h, delta) in enumerate(heads):
                    kh = k_ref[rows, cols]
                    s = lax.dot_general(qh, kh, _NT, preferred_element_type=F32)
                    p = jnp.exp(s - lse_h)
                    dp = lax.dot_general(dohb, v, _NT, preferred_element_type=F32)
                    ds = (p * (dp - delta)).astype(CDT)
                    dq = jnp.dot(ds, kh, preferred_element_type=F32)
                    dqs[hd] = dqs[hd] + (dq if split else jnp.where(m, dq, 0.0))
                    dk_ref[rows, cols] += lax.dot_general(ds, qh, _TN, preferred_element_type=F32)
                    dv = dv + lax.dot_general(p.astype(CDT), dohb, _TN, preferred_element_type=F32)
                dv_ref[rows, :] += dv
            if split:
                dq_ref[qrows, 0:LANES] = dqs[0]
                dq_ref[qrows, LANES:2 * LANES] = dqs[1]
            else:
                dq_ref[qrows, :] = dqs[0] + dqs[1]

    kvo = (lambda p, i: (0, 0)) if shared_kv else (lambda p, i: (0, p))
    return pl.pallas_call(
        body, name=name, grid=(npairs, S // (nsub * tq)),
        in_specs=[pl.BlockSpec((nsub * tq, qw), lambda p, i: (i, p)),
                  pl.BlockSpec((S, qw), lambda p, i: (0, kblk(p))),
                  pl.BlockSpec((S, LANES), lambda p, i: (0, vblk(p))),
                  pl.BlockSpec((nsub * tq, LANES), lambda p, i: (i, doblk(p))),
                  pl.BlockSpec((nsub * tq, LANES), lambda p, i: (i, p)),
                  pl.BlockSpec((nsub * tq, LANES), lambda p, i: (i, p))],
        out_specs=[pl.BlockSpec((nsub * tq, qw), lambda p, i: (i, p)),
                   pl.BlockSpec((S, qw), kvo), pl.BlockSpec((S, LANES), kvo)],
        out_shape=[_sds((S, qw * npairs), F32), _sds((S, qw * nkv), F32), _sds((S, LANES * nkv), F32)],
        compiler_params=_cparams(2),
    )(q, k, v, do, o, lse)


def _bias_expand(idx, rel_bias, name):
    tq, kw = idx.shape

    def body(idx_ref, rb_ref, o_ref):
        idx = idx_ref[...]
        for hd in range(DIL_HEADS):
            acc = jnp.full((tq, kw), NEG_INF, F32)
            for u in range(REL_BUCKETS):
                acc = jnp.where(idx == u, rb_ref[u, hd], acc)
            o_ref[hd] = acc

    return pl.pallas_call(
        body, name=name,
        in_specs=[pl.BlockSpec(memory_space=pltpu.VMEM), pl.BlockSpec(memory_space=pltpu.SMEM)],
        out_specs=pl.BlockSpec(memory_space=pltpu.VMEM),
        out_shape=_sds((DIL_HEADS, tq, kw), F32),
    )(idx, rel_bias)


def _bias_reduce(idx, dtab, name):
    tq, kw = idx.shape

    def body(idx_ref, d_ref, o_ref):
        idx = idx_ref[...]
        rowid = lax.broadcasted_iota(jnp.int32, (REL_BUCKETS, kw), 0)
        for hd in range(DIL_HEADS):
            d = d_ref[hd]
            acc = jnp.zeros((REL_BUCKETS, kw), F32)
            for u in range(REL_BUCKETS):
                r = jnp.sum(jnp.where(idx == u, d, 0.0), axis=0, keepdims=True)
                acc = jnp.where(rowid == u, r, acc)
            o_ref[hd] = jnp.sum(acc, axis=1, keepdims=True)

    return pl.pallas_call(
        body, name=name,
        in_specs=[pl.BlockSpec(memory_space=pltpu.VMEM)] * 2, out_specs=pl.BlockSpec(memory_space=pltpu.VMEM),
        out_shape=_sds((DIL_HEADS, REL_BUCKETS, 1), F32),
    )(idx, dtab)


DIL_TILES_PER_STEP = 4


def _dil_tiles_per_step(L, tq):
    return DIL_TILES_PER_STEP if L % (DIL_TILES_PER_STEP * tq) == 0 else 1


def _dil_window(i, tq, kw, L):
    start = pl.multiple_of(i * tq, DIL_HALF)
    key = start + lax.broadcasted_iota(jnp.int32, (1, kw), 1) - DIL_HALF
    return start, (key >= 0) & (key < L)


def _dil_fwd(qv, kv, vv, tab, *, dil, L, tq, name):
    kw = tq + 2 * DIL_HALF
    npair = DIL_HEADS // 2
    nsub = _dil_tiles_per_step(L, tq)

    def body(q_ref, k_ref, v_ref, t_ref, o_ref, lse_ref):
        masks = _pair_masks()
        for j in range(nsub):
            rows = slice(j * tq, (j + 1) * tq)
            start, valid = _dil_window(pl.program_id(2) * nsub + j, tq, kw, L)
            kwin = k_ref[pl.ds(start, kw), :]
            vwin = v_ref[pl.ds(start, kw), :]
            q = q_ref[rows, :]
            outs, lses = [], []
            for hd in range(2):
                qh = jnp.where(masks[hd], q, jnp.zeros_like(q))
                s = lax.dot_general(qh, kwin, _NT, preferred_element_type=F32) + t_ref[hd]
                s = jnp.where(valid, s, NEG_INF)
                mx = jnp.max(s, axis=-1, keepdims=True)
                p = jnp.exp(s - mx)
                l = jnp.sum(p, axis=-1, keepdims=True)
                outs.append(jnp.dot(p.astype(CDT), vwin, preferred_element_type=F32) / l)
                lses.append(jnp.broadcast_to(mx + jnp.log(l), (tq, LANES)))
            o_ref[rows, :] = jnp.where(masks[0], outs[0], outs[1])
            lse_ref[rows, :] = jnp.where(masks[0], lses[0], lses[1])

    blk = pl.BlockSpec((None, nsub * tq, LANES), lambda p, c, i: (c, i, p))
    res = pl.BlockSpec((None, L + 2 * DIL_HALF, LANES), lambda p, c, i: (c, 0, p))
    return pl.pallas_call(
        body, name=name, grid=(npair, dil, L // (nsub * tq)),
        in_specs=[blk, res, res, pl.BlockSpec((2, tq, kw), lambda p, c, i: (p, 0, 0))],
        out_specs=[blk] * 2, out_shape=[_sds(qv.shape, F32)] * 2,
        compiler_params=_cparams(3),
    )(qv, kv, vv, tab)


def _dil_bwd(qv, kv, vv, tab, dov, lsev, deltav, *, dil, L, tq, name):
    kw = tq + 2 * DIL_HALF
    npair = DIL_HEADS // 2
    nsub = _dil_tiles_per_step(L, tq)

    def body(q_ref, k_ref, v_ref, t_ref, do_ref, lse_ref, dl_ref, dq_ref, dk_ref, dv_ref, dt_ref):
        masks = _pair_masks()
        c_id, i_id = pl.program_id(1), pl.program_id(2)

        @pl.when(i_id == 0)
        def _():
            dk_ref[...] = jnp.zeros_like(dk_ref)
            dv_ref[...] = jnp.zeros_like(dv_ref)

        @pl.when((i_id == 0) & (c_id == 0))
        def _():
            dt_ref[...] = jnp.zeros_like(dt_ref)

        dts = [jnp.zeros((tq, kw), F32), jnp.zeros((tq, kw), F32)]
        for j in range(nsub):
            rows = slice(j * tq, (j + 1) * tq)
            start, valid = _dil_window(i_id * nsub + j, tq, kw, L)
            kwin = k_ref[pl.ds(start, kw), :]
            vwin = v_ref[pl.ds(start, kw), :]
            q, do, lse, dl = q_ref[rows, :], do_ref[rows, :], lse_ref[rows, :], dl_ref[rows, :]
            dq = jnp.zeros((tq, LANES), F32)
            dk = jnp.zeros((kw, LANES), F32)
            dv = jnp.zeros((kw, LANES), F32)
            for hd in range(2):
                m = masks[hd]
                qh = jnp.where(m, q, jnp.zeros_like(q))
                doh = jnp.where(m, do, jnp.zeros_like(do))
                s = lax.dot_general(qh, kwin, _NT, preferred_element_type=F32) + t_ref[hd]
                s = jnp.where(valid, s, NEG_INF)
                p = jnp.exp(s - _head_scalar(lse, m))
                dp = lax.dot_general(doh, vwin, _NT, preferred_element_type=F32)
                ds = p * (dp - _head_scalar(dl, m))
                dts[hd] = dts[hd] + ds
                dsb = ds.astype(CDT)
                dq = dq + jnp.where(m, jnp.dot(dsb, kwin, preferred_element_type=F32), 0.0)
                dk = dk + lax.dot_general(dsb, qh, _TN, preferred_element_type=F32)
                dv = dv + lax.dot_general(p.astype(CDT), doh, _TN, preferred_element_type=F32)
            dq_ref[rows, :] = dq
            dk_ref[pl.ds(start, kw), :] += dk
            dv_ref[pl.ds(start, kw), :] += dv
        for hd in range(2):
            dt_ref[hd] += dts[hd]

    blk = pl.BlockSpec((None, nsub * tq, LANES), lambda p, c, i: (c, i, p))
    res = pl.BlockSpec((None, L + 2 * DIL_HALF, LANES), lambda p, c, i: (c, 0, p))
    tsp = pl.BlockSpec((2, tq, kw), lambda p, c, i: (p, 0, 0))
    return pl.pallas_call(
        body, name=name, grid=(npair, dil, L // (nsub * tq)),
        in_specs=[blk, res, res, tsp, blk, blk, blk], out_specs=[blk, res, res, tsp],
        out_shape=[_sds(qv.shape, F32), _sds(kv.shape, F32), _sds(kv.shape, F32), _sds(tab.shape, F32)],
        compiler_params=_cparams(3),
    )(qv, kv, vv, tab, dov, lsev, deltav)


def _mix_weights(l1, l2, l3):
    mx = jnp.maximum(jnp.maximum(l1, l2), l3)
    e1, e2, e3 = jnp.exp(l1 - mx), jnp.exp(l2 - mx), jnp.exp(l3 - mx)
    inv = 1.0 / (e1 + e2 + e3)
    return e1 * inv, e2 * inv, e3 * inv


def _branch_specs(S, ts):
    dils = [d for _, d in DIL_BRANCHES]
    return dils, [_res_spec(d, ts // d) for d in dils], [(d, S // d, DIL_W) for d in dils]


def _dil_mix_fwd(os, ls):
    S = os[0].shape[0] * os[0].shape[1]
    ts = min(512, S)
    dils, specs, _ = _branch_specs(S, ts)

    def body(o1, o2, o3, l1, l2, l3, out, scr):
        o1, o2, o3, l1, l2, l3 = [_by_token(r, scr, d) for r, d in zip((o1, o2, o3, l1, l2, l3), dils + dils)]
        w1, w2, w3 = _mix_weights(l1, l2, l3)
        out[...] = (w1 * o1 + w2 * o2 + w3 * o3).astype(CDT)

    return pl.pallas_call(
        body, name="dil_mix_fwd", grid=(S // ts,), in_specs=specs + specs, out_specs=_row(ts, DIL_W),
        out_shape=_sds((S, DIL_W), CDT), scratch_shapes=[_TOKEN_SCRATCH(ts)],
        compiler_params=_cparams(1),
    )(*os, *ls)


def _dil_mix_bwd(dcat, os, ls, j384):
    S = os[0].shape[0] * os[0].shape[1]
    ts = min(512, S)
    dils, specs, shapes = _branch_specs(S, ts)

    def body(do_ref, o1, o2, o3, l1, l2, l3, j_ref, d1, d2, d3, e1, e2, e3, scr):
        o1, o2, o3, l1, l2, l3 = [_by_token(r, scr, d) for r, d in zip((o1, o2, o3, l1, l2, l3), dils + dils)]
        ws = _mix_weights(l1, l2, l3)
        do = do_ref[...]
        o = ws[0] * o1 + ws[1] * o2 + ws[2] * o3
        dot = _headsum(do * o, j_ref[...])
        for w, d, d_o, e_o in zip(ws, dils, (d1, d2, d3), (e1, e2, e3)):
            _by_residue(w * do, scr, d_o, d)
            _by_residue(w * dot, scr, e_o, d)

    return pl.pallas_call(
        body, name="dil_mix_bwd", grid=(S // ts,),
        in_specs=[_row(ts, DIL_W, 1)] + specs + specs + [_full(j384.shape)],
        out_specs=specs + specs,
        out_shape=[_sds(s, CDT) for s in shapes] + [_sds(s, F32) for s in shapes],
        scratch_shapes=[_TOKEN_SCRATCH(ts)],
        compiler_params=_cparams(1),
    )(dcat, *os, *ls, j384)


def _adamw_math(w, g, m, v):
    m = ADAM_B1 * m + (1.0 - ADAM_B1) * g
    v = ADAM_B2 * v + (1.0 - ADAM_B2) * (g * g)
    m_hat = m / (1.0 - ADAM_B1 ** ADAM_STEP)
    v_hat = v / (1.0 - ADAM_B2 ** ADAM_STEP)
    delta = -ADAM_LR * (m_hat / (jnp.sqrt(v_hat) + ADAM_EPS) + ADAM_WD * w)
    return delta, m, v


def _pick8(n, target):
    best = None
    for t in range(16, min(n, target) + 1, 16):
        if n % t == 0:
            best = t
    return best if best is not None else n


_ELEMS_PER_BLOCK = 256 * 1024


def _adamw(w, reds, sibs, m, v, owner, name):
    L, a, b = w.shape
    ta = _pick8(a, max(16, _ELEMS_PER_BLOCK // b))
    spec = pl.BlockSpec((None, ta, b), lambda l, i, own: (l, i, 0))
    def gspec(mine, layer):
        def index(l, i, own):
            use = (own[0] if mine else 1 - own[0]) * (l if layer else 1 - l)
            return i * use, 0
        return pl.BlockSpec((ta, b), index)

    def body(own_ref, w_ref, r0_ref, r1_ref, s0_ref, s1_ref, m_ref, v_ref, g_o, d_o, m_o, v_o):
        mine = own_ref[0] == 1
        g0 = jnp.where(mine, r0_ref[...], s0_ref[...])
        g1 = jnp.where(mine, r1_ref[...], s1_ref[...])
        g = jnp.where(pl.program_id(0) == 0, g0, g1)
        d, mm, vv = _adamw_math(w_ref[...], g, m_ref[...], v_ref[...])
        g_o[...] = g
        d_o[...] = d
        m_o[...] = mm
        v_o[...] = vv

    return pl.pallas_call(
        body, name=name, out_shape=[_sds(w.shape, F32)] * 4,
        grid_spec=pltpu.PrefetchScalarGridSpec(
            num_scalar_prefetch=1, grid=(L, a // ta),
            in_specs=[spec, gspec(True, 0), gspec(True, 1), gspec(False, 0), gspec(False, 1), spec, spec],
            out_specs=[spec] * 4),
        compiler_params=_cparams(2),
    )(_is_core(owner), w, *reds, *sibs, m, v)


def _adamw_small(w, gall, m, v):
    R = w.shape[0]

    def body(w_ref, g_ref, m_ref, v_ref, g_o, d_o, m_o, v_o):
        g = g_ref[0]
        for k in range(1, 8):
            g = g + g_ref[k]
        d, mm, vv = _adamw_math(w_ref[...], g, m_ref[...], v_ref[...])
        g_o[...] = g
        d_o[...] = d
        m_o[...] = mm
        v_o[...] = vv

    vm = pl.BlockSpec(memory_space=pltpu.VMEM)
    return pl.pallas_call(
        body, name="adamw_small", in_specs=[vm] * 4, out_specs=[vm] * 4, out_shape=[_sds((R, LANES), F32)] * 4,
    )(w, gall, m, v)


def _sum_pair(g, t, owner, name):
    n, a, b = t.shape
    ta = _pick8(a, max(16, _ELEMS_PER_BLOCK // b))
    spec = pl.BlockSpec((None, ta, b), lambda k, i, own: (k * own[0], i * own[0], 0))

    def body(own_ref, g_ref, t_ref, o_ref):
        @pl.when(own_ref[0] == 1)
        def _():
            o_ref[...] = (g_ref[...].astype(F32) + t_ref[...].astype(F32)).astype(o_ref.dtype)

    return pl.pallas_call(
        body, name=name, out_shape=_sds(t.shape, WIRE),
        grid_spec=pltpu.PrefetchScalarGridSpec(num_scalar_prefetch=1, grid=(n, a // ta), in_specs=[spec] * 2,
                                               out_specs=spec),
        compiler_params=_cparams(2),
    )(_is_core(owner), g, t)


def _sum_chips(pair, t, owner, name):
    _, a, b = t.shape
    ta = _pick8(a, max(16, _ELEMS_PER_BLOCK // b))

    def body(own_ref, p_ref, t_ref, o_ref):
        @pl.when(own_ref[0] == 1)
        def _():
            me = 2 * lax.axis_index("x") + lax.axis_index("y")
            acc = p_ref[me].astype(F32)
            for k in range(3):
                acc = acc + t_ref[k].astype(F32)
            o_ref[...] = acc

    return pl.pallas_call(
        body, name=name, out_shape=_sds((a, b), F32),
        grid_spec=pltpu.PrefetchScalarGridSpec(
            num_scalar_prefetch=1, grid=(a // ta,),
            in_specs=[pl.BlockSpec((4, ta, b), lambda i, own: (0, i * own[0], 0)),
                      pl.BlockSpec((3, ta, b), lambda i, own: (0, i * own[0], 0))],
            out_specs=pl.BlockSpec((ta, b), lambda i, own: (i * own[0], 0))),
        compiler_params=_cparams(1),
    )(_is_core(owner), pair, t)


def _is_core(core):
    return (lax.axis_index("c") == core).astype(jnp.int32).reshape(1)


_HBM = pl.BlockSpec(memory_space=pltpu.HBM)


def _place():
    x, y, c = lax.axis_index("x"), lax.axis_index("y"), lax.axis_index("c")
    chips = [(1 - x, y), (x, 1 - y), (1 - x, 1 - y)]
    return x, y, c, chips


def _remote(src, dst, ssem, rsem, to):
    return pltpu.make_async_remote_copy(src_ref=src, dst_ref=dst, send_sem=ssem, recv_sem=rsem, device_id=to,
                                        device_id_type=MESH_ID)


def _dma_sems(n):
    return pltpu.SemaphoreType.DMA((n,))


_SEM = pl.BlockSpec(memory_space=pltpu.SEMAPHORE)
_ANY = pl.BlockSpec(memory_space=pl.ANY)
_EFFECT = pltpu.SideEffectType.DATAFLOW_SIDE_EFFECTING
_BIG = ("w_in", "mla_w_uq", "mla_w_ukv", "w_out", "ffn_w_gate", "ffn_w_up", "ffn_w_down")
_OWNER = dict(zip(_BIG, (1, 0, 0, 1, 0, 0, 1)))
_ATTN_WEIGHTS, _FFN_WEIGHTS = _BIG[:4], _BIG[4:]


def _hbm(a):
    return pltpu.with_memory_space_constraint(a, pltpu.HBM)


def _per_core(c, owners, fn):
    for g in range(2):
        mine = tuple(p for p, o in enumerate(owners) if o == g)
        theirs = tuple(p for p, o in enumerate(owners) if o != g)
        pl.when(c == g)(functools.partial(fn, mine, theirs))


def _token_spec():
    return pl.BlockSpec(memory_space=pltpu.VMEM), _sds((8, LANES), F32)


def _gather_start(shards, owners, layer, tag):
    n = len(shards)
    lands = [_hbm(lax.empty((4,) + s.shape[1:], s.dtype)) for s in shards]

    def body(*refs):
        w_refs, l_refs = refs[:n], refs[n:2 * n]
        ssem, rsem, token = refs[2 * n], refs[2 * n + 1], refs[-1]
        x, y, c, chips = _place()
        me = 2 * x + y

        def send(mine, _):
            for i in mine:
                for j, (cx, cy) in enumerate(chips):
                    _remote(w_refs[i].at[layer], l_refs[i].at[me], ssem.at[3 * i + j], rsem.at[3 * i + j],
                            (cx, cy, c)).start()

        _per_core(c, owners, send)
        token[...] = jnp.zeros_like(token)

    tspec, tshape = _token_spec()
    out = pl.pallas_call(
        body, name=f"gather_start_{tag}", in_specs=[_HBM] * (2 * n),
        out_specs=[_SEM, _SEM] + [_HBM] * n + [tspec],
        out_shape=[_dma_sems(3 * n), _dma_sems(3 * n)] + [pltpu.HBM(l.shape, l.dtype) for l in lands] + [tshape],
        input_output_aliases={n + i: 2 + i for i in range(n)},
        compiler_params=pltpu.CompilerParams(has_side_effects=_EFFECT),
    )(*[_hbm(s) for s in shards], *lands)
    return out[0], out[1], list(out[2:2 + n]), out[-1]


def _gather_wait(ssem, rsem, shards, lands, after, owners, layer, tag):
    n = len(shards)

    def body(*refs):
        w_refs, l_refs = refs[:n], refs[n:2 * n]
        ssem, rsem = refs[2 * n], refs[2 * n + 1]
        x, y, c, chips = _place()

        def wait(mine, _):
            for i in mine:
                for j, (cx, cy) in enumerate(chips):
                    cp = _remote(w_refs[i].at[layer], l_refs[i].at[2 * cx + cy], ssem.at[3 * i + j], rsem.at[3 * i + j],
                                 (cx, cy, c))
                    cp.wait_send()
                    cp.wait_recv()

        _per_core(c, owners, wait)

    return list(pl.pallas_call(
        body, name=f"gather_wait_{tag}", in_specs=[_HBM] * (2 * n) + [_SEM, _SEM, _ANY], out_specs=[_HBM] * n,
        out_shape=[pltpu.HBM(l.shape, l.dtype) for l in lands],
        input_output_aliases={n + i: i for i in range(n)},
        compiler_params=pltpu.CompilerParams(has_side_effects=_EFFECT),
    )(*[_hbm(s) for s in shards], *lands, ssem, rsem, after))


def _gather_finish(shards, lands, owners, layer, tag):
    n = len(shards)

    def body(*refs):
        w_refs, g_refs = refs[:n], refs[2 * n:3 * n]
        ssem, rsem = refs[3 * n:]
        x, y, c, chips = _place()
        me = 2 * x + y
        sib = (x, y, 1 - c)
        owns = [_remote(w.at[layer], g.at[me], ssem.at[i], rsem.at[i], sib) for i, (w, g) in enumerate(zip(w_refs, g_refs))]
        for cp in owns:
            cp.start()

        def forward(mine, theirs):
            def blk(i, j):
                b = g_refs[i].at[2 * chips[j][0] + chips[j][1]]
                return _remote(b, b, ssem.at[n + 3 * i + j], rsem.at[n + 3 * i + j], sib)

            for i in mine:
                for j in range(3):
                    blk(i, j).start()
            for i in theirs:
                for j in range(3):
                    blk(i, j).wait_recv()
            for i in mine:
                for j in range(3):
                    blk(i, j).wait_send()

        _per_core(c, owners, forward)
        for cp in owns:
            cp.wait_recv()
            cp.wait_send()

    return list(pl.pallas_call(
        body, name=f"gather_finish_{tag}", in_specs=[_HBM] * (2 * n), out_specs=[_HBM] * n,
        out_shape=[_sds(l.shape, l.dtype) for l in lands], input_output_aliases={n + i: i for i in range(n)},
        scratch_shapes=[_dma_sems(4 * n), _dma_sems(4 * n)],
    )(*shards, *lands))


def _rs_to_owner(grads, owners, tag):
    n = len(grads)

    def body(*refs):
        g_refs, t_refs = refs[:n], refs[n:2 * n]
        ssem, rsem = refs[2 * n:]
        x, y, c, _ = _place()

        def swap(mine, theirs):
            cps = [_remote(g_refs[i], t_refs[i], ssem.at[i], rsem.at[i], (x, y, 1 - c)) for i in theirs]
            for cp in cps:
                cp.start()
            for i in mine:
                _remote(g_refs[i], t_refs[i], ssem.at[i], rsem.at[i], (x, y, 1 - c)).wait_recv()
            for cp in cps:
                cp.wait_send()

        _per_core(c, owners, swap)

    return list(pl.pallas_call(
        body, name=f"rs_to_owner_{tag}", in_specs=[_HBM] * n, out_specs=[_HBM] * n,
        out_shape=[_sds(g.shape, g.dtype) for g in grads], scratch_shapes=[_dma_sems(n), _dma_sems(n)],
    )(*grads))


def _a2a_start(pairs, owners, tag):
    n = len(pairs)
    lands = [_hbm(lax.empty((3,) + p.shape[1:], p.dtype)) for p in pairs]

    def body(*refs):
        a_refs, t_refs = refs[:n], refs[n:2 * n]
        ssem, rsem, token = refs[2 * n], refs[2 * n + 1], refs[-1]
        x, y, c, chips = _place()

        def send(mine, _):
            for i in mine:
                for j, (cx, cy) in enumerate(chips):
                    _remote(a_refs[i].at[2 * cx + cy], t_refs[i].at[j], ssem.at[3 * i + j], rsem.at[3 * i + j],
                            (cx, cy, c)).start()

        _per_core(c, owners, send)
        token[...] = jnp.zeros_like(token)

    tspec, tshape = _token_spec()
    out = pl.pallas_call(
        body, name=f"rs_a2a_start_{tag}", in_specs=[_HBM] * (2 * n),
        out_specs=[_SEM, _SEM] + [_HBM] * n + [tspec],
        out_shape=[_dma_sems(3 * n), _dma_sems(3 * n)] + [pltpu.HBM(l.shape, l.dtype) for l in lands] + [tshape],
        input_output_aliases={n + i: 2 + i for i in range(n)},
        compiler_params=pltpu.CompilerParams(has_side_effects=_EFFECT),
    )(*[_hbm(p) for p in pairs], *lands)
    return out[0], out[1], list(out[2:2 + n]), out[-1]


def _a2a_wait(ssem, rsem, pairs, lands, after, owners, tag):
    n = len(pairs)

    def body(*refs):
        a_refs, t_refs = refs[:n], refs[n:2 * n]
        ssem, rsem = refs[2 * n], refs[2 * n + 1]
        x, y, c, chips = _place()

        def wait(mine, _):
            for i in mine:
                for j, (cx, cy) in enumerate(chips):
                    cp = _remote(a_refs[i].at[2 * cx + cy], t_refs[i].at[j], ssem.at[3 * i + j], rsem.at[3 * i + j],
                                 (cx, cy, c))
                    cp.wait_send()
                    cp.wait_recv()

        _per_core(c, owners, wait)

    return list(pl.pallas_call(
        body, name=f"rs_a2a_wait_{tag}", in_specs=[_HBM] * (2 * n) + [_SEM, _SEM, _ANY], out_specs=[_HBM] * n,
        out_shape=[pltpu.HBM(l.shape, l.dtype) for l in lands],
        input_output_aliases={n + i: i for i in range(n)},
        compiler_params=pltpu.CompilerParams(has_side_effects=_EFFECT),
    )(*[_hbm(p) for p in pairs], *lands, ssem, rsem, after))


def _rs_from_owner(reds, owners):
    n = len(reds)

    def body(*refs):
        q_refs, o_refs = refs[:n], refs[n:2 * n]
        ssem, rsem = refs[2 * n:]
        x, y, c, _ = _place()

        def swap(mine, theirs):
            cps = [_remote(q_refs[k], o_refs[k], ssem.at[k], rsem.at[k], (x, y, 1 - c)) for k in mine]
            for cp in cps:
                cp.start()
            for k in theirs:
                _remote(q_refs[k], o_refs[k], ssem.at[k], rsem.at[k], (x, y, 1 - c)).wait_recv()
            for cp in cps:
                cp.wait_send()

        _per_core(c, owners, swap)

    return list(pl.pallas_call(
        body, name="rs_from_owner", in_specs=[_HBM] * n, out_specs=[_HBM] * n,
        out_shape=[_sds(q.shape, q.dtype) for q in reds], scratch_shapes=[_dma_sems(n), _dma_sems(n)],
    )(*reds))


def _gather_small(s):
    R, _ = s.shape

    def body(s_ref, o_ref, ssem, rsem, lsem):
        x, y, c, _ = _place()
        me = 4 * x + 2 * y + c
        own = pltpu.make_async_copy(s_ref, o_ref.at[me], lsem)
        own.start()
        sends = []
        for k in range(1, 8):
            px, py, pc = x ^ (k >> 2), y ^ ((k >> 1) & 1), c ^ (k & 1)
            cp = _remote(s_ref, o_ref.at[me], ssem.at[k - 1], rsem.at[k - 1], (px, py, pc))
            cp.start()
            sends.append(cp)
        for k in range(1, 8):
            px, py, pc = x ^ (k >> 2), y ^ ((k >> 1) & 1), c ^ (k & 1)
            blk = o_ref.at[4 * px + 2 * py + pc]
            _remote(blk, blk, ssem.at[k - 1], rsem.at[k - 1], (px, py, pc)).wait_recv()
        for cp in sends:
            cp.wait_send()
        own.wait()

    vm = pl.BlockSpec(memory_space=pltpu.VMEM)
    return pl.pallas_call(
        body, name="gather_small", in_specs=[vm], out_specs=vm, out_shape=_sds((8, R, LANES), s.dtype),
        scratch_shapes=[pltpu.SemaphoreType.DMA((7,)), pltpu.SemaphoreType.DMA((7,)), pltpu.SemaphoreType.DMA],
    )(s)


_COL_SHARDED =("w_in", "mla_w_uq", "mla_w_ukv", "ffn_w_gate", "ffn_w_up")
_SMALL = ("mla_q_norm", "mla_kv_norm", "gqa_q_norm", "gqa_k_norm", "rel_bias", "ln1_g", "ln1_b", "ln2_g", "ln2_b")


def _pack_flat(arrs, align):
    flat = jnp.concatenate([a.reshape(-1) for a in arrs])
    pad = (-flat.shape[0]) % align
    return jnp.pad(flat, (0, pad)) if pad else flat


def _unpack_flat(flat, shapes):
    out, off = [], 0
    for s in shapes:
        n = int(np.prod(s))
        out.append(flat[off:off + n].reshape(s))
        off += n
    return out


def _perm_gqa_rows(w):
    return jnp.concatenate([w[:832], w[896:960], w[832:896], w[960:]], axis=0)


def _local_step(x, target, small, depth, weights_of_layer, grads_done):
    S, D = x.shape
    alpha = (2.0 * depth) ** 0.25
    in_idx, uq_idx, ukv_idx = _in_cols(), _uq_cols(), _ukv_cols()
    win, wuq, wukv, wout, wg, wu, wdn = ([None] * depth for _ in range(7))

    tm, tg = _rope_tables(S)
    j256, j384 = _head_ones(256), _head_ones(384)
    mla_scale = (64 + MLA_ROPE_DIM) ** -0.5
    branches = []
    for (_, dil) in DIL_BRANCHES:
        L = S // dil
        tq = min(256, L)
        idx = jnp.asarray(_branch_bucket_idx(tq, dil))
        branches.append((dil, L, tq, idx))
    tabs = [_bias_expand(idx, small["rel_bias"], name=f"bias_expand_{b}") for b, (_, _, _, idx) in enumerate(branches)]

    def padded(a):
        z = jnp.zeros((DIL_HALF, a.shape[1]), a.dtype)
        return jnp.concatenate([z, a, z], axis=0)[None]

    saved = []
    xf, xb = x, x.astype(CDT)
    for l in range(depth):
        W, token = weights_of_layer(l, "attn", xb)
        win[l] = _rows_from_shards(W["w_in"], in_idx)
        wuq[l] = _rows_from_shards(W["mla_w_uq"], uq_idx)
        wukv[l] = _rows_from_shards(W["mla_w_ukv"], ukv_idx)
        wout[l] = _perm_gqa_rows(W["w_out"].reshape(-1, D))
        gq, gkv = small["mla_q_norm"][l][None], small["mla_kv_norm"][l][None]
        if token is not None:
            gq = gq + token[0, 0]
        ggq = jnp.tile(small["gqa_q_norm"][l], 4)[None]
        ggk = jnp.tile(small["gqa_k_norm"][l], 2)[None]
        h = _mm(xb, win[l], tb=True, name="mm_in")
        cq, ckv, kr, qd, kd, vd, qg, kg, vg, *strided = _prep_fwd(h, gq, gkv, ggq, ggk, tm, tg, j256)
        qkv = [(qd[None], padded(kd), padded(vd))] + [tuple(strided[3 * b:3 * b + 3]) for b in range(len(DIL_STRIDES))]
        qa = _mm(cq, wuq[l], tb=True, name="mm_uq")
        kvp = _mm(ckv, wukv[l], tb=True, out_dtype=CDT, name="mm_ukv")
        qm, km = _mla_prep_fwd(qa, kvp, kr, tm, mla_scale)
        oa, lsa = _attn_fwd(qm, km, kvp, split=True, npairs=3, kblk=lambda p: p, vblk=lambda p: 6 + p,
                            name="mla_attn_fwd")
        oc, lsc = _attn_fwd(qg, kg, vg, split=False, npairs=2, kblk=lambda p: 0, vblk=lambda p: 0,
                            name="gqa_attn_fwd")
        obs, lbs = [], []
        for b, (dil, L, tq, _) in enumerate(branches):
            o_b, l_b = _dil_fwd(*qkv[b], tabs[b], dil=dil, L=L, tq=tq, name=f"dil_fwd_{b}")
            obs.append(o_b)
            lbs.append(l_b)
        ob = _dil_mix_fwd(obs, lbs)
        cat = jnp.concatenate([oa, ob, oc], axis=1)
        mix = _mm(cat, wout[l], name="mm_out")
        x1, x1b, z1 = _ln_fwd(xf, mix, small["ln1_g"][l][None], small["ln1_b"][l][None], alpha, name="ln1_fwd")
        W, _ = weights_of_layer(l, "ffn", x1b)
        wg[l], wu[l], wdn[l] = W["ffn_w_gate"], W["ffn_w_up"], W["ffn_w_down"]
        g3, u3, act = _ffn_up(x1b, wg[l], wu[l])
        ff = _ffn_down(act, wdn[l])
        x2, x2b, z2 = _ln_fwd(x1, ff, small["ln2_g"][l][None], small["ln2_b"][l][None], alpha, name="ln2_fwd")
        saved.append(dict(xb=xb, h=h, cq=cq, ckv=ckv, qg=qg, kg=kg, vg=vg, kvp=kvp, qm=qm, km=km, oa=oa, lsa=lsa,
                          oc=oc, lsc=lsc, obs=obs, lbs=lbs, qkv=qkv, cat=cat, z1=z1, x1b=x1b, g3=g3, u3=u3, act=act, z2=z2,
                          gq=gq, gkv=gkv, ggq=ggq, ggk=ggk))
        xf, xb = x2, x2b

    gW = {k: [None] * depth for k in _BIG}
    gS = {k: [None] * depth for k in ("mla_q_norm", "mla_kv_norm", "gqa_q_norm", "gqa_k_norm", "ln1_g", "ln1_b", "ln2_g",
                                      "ln2_b")}
    g_rel = None
    dya, dyb = xf, target
    token = None
    for l in reversed(range(depth)):
        sv = saved[l]
        ln2_g = small["ln2_g"][l][None]
        if token is not None:
            ln2_g = ln2_g + token[0, 0]
        if l == depth - 1:
            dz2, dz2b, gS["ln2_g"][l], gS["ln2_b"][l], loss = _ln_bwd(dya, dyb, sv["z2"], ln2_g, alpha,
                                                                       name="ln2_bwd_loss", loss_head=True)
        else:
            dz2, dz2b, gS["ln2_g"][l], gS["ln2_b"][l] = _ln_bwd(dya, dyb, sv["z2"], ln2_g, alpha, name="ln2_bwd")
        gW["ffn_w_down"][l] = _mm(sv["act"], dz2b, ta=True, ga=True, go=True, out_dtype=WIRE, name="mm_down_dw")
        dg3, du3 = _ffn_down_dx(dz2b, wdn[l], sv["g3"], sv["u3"])
        gW["ffn_w_gate"][l] = _mm(dg3, sv["x1b"], ta=True, ga=True, go=True, out_dtype=WIRE, name="mm_gate_dw")
        gW["ffn_w_up"][l] = _mm(du3, sv["x1b"], ta=True, ga=True, go=True, out_dtype=WIRE, name="mm_up_dw")
        dx1 = _ffn_up_dx(dg3, du3, wg[l], wu[l])
        token = grads_done(l, "ffn", {n: gW[n][l] for n in _FFN_WEIGHTS})
        ln1_g = small["ln1_g"][l][None]
        if token is not None:
            ln1_g = ln1_g + token[0, 0]
        dz1, dz1b, gS["ln1_g"][l], gS["ln1_b"][l] = _ln_bwd(dx1, dz2, sv["z1"], ln1_g, alpha, name="ln1_bwd")
        gW["w_out"][l] = _perm_gqa_rows(_mm(sv["cat"], dz1b, ta=True, out_dtype=WIRE, name="mm_out_dw")).reshape(4, -1, D)
        dcat = _mm(dz1b, wout[l], tb=True, name="mm_out_dx")
        dqg, dkg, dvg = _attn_bwd(sv["qg"], sv["kg"], sv["vg"], dcat, sv["oc"], sv["lsc"], split=False, npairs=2,
                                  kblk=lambda p: 0, vblk=lambda p: 0, doblk=lambda p: 6 + p, shared_kv=True,
                                  name="gqa_attn_bwd")
        dqm, dkm, dvm = _attn_bwd(sv["qm"], sv["km"], sv["kvp"], dcat, sv["oa"], sv["lsa"], split=True, npairs=3,
                                  kblk=lambda p: p, vblk=lambda p: 6 + p, doblk=lambda p: p, shared_kv=False,
                                  name="mla_attn_bwd")
        dqa, dkvp, dkr = _mla_prep_bwd(dqm, dkm, dvm, tm, mla_scale)
        gW["mla_w_uq"][l] = _rows_to_shards(_mm(dqa, sv["cq"], ta=True, out_dtype=WIRE, name="mm_uq_dw"), uq_idx, MLA_HEADS * 96)
        dcq = _mm(dqa, wuq[l], name="mm_uq_dx")
        gW["mla_w_ukv"][l] = _rows_to_shards(_mm(dkvp, sv["ckv"], ta=True, out_dtype=WIRE, name="mm_ukv_dw"), ukv_idx, MLA_HEADS * 128)
        dckv = _mm(dkvp, wukv[l], name="mm_ukv_dx")
        mixb = _dil_mix_bwd(dcat, sv["obs"], sv["lbs"], j384)
        ddq, ddk, ddv = [], [], []
        for b, (dil, L, tq, idx) in enumerate(branches):
            dq_b, dk_b, dv_b, dtab = _dil_bwd(*sv["qkv"][b], tabs[b], mixb[b], sv["lbs"][b], mixb[3 + b], dil=dil, L=L,
                                              tq=tq, name=f"dil_bwd_{b}")
            if dil == 1:
                dq_b, dk_b, dv_b = dq_b[0], dk_b[0, DIL_HALF:DIL_HALF + S], dv_b[0, DIL_HALF:DIL_HALF + S]
            ddq.append(dq_b)
            ddk.append(dk_b)
            ddv.append(dv_b)
            g_b = _bias_reduce(idx, dtab, name=f"bias_reduce_{b}")[:, :, 0].T
            g_rel = g_b if g_rel is None else g_rel + g_b
        dh, n1, n2, n3, n4 = _prep_bwd(sv["h"], dcq, dckv, dkr, ddq, ddk, ddv, dqg, dkg, dvg, sv["gq"], sv["gkv"],
                                       sv["ggq"], sv["ggk"], tg, j256)
        gS["mla_q_norm"][l], gS["mla_kv_norm"][l] = n1[0], n2[0]
        gS["gqa_q_norm"][l] = n3[0].reshape(4, 64).sum(0)
        gS["gqa_k_norm"][l] = n4[0].reshape(2, 64).sum(0)
        gW["w_in"][l] = _rows_to_shards(_mm(dh, sv["xb"], ta=True, out_dtype=WIRE, name="mm_in_dw"), in_idx, IN_W)
        dya = _mm(dh, win[l], name="mm_in_dx")
        dyb = dz1
        token = grads_done(l, "attn", {n: gW[n][l] for n in _ATTN_WEIGHTS})
    grad_x = _axpy(dya, dyb, alpha, name="grad_x")

    gsmall = {k: jnp.stack([a.reshape(-1) for a in v]) for k, v in gS.items()}
    gsmall["rel_bias"] = g_rel
    return loss, grad_x, gsmall


_ORDER = ("w_in", "mla_q_norm", "mla_kv_norm", "mla_w_uq", "mla_w_ukv", "gqa_q_norm", "gqa_k_norm", "rel_bias", "w_out",
          "ln1_g", "ln1_b", "ffn_w_gate", "ffn_w_up", "ffn_w_down", "ln2_g", "ln2_b")


def kernel(x, w_in, mla_q_norm, mla_kv_norm, mla_w_uq, mla_w_ukv, gqa_q_norm, gqa_k_norm, rel_bias, w_out, ln1_g, ln1_b, ffn_w_gate, ffn_w_up, ffn_w_down, ln2_g, ln2_b, loss_target, m_w_in, m_mla_q_norm, m_mla_kv_norm, m_mla_w_uq, m_mla_w_ukv, m_gqa_q_norm, m_gqa_k_norm, m_rel_bias, m_w_out, m_ln1_g, m_ln1_b, m_ffn_w_gate, m_ffn_w_up, m_ffn_w_down, m_ln2_g, m_ln2_b, v_w_in, v_mla_q_norm, v_mla_kv_norm, v_mla_w_uq, v_mla_w_ukv, v_gqa_q_norm, v_gqa_k_norm, v_rel_bias, v_w_out, v_ln1_g, v_ln1_b, v_ffn_w_gate, v_ffn_w_up, v_ffn_w_down, v_ln2_g, v_ln2_b):
    wts = dict(zip(_ORDER, (w_in, mla_q_norm, mla_kv_norm, mla_w_uq, mla_w_ukv, gqa_q_norm, gqa_k_norm, rel_bias, w_out,
                            ln1_g, ln1_b, ffn_w_gate, ffn_w_up, ffn_w_down, ln2_g, ln2_b)))
    mom = dict(zip(_ORDER, (m_w_in, m_mla_q_norm, m_mla_kv_norm, m_mla_w_uq, m_mla_w_ukv, m_gqa_q_norm, m_gqa_k_norm,
                            m_rel_bias, m_w_out, m_ln1_g, m_ln1_b, m_ffn_w_gate, m_ffn_w_up, m_ffn_w_down, m_ln2_g,
                            m_ln2_b)))
    var = dict(zip(_ORDER, (v_w_in, v_mla_q_norm, v_mla_kv_norm, v_mla_w_uq, v_mla_w_ukv, v_gqa_q_norm, v_gqa_k_norm,
                            v_rel_bias, v_w_out, v_ln1_g, v_ln1_b, v_ffn_w_gate, v_ffn_w_up, v_ffn_w_down, v_ln2_g,
                            v_ln2_b)))
    small_shapes = [wts[n].shape for n in _SMALL]
    for d in (wts, mom, var):
        for n in _COL_SHARDED:
            d[n] = d[n].transpose(0, 2, 1)

    depth = 2
    shards = {n: wts[n].astype(WIRE) for n in _BIG}
    flying = {}

    def start_gather(names, l, tag):
        own = tuple(_OWNER[n] for n in names)
        sh = [shards[n] for n in names]
        ssem, rsem, lands, token = _gather_start(sh, own, l, tag)
        return (names, own, sh, ssem, rsem, lands, l, tag), token

    def end_gather(flight, after):
        names, own, sh, ssem, rsem, lands, l, tag = flight
        got = _gather_finish(sh, _gather_wait(ssem, rsem, sh, lands, after, own, l, tag), own, l, tag)
        return {n: g.astype(CDT) for n, g in zip(names, got)}

    def weights_of_layer(l, part, after):
        if (l, part) == (0, "attn"):
            got = end_gather(start_gather(_ATTN_WEIGHTS, 0, "attn0")[0], after)
            flying["ffn0"], t0 = start_gather(_FFN_WEIGHTS, 0, "ffn0")
            flying["layer1"], t1 = start_gather(_BIG, 1, "layer1")
            return got, t0 + t1
        if (l, part) == (0, "ffn"):
            return end_gather(flying.pop("ffn0"), after), None
        if part == "attn":
            flying["w1"] = end_gather(flying.pop("layer1"), after)
        return flying["w1"], None

    def grads_done(l, part, grads):
        names = tuple(grads)
        own = tuple(_OWNER[n] for n in names)
        tag = f"{part}{l}"
        gl = [grads[n] for n in names]
        theirs = _rs_to_owner(gl, own, tag)
        pairs = [_sum_pair(g, t, o, name=f"rs_pair_sum_{n}") for n, g, t, o in zip(names, gl, theirs, own)]
        ssem, rsem, lands, token = _a2a_start(pairs, own, tag)
        flying[tag] = (names, own, ssem, rsem, pairs, lands)
        return token

    small = {n: wts[n] for n in _SMALL}
    loss, grad_x, gsmall = _local_step(x[0], loss_target[0], small, depth, weights_of_layer, grads_done)

    reds = {}
    for l in reversed(range(depth)):
        for part in ("ffn", "attn"):
            tag = f"{part}{l}"
            names, own, ssem, rsem, pairs, lands = flying.pop(tag)
            got = _a2a_wait(ssem, rsem, pairs, lands, grad_x, own, tag)
            for n, p, t, o in zip(names, pairs, got, own):
                reds[n, l] = _sum_chips(p, t, o, name=f"rs_sum_chips_{n}")
    order = [(n, l) for l in range(depth) for n in _BIG]
    sibs = dict(zip(order, _rs_from_owner([reds[k] for k in order], tuple(_OWNER[n] for n, _ in order))))

    sflat = _pack_flat([gsmall[n].reshape(-1) for n in _SMALL], 8 * LANES)
    rs = sflat.shape[0] // LANES
    sall = _gather_small(sflat.reshape(rs, LANES))

    def packed(d):
        return _pack_flat([d[n] for n in _SMALL], 8 * LANES).reshape(rs, LANES)

    outs = {tag: {} for tag in ("grad", "delta", "new_m", "new_v")}
    for n in _BIG:
        res = _adamw(wts[n], [reds[n, 0], reds[n, 1]], [sibs[n, 0], sibs[n, 1]], mom[n], var[n], _OWNER[n],
                     name=f"adamw_{n}")
        for tag, r in zip(("grad", "delta", "new_m", "new_v"), res):
            outs[tag][n] = r.transpose(0, 2, 1) if n in _COL_SHARDED else r
    for tag, smallflat in zip(("grad", "delta", "new_m", "new_v"), _adamw_small(packed(wts), sall, packed(mom), packed(var))):
        outs[tag].update(zip(_SMALL, _unpack_flat(smallflat.reshape(-1), small_shapes)))

    total = lax.psum(loss[0, 0], ("x", "y", "c"))
    return (total, grad_x[None], *[outs["grad"][n] for n in _ORDER], *[outs["delta"][n] for n in _ORDER],
            *[outs["new_m"][n] for n in _ORDER], *[outs["new_v"][n] for n in _ORDER])
```

```python
import functools
import math

import numpy as np
import jax
import jax.numpy as jnp
from jax import lax
from jax.experimental import pallas as pl
from jax.experimental.pallas import tpu as pltpu

F32 = jnp.float32
CDT = jnp.bfloat16
WIRE = jnp.bfloat16

HEAD_DIM = 64
GRID_W = 64
ROPE_THETA = 10000.0
MLA_HEADS = 6
MLA_Q_RANK = 256
MLA_KV_RANK = 128
MLA_ROPE_DIM = 32
DIL_HEADS = 6
DIL_BRANCHES = ((128, 1), (512, 4), (2048, 16))
DIL_HALF = 64
GQA_Q_HEADS = 4
REL_BUCKETS = 32
REL_MAX_DIST = 1024
NEG_INF = -1e30
LANES = 128
VMEM_LIMIT = 56 * 1024 * 1024

ADAM_LR, ADAM_B1, ADAM_B2, ADAM_EPS, ADAM_WD, ADAM_STEP = 0.001, 0.9, 0.999, 1e-08, 0.01, 10

C_CQ, C_CKV, C_KR, C_DQ, C_DK, C_DV, C_GQ, C_GK, C_GV, IN_P = 0, 256, 384, 512, 896, 1280, 1664, 1920, 2048, 2176
IN_W = 2080
MESH_ID = pl.DeviceIdType.MESH


def _cparams(n_axes, vmem=VMEM_LIMIT):
    return pltpu.CompilerParams(dimension_semantics=("arbitrary",) * n_axes, vmem_limit_bytes=vmem)


MAX_WHOLE_DIM = 2304
WHOLE_K_WINDOW_BYTES = 36 * 1024 * 1024


def _pick(n, target):
    best = None
    for t in range(LANES, min(n, target) + 1, LANES):
        if n % t == 0:
            best = t
    if best is not None and (2 * best >= target or n > MAX_WHOLE_DIM):
        return best
    return n


def _sds(shape, dtype):
    return jax.ShapeDtypeStruct(tuple(shape), dtype)


def _in_cols():
    idx = -np.ones((IN_P,), np.int64)
    idx[C_CQ:C_CQ + 256] = np.arange(0, 256)
    idx[C_CKV:C_CKV + 128] = np.arange(256, 384)
    idx[C_KR + 64:C_KR + 96] = np.arange(384, 416)
    idx[C_DQ:C_DQ + 1152] = np.arange(416, 1568)
    gq = 1568 + (np.array([0, 2, 1, 3])[:, None] * 64 + np.arange(64)[None, :]).reshape(-1)
    idx[C_GQ:C_GQ + 256] = gq
    idx[C_GK:C_GK + 256] = np.arange(1824, 2080)
    return idx


def _uq_cols():
    idx = -np.ones((MLA_HEADS * 128,), np.int64)
    for h in range(MLA_HEADS):
        idx[h * 128:h * 128 + 96] = np.arange(96 * h, 96 * h + 96)
    return idx


def _ukv_cols():
    idx = -np.ones((MLA_HEADS * 128 + MLA_HEADS * 64,), np.int64)
    for h in range(MLA_HEADS):
        idx[h * 128:h * 128 + 64] = np.arange(128 * h, 128 * h + 64)
        idx[768 + h * 64:768 + h * 64 + 64] = np.arange(128 * h + 64, 128 * h + 128)
    return idx


def _runs(idx):
    out, i = [], 0
    while i < len(idx):
        j = i + 1
        while j < len(idx) and ((idx[i] < 0 and idx[j] < 0) or (idx[i] >= 0 and idx[j] == idx[j - 1] + 1)):
            j += 1
        out.append((int(idx[i]), j - i))
        i = j
    return out


def _rows_from_shards(sh, idx):
    _, cs, r = sh.shape
    pieces = []
    for first, ln in _runs(idx):
        if first < 0:
            pieces.append(jnp.zeros((ln, r), sh.dtype))
            continue
        while ln > 0:
            k, off = divmod(first, cs)
            take = min(ln, cs - off)
            pieces.append(sh[k, off:off + take, :])
            first, ln = first + take, ln - take
    return jnp.concatenate(pieces, axis=0)


def _rows_to_shards(wp, idx, n):
    inv = np.zeros((n,), np.int64)
    pos = np.nonzero(idx >= 0)[0]
    inv[idx[pos]] = pos
    cs = n // 4
    shards = []
    for k in range(4):
        pieces = [wp[first:first + ln, :] for first, ln in _runs(inv[k * cs:(k + 1) * cs])]
        shards.append(jnp.concatenate(pieces, axis=0))
    return jnp.stack(shards)


def _t5_bucket_np(rel):
    nb = REL_BUCKETS // 2
    exact = nb // 2
    ret = np.where(rel > 0, nb, 0)
    n = np.abs(rel)
    nf = np.maximum(n, 1).astype(np.float32)
    large = exact + (np.log(nf / np.float32(exact)) / np.float32(math.log(REL_MAX_DIST / exact))
                     * np.float32(nb - exact)).astype(np.int32)
    large = np.minimum(large, nb - 1)
    return ret + np.where(n < exact, n, large)


def _branch_bucket_idx(tq, dil):
    kw = tq + 2 * DIL_HALF
    rel = np.arange(kw)[None, :] - DIL_HALF - np.arange(tq)[:, None]
    idx = _t5_bucket_np(rel * dil)
    return np.where(np.abs(rel) <= DIL_HALF, idx, -1).astype(np.int32)


def _rope_tables(S):
    inv = ROPE_THETA ** (-jnp.arange(0, 32, 2, dtype=F32) / 32)
    t = jnp.arange(S)
    pos = t.astype(F32)
    row = (t // GRID_W).astype(F32)
    col = (t % GRID_W).astype(F32)
    lane = np.arange(LANES)
    wm = lane - 64
    is_rope = (lane >= 64) & (lane < 96)
    ang = pos[:, None] * inv[np.where(is_rope, wm % 16, 0)][None, :]
    cm = jnp.where(is_rope[None], jnp.cos(ang), 1.0)
    smm = jnp.where((is_rope & (wm < 16))[None], -jnp.sin(ang), 0.0)
    spm = jnp.where((is_rope & (wm >= 16))[None], jnp.sin(ang), 0.0)
    g = lane % 64
    w = g % 32
    angg = jnp.where((g < 32)[None], row[:, None], col[:, None]) * inv[w % 16][None, :]
    cg = jnp.cos(angg)
    smg = jnp.where((w < 16)[None], -jnp.sin(angg), 0.0)
    spg = jnp.where((w >= 16)[None], jnp.sin(angg), 0.0)
    return (cm, smm, spm), (cg, smg, spg)


def _lanes(t, width):
    return t if width == LANES else jnp.concatenate([t] * (width // LANES), axis=1)


def _rope(x, tabs):
    c, sm, sp = (_lanes(t, x.shape[1]) for t in tabs)
    w = x.shape[1]
    return x * c + pltpu.roll(x, w - 16, 1) * sm + pltpu.roll(x, 16, 1) * sp


def _rope_t(dy, tabs):
    c, sm, sp = (_lanes(t, dy.shape[1]) for t in tabs)
    w = dy.shape[1]
    return dy * c + pltpu.roll(dy * sm, 16, 1) + pltpu.roll(dy * sp, w - 16, 1)


def _head_ones(width):
    i = np.arange(width)
    return jnp.asarray((i[:, None] // HEAD_DIM == i[None, :] // HEAD_DIM).astype(np.float32))


def _headsum(x, j):
    return jnp.dot(x, j, preferred_element_type=F32, precision=lax.Precision.HIGHEST)


def _mm(a, b, *, ta=False, tb=False, ga=False, gb=False, go=False, out_dtype=F32, name):
    G = a.shape[0] if ga else (b.shape[0] if gb else 1)
    a2 = a.shape[1:] if ga else a.shape
    b2 = b.shape[1:] if gb else b.shape
    K, M = a2 if ta else a2[::-1]
    N = b2[0] if tb else b2[1]
    assert (b2[1] if tb else b2[0]) == K
    tm, tn, tk = _pick(M, 1024), _pick(N, 1024), _pick(K, 2048)
    if tm * tn > 1024 * 1152:
        tm = _pick(M, 512)
    if 2 * K * (tm + tn) * jnp.dtype(CDT).itemsize <= WHOLE_K_WINDOW_BYTES:
        tk = K
    nk = K // tk
    steps = nk if (go or G == 1) else G * nk
    dn = (((0 if ta else 1,), (1 if tb else 0,)), ((), ()))

    def body(a_ref, b_ref, o_ref, *acc):
        part = lax.dot_general(a_ref[...], b_ref[...], dn, preferred_element_type=F32)
        if steps == 1:
            o_ref[...] = part.astype(o_ref.dtype)
            return
        acc_ref, = acc
        s = pl.program_id(3)

        @pl.when(s == 0)
        def _():
            acc_ref[...] = part

        @pl.when(s > 0)
        def _():
            acc_ref[...] += part

        @pl.when(s == steps - 1)
        def _():
            o_ref[...] = acc_ref[...].astype(o_ref.dtype)

    def grp(g, s):
        return g if go else s // nk

    def kk(s):
        return s if steps == nk else s % nk

    def spec(grouped, block, index):
        if grouped:
            return pl.BlockSpec((None,) + block, lambda g, i, j, s: (grp(g, s),) + index(i, j, s))
        return pl.BlockSpec(block, lambda g, i, j, s: index(i, j, s))

    a_spec = (spec(ga, (tk, tm), lambda i, j, s: (kk(s), i)) if ta else spec(ga, (tm, tk), lambda i, j, s: (i, kk(s))))
    b_spec = (spec(gb, (tn, tk), lambda i, j, s: (j, kk(s))) if tb else spec(gb, (tk, tn), lambda i, j, s: (kk(s), j)))
    o_spec = spec(go, (tm, tn), lambda i, j, s: (i, j))
    return pl.pallas_call(
        body, name=name, grid=(G if go else 1, M // tm, N // tn, steps),
        in_specs=[a_spec, b_spec], out_specs=o_spec,
        out_shape=_sds(((G,) if go else ()) + (M, N), out_dtype),
        scratch_shapes=[pltpu.VMEM((tm, tn), F32)] if steps > 1 else [],
        compiler_params=_cparams(4),
    )(a, b)


def _row(ts, w, cb=0):
    return pl.BlockSpec((ts, w), lambda i: (i, cb))


def _full(shape):
    nd = len(shape)
    return pl.BlockSpec(tuple(shape), lambda i: (0,) * nd)


def _rms_fwd(x, g, eps=1e-6):
    r = lax.rsqrt(jnp.mean(x * x, axis=-1, keepdims=True) + eps)
    return x * r * g


def _rms_bwd(x, g, dy, eps=1e-6):
    r = lax.rsqrt(jnp.mean(x * x, axis=-1, keepdims=True) + eps)
    gdy = g * dy
    dx = r * gdy - x * (r * r * r) * jnp.mean(x * gdy, axis=-1, keepdims=True)
    return dx, x * r * dy


def _rms_head_fwd(x, g, j, eps=1e-6):
    r = lax.rsqrt(_headsum(x * x, j) * (1.0 / HEAD_DIM) + eps)
    return x * r * g


def _rms_head_bwd(x, g, dy, j, eps=1e-6):
    r = lax.rsqrt(_headsum(x * x, j) * (1.0 / HEAD_DIM) + eps)
    gdy = g * dy
    dx = r * gdy - x * (r * r * r) * (_headsum(x * gdy, j) * (1.0 / HEAD_DIM))
    return dx, x * r * dy


DIL_STRIDES = tuple(d for _, d in DIL_BRANCHES if d > 1)
DIL_W = DIL_HEADS * HEAD_DIM


def _res_spec(d, n, pad_blocks=0):
    return pl.BlockSpec((d, n, DIL_W), lambda i: (0, i + pad_blocks, 0))


def _prep_fwd(h, gq, gkv, ggq, ggk, tm, tg, j256):
    S = h.shape[0]
    ts = min(256, S)
    scale = HEAD_DIM ** -0.5
    nres = len(DIL_STRIDES)

    def body(*refs):
        (h_ref, gq_ref, gkv_ref, ggq_ref, ggk_ref, cm, smm, spm, cg, smg, spg, j_ref), refs = refs[:12], refs[12:]
        refs = refs[2 * nres:]
        (cq_o, ckv_o, kr_o, dq_o, dk_o, dv_o, gq_o, gk_o, gv_o), res_o = refs[:9], refs[9:-1]
        st = refs[-1]
        tabm = (cm[...], smm[...], spm[...])
        tabg = (cg[...], smg[...], spg[...])
        cq_o[...] = _rms_fwd(h_ref[:, C_CQ:C_CQ + 256], gq_ref[...]).astype(CDT)
        ckv_o[...] = _rms_fwd(h_ref[:, C_CKV:C_CKV + 128], gkv_ref[...]).astype(CDT)
        kr_o[...] = _rope(h_ref[:, C_KR:C_KR + 128], tabm).astype(CDT)
        dq_o[...] = (h_ref[:, C_DQ:C_DQ + 384] * scale).astype(CDT)
        dk_o[...] = h_ref[:, C_DK:C_DK + 384].astype(CDT)
        dv_o[...] = h_ref[:, C_DV:C_DV + 384].astype(CDT)
        for j, lanes in _lane_blocks(3 * DIL_W):
            st[j] = h_ref[:, C_DQ + lanes.start:C_DQ + lanes.stop] * (scale if j < 3 else 1.0)
        for bi, d in enumerate(DIL_STRIDES):
            for c in range(d):
                rows = pl.ds(c, ts // d, stride=d)
                for j, lanes in _lane_blocks(3 * DIL_W):
                    res_o[3 * bi + j // 3][c, :, (j % 3) * LANES:(j % 3 + 1) * LANES] = st.at[j][rows, :].astype(CDT)
        qn = _rms_head_fwd(h_ref[:, C_GQ:C_GQ + 256], ggq_ref[...], j_ref[...])
        gq_o[...] = (_rope(qn, tabg) * scale).astype(CDT)
        kn = _rms_head_fwd(h_ref[:, C_GK:C_GK + 128], ggk_ref[...], j_ref[0:128, 0:128])
        gk_o[...] = _rope(kn, tabg).astype(CDT)
        gv_o[...] = h_ref[:, C_GV:C_GV + 128].astype(CDT)

    widths = (256, 128, 128, 384, 384, 384, 256, 128, 128)
    out_specs = [_row(ts, w) for w in widths]
    out_shape = [_sds((S, w), CDT) for w in widths]
    zeros, aliases = [], {}
    for d in DIL_STRIDES:
        n, L = ts // d, S // d
        out_specs += [_res_spec(d, n), _res_spec(d, n, DIL_HALF // n), _res_spec(d, n, DIL_HALF // n)]
        out_shape += [_sds((d, L, DIL_W), CDT)] + [_sds((d, L + 2 * DIL_HALF, DIL_W), CDT)] * 2
        for t in range(2):
            aliases[12 + len(zeros)] = len(out_shape) - 2 + t
            zeros.append(jnp.zeros((d, L + 2 * DIL_HALF, DIL_W), CDT))
    return pl.pallas_call(
        body, name="prep_fwd", grid=(S // ts,),
        in_specs=[_row(ts, IN_P), _full(gq.shape), _full(gkv.shape), _full(ggq.shape), _full(ggk.shape)]
        + [_row(ts, LANES)] * 6 + [_full(j256.shape)] + [pl.BlockSpec(memory_space=pl.ANY)] * len(zeros),
        out_specs=out_specs, out_shape=out_shape, input_output_aliases=aliases,
        scratch_shapes=[pltpu.VMEM((3 * DIL_W // LANES, ts, LANES), F32)],
        compiler_params=_cparams(1),
    )(h, gq, gkv, ggq, ggk, *tm, *tg, j256, *zeros)


def _prep_bwd(h, dcq, dckv, dkr, ddq, ddk, ddv, dgq, dgk, dgv, gq, gkv, ggq, ggk, tg, j256):
    S = h.shape[0]
    ts = min(256, S)
    scale = HEAD_DIM ** -0.5

    def body(h_ref, dcq_r, dckv_r, dkr_r, q1, q2, q3, k1, k2, k3, v1, v2, v3, dgq_r, dgk_r, dgv_r,
             gq_ref, gkv_ref, ggq_ref, ggk_ref, cg, smg, spg, j_ref,
             dh_o, ngq_o, ngkv_o, nggq_o, nggk_o, *scr):
        tabg = (cg[...], smg[...], spg[...])
        first = pl.program_id(0) == 0
        scr, = scr
        d2, d3 = DIL_STRIDES
        dq = q1[...] + _by_token(q2, scr, d2) + _by_token(q3, scr, d3)
        dk = k1[...] + _by_token(k2, scr, d2) + _by_token(k3, scr, d3)
        dv = v1[...] + _by_token(v2, scr, d2) + _by_token(v3, scr, d3)

        def acc(o_ref, val):
            s = jnp.sum(val, axis=0, keepdims=True)

            @pl.when(first)
            def _():
                o_ref[...] = s

            @pl.when(jnp.logical_not(first))
            def _():
                o_ref[...] += s

        dx, dg = _rms_bwd(h_ref[:, C_CQ:C_CQ + 256], gq_ref[...], dcq_r[...])
        dh_o[:, C_CQ:C_CQ + 256] = dx.astype(CDT)
        acc(ngq_o, dg)
        dx, dg = _rms_bwd(h_ref[:, C_CKV:C_CKV + 128], gkv_ref[...], dckv_r[...])
        dh_o[:, C_CKV:C_CKV + 128] = dx.astype(CDT)
        acc(ngkv_o, dg)
        dh_o[:, C_KR:C_KR + 128] = dkr_r[...].astype(CDT)
        dh_o[:, C_DQ:C_DQ + 384] = (dq * scale).astype(CDT)
        dh_o[:, C_DK:C_DK + 384] = dk.astype(CDT)
        dh_o[:, C_DV:C_DV + 384] = dv.astype(CDT)
        dqn = _rope_t(dgq_r[...] * scale, tabg)
        dx, dg = _rms_head_bwd(h_ref[:, C_GQ:C_GQ + 256], ggq_ref[...], dqn, j_ref[...])
        dh_o[:, C_GQ:C_GQ + 256] = dx.astype(CDT)
        acc(nggq_o, dg)
        dkn = _rope_t(dgk_r[...], tabg)
        dx, dg = _rms_head_bwd(h_ref[:, C_GK:C_GK + 128], ggk_ref[...], dkn, j_ref[0:128, 0:128])
        dh_o[:, C_GK:C_GK + 128] = dx.astype(CDT)
        acc(nggk_o, dg)
        dh_o[:, C_GV:C_GV + 128] = dgv_r[...].astype(CDT)

    d2, d3 = DIL_STRIDES
    n2, n3 = ts // d2, ts // d3
    tok = _row(ts, DIL_W)
    return pl.pallas_call(
        body, name="prep_bwd", grid=(S // ts,),
        in_specs=[_row(ts, IN_P), _row(ts, 256), _row(ts, 128), _row(ts, 128)]
        + [tok, _res_spec(d2, n2), _res_spec(d3, n3)]
        + [tok, _res_spec(d2, n2, DIL_HALF // n2), _res_spec(d3, n3, DIL_HALF // n3)] * 2
        + [_row(ts, 256), _row(ts, 128), _row(ts, 128)]
        + [_full(gq.shape), _full(gkv.shape), _full(ggq.shape), _full(ggk.shape)] + [_row(ts, LANES)] * 3
        + [_full(j256.shape)],
        out_specs=[_row(ts, IN_P), _full((1, 256)), _full((1, 128)), _full((1, 256)), _full((1, 128))],
        out_shape=[_sds((S, IN_P), CDT), _sds((1, 256), F32), _sds((1, 128), F32), _sds((1, 256), F32),
                   _sds((1, 128), F32)],
        scratch_shapes=[_TOKEN_SCRATCH(ts)],
        compiler_params=_cparams(1),
    )(h, dcq, dckv, dkr, *ddq, *ddk, *ddv, dgq, dgk, dgv, gq, gkv, ggq, ggk, *tg, j256)


def _lane_blocks(width):
    return [(j, slice(j * LANES, (j + 1) * LANES)) for j in range(width // LANES)]


_TOKEN_SCRATCH = lambda ts: pltpu.VMEM((DIL_W // LANES, ts, LANES), F32)


def _by_token(res_ref, scr_ref, d):
    n = res_ref.shape[1]
    if d == 1:
        return res_ref[0].astype(F32)
    for c in range(d):
        for j, lanes in _lane_blocks(res_ref.shape[2]):
            scr_ref.at[j][pl.ds(c, n, stride=d), :] = res_ref[c, :, lanes].astype(F32)
    return jnp.concatenate([scr_ref[j] for j, _ in _lane_blocks(res_ref.shape[2])], axis=1)


def _by_residue(val, scr_ref, out_ref, d):
    n = out_ref.shape[1]
    if d == 1:
        out_ref[0] = val.astype(out_ref.dtype)
        return
    for j, lanes in _lane_blocks(out_ref.shape[2]):
        scr_ref[j] = val[:, lanes]
    for c in range(d):
        for j, lanes in _lane_blocks(out_ref.shape[2]):
            out_ref[c, :, lanes] = scr_ref.at[j][pl.ds(c, n, stride=d), :].astype(out_ref.dtype)


def _mla_prep_fwd(qa, kvp, kr, tm, scale):
    S = qa.shape[0]
    ts = min(512, S)

    def body(qa_ref, kv_ref, kr_ref, cm, smm, spm, q_o, k_o):
        tabm = (cm[...], smm[...], spm[...])
        q_o[...] = (_rope(qa_ref[...], tabm) * scale).astype(CDT)
        k_o[...] = kv_ref[:, 0:768] + _lanes(kr_ref[...], 768)

    return pl.pallas_call(
        body, name="mla_prep_fwd", grid=(S // ts,),
        in_specs=[_row(ts, 768), _row(ts, 1152), _row(ts, 128)] + [_row(ts, LANES)] * 3,
        out_specs=[_row(ts, 768)] * 2, out_shape=[_sds((S, 768), CDT)] * 2,
        compiler_params=_cparams(1),
    )(qa, kvp, kr, *tm)


def _mla_prep_bwd(dq, dk, dv, tm, scale):
    S = dq.shape[0]
    ts = min(512, S)

    def body(dq_ref, dk_ref, dv_ref, cm, smm, spm, dqa_o, dkv_o, dkr_o):
        tabm = (cm[...], smm[...], spm[...])
        lane = lax.broadcasted_iota(jnp.int32, (1, LANES), 1)
        dqa_o[...] = _rope_t(dq_ref[...] * scale, tabm).astype(CDT)
        dkr = jnp.zeros((ts, LANES), F32)
        for hd in range(MLA_HEADS):
            blk = dk_ref[:, hd * 128:(hd + 1) * 128]
            dkv_o[:, hd * 128:(hd + 1) * 128] = jnp.where(lane < 64, blk, 0.0).astype(CDT)
            dkr = dkr + jnp.where((lane >= 64) & (lane < 96), blk, 0.0)
        dkv_o[:, 768:1152] = dv_ref[...].astype(CDT)
        dkr_o[...] = jnp.where((lane >= 64) & (lane < 96), _rope_t(dkr, tabm), 0.0)

    return pl.pallas_call(
        body, name="mla_prep_bwd", grid=(S // ts,),
        in_specs=[_row(ts, 768), _row(ts, 768), _row(ts, 384)] + [_row(ts, LANES)] * 3,
        out_specs=[_row(ts, 768), _row(ts, 1152), _row(ts, 128)],
        out_shape=[_sds((S, 768), CDT), _sds((S, 1152), CDT), _sds((S, 128), F32)],
        compiler_params=_cparams(1),
    )(dq, dk, dv, *tm)


def _ln_fwd(xa, xb, g, b, alpha, name):
    S, D = xa.shape
    ts = min(512, S)

    def body(xa_ref, xb_ref, g_ref, b_ref, y_o, yb_o, z_o):
        z = alpha * xa_ref[...] + xb_ref[...]
        mu = jnp.mean(z, axis=-1, keepdims=True)
        zc = z - mu
        var = jnp.mean(zc * zc, axis=-1, keepdims=True)
        y = zc * lax.rsqrt(var + 1e-5) * g_ref[...] + b_ref[...]
        y_o[...] = y
        yb_o[...] = y.astype(CDT)
        z_o[...] = z

    return pl.pallas_call(
        body, name=name, grid=(S // ts,),
        in_specs=[_row(ts, D), _row(ts, D), _full(g.shape), _full(b.shape)],
        out_specs=[_row(ts, D)] * 3, out_shape=[_sds((S, D), F32), _sds((S, D), CDT), _sds((S, D), F32)],
        compiler_params=_cparams(1),
    )(xa, xb, g, b)


def _ln_bwd(dya, dyb, z, g, alpha, name, loss_head=False):
    S, D = z.shape
    ts = min(512, S)

    def body(dya_ref, dyb_ref, z_ref, g_ref, dz_o, dzb_o, dg_o, db_o, *loss_o):
        first = pl.program_id(0) == 0
        if loss_head:
            err = dya_ref[...] - dyb_ref[...]
            dy = err * (1.0 / D)
            part = jnp.sum(jnp.sum(err * err, axis=1, keepdims=True), axis=0, keepdims=True) * (0.5 / D)

            @pl.when(first)
            def _():
                loss_o[0][...] = part

            @pl.when(jnp.logical_not(first))
            def _():
                loss_o[0][...] += part
        else:
            dy = dya_ref[...] + alpha * dyb_ref[...]
        z = z_ref[...]
        mu = jnp.mean(z, axis=-1, keepdims=True)
        zc = z - mu
        r = lax.rsqrt(jnp.mean(zc * zc, axis=-1, keepdims=True) + 1e-5)
        xh = zc * r
        dxh = dy * g_ref[...]
        dz = r * (dxh - jnp.mean(dxh, axis=-1, keepdims=True) - xh * jnp.mean(dxh * xh, axis=-1, keepdims=True))
        dz_o[...] = dz
        dzb_o[...] = dz.astype(CDT)
        sg = jnp.sum(dy * xh, axis=0, keepdims=True)
        sb = jnp.sum(dy, axis=0, keepdims=True)

        @pl.when(first)
        def _():
            dg_o[...] = sg
            db_o[...] = sb

        @pl.when(jnp.logical_not(first))
        def _():
            dg_o[...] += sg
            db_o[...] += sb

    extra = ([_full((1, 1))], [_sds((1, 1), F32)]) if loss_head else ([], [])
    return pl.pallas_call(
        body, name=name, grid=(S // ts,),
        in_specs=[_row(ts, D)] * 3 + [_full(g.shape)],
        out_specs=[_row(ts, D), _row(ts, D), _full((1, D)), _full((1, D))] + extra[0],
        out_shape=[_sds((S, D), F32), _sds((S, D), CDT), _sds((1, D), F32), _sds((1, D), F32)] + extra[1],
        compiler_params=_cparams(1),
    )(dya, dyb, z, g)


def _grp_spec(ts, w):
    return pl.BlockSpec((None, ts, w), lambda k, i: (k, i, 0))


def _ffn_up(xb, wg3, wu3):
    S, D = xb.shape
    G, Fc, _ = wg3.shape
    tm = _pick(S, 1024)
    wspec = pl.BlockSpec((None, Fc, D), lambda k, i: (k, 0, 0))

    def body(x_ref, wg_ref, wu_ref, g_o, u_o, a_o):
        x = x_ref[...]
        g = lax.dot_general(x, wg_ref[...], _NT, preferred_element_type=F32)
        u = lax.dot_general(x, wu_ref[...], _NT, preferred_element_type=F32)
        g_o[...] = g.astype(CDT)
        u_o[...] = u.astype(CDT)
        a_o[...] = (g / (1.0 + jnp.exp(-g)) * u).astype(CDT)

    return pl.pallas_call(
        body, name="ffn_up", grid=(G, S // tm),
        in_specs=[pl.BlockSpec((tm, D), lambda k, i: (i, 0)), wspec, wspec], out_specs=[_grp_spec(tm, Fc)] * 3,
        out_shape=[_sds((G, S, Fc), CDT)] * 3, compiler_params=_cparams(2),
    )(xb, wg3, wu3)


def _ffn_up_dx(dg3, du3, wg3, wu3):
    G, S, Fc = dg3.shape
    D = wg3.shape[2]
    tm = _pick(S, 512)
    wspec = pl.BlockSpec((G, Fc, D), lambda i: (0, 0, 0))
    aspec = pl.BlockSpec((G, tm, Fc), lambda i: (0, i, 0))

    def body(dg_ref, du_ref, wg_ref, wu_ref, o_ref):
        acc = None
        for k in range(G):
            for a_ref, w_ref in ((dg_ref, wg_ref), (du_ref, wu_ref)):
                part = jnp.dot(a_ref[k], w_ref[k], preferred_element_type=F32)
                acc = part if acc is None else acc + part
        o_ref[...] = acc

    return pl.pallas_call(
        body, name="ffn_up_dx", grid=(S // tm,), in_specs=[aspec, aspec, wspec, wspec],
        out_specs=pl.BlockSpec((tm, D), lambda i: (i, 0)), out_shape=_sds((S, D), F32), compiler_params=_cparams(1),
    )(dg3, du3, wg3, wu3)


def _ffn_down(act3, wd3):
    G, S, Fc = act3.shape
    D = wd3.shape[2]
    tm = _pick(S, 1024)

    def body(a_ref, w_ref, o_ref):
        acc = jnp.dot(a_ref[0], w_ref[0], preferred_element_type=F32)
        for k in range(1, G):
            acc = acc + jnp.dot(a_ref[k], w_ref[k], preferred_element_type=F32)
        o_ref[...] = acc

    return pl.pallas_call(
        body, name="ffn_down", grid=(S // tm,),
        in_specs=[pl.BlockSpec((G, tm, Fc), lambda i: (0, i, 0)), pl.BlockSpec((G, Fc, D), lambda i: (0, 0, 0))],
        out_specs=pl.BlockSpec((tm, D), lambda i: (i, 0)), out_shape=_sds((S, D), F32), compiler_params=_cparams(1),
    )(act3, wd3)


def _ffn_down_dx(dzb, wd3, g3, u3):
    S, D = dzb.shape
    G, Fc, _ = wd3.shape
    tm = _pick(S, 1024)

    def body(dz_ref, wd_ref, g_ref, u_ref, dg_o, du_o):
        da = lax.dot_general(dz_ref[...], wd_ref[...], _NT, preferred_element_type=F32)
        g = g_ref[...].astype(F32)
        sg = 1.0 / (1.0 + jnp.exp(-g))
        dg_o[...] = (da * u_ref[...].astype(F32) * (sg * (1.0 + g * (1.0 - sg)))).astype(CDT)
        du_o[...] = (da * (g * sg)).astype(CDT)

    return pl.pallas_call(
        body, name="ffn_down_dx", grid=(G, S // tm),
        in_specs=[pl.BlockSpec((tm, D), lambda k, i: (i, 0)), pl.BlockSpec((None, Fc, D), lambda k, i: (k, 0, 0)),
                  _grp_spec(tm, Fc), _grp_spec(tm, Fc)],
        out_specs=[_grp_spec(tm, Fc)] * 2, out_shape=[_sds((G, S, Fc), CDT)] * 2, compiler_params=_cparams(2),
    )(dzb, wd3, g3, u3)


def _axpy(a, b, alpha, name):
    S, D = a.shape
    ts = min(512, S)

    def body(a_ref, b_ref, o_ref):
        o_ref[...] = a_ref[...] + alpha * b_ref[...]

    return pl.pallas_call(
        body, name=name, grid=(S // ts,), in_specs=[_row(ts, D)] * 2, out_specs=_row(ts, D),
        out_shape=_sds((S, D), F32), compiler_params=_cparams(1),
    )(a, b)


def _pair_masks():
    lane = lax.broadcasted_iota(jnp.int32, (1, LANES), 1)
    first = lane < HEAD_DIM
    return first, jnp.logical_not(first)


def _head_scalar(x, m):
    return jnp.max(jnp.where(m, x, -jnp.inf), axis=-1, keepdims=True)


_NT = (((1,), (1,)), ((), ()))
_TN = (((0,), (0,)), ((), ()))
ATTN_KEY_CHUNK = 1024
ATTN_FWD_TILES_PER_STEP = 4
ATTN_BWD_TILES_PER_STEP = 2


def _attn_fwd(q, k, v, *, split, npairs, kblk, vblk, name):
    S = q.shape[0]
    qw = 256 if split else LANES
    tq = min(256, S)
    nsub = ATTN_FWD_TILES_PER_STEP if S % (ATTN_FWD_TILES_PER_STEP * tq) == 0 else 1

    def body(q_ref, k_ref, v_ref, o_ref, lse_ref):
        masks = _pair_masks()
        for j in range(nsub):
            rows = slice(j * tq, (j + 1) * tq)
            outs, lses = [], []
            for hd in range(2):
                if split:
                    qh = q_ref[rows, hd * LANES:(hd + 1) * LANES]
                    kh = k_ref[:, hd * LANES:(hd + 1) * LANES]
                else:
                    qh = jnp.where(masks[hd], q_ref[rows, :], jnp.zeros_like(q_ref[rows, :]))
                    kh = k_ref[...]
                s = lax.dot_general(qh, kh, _NT, preferred_element_type=F32)
                mx = jnp.max(s, axis=-1, keepdims=True)
                p = jnp.exp(s - mx)
                l = jnp.sum(p, axis=-1, keepdims=True)
                o = jnp.dot(p.astype(CDT), v_ref[...], preferred_element_type=F32)
                outs.append(o / l)
                lses.append(jnp.broadcast_to(mx + jnp.log(l), (tq, LANES)))
            o_ref[rows, :] = jnp.where(masks[0], outs[0], outs[1]).astype(o_ref.dtype)
            lse_ref[rows, :] = jnp.where(masks[0], lses[0], lses[1])

    return pl.pallas_call(
        body, name=name, grid=(npairs, S // (nsub * tq)),
        in_specs=[pl.BlockSpec((nsub * tq, qw), lambda p, i: (i, p)),
                  pl.BlockSpec((S, qw), lambda p, i: (0, kblk(p))),
                  pl.BlockSpec((S, LANES), lambda p, i: (0, vblk(p)))],
        out_specs=[pl.BlockSpec((nsub * tq, LANES), lambda p, i: (i, p))] * 2,
        out_shape=[_sds((S, LANES * npairs), CDT), _sds((S, LANES * npairs), F32)],
        compiler_params=_cparams(2),
    )(q, k, v)


def _attn_bwd(q, k, v, do, o, lse, *, split, npairs, kblk, vblk, doblk, shared_kv, name):
    S = q.shape[0]
    qw = 256 if split else LANES
    tq = min(512, S)
    tkv = min(ATTN_KEY_CHUNK, S)
    nsub = ATTN_BWD_TILES_PER_STEP if S % (ATTN_BWD_TILES_PER_STEP * tq) == 0 else 1
    nkv = 1 if shared_kv else npairs

    def body(q_ref, k_ref, v_ref, do_ref, o_ref, lse_ref, dq_ref, dk_ref, dv_ref):
        masks = _pair_masks()
        p_id, i_id = pl.program_id(0), pl.program_id(1)
        first = (i_id == 0) & ((p_id == 0) if shared_kv else True)

        @pl.when(first)
        def _():
            dk_ref[...] = jnp.zeros_like(dk_ref)
            dv_ref[...] = jnp.zeros_like(dv_ref)

        for j in range(nsub):
            qrows = slice(j * tq, (j + 1) * tq)
            do = do_ref[qrows, :]
            o = o_ref[qrows, :].astype(F32)
            lse = lse_ref[qrows, :]
            heads = []
            for hd in range(2):
                m = masks[hd]
                cols = slice(hd * LANES, (hd + 1) * LANES) if split else slice(None)
                qh = q_ref[qrows, cols] if split else jnp.where(m, q_ref[qrows, :], jnp.zeros_like(q_ref[qrows, :]))
                doh = jnp.where(m, do, 0.0)
                heads.append((m, cols, qh, doh.astype(CDT), _head_scalar(lse, m),
                              jnp.sum(doh * o, axis=-1, keepdims=True)))
            dqs = [jnp.zeros((tq, LANES), F32), jnp.zeros((tq, LANES), F32)]
            for ck in range(S // tkv):
                rows = slice(ck * tkv, (ck + 1) * tkv)
                v = v_ref[rows, :]
                dv = jnp.zeros((tkv, LANES), F32)
                for hd, (m, cols, qh, dohb, lse_h, delta) in enumerate(heads):
                    kh = k_ref[rows, cols]
                    s = lax.dot_general(qh, kh, _NT, preferred_element_type=F32)
                    p = jnp.exp(s - lse_h)
                    dp = lax.dot_general(dohb, v, _NT, preferred_element_type=F32)
                    ds = (p * (dp - delta)).astype(CDT)
                    dq = jnp.dot(ds, kh, preferred_element_type=F32)
                    dqs[hd] = dqs[hd] + (dq if split else jnp.where(m, dq, 0.0))
                    dk_ref[rows, cols] += lax.dot_general(ds, qh, _TN, preferred_element_type=F32)
                    dv = dv + lax.dot_general(p.astype(CDT), dohb, _TN, preferred_element_type=F32)
                dv_ref[rows, :] += dv
            if split:
                dq_ref[qrows, 0:LANES] = dqs[0]
                dq_ref[qrows, LANES:2 * LANES] = dqs[1]
            else:
                dq_ref[qrows, :] = dqs[0] + dqs[1]

    kvo = (lambda p, i: (0, 0)) if shared_kv else (lambda p, i: (0, p))
    return pl.pallas_call(
        body, name=name, grid=(npairs, S // (nsub * tq)),
        in_specs=[pl.BlockSpec((nsub * tq, qw), lambda p, i: (i, p)),
                  pl.BlockSpec((S, qw), lambda p, i: (0, kblk(p))),
                  pl.BlockSpec((S, LANES), lambda p, i: (0, vblk(p))),
                  pl.BlockSpec((nsub * tq, LANES), lambda p, i: (i, doblk(p))),
                  pl.BlockSpec((nsub * tq, LANES), lambda p, i: (i, p)),
                  pl.BlockSpec((nsub * tq, LANES), lambda p, i: (i, p))],
        out_specs=[pl.BlockSpec((nsub * tq, qw), lambda p, i: (i, p)),
                   pl.BlockSpec((S, qw), kvo), pl.BlockSpec((S, LANES), kvo)],
        out_shape=[_sds((S, qw * npairs), F32), _sds((S, qw * nkv), F32), _sds((S, LANES * nkv), F32)],
        compiler_params=_cparams(2),
    )(q, k, v, do, o, lse)


def _bias_expand(idx, rel_bias, name):
    tq, kw = idx.shape

    def body(idx_ref, rb_ref, o_ref):
        idx = idx_ref[...]
        for hd in range(DIL_HEADS):
            acc = jnp.full((tq, kw), NEG_INF, F32)
            for u in range(REL_BUCKETS):
                acc = jnp.where(idx == u, rb_ref[u, hd], acc)
            o_ref[hd] = acc

    return pl.pallas_call(
        body, name=name,
        in_specs=[pl.BlockSpec(memory_space=pltpu.VMEM), pl.BlockSpec(memory_space=pltpu.SMEM)],
        out_specs=pl.BlockSpec(memory_space=pltpu.VMEM),
        out_shape=_sds((DIL_HEADS, tq, kw), F32),
    )(idx, rel_bias)


def _bias_reduce(idx, dtab, name):
    tq, kw = idx.shape

    def body(idx_ref, d_ref, o_ref):
        idx = idx_ref[...]
        rowid = lax.broadcasted_iota(jnp.int32, (REL_BUCKETS, kw), 0)
        for hd in range(DIL_HEADS):
            d = d_ref[hd]
            acc = jnp.zeros((REL_BUCKETS, kw), F32)
            for u in range(REL_BUCKETS):
                r = jnp.sum(jnp.where(idx == u, d, 0.0), axis=0, keepdims=True)
                acc = jnp.where(rowid == u, r, acc)
            o_ref[hd] = jnp.sum(acc, axis=1, keepdims=True)

    return pl.pallas_call(
        body, name=name,
        in_specs=[pl.BlockSpec(memory_space=pltpu.VMEM)] * 2, out_specs=pl.BlockSpec(memory_space=pltpu.VMEM),
        out_shape=_sds((DIL_HEADS, REL_BUCKETS, 1), F32),
    )(idx, dtab)


DIL_TILES_PER_STEP = 4


def _dil_tiles_per_step(L, tq):
    return DIL_TILES_PER_STEP if L % (DIL_TILES_PER_STEP * tq) == 0 else 1


def _dil_window(i, tq, kw, L):
    start = pl.multiple_of(i * tq, DIL_HALF)
    key = start + lax.broadcasted_iota(jnp.int32, (1, kw), 1) - DIL_HALF
    return start, (key >= 0) & (key < L)


def _dil_fwd(qv, kv, vv, tab, *, dil, L, tq, name):
    kw = tq + 2 * DIL_HALF
    npair = DIL_HEADS // 2
    nsub = _dil_tiles_per_step(L, tq)

    def body(q_ref, k_ref, v_ref, t_ref, o_ref, lse_ref):
        masks = _pair_masks()
        for j in range(nsub):
            rows = slice(j * tq, (j + 1) * tq)
            start, valid = _dil_window(pl.program_id(2) * nsub + j, tq, kw, L)
            kwin = k_ref[pl.ds(start, kw), :]
            vwin = v_ref[pl.ds(start, kw), :]
            q = q_ref[rows, :]
            outs, lses = [], []
            for hd in range(2):
                qh = jnp.where(masks[hd], q, jnp.zeros_like(q))
                s = lax.dot_general(qh, kwin, _NT, preferred_element_type=F32) + t_ref[hd]
                s = jnp.where(valid, s, NEG_INF)
                mx = jnp.max(s, axis=-1, keepdims=True)
                p = jnp.exp(s - mx)
                l = jnp.sum(p, axis=-1, keepdims=True)
                outs.append(jnp.dot(p.astype(CDT), vwin, preferred_element_type=F32) / l)
                lses.append(jnp.broadcast_to(mx + jnp.log(l), (tq, LANES)))
            o_ref[rows, :] = jnp.where(masks[0], outs[0], outs[1])
            lse_ref[rows, :] = jnp.where(masks[0], lses[0], lses[1])

    blk = pl.BlockSpec((None, nsub * tq, LANES), lambda p, c, i: (c, i, p))
    res = pl.BlockSpec((None, L + 2 * DIL_HALF, LANES), lambda p, c, i: (c, 0, p))
    return pl.pallas_call(
        body, name=name, grid=(npair, dil, L // (nsub * tq)),
        in_specs=[blk, res, res, pl.BlockSpec((2, tq, kw), lambda p, c, i: (p, 0, 0))],
        out_specs=[blk] * 2, out_shape=[_sds(qv.shape, F32)] * 2,
        compiler_params=_cparams(3),
    )(qv, kv, vv, tab)


def _dil_bwd(qv, kv, vv, tab, dov, lsev, deltav, *, dil, L, tq, name):
    kw = tq + 2 * DIL_HALF
    npair = DIL_HEADS // 2
    nsub = _dil_tiles_per_step(L, tq)

    def body(q_ref, k_ref, v_ref, t_ref, do_ref, lse_ref, dl_ref, dq_ref, dk_ref, dv_ref, dt_ref):
        masks = _pair_masks()
        c_id, i_id = pl.program_id(1), pl.program_id(2)

        @pl.when(i_id == 0)
        def _():
            dk_ref[...] = jnp.zeros_like(dk_ref)
            dv_ref[...] = jnp.zeros_like(dv_ref)

        @pl.when((i_id == 0) & (c_id == 0))
        def _():
            dt_ref[...] = jnp.zeros_like(dt_ref)

        dts = [jnp.zeros((tq, kw), F32), jnp.zeros((tq, kw), F32)]
        for j in range(nsub):
            rows = slice(j * tq, (j + 1) * tq)
            start, valid = _dil_window(i_id * nsub + j, tq, kw, L)
            kwin = k_ref[pl.ds(start, kw), :]
            vwin = v_ref[pl.ds(start, kw), :]
            q, do, lse, dl = q_ref[rows, :], do_ref[rows, :], lse_ref[rows, :], dl_ref[rows, :]
            dq = jnp.zeros((tq, LANES), F32)
            dk = jnp.zeros((kw, LANES), F32)
            dv = jnp.zeros((kw, LANES), F32)
            for hd in range(2):
                m = masks[hd]
                qh = jnp.where(m, q, jnp.zeros_like(q))
                doh = jnp.where(m, do, jnp.zeros_like(do))
                s = lax.dot_general(qh, kwin, _NT, preferred_element_type=F32) + t_ref[hd]
                s = jnp.where(valid, s, NEG_INF)
                p = jnp.exp(s - _head_scalar(lse, m))
                dp = lax.dot_general(doh, vwin, _NT, preferred_element_type=F32)
                ds = p * (dp - _head_scalar(dl, m))
                dts[hd] = dts[hd] + ds
                dsb = ds.astype(CDT)
                dq = dq + jnp.where(m, jnp.dot(dsb, kwin, preferred_element_type=F32), 0.0)
                dk = dk + lax.dot_general(dsb, qh, _TN, preferred_element_type=F32)
                dv = dv + lax.dot_general(p.astype(CDT), doh, _TN, preferred_element_type=F32)
            dq_ref[rows, :] = dq
            dk_ref[pl.ds(start, kw), :] += dk
            dv_ref[pl.ds(start, kw), :] += dv
        for hd in range(2):
            dt_ref[hd] += dts[hd]

    blk = pl.BlockSpec((None, nsub * tq, LANES), lambda p, c, i: (c, i, p))
    res = pl.BlockSpec((None, L + 2 * DIL_HALF, LANES), lambda p, c, i: (c, 0, p))
    tsp = pl.BlockSpec((2, tq, kw), lambda p, c, i: (p, 0, 0))
    return pl.pallas_call(
        body, name=name, grid=(npair, dil, L // (nsub * tq)),
        in_specs=[blk, res, res, tsp, blk, blk, blk], out_specs=[blk, res, res, tsp],
        out_shape=[_sds(qv.shape, F32), _sds(kv.shape, F32), _sds(kv.shape, F32), _sds(tab.shape, F32)],
        compiler_params=_cparams(3),
    )(qv, kv, vv, tab, dov, lsev, deltav)


def _mix_weights(l1, l2, l3):
    mx = jnp.maximum(jnp.maximum(l1, l2), l3)
    e1, e2, e3 = jnp.exp(l1 - mx), jnp.exp(l2 - mx), jnp.exp(l3 - mx)
    inv = 1.0 / (e1 + e2 + e3)
    return e1 * inv, e2 * inv, e3 * inv


def _branch_specs(S, ts):
    dils = [d for _, d in DIL_BRANCHES]
    return dils, [_res_spec(d, ts // d) for d in dils], [(d, S // d, DIL_W) for d in dils]


def _dil_mix_fwd(os, ls):
    S = os[0].shape[0] * os[0].shape[1]
    ts = min(512, S)
    dils, specs, _ = _branch_specs(S, ts)

    def body(o1, o2, o3, l1, l2, l3, out, scr):
        o1, o2, o3, l1, l2, l3 = [_by_token(r, scr, d) for r, d in zip((o1, o2, o3, l1, l2, l3), dils + dils)]
        w1, w2, w3 = _mix_weights(l1, l2, l3)
        out[...] = (w1 * o1 + w2 * o2 + w3 * o3).astype(CDT)

    return pl.pallas_call(
        body, name="dil_mix_fwd", grid=(S // ts,), in_specs=specs + specs, out_specs=_row(ts, DIL_W),
        out_shape=_sds((S, DIL_W), CDT), scratch_shapes=[_TOKEN_SCRATCH(ts)],
        compiler_params=_cparams(1),
    )(*os, *ls)


def _dil_mix_bwd(dcat, os, ls, j384):
    S = os[0].shape[0] * os[0].shape[1]
    ts = min(512, S)
    dils, specs, shapes = _branch_specs(S, ts)

    def body(do_ref, o1, o2, o3, l1, l2, l3, j_ref, d1, d2, d3, e1, e2, e3, scr):
        o1, o2, o3, l1, l2, l3 = [_by_token(r, scr, d) for r, d in zip((o1, o2, o3, l1, l2, l3), dils + dils)]
        ws = _mix_weights(l1, l2, l3)
        do = do_ref[...]
        o = ws[0] * o1 + ws[1] * o2 + ws[2] * o3
        dot = _headsum(do * o, j_ref[...])
        for w, d, d_o, e_o in zip(ws, dils, (d1, d2, d3), (e1, e2, e3)):
            _by_residue(w * do, scr, d_o, d)
            _by_residue(w * dot, scr, e_o, d)

    return pl.pallas_call(
        body, name="dil_mix_bwd", grid=(S // ts,),
        in_specs=[_row(ts, DIL_W, 1)] + specs + specs + [_full(j384.shape)],
        out_specs=specs + specs,
        out_shape=[_sds(s, CDT) for s in shapes] + [_sds(s, F32) for s in shapes],
        scratch_shapes=[_TOKEN_SCRATCH(ts)],
        compiler_params=_cparams(1),
    )(dcat, *os, *ls, j384)


def _adamw_math(w, g, m, v):
    m = ADAM_B1 * m + (1.0 - ADAM_B1) * g
    v = ADAM_B2 * v + (1.0 - ADAM_B2) * (g * g)
    m_hat = m / (1.0 - ADAM_B1 ** ADAM_STEP)
    v_hat = v / (1.0 - ADAM_B2 ** ADAM_STEP)
    delta = -ADAM_LR * (m_hat / (jnp.sqrt(v_hat) + ADAM_EPS) + ADAM_WD * w)
    return delta, m, v


def _pick8(n, target):
    best = None
    for t in range(16, min(n, target) + 1, 16):
        if n % t == 0:
            best = t
    return best if best is not None else n


_ELEMS_PER_BLOCK = 256 * 1024


def _adamw(w, reds, sibs, m, v, owner, name):
    L, a, b = w.shape
    ta = _pick8(a, max(16, _ELEMS_PER_BLOCK // b))
    spec = pl.BlockSpec((None, ta, b), lambda l, i, own: (l, i, 0))
    def gspec(mine, layer):
        def index(l, i, own):
            use = (own[0] if mine else 1 - own[0]) * (l if layer else 1 - l)
            return i * use, 0
        return pl.BlockSpec((ta, b), index)

    def body(own_ref, w_ref, r0_ref, r1_ref, s0_ref, s1_ref, m_ref, v_ref, g_o, d_o, m_o, v_o):
        mine = own_ref[0] == 1
        g0 = jnp.where(mine, r0_ref[...], s0_ref[...])
        g1 = jnp.where(mine, r1_ref[...], s1_ref[...])
        g = jnp.where(pl.program_id(0) == 0, g0, g1)
        d, mm, vv = _adamw_math(w_ref[...], g, m_ref[...], v_ref[...])
        g_o[...] = g
        d_o[...] = d
        m_o[...] = mm
        v_o[...] = vv

    return pl.pallas_call(
        body, name=name, out_shape=[_sds(w.shape, F32)] * 4,
        grid_spec=pltpu.PrefetchScalarGridSpec(
            num_scalar_prefetch=1, grid=(L, a // ta),
            in_specs=[spec, gspec(True, 0), gspec(True, 1), gspec(False, 0), gspec(False, 1), spec, spec],
            out_specs=[spec] * 4),
        compiler_params=_cparams(2),
    )(_is_core(owner), w, *reds, *sibs, m, v)


def _adamw_small(w, gall, m, v):
    R = w.shape[0]

    def body(w_ref, g_ref, m_ref, v_ref, g_o, d_o, m_o, v_o):
        g = g_ref[0]
        for k in range(1, 8):
            g = g + g_ref[k]
        d, mm, vv = _adamw_math(w_ref[...], g, m_ref[...], v_ref[...])
        g_o[...] = g
        d_o[...] = d
        m_o[...] = mm
        v_o[...] = vv

    vm = pl.BlockSpec(memory_space=pltpu.VMEM)
    return pl.pallas_call(
        body, name="adamw_small", in_specs=[vm] * 4, out_specs=[vm] * 4, out_shape=[_sds((R, LANES), F32)] * 4,
    )(w, gall, m, v)


def _sum_pair(g, t, owner, name):
    n, a, b = t.shape
    ta = _pick8(a, max(16, _ELEMS_PER_BLOCK // b))
    spec = pl.BlockSpec((None, ta, b), lambda k, i, own: (k * own[0], i * own[0], 0))

    def body(own_ref, g_ref, t_ref, o_ref):
        @pl.when(own_ref[0] == 1)
        def _():
            o_ref[...] = (g_ref[...].astype(F32) + t_ref[...].astype(F32)).astype(o_ref.dtype)

    return pl.pallas_call(
        body, name=name, out_shape=_sds(t.shape, WIRE),
        grid_spec=pltpu.PrefetchScalarGridSpec(num_scalar_prefetch=1, grid=(n, a // ta), in_specs=[spec] * 2,
                                               out_specs=spec),
        compiler_params=_cparams(2),
    )(_is_core(owner), g, t)


def _sum_chips(pair, t, owner, name):
    _, a, b = t.shape
    ta = _pick8(a, max(16, _ELEMS_PER_BLOCK // b))

    def body(own_ref, p_ref, t_ref, o_ref):
        @pl.when(own_ref[0] == 1)
        def _():
            me = 2 * lax.axis_index("x") + lax.axis_index("y")
            acc = p_ref[me].astype(F32)
            for k in range(3):
                acc = acc + t_ref[k].astype(F32)
            o_ref[...] = acc

    return pl.pallas_call(
        body, name=name, out_shape=_sds((a, b), F32),
        grid_spec=pltpu.PrefetchScalarGridSpec(
            num_scalar_prefetch=1, grid=(a // ta,),
            in_specs=[pl.BlockSpec((4, ta, b), lambda i, own: (0, i * own[0], 0)),
                      pl.BlockSpec((3, ta, b), lambda i, own: (0, i * own[0], 0))],
            out_specs=pl.BlockSpec((ta, b), lambda i, own: (i * own[0], 0))),
        compiler_params=_cparams(1),
    )(_is_core(owner), pair, t)


def _is_core(core):
    return (lax.axis_index("c") == core).astype(jnp.int32).reshape(1)


_HBM = pl.BlockSpec(memory_space=pltpu.HBM)


def _place():
    x, y, c = lax.axis_index("x"), lax.axis_index("y"), lax.axis_index("c")
    chips = [(1 - x, y), (x, 1 - y), (1 - x, 1 - y)]
    return x, y, c, chips


def _remote(src, dst, ssem, rsem, to):
    return pltpu.make_async_remote_copy(src_ref=src, dst_ref=dst, send_sem=ssem, recv_sem=rsem, device_id=to,
                                        device_id_type=MESH_ID)


def _dma_sems(n):
    return pltpu.SemaphoreType.DMA((n,))


_SEM = pl.BlockSpec(memory_space=pltpu.SEMAPHORE)
_ANY = pl.BlockSpec(memory_space=pl.ANY)
_EFFECT = pltpu.SideEffectType.DATAFLOW_SIDE_EFFECTING
_BIG = ("w_in", "mla_w_uq", "mla_w_ukv", "w_out", "ffn_w_gate", "ffn_w_up", "ffn_w_down")
_OWNER = dict(zip(_BIG, (1, 0, 0, 1, 0, 0, 1)))
_ATTN_WEIGHTS, _FFN_WEIGHTS = _BIG[:4], _BIG[4:]


def _hbm(a):
    return pltpu.with_memory_space_constraint(a, pltpu.HBM)


def _per_core(c, owners, fn):
    for g in range(2):
        mine = tuple(p for p, o in enumerate(owners) if o == g)
        theirs = tuple(p for p, o in enumerate(owners) if o != g)
        pl.when(c == g)(functools.partial(fn, mine, theirs))


def _token_spec():
    return pl.BlockSpec(memory_space=pltpu.VMEM), _sds((8, LANES), F32)


def _gather_start(shards, owners, layer, tag):
    n = len(shards)
    lands = [_hbm(lax.empty((4,) + s.shape[1:], s.dtype)) for s in shards]

    def body(*refs):
        w_refs, l_refs = refs[:n], refs[n:2 * n]
        ssem, rsem, token = refs[2 * n], refs[2 * n + 1], refs[-1]
        x, y, c, chips = _place()
        me = 2 * x + y

        def send(mine, _):
            for i in mine:
                for j, (cx, cy) in enumerate(chips):
                    _remote(w_refs[i].at[layer], l_refs[i].at[me], ssem.at[3 * i + j], rsem.at[3 * i + j],
                            (cx, cy, c)).start()

        _per_core(c, owners, send)
        token[...] = jnp.zeros_like(token)

    tspec, tshape = _token_spec()
    out = pl.pallas_call(
        body, name=f"gather_start_{tag}", in_specs=[_HBM] * (2 * n),
        out_specs=[_SEM, _SEM] + [_HBM] * n + [tspec],
        out_shape=[_dma_sems(3 * n), _dma_sems(3 * n)] + [pltpu.HBM(l.shape, l.dtype) for l in lands] + [tshape],
        input_output_aliases={n + i: 2 + i for i in range(n)},
        compiler_params=pltpu.CompilerParams(has_side_effects=_EFFECT),
    )(*[_hbm(s) for s in shards], *lands)
    return out[0], out[1], list(out[2:2 + n]), out[-1]


def _gather_wait(ssem, rsem, shards, lands, after, owners, layer, tag):
    n = len(shards)

    def body(*refs):
        w_refs, l_refs = refs[:n], refs[n:2 * n]
        ssem, rsem = refs[2 * n], refs[2 * n + 1]
        x, y, c, chips = _place()

        def wait(mine, _):
            for i in mine:
                for j, (cx, cy) in enumerate(chips):
                    cp = _remote(w_refs[i].at[layer], l_refs[i].at[2 * cx + cy], ssem.at[3 * i + j], rsem.at[3 * i + j],
                                 (cx, cy, c))
                    cp.wait_send()
                    cp.wait_recv()

        _per_core(c, owners, wait)

    return list(pl.pallas_call(
        body, name=f"gather_wait_{tag}", in_specs=[_HBM] * (2 * n) + [_SEM, _SEM, _ANY], out_specs=[_HBM] * n,
        out_shape=[pltpu.HBM(l.shape, l.dtype) for l in lands],
        input_output_aliases={n + i: i for i in range(n)},
        compiler_params=pltpu.CompilerParams(has_side_effects=_EFFECT),
    )(*[_hbm(s) for s in shards], *lands, ssem, rsem, after))


def _gather_finish(shards, lands, owners, layer, tag):
    n = len(shards)

    def body(*refs):
        w_refs, g_refs = refs[:n], refs[2 * n:3 * n]
        ssem, rsem = refs[3 * n:]
        x, y, c, chips = _place()
        me = 2 * x + y
        sib = (x, y, 1 - c)
        owns = [_remote(w.at[layer], g.at[me], ssem.at[i], rsem.at[i], sib) for i, (w, g) in enumerate(zip(w_refs, g_refs))]
        for cp in owns:
            cp.start()

        def forward(mine, theirs):
            def blk(i, j):
                b = g_refs[i].at[2 * chips[j][0] + chips[j][1]]
                return _remote(b, b, ssem.at[n + 3 * i + j], rsem.at[n + 3 * i + j], sib)

            for i in mine:
                for j in range(3):
                    blk(i, j).start()
            for i in theirs:
                for j in range(3):
                    blk(i, j).wait_recv()
            for i in mine:
                for j in range(3):
                    blk(i, j).wait_send()

        _per_core(c, owners, forward)
        for cp in owns:
            cp.wait_recv()
            cp.wait_send()

    return list(pl.pallas_call(
        body, name=f"gather_finish_{tag}", in_specs=[_HBM] * (2 * n), out_specs=[_HBM] * n,
        out_shape=[_sds(l.shape, l.dtype) for l in lands], input_output_aliases={n + i: i for i in range(n)},
        scratch_shapes=[_dma_sems(4 * n), _dma_sems(4 * n)],
    )(*shards, *lands))


def _rs_to_owner(grads, owners, tag):
    n = len(grads)

    def body(*refs):
        g_refs, t_refs = refs[:n], refs[n:2 * n]
        ssem, rsem = refs[2 * n:]
        x, y, c, _ = _place()

        def swap(mine, theirs):
            cps = [_remote(g_refs[i], t_refs[i], ssem.at[i], rsem.at[i], (x, y, 1 - c)) for i in theirs]
            for cp in cps:
                cp.start()
            for i in mine:
                _remote(g_refs[i], t_refs[i], ssem.at[i], rsem.at[i], (x, y, 1 - c)).wait_recv()
            for cp in cps:
                cp.wait_send()

        _per_core(c, owners, swap)

    return list(pl.pallas_call(
        body, name=f"rs_to_owner_{tag}", in_specs=[_HBM] * n, out_specs=[_HBM] * n,
        out_shape=[_sds(g.shape, g.dtype) for g in grads], scratch_shapes=[_dma_sems(n), _dma_sems(n)],
    )(*grads))


def _a2a_start(pairs, owners, tag):
    n = len(pairs)
    lands = [_hbm(lax.empty((3,) + p.shape[1:], p.dtype)) for p in pairs]

    def body(*refs):
        a_refs, t_refs = refs[:n], refs[n:2 * n]
        ssem, rsem, token = refs[2 * n], refs[2 * n + 1], refs[-1]
        x, y, c, chips = _place()

        def send(mine, _):
            for i in mine:
                for j, (cx, cy) in enumerate(chips):
                    _remote(a_refs[i].at[2 * cx + cy], t_refs[i].at[j], ssem.at[3 * i + j], rsem.at[3 * i + j],
                            (cx, cy, c)).start()

        _per_core(c, owners, send)
        token[...] = jnp.zeros_like(token)

    tspec, tshape = _token_spec()
    out = pl.pallas_call(
        body, name=f"rs_a2a_start_{tag}", in_specs=[_HBM] * (2 * n),
        out_specs=[_SEM, _SEM] + [_HBM] * n + [tspec],
        out_shape=[_dma_sems(3 * n), _dma_sems(3 * n)] + [pltpu.HBM(l.shape, l.dtype) for l in lands] + [tshape],
        input_output_aliases={n + i: 2 + i for i in range(n)},
        compiler_params=pltpu.CompilerParams(has_side_effects=_EFFECT),
    )(*[_hbm(p) for p in pairs], *lands)
    return out[0], out[1], list(out[2:2 + n]), out[-1]


def _a2a_wait(ssem, rsem, pairs, lands, after, owners, tag):
    n = len(pairs)

    def body(*refs):
        a_refs, t_refs = refs[:n], refs[n:2 * n]
        ssem, rsem = refs[2 * n], refs[2 * n + 1]
        x, y, c, chips = _place()

        def wait(mine, _):
            for i in mine:
                for j, (cx, cy) in enumerate(chips):
                    cp = _remote(a_refs[i].at[2 * cx + cy], t_refs[i].at[j], ssem.at[3 * i + j], rsem.at[3 * i + j],
                                 (cx, cy, c))
                    cp.wait_send()
                    cp.wait_recv()

        _per_core(c, owners, wait)

    return list(pl.pallas_call(
        body, name=f"rs_a2a_wait_{tag}", in_specs=[_HBM] * (2 * n) + [_SEM, _SEM, _ANY], out_specs=[_HBM] * n,
        out_shape=[pltpu.HBM(l.shape, l.dtype) for l in lands],
        input_output_aliases={n + i: i for i in range(n)},
        compiler_params=pltpu.CompilerParams(has_side_effects=_EFFECT),
    )(*[_hbm(p) for p in pairs], *lands, ssem, rsem, after))


def _rs_from_owner(reds, owners, small):
    n = len(reds)

    def body(*refs):
        q_refs, s_ref, o_refs, a_ref = refs[:n], refs[n], refs[n + 1:2 * n + 1], refs[2 * n + 1]
        ssem, rsem, lsem = refs[2 * n + 2:]
        x, y, c, _ = _place()
        me = 4 * x + 2 * y + c
        peers = [(x ^ (k >> 2), y ^ ((k >> 1) & 1), c ^ (k & 1)) for k in range(1, 8)]
        own = pltpu.make_async_copy(s_ref, a_ref.at[me], lsem)
        own.start()
        alls = [_remote(s_ref, a_ref.at[me], ssem.at[n + k], rsem.at[n + k], peer) for k, peer in enumerate(peers)]
        for cp in alls:
            cp.start()

        def swap(mine, theirs):
            cps = [_remote(q_refs[k], o_refs[k], ssem.at[k], rsem.at[k], (x, y, 1 - c)) for k in mine]
            for cp in cps:
                cp.start()
            for k in theirs:
                _remote(q_refs[k], o_refs[k], ssem.at[k], rsem.at[k], (x, y, 1 - c)).wait_recv()
            for cp in cps:
                cp.wait_send()

        _per_core(c, owners, swap)
        for k, (px, py, pc) in enumerate(peers):
            blk = a_ref.at[4 * px + 2 * py + pc]
            _remote(blk, blk, ssem.at[n + k], rsem.at[n + k], (px, py, pc)).wait_recv()
        for cp in alls:
            cp.wait_send()
        own.wait()

    out = pl.pallas_call(
        body, name="rs_from_owner", in_specs=[_HBM] * (n + 1), out_specs=[_HBM] * (n + 1),
        out_shape=[_sds(q.shape, q.dtype) for q in reds] + [_sds((8,) + small.shape, small.dtype)],
        scratch_shapes=[_dma_sems(n + 7), _dma_sems(n + 7), pltpu.SemaphoreType.DMA],
    )(*reds, small)
    return list(out[:n]), out[n]


_COL_SHARDED =("w_in", "mla_w_uq", "mla_w_ukv", "ffn_w_gate", "ffn_w_up")
_SMALL = ("mla_q_norm", "mla_kv_norm", "gqa_q_norm", "gqa_k_norm", "rel_bias", "ln1_g", "ln1_b", "ln2_g", "ln2_b")


def _pack_flat(arrs, align):
    flat = jnp.concatenate([a.reshape(-1) for a in arrs])
    pad = (-flat.shape[0]) % align
    return jnp.pad(flat, (0, pad)) if pad else flat


def _unpack_flat(flat, shapes):
    out, off = [], 0
    for s in shapes:
        n = int(np.prod(s))
        out.append(flat[off:off + n].reshape(s))
        off += n
    return out


def _perm_gqa_rows(w):
    return jnp.concatenate([w[:832], w[896:960], w[832:896], w[960:]], axis=0)


def _local_step(x, target, small, depth, weights_of_layer, grads_done):
    S, D = x.shape
    alpha = (2.0 * depth) ** 0.25
    in_idx, uq_idx, ukv_idx = _in_cols(), _uq_cols(), _ukv_cols()
    win, wuq, wukv, wout, wg, wu, wdn = ([None] * depth for _ in range(7))

    tm, tg = _rope_tables(S)
    j256, j384 = _head_ones(256), _head_ones(384)
    mla_scale = (64 + MLA_ROPE_DIM) ** -0.5
    branches = []
    for (_, dil) in DIL_BRANCHES:
        L = S // dil
        tq = min(256, L)
        idx = jnp.asarray(_branch_bucket_idx(tq, dil))
        branches.append((dil, L, tq, idx))
    tabs = [_bias_expand(idx, small["rel_bias"], name=f"bias_expand_{b}") for b, (_, _, _, idx) in enumerate(branches)]

    def padded(a):
        z = jnp.zeros((DIL_HALF, a.shape[1]), a.dtype)
        return jnp.concatenate([z, a, z], axis=0)[None]

    saved = []
    xf, xb = x, x.astype(CDT)
    for l in range(depth):
        W, token = weights_of_layer(l, "attn", xb)
        win[l] = _rows_from_shards(W["w_in"], in_idx)
        wuq[l] = _rows_from_shards(W["mla_w_uq"], uq_idx)
        wukv[l] = _rows_from_shards(W["mla_w_ukv"], ukv_idx)
        wout[l] = _perm_gqa_rows(W["w_out"].reshape(-1, D))
        gq, gkv = small["mla_q_norm"][l][None], small["mla_kv_norm"][l][None]
        if token is not None:
            gq = gq + token[0, 0]
        ggq = jnp.tile(small["gqa_q_norm"][l], 4)[None]
        ggk = jnp.tile(small["gqa_k_norm"][l], 2)[None]
        h = _mm(xb, win[l], tb=True, name="mm_in")
        cq, ckv, kr, qd, kd, vd, qg, kg, vg, *strided = _prep_fwd(h, gq, gkv, ggq, ggk, tm, tg, j256)
        qkv = [(qd[None], padded(kd), padded(vd))] + [tuple(strided[3 * b:3 * b + 3]) for b in range(len(DIL_STRIDES))]
        qa = _mm(cq, wuq[l], tb=True, name="mm_uq")
        kvp = _mm(ckv, wukv[l], tb=True, out_dtype=CDT, name="mm_ukv")
        qm, km = _mla_prep_fwd(qa, kvp, kr, tm, mla_scale)
        oa, lsa = _attn_fwd(qm, km, kvp, split=True, npairs=3, kblk=lambda p: p, vblk=lambda p: 6 + p,
                            name="mla_attn_fwd")
        oc, lsc = _attn_fwd(qg, kg, vg, split=False, npairs=2, kblk=lambda p: 0, vblk=lambda p: 0,
                            name="gqa_attn_fwd")
        obs, lbs = [], []
        for b, (dil, L, tq, _) in enumerate(branches):
            o_b, l_b = _dil_fwd(*qkv[b], tabs[b], dil=dil, L=L, tq=tq, name=f"dil_fwd_{b}")
            obs.append(o_b)
            lbs.append(l_b)
        ob = _dil_mix_fwd(obs, lbs)
        cat = jnp.concatenate([oa, ob, oc], axis=1)
        mix = _mm(cat, wout[l], name="mm_out")
        x1, x1b, z1 = _ln_fwd(xf, mix, small["ln1_g"][l][None], small["ln1_b"][l][None], alpha, name="ln1_fwd")
        W, _ = weights_of_layer(l, "ffn", x1b)
        wg[l], wu[l], wdn[l] = W["ffn_w_gate"], W["ffn_w_up"], W["ffn_w_down"]
        g3, u3, act = _ffn_up(x1b, wg[l], wu[l])
        ff = _ffn_down(act, wdn[l])
        x2, x2b, z2 = _ln_fwd(x1, ff, small["ln2_g"][l][None], small["ln2_b"][l][None], alpha, name="ln2_fwd")
        saved.append(dict(xb=xb, h=h, cq=cq, ckv=ckv, qg=qg, kg=kg, vg=vg, kvp=kvp, qm=qm, km=km, oa=oa, lsa=lsa,
                          oc=oc, lsc=lsc, obs=obs, lbs=lbs, qkv=qkv, cat=cat, z1=z1, x1b=x1b, g3=g3, u3=u3, act=act, z2=z2,
                          gq=gq, gkv=gkv, ggq=ggq, ggk=ggk))
        xf, xb = x2, x2b

    gW = {k: [None] * depth for k in _BIG}
    gS = {k: [None] * depth for k in ("mla_q_norm", "mla_kv_norm", "gqa_q_norm", "gqa_k_norm", "ln1_g", "ln1_b", "ln2_g",
                                      "ln2_b")}
    g_rel = None
    dya, dyb = xf, target
    token = None
    for l in reversed(range(depth)):
        sv = saved[l]
        ln2_g = small["ln2_g"][l][None]
        if token is not None:
            ln2_g = ln2_g + token[0, 0]
        if l == depth - 1:
            dz2, dz2b, gS["ln2_g"][l], gS["ln2_b"][l], loss = _ln_bwd(dya, dyb, sv["z2"], ln2_g, alpha,
                                                                       name="ln2_bwd_loss", loss_head=True)
        else:
            dz2, dz2b, gS["ln2_g"][l], gS["ln2_b"][l] = _ln_bwd(dya, dyb, sv["z2"], ln2_g, alpha, name="ln2_bwd")
        gW["ffn_w_down"][l] = _mm(sv["act"], dz2b, ta=True, ga=True, go=True, out_dtype=WIRE, name="mm_down_dw")
        dg3, du3 = _ffn_down_dx(dz2b, wdn[l], sv["g3"], sv["u3"])
        gW["ffn_w_gate"][l] = _mm(dg3, sv["x1b"], ta=True, ga=True, go=True, out_dtype=WIRE, name="mm_gate_dw")
        gW["ffn_w_up"][l] = _mm(du3, sv["x1b"], ta=True, ga=True, go=True, out_dtype=WIRE, name="mm_up_dw")
        dx1 = _ffn_up_dx(dg3, du3, wg[l], wu[l])
        token = grads_done(l, "ffn", {n: gW[n][l] for n in _FFN_WEIGHTS})
        ln1_g = small["ln1_g"][l][None]
        if token is not None:
            ln1_g = ln1_g + token[0, 0]
        dz1, dz1b, gS["ln1_g"][l], gS["ln1_b"][l] = _ln_bwd(dx1, dz2, sv["z1"], ln1_g, alpha, name="ln1_bwd")
        gW["w_out"][l] = _perm_gqa_rows(_mm(sv["cat"], dz1b, ta=True, out_dtype=WIRE, name="mm_out_dw")).reshape(4, -1, D)
        dcat = _mm(dz1b, wout[l], tb=True, name="mm_out_dx")
        dqg, dkg, dvg = _attn_bwd(sv["qg"], sv["kg"], sv["vg"], dcat, sv["oc"], sv["lsc"], split=False, npairs=2,
                                  kblk=lambda p: 0, vblk=lambda p: 0, doblk=lambda p: 6 + p, shared_kv=True,
                                  name="gqa_attn_bwd")
        dqm, dkm, dvm = _attn_bwd(sv["qm"], sv["km"], sv["kvp"], dcat, sv["oa"], sv["lsa"], split=True, npairs=3,
                                  kblk=lambda p: p, vblk=lambda p: 6 + p, doblk=lambda p: p, shared_kv=False,
                                  name="mla_attn_bwd")
        dqa, dkvp, dkr = _mla_prep_bwd(dqm, dkm, dvm, tm, mla_scale)
        gW["mla_w_uq"][l] = _rows_to_shards(_mm(dqa, sv["cq"], ta=True, out_dtype=WIRE, name="mm_uq_dw"), uq_idx, MLA_HEADS * 96)
        dcq = _mm(dqa, wuq[l], name="mm_uq_dx")
        gW["mla_w_ukv"][l] = _rows_to_shards(_mm(dkvp, sv["ckv"], ta=True, out_dtype=WIRE, name="mm_ukv_dw"), ukv_idx, MLA_HEADS * 128)
        dckv = _mm(dkvp, wukv[l], name="mm_ukv_dx")
        mixb = _dil_mix_bwd(dcat, sv["obs"], sv["lbs"], j384)
        ddq, ddk, ddv = [], [], []
        for b, (dil, L, tq, idx) in enumerate(branches):
            dq_b, dk_b, dv_b, dtab = _dil_bwd(*sv["qkv"][b], tabs[b], mixb[b], sv["lbs"][b], mixb[3 + b], dil=dil, L=L,
                                              tq=tq, name=f"dil_bwd_{b}")
            if dil == 1:
                dq_b, dk_b, dv_b = dq_b[0], dk_b[0, DIL_HALF:DIL_HALF + S], dv_b[0, DIL_HALF:DIL_HALF + S]
            ddq.append(dq_b)
            ddk.append(dk_b)
            ddv.append(dv_b)
            g_b = _bias_reduce(idx, dtab, name=f"bias_reduce_{b}")[:, :, 0].T
            g_rel = g_b if g_rel is None else g_rel + g_b
        dh, n1, n2, n3, n4 = _prep_bwd(sv["h"], dcq, dckv, dkr, ddq, ddk, ddv, dqg, dkg, dvg, sv["gq"], sv["gkv"],
                                       sv["ggq"], sv["ggk"], tg, j256)
        gS["mla_q_norm"][l], gS["mla_kv_norm"][l] = n1[0], n2[0]
        gS["gqa_q_norm"][l] = n3[0].reshape(4, 64).sum(0)
        gS["gqa_k_norm"][l] = n4[0].reshape(2, 64).sum(0)
        gW["w_in"][l] = _rows_to_shards(_mm(dh, sv["xb"], ta=True, out_dtype=WIRE, name="mm_in_dw"), in_idx, IN_W)
        dya = _mm(dh, win[l], name="mm_in_dx")
        dyb = dz1
        token = grads_done(l, "attn", {n: gW[n][l] for n in _ATTN_WEIGHTS})
    grad_x = _axpy(dya, dyb, alpha, name="grad_x")

    gsmall = {k: jnp.stack([a.reshape(-1) for a in v]) for k, v in gS.items()}
    gsmall["rel_bias"] = g_rel
    return loss, grad_x, gsmall


_ORDER = ("w_in", "mla_q_norm", "mla_kv_norm", "mla_w_uq", "mla_w_ukv", "gqa_q_norm", "gqa_k_norm", "rel_bias", "w_out",
          "ln1_g", "ln1_b", "ffn_w_gate", "ffn_w_up", "ffn_w_down", "ln2_g", "ln2_b")


def kernel(x, w_in, mla_q_norm, mla_kv_norm, mla_w_uq, mla_w_ukv, gqa_q_norm, gqa_k_norm, rel_bias, w_out, ln1_g, ln1_b, ffn_w_gate, ffn_w_up, ffn_w_down, ln2_g, ln2_b, loss_target, m_w_in, m_mla_q_norm, m_mla_kv_norm, m_mla_w_uq, m_mla_w_ukv, m_gqa_q_norm, m_gqa_k_norm, m_rel_bias, m_w_out, m_ln1_g, m_ln1_b, m_ffn_w_gate, m_ffn_w_up, m_ffn_w_down, m_ln2_g, m_ln2_b, v_w_in, v_mla_q_norm, v_mla_kv_norm, v_mla_w_uq, v_mla_w_ukv, v_gqa_q_norm, v_gqa_k_norm, v_rel_bias, v_w_out, v_ln1_g, v_ln1_b, v_ffn_w_gate, v_ffn_w_up, v_ffn_w_down, v_ln2_g, v_ln2_b):
    wts = dict(zip(_ORDER, (w_in, mla_q_norm, mla_kv_norm, mla_w_uq, mla_w_ukv, gqa_q_norm, gqa_k_norm, rel_bias, w_out,
                            ln1_g, ln1_b, ffn_w_gate, ffn_w_up, ffn_w_down, ln2_g, ln2_b)))
    mom = dict(zip(_ORDER, (m_w_in, m_mla_q_norm, m_mla_kv_norm, m_mla_w_uq, m_mla_w_ukv, m_gqa_q_norm, m_gqa_k_norm,
                            m_rel_bias, m_w_out, m_ln1_g, m_ln1_b, m_ffn_w_gate, m_ffn_w_up, m_ffn_w_down, m_ln2_g,
                            m_ln2_b)))
    var = dict(zip(_ORDER, (v_w_in, v_mla_q_norm, v_mla_kv_norm, v_mla_w_uq, v_mla_w_ukv, v_gqa_q_norm, v_gqa_k_norm,
                            v_rel_bias, v_w_out, v_ln1_g, v_ln1_b, v_ffn_w_gate, v_ffn_w_up, v_ffn_w_down, v_ln2_g,
                            v_ln2_b)))
    small_shapes = [wts[n].shape for n in _SMALL]
    for d in (wts, mom, var):
        for n in _COL_SHARDED:
            d[n] = d[n].transpose(0, 2, 1)

    depth = 2
    shards = {n: wts[n].astype(WIRE) for n in _BIG}
    flying = {}

    def start_gather(names, l, tag):
        own = tuple(_OWNER[n] for n in names)
        sh = [shards[n] for n in names]
        ssem, rsem, lands, token = _gather_start(sh, own, l, tag)
        return (names, own, sh, ssem, rsem, lands, l, tag), token

    def end_gather(flight, after):
        names, own, sh, ssem, rsem, lands, l, tag = flight
        got = _gather_finish(sh, _gather_wait(ssem, rsem, sh, lands, after, own, l, tag), own, l, tag)
        return {n: g.astype(CDT) for n, g in zip(names, got)}

    def weights_of_layer(l, part, after):
        if (l, part) == (0, "attn"):
            got = end_gather(start_gather(_ATTN_WEIGHTS, 0, "attn0")[0], after)
            flying["ffn0"], t0 = start_gather(_FFN_WEIGHTS, 0, "ffn0")
            flying["layer1"], t1 = start_gather(_BIG, 1, "layer1")
            return got, t0 + t1
        if (l, part) == (0, "ffn"):
            return end_gather(flying.pop("ffn0"), after), None
        if part == "attn":
            flying["w1"] = end_gather(flying.pop("layer1"), after)
        return flying["w1"], None

    def grads_done(l, part, grads):
        names = tuple(grads)
        own = tuple(_OWNER[n] for n in names)
        tag = f"{part}{l}"
        gl = [grads[n] for n in names]
        theirs = _rs_to_owner(gl, own, tag)
        pairs = [_sum_pair(g, t, o, name=f"rs_pair_sum_{n}") for n, g, t, o in zip(names, gl, theirs, own)]
        ssem, rsem, lands, token = _a2a_start(pairs, own, tag)
        flying[tag] = (names, own, ssem, rsem, pairs, lands)
        return token

    small = {n: wts[n] for n in _SMALL}
    loss, grad_x, gsmall = _local_step(x[0], loss_target[0], small, depth, weights_of_layer, grads_done)

    reds = {}
    for l in reversed(range(depth)):
        for part in ("ffn", "attn"):
            tag = f"{part}{l}"
            names, own, ssem, rsem, pairs, lands = flying.pop(tag)
            got = _a2a_wait(ssem, rsem, pairs, lands, grad_x, own, tag)
            for n, p, t, o in zip(names, pairs, got, own):
                reds[n, l] = _sum_chips(p, t, o, name=f"rs_sum_chips_{n}")
    sflat = _pack_flat([gsmall[n].reshape(-1) for n in _SMALL], 8 * LANES)
    rs = sflat.shape[0] // LANES
    order = [(n, l) for l in range(depth) for n in _BIG]
    got, sall = _rs_from_owner([reds[k] for k in order], tuple(_OWNER[n] for n, _ in order), sflat.reshape(rs, LANES))
    sibs = dict(zip(order, got))

    def packed(d):
        return _pack_flat([d[n] for n in _SMALL], 8 * LANES).reshape(rs, LANES)

    outs = {tag: {} for tag in ("grad", "delta", "new_m", "new_v")}
    for n in _BIG:
        res = _adamw(wts[n], [reds[n, 0], reds[n, 1]], [sibs[n, 0], sibs[n, 1]], mom[n], var[n], _OWNER[n],
                     name=f"adamw_{n}")
        for tag, r in zip(("grad", "delta", "new_m", "new_v"), res):
            outs[tag][n] = r.transpose(0, 2, 1) if n in _COL_SHARDED else r
    for tag, smallflat in zip(("grad", "delta", "new_m", "new_v"), _adamw_small(packed(wts), sall, packed(mom), packed(var))):
        outs[tag].update(zip(_SMALL, _unpack_flat(smallflat.reshape(-1), small_shapes)))

    total = lax.psum(loss[0, 0], ("x", "y", "c"))
    return (total, grad_x[None], *[outs["grad"][n] for n in _ORDER], *[outs["delta"][n] for n in _ORDER],
            *[outs["new_m"][n] for n in _ORDER], *[outs["new_v"][n] for n in _ORDER])
```
